```python
import jax, jax.numpy as jnp
from jax import lax
import numpy as np

D_MODEL = 2048
BATCH = 4
SEQ = 2048
DEPTH = 1

NSA_WIDTH = D_MODEL // 2
GLA_WIDTH = D_MODEL - NSA_WIDTH
MIX_WIDTH = NSA_WIDTH + GLA_WIDTH
NSA_HEAD_DIM = 64
NSA_HEADS = NSA_WIDTH // NSA_HEAD_DIM
NSA_KV_HEADS = 4
NSA_Q_PER_KV = NSA_HEADS // NSA_KV_HEADS
CMP_BLOCK = 32
CMP_STRIDE = 16
SEL_BLOCK = 64
N_SEL = 16
WINDOW = 512
N_BRANCH = 3
GLA_HEADS = 4
GLA_DV = GLA_WIDTH // GLA_HEADS
GLA_DK = GLA_DV // 2
GLA_GATE_RANK = 16
GLA_GATE_NORM = 16.0
GLA_CHUNK = 64
N_GROUPS = 4
EXPERTS_PER_GROUP = 8
N_EXPERTS = N_GROUPS * EXPERTS_PER_GROUP
TOP_K = 2
EXPERT_FF = D_MODEL // 4
MOE_BLOCK = 128
EPS = 1e-6
NEG = -1e30
FORCE = 1e30
IN_SPLITS = (NSA_HEADS * NSA_HEAD_DIM,
             2 * N_BRANCH * NSA_KV_HEADS * NSA_HEAD_DIM,
             N_BRANCH * NSA_HEADS,
             GLA_HEADS * GLA_DK,
             GLA_HEADS * GLA_DK,
             GLA_HEADS * GLA_DV,
             GLA_HEADS * GLA_DV,
             GLA_GATE_RANK)
IN_WIDTH = sum(IN_SPLITS)

kernel_name = 'hybrid_nsa_gla_hmoe_block'


def _rmsnorm(x, w):
    xf = x.astype(jnp.float32)
    xf = xf * lax.rsqrt(jnp.mean(xf * xf, axis=-1, keepdims=True) + EPS)
    return (xf * w.astype(jnp.float32)).astype(x.dtype)


def _alibi_slopes(n):
    return jnp.asarray(2.0 ** (-8.0 * np.arange(1, n + 1) / n), dtype=jnp.float32)


def _nsa(q, kv, gate_logits, pos_k, w1_k, w2_k, pos_v, w1_v, w2_v):
    B, S, _ = q.shape
    G, R, Dh = NSA_KV_HEADS, NSA_Q_PER_KV, NSA_HEAD_DIM
    f32 = jnp.float32
    out_dtype = q.dtype
    q = q.reshape(B, S, G, R, Dh).transpose(0, 2, 3, 1, 4) * (Dh ** -0.5)
    k_c, v_c, k_s, v_s, k_w, v_w = [a.reshape(B, S, G, Dh).transpose(0, 2, 1, 3)
                                    for a in jnp.split(kv, 6, axis=-1)]
    slopes = _alibi_slopes(NSA_HEADS).reshape(G, R)
    t = jnp.arange(S)

    n_cmp = (S - CMP_BLOCK) // CMP_STRIDE + 1
    starts = np.arange(n_cmp) * CMP_STRIDE
    tok = starts[:, None] + np.arange(CMP_BLOCK)[None, :]

    def compress(a, pos, w1, w2):
        blk = (a[:, :, tok, :] + pos).reshape(B, G, n_cmp, CMP_BLOCK * Dh)
        return jax.nn.gelu(blk @ w1) @ w2

    kc = compress(k_c, pos_k, w1_k, w2_k)
    vc = compress(v_c, pos_v, w1_v, w2_v)
    blk_end = jnp.asarray(starts + CMP_BLOCK - 1)
    blk_center = jnp.asarray(starts + (CMP_BLOCK - 1) / 2.0, dtype=f32)
    valid_c = blk_end[None, :] <= t[:, None]
    dist_c = t[:, None].astype(f32) - blk_center[None, :]
    s_c = jnp.einsum('bgrsd,bgnd->bgrsn', q, kc).astype(f32)
    s_c = jnp.where(valid_c, s_c - slopes[None, :, :, None, None] * dist_c, NEG)
    p_c = jnp.where(valid_c, jax.nn.softmax(s_c, axis=-1), 0.0)
    o_c = jnp.einsum('bgrsn,bgnd->bgrsd', p_c.astype(vc.dtype), vc)

    nb = S // SEL_BLOCK
    sel_start = np.arange(nb) * SEL_BLOCK
    overlap = ((starts[:, None] < sel_start[None, :] + SEL_BLOCK) &
               (starts[:, None] + CMP_BLOCK > sel_start[None, :])).astype(np.float32)
    imp = jnp.einsum('bgrsn,nj->bgsj', p_c, jnp.asarray(overlap))
    qblk = t // SEL_BLOCK
    j = jnp.arange(nb)
    forced = (j[None, :] == 0) | (j[None, :] == qblk[:, None]) | (j[None, :] == qblk[:, None] - 1)
    imp = jnp.where(forced, FORCE, jnp.where(j[None, :] <= qblk[:, None], imp, NEG))
    n_sel = min(N_SEL, nb)
    _, sel_idx = lax.top_k(imp, n_sel)

    QB = SEL_BLOCK
    nqb = S // QB
    ks_blk = k_s.reshape(B, G, nb, SEL_BLOCK, Dh)
    vs_blk = v_s.reshape(B, G, nb, SEL_BLOCK, Dh)
    kw_pad = jnp.pad(k_w, ((0, 0), (0, 0), (WINDOW, 0), (0, 0)))
    vw_pad = jnp.pad(v_w, ((0, 0), (0, 0), (WINDOW, 0), (0, 0)))
    q_blocks = jnp.moveaxis(q.reshape(B, G, R, nqb, QB, Dh), 3, 0)
    idx_blocks = jnp.moveaxis(sel_idx.reshape(B, G, nqb, QB, n_sel), 2, 0)
    gather = jax.vmap(jax.vmap(lambda blocks, ix: blocks[ix]))

    def block_step(args):
        i, qi, ix = args
        tq = i * QB + jnp.arange(QB)
        kg = gather(ks_blk, ix)
        vg = gather(vs_blk, ix)
        kpos = ix[..., None] * SEL_BLOCK + jnp.arange(SEL_BLOCK)
        dist = tq[None, None, :, None, None] - kpos
        s = jnp.einsum('bgrqd,bgqkld->bgrqkl', qi, kg).astype(f32)
        s = s - slopes[None, :, :, None, None, None] * dist[:, :, None].astype(f32)
        s = jnp.where((dist >= 0)[:, :, None], s, NEG)
        p = jax.nn.softmax(s, axis=(-2, -1))
        o_s = jnp.einsum('bgrqkl,bgqkld->bgrqd', p.astype(vg.dtype), vg)
        kw = lax.dynamic_slice_in_dim(kw_pad, i * QB, WINDOW + QB, axis=2)
        vw = lax.dynamic_slice_in_dim(vw_pad, i * QB, WINDOW + QB, axis=2)
        kpos_w = i * QB - WINDOW + jnp.arange(WINDOW + QB)
        dist_w = tq[:, None] - kpos_w[None, :]
        valid_w = (dist_w >= 0) & (dist_w < WINDOW) & (kpos_w[None, :] >= 0)
        s = jnp.einsum('bgrqd,bgkd->bgrqk', qi, kw).astype(f32)
        s = jnp.where(valid_w, s - slopes[None, :, :, None, None] * dist_w.astype(f32), NEG)
        p = jax.nn.softmax(s, axis=-1)
        o_w = jnp.einsum('bgrqk,bgkd->bgrqd', p.astype(vw.dtype), vw)
        return o_s, o_w

    o_s, o_w = lax.map(block_step, (jnp.arange(nqb), q_blocks, idx_blocks))
    o_s = jnp.moveaxis(o_s, 0, 3).reshape(B, G, R, S, Dh)
    o_w = jnp.moveaxis(o_w, 0, 3).reshape(B, G, R, S, Dh)

    g = jax.nn.sigmoid(gate_logits.astype(f32)).reshape(B, S, G, R, N_BRANCH).transpose(0, 2, 3, 1, 4)
    o = g[..., 0:1] * o_c + g[..., 1:2] * o_s + g[..., 2:3] * o_w
    return o.transpose(0, 3, 1, 2, 4).reshape(B, S, NSA_HEADS * Dh).astype(out_dtype)


def _gla(q, k, v, out_gate, gate_lr, w_gate2, b_gate, norm_w):
    B, S, _ = q.shape
    H, Dk, Dv, C = GLA_HEADS, GLA_DK, GLA_DV, GLA_CHUNK
    f32 = jnp.float32
    out_dtype = q.dtype
    log_a = jax.nn.log_sigmoid((gate_lr @ w_gate2 + b_gate).astype(f32)) / GLA_GATE_NORM
    nc = S // C

    def chunks(a, d):
        return jnp.moveaxis(a.astype(f32).reshape(B, nc, C, H, d), 1, 0).transpose(0, 1, 3, 2, 4)

    qs = chunks(q, Dk) * (Dk ** -0.5)
    ks = chunks(k, Dk)
    vs = chunks(v, Dv)
    gs = chunks(log_a, Dk)
    causal = jnp.tril(jnp.ones((C, C), dtype=bool))

    def step(state, inp):
        qc, kc, vc, gc = inp
        b = jnp.cumsum(gc, axis=2)
        o_inter = jnp.einsum('bhik,bhkv->bhiv', qc * jnp.exp(b), state)
        decay = jnp.exp(jnp.where(causal[:, :, None], b[:, :, :, None, :] - b[:, :, None, :, :], -jnp.inf))
        attn = jnp.einsum('bhik,bhjk,bhijk->bhij', qc, kc, decay)
        o_intra = jnp.einsum('bhij,bhjv->bhiv', attn, vc)
        b_last = b[:, :, -1, :]
        state = jnp.exp(b_last)[..., None] * state + jnp.einsum(
            'bhjk,bhjv->bhkv', kc * jnp.exp(b_last[:, :, None, :] - b), vc)
        return state, o_inter + o_intra

    state0 = jnp.zeros((B, H, Dk, Dv), f32)
    _, o = lax.scan(step, state0, (qs, ks, vs, gs))
    o = o.transpose(1, 0, 3, 2, 4).reshape(B, S, H, Dv)
    o = o * lax.rsqrt(jnp.mean(o * o, axis=-1, keepdims=True) + EPS) * norm_w.astype(f32)
    o = o.reshape(B, S, H * Dv) * jax.nn.silu(out_gate.astype(f32))
    return o.astype(out_dtype)


def _moe(h, w_rg, b_rg, w_re, b_re, w_gate, w_up, w_down):
    B, S, D = h.shape
    N = B * S
    NK = N * TOP_K
    f32 = jnp.float32
    xt = h.reshape(N, D)
    grp_prob = jax.nn.softmax((xt @ w_rg + b_rg).astype(f32), axis=-1)
    grp_p, grp = lax.top_k(grp_prob, 1)
    exp_logits = (xt @ w_re + b_re).astype(f32).reshape(N, N_GROUPS, EXPERTS_PER_GROUP)
    exp_logits = jnp.take_along_axis(exp_logits, grp[:, :, None], axis=1)[:, 0]
    top_v, top_i = lax.top_k(exp_logits, TOP_K)
    weight = grp_p * jax.nn.softmax(top_v, axis=-1)
    expert = grp * EXPERTS_PER_GROUP + top_i
    flat_e = expert.reshape(-1)
    flat_w = weight.reshape(-1)
    flat_tok = jnp.arange(NK) // TOP_K
    counts = jnp.zeros((N_EXPERTS,), jnp.int32).at[flat_e].add(1)
    padded = (counts + MOE_BLOCK - 1) // MOE_BLOCK * MOE_BLOCK
    pad_end = jnp.cumsum(padded)
    pad_start = pad_end - padded
    start = jnp.cumsum(counts) - counts
    order = jnp.argsort(flat_e, stable=True)
    e_sorted = flat_e[order]
    dest = pad_start[e_sorted] + jnp.arange(NK) - start[e_sorted]
    cap = ((NK + N_EXPERTS * (MOE_BLOCK - 1) + MOE_BLOCK - 1) // MOE_BLOCK) * MOE_BLOCK
    n_blk = cap // MOE_BLOCK
    slot_tok = jnp.full((cap,), N, jnp.int32).at[dest].set(flat_tok[order])
    slot_w = jnp.zeros((cap,), f32).at[dest].set(flat_w[order])
    blk_expert = jnp.minimum(jnp.searchsorted(pad_end, jnp.arange(n_blk) * MOE_BLOCK, side='right'),
                             N_EXPERTS - 1)
    x_slots = jnp.concatenate([xt, jnp.zeros((1, D), xt.dtype)], axis=0)[slot_tok]
    x_slots = x_slots.reshape(n_blk, MOE_BLOCK, D)

    def expert_ffn(args):
        xb, e = args
        return (jax.nn.silu(xb @ w_gate[e]) * (xb @ w_up[e])) @ w_down[e]

    y = lax.map(expert_ffn, (x_slots, blk_expert)).reshape(cap, D)
    y = y * slot_w[:, None].astype(y.dtype)
    y = jnp.zeros((N + 1, D), y.dtype).at[slot_tok].add(y)[:N]
    return y.reshape(B, S, D)


def setup_inputs(seed: int = 0) -> dict:
    key = jax.random.key(seed)
    ks = jax.random.split(key, 32)
    L, D, Dh = DEPTH, D_MODEL, NSA_HEAD_DIM

    def nrm(k, shape, scale):
        return jax.random.normal(k, shape, jnp.float32) * scale

    return {
        'x': nrm(ks[0], (BATCH, SEQ, D), 1.0),
        'c': nrm(ks[1], (BATCH, D), 1.0),
        'w_ada': nrm(ks[2], (L, D, 6 * D), 0.5 * D ** -0.5),
        'b_ada': nrm(ks[3], (L, 6 * D), 0.02),
        'norm1_w': 1.0 + nrm(ks[4], (L, D), 0.02),
        'w_in': nrm(ks[5], (L, D, IN_WIDTH), D ** -0.5),
        'cmp_pos_k': nrm(ks[6], (L, CMP_BLOCK, Dh), 0.1),
        'cmp_w1_k': nrm(ks[7], (L, CMP_BLOCK * Dh, Dh), (CMP_BLOCK * Dh) ** -0.5),
        'cmp_w2_k': nrm(ks[8], (L, Dh, Dh), Dh ** -0.5),
        'cmp_pos_v': nrm(ks[9], (L, CMP_BLOCK, Dh), 0.1),
        'cmp_w1_v': nrm(ks[10], (L, CMP_BLOCK * Dh, Dh), (CMP_BLOCK * Dh) ** -0.5),
        'cmp_w2_v': nrm(ks[11], (L, Dh, Dh), Dh ** -0.5),
        'gla_w_gate2': nrm(ks[12], (L, GLA_GATE_RANK, GLA_HEADS * GLA_DK), GLA_GATE_RANK ** -0.5),
        'gla_b_gate': nrm(ks[13], (L, GLA_HEADS * GLA_DK), 0.1),
        'gla_norm_w': 1.0 + nrm(ks[14], (L, GLA_DV), 0.02),
        'w_out': nrm(ks[15], (L, MIX_WIDTH, D), MIX_WIDTH ** -0.5),
        'norm2_w': 1.0 + nrm(ks[16], (L, D), 0.02),
        'w_router_group': nrm(ks[17], (L, D, N_GROUPS), D ** -0.5),
        'b_router_group': nrm(ks[18], (L, N_GROUPS), 0.01),
        'w_router_expert': nrm(ks[19], (L, D, N_EXPERTS), D ** -0.5),
        'b_router_expert': nrm(ks[20], (L, N_EXPERTS), 0.01),
        'w_expert_gate': nrm(ks[21], (L, N_EXPERTS, D, EXPERT_FF), D ** -0.5),
        'w_expert_up': nrm(ks[22], (L, N_EXPERTS, D, EXPERT_FF), D ** -0.5),
        'w_expert_down': nrm(ks[23], (L, N_EXPERTS, EXPERT_FF, D), EXPERT_FF ** -0.5),
        'norm_f_w': 1.0 + nrm(ks[24], (D,), 0.02),
    }


def reference(x, c, w_ada, b_ada, norm1_w, w_in, cmp_pos_k, cmp_w1_k, cmp_w2_k, cmp_pos_v,
              cmp_w1_v, cmp_w2_v, gla_w_gate2, gla_b_gate, gla_norm_w, w_out, norm2_w,
              w_router_group, b_router_group, w_router_expert, b_router_expert,
              w_expert_gate, w_expert_up, w_expert_down, norm_f_w):
    offsets = np.cumsum(IN_SPLITS)[:-1].tolist()
    for l in range(DEPTH):
        mod = jax.nn.silu(c) @ w_ada[l] + b_ada[l]
        sh1, sc1, g1, sh2, sc2, g2 = [m[:, None, :] for m in jnp.split(mod, 6, axis=-1)]
        h = _rmsnorm(x, norm1_w[l]) * (1.0 + sc1) + sh1
        proj = h @ w_in[l]
        nsa_q, nsa_kv, nsa_g, gla_q, gla_k, gla_v, gla_og, gla_lr = jnp.split(proj, offsets, axis=-1)
        o_nsa = _nsa(nsa_q, nsa_kv, nsa_g, cmp_pos_k[l], cmp_w1_k[l], cmp_w2_k[l],
                     cmp_pos_v[l], cmp_w1_v[l], cmp_w2_v[l])
        o_gla = _gla(gla_q, gla_k, gla_v, gla_og, gla_lr, gla_w_gate2[l], gla_b_gate[l], gla_norm_w[l])
        mix = jnp.concatenate([o_nsa, o_gla], axis=-1)
        x = x + g1 * (mix @ w_out[l])
        h = _rmsnorm(x, norm2_w[l]) * (1.0 + sc2) + sh2
        x = x + g2 * _moe(h, w_router_group[l], b_router_group[l], w_router_expert[l],
                          b_router_expert[l], w_expert_gate[l], w_expert_up[l], w_expert_down[l])
    return _rmsnorm(x, norm_f_w)
```

```python
import functools

import numpy as np
import jax
import jax.numpy as jnp
from jax import lax
from jax.experimental import pallas as pl
from jax.experimental.pallas import tpu as pltpu

F32 = jnp.float32
BF16 = jnp.bfloat16
I32 = jnp.int32
HI = lax.Precision.HIGHEST

D_MODEL = 2048
NSA_HEAD_DIM = 64
NSA_HEADS = 16
NSA_KV_HEADS = 4
NSA_Q_PER_KV = 4
CMP_BLOCK = 32
CMP_STRIDE = 16
SEL_BLOCK = 64
N_SEL = 16
WINDOW = 512
N_BRANCH = 3
GLA_HEADS = 4
GLA_DV = 256
GLA_DK = 128
GLA_GATE_RANK = 16
GLA_GATE_NORM = 16.0
GLA_CHUNK = 64
GLA_SUB = 16
N_GROUPS = 4
EXPERTS_PER_GROUP = 8
N_EXPERTS = 32
TOP_K = 2
EXPERT_FF = 512
EPS = 1e-6
NEG = -1e30
FORCE = 1e30

NSA_Q_COLS = NSA_HEADS * NSA_HEAD_DIM
NSA_KV_COLS = 2 * N_BRANCH * NSA_KV_HEADS * NSA_HEAD_DIM
NSA_GATE_COLS = N_BRANCH * NSA_HEADS
NSA_COLS = NSA_Q_COLS + NSA_KV_COLS
NSA_SLOTS = NSA_COLS // NSA_HEAD_DIM
GLA_Q_OFF = 0
GLA_K_OFF = GLA_HEADS * GLA_DK
GLA_V_OFF = 2 * GLA_HEADS * GLA_DK
GLA_OG_OFF = GLA_V_OFF + GLA_HEADS * GLA_DV
GLA_MISC_OFF = GLA_OG_OFF + GLA_HEADS * GLA_DV
LANES = 128
GLA_COLS = GLA_MISC_OFF + LANES

VMEM_LIMIT = 56 * 1024 * 1024

ADA_TN = 768
INPROJ_TM = 512
INPROJ_TN_NSA = 512
INPROJ_TN_GLA = 640
NSA_TQ = 256
NSA_TK = 256
OUT_TM = 256
RANK_TM = 256
MOE_TB = 256
DISPATCH_CH = 128
COMB_TM = 256


def _cparams(sem):
    return pltpu.CompilerParams(dimension_semantics=sem, vmem_limit_bytes=VMEM_LIMIT)


def _adaln_kernel(ct_ref, w_ref, b_ref, o_ref, s_scr):
    ct = ct_ref[...]
    s_scr[...] = ct * jax.nn.sigmoid(ct)
    nb = ct_ref.shape[1]
    kdim, tn = w_ref.shape

    def body(k, accs):
        r = pl.multiple_of(k * 8, 8)
        w8 = w_ref[pl.ds(r, 8), :]
        s8 = s_scr[pl.ds(r, 8), :]
        return tuple(acc + w8 * s8[:, b:b + 1] for b, acc in enumerate(accs))

    accs = lax.fori_loop(0, kdim // 8, body, tuple(jnp.zeros((8, tn), F32) for _ in range(nb)), unroll=2)
    bias = b_ref[...]
    for b, acc in enumerate(accs):
        o_ref[b:b + 1, :] = jnp.sum(acc, axis=0, keepdims=True) + bias


def _adaln(c, w, b):
    nb, d = c.shape
    n = w.shape[1]
    return pl.pallas_call(
        _adaln_kernel,
        grid=(n // ADA_TN,),
        in_specs=[pl.BlockSpec((d, nb), lambda j: (0, 0)),
                  pl.BlockSpec((d, ADA_TN), lambda j: (0, j)),
                  pl.BlockSpec((1, ADA_TN), lambda j: (0, j))],
        out_specs=pl.BlockSpec((nb, ADA_TN), lambda j: (0, j)),
        out_shape=jax.ShapeDtypeStruct((nb, n), F32),
        scratch_shapes=[pltpu.VMEM((d, nb), F32)],
        compiler_params=_cparams(("arbitrary",)),
        name="adaln",
    )(c.T, w, b.reshape(1, n))


def _modulated_norm(x, nw, sc, sh):
    ms = jnp.mean(x * x, axis=-1, keepdims=True)
    h = x * lax.rsqrt(ms + EPS) * nw
    return h * (1.0 + sc) + sh


def _inproj_kernel(x_ref, sc_ref, sh_ref, nw_ref, w_ref, o_ref, h_scr, *, head_major):
    @pl.when(pl.program_id(2) == 0)
    def _():
        h = _modulated_norm(x_ref[0], nw_ref[...], sc_ref[0, 0], sh_ref[0, 0])
        h_scr[...] = h.astype(BF16)

    acc = jnp.dot(h_scr[...], w_ref[...], preferred_element_type=F32)
    if head_major:
        for u in range(acc.shape[1] // NSA_HEAD_DIM):
            o_ref[0, u] = acc[:, u * NSA_HEAD_DIM:(u + 1) * NSA_HEAD_DIM]
    else:
        o_ref[0] = acc


def _inproj(x, mod4, nw, w, *, sc_idx, sh_idx, tn, head_major):
    bsz, s, d = x.shape
    n = w.shape[1]
    tm = INPROJ_TM
    if head_major:
        upb = tn // NSA_HEAD_DIM
        out_shape = jax.ShapeDtypeStruct((bsz, n // NSA_HEAD_DIM, s, NSA_HEAD_DIM), F32)
        out_spec = pl.BlockSpec((1, upb, tm, NSA_HEAD_DIM), lambda b, i, j: (b, j, i, 0))
    else:
        out_shape = jax.ShapeDtypeStruct((bsz, s, n), F32)
        out_spec = pl.BlockSpec((1, tm, tn), lambda b, i, j: (b, i, j))
    return pl.pallas_call(
        functools.partial(_inproj_kernel, head_major=head_major),
        grid=(bsz, s // tm, n // tn),
        in_specs=[pl.BlockSpec((1, tm, d), lambda b, i, j: (b, i, 0)),
                  pl.BlockSpec((1, 1, 1, d), lambda b, i, j: (b, sc_idx, 0, 0)),
                  pl.BlockSpec((1, 1, 1, d), lambda b, i, j: (b, sh_idx, 0, 0)),
                  pl.BlockSpec((1, d), lambda b, i, j: (0, 0)),
                  pl.BlockSpec((d, tn), lambda b, i, j: (0, j))],
        out_specs=out_spec,
        out_shape=out_shape,
        scratch_shapes=[pltpu.VMEM((tm, d), BF16)],
        compiler_params=_cparams(("parallel", "parallel", "arbitrary")),
        name="inproj_nsa" if head_major else "inproj_gla",
    )(x, mod4, mod4, nw, w)


def _compress_kernel(a_ref, pos_ref, w1_ref, w2_ref, o_ref):
    a = a_ref[0, 0]
    pos = pos_ref[0]
    half = a.shape[1]
    y1 = jnp.dot(a + pos[0:1], w1_ref[0, :half, :], precision=HI, preferred_element_type=F32)
    y2 = jnp.dot(a + pos[1:2], w1_ref[0, half:, :], precision=HI, preferred_element_type=F32)
    nrow = a.shape[0]
    h = y1 + pltpu.roll(y2, nrow - 1, axis=0)
    out = jnp.dot(jax.nn.gelu(h), w2_ref[0], precision=HI, preferred_element_type=F32)
    row = lax.broadcasted_iota(I32, out.shape, 0)
    o_ref[0, 0] = jnp.where(row < nrow - 1, out, 0.0)


def _compress(kv_chunks, pos, w1, w2):
    bsz, nslot, nch, cw = kv_chunks.shape
    dh = NSA_HEAD_DIM
    return pl.pallas_call(
        _compress_kernel,
        grid=(bsz, nslot),
        in_specs=[pl.BlockSpec((1, 1, nch, cw), lambda b, t: (b, t, 0, 0)),
                  pl.BlockSpec((1, 2, cw), lambda b, t: (t // NSA_KV_HEADS, 0, 0)),
                  pl.BlockSpec((1, 2 * cw, dh), lambda b, t: (t // NSA_KV_HEADS, 0, 0)),
                  pl.BlockSpec((1, dh, dh), lambda b, t: (t // NSA_KV_HEADS, 0, 0))],
        out_specs=pl.BlockSpec((1, 1, nch, dh), lambda b, t: (b, t, 0, 0)),
        out_shape=jax.ShapeDtypeStruct((bsz, nslot, nch, dh), F32),
        compiler_params=_cparams(("parallel", "parallel")),
        name="nsa_compress",
    )(kv_chunks, pos, w1, w2)


def _nt_dot(a, b, **kw):
    return lax.dot_general(a, b, (((1,), (1,)), ((), ())), preferred_element_type=F32, **kw)


def _nsa_kernel(slopes_ref, q_ref, kc_ref, vc_ref, ks_ref, vs_ref, kw_ref, vw_ref, gate_ref, o_ref,
                ksb, vsb, kwb, vwb, m_scr, l_scr, acc_scr):
    g = pl.program_id(1)
    qi = pl.program_id(2)
    tq_n = q_ref.shape[2]
    dh = NSA_HEAD_DIM
    nr = NSA_Q_PER_KV
    seq = ks_ref.shape[2]

    @pl.when(qi == 0)
    def _():
        ksb[...] = ks_ref[0, 0].astype(BF16)
        vsb[...] = vs_ref[0, 0].astype(BF16)
        kwb[...] = kw_ref[0, 0].astype(BF16)
        vwb[...] = vw_ref[0, 0].astype(BF16)

    t0 = qi * tq_n
    tq = t0 + lax.broadcasted_iota(I32, (tq_n, 1), 0)
    tq_f = tq.astype(F32)
    slopes = [slopes_ref[g * nr + r] for r in range(nr)]
    scale = dh ** -0.5
    qs = [q_ref[0, r] * scale for r in range(nr)]

    ncp = kc_ref.shape[2]
    kc = kc_ref[0, 0]
    vc_b = vc_ref[0, 0].astype(BF16)
    n_lane = lax.broadcasted_iota(I32, (1, ncp), 1)
    blk_end = n_lane * CMP_STRIDE + (CMP_BLOCK - 1)
    center = n_lane.astype(F32) * CMP_STRIDE + (CMP_BLOCK - 1) / 2.0
    valid_c = blk_end <= tq
    dist_c = tq_f - center
    psum = jnp.zeros((tq_n, ncp), F32)
    o_c = []
    for r in range(nr):
        s = _nt_dot(qs[r], kc, precision=HI)
        s = jnp.where(valid_c, s - slopes[r] * dist_c, NEG)
        e = jnp.exp(s - jnp.max(s, axis=-1, keepdims=True))
        p = jnp.where(valid_c, e / jnp.sum(e, axis=-1, keepdims=True), 0.0)
        psum = psum + p
        o_c.append(jnp.dot(p.astype(BF16), vc_b, preferred_element_type=F32))

    nb = seq // SEL_BLOCK
    rown = lax.broadcasted_iota(I32, (ncp, LANES), 0) * CMP_STRIDE
    colj = lax.broadcasted_iota(I32, (ncp, LANES), 1) * SEL_BLOCK
    overlap = jnp.where((rown < colj + SEL_BLOCK) & (rown + CMP_BLOCK > colj)
                        & (rown < (ncp - 1) * CMP_STRIDE) & (colj < nb * SEL_BLOCK), 1.0, 0.0)
    imp = jnp.dot(psum, overlap, precision=HI, preferred_element_type=F32)
    j_lane = lax.broadcasted_iota(I32, (1, LANES), 1)
    qblk = tq // SEL_BLOCK
    forced = (j_lane == 0) | (j_lane == qblk) | (j_lane == qblk - 1)
    imp = jnp.where(forced, FORCE, jnp.where(j_lane <= qblk, imp, NEG))
    cnt = jnp.zeros((tq_n, LANES), F32)
    for i in range(nb):
        ci = imp[:, i:i + 1]
        tie = jnp.where(j_lane > i, 1.0, 0.0)
        cnt = cnt + jnp.where(ci > imp, 1.0, jnp.where(ci == imp, tie, 0.0))
    sel = jnp.where((cnt < float(min(N_SEL, nb))) & (j_lane < nb), 1.0, 0.0).astype(BF16)

    q4 = jnp.concatenate(qs, axis=0).astype(BF16)
    tk_n = NSA_TK
    m_scr[...] = jnp.full(m_scr.shape, NEG, F32)
    l_scr[...] = jnp.zeros(l_scr.shape, F32)
    acc_scr[...] = jnp.zeros(acc_scr.shape, F32)
    bpt = tk_n // SEL_BLOCK

    def kv_step(kt, carry):
        k0 = pl.multiple_of(kt * tk_n, tk_n)
        k = ksb[pl.ds(k0, tk_n), :]
        v = vsb[pl.ds(k0, tk_n), :]
        s4 = _nt_dot(q4, k)
        kpos = k0 + lax.broadcasted_iota(I32, (1, tk_n), 1)
        dist = tq - kpos
        erow = lax.broadcasted_iota(I32, (LANES, tk_n), 0)
        ecol = lax.broadcasted_iota(I32, (LANES, tk_n), 1) // SEL_BLOCK + kt * bpt
        expand = jnp.where(erow == ecol, 1.0, 0.0).astype(BF16)
        selx = jnp.dot(sel, expand, preferred_element_type=F32)
        ok = (selx > 0.5) & (dist >= 0)
        dist_f = dist.astype(F32)
        for r in range(nr):
            rows = pl.ds(r * tq_n, tq_n)
            s = jnp.where(ok, s4[r * tq_n:(r + 1) * tq_n] - slopes[r] * dist_f, NEG)
            m_prev = m_scr[rows, :]
            m_new = jnp.maximum(m_prev, jnp.max(s, axis=-1, keepdims=True))
            alpha = jnp.exp(m_prev - m_new)
            p = jnp.exp(s - m_new[:, 0:1])
            l_scr[rows, :] = alpha * l_scr[rows, :] + jnp.sum(p, axis=-1, keepdims=True)
            acc_scr[rows, :] = alpha[:, 0:dh] * acc_scr[rows, :] + jnp.dot(
                p.astype(BF16), v, preferred_element_type=F32)
            m_scr[rows, :] = m_new
        return carry

    lax.fori_loop(0, (t0 + tq_n) // tk_n, kv_step, 0)

    wk = WINDOW + tq_n
    w0 = pl.multiple_of(jnp.maximum(t0 - WINDOW, 0), tq_n)
    kwin = kwb[pl.ds(w0, wk), :]
    vwin = vwb[pl.ds(w0, wk), :]
    dist_w = tq - (w0 + lax.broadcasted_iota(I32, (1, wk), 1))
    ok_w = (dist_w >= 0) & (dist_w < WINDOW)
    dist_wf = dist_w.astype(F32)
    s4w = _nt_dot(q4, kwin)

    gsel = jnp.where(lax.broadcasted_iota(I32, (LANES, LANES), 0)
                     == lax.broadcasted_iota(I32, (LANES, LANES), 1) + g * (nr * N_BRANCH), 1.0, 0.0)
    gates = jax.nn.sigmoid(jnp.dot(gate_ref[0], gsel, precision=HI, preferred_element_type=F32))

    for r in range(nr):
        s = jnp.where(ok_w, s4w[r * tq_n:(r + 1) * tq_n] - slopes[r] * dist_wf, NEG)
        e = jnp.exp(s - jnp.max(s, axis=-1, keepdims=True))
        p = e / jnp.sum(e, axis=-1, keepdims=True)
        o_w = jnp.dot(p.astype(BF16), vwin, preferred_element_type=F32)
        rows = pl.ds(r * tq_n, tq_n)
        o_s = acc_scr[rows, :] / l_scr[rows, 0:dh]
        c0 = r * N_BRANCH
        o = gates[:, c0:c0 + 1] * o_c[r] + gates[:, c0 + 1:c0 + 2] * o_s + gates[:, c0 + 2:c0 + 3] * o_w
        o_ref[0, :, r * dh:(r + 1) * dh] = o


def _nsa(slopes, proj_nsa, kvc, proj_gla):
    bsz, _, s, dh = proj_nsa.shape
    g_n, nr = NSA_KV_HEADS, NSA_Q_PER_KV
    tq = NSA_TQ
    ncp = kvc.shape[2]
    kv0 = NSA_HEADS

    def kv_spec(i):
        return pl.BlockSpec((1, 1, s, dh), lambda b, g, q, i=i: (b, kv0 + i * g_n + g, 0, 0))

    return pl.pallas_call(
        _nsa_kernel,
        grid=(bsz, g_n, s // tq),
        in_specs=[pl.BlockSpec(memory_space=pltpu.SMEM),
                  pl.BlockSpec((1, nr, tq, dh), lambda b, g, q: (b, g, q, 0)),
                  pl.BlockSpec((1, 1, ncp, dh), lambda b, g, q: (b, g, 0, 0)),
                  pl.BlockSpec((1, 1, ncp, dh), lambda b, g, q: (b, g_n + g, 0, 0)),
                  kv_spec(2), kv_spec(3), kv_spec(4), kv_spec(5),
                  pl.BlockSpec((1, tq, LANES), lambda b, g, q: (b, q, GLA_MISC_OFF // LANES))],
        out_specs=pl.BlockSpec((1, tq, nr * dh), lambda b, g, q: (b, q, g)),
        out_shape=jax.ShapeDtypeStruct((bsz, s, NSA_HEADS * dh), F32),
        scratch_shapes=[pltpu.VMEM((s, dh), BF16)] * 4 + [
            pltpu.VMEM((nr * tq, LANES), F32), pltpu.VMEM((nr * tq, LANES), F32), pltpu.VMEM((nr * tq, dh), F32)],
        compiler_params=_cparams(("parallel", "parallel", "arbitrary")),
        name="nsa_attention",
    )(slopes, proj_nsa, kvc, kvc, proj_nsa, proj_nsa, proj_nsa, proj_nsa, proj_gla)


def _gla_kernel(q_ref, k_ref, v_ref, og_ref, lr_ref, wg_ref, bg_ref, nw_ref, o_ref, st_scr, la_scr, b_scr):
    seq = q_ref.shape[1]
    c_n, sub = GLA_CHUNK, GLA_SUB
    dk = GLA_DK
    z = jnp.dot(lr_ref[0], wg_ref[...], precision=HI, preferred_element_type=F32) + bg_ref[...]
    la_scr[...] = (jnp.minimum(z, 0.0) - jnp.log1p(jnp.exp(-jnp.abs(z)))) * (1.0 / GLA_GATE_NORM)
    st_scr[...] = jnp.zeros(st_scr.shape, F32)
    tril = jnp.where(lax.broadcasted_iota(I32, (c_n, c_n), 0) >= lax.broadcasted_iota(I32, (c_n, c_n), 1), 1.0, 0.0)
    row_c = lax.broadcasted_iota(I32, (c_n, 1), 0)
    row_s = lax.broadcasted_iota(I32, (sub, 1), 0)
    lane_c = lax.broadcasted_iota(I32, (1, c_n), 1)
    nw = nw_ref[...]

    def chunk(c, carry):
        r0 = pl.multiple_of(c * c_n, c_n)
        qc = q_ref[0, pl.ds(r0, c_n), :] * (dk ** -0.5)
        kc = k_ref[0, pl.ds(r0, c_n), :]
        vc = v_ref[0, pl.ds(r0, c_n), :].astype(BF16)
        b = jnp.dot(tril, la_scr[pl.ds(r0, c_n), :], precision=HI, preferred_element_type=F32)
        b_scr[...] = b
        st = st_scr[...]
        o = _nt_dot((qc * jnp.exp(b)).astype(BF16), st.astype(BF16))
        strips = []
        for blk in range(c_n // sub):
            lo = blk * sub
            q_i = qc[lo:lo + sub]
            b_i = b[lo:lo + sub]
            a = jnp.zeros((sub, c_n), F32)
            for j in range(sub):
                b_j = b_scr[lo + j:lo + j + 1, :]
                k_j = k_ref[0, pl.ds(r0 + lo + j, 1), :]
                e = jnp.exp(jnp.where(row_s >= j, b_i - b_j, -jnp.inf))
                col = jnp.sum(q_i * k_j * e, axis=-1, keepdims=True)
                a = jnp.where(lane_c == lo + j, col, a)
            if blk > 0:
                b_r = b_scr[lo - 1:lo, :]
                q_d = q_i * jnp.exp(b_i - b_r)
                k_d = kc * jnp.exp(jnp.where(row_c < lo, b_r - b, -jnp.inf))
                a = a + _nt_dot(q_d.astype(BF16), k_d.astype(BF16))
            strips.append(a)
        attn = jnp.concatenate(strips, axis=0)
        o = o + jnp.dot(attn.astype(BF16), vc, preferred_element_type=F32)
        b_last = b_scr[c_n - 1:c_n, :]
        k_dec = (kc * jnp.exp(b_last - b)).astype(BF16)
        st_scr[...] = st * jnp.exp(b_last) + lax.dot_general(
            vc, k_dec, (((0,), (0,)), ((), ())), preferred_element_type=F32)
        o = o * lax.rsqrt(jnp.mean(o * o, axis=-1, keepdims=True) + EPS) * nw
        og = og_ref[0, pl.ds(r0, c_n), :]
        o_ref[0, pl.ds(r0, c_n), :] = o * (og * jax.nn.sigmoid(og))
        return carry

    lax.fori_loop(0, seq // c_n, chunk, 0)


def _gla(proj_gla, wg_pad, bg, nw):
    bsz, s, _ = proj_gla.shape
    h_n, dk, dv = GLA_HEADS, GLA_DK, GLA_DV
    return pl.pallas_call(
        _gla_kernel,
        grid=(bsz, h_n),
        in_specs=[pl.BlockSpec((1, s, dk), lambda b, h: (b, 0, GLA_Q_OFF // dk + h)),
                  pl.BlockSpec((1, s, dk), lambda b, h: (b, 0, GLA_K_OFF // dk + h)),
                  pl.BlockSpec((1, s, dv), lambda b, h: (b, 0, GLA_V_OFF // dv + h)),
                  pl.BlockSpec((1, s, dv), lambda b, h: (b, 0, GLA_OG_OFF // dv + h)),
                  pl.BlockSpec((1, s, LANES), lambda b, h: (b, 0, GLA_MISC_OFF // LANES)),
                  pl.BlockSpec((LANES, dk), lambda b, h: (0, h)),
                  pl.BlockSpec((1, dk), lambda b, h: (0, h)),
                  pl.BlockSpec((1, dv), lambda b, h: (0, 0))],
        out_specs=pl.BlockSpec((1, s, dv), lambda b, h: (b, 0, h)),
        out_shape=jax.ShapeDtypeStruct((bsz, s, h_n * dv), F32),
        scratch_shapes=[pltpu.VMEM((dv, dk), F32), pltpu.VMEM((s, dk), F32), pltpu.VMEM((GLA_CHUNK, dk), F32)],
        compiler_params=_cparams(("parallel", "parallel")),
        name="gla",
    )(proj_gla, proj_gla, proj_gla, proj_gla, proj_gla, wg_pad, bg, nw)


def _outproj_kernel(nsa_ref, gla_ref, x_ref, wo_ref, g1_ref, sc_ref, sh_ref, nw_ref, wr_ref, br_ref,
                    x1_ref, h_ref, route_ref):
    half = nsa_ref.shape[2]
    acc = jnp.dot(nsa_ref[0].astype(BF16), wo_ref[:half, :], preferred_element_type=F32)
    acc = acc + jnp.dot(gla_ref[0].astype(BF16), wo_ref[half:, :], preferred_element_type=F32)
    x1 = x_ref[0] + g1_ref[0, 0] * acc
    x1_ref[0] = x1
    h = _modulated_norm(x1, nw_ref[...], sc_ref[0, 0], sh_ref[0, 0])
    h_ref[0] = h
    logits = jnp.dot(h, wr_ref[...], precision=HI, preferred_element_type=F32) + br_ref[...]
    lane = lax.broadcasted_iota(I32, (1, LANES), 1)
    ninf = -jnp.inf
    is_g = (lane >= N_EXPERTS) & (lane < N_EXPERTS + N_GROUPS)
    gl = jnp.where(is_g, logits, ninf)
    ge = jnp.exp(gl - jnp.max(gl, axis=-1, keepdims=True))
    gp = ge / jnp.sum(ge, axis=-1, keepdims=True)
    gp_max = jnp.max(gp, axis=-1, keepdims=True)
    grp = jnp.min(jnp.where((gp == gp_max) & is_g, lane - N_EXPERTS, LANES), axis=-1, keepdims=True)
    in_grp = (lane // EXPERTS_PER_GROUP == grp) & (lane < N_EXPERTS)
    el = jnp.where(in_grp, logits, ninf)
    v1 = jnp.max(el, axis=-1, keepdims=True)
    i1 = jnp.min(jnp.where(el == v1, lane, LANES), axis=-1, keepdims=True)
    el2 = jnp.where(lane == i1, ninf, el)
    v2 = jnp.max(el2, axis=-1, keepdims=True)
    i2 = jnp.min(jnp.where(el2 == v2, lane, LANES), axis=-1, keepdims=True)
    e2 = jnp.exp(v2 - v1)
    den = 1.0 + e2
    w1 = gp_max * (1.0 / den)
    w2 = gp_max * (e2 / den)
    route_ref[0] = jnp.where(lane == 0, i1.astype(F32), jnp.where(lane == 1, i2.astype(F32), jnp.where(
        lane == 2, w1, jnp.where(lane == 3, w2, 0.0))))


def _outproj(o_nsa, o_gla, x, wo, mod4, nw, wr, br):
    bsz, s, d = x.shape
    tm = OUT_TM
    half = o_nsa.shape[2]

    def mod_spec(idx):
        return pl.BlockSpec((1, 1, 1, d), lambda b, i: (b, idx, 0, 0))

    row = lambda w: pl.BlockSpec((1, tm, w), lambda b, i: (b, i, 0))
    return pl.pallas_call(
        _outproj_kernel,
        grid=(bsz, s // tm),
        in_specs=[row(half), row(half), row(d),
                  pl.BlockSpec((2 * half, d), lambda b, i: (0, 0)),
                  mod_spec(2), mod_spec(4), mod_spec(3),
                  pl.BlockSpec((1, d), lambda b, i: (0, 0)),
                  pl.BlockSpec((d, LANES), lambda b, i: (0, 0)),
                  pl.BlockSpec((1, LANES), lambda b, i: (0, 0))],
        out_specs=[row(d), row(d), row(LANES)],
        out_shape=[jax.ShapeDtypeStruct((bsz, s, d), F32), jax.ShapeDtypeStruct((bsz, s, d), F32),
                   jax.ShapeDtypeStruct((bsz, s, LANES), F32)],
        compiler_params=_cparams(("parallel", "parallel")),
        name="outproj_router",
    )(o_nsa, o_gla, x, wo, mod4, mod4, mod4, nw, wr, br)


def _rank_kernel(route_ref, dest_ref, meta_ref, rank_scr):
    n = route_ref.shape[0]
    tm = RANK_TM
    lane_i = lax.broadcasted_iota(I32, (1, LANES), 1)
    lane = lane_i.astype(F32)
    strict = jnp.where(lax.broadcasted_iota(I32, (tm, tm), 0) > lax.broadcasted_iota(I32, (tm, tm), 1),
                       1.0, 0.0).astype(BF16)

    def two_lanes(a, b):
        return jnp.where(lane_i == 0, a, jnp.where(lane_i == 1, b, 0.0))

    def pick(e, table):
        return jnp.sum(jnp.where(lane == e, table, 0.0), axis=-1, keepdims=True)

    def count(i, seen):
        r0 = pl.multiple_of(i * tm, tm)
        rt = route_ref[pl.ds(r0, tm), :]
        e1, e2 = rt[:, 0:1], rt[:, 1:2]
        member = jnp.where(lane == e1, 1.0, jnp.where(lane == e2, 1.0, 0.0))
        before = jnp.dot(strict, member.astype(BF16), preferred_element_type=F32) + seen
        rank_scr[pl.ds(r0, tm), :] = two_lanes(pick(e1, before), pick(e2, before))
        return seen + jnp.sum(member, axis=0, keepdims=True)

    counts = lax.fori_loop(0, n // tm, count, jnp.zeros((1, LANES), F32))
    ntile = jnp.floor((counts + (MOE_TB - 1)) * (1.0 / MOE_TB))
    incl = jnp.where(lax.broadcasted_iota(I32, (LANES, LANES), 0) <= lax.broadcasted_iota(I32, (LANES, LANES), 1),
                     1.0, 0.0).astype(BF16)
    tile_end = jnp.dot(jnp.broadcast_to(ntile, (8, LANES)).astype(BF16), incl,
                       preferred_element_type=F32)[0:1]
    row_start = (tile_end - ntile) * MOE_TB

    def place(i, carry):
        r0 = pl.multiple_of(i * tm, tm)
        rt = route_ref[pl.ds(r0, tm), :]
        rk = rank_scr[pl.ds(r0, tm), :]
        d1 = pick(rt[:, 0:1], row_start) + rk[:, 0:1]
        d2 = pick(rt[:, 1:2], row_start) + rk[:, 1:2]
        dest_ref[pl.ds(r0, tm), :] = two_lanes(d1, d2).astype(I32)
        return carry

    lax.fori_loop(0, n // tm, place, 0)
    trow = lax.broadcasted_iota(I32, (meta_ref.shape[0], 1), 0).astype(F32)
    texp = jnp.sum(jnp.where((tile_end <= trow) & (lane_i < N_EXPERTS), 1.0, 0.0), axis=-1, keepdims=True)
    texp = jnp.minimum(texp, N_EXPERTS - 1.0)
    used = pick(N_EXPERTS - 1.0, tile_end)
    meta_ref[...] = two_lanes(texp, jnp.broadcast_to(used, texp.shape)).astype(I32)


def _rank(route):
    n = route.shape[0]
    return pl.pallas_call(
        _rank_kernel,
        out_shape=[jax.ShapeDtypeStruct((n, LANES), I32), jax.ShapeDtypeStruct((LANES, LANES), I32)],
        scratch_shapes=[pltpu.VMEM((n, LANES), F32)],
        compiler_params=pltpu.CompilerParams(vmem_limit_bytes=VMEM_LIMIT),
        name="moe_rank",
    )(route)


def _dispatch_kernel(dest_ref, h_ref, xs_in_ref, xs_ref, sem):
    del xs_in_ref
    npair = dest_ref.shape[0]
    ch = DISPATCH_CH

    def row_copy(p):
        return pltpu.make_async_copy(h_ref.at[pl.ds(p // TOP_K, 1)], xs_ref.at[pl.ds(dest_ref[p], 1)], sem)

    def group_wait():
        pltpu.make_async_copy(h_ref.at[pl.ds(0, ch)], xs_ref.at[pl.ds(0, ch)], sem).wait()

    def group(c, carry):
        def issue(i, carry2):
            row_copy(c * ch + i).start()
            return carry2

        lax.fori_loop(0, ch, issue, 0)

        @pl.when(c > 0)
        def _():
            group_wait()

        return carry

    lax.fori_loop(0, npair // ch, group, 0)
    group_wait()


def _dispatch(dest_flat, h, cap):
    n, d = h.shape
    xs0 = jnp.zeros((cap, d), F32)
    return pl.pallas_call(
        _dispatch_kernel,
        grid_spec=pltpu.PrefetchScalarGridSpec(
            num_scalar_prefetch=1, grid=(1,),
            in_specs=[pl.BlockSpec(memory_space=pl.ANY), pl.BlockSpec(memory_space=pl.ANY)],
            out_specs=pl.BlockSpec(memory_space=pl.ANY),
            scratch_shapes=[pltpu.SemaphoreType.DMA(())]),
        out_shape=jax.ShapeDtypeStruct((cap, d), F32),
        input_output_aliases={2: 0},
        compiler_params=pltpu.CompilerParams(dimension_semantics=("arbitrary",), has_side_effects=True),
        name="moe_dispatch",
    )(dest_flat, h, xs0)


def _ffn_kernel(meta_ref, x_ref, wg_ref, wu_ref, wd_ref, y_ref, wgb, wub, wdb):
    t = pl.program_id(0)
    ntile = pl.num_programs(0)
    e = meta_ref[t]
    e_prev = meta_ref[jnp.maximum(t - 1, 0)]
    active = t < meta_ref[ntile]

    @pl.when(active & ((t == 0) | (e != e_prev)))
    def _():
        wgb[...] = wg_ref[0].astype(BF16)
        wub[...] = wu_ref[0].astype(BF16)
        wdb[...] = wd_ref[0].astype(BF16)

    @pl.when(active)
    def _():
        x = x_ref[...].astype(BF16)
        gate = jnp.dot(x, wgb[...], preferred_element_type=F32)
        up = jnp.dot(x, wub[...], preferred_element_type=F32)
        act = (gate * jax.nn.sigmoid(gate)) * up
        y_ref[...] = jnp.dot(act.astype(BF16), wdb[...], preferred_element_type=F32)

    @pl.when(jnp.logical_not(active))
    def _():
        y_ref[...] = jnp.zeros(y_ref.shape, F32)


def _ffn(meta_flat, xs, wg, wu, wd):
    cap, d = xs.shape
    ff = wg.shape[2]
    tb = MOE_TB
    ntile = cap // tb
    return pl.pallas_call(
        _ffn_kernel,
        grid_spec=pltpu.PrefetchScalarGridSpec(
            num_scalar_prefetch=1, grid=(ntile,),
            in_specs=[pl.BlockSpec((tb, d), lambda t, m: (t, 0)),
                      pl.BlockSpec((1, d, ff), lambda t, m: (m[t], 0, 0)),
                      pl.BlockSpec((1, d, ff), lambda t, m: (m[t], 0, 0)),
                      pl.BlockSpec((1, ff, d), lambda t, m: (m[t], 0, 0))],
            out_specs=pl.BlockSpec((tb, d), lambda t, m: (t, 0)),
            scratch_shapes=[pltpu.VMEM((d, ff), BF16), pltpu.VMEM((d, ff), BF16), pltpu.VMEM((ff, d), BF16)]),
        out_shape=jax.ShapeDtypeStruct((cap, d), F32),
        compiler_params=_cparams(("arbitrary",)),
        name="moe_ffn",
    )(meta_flat, xs, wg, wu, wd)


def _combine_kernel(dest_ref, y_ref, x1_ref, route_ref, g2_ref, nf_ref, o_ref, ybuf, sem):
    i = pl.program_id(0)
    nstep = pl.num_programs(0)
    tm = x1_ref.shape[0]

    def issue(tile, slot):
        def body(r, carry):
            p = (tile * tm + r) * TOP_K
            for k in range(TOP_K):
                pltpu.make_async_copy(y_ref.at[pl.ds(dest_ref[p + k], 1)], ybuf.at[slot, k, pl.ds(r, 1)],
                                      sem.at[slot]).start()
            return carry

        lax.fori_loop(0, tm, body, 0)

    @pl.when(i == 0)
    def _():
        issue(0, 0)

    @pl.when(i + 1 < nstep)
    def _():
        issue(i + 1, (i + 1) % 2)

    slot = i % 2
    for k in range(TOP_K):
        pltpu.make_async_copy(y_ref.at[pl.ds(0, tm)], ybuf.at[slot, k], sem.at[slot]).wait()
    rt = route_ref[...]
    moe = rt[:, 2:3] * ybuf[slot, 0] + rt[:, 3:4] * ybuf[slot, 1]
    xo = x1_ref[...] + g2_ref[0, 0] * moe
    o_ref[...] = xo * lax.rsqrt(jnp.mean(xo * xo, axis=-1, keepdims=True) + EPS) * nf_ref[...]


def _combine(dest_flat, y, x1, route, mod4, nf, seq):
    n, d = x1.shape
    tm = COMB_TM
    tiles_per_seq = seq // tm
    return pl.pallas_call(
        _combine_kernel,
        grid_spec=pltpu.PrefetchScalarGridSpec(
            num_scalar_prefetch=1, grid=(n // tm,),
            in_specs=[pl.BlockSpec(memory_space=pl.ANY),
                      pl.BlockSpec((tm, d), lambda i, dst: (i, 0)),
                      pl.BlockSpec((tm, LANES), lambda i, dst: (i, 0)),
                      pl.BlockSpec((1, 1, 1, d), lambda i, dst: (i // tiles_per_seq, 5, 0, 0)),
                      pl.BlockSpec((1, d), lambda i, dst: (0, 0))],
            out_specs=pl.BlockSpec((tm, d), lambda i, dst: (i, 0)),
            scratch_shapes=[pltpu.VMEM((2, TOP_K, tm, d), F32), pltpu.SemaphoreType.DMA((2,))]),
        out_shape=jax.ShapeDtypeStruct((n, d), F32),
        compiler_params=_cparams(("arbitrary",)),
        name="moe_combine",
    )(dest_flat, y, x1, route, mod4, nf)


def _alibi_slopes():
    n = NSA_HEADS
    return jnp.asarray(2.0 ** (-8.0 * np.arange(1, n + 1) / n), dtype=F32)


def _layer(x, c, w_ada, b_ada, norm1_w, w_in, cmp_pos_k, cmp_w1_k, cmp_w2_k, cmp_pos_v, cmp_w1_v, cmp_w2_v,
           gla_w_gate2, gla_b_gate, gla_norm_w, w_out, norm2_w, w_rg, b_rg, w_re, b_re, w_eg, w_eu, w_ed):
    bsz, s, d = x.shape
    dh = NSA_HEAD_DIM
    mod4 = _adaln(c, w_ada, b_ada).reshape(bsz, 6, 1, d)

    o_gate = NSA_COLS
    o_gla = o_gate + NSA_GATE_COLS
    o_lr = o_gla + 2 * GLA_HEADS * GLA_DK + 2 * GLA_HEADS * GLA_DV
    w_nsa = w_in[:, :NSA_COLS].astype(BF16)
    w_gla = jnp.concatenate(
        [w_in[:, o_gla:o_lr], w_in[:, o_gate:o_gla], w_in[:, o_lr:],
         jnp.zeros((d, LANES - NSA_GATE_COLS - GLA_GATE_RANK), F32)], axis=1).astype(BF16)
    nw1 = norm1_w.reshape(1, d)
    proj_nsa = _inproj(x, mod4, nw1, w_nsa, sc_idx=1, sh_idx=0, tn=INPROJ_TN_NSA, head_major=True)
    proj_gla = _inproj(x, mod4, nw1, w_gla, sc_idx=1, sh_idx=0, tn=INPROJ_TN_GLA, head_major=False)

    nkv = 2 * NSA_KV_HEADS
    chunk_w = CMP_STRIDE * dh
    kv_chunks = proj_nsa[:, NSA_HEADS:NSA_HEADS + nkv].reshape(bsz, nkv, s // CMP_STRIDE, chunk_w)
    pos = jnp.stack([cmp_pos_k, cmp_pos_v]).reshape(2, 2, chunk_w)
    kvc = _compress(kv_chunks, pos, jnp.stack([cmp_w1_k, cmp_w1_v]), jnp.stack([cmp_w2_k, cmp_w2_v]))
    o_nsa = _nsa(_alibi_slopes(), proj_nsa, kvc, proj_gla)

    wg_pad = jnp.zeros((LANES, GLA_HEADS * GLA_DK), F32).at[
        NSA_GATE_COLS:NSA_GATE_COLS + GLA_GATE_RANK].set(gla_w_gate2)
    o_gla_out = _gla(proj_gla, wg_pad, gla_b_gate.reshape(1, -1), gla_norm_w.reshape(1, -1))

    wr = jnp.concatenate([w_re, w_rg, jnp.zeros((d, LANES - N_EXPERTS - N_GROUPS), F32)], axis=1)
    br = jnp.concatenate([b_re, b_rg, jnp.zeros((LANES - N_EXPERTS - N_GROUPS,), F32)]).reshape(1, LANES)
    x1, h2, route = _outproj(o_nsa, o_gla_out, x, w_out.astype(BF16), mod4, norm2_w.reshape(1, d), wr, br)

    n = bsz * s
    npair = n * TOP_K
    cap = npair + N_EXPERTS * MOE_TB
    ntile = cap // MOE_TB
    route2 = route.reshape(n, LANES)
    dest, meta = _rank(route2)
    dest_flat = dest[:, :TOP_K].reshape(npair)
    meta_flat = jnp.concatenate([meta[:ntile, 0], meta[:1, 1]])
    xs = _dispatch(dest_flat, h2.reshape(n, d), cap)
    y = _ffn(meta_flat, xs, w_eg, w_eu, w_ed)
    return x1.reshape(n, d), y, dest_flat, route2, mod4


def kernel(x, c, w_ada, b_ada, norm1_w, w_in, cmp_pos_k, cmp_w1_k, cmp_w2_k, cmp_pos_v, cmp_w1_v, cmp_w2_v,
           gla_w_gate2, gla_b_gate, gla_norm_w, w_out, norm2_w, w_router_group, b_router_group, w_router_expert,
           b_router_expert, w_expert_gate, w_expert_up, w_expert_down, norm_f_w):
    bsz, s, d = x.shape
    assert w_ada.shape[0] == 1, "single layer"
    x1, y, dest_flat, route2, mod4 = _layer(
        x, c, w_ada[0], b_ada[0], norm1_w[0], w_in[0], cmp_pos_k[0], cmp_w1_k[0], cmp_w2_k[0], cmp_pos_v[0],
        cmp_w1_v[0], cmp_w2_v[0], gla_w_gate2[0], gla_b_gate[0], gla_norm_w[0], w_out[0], norm2_w[0],
        w_router_group[0], b_router_group[0], w_router_expert[0], b_router_expert[0],
        w_expert_gate[0], w_expert_up[0], w_expert_down[0])
    out = _combine(dest_flat, y, x1, route2, mod4, norm_f_w.reshape(1, d), s)
    return out.reshape(bsz, s, d)
```

```python
import functools

import numpy as np
import jax
import jax.numpy as jnp
from jax import lax
from jax.experimental import pallas as pl
from jax.experimental.pallas import tpu as pltpu

F32 = jnp.float32
BF16 = jnp.bfloat16
I32 = jnp.int32
HI = lax.Precision.HIGHEST

D_MODEL = 2048
NSA_HEAD_DIM = 64
NSA_HEADS = 16
NSA_KV_HEADS = 4
NSA_Q_PER_KV = 4
CMP_BLOCK = 32
CMP_STRIDE = 16
SEL_BLOCK = 64
N_SEL = 16
WINDOW = 512
N_BRANCH = 3
GLA_HEADS = 4
GLA_DV = 256
GLA_DK = 128
GLA_GATE_RANK = 16
GLA_GATE_NORM = 16.0
GLA_CHUNK = 64
GLA_SUB = 16
N_GROUPS = 4
EXPERTS_PER_GROUP = 8
N_EXPERTS = 32
TOP_K = 2
EXPERT_FF = 512
EPS = 1e-6
NEG = -1e30
FORCE = 1e30
MASKED = 2.0 ** 100

NSA_Q_COLS = NSA_HEADS * NSA_HEAD_DIM
NSA_KV_COLS = 2 * N_BRANCH * NSA_KV_HEADS * NSA_HEAD_DIM
NSA_GATE_COLS = N_BRANCH * NSA_HEADS
NSA_COLS = NSA_Q_COLS + NSA_KV_COLS
NSA_SLOTS = NSA_COLS // NSA_HEAD_DIM
GLA_Q_OFF = 0
GLA_K_OFF = GLA_HEADS * GLA_DK
GLA_V_OFF = 2 * GLA_HEADS * GLA_DK
GLA_OG_OFF = GLA_V_OFF + GLA_HEADS * GLA_DV
GLA_MISC_OFF = GLA_OG_OFF + GLA_HEADS * GLA_DV
LANES = 128
GLA_COLS = GLA_MISC_OFF + LANES

VMEM_LIMIT = 56 * 1024 * 1024

ADA_TN = 768
INPROJ_TM = 512
INPROJ_TN_NSA = 512
INPROJ_TN_GLA = 640
NSA_TQ = 256
NSA_TK = 256
OUT_TM = 256
RANK_TM = 256
MOE_TB = 256
DISPATCH_TM = 256
COMB_TM = 256


def _cparams(sem):
    return pltpu.CompilerParams(dimension_semantics=sem, vmem_limit_bytes=VMEM_LIMIT)


def _adaln_kernel(ct_ref, w_ref, b_ref, o_ref, s_scr):
    nb = ct_ref.shape[1]
    kdim, tn = w_ref.shape

    @pl.when(pl.program_id(0) == 0)
    def _():
        ct = ct_ref[...]
        s = ct * jax.nn.sigmoid(ct)
        for b in range(nb):
            s_scr[b] = jnp.broadcast_to(s[:, b:b + 1], (kdim, LANES))

    def body(k, accs):
        r = pl.multiple_of(k * 8, 8)
        w8 = w_ref[pl.ds(r, 8), :]
        out = []
        for b, acc in enumerate(accs):
            s8 = s_scr[b, pl.ds(r, 8), :]
            out.append(acc + w8 * jnp.concatenate([s8] * (tn // LANES), axis=1))
        return tuple(out)

    accs = lax.fori_loop(0, kdim // 8, body, tuple(jnp.zeros((8, tn), F32) for _ in range(nb)), unroll=2)
    bias = b_ref[...]
    for b, acc in enumerate(accs):
        o_ref[b:b + 1, :] = jnp.sum(acc, axis=0, keepdims=True) + bias


def _adaln(c, w, b):
    nb, d = c.shape
    n = w.shape[1]
    return pl.pallas_call(
        _adaln_kernel,
        grid=(n // ADA_TN,),
        in_specs=[pl.BlockSpec((d, nb), lambda j: (0, 0)),
                  pl.BlockSpec((d, ADA_TN), lambda j: (0, j)),
                  pl.BlockSpec((1, ADA_TN), lambda j: (0, j))],
        out_specs=pl.BlockSpec((nb, ADA_TN), lambda j: (0, j)),
        out_shape=jax.ShapeDtypeStruct((nb, n), F32),
        scratch_shapes=[pltpu.VMEM((nb, d, LANES), F32)],
        compiler_params=_cparams(("arbitrary",)),
        name="adaln",
    )(c.T, w, b.reshape(1, n))


def _modulated_norm(x, nw, sc, sh):
    ms = jnp.mean(x * x, axis=-1, keepdims=True)
    h = x * lax.rsqrt(ms + EPS) * nw
    return h * (1.0 + sc) + sh


def _inproj_kernel(x_ref, sc_ref, sh_ref, nw_ref, w_ref, o_ref, h_scr, *, head_major):
    @pl.when(pl.program_id(2) == 0)
    def _():
        h = _modulated_norm(x_ref[0], nw_ref[...], sc_ref[0, 0], sh_ref[0, 0])
        h_scr[...] = h.astype(BF16)

    acc = jnp.dot(h_scr[...], w_ref[...], preferred_element_type=F32)
    if head_major:
        for u in range(acc.shape[1] // NSA_HEAD_DIM):
            o_ref[0, u] = acc[:, u * NSA_HEAD_DIM:(u + 1) * NSA_HEAD_DIM]
    else:
        o_ref[0] = acc


def _inproj(x, mod4, nw, w, *, sc_idx, sh_idx, tn, head_major):
    bsz, s, d = x.shape
    n = w.shape[1]
    tm = INPROJ_TM
    if head_major:
        upb = tn // NSA_HEAD_DIM
        out_shape = jax.ShapeDtypeStruct((bsz, n // NSA_HEAD_DIM, s, NSA_HEAD_DIM), F32)
        out_spec = pl.BlockSpec((1, upb, tm, NSA_HEAD_DIM), lambda b, i, j: (b, j, i, 0))
    else:
        out_shape = jax.ShapeDtypeStruct((bsz, s, n), F32)
        out_spec = pl.BlockSpec((1, tm, tn), lambda b, i, j: (b, i, j))
    return pl.pallas_call(
        functools.partial(_inproj_kernel, head_major=head_major),
        grid=(bsz, s // tm, n // tn),
        in_specs=[pl.BlockSpec((1, tm, d), lambda b, i, j: (b, i, 0)),
                  pl.BlockSpec((1, 1, 1, d), lambda b, i, j: (b, sc_idx, 0, 0)),
                  pl.BlockSpec((1, 1, 1, d), lambda b, i, j: (b, sh_idx, 0, 0)),
                  pl.BlockSpec((1, d), lambda b, i, j: (0, 0)),
                  pl.BlockSpec((d, tn), lambda b, i, j: (0, j))],
        out_specs=out_spec,
        out_shape=out_shape,
        scratch_shapes=[pltpu.VMEM((tm, d), BF16)],
        compiler_params=_cparams(("parallel", "parallel", "arbitrary")),
        name="inproj_nsa" if head_major else "inproj_gla",
    )(x, mod4, mod4, nw, w)


def _compress_kernel(a_ref, pos_ref, w1_ref, w2_ref, o_ref):
    a = a_ref[0, 0]
    pos = pos_ref[0]
    half = a.shape[1]
    y1 = jnp.dot(a + pos[0:1], w1_ref[0, :half, :], precision=HI, preferred_element_type=F32)
    y2 = jnp.dot(a + pos[1:2], w1_ref[0, half:, :], precision=HI, preferred_element_type=F32)
    nrow = a.shape[0]
    h = y1 + pltpu.roll(y2, nrow - 1, axis=0)
    out = jnp.dot(jax.nn.gelu(h), w2_ref[0], precision=HI, preferred_element_type=F32)
    row = lax.broadcasted_iota(I32, out.shape, 0)
    o_ref[0, 0] = jnp.where(row < nrow - 1, out, 0.0)


def _compress(kv_chunks, pos, w1, w2):
    bsz, nslot, nch, cw = kv_chunks.shape
    dh = NSA_HEAD_DIM
    return pl.pallas_call(
        _compress_kernel,
        grid=(bsz, nslot),
        in_specs=[pl.BlockSpec((1, 1, nch, cw), lambda b, t: (b, t, 0, 0)),
                  pl.BlockSpec((1, 2, cw), lambda b, t: (t // NSA_KV_HEADS, 0, 0)),
                  pl.BlockSpec((1, 2 * cw, dh), lambda b, t: (t // NSA_KV_HEADS, 0, 0)),
                  pl.BlockSpec((1, dh, dh), lambda b, t: (t // NSA_KV_HEADS, 0, 0))],
        out_specs=pl.BlockSpec((1, 1, nch, dh), lambda b, t: (b, t, 0, 0)),
        out_shape=jax.ShapeDtypeStruct((bsz, nslot, nch, dh), F32),
        compiler_params=_cparams(("parallel", "parallel")),
        name="nsa_compress",
    )(kv_chunks, pos, w1, w2)


def _nt_dot(a, b, **kw):
    return lax.dot_general(a, b, (((1,), (1,)), ((), ())), preferred_element_type=F32, **kw)


def _nsa_kernel(slopes_ref, q_ref, kc_ref, vc_ref, ks_ref, vs_ref, kw_ref, vw_ref, gate_ref, o_ref,
                ksb, vst, kwb, vwt, s_scr, p_scr, a_scr, m_s, l_s, acc_s, m_w, l_w, acc_w):
    g = pl.program_id(1)
    qi = pl.program_id(2)
    tq_n = q_ref.shape[2]
    dh = NSA_HEAD_DIM
    nr = NSA_Q_PER_KV
    seq = ks_ref.shape[2]
    tk_n = NSA_TK
    nb = seq // SEL_BLOCK

    @pl.when(qi == 0)
    def _():
        row = lax.broadcasted_iota(I32, (seq, dh), 0)
        lane = lax.broadcasted_iota(I32, (seq, dh), 1)
        blk = row // SEL_BLOCK
        pos = jnp.where((lane >= nb) & (lane < nb + 3), (blk * SEL_BLOCK).astype(F32),
                        jnp.where((lane >= nb + 3) & (lane < nb + 6), (row % SEL_BLOCK).astype(F32), 0.0))
        ksb[...] = jnp.concatenate([jnp.where(lane == blk, -MASKED, pos), ks_ref[0, 0]], axis=1).astype(BF16)
        kwb[...] = jnp.concatenate([pos, kw_ref[0, 0]], axis=1).astype(BF16)
        for c in range(seq // tk_n):
            rows = slice(c * tk_n, (c + 1) * tk_n)
            for src, dst in ((vs_ref, vst), (vw_ref, vwt)):
                v = src[0, 0, rows, :]
                dst[c] = jnp.concatenate([v, v], axis=1).T[:dh].astype(BF16)

    t0 = qi * tq_n
    tq = t0 + lax.broadcasted_iota(I32, (tq_n, 1), 0)
    tq_f = tq.astype(F32)
    slopes = [slopes_ref[g * nr + r] for r in range(nr)]
    scale = dh ** -0.5
    qs = [q_ref[0, r] * scale for r in range(nr)]

    ncp = kc_ref.shape[2]
    kc = kc_ref[0, 0]
    vc_b = vc_ref[0, 0].astype(BF16)
    n_lane = lax.broadcasted_iota(I32, (1, ncp), 1)
    blk_end = n_lane * CMP_STRIDE + (CMP_BLOCK - 1)
    center = n_lane.astype(F32) * CMP_STRIDE + (CMP_BLOCK - 1) / 2.0
    valid_c = blk_end <= tq
    dist_c = tq_f - center
    psum = jnp.zeros((tq_n, ncp), F32)
    o_c = []
    for r in range(nr):
        s = _nt_dot(qs[r], kc, precision=HI)
        s = jnp.where(valid_c, s - slopes[r] * dist_c, NEG)
        e = jnp.exp(s - jnp.max(s, axis=-1, keepdims=True))
        p = jnp.where(valid_c, e / jnp.sum(e, axis=-1, keepdims=True), 0.0)
        psum = psum + p
        o_c.append(jnp.dot(p.astype(BF16), vc_b, preferred_element_type=F32))

    rown = lax.broadcasted_iota(I32, (ncp, LANES), 0) * CMP_STRIDE
    colj = lax.broadcasted_iota(I32, (ncp, LANES), 1) * SEL_BLOCK
    overlap = jnp.where((rown < colj + SEL_BLOCK) & (rown + CMP_BLOCK > colj)
                        & (rown < (ncp - 1) * CMP_STRIDE) & (colj < nb * SEL_BLOCK), 1.0, 0.0)
    imp = jnp.dot(psum, overlap, precision=HI, preferred_element_type=F32)
    j_lane = lax.broadcasted_iota(I32, (1, LANES), 1)
    qblk = tq // SEL_BLOCK
    forced = (j_lane == 0) | (j_lane == qblk) | (j_lane == qblk - 1)
    imp = jnp.where(forced, FORCE, jnp.where(j_lane <= qblk, imp, NEG))
    cnt = jnp.zeros((tq_n, LANES), F32)
    for i in range(nb):
        ci = imp[:, i:i + 1]
        tie = jnp.where(j_lane > i, 1.0, 0.0)
        cnt = cnt + jnp.where(ci > imp, 1.0, jnp.where(ci == imp, tie, 0.0))
    notsel = jnp.where(cnt < float(min(N_SEL, nb)), 0.0, 1.0)[:, :dh]

    lane_h = lax.broadcasted_iota(I32, (1, dh), 1)
    q_aug = []
    for r in range(nr):
        srow = jnp.zeros((1, dh), F32)
        for i in range(3):
            piece = slopes_ref[(i + 1) * NSA_HEADS + g * nr + r]
            srow = jnp.where((lane_h == nb + i) | (lane_h == nb + 3 + i), piece, srow)
        q_aug.append(jnp.concatenate([jnp.where(lane_h < nb, notsel, srow), qs[r]], axis=1))
    q4 = jnp.concatenate(q_aug, axis=0).astype(BF16)

    for m_ref, l_ref, acc_ref in ((m_s, l_s, acc_s), (m_w, l_w, acc_w)):
        m_ref[...] = jnp.full(m_ref.shape, NEG, F32)
        l_ref[...] = jnp.zeros(l_ref.shape, F32)
        acc_ref[...] = jnp.zeros(acc_ref.shape, F32)
    key_i = lax.broadcasted_iota(I32, (tk_n, LANES), 0)
    qry_j = lax.broadcasted_iota(I32, (tk_n, LANES), 1)

    def tile(k_ref, vt_ref, kt, mode, m_ref, l_ref, acc_ref):
        s_scr[...] = _nt_dot(k_ref[pl.ds(pl.multiple_of(kt * tk_n, tk_n), tk_n), :], q4)
        for cb in range(nr * tq_n // LANES):
            cols = slice(cb * LANES, (cb + 1) * LANES)
            s = s_scr[:, cols]
            if mode is not None:
                j = qry_j + (cb * LANES) % tq_n
                s = jnp.where(key_i <= j if mode == "causal" else key_i > j, s, -MASKED)
            m_prev = m_ref[:, cols]
            m_new = jnp.maximum(m_prev, jnp.max(s, axis=0, keepdims=True))
            alpha = jnp.exp(m_prev - m_new)
            p = jnp.exp(s - m_new)
            l_ref[:, cols] = alpha * l_ref[:, cols] + jnp.sum(p, axis=0, keepdims=True)
            m_ref[:, cols] = m_new
            a_scr[:, cols] = alpha
            p_scr[:, cols] = p.astype(BF16)
        pv = jnp.dot(vt_ref[kt], p_scr[...], preferred_element_type=F32)
        acc_ref[...] = acc_ref[...] * a_scr[...] + pv

    def sel_step(kt, carry):
        tile(ksb, vst, kt, None, m_s, l_s, acc_s)
        return carry

    lax.fori_loop(0, qi, sel_step, 0)
    tile(ksb, vst, qi, "causal", m_s, l_s, acc_s)

    nwt = WINDOW // tk_n
    for back in range(nwt, 0, -1):
        @pl.when(qi >= back)
        def _(back=back):
            tile(kwb, vwt, qi - back, "band" if back == nwt else None, m_w, l_w, acc_w)

    tile(kwb, vwt, qi, "causal", m_w, l_w, acc_w)

    gsel = jnp.where(lax.broadcasted_iota(I32, (LANES, LANES), 0)
                     == lax.broadcasted_iota(I32, (LANES, LANES), 1) + g * (nr * N_BRANCH), 1.0, 0.0)
    gates = jax.nn.sigmoid(jnp.dot(gate_ref[0], gsel, precision=HI, preferred_element_type=F32))
    gates_t = gates.T
    inv_s = 1.0 / l_s[...]
    inv_w = 1.0 / l_w[...]
    for pair in range(nr // 2):
        o_t, o_cmp = [], []
        for r in (2 * pair, 2 * pair + 1):
            cols = slice(r * tq_n, (r + 1) * tq_n)
            c0 = r * N_BRANCH
            o_t.append((gates_t[c0 + 1:c0 + 2, :] * inv_s[:, cols]) * acc_s[:, cols]
                       + (gates_t[c0 + 2:c0 + 3, :] * inv_w[:, cols]) * acc_w[:, cols])
            o_cmp.append(gates[:, c0:c0 + 1] * o_c[r])
        o_ref[0, :, pair * LANES:(pair + 1) * LANES] = (
            jnp.concatenate(o_t, axis=0).T + jnp.concatenate(o_cmp, axis=1))


def _nsa(slopes, proj_nsa, kvc, proj_gla):
    bsz, _, s, dh = proj_nsa.shape
    g_n, nr = NSA_KV_HEADS, NSA_Q_PER_KV
    tq = NSA_TQ
    tk = NSA_TK
    assert tq == tk and WINDOW % tk == 0 and 2 * dh == LANES
    ncp = kvc.shape[2]
    kv0 = NSA_HEADS
    nq = nr * tq

    def kv_spec(i):
        return pl.BlockSpec((1, 1, s, dh), lambda b, g, q, i=i: (b, kv0 + i * g_n + g, 0, 0))

    return pl.pallas_call(
        _nsa_kernel,
        grid=(bsz, g_n, s // tq),
        in_specs=[pl.BlockSpec(memory_space=pltpu.SMEM),
                  pl.BlockSpec((1, nr, tq, dh), lambda b, g, q: (b, g, q, 0)),
                  pl.BlockSpec((1, 1, ncp, dh), lambda b, g, q: (b, g, 0, 0)),
                  pl.BlockSpec((1, 1, ncp, dh), lambda b, g, q: (b, g_n + g, 0, 0)),
                  kv_spec(2), kv_spec(3), kv_spec(4), kv_spec(5),
                  pl.BlockSpec((1, tq, LANES), lambda b, g, q: (b, q, GLA_MISC_OFF // LANES))],
        out_specs=pl.BlockSpec((1, tq, nr * dh), lambda b, g, q: (b, q, g)),
        out_shape=jax.ShapeDtypeStruct((bsz, s, NSA_HEADS * dh), F32),
        scratch_shapes=[pltpu.VMEM((s, LANES), BF16), pltpu.VMEM((s // tk, dh, tk), BF16)] * 2 + [
            pltpu.VMEM((tk, nq), F32), pltpu.VMEM((tk, nq), BF16), pltpu.VMEM((1, nq), F32)] + [
            pltpu.VMEM((1, nq), F32), pltpu.VMEM((1, nq), F32), pltpu.VMEM((dh, nq), F32)] * 2,
        compiler_params=_cparams(("parallel", "parallel", "arbitrary")),
        name="nsa_attention",
    )(slopes, proj_nsa, kvc, kvc, proj_nsa, proj_nsa, proj_nsa, proj_nsa, proj_gla)


def _gla_kernel(q_ref, k_ref, v_ref, og_ref, lr_ref, wg_ref, bg_ref, nw_ref, o_ref, st_scr, la_scr, b_scr):
    seq = q_ref.shape[1]
    c_n, sub = GLA_CHUNK, GLA_SUB
    dk = GLA_DK
    z = jnp.dot(lr_ref[0], wg_ref[...], precision=HI, preferred_element_type=F32) + bg_ref[...]
    la_scr[...] = (jnp.minimum(z, 0.0) - jnp.log1p(jnp.exp(-jnp.abs(z)))) * (1.0 / GLA_GATE_NORM)
    st_scr[...] = jnp.zeros(st_scr.shape, F32)
    tril = jnp.where(lax.broadcasted_iota(I32, (c_n, c_n), 0) >= lax.broadcasted_iota(I32, (c_n, c_n), 1), 1.0, 0.0)
    row_c = lax.broadcasted_iota(I32, (c_n, 1), 0)
    row_s = lax.broadcasted_iota(I32, (sub, 1), 0)
    lane_c = lax.broadcasted_iota(I32, (1, c_n), 1)
    nw = nw_ref[...]

    def chunk(c, carry):
        r0 = pl.multiple_of(c * c_n, c_n)
        qc = q_ref[0, pl.ds(r0, c_n), :] * (dk ** -0.5)
        kc = k_ref[0, pl.ds(r0, c_n), :]
        vc = v_ref[0, pl.ds(r0, c_n), :].astype(BF16)
        b = jnp.dot(tril, la_scr[pl.ds(r0, c_n), :], precision=HI, preferred_element_type=F32)
        b_scr[...] = b
        st = st_scr[...]
        o = _nt_dot((qc * jnp.exp(b)).astype(BF16), st.astype(BF16))
        strips = []
        for blk in range(c_n // sub):
            lo = blk * sub
            q_i = qc[lo:lo + sub]
            b_i = b[lo:lo + sub]
            a = jnp.zeros((sub, c_n), F32)
            for j in range(sub):
                b_j = b_scr[lo + j:lo + j + 1, :]
                k_j = k_ref[0, pl.ds(r0 + lo + j, 1), :]
                e = jnp.exp(jnp.where(row_s >= j, b_i - b_j, -jnp.inf))
                col = jnp.sum(q_i * k_j * e, axis=-1, keepdims=True)
                a = jnp.where(lane_c == lo + j, col, a)
            if blk > 0:
                b_r = b_scr[lo - 1:lo, :]
                q_d = q_i * jnp.exp(b_i - b_r)
                k_d = kc * jnp.exp(jnp.where(row_c < lo, b_r - b, -jnp.inf))
                a = a + _nt_dot(q_d.astype(BF16), k_d.astype(BF16))
            strips.append(a)
        attn = jnp.concatenate(strips, axis=0)
        o = o + jnp.dot(attn.astype(BF16), vc, preferred_element_type=F32)
        b_last = b_scr[c_n - 1:c_n, :]
        k_dec = (kc * jnp.exp(b_last - b)).astype(BF16)
        st_scr[...] = st * jnp.exp(b_last) + lax.dot_general(
            vc, k_dec, (((0,), (0,)), ((), ())), preferred_element_type=F32)
        o = o * lax.rsqrt(jnp.mean(o * o, axis=-1, keepdims=True) + EPS) * nw
        og = og_ref[0, pl.ds(r0, c_n), :]
        o_ref[0, pl.ds(r0, c_n), :] = o * (og * jax.nn.sigmoid(og))
        return carry

    lax.fori_loop(0, seq // c_n, chunk, 0)


def _gla(proj_gla, wg_pad, bg, nw):
    bsz, s, _ = proj_gla.shape
    h_n, dk, dv = GLA_HEADS, GLA_DK, GLA_DV
    return pl.pallas_call(
        _gla_kernel,
        grid=(bsz, h_n),
        in_specs=[pl.BlockSpec((1, s, dk), lambda b, h: (b, 0, GLA_Q_OFF // dk + h)),
                  pl.BlockSpec((1, s, dk), lambda b, h: (b, 0, GLA_K_OFF // dk + h)),
                  pl.BlockSpec((1, s, dv), lambda b, h: (b, 0, GLA_V_OFF // dv + h)),
                  pl.BlockSpec((1, s, dv), lambda b, h: (b, 0, GLA_OG_OFF // dv + h)),
                  pl.BlockSpec((1, s, LANES), lambda b, h: (b, 0, GLA_MISC_OFF // LANES)),
                  pl.BlockSpec((LANES, dk), lambda b, h: (0, h)),
                  pl.BlockSpec((1, dk), lambda b, h: (0, h)),
                  pl.BlockSpec((1, dv), lambda b, h: (0, 0))],
        out_specs=pl.BlockSpec((1, s, dv), lambda b, h: (b, 0, h)),
        out_shape=jax.ShapeDtypeStruct((bsz, s, h_n * dv), F32),
        scratch_shapes=[pltpu.VMEM((dv, dk), F32), pltpu.VMEM((s, dk), F32), pltpu.VMEM((GLA_CHUNK, dk), F32)],
        compiler_params=_cparams(("parallel", "parallel")),
        name="gla",
    )(proj_gla, proj_gla, proj_gla, proj_gla, proj_gla, wg_pad, bg, nw)


def _outproj_kernel(nsa_ref, gla_ref, x_ref, wo_ref, g1_ref, sc_ref, sh_ref, nw_ref, wr_ref, br_ref,
                    x1_ref, h_ref, route_ref):
    half = nsa_ref.shape[2]
    acc = jnp.dot(nsa_ref[0].astype(BF16), wo_ref[:half, :], preferred_element_type=F32)
    acc = acc + jnp.dot(gla_ref[0].astype(BF16), wo_ref[half:, :], preferred_element_type=F32)
    x1 = x_ref[0] + g1_ref[0, 0] * acc
    x1_ref[0] = x1
    h = _modulated_norm(x1, nw_ref[...], sc_ref[0, 0], sh_ref[0, 0])
    h_ref[0] = h
    logits = jnp.dot(h, wr_ref[...], precision=HI, preferred_element_type=F32) + br_ref[...]
    lane = lax.broadcasted_iota(I32, (1, LANES), 1)
    ninf = -jnp.inf
    is_g = (lane >= N_EXPERTS) & (lane < N_EXPERTS + N_GROUPS)
    gl = jnp.where(is_g, logits, ninf)
    ge = jnp.exp(gl - jnp.max(gl, axis=-1, keepdims=True))
    gp = ge / jnp.sum(ge, axis=-1, keepdims=True)
    gp_max = jnp.max(gp, axis=-1, keepdims=True)
    grp = jnp.min(jnp.where((gp == gp_max) & is_g, lane - N_EXPERTS, LANES), axis=-1, keepdims=True)
    in_grp = (lane // EXPERTS_PER_GROUP == grp) & (lane < N_EXPERTS)
    el = jnp.where(in_grp, logits, ninf)
    v1 = jnp.max(el, axis=-1, keepdims=True)
    i1 = jnp.min(jnp.where(el == v1, lane, LANES), axis=-1, keepdims=True)
    el2 = jnp.where(lane == i1, ninf, el)
    v2 = jnp.max(el2, axis=-1, keepdims=True)
    i2 = jnp.min(jnp.where(el2 == v2, lane, LANES), axis=-1, keepdims=True)
    e2 = jnp.exp(v2 - v1)
    den = 1.0 + e2
    w1 = gp_max * (1.0 / den)
    w2 = gp_max * (e2 / den)
    route_ref[0] = jnp.where(lane == 0, i1.astype(F32), jnp.where(lane == 1, i2.astype(F32), jnp.where(
        lane == 2, w1, jnp.where(lane == 3, w2, 0.0))))


def _outproj(o_nsa, o_gla, x, wo, mod4, nw, wr, br):
    bsz, s, d = x.shape
    tm = OUT_TM
    half = o_nsa.shape[2]

    def mod_spec(idx):
        return pl.BlockSpec((1, 1, 1, d), lambda b, i: (b, idx, 0, 0))

    row = lambda w: pl.BlockSpec((1, tm, w), lambda b, i: (b, i, 0))
    return pl.pallas_call(
        _outproj_kernel,
        grid=(bsz, s // tm),
        in_specs=[row(half), row(half), row(d),
                  pl.BlockSpec((2 * half, d), lambda b, i: (0, 0)),
                  mod_spec(2), mod_spec(4), mod_spec(3),
                  pl.BlockSpec((1, d), lambda b, i: (0, 0)),
                  pl.BlockSpec((d, LANES), lambda b, i: (0, 0)),
                  pl.BlockSpec((1, LANES), lambda b, i: (0, 0))],
        out_specs=[row(d), row(d), row(LANES)],
        out_shape=[jax.ShapeDtypeStruct((bsz, s, d), F32), jax.ShapeDtypeStruct((bsz, s, d), F32),
                   jax.ShapeDtypeStruct((bsz, s, LANES), F32)],
        compiler_params=_cparams(("parallel", "parallel")),
        name="outproj_router",
    )(o_nsa, o_gla, x, wo, mod4, mod4, mod4, nw, wr, br)


def _rank_kernel(route_ref, dest_ref, meta_ref, rank_scr):
    n = route_ref.shape[0]
    tm = RANK_TM
    lane_i = lax.broadcasted_iota(I32, (1, LANES), 1)
    lane = lane_i.astype(F32)
    strict = jnp.where(lax.broadcasted_iota(I32, (tm, tm), 0) > lax.broadcasted_iota(I32, (tm, tm), 1),
                       1.0, 0.0).astype(BF16)

    def two_lanes(a, b):
        return jnp.where(lane_i == 0, a, jnp.where(lane_i == 1, b, 0.0))

    def pick(e, table):
        return jnp.sum(jnp.where(lane == e, table, 0.0), axis=-1, keepdims=True)

    def count(i, seen):
        r0 = pl.multiple_of(i * tm, tm)
        rt = route_ref[pl.ds(r0, tm), :]
        e1, e2 = rt[:, 0:1], rt[:, 1:2]
        member = jnp.where(lane == e1, 1.0, jnp.where(lane == e2, 1.0, 0.0))
        before = jnp.dot(strict, member.astype(BF16), preferred_element_type=F32) + seen
        rank_scr[pl.ds(r0, tm), :] = two_lanes(pick(e1, before), pick(e2, before))
        return seen + jnp.sum(member, axis=0, keepdims=True)

    counts = lax.fori_loop(0, n // tm, count, jnp.zeros((1, LANES), F32))
    ntile = jnp.floor((counts + (MOE_TB - 1)) * (1.0 / MOE_TB))
    incl = jnp.where(lax.broadcasted_iota(I32, (LANES, LANES), 0) <= lax.broadcasted_iota(I32, (LANES, LANES), 1),
                     1.0, 0.0).astype(BF16)
    tile_end = jnp.dot(jnp.broadcast_to(ntile, (8, LANES)).astype(BF16), incl,
                       preferred_element_type=F32)[0:1]
    row_start = (tile_end - ntile) * MOE_TB

    def place(i, carry):
        r0 = pl.multiple_of(i * tm, tm)
        rt = route_ref[pl.ds(r0, tm), :]
        rk = rank_scr[pl.ds(r0, tm), :]
        d1 = pick(rt[:, 0:1], row_start) + rk[:, 0:1]
        d2 = pick(rt[:, 1:2], row_start) + rk[:, 1:2]
        dest_ref[pl.ds(r0, tm), :] = two_lanes(d1, d2).astype(I32)
        return carry

    lax.fori_loop(0, n // tm, place, 0)
    trow = lax.broadcasted_iota(I32, (meta_ref.shape[0], 1), 0).astype(F32)
    texp = jnp.sum(jnp.where((tile_end <= trow) & (lane_i < N_EXPERTS), 1.0, 0.0), axis=-1, keepdims=True)
    texp = jnp.minimum(texp, N_EXPERTS - 1.0)
    used = pick(N_EXPERTS - 1.0, tile_end)
    diag = lax.broadcasted_iota(I32, (meta_ref.shape[0], LANES), 0) == lane_i
    end_rows = jnp.sum(jnp.where(diag, tile_end, 0.0), axis=-1, keepdims=True)
    ntile_rows = jnp.sum(jnp.where(diag, ntile, 0.0), axis=-1, keepdims=True)
    meta_ref[...] = jnp.where(lane_i == 2, end_rows, jnp.where(lane_i == 3, ntile_rows, two_lanes(
        texp, jnp.broadcast_to(used, texp.shape)))).astype(I32)


def _rank(route):
    n = route.shape[0]
    return pl.pallas_call(
        _rank_kernel,
        out_shape=[jax.ShapeDtypeStruct((n, LANES), I32), jax.ShapeDtypeStruct((LANES, LANES), I32)],
        scratch_shapes=[pltpu.VMEM((n, LANES), F32)],
        compiler_params=pltpu.CompilerParams(vmem_limit_bytes=VMEM_LIMIT),
        name="moe_rank",
    )(route)


def _dispatch_kernel(dest_ref, ends_ref, h_ref, xs_ref, zero_scr, sem, zsem):
    i = pl.program_id(0)
    tm = h_ref.shape[0]
    tb = zero_scr.shape[0]

    @pl.when(i == 0)
    def _():
        zero_scr[...] = jnp.zeros(zero_scr.shape, F32)

        def zero_copy(e):
            r0 = pl.multiple_of((ends_ref[e] - 1) * tb, tb)
            return pltpu.make_async_copy(zero_scr, xs_ref.at[pl.ds(r0, tb)], zsem)

        def start(e, carry):
            @pl.when(ends_ref[N_EXPERTS + e] > 0)
            def _():
                zero_copy(e).start()
            return carry

        def wait(e, carry):
            @pl.when(ends_ref[N_EXPERTS + e] > 0)
            def _():
                zero_copy(e).wait()
            return carry

        def tail_copy(t):
            return pltpu.make_async_copy(zero_scr, xs_ref.at[pl.ds(pl.multiple_of(t * tb, tb), tb)], zsem)

        def tail_start(t, carry):
            tail_copy(t).start()
            return carry

        def tail_wait(t, carry):
            tail_copy(t).wait()
            return carry

        used = ends_ref[2 * N_EXPERTS]
        lax.fori_loop(0, N_EXPERTS, start, 0)
        lax.fori_loop(used, xs_ref.shape[0] // tb, tail_start, 0)
        lax.fori_loop(0, N_EXPERTS, wait, 0)
        lax.fori_loop(used, xs_ref.shape[0] // tb, tail_wait, 0)

    def issue(r, carry):
        p = (i * tm + r) * TOP_K
        for k in range(TOP_K):
            pltpu.make_async_copy(h_ref.at[pl.ds(r, 1)], xs_ref.at[pl.ds(dest_ref[p + k], 1)], sem).start()
        return carry

    lax.fori_loop(0, tm, issue, 0)
    for k in range(TOP_K):
        pltpu.make_async_copy(h_ref, xs_ref.at[pl.ds(0, tm)], sem).wait()


def _dispatch(dest_flat, ends_flat, h, cap):
    n, d = h.shape
    tm = DISPATCH_TM
    return pl.pallas_call(
        _dispatch_kernel,
        grid_spec=pltpu.PrefetchScalarGridSpec(
            num_scalar_prefetch=2, grid=(n // tm,),
            in_specs=[pl.BlockSpec((tm, d), lambda i, dst, ends: (i, 0))],
            out_specs=pl.BlockSpec(memory_space=pl.ANY),
            scratch_shapes=[pltpu.VMEM((MOE_TB, d), F32), pltpu.SemaphoreType.DMA(()), pltpu.SemaphoreType.DMA(())]),
        out_shape=jax.ShapeDtypeStruct((cap, d), F32),
        compiler_params=_cparams(("arbitrary",)),
        name="moe_dispatch",
    )(dest_flat, ends_flat, h)


def _ffn_kernel(meta_ref, x_ref, wg_ref, wu_ref, wd_ref, y_ref, wgb, wub, wdb):
    t = pl.program_id(0)
    ntile = pl.num_programs(0)
    e = meta_ref[t]
    e_prev = meta_ref[jnp.maximum(t - 1, 0)]
    active = t < meta_ref[ntile]

    @pl.when(active & ((t == 0) | (e != e_prev)))
    def _():
        wgb[...] = wg_ref[0].astype(BF16)
        wub[...] = wu_ref[0].astype(BF16)
        wdb[...] = wd_ref[0].astype(BF16)

    @pl.when(active)
    def _():
        x = x_ref[...].astype(BF16)
        gate = jnp.dot(x, wgb[...], preferred_element_type=F32)
        up = jnp.dot(x, wub[...], preferred_element_type=F32)
        act = (gate * jax.nn.sigmoid(gate)) * up
        y_ref[...] = jnp.dot(act.astype(BF16), wdb[...], preferred_element_type=F32)

    @pl.when(jnp.logical_not(active))
    def _():
        y_ref[...] = jnp.zeros(y_ref.shape, F32)


def _ffn(meta_flat, xs, wg, wu, wd):
    cap, d = xs.shape
    ff = wg.shape[2]
    tb = MOE_TB
    ntile = cap // tb
    return pl.pallas_call(
        _ffn_kernel,
        grid_spec=pltpu.PrefetchScalarGridSpec(
            num_scalar_prefetch=1, grid=(ntile,),
            in_specs=[pl.BlockSpec((tb, d), lambda t, m: (jnp.minimum(t, m[ntile] - 1), 0)),
                      pl.BlockSpec((1, d, ff), lambda t, m: (m[t], 0, 0)),
                      pl.BlockSpec((1, d, ff), lambda t, m: (m[t], 0, 0)),
                      pl.BlockSpec((1, ff, d), lambda t, m: (m[t], 0, 0))],
            out_specs=pl.BlockSpec((tb, d), lambda t, m: (t, 0)),
            scratch_shapes=[pltpu.VMEM((d, ff), BF16), pltpu.VMEM((d, ff), BF16), pltpu.VMEM((ff, d), BF16)]),
        out_shape=jax.ShapeDtypeStruct((cap, d), F32),
        compiler_params=_cparams(("arbitrary",)),
        name="moe_ffn",
    )(meta_flat, xs, wg, wu, wd)


def _combine_kernel(dest_ref, y_ref, x1_ref, route_ref, g2_ref, nf_ref, o_ref, ybuf, sem):
    i = pl.program_id(0)
    nstep = pl.num_programs(0)
    tm = x1_ref.shape[0]

    def issue(tile, slot):
        def body(r, carry):
            p = (tile * tm + r) * TOP_K
            for k in range(TOP_K):
                pltpu.make_async_copy(y_ref.at[pl.ds(dest_ref[p + k], 1)], ybuf.at[slot, k, pl.ds(r, 1)],
                                      sem.at[slot]).start()
            return carry

        lax.fori_loop(0, tm, body, 0)

    @pl.when(i == 0)
    def _():
        issue(0, 0)

    @pl.when(i + 1 < nstep)
    def _():
        issue(i + 1, (i + 1) % 2)

    slot = i % 2
    for k in range(TOP_K):
        pltpu.make_async_copy(y_ref.at[pl.ds(0, tm)], ybuf.at[slot, k], sem.at[slot]).wait()
    rt = route_ref[...]
    moe = rt[:, 2:3] * ybuf[slot, 0] + rt[:, 3:4] * ybuf[slot, 1]
    xo = x1_ref[...] + g2_ref[0, 0] * moe
    o_ref[...] = xo * lax.rsqrt(jnp.mean(xo * xo, axis=-1, keepdims=True) + EPS) * nf_ref[...]


def _combine(dest_flat, y, x1, route, mod4, nf, seq):
    n, d = x1.shape
    tm = COMB_TM
    tiles_per_seq = seq // tm
    return pl.pallas_call(
        _combine_kernel,
        grid_spec=pltpu.PrefetchScalarGridSpec(
            num_scalar_prefetch=1, grid=(n // tm,),
            in_specs=[pl.BlockSpec(memory_space=pl.ANY),
                      pl.BlockSpec((tm, d), lambda i, dst: (i, 0)),
                      pl.BlockSpec((tm, LANES), lambda i, dst: (i, 0)),
                      pl.BlockSpec((1, 1, 1, d), lambda i, dst: (i // tiles_per_seq, 5, 0, 0)),
                      pl.BlockSpec((1, d), lambda i, dst: (0, 0))],
            out_specs=pl.BlockSpec((tm, d), lambda i, dst: (i, 0)),
            scratch_shapes=[pltpu.VMEM((2, TOP_K, tm, d), F32), pltpu.SemaphoreType.DMA((2,))]),
        out_shape=jax.ShapeDtypeStruct((n, d), F32),
        compiler_params=_cparams(("arbitrary",)),
        name="moe_combine",
    )(dest_flat, y, x1, route, mod4, nf)


def _alibi_slopes():
    n = NSA_HEADS
    full = jnp.asarray(2.0 ** (-8.0 * np.arange(1, n + 1) / n), dtype=F32)
    pieces, rest = [], full
    for _ in range(3):
        piece = rest.astype(BF16).astype(F32)
        pieces.append(piece)
        rest = rest - piece
    return jnp.concatenate([full] + pieces)


def _layer(x, c, w_ada, b_ada, norm1_w, w_in, cmp_pos_k, cmp_w1_k, cmp_w2_k, cmp_pos_v, cmp_w1_v, cmp_w2_v,
           gla_w_gate2, gla_b_gate, gla_norm_w, w_out, norm2_w, w_rg, b_rg, w_re, b_re, w_eg, w_eu, w_ed):
    bsz, s, d = x.shape
    dh = NSA_HEAD_DIM
    mod4 = _adaln(c, w_ada, b_ada).reshape(bsz, 6, 1, d)

    o_gate = NSA_COLS
    o_gla = o_gate + NSA_GATE_COLS
    o_lr = o_gla + 2 * GLA_HEADS * GLA_DK + 2 * GLA_HEADS * GLA_DV
    w_nsa = w_in[:, :NSA_COLS].astype(BF16)
    w_gla = jnp.concatenate(
        [w_in[:, o_gla:o_lr], w_in[:, o_gate:o_gla], w_in[:, o_lr:],
         jnp.zeros((d, LANES - NSA_GATE_COLS - GLA_GATE_RANK), F32)], axis=1).astype(BF16)
    nw1 = norm1_w.reshape(1, d)
    proj_nsa = _inproj(x, mod4, nw1, w_nsa, sc_idx=1, sh_idx=0, tn=INPROJ_TN_NSA, head_major=True)
    proj_gla = _inproj(x, mod4, nw1, w_gla, sc_idx=1, sh_idx=0, tn=INPROJ_TN_GLA, head_major=False)

    nkv = 2 * NSA_KV_HEADS
    chunk_w = CMP_STRIDE * dh
    kv_chunks = proj_nsa[:, NSA_HEADS:NSA_HEADS + nkv].reshape(bsz, nkv, s // CMP_STRIDE, chunk_w)
    pos = jnp.stack([cmp_pos_k, cmp_pos_v]).reshape(2, 2, chunk_w)
    kvc = _compress(kv_chunks, pos, jnp.stack([cmp_w1_k, cmp_w1_v]), jnp.stack([cmp_w2_k, cmp_w2_v]))
    o_nsa = _nsa(_alibi_slopes(), proj_nsa, kvc, proj_gla)

    wg_pad = jnp.zeros((LANES, GLA_HEADS * GLA_DK), F32).at[
        NSA_GATE_COLS:NSA_GATE_COLS + GLA_GATE_RANK].set(gla_w_gate2)
    o_gla_out = _gla(proj_gla, wg_pad, gla_b_gate.reshape(1, -1), gla_norm_w.reshape(1, -1))

    wr = jnp.concatenate([w_re, w_rg, jnp.zeros((d, LANES - N_EXPERTS - N_GROUPS), F32)], axis=1)
    br = jnp.concatenate([b_re, b_rg, jnp.zeros((LANES - N_EXPERTS - N_GROUPS,), F32)]).reshape(1, LANES)
    x1, h2, route = _outproj(o_nsa, o_gla_out, x, w_out.astype(BF16), mod4, norm2_w.reshape(1, d), wr, br)

    n = bsz * s
    npair = n * TOP_K
    cap = npair + N_EXPERTS * MOE_TB
    ntile = cap // MOE_TB
    route2 = route.reshape(n, LANES)
    dest, meta = _rank(route2)
    dest_flat = dest[:, :TOP_K].reshape(npair)
    meta_flat = jnp.concatenate([meta[:ntile, 0], meta[:1, 1]])
    ends_flat = jnp.concatenate([meta[:N_EXPERTS, 2], meta[:N_EXPERTS, 3], meta[:1, 1]])
    xs = _dispatch(dest_flat, ends_flat, h2.reshape(n, d), cap)
    y = _ffn(meta_flat, xs, w_eg, w_eu, w_ed)
    return x1.reshape(n, d), y, dest_flat, route2, mod4


def kernel(x, c, w_ada, b_ada, norm1_w, w_in, cmp_pos_k, cmp_w1_k, cmp_w2_k, cmp_pos_v, cmp_w1_v, cmp_w2_v,
           gla_w_gate2, gla_b_gate, gla_norm_w, w_out, norm2_w, w_router_group, b_router_group, w_router_expert,
           b_router_expert, w_expert_gate, w_expert_up, w_expert_down, norm_f_w):
    bsz, s, d = x.shape
    assert w_ada.shape[0] == 1, "single layer"
    x1, y, dest_flat, route2, mod4 = _layer(
        x, c, w_ada[0], b_ada[0], norm1_w[0], w_in[0], cmp_pos_k[0], cmp_w1_k[0], cmp_w2_k[0], cmp_pos_v[0],
        cmp_w1_v[0], cmp_w2_v[0], gla_w_gate2[0], gla_b_gate[0], gla_norm_w[0], w_out[0], norm2_w[0],
        w_router_group[0], b_router_group[0], w_router_expert[0], b_router_expert[0],
        w_expert_gate[0], w_expert_up[0], w_expert_down[0])
    out = _combine(dest_flat, y, x1, route2, mod4, norm_f_w.reshape(1, d), s)
    return out.reshape(bsz, s, d)
```

```python
import functools

import numpy as np
import jax
import jax.numpy as jnp
from jax import lax
from jax.experimental import pallas as pl
from jax.experimental.pallas import tpu as pltpu

F32 = jnp.float32
BF16 = jnp.bfloat16
I32 = jnp.int32
HI = lax.Precision.HIGHEST

D_MODEL = 2048
NSA_HEAD_DIM = 64
NSA_HEADS = 16
NSA_KV_HEADS = 4
NSA_Q_PER_KV = 4
CMP_BLOCK = 32
CMP_STRIDE = 16
SEL_BLOCK = 64
N_SEL = 16
WINDOW = 512
N_BRANCH = 3
GLA_HEADS = 4
GLA_DV = 256
GLA_DK = 128
GLA_GATE_RANK = 16
GLA_GATE_NORM = 16.0
GLA_CHUNK = 64
GLA_SUB = 16
N_GROUPS = 4
EXPERTS_PER_GROUP = 8
N_EXPERTS = 32
TOP_K = 2
EXPERT_FF = 512
EPS = 1e-6
NEG = -1e30
FORCE = 1e30
MASKED = 2.0 ** 100

NSA_Q_COLS = NSA_HEADS * NSA_HEAD_DIM
NSA_KV_COLS = 2 * N_BRANCH * NSA_KV_HEADS * NSA_HEAD_DIM
NSA_GATE_COLS = N_BRANCH * NSA_HEADS
NSA_COLS = NSA_Q_COLS + NSA_KV_COLS
NSA_SLOTS = NSA_COLS // NSA_HEAD_DIM
GLA_Q_OFF = 0
GLA_K_OFF = GLA_HEADS * GLA_DK
GLA_V_OFF = 2 * GLA_HEADS * GLA_DK
GLA_OG_OFF = GLA_V_OFF + GLA_HEADS * GLA_DV
GLA_MISC_OFF = GLA_OG_OFF + GLA_HEADS * GLA_DV
LANES = 128
GLA_COLS = GLA_MISC_OFF + LANES

VMEM_LIMIT = 56 * 1024 * 1024

ADA_TN = 768
INPROJ_TM = 512
INPROJ_TN_NSA = 512
INPROJ_TN_GLA = 640
NSA_TQ = 256
NSA_TK = 256
OUT_TM = 256
RANK_TM = 256
MOE_TB = 256
DISPATCH_TM = 256
COMB_TM = 256


def _cparams(sem):
    return pltpu.CompilerParams(dimension_semantics=sem, vmem_limit_bytes=VMEM_LIMIT)


def _adaln_kernel(ct_ref, w_ref, b_ref, o_ref, s_scr):
    nb = ct_ref.shape[1]
    kdim, tn = w_ref.shape

    @pl.when(pl.program_id(0) == 0)
    def _():
        ct = ct_ref[...]
        s = ct * jax.nn.sigmoid(ct)
        for b in range(nb):
            s_scr[b] = jnp.broadcast_to(s[:, b:b + 1], (kdim, LANES))

    def body(k, accs):
        r = pl.multiple_of(k * 8, 8)
        w8 = w_ref[pl.ds(r, 8), :]
        out = []
        for b, acc in enumerate(accs):
            s8 = s_scr[b, pl.ds(r, 8), :]
            out.append(acc + w8 * jnp.concatenate([s8] * (tn // LANES), axis=1))
        return tuple(out)

    accs = lax.fori_loop(0, kdim // 8, body, tuple(jnp.zeros((8, tn), F32) for _ in range(nb)), unroll=2)
    bias = b_ref[...]
    for b, acc in enumerate(accs):
        o_ref[b:b + 1, :] = jnp.sum(acc, axis=0, keepdims=True) + bias


def _adaln(c, w, b):
    nb, d = c.shape
    n = w.shape[1]
    return pl.pallas_call(
        _adaln_kernel,
        grid=(n // ADA_TN,),
        in_specs=[pl.BlockSpec((d, nb), lambda j: (0, 0)),
                  pl.BlockSpec((d, ADA_TN), lambda j: (0, j)),
                  pl.BlockSpec((1, ADA_TN), lambda j: (0, j))],
        out_specs=pl.BlockSpec((nb, ADA_TN), lambda j: (0, j)),
        out_shape=jax.ShapeDtypeStruct((nb, n), F32),
        scratch_shapes=[pltpu.VMEM((nb, d, LANES), F32)],
        compiler_params=_cparams(("arbitrary",)),
        name="adaln",
    )(c.T, w, b.reshape(1, n))


def _modulated_norm(x, nw, sc, sh):
    ms = jnp.mean(x * x, axis=-1, keepdims=True)
    h = x * lax.rsqrt(ms + EPS) * nw
    return h * (1.0 + sc) + sh


def _inproj_kernel(x_ref, sc_ref, sh_ref, nw_ref, w_ref, o_ref, h_scr, *, head_major):
    @pl.when(pl.program_id(2) == 0)
    def _():
        h = _modulated_norm(x_ref[0], nw_ref[...], sc_ref[0, 0], sh_ref[0, 0])
        h_scr[...] = h.astype(BF16)

    acc = jnp.dot(h_scr[...], w_ref[...], preferred_element_type=F32)
    if head_major:
        for u in range(acc.shape[1] // NSA_HEAD_DIM):
            o_ref[0, u] = acc[:, u * NSA_HEAD_DIM:(u + 1) * NSA_HEAD_DIM]
    else:
        o_ref[0] = acc


def _inproj(x, mod4, nw, w, *, sc_idx, sh_idx, tn, head_major):
    bsz, s, d = x.shape
    n = w.shape[1]
    tm = INPROJ_TM
    if head_major:
        upb = tn // NSA_HEAD_DIM
        out_shape = jax.ShapeDtypeStruct((bsz, n // NSA_HEAD_DIM, s, NSA_HEAD_DIM), F32)
        out_spec = pl.BlockSpec((1, upb, tm, NSA_HEAD_DIM), lambda b, i, j: (b, j, i, 0))
    else:
        out_shape = jax.ShapeDtypeStruct((bsz, s, n), F32)
        out_spec = pl.BlockSpec((1, tm, tn), lambda b, i, j: (b, i, j))
    return pl.pallas_call(
        functools.partial(_inproj_kernel, head_major=head_major),
        grid=(bsz, s // tm, n // tn),
        in_specs=[pl.BlockSpec((1, tm, d), lambda b, i, j: (b, i, 0)),
                  pl.BlockSpec((1, 1, 1, d), lambda b, i, j: (b, sc_idx, 0, 0)),
                  pl.BlockSpec((1, 1, 1, d), lambda b, i, j: (b, sh_idx, 0, 0)),
                  pl.BlockSpec((1, d), lambda b, i, j: (0, 0)),
                  pl.BlockSpec((d, tn), lambda b, i, j: (0, j))],
        out_specs=out_spec,
        out_shape=out_shape,
        scratch_shapes=[pltpu.VMEM((tm, d), BF16)],
        compiler_params=_cparams(("parallel", "parallel", "arbitrary")),
        name="inproj_nsa" if head_major else "inproj_gla",
    )(x, mod4, mod4, nw, w)


def _compress_kernel(a_ref, pos_ref, w1_ref, w2_ref, o_ref):
    a = a_ref[0, 0]
    pos = pos_ref[0]
    half = a.shape[1]
    y1 = jnp.dot(a + pos[0:1], w1_ref[0, :half, :], precision=HI, preferred_element_type=F32)
    y2 = jnp.dot(a + pos[1:2], w1_ref[0, half:, :], precision=HI, preferred_element_type=F32)
    nrow = a.shape[0]
    h = y1 + pltpu.roll(y2, nrow - 1, axis=0)
    out = jnp.dot(jax.nn.gelu(h), w2_ref[0], precision=HI, preferred_element_type=F32)
    row = lax.broadcasted_iota(I32, out.shape, 0)
    o_ref[0, 0] = jnp.where(row < nrow - 1, out, 0.0)


def _compress(kv_chunks, pos, w1, w2):
    bsz, nslot, nch, cw = kv_chunks.shape
    dh = NSA_HEAD_DIM
    return pl.pallas_call(
        _compress_kernel,
        grid=(bsz, nslot),
        in_specs=[pl.BlockSpec((1, 1, nch, cw), lambda b, t: (b, t, 0, 0)),
                  pl.BlockSpec((1, 2, cw), lambda b, t: (t // NSA_KV_HEADS, 0, 0)),
                  pl.BlockSpec((1, 2 * cw, dh), lambda b, t: (t // NSA_KV_HEADS, 0, 0)),
                  pl.BlockSpec((1, dh, dh), lambda b, t: (t // NSA_KV_HEADS, 0, 0))],
        out_specs=pl.BlockSpec((1, 1, nch, dh), lambda b, t: (b, t, 0, 0)),
        out_shape=jax.ShapeDtypeStruct((bsz, nslot, nch, dh), F32),
        compiler_params=_cparams(("parallel", "parallel")),
        name="nsa_compress",
    )(kv_chunks, pos, w1, w2)


def _nt_dot(a, b, **kw):
    return lax.dot_general(a, b, (((1,), (1,)), ((), ())), preferred_element_type=F32, **kw)


def _nsa_kernel(slopes_ref, q_ref, kc_ref, vc_ref, ks_ref, vs_ref, kw_ref, vw_ref, gate_ref, o_ref,
                ksb, vst, kwb, vwt, q4_scr, *scr):
    g = pl.program_id(1)
    qi = pl.program_id(2)
    tq_n = q_ref.shape[2]
    dh = NSA_HEAD_DIM
    nr = NSA_Q_PER_KV
    seq = ks_ref.shape[2]
    tk_n = NSA_TK
    nb = seq // SEL_BLOCK

    @pl.when(qi == 0)
    def _():
        row = lax.broadcasted_iota(I32, (seq, dh), 0)
        lane = lax.broadcasted_iota(I32, (seq, dh), 1)
        blk = row // SEL_BLOCK
        pos = jnp.where((lane >= nb) & (lane < nb + 3), (blk * SEL_BLOCK).astype(F32),
                        jnp.where((lane >= nb + 3) & (lane < nb + 6), (row % SEL_BLOCK).astype(F32), 0.0))
        ksb[...] = jnp.concatenate([jnp.where(lane == blk, -MASKED, pos), ks_ref[0, 0]], axis=1).astype(BF16)
        kwb[...] = jnp.concatenate([pos, kw_ref[0, 0]], axis=1).astype(BF16)
        for c in range(seq // tk_n):
            rows = slice(c * tk_n, (c + 1) * tk_n)
            for src, dst in ((vs_ref, vst), (vw_ref, vwt)):
                v = src[0, 0, rows, :]
                dst[c] = jnp.concatenate([v, v], axis=1).T[:dh].astype(BF16)

    t0 = qi * tq_n
    tq = t0 + lax.broadcasted_iota(I32, (1, tq_n), 1)
    tq_f = tq.astype(F32)
    slopes = [slopes_ref[g * nr + r] for r in range(nr)]
    scale = dh ** -0.5
    q_t = []
    for pair in range(nr // 2):
        both = jnp.concatenate([q_ref[0, 2 * pair], q_ref[0, 2 * pair + 1]], axis=1).T * scale
        q_t += [both[:dh], both[dh:]]

    ncp = kc_ref.shape[2]
    kc = kc_ref[0, 0]
    vc = vc_ref[0, 0]
    vc_t = jnp.concatenate([vc, vc], axis=1).T[:dh].astype(BF16)
    n_sub = lax.broadcasted_iota(I32, (ncp, 1), 0)
    blk_end = n_sub * CMP_STRIDE + (CMP_BLOCK - 1)
    center = n_sub.astype(F32) * CMP_STRIDE + (CMP_BLOCK - 1) / 2.0
    valid_c = blk_end <= tq
    dist_c = tq_f - center
    psum = jnp.zeros((ncp, tq_n), F32)
    o_c = []
    for r in range(nr):
        s = jnp.dot(kc, q_t[r], precision=HI, preferred_element_type=F32)
        s = jnp.where(valid_c, s - slopes[r] * dist_c, NEG)
        e = jnp.exp(s - jnp.max(s, axis=0, keepdims=True))
        p = jnp.where(valid_c, e / jnp.sum(e, axis=0, keepdims=True), 0.0)
        psum = psum + p
        o_c.append(jnp.dot(vc_t, p.astype(BF16), preferred_element_type=F32))

    rowj = lax.broadcasted_iota(I32, (LANES, ncp), 0) * SEL_BLOCK
    coln = lax.broadcasted_iota(I32, (LANES, ncp), 1) * CMP_STRIDE
    overlap = jnp.where((coln < rowj + SEL_BLOCK) & (coln + CMP_BLOCK > rowj)
                        & (coln < (ncp - 1) * CMP_STRIDE) & (rowj < nb * SEL_BLOCK), 1.0, 0.0)
    imp = jnp.dot(overlap, psum, precision=HI, preferred_element_type=F32)[:nb]
    j_sub = lax.broadcasted_iota(I32, (nb, 1), 0)
    qblk = tq // SEL_BLOCK
    forced = (j_sub == 0) | (j_sub == qblk) | (j_sub == qblk - 1)
    imp = jnp.where(forced, FORCE, jnp.where(j_sub <= qblk, imp, NEG))
    cnt = jnp.zeros((nb, tq_n), F32)
    for i in range(nb):
        ci = imp[i:i + 1, :]
        tie = jnp.where(j_sub > i, 1.0, 0.0)
        cnt = cnt + jnp.where(ci > imp, 1.0, jnp.where(ci == imp, tie, 0.0))
    notsel = jnp.where(cnt < float(min(N_SEL, nb)), 0.0, 1.0)

    sub_h = lax.broadcasted_iota(I32, (dh - nb, 1), 0)
    for r in range(nr):
        scol = jnp.zeros((dh - nb, 1), F32)
        for i in range(3):
            piece = slopes_ref[(i + 1) * NSA_HEADS + g * nr + r]
            scol = jnp.where((sub_h == i) | (sub_h == 3 + i), piece, scol)
        q4_scr[:, r * tq_n:(r + 1) * tq_n] = jnp.concatenate(
            [notsel, jnp.broadcast_to(scol, (dh - nb, tq_n)), q_t[r]], axis=0).astype(BF16)

    nwt = WINDOW // tk_n
    nbuf = nwt + 2
    stats_s, stats_w = scr[0:3], scr[3:6]
    s_buf, p_buf, a_buf = scr[6:6 + nbuf], scr[6 + nbuf:6 + 2 * nbuf], scr[6 + 2 * nbuf:6 + 3 * nbuf]
    for m_ref, l_ref, acc_ref in (stats_s, stats_w):
        m_ref[...] = jnp.full(m_ref.shape, NEG, F32)
        l_ref[...] = jnp.zeros(l_ref.shape, F32)
        acc_ref[...] = jnp.zeros(acc_ref.shape, F32)
    key_i = lax.broadcasted_iota(I32, (tk_n, LANES), 0)
    qry_j = lax.broadcasted_iota(I32, (tk_n, LANES), 1)

    def scores(k_ref, kt, buf):
        k_tile = k_ref[pl.ds(pl.multiple_of(kt * tk_n, tk_n), tk_n), :]
        s_buf[buf][...] = jnp.dot(k_tile, q4_scr[...], preferred_element_type=F32)

    def softmax(buf, keep_fn, stats):
        m_ref, l_ref, _ = stats
        for cb in range(nr * tq_n // LANES):
            cols = slice(cb * LANES, (cb + 1) * LANES)
            s = s_buf[buf][:, cols]
            if keep_fn is not None:
                s = jnp.where(keep_fn(qry_j + (cb * LANES) % tq_n), s, -MASKED)
            m_prev = m_ref[:, cols]
            m_new = jnp.maximum(m_prev, jnp.max(s, axis=0, keepdims=True))
            alpha = jnp.exp(m_prev - m_new)
            p = jnp.exp(s - m_new)
            l_ref[:, cols] = alpha * l_ref[:, cols] + jnp.sum(p, axis=0, keepdims=True)
            m_ref[:, cols] = m_new
            a_buf[buf][:, cols] = alpha
            p_buf[buf][:, cols] = p.astype(BF16)

    def values(vt_ref, kt, buf, stats):
        acc_ref = stats[2]
        pv = jnp.dot(vt_ref[kt], p_buf[buf][...], preferred_element_type=F32)
        acc_ref[...] = acc_ref[...] * a_buf[buf][...] + pv

    def batch(jobs):
        for buf, (k_ref, _, kt, _, _) in enumerate(jobs):
            scores(k_ref, kt, buf)
        for buf, (_, vt_ref, kt, keep_fn, stats) in enumerate(jobs):
            softmax(buf, keep_fn, stats)
            values(vt_ref, kt, buf, stats)

    def sel_pair(i, carry):
        batch([(ksb, vst, 2 * i, None, stats_s), (ksb, vst, 2 * i + 1, None, stats_s)])
        return carry

    lax.fori_loop(0, qi // 2, sel_pair, 0)

    @pl.when(qi % 2 == 1)
    def _():
        batch([(ksb, vst, qi - 1, None, stats_s)])

    def causal(j):
        return key_i <= j

    jobs = []
    for back in range(nwt, 0, -1):
        off = jnp.where(qi >= back, 0, tk_n)
        keep_fn = (lambda j, off=off: key_i - off > j) if back == nwt else (lambda j, off=off: key_i >= off)
        jobs.append((kwb, vwt, jnp.maximum(qi - back, 0), keep_fn, stats_w))
    jobs.append((kwb, vwt, qi, causal, stats_w))
    jobs.append((ksb, vst, qi, causal, stats_s))
    batch(jobs)

    gsel = jnp.where(lax.broadcasted_iota(I32, (LANES, LANES), 0)
                     == lax.broadcasted_iota(I32, (LANES, LANES), 1) + g * (nr * N_BRANCH), 1.0, 0.0)
    gates = jax.nn.sigmoid(jnp.dot(gate_ref[0], gsel, precision=HI, preferred_element_type=F32)).T
    (_, l_s, acc_s), (_, l_w, acc_w) = stats_s, stats_w
    for pair in range(nr // 2):
        o_t = []
        for r in (2 * pair, 2 * pair + 1):
            c0 = r * N_BRANCH
            cols = slice(r * tq_n, (r + 1) * tq_n)
            o_t.append(gates[c0:c0 + 1, :] * o_c[r] + (gates[c0 + 1:c0 + 2, :] / l_s[:, cols]) * acc_s[:, cols]
                       + (gates[c0 + 2:c0 + 3, :] / l_w[:, cols]) * acc_w[:, cols])
        o_ref[0, :, pair * LANES:(pair + 1) * LANES] = jnp.concatenate(o_t, axis=0).T


def _nsa(slopes, proj_nsa, kvc, proj_gla):
    bsz, _, s, dh = proj_nsa.shape
    g_n, nr = NSA_KV_HEADS, NSA_Q_PER_KV
    tq = NSA_TQ
    tk = NSA_TK
    assert tq == tk and WINDOW % tk == 0 and 2 * dh == LANES
    ncp = kvc.shape[2]
    kv0 = NSA_HEADS
    nq = nr * tq
    nbuf = WINDOW // tk + 2

    def kv_spec(i):
        return pl.BlockSpec((1, 1, s, dh), lambda b, g, q, i=i: (b, kv0 + i * g_n + g, 0, 0))

    return pl.pallas_call(
        _nsa_kernel,
        grid=(bsz, g_n, s // tq),
        in_specs=[pl.BlockSpec(memory_space=pltpu.SMEM),
                  pl.BlockSpec((1, nr, tq, dh), lambda b, g, q: (b, g, q, 0)),
                  pl.BlockSpec((1, 1, ncp, dh), lambda b, g, q: (b, g, 0, 0)),
                  pl.BlockSpec((1, 1, ncp, dh), lambda b, g, q: (b, g_n + g, 0, 0)),
                  kv_spec(2), kv_spec(3), kv_spec(4), kv_spec(5),
                  pl.BlockSpec((1, tq, LANES), lambda b, g, q: (b, q, GLA_MISC_OFF // LANES))],
        out_specs=pl.BlockSpec((1, tq, nr * dh), lambda b, g, q: (b, q, g)),
        out_shape=jax.ShapeDtypeStruct((bsz, s, NSA_HEADS * dh), F32),
        scratch_shapes=[pltpu.VMEM((s, LANES), BF16), pltpu.VMEM((s // tk, dh, tk), BF16)] * 2
        + [pltpu.VMEM((LANES, nq), BF16)]
        + [pltpu.VMEM((1, nq), F32), pltpu.VMEM((1, nq), F32), pltpu.VMEM((dh, nq), F32)] * 2
        + [pltpu.VMEM((tk, nq), F32)] * nbuf + [pltpu.VMEM((tk, nq), BF16)] * nbuf + [pltpu.VMEM((1, nq), F32)] * nbuf,
        compiler_params=_cparams(("parallel", "parallel", "arbitrary")),
        name="nsa_attention",
    )(slopes, proj_nsa, kvc, kvc, proj_nsa, proj_nsa, proj_nsa, proj_nsa, proj_gla)


def _gla_kernel(q_ref, k_ref, v_ref, og_ref, lr_ref, wg_ref, bg_ref, nw_ref, o_ref, st_scr, la_scr, b_scr):
    seq = q_ref.shape[1]
    c_n, sub = GLA_CHUNK, GLA_SUB
    dk = GLA_DK
    z = jnp.dot(lr_ref[0], wg_ref[...], precision=HI, preferred_element_type=F32) + bg_ref[...]
    la_scr[...] = (jnp.minimum(z, 0.0) - jnp.log1p(jnp.exp(-jnp.abs(z)))) * (1.0 / GLA_GATE_NORM)
    st_scr[...] = jnp.zeros(st_scr.shape, F32)
    tril = jnp.where(lax.broadcasted_iota(I32, (c_n, c_n), 0) >= lax.broadcasted_iota(I32, (c_n, c_n), 1), 1.0, 0.0)
    row_c = lax.broadcasted_iota(I32, (c_n, 1), 0)
    row_s = lax.broadcasted_iota(I32, (sub, 1), 0)
    lane_c = lax.broadcasted_iota(I32, (1, c_n), 1)
    nw = nw_ref[...]

    def chunk(c, carry):
        r0 = pl.multiple_of(c * c_n, c_n)
        qc = q_ref[0, pl.ds(r0, c_n), :] * (dk ** -0.5)
        kc = k_ref[0, pl.ds(r0, c_n), :]
        vc = v_ref[0, pl.ds(r0, c_n), :].astype(BF16)
        b = jnp.dot(tril, la_scr[pl.ds(r0, c_n), :], precision=HI, preferred_element_type=F32)
        b_scr[...] = b
        st = st_scr[...]
        o = _nt_dot((qc * jnp.exp(b)).astype(BF16), st.astype(BF16))
        strips = []
        for blk in range(c_n // sub):
            lo = blk * sub
            q_i = qc[lo:lo + sub]
            b_i = b[lo:lo + sub]
            a = jnp.zeros((sub, c_n), F32)
            for j in range(sub):
                b_j = b_scr[lo + j:lo + j + 1, :]
                k_j = k_ref[0, pl.ds(r0 + lo + j, 1), :]
                e = jnp.exp(jnp.where(row_s >= j, b_i - b_j, -jnp.inf))
                col = jnp.sum(q_i * k_j * e, axis=-1, keepdims=True)
                a = jnp.where(lane_c == lo + j, col, a)
            if blk > 0:
                b_r = b_scr[lo - 1:lo, :]
                q_d = q_i * jnp.exp(b_i - b_r)
                k_d = kc * jnp.exp(jnp.where(row_c < lo, b_r - b, -jnp.inf))
                a = a + _nt_dot(q_d.astype(BF16), k_d.astype(BF16))
            strips.append(a)
        attn = jnp.concatenate(strips, axis=0)
        o = o + jnp.dot(attn.astype(BF16), vc, preferred_element_type=F32)
        b_last = b_scr[c_n - 1:c_n, :]
        k_dec = (kc * jnp.exp(b_last - b)).astype(BF16)
        st_scr[...] = st * jnp.exp(b_last) + lax.dot_general(
            vc, k_dec, (((0,), (0,)), ((), ())), preferred_element_type=F32)
        o = o * lax.rsqrt(jnp.mean(o * o, axis=-1, keepdims=True) + EPS) * nw
        og = og_ref[0, pl.ds(r0, c_n), :]
        o_ref[0, pl.ds(r0, c_n), :] = o * (og * jax.nn.sigmoid(og))
        return carry

    lax.fori_loop(0, seq // c_n, chunk, 0)


def _gla(proj_gla, wg_pad, bg, nw):
    bsz, s, _ = proj_gla.shape
    h_n, dk, dv = GLA_HEADS, GLA_DK, GLA_DV
    return pl.pallas_call(
        _gla_kernel,
        grid=(bsz, h_n),
        in_specs=[pl.BlockSpec((1, s, dk), lambda b, h: (b, 0, GLA_Q_OFF // dk + h)),
                  pl.BlockSpec((1, s, dk), lambda b, h: (b, 0, GLA_K_OFF // dk + h)),
                  pl.BlockSpec((1, s, dv), lambda b, h: (b, 0, GLA_V_OFF // dv + h)),
                  pl.BlockSpec((1, s, dv), lambda b, h: (b, 0, GLA_OG_OFF // dv + h)),
                  pl.BlockSpec((1, s, LANES), lambda b, h: (b, 0, GLA_MISC_OFF // LANES)),
                  pl.BlockSpec((LANES, dk), lambda b, h: (0, h)),
                  pl.BlockSpec((1, dk), lambda b, h: (0, h)),
                  pl.BlockSpec((1, dv), lambda b, h: (0, 0))],
        out_specs=pl.BlockSpec((1, s, dv), lambda b, h: (b, 0, h)),
        out_shape=jax.ShapeDtypeStruct((bsz, s, h_n * dv), F32),
        scratch_shapes=[pltpu.VMEM((dv, dk), F32), pltpu.VMEM((s, dk), F32), pltpu.VMEM((GLA_CHUNK, dk), F32)],
        compiler_params=_cparams(("parallel", "parallel")),
        name="gla",
    )(proj_gla, proj_gla, proj_gla, proj_gla, proj_gla, wg_pad, bg, nw)


def _outproj_kernel(nsa_ref, gla_ref, x_ref, wo_ref, g1_ref, sc_ref, sh_ref, nw_ref, wr_ref, br_ref,
                    x1_ref, h_ref, route_ref):
    half = nsa_ref.shape[2]
    acc = jnp.dot(nsa_ref[0].astype(BF16), wo_ref[:half, :], preferred_element_type=F32)
    acc = acc + jnp.dot(gla_ref[0].astype(BF16), wo_ref[half:, :], preferred_element_type=F32)
    x1 = x_ref[0] + g1_ref[0, 0] * acc
    x1_ref[0] = x1
    h = _modulated_norm(x1, nw_ref[...], sc_ref[0, 0], sh_ref[0, 0])
    h_ref[0] = h
    logits = jnp.dot(h, wr_ref[...], precision=HI, preferred_element_type=F32) + br_ref[...]
    lane = lax.broadcasted_iota(I32, (1, LANES), 1)
    ninf = -jnp.inf
    is_g = (lane >= N_EXPERTS) & (lane < N_EXPERTS + N_GROUPS)
    gl = jnp.where(is_g, logits, ninf)
    ge = jnp.exp(gl - jnp.max(gl, axis=-1, keepdims=True))
    gp = ge / jnp.sum(ge, axis=-1, keepdims=True)
    gp_max = jnp.max(gp, axis=-1, keepdims=True)
    grp = jnp.min(jnp.where((gp == gp_max) & is_g, lane - N_EXPERTS, LANES), axis=-1, keepdims=True)
    in_grp = (lane // EXPERTS_PER_GROUP == grp) & (lane < N_EXPERTS)
    el = jnp.where(in_grp, logits, ninf)
    v1 = jnp.max(el, axis=-1, keepdims=True)
    i1 = jnp.min(jnp.where(el == v1, lane, LANES), axis=-1, keepdims=True)
    el2 = jnp.where(lane == i1, ninf, el)
    v2 = jnp.max(el2, axis=-1, keepdims=True)
    i2 = jnp.min(jnp.where(el2 == v2, lane, LANES), axis=-1, keepdims=True)
    e2 = jnp.exp(v2 - v1)
    den = 1.0 + e2
    w1 = gp_max * (1.0 / den)
    w2 = gp_max * (e2 / den)
    route_ref[0] = jnp.where(lane == 0, i1.astype(F32), jnp.where(lane == 1, i2.astype(F32), jnp.where(
        lane == 2, w1, jnp.where(lane == 3, w2, 0.0))))


def _outproj(o_nsa, o_gla, x, wo, mod4, nw, wr, br):
    bsz, s, d = x.shape
    tm = OUT_TM
    half = o_nsa.shape[2]

    def mod_spec(idx):
        return pl.BlockSpec((1, 1, 1, d), lambda b, i: (b, idx, 0, 0))

    row = lambda w: pl.BlockSpec((1, tm, w), lambda b, i: (b, i, 0))
    return pl.pallas_call(
        _outproj_kernel,
        grid=(bsz, s // tm),
        in_specs=[row(half), row(half), row(d),
                  pl.BlockSpec((2 * half, d), lambda b, i: (0, 0)),
                  mod_spec(2), mod_spec(4), mod_spec(3),
                  pl.BlockSpec((1, d), lambda b, i: (0, 0)),
                  pl.BlockSpec((d, LANES), lambda b, i: (0, 0)),
                  pl.BlockSpec((1, LANES), lambda b, i: (0, 0))],
        out_specs=[row(d), row(d), row(LANES)],
        out_shape=[jax.ShapeDtypeStruct((bsz, s, d), F32), jax.ShapeDtypeStruct((bsz, s, d), F32),
                   jax.ShapeDtypeStruct((bsz, s, LANES), F32)],
        compiler_params=_cparams(("parallel", "parallel")),
        name="outproj_router",
    )(o_nsa, o_gla, x, wo, mod4, mod4, mod4, nw, wr, br)


def _rank_kernel(route_ref, dest_ref, meta_ref, rank_scr):
    n = route_ref.shape[0]
    tm = RANK_TM
    lane_i = lax.broadcasted_iota(I32, (1, LANES), 1)
    lane = lane_i.astype(F32)
    strict = jnp.where(lax.broadcasted_iota(I32, (tm, tm), 0) > lax.broadcasted_iota(I32, (tm, tm), 1),
                       1.0, 0.0).astype(BF16)

    def two_lanes(a, b):
        return jnp.where(lane_i == 0, a, jnp.where(lane_i == 1, b, 0.0))

    def pick(e, table):
        return jnp.sum(jnp.where(lane == e, table, 0.0), axis=-1, keepdims=True)

    def count(i, seen):
        r0 = pl.multiple_of(i * tm, tm)
        rt = route_ref[pl.ds(r0, tm), :]
        e1, e2 = rt[:, 0:1], rt[:, 1:2]
        member = jnp.where(lane == e1, 1.0, jnp.where(lane == e2, 1.0, 0.0))
        before = jnp.dot(strict, member.astype(BF16), preferred_element_type=F32) + seen
        rank_scr[pl.ds(r0, tm), :] = two_lanes(pick(e1, before), pick(e2, before))
        return seen + jnp.sum(member, axis=0, keepdims=True)

    counts = lax.fori_loop(0, n // tm, count, jnp.zeros((1, LANES), F32))
    ntile = jnp.floor((counts + (MOE_TB - 1)) * (1.0 / MOE_TB))
    incl = jnp.where(lax.broadcasted_iota(I32, (LANES, LANES), 0) <= lax.broadcasted_iota(I32, (LANES, LANES), 1),
                     1.0, 0.0).astype(BF16)
    tile_end = jnp.dot(jnp.broadcast_to(ntile, (8, LANES)).astype(BF16), incl,
                       preferred_element_type=F32)[0:1]
    row_start = (tile_end - ntile) * MOE_TB

    def place(i, carry):
        r0 = pl.multiple_of(i * tm, tm)
        rt = route_ref[pl.ds(r0, tm), :]
        rk = rank_scr[pl.ds(r0, tm), :]
        d1 = pick(rt[:, 0:1], row_start) + rk[:, 0:1]
        d2 = pick(rt[:, 1:2], row_start) + rk[:, 1:2]
        dest_ref[pl.ds(r0, tm), :] = two_lanes(d1, d2).astype(I32)
        return carry

    lax.fori_loop(0, n // tm, place, 0)
    trow = lax.broadcasted_iota(I32, (meta_ref.shape[0], 1), 0).astype(F32)
    texp = jnp.sum(jnp.where((tile_end <= trow) & (lane_i < N_EXPERTS), 1.0, 0.0), axis=-1, keepdims=True)
    texp = jnp.minimum(texp, N_EXPERTS - 1.0)
    used = pick(N_EXPERTS - 1.0, tile_end)
    diag = lax.broadcasted_iota(I32, (meta_ref.shape[0], LANES), 0) == lane_i
    end_rows = jnp.sum(jnp.where(diag, tile_end, 0.0), axis=-1, keepdims=True)
    ntile_rows = jnp.sum(jnp.where(diag, ntile, 0.0), axis=-1, keepdims=True)
    meta_ref[...] = jnp.where(lane_i == 2, end_rows, jnp.where(lane_i == 3, ntile_rows, two_lanes(
        texp, jnp.broadcast_to(used, texp.shape)))).astype(I32)


def _rank(route):
    n = route.shape[0]
    return pl.pallas_call(
        _rank_kernel,
        out_shape=[jax.ShapeDtypeStruct((n, LANES), I32), jax.ShapeDtypeStruct((LANES, LANES), I32)],
        scratch_shapes=[pltpu.VMEM((n, LANES), F32)],
        compiler_params=pltpu.CompilerParams(vmem_limit_bytes=VMEM_LIMIT),
        name="moe_rank",
    )(route)


def _dispatch_kernel(dest_ref, ends_ref, h_ref, xs_ref, zero_scr, sem, zsem):
    i = pl.program_id(0)
    tm = h_ref.shape[0]
    tb = zero_scr.shape[0]

    @pl.when(i == 0)
    def _():
        zero_scr[...] = jnp.zeros(zero_scr.shape, F32)

        def zero_copy(e):
            r0 = pl.multiple_of((ends_ref[e] - 1) * tb, tb)
            return pltpu.make_async_copy(zero_scr, xs_ref.at[pl.ds(r0, tb)], zsem)

        def start(e, carry):
            @pl.when(ends_ref[N_EXPERTS + e] > 0)
            def _():
                zero_copy(e).start()
            return carry

        def wait(e, carry):
            @pl.when(ends_ref[N_EXPERTS + e] > 0)
            def _():
                zero_copy(e).wait()
            return carry

        def tail_copy(t):
            return pltpu.make_async_copy(zero_scr, xs_ref.at[pl.ds(pl.multiple_of(t * tb, tb), tb)], zsem)

        def tail_start(t, carry):
            tail_copy(t).start()
            return carry

        def tail_wait(t, carry):
            tail_copy(t).wait()
            return carry

        used = ends_ref[2 * N_EXPERTS]
        lax.fori_loop(0, N_EXPERTS, start, 0)
        lax.fori_loop(used, xs_ref.shape[0] // tb, tail_start, 0)
        lax.fori_loop(0, N_EXPERTS, wait, 0)
        lax.fori_loop(used, xs_ref.shape[0] // tb, tail_wait, 0)

    def issue(r, carry):
        p = (i * tm + r) * TOP_K
        for k in range(TOP_K):
            pltpu.make_async_copy(h_ref.at[pl.ds(r, 1)], xs_ref.at[pl.ds(dest_ref[p + k], 1)], sem).start()
        return carry

    lax.fori_loop(0, tm, issue, 0)
    for k in range(TOP_K):
        pltpu.make_async_copy(h_ref, xs_ref.at[pl.ds(0, tm)], sem).wait()


def _dispatch(dest_flat, ends_flat, h, cap):
    n, d = h.shape
    tm = DISPATCH_TM
    return pl.pallas_call(
        _dispatch_kernel,
        grid_spec=pltpu.PrefetchScalarGridSpec(
            num_scalar_prefetch=2, grid=(n // tm,),
            in_specs=[pl.BlockSpec((tm, d), lambda i, dst, ends: (i, 0))],
            out_specs=pl.BlockSpec(memory_space=pl.ANY),
            scratch_shapes=[pltpu.VMEM((MOE_TB, d), F32), pltpu.SemaphoreType.DMA(()), pltpu.SemaphoreType.DMA(())]),
        out_shape=jax.ShapeDtypeStruct((cap, d), F32),
        compiler_params=_cparams(("arbitrary",)),
        name="moe_dispatch",
    )(dest_flat, ends_flat, h)


def _ffn_kernel(meta_ref, x_ref, wg_ref, wu_ref, wd_ref, y_ref, wgb, wub, wdb):
    t = pl.program_id(0)
    ntile = pl.num_programs(0)
    e = meta_ref[t]
    e_prev = meta_ref[jnp.maximum(t - 1, 0)]
    active = t < meta_ref[ntile]

    @pl.when(active & ((t == 0) | (e != e_prev)))
    def _():
        wgb[...] = wg_ref[0].astype(BF16)
        wub[...] = wu_ref[0].astype(BF16)
        wdb[...] = wd_ref[0].astype(BF16)

    @pl.when(active)
    def _():
        x = x_ref[...].astype(BF16)
        gate = jnp.dot(x, wgb[...], preferred_element_type=F32)
        up = jnp.dot(x, wub[...], preferred_element_type=F32)
        act = (gate * jax.nn.sigmoid(gate)) * up
        y_ref[...] = jnp.dot(act.astype(BF16), wdb[...], preferred_element_type=F32)

    @pl.when(jnp.logical_not(active))
    def _():
        y_ref[...] = jnp.zeros(y_ref.shape, F32)


def _ffn(meta_flat, xs, wg, wu, wd):
    cap, d = xs.shape
    ff = wg.shape[2]
    tb = MOE_TB
    ntile = cap // tb
    return pl.pallas_call(
        _ffn_kernel,
        grid_spec=pltpu.PrefetchScalarGridSpec(
            num_scalar_prefetch=1, grid=(ntile,),
            in_specs=[pl.BlockSpec((tb, d), lambda t, m: (jnp.minimum(t, m[ntile] - 1), 0)),
                      pl.BlockSpec((1, d, ff), lambda t, m: (m[t], 0, 0)),
                      pl.BlockSpec((1, d, ff), lambda t, m: (m[t], 0, 0)),
                      pl.BlockSpec((1, ff, d), lambda t, m: (m[t], 0, 0))],
            out_specs=pl.BlockSpec((tb, d), lambda t, m: (t, 0)),
            scratch_shapes=[pltpu.VMEM((d, ff), BF16), pltpu.VMEM((d, ff), BF16), pltpu.VMEM((ff, d), BF16)]),
        out_shape=jax.ShapeDtypeStruct((cap, d), F32),
        compiler_params=_cparams(("arbitrary",)),
        name="moe_ffn",
    )(meta_flat, xs, wg, wu, wd)


def _combine_kernel(dest_ref, y_ref, x1_ref, route_ref, g2_ref, nf_ref, o_ref, ybuf, sem):
    i = pl.program_id(0)
    nstep = pl.num_programs(0)
    tm = x1_ref.shape[0]

    def issue(tile, slot):
        def body(r, carry):
            p = (tile * tm + r) * TOP_K
            for k in range(TOP_K):
                pltpu.make_async_copy(y_ref.at[pl.ds(dest_ref[p + k], 1)], ybuf.at[slot, k, pl.ds(r, 1)],
                                      sem.at[slot]).start()
            return carry

        lax.fori_loop(0, tm, body, 0)

    @pl.when(i == 0)
    def _():
        issue(0, 0)

    @pl.when(i + 1 < nstep)
    def _():
        issue(i + 1, (i + 1) % 2)

    slot = i % 2
    for k in range(TOP_K):
        pltpu.make_async_copy(y_ref.at[pl.ds(0, tm)], ybuf.at[slot, k], sem.at[slot]).wait()
    rt = route_ref[...]
    moe = rt[:, 2:3] * ybuf[slot, 0] + rt[:, 3:4] * ybuf[slot, 1]
    xo = x1_ref[...] + g2_ref[0, 0] * moe
    o_ref[...] = xo * lax.rsqrt(jnp.mean(xo * xo, axis=-1, keepdims=True) + EPS) * nf_ref[...]


def _combine(dest_flat, y, x1, route, mod4, nf, seq):
    n, d = x1.shape
    tm = COMB_TM
    tiles_per_seq = seq // tm
    return pl.pallas_call(
        _combine_kernel,
        grid_spec=pltpu.PrefetchScalarGridSpec(
            num_scalar_prefetch=1, grid=(n // tm,),
            in_specs=[pl.BlockSpec(memory_space=pl.ANY),
                      pl.BlockSpec((tm, d), lambda i, dst: (i, 0)),
                      pl.BlockSpec((tm, LANES), lambda i, dst: (i, 0)),
                      pl.BlockSpec((1, 1, 1, d), lambda i, dst: (i // tiles_per_seq, 5, 0, 0)),
                      pl.BlockSpec((1, d), lambda i, dst: (0, 0))],
            out_specs=pl.BlockSpec((tm, d), lambda i, dst: (i, 0)),
            scratch_shapes=[pltpu.VMEM((2, TOP_K, tm, d), F32), pltpu.SemaphoreType.DMA((2,))]),
        out_shape=jax.ShapeDtypeStruct((n, d), F32),
        compiler_params=_cparams(("arbitrary",)),
        name="moe_combine",
    )(dest_flat, y, x1, route, mod4, nf)


def _alibi_slopes():
    n = NSA_HEADS
    full = jnp.asarray(2.0 ** (-8.0 * np.arange(1, n + 1) / n), dtype=F32)
    pieces, rest = [], full
    for _ in range(3):
        piece = rest.astype(BF16).astype(F32)
        pieces.append(piece)
        rest = rest - piece
    return jnp.concatenate([full] + pieces)


def _layer(x, c, w_ada, b_ada, norm1_w, w_in, cmp_pos_k, cmp_w1_k, cmp_w2_k, cmp_pos_v, cmp_w1_v, cmp_w2_v,
           gla_w_gate2, gla_b_gate, gla_norm_w, w_out, norm2_w, w_rg, b_rg, w_re, b_re, w_eg, w_eu, w_ed):
    bsz, s, d = x.shape
    dh = NSA_HEAD_DIM
    mod4 = _adaln(c, w_ada, b_ada).reshape(bsz, 6, 1, d)

    o_gate = NSA_COLS
    o_gla = o_gate + NSA_GATE_COLS
    o_lr = o_gla + 2 * GLA_HEADS * GLA_DK + 2 * GLA_HEADS * GLA_DV
    w_nsa = w_in[:, :NSA_COLS].astype(BF16)
    w_gla = jnp.concatenate(
        [w_in[:, o_gla:o_lr], w_in[:, o_gate:o_gla], w_in[:, o_lr:],
         jnp.zeros((d, LANES - NSA_GATE_COLS - GLA_GATE_RANK), F32)], axis=1).astype(BF16)
    nw1 = norm1_w.reshape(1, d)
    proj_nsa = _inproj(x, mod4, nw1, w_nsa, sc_idx=1, sh_idx=0, tn=INPROJ_TN_NSA, head_major=True)
    proj_gla = _inproj(x, mod4, nw1, w_gla, sc_idx=1, sh_idx=0, tn=INPROJ_TN_GLA, head_major=False)

    nkv = 2 * NSA_KV_HEADS
    chunk_w = CMP_STRIDE * dh
    kv_chunks = proj_nsa[:, NSA_HEADS:NSA_HEADS + nkv].reshape(bsz, nkv, s // CMP_STRIDE, chunk_w)
    pos = jnp.stack([cmp_pos_k, cmp_pos_v]).reshape(2, 2, chunk_w)
    kvc = _compress(kv_chunks, pos, jnp.stack([cmp_w1_k, cmp_w1_v]), jnp.stack([cmp_w2_k, cmp_w2_v]))
    o_nsa = _nsa(_alibi_slopes(), proj_nsa, kvc, proj_gla)

    wg_pad = jnp.zeros((LANES, GLA_HEADS * GLA_DK), F32).at[
        NSA_GATE_COLS:NSA_GATE_COLS + GLA_GATE_RANK].set(gla_w_gate2)
    o_gla_out = _gla(proj_gla, wg_pad, gla_b_gate.reshape(1, -1), gla_norm_w.reshape(1, -1))

    wr = jnp.concatenate([w_re, w_rg, jnp.zeros((d, LANES - N_EXPERTS - N_GROUPS), F32)], axis=1)
    br = jnp.concatenate([b_re, b_rg, jnp.zeros((LANES - N_EXPERTS - N_GROUPS,), F32)]).reshape(1, LANES)
    x1, h2, route = _outproj(o_nsa, o_gla_out, x, w_out.astype(BF16), mod4, norm2_w.reshape(1, d), wr, br)

    n = bsz * s
    npair = n * TOP_K
    cap = npair + N_EXPERTS * MOE_TB
    ntile = cap // MOE_TB
    route2 = route.reshape(n, LANES)
    dest, meta = _rank(route2)
    dest_flat = dest[:, :TOP_K].reshape(npair)
    meta_flat = jnp.concatenate([meta[:ntile, 0], meta[:1, 1]])
    ends_flat = jnp.concatenate([meta[:N_EXPERTS, 2], meta[:N_EXPERTS, 3], meta[:1, 1]])
    xs = _dispatch(dest_flat, ends_flat, h2.reshape(n, d), cap)
    y = _ffn(meta_flat, xs, w_eg, w_eu, w_ed)
    return x1.reshape(n, d), y, dest_flat, route2, mod4


def kernel(x, c, w_ada, b_ada, norm1_w, w_in, cmp_pos_k, cmp_w1_k, cmp_w2_k, cmp_pos_v, cmp_w1_v, cmp_w2_v,
           gla_w_gate2, gla_b_gate, gla_norm_w, w_out, norm2_w, w_router_group, b_router_group, w_router_expert,
           b_router_expert, w_expert_gate, w_expert_up, w_expert_down, norm_f_w):
    bsz, s, d = x.shape
    assert w_ada.shape[0] == 1, "single layer"
    x1, y, dest_flat, route2, mod4 = _layer(
        x, c, w_ada[0], b_ada[0], norm1_w[0], w_in[0], cmp_pos_k[0], cmp_w1_k[0], cmp_w2_k[0], cmp_pos_v[0],
        cmp_w1_v[0], cmp_w2_v[0], gla_w_gate2[0], gla_b_gate[0], gla_norm_w[0], w_out[0], norm2_w[0],
        w_router_group[0], b_router_group[0], w_router_expert[0], b_router_expert[0],
        w_expert_gate[0], w_expert_up[0], w_expert_down[0])
    out = _combine(dest_flat, y, x1, route2, mod4, norm_f_w.reshape(1, d), s)
    return out.reshape(bsz, s, d)
```

```python
import functools

import numpy as np
import jax
import jax.numpy as jnp
from jax import lax
from jax.experimental import pallas as pl
from jax.experimental.pallas import tpu as pltpu

F32 = jnp.float32
BF16 = jnp.bfloat16
I32 = jnp.int32
HI = lax.Precision.HIGHEST

D_MODEL = 2048
NSA_HEAD_DIM = 64
NSA_HEADS = 16
NSA_KV_HEADS = 4
NSA_Q_PER_KV = 4
CMP_BLOCK = 32
CMP_STRIDE = 16
SEL_BLOCK = 64
N_SEL = 16
WINDOW = 512
N_BRANCH = 3
GLA_HEADS = 4
GLA_DV = 256
GLA_DK = 128
GLA_GATE_RANK = 16
GLA_GATE_NORM = 16.0
GLA_CHUNK = 64
GLA_SUB = 16
N_GROUPS = 4
EXPERTS_PER_GROUP = 8
N_EXPERTS = 32
TOP_K = 2
EXPERT_FF = 512
EPS = 1e-6
NEG = -1e30
FORCE = 1e30
MASKED = 2.0 ** 100

NSA_Q_COLS = NSA_HEADS * NSA_HEAD_DIM
NSA_KV_COLS = 2 * N_BRANCH * NSA_KV_HEADS * NSA_HEAD_DIM
NSA_GATE_COLS = N_BRANCH * NSA_HEADS
NSA_COLS = NSA_Q_COLS + NSA_KV_COLS
NSA_SLOTS = NSA_COLS // NSA_HEAD_DIM
GLA_Q_OFF = 0
GLA_K_OFF = GLA_HEADS * GLA_DK
GLA_V_OFF = 2 * GLA_HEADS * GLA_DK
GLA_OG_OFF = GLA_V_OFF + GLA_HEADS * GLA_DV
GLA_MISC_OFF = GLA_OG_OFF + GLA_HEADS * GLA_DV
LANES = 128
GLA_COLS = GLA_MISC_OFF + LANES

VMEM_LIMIT = 56 * 1024 * 1024

ADA_TN = 768
INPROJ_TM = 512
INPROJ_TN_NSA = 512
INPROJ_TN_GLA = 640
NSA_TQ = 256
NSA_TK = 256
GLA_TS = 512
OUT_TM = 256
RANK_TM = 256
MOE_TB = 256
DISPATCH_TM = 256
COMB_TM = 256


def _cparams(sem):
    return pltpu.CompilerParams(dimension_semantics=sem, vmem_limit_bytes=VMEM_LIMIT)


def _adaln_kernel(ct_ref, w_ref, b_ref, o_ref, s_scr):
    nb = ct_ref.shape[1]
    kdim, tn = w_ref.shape

    @pl.when(pl.program_id(0) == 0)
    def _():
        ct = ct_ref[...]
        s = ct * jax.nn.sigmoid(ct)
        for b in range(nb):
            s_scr[b] = jnp.broadcast_to(s[:, b:b + 1], (kdim, LANES))

    def body(k, accs):
        r = pl.multiple_of(k * 8, 8)
        w8 = w_ref[pl.ds(r, 8), :]
        out = []
        for b, acc in enumerate(accs):
            s8 = s_scr[b, pl.ds(r, 8), :]
            out.append(acc + w8 * jnp.concatenate([s8] * (tn // LANES), axis=1))
        return tuple(out)

    accs = lax.fori_loop(0, kdim // 8, body, tuple(jnp.zeros((8, tn), F32) for _ in range(nb)), unroll=2)
    bias = b_ref[...]
    for b, acc in enumerate(accs):
        o_ref[b:b + 1, :] = jnp.sum(acc, axis=0, keepdims=True) + bias


def _adaln(c, w, b):
    nb, d = c.shape
    n = w.shape[1]
    return pl.pallas_call(
        _adaln_kernel,
        grid=(n // ADA_TN,),
        in_specs=[pl.BlockSpec((d, nb), lambda j: (0, 0)),
                  pl.BlockSpec((d, ADA_TN), lambda j: (0, j)),
                  pl.BlockSpec((1, ADA_TN), lambda j: (0, j))],
        out_specs=pl.BlockSpec((nb, ADA_TN), lambda j: (0, j)),
        out_shape=jax.ShapeDtypeStruct((nb, n), F32),
        scratch_shapes=[pltpu.VMEM((nb, d, LANES), F32)],
        compiler_params=_cparams(("arbitrary",)),
        name="adaln",
    )(c.T, w, b.reshape(1, n))


def _modulated_norm(x, nw, sc, sh):
    ms = jnp.mean(x * x, axis=-1, keepdims=True)
    h = x * lax.rsqrt(ms + EPS) * nw
    return h * (1.0 + sc) + sh


def _inproj_kernel(x_ref, sc_ref, sh_ref, nw_ref, w_ref, o_ref, h_scr, *, head_major):
    @pl.when(pl.program_id(2) == 0)
    def _():
        h = _modulated_norm(x_ref[0], nw_ref[...], sc_ref[0, 0], sh_ref[0, 0])
        h_scr[...] = h.astype(BF16)

    acc = jnp.dot(h_scr[...], w_ref[...], preferred_element_type=F32)
    if head_major:
        for u in range(acc.shape[1] // NSA_HEAD_DIM):
            o_ref[0, u] = acc[:, u * NSA_HEAD_DIM:(u + 1) * NSA_HEAD_DIM]
    else:
        o_ref[0] = acc


def _inproj(x, mod4, nw, w, *, sc_idx, sh_idx, tn, head_major):
    bsz, s, d = x.shape
    n = w.shape[1]
    tm = INPROJ_TM
    if head_major:
        upb = tn // NSA_HEAD_DIM
        out_shape = jax.ShapeDtypeStruct((bsz, n // NSA_HEAD_DIM, s, NSA_HEAD_DIM), F32)
        out_spec = pl.BlockSpec((1, upb, tm, NSA_HEAD_DIM), lambda b, i, j: (b, j, i, 0))
    else:
        out_shape = jax.ShapeDtypeStruct((bsz, s, n), F32)
        out_spec = pl.BlockSpec((1, tm, tn), lambda b, i, j: (b, i, j))
    return pl.pallas_call(
        functools.partial(_inproj_kernel, head_major=head_major),
        grid=(bsz, s // tm, n // tn),
        in_specs=[pl.BlockSpec((1, tm, d), lambda b, i, j: (b, i, 0)),
                  pl.BlockSpec((1, 1, 1, d), lambda b, i, j: (b, sc_idx, 0, 0)),
                  pl.BlockSpec((1, 1, 1, d), lambda b, i, j: (b, sh_idx, 0, 0)),
                  pl.BlockSpec((1, d), lambda b, i, j: (0, 0)),
                  pl.BlockSpec((d, tn), lambda b, i, j: (0, j))],
        out_specs=out_spec,
        out_shape=out_shape,
        scratch_shapes=[pltpu.VMEM((tm, d), BF16)],
        compiler_params=_cparams(("parallel", "parallel", "arbitrary")),
        name="inproj_nsa" if head_major else "inproj_gla",
    )(x, mod4, mod4, nw, w)


def _compress_kernel(a_ref, pos_ref, w1_ref, w2_ref, o_ref):
    a = a_ref[0, 0]
    pos = pos_ref[0]
    half = a.shape[1]
    y1 = jnp.dot(a + pos[0:1], w1_ref[0, :half, :], precision=HI, preferred_element_type=F32)
    y2 = jnp.dot(a + pos[1:2], w1_ref[0, half:, :], precision=HI, preferred_element_type=F32)
    nrow = a.shape[0]
    h = y1 + pltpu.roll(y2, nrow - 1, axis=0)
    out = jnp.dot(jax.nn.gelu(h), w2_ref[0], precision=HI, preferred_element_type=F32)
    row = lax.broadcasted_iota(I32, out.shape, 0)
    o_ref[0, 0] = jnp.where(row < nrow - 1, out, 0.0)


def _compress(kv_chunks, pos, w1, w2):
    bsz, nslot, nch, cw = kv_chunks.shape
    dh = NSA_HEAD_DIM
    return pl.pallas_call(
        _compress_kernel,
        grid=(bsz, nslot),
        in_specs=[pl.BlockSpec((1, 1, nch, cw), lambda b, t: (b, t, 0, 0)),
                  pl.BlockSpec((1, 2, cw), lambda b, t: (t // NSA_KV_HEADS, 0, 0)),
                  pl.BlockSpec((1, 2 * cw, dh), lambda b, t: (t // NSA_KV_HEADS, 0, 0)),
                  pl.BlockSpec((1, dh, dh), lambda b, t: (t // NSA_KV_HEADS, 0, 0))],
        out_specs=pl.BlockSpec((1, 1, nch, dh), lambda b, t: (b, t, 0, 0)),
        out_shape=jax.ShapeDtypeStruct((bsz, nslot, nch, dh), F32),
        compiler_params=_cparams(("parallel", "parallel")),
        name="nsa_compress",
    )(kv_chunks, pos, w1, w2)


def _nt_dot(a, b, **kw):
    return lax.dot_general(a, b, (((1,), (1,)), ((), ())), preferred_element_type=F32, **kw)


def _nsa_kernel(slopes_ref, q_ref, kc_ref, vc_ref, ks_ref, vs_ref, kw_ref, vw_ref, gate_ref, o_ref,
                ksb, vst, kwb, vwt, q4_scr, *scr):
    g = pl.program_id(1)
    qi = pl.program_id(2)
    tq_n = q_ref.shape[2]
    dh = NSA_HEAD_DIM
    nr = NSA_Q_PER_KV
    seq = ks_ref.shape[2]
    tk_n = NSA_TK
    nb = seq // SEL_BLOCK

    @pl.when(qi == 0)
    def _():
        row = lax.broadcasted_iota(I32, (seq, dh), 0)
        lane = lax.broadcasted_iota(I32, (seq, dh), 1)
        blk = row // SEL_BLOCK
        pos = jnp.where((lane >= nb) & (lane < nb + 3), (blk * SEL_BLOCK).astype(F32),
                        jnp.where((lane >= nb + 3) & (lane < nb + 6), (row % SEL_BLOCK).astype(F32), 0.0))
        ksb[...] = jnp.concatenate([jnp.where(lane == blk, -MASKED, pos), ks_ref[0, 0]], axis=1).astype(BF16)
        kwb[...] = jnp.concatenate([pos, kw_ref[0, 0]], axis=1).astype(BF16)
        for c in range(seq // tk_n):
            rows = slice(c * tk_n, (c + 1) * tk_n)
            for src, dst in ((vs_ref, vst), (vw_ref, vwt)):
                v = src[0, 0, rows, :]
                dst[c] = jnp.concatenate([v, v], axis=1).T[:dh].astype(BF16)

    t0 = qi * tq_n
    tq = t0 + lax.broadcasted_iota(I32, (1, tq_n), 1)
    tq_f = tq.astype(F32)
    slopes = [slopes_ref[g * nr + r] for r in range(nr)]
    scale = dh ** -0.5
    q_t = []
    for pair in range(nr // 2):
        both = jnp.concatenate([q_ref[0, 2 * pair], q_ref[0, 2 * pair + 1]], axis=1).T * scale
        q_t += [both[:dh], both[dh:]]

    ncp = kc_ref.shape[2]
    kc = kc_ref[0, 0]
    vc = vc_ref[0, 0]
    vc_t = jnp.concatenate([vc, vc], axis=1).T[:dh].astype(BF16)
    n_sub = lax.broadcasted_iota(I32, (ncp, 1), 0)
    blk_end = n_sub * CMP_STRIDE + (CMP_BLOCK - 1)
    center = n_sub.astype(F32) * CMP_STRIDE + (CMP_BLOCK - 1) / 2.0
    valid_c = blk_end <= tq
    dist_c = tq_f - center
    psum = jnp.zeros((ncp, tq_n), F32)
    o_c = []
    for r in range(nr):
        s = jnp.dot(kc, q_t[r], precision=HI, preferred_element_type=F32)
        s = jnp.where(valid_c, s - slopes[r] * dist_c, NEG)
        e = jnp.exp(s - jnp.max(s, axis=0, keepdims=True))
        p = jnp.where(valid_c, e / jnp.sum(e, axis=0, keepdims=True), 0.0)
        psum = psum + p
        o_c.append(jnp.dot(vc_t, p.astype(BF16), preferred_element_type=F32))

    rowj = lax.broadcasted_iota(I32, (LANES, ncp), 0) * SEL_BLOCK
    coln = lax.broadcasted_iota(I32, (LANES, ncp), 1) * CMP_STRIDE
    overlap = jnp.where((coln < rowj + SEL_BLOCK) & (coln + CMP_BLOCK > rowj)
                        & (coln < (ncp - 1) * CMP_STRIDE) & (rowj < nb * SEL_BLOCK), 1.0, 0.0)
    imp = jnp.dot(overlap, psum, precision=HI, preferred_element_type=F32)[:nb]
    j_sub = lax.broadcasted_iota(I32, (nb, 1), 0)
    qblk = tq // SEL_BLOCK
    forced = (j_sub == 0) | (j_sub == qblk) | (j_sub == qblk - 1)
    imp = jnp.where(forced, FORCE, jnp.where(j_sub <= qblk, imp, NEG))
    cnt = jnp.zeros((nb, tq_n), F32)
    for i in range(nb):
        ci = imp[i:i + 1, :]
        tie = jnp.where(j_sub > i, 1.0, 0.0)
        cnt = cnt + jnp.where(ci > imp, 1.0, jnp.where(ci == imp, tie, 0.0))
    notsel = jnp.where(cnt < float(min(N_SEL, nb)), 0.0, 1.0)

    sub_h = lax.broadcasted_iota(I32, (dh - nb, 1), 0)
    for r in range(nr):
        scol = jnp.zeros((dh - nb, 1), F32)
        for i in range(3):
            piece = slopes_ref[(i + 1) * NSA_HEADS + g * nr + r]
            scol = jnp.where((sub_h == i) | (sub_h == 3 + i), piece, scol)
        q4_scr[:, r * tq_n:(r + 1) * tq_n] = jnp.concatenate(
            [notsel, jnp.broadcast_to(scol, (dh - nb, tq_n)), q_t[r]], axis=0).astype(BF16)

    nwt = WINDOW // tk_n
    nbuf = nwt + 2
    stats_s, stats_w = scr[0:3], scr[3:6]
    s_buf, p_buf, a_buf = scr[6:6 + nbuf], scr[6 + nbuf:6 + 2 * nbuf], scr[6 + 2 * nbuf:6 + 3 * nbuf]
    for m_ref, l_ref, acc_ref in (stats_s, stats_w):
        m_ref[...] = jnp.full(m_ref.shape, NEG, F32)
        l_ref[...] = jnp.zeros(l_ref.shape, F32)
        acc_ref[...] = jnp.zeros(acc_ref.shape, F32)
    key_i = lax.broadcasted_iota(I32, (tk_n, LANES), 0)
    qry_j = lax.broadcasted_iota(I32, (tk_n, LANES), 1)

    def scores(k_ref, kt, buf):
        k_tile = k_ref[pl.ds(pl.multiple_of(kt * tk_n, tk_n), tk_n), :]
        s_buf[buf][...] = jnp.dot(k_tile, q4_scr[...], preferred_element_type=F32)

    def softmax(buf, keep_fn, stats):
        m_ref, l_ref, _ = stats
        for cb in range(nr * tq_n // LANES):
            cols = slice(cb * LANES, (cb + 1) * LANES)
            s = s_buf[buf][:, cols]
            if keep_fn is not None:
                s = jnp.where(keep_fn(qry_j + (cb * LANES) % tq_n), s, -MASKED)
            m_prev = m_ref[:, cols]
            m_new = jnp.maximum(m_prev, jnp.max(s, axis=0, keepdims=True))
            alpha = jnp.exp(m_prev - m_new)
            p = jnp.exp(s - m_new)
            l_ref[:, cols] = alpha * l_ref[:, cols] + jnp.sum(p, axis=0, keepdims=True)
            m_ref[:, cols] = m_new
            a_buf[buf][:, cols] = alpha
            p_buf[buf][:, cols] = p.astype(BF16)

    def values(vt_ref, kt, buf, stats):
        acc_ref = stats[2]
        pv = jnp.dot(vt_ref[kt], p_buf[buf][...], preferred_element_type=F32)
        acc_ref[...] = acc_ref[...] * a_buf[buf][...] + pv

    def batch(jobs):
        for buf, (k_ref, _, kt, _, _) in enumerate(jobs):
            scores(k_ref, kt, buf)
        for buf, (_, vt_ref, kt, keep_fn, stats) in enumerate(jobs):
            softmax(buf, keep_fn, stats)
            values(vt_ref, kt, buf, stats)

    def sel_pair(i, carry):
        batch([(ksb, vst, 2 * i, None, stats_s), (ksb, vst, 2 * i + 1, None, stats_s)])
        return carry

    lax.fori_loop(0, qi // 2, sel_pair, 0)

    @pl.when(qi % 2 == 1)
    def _():
        batch([(ksb, vst, qi - 1, None, stats_s)])

    def causal(j):
        return key_i <= j

    jobs = []
    for back in range(nwt, 0, -1):
        off = jnp.where(qi >= back, 0, tk_n)
        keep_fn = (lambda j, off=off: key_i - off > j) if back == nwt else (lambda j, off=off: key_i >= off)
        jobs.append((kwb, vwt, jnp.maximum(qi - back, 0), keep_fn, stats_w))
    jobs.append((kwb, vwt, qi, causal, stats_w))
    jobs.append((ksb, vst, qi, causal, stats_s))
    batch(jobs)

    gsel = jnp.where(lax.broadcasted_iota(I32, (LANES, LANES), 0)
                     == lax.broadcasted_iota(I32, (LANES, LANES), 1) + g * (nr * N_BRANCH), 1.0, 0.0)
    gates = jax.nn.sigmoid(jnp.dot(gate_ref[0], gsel, precision=HI, preferred_element_type=F32)).T
    (_, l_s, acc_s), (_, l_w, acc_w) = stats_s, stats_w
    for pair in range(nr // 2):
        o_t = []
        for r in (2 * pair, 2 * pair + 1):
            c0 = r * N_BRANCH
            cols = slice(r * tq_n, (r + 1) * tq_n)
            o_t.append(gates[c0:c0 + 1, :] * o_c[r] + (gates[c0 + 1:c0 + 2, :] / l_s[:, cols]) * acc_s[:, cols]
                       + (gates[c0 + 2:c0 + 3, :] / l_w[:, cols]) * acc_w[:, cols])
        o_ref[0, :, pair * LANES:(pair + 1) * LANES] = jnp.concatenate(o_t, axis=0).T


def _nsa(slopes, proj_nsa, kvc, proj_gla):
    bsz, _, s, dh = proj_nsa.shape
    g_n, nr = NSA_KV_HEADS, NSA_Q_PER_KV
    tq = NSA_TQ
    tk = NSA_TK
    assert tq == tk and WINDOW % tk == 0 and 2 * dh == LANES
    ncp = kvc.shape[2]
    kv0 = NSA_HEADS
    nq = nr * tq
    nbuf = WINDOW // tk + 2

    def kv_spec(i):
        return pl.BlockSpec((1, 1, s, dh), lambda b, g, q, i=i: (b, kv0 + i * g_n + g, 0, 0))

    return pl.pallas_call(
        _nsa_kernel,
        grid=(bsz, g_n, s // tq),
        in_specs=[pl.BlockSpec(memory_space=pltpu.SMEM),
                  pl.BlockSpec((1, nr, tq, dh), lambda b, g, q: (b, g, q, 0)),
                  pl.BlockSpec((1, 1, ncp, dh), lambda b, g, q: (b, g, 0, 0)),
                  pl.BlockSpec((1, 1, ncp, dh), lambda b, g, q: (b, g_n + g, 0, 0)),
                  kv_spec(2), kv_spec(3), kv_spec(4), kv_spec(5),
                  pl.BlockSpec((1, tq, LANES), lambda b, g, q: (b, q, GLA_MISC_OFF // LANES))],
        out_specs=pl.BlockSpec((1, tq, nr * dh), lambda b, g, q: (b, q, g)),
        out_shape=jax.ShapeDtypeStruct((bsz, s, NSA_HEADS * dh), F32),
        scratch_shapes=[pltpu.VMEM((s, LANES), BF16), pltpu.VMEM((s // tk, dh, tk), BF16)] * 2
        + [pltpu.VMEM((LANES, nq), BF16)]
        + [pltpu.VMEM((1, nq), F32), pltpu.VMEM((1, nq), F32), pltpu.VMEM((dh, nq), F32)] * 2
        + [pltpu.VMEM((tk, nq), F32)] * nbuf + [pltpu.VMEM((tk, nq), BF16)] * nbuf + [pltpu.VMEM((1, nq), F32)] * nbuf,
        compiler_params=_cparams(("parallel", "parallel", "arbitrary")),
        name="nsa_attention",
    )(slopes, proj_nsa, kvc, kvc, proj_nsa, proj_nsa, proj_nsa, proj_nsa, proj_gla)


def _gla_kernel(q_ref, k_ref, v_ref, og_ref, lr_ref, wg_ref, bg_ref, nw_ref, o_ref, st_scr, la_scr, b_scr):
    rows_n = q_ref.shape[1]
    c_n, sub = GLA_CHUNK, GLA_SUB
    nh, dk, dv = GLA_HEADS, GLA_DK, GLA_DV

    @pl.when(pl.program_id(1) == 0)
    def _():
        st_scr[...] = jnp.zeros(st_scr.shape, F32)

    z = jnp.dot(lr_ref[0], wg_ref[...], precision=HI, preferred_element_type=F32) + bg_ref[...]
    la_scr[...] = (jnp.minimum(z, 0.0) - jnp.log1p(jnp.exp(-jnp.abs(z)))) * (1.0 / GLA_GATE_NORM)
    tril = jnp.where(lax.broadcasted_iota(I32, (c_n, c_n), 0) >= lax.broadcasted_iota(I32, (c_n, c_n), 1), 1.0, 0.0)
    row_c = lax.broadcasted_iota(I32, (c_n, 1), 0)
    row_s = lax.broadcasted_iota(I32, (sub, 1), 0)
    lane_c = lax.broadcasted_iota(I32, (1, c_n), 1)
    nw = nw_ref[...]
    hk = [slice(h * dk, (h + 1) * dk) for h in range(nh)]
    hv = [slice(h * dv, (h + 1) * dv) for h in range(nh)]

    def chunk(c, carry):
        r0 = pl.multiple_of(c * c_n, c_n)
        rows = pl.ds(r0, c_n)
        qc = q_ref[0, rows, :] * (dk ** -0.5)
        kc = k_ref[0, rows, :]
        vc = [v_ref[0, rows, hv[h]].astype(BF16) for h in range(nh)]
        b = jnp.dot(tril, la_scr[rows, :], precision=HI, preferred_element_type=F32)
        b_scr[...] = b
        st = [st_scr[h] for h in range(nh)]
        q_e = (qc * jnp.exp(b)).astype(BF16)
        o = [_nt_dot(q_e[:, hk[h]], st[h].astype(BF16)) for h in range(nh)]
        strips = [[] for _ in range(nh)]
        for blk in range(c_n // sub):
            lo = blk * sub
            q_i = qc[lo:lo + sub]
            b_i = b[lo:lo + sub]
            a = [jnp.zeros((sub, c_n), F32) for _ in range(nh)]
            if blk > 0:
                b_r = b_scr[lo - 1:lo, :]
                q_d = (q_i * jnp.exp(b_i - b_r)).astype(BF16)
                k_d = (kc * jnp.exp(jnp.where(row_c < lo, b_r - b, -jnp.inf))).astype(BF16)
                a = [_nt_dot(q_d[:, hk[h]], k_d[:, hk[h]]) for h in range(nh)]
            for j in range(sub):
                b_j = b_scr[lo + j:lo + j + 1, :]
                k_j = k_ref[0, pl.ds(r0 + lo + j, 1), :]
                prod = q_i * k_j * jnp.exp(jnp.where(row_s >= j, b_i - b_j, -jnp.inf))
                for h in range(nh):
                    col = jnp.sum(prod[:, hk[h]], axis=-1, keepdims=True)
                    a[h] = jnp.where(lane_c == lo + j, col, a[h]) if blk == 0 else (
                        a[h] + jnp.where(lane_c == lo + j, col, 0.0))
            for h in range(nh):
                strips[h].append(a[h])
        for h in range(nh):
            attn = jnp.concatenate(strips[h], axis=0)
            o[h] = o[h] + jnp.dot(attn.astype(BF16), vc[h], preferred_element_type=F32)
        b_last = b_scr[c_n - 1:c_n, :]
        k_dec = (kc * jnp.exp(b_last - b)).astype(BF16)
        decay = jnp.exp(b_last)
        for h in range(nh):
            st_scr[h] = st[h] * decay[:, hk[h]] + lax.dot_general(
                vc[h], k_dec[:, hk[h]], (((0,), (0,)), ((), ())), preferred_element_type=F32)
        for h in range(nh):
            og = og_ref[0, rows, hv[h]]
            on = o[h] * lax.rsqrt(jnp.mean(o[h] * o[h], axis=-1, keepdims=True) + EPS) * nw
            o_ref[0, rows, hv[h]] = on * (og * jax.nn.sigmoid(og))
        return carry

    lax.fori_loop(0, rows_n // c_n, chunk, 0)


def _gla(proj_gla, wg_pad, bg, nw):
    bsz, s, _ = proj_gla.shape
    nh, dk, dv = GLA_HEADS, GLA_DK, GLA_DV
    ts = GLA_TS
    wk, wv = nh * dk, nh * dv
    return pl.pallas_call(
        _gla_kernel,
        grid=(bsz, s // ts),
        in_specs=[pl.BlockSpec((1, ts, wk), lambda b, i: (b, i, GLA_Q_OFF // wk)),
                  pl.BlockSpec((1, ts, wk), lambda b, i: (b, i, GLA_K_OFF // wk)),
                  pl.BlockSpec((1, ts, wv), lambda b, i: (b, i, GLA_V_OFF // wv)),
                  pl.BlockSpec((1, ts, wv), lambda b, i: (b, i, GLA_OG_OFF // wv)),
                  pl.BlockSpec((1, ts, LANES), lambda b, i: (b, i, GLA_MISC_OFF // LANES)),
                  pl.BlockSpec((LANES, wk), lambda b, i: (0, 0)),
                  pl.BlockSpec((1, wk), lambda b, i: (0, 0)),
                  pl.BlockSpec((1, dv), lambda b, i: (0, 0))],
        out_specs=pl.BlockSpec((1, ts, wv), lambda b, i: (b, i, 0)),
        out_shape=jax.ShapeDtypeStruct((bsz, s, wv), F32),
        scratch_shapes=[pltpu.VMEM((nh, dv, dk), F32), pltpu.VMEM((ts, wk), F32), pltpu.VMEM((GLA_CHUNK, wk), F32)],
        compiler_params=_cparams(("parallel", "arbitrary")),
        name="gla",
    )(proj_gla, proj_gla, proj_gla, proj_gla, proj_gla, wg_pad, bg, nw)


def _outproj_kernel(nsa_ref, gla_ref, x_ref, wo_ref, g1_ref, sc_ref, sh_ref, nw_ref, wr_ref, br_ref,
                    x1_ref, h_ref, route_ref):
    half = nsa_ref.shape[2]
    acc = jnp.dot(nsa_ref[0].astype(BF16), wo_ref[:half, :], preferred_element_type=F32)
    acc = acc + jnp.dot(gla_ref[0].astype(BF16), wo_ref[half:, :], preferred_element_type=F32)
    x1 = x_ref[0] + g1_ref[0, 0] * acc
    x1_ref[0] = x1
    h = _modulated_norm(x1, nw_ref[...], sc_ref[0, 0], sh_ref[0, 0])
    h_ref[0] = h
    h_hi = h.astype(BF16)
    h_lo = (h - h_hi.astype(F32)).astype(BF16)
    t = jnp.dot(h_hi, wr_ref[...], preferred_element_type=F32)
    logits = (t[:, :LANES] + t[:, LANES:] + jnp.dot(h_lo, wr_ref[:, :LANES], preferred_element_type=F32)
              + br_ref[...])
    lane = lax.broadcasted_iota(I32, (1, LANES), 1)
    ninf = -jnp.inf
    is_g = (lane >= N_EXPERTS) & (lane < N_EXPERTS + N_GROUPS)
    gl = jnp.where(is_g, logits, ninf)
    ge = jnp.exp(gl - jnp.max(gl, axis=-1, keepdims=True))
    gp = ge / jnp.sum(ge, axis=-1, keepdims=True)
    gp_max = jnp.max(gp, axis=-1, keepdims=True)
    grp = jnp.min(jnp.where((gp == gp_max) & is_g, lane - N_EXPERTS, LANES), axis=-1, keepdims=True)
    in_grp = (lane // EXPERTS_PER_GROUP == grp) & (lane < N_EXPERTS)
    el = jnp.where(in_grp, logits, ninf)
    v1 = jnp.max(el, axis=-1, keepdims=True)
    i1 = jnp.min(jnp.where(el == v1, lane, LANES), axis=-1, keepdims=True)
    el2 = jnp.where(lane == i1, ninf, el)
    v2 = jnp.max(el2, axis=-1, keepdims=True)
    i2 = jnp.min(jnp.where(el2 == v2, lane, LANES), axis=-1, keepdims=True)
    e2 = jnp.exp(v2 - v1)
    den = 1.0 + e2
    w1 = gp_max * (1.0 / den)
    w2 = gp_max * (e2 / den)
    route_ref[0] = jnp.where(lane == 0, i1.astype(F32), jnp.where(lane == 1, i2.astype(F32), jnp.where(
        lane == 2, w1, jnp.where(lane == 3, w2, 0.0))))


def _outproj(o_nsa, o_gla, x, wo, mod4, nw, wr, br):
    bsz, s, d = x.shape
    tm = OUT_TM
    half = o_nsa.shape[2]

    def mod_spec(idx):
        return pl.BlockSpec((1, 1, 1, d), lambda b, i: (b, idx, 0, 0))

    row = lambda w: pl.BlockSpec((1, tm, w), lambda b, i: (b, i, 0))
    return pl.pallas_call(
        _outproj_kernel,
        grid=(bsz, s // tm),
        in_specs=[row(half), row(half), row(d),
                  pl.BlockSpec((2 * half, d), lambda b, i: (0, 0)),
                  mod_spec(2), mod_spec(4), mod_spec(3),
                  pl.BlockSpec((1, d), lambda b, i: (0, 0)),
                  pl.BlockSpec((d, 2 * LANES), lambda b, i: (0, 0)),
                  pl.BlockSpec((1, LANES), lambda b, i: (0, 0))],
        out_specs=[row(d), row(d), row(LANES)],
        out_shape=[jax.ShapeDtypeStruct((bsz, s, d), F32), jax.ShapeDtypeStruct((bsz, s, d), F32),
                   jax.ShapeDtypeStruct((bsz, s, LANES), F32)],
        compiler_params=_cparams(("parallel", "parallel")),
        name="outproj_router",
    )(o_nsa, o_gla, x, wo, mod4, mod4, mod4, nw, wr, br)


def _rank_kernel(route_ref, dest_ref, meta_ref, rank_scr):
    n = route_ref.shape[0]
    tm = RANK_TM
    lane_i = lax.broadcasted_iota(I32, (1, LANES), 1)
    lane = lane_i.astype(F32)
    strict = jnp.where(lax.broadcasted_iota(I32, (tm, tm), 0) > lax.broadcasted_iota(I32, (tm, tm), 1),
                       1.0, 0.0).astype(BF16)

    def two_lanes(a, b):
        return jnp.where(lane_i == 0, a, jnp.where(lane_i == 1, b, 0.0))

    def pick(e, table):
        return jnp.sum(jnp.where(lane == e, table, 0.0), axis=-1, keepdims=True)

    def count(i, seen):
        r0 = pl.multiple_of(i * tm, tm)
        rt = route_ref[pl.ds(r0, tm), :]
        e1, e2 = rt[:, 0:1], rt[:, 1:2]
        member = jnp.where(lane == e1, 1.0, jnp.where(lane == e2, 1.0, 0.0))
        before = jnp.dot(strict, member.astype(BF16), preferred_element_type=F32) + seen
        rank_scr[pl.ds(r0, tm), :] = two_lanes(pick(e1, before), pick(e2, before))
        return seen + jnp.sum(member, axis=0, keepdims=True)

    counts = lax.fori_loop(0, n // tm, count, jnp.zeros((1, LANES), F32))
    ntile = jnp.floor((counts + (MOE_TB - 1)) * (1.0 / MOE_TB))
    incl = jnp.where(lax.broadcasted_iota(I32, (LANES, LANES), 0) <= lax.broadcasted_iota(I32, (LANES, LANES), 1),
                     1.0, 0.0).astype(BF16)
    tile_end = jnp.dot(jnp.broadcast_to(ntile, (8, LANES)).astype(BF16), incl,
                       preferred_element_type=F32)[0:1]
    row_start = (tile_end - ntile) * MOE_TB

    def place(i, carry):
        r0 = pl.multiple_of(i * tm, tm)
        rt = route_ref[pl.ds(r0, tm), :]
        rk = rank_scr[pl.ds(r0, tm), :]
        d1 = pick(rt[:, 0:1], row_start) + rk[:, 0:1]
        d2 = pick(rt[:, 1:2], row_start) + rk[:, 1:2]
        dest_ref[pl.ds(r0, tm), :] = two_lanes(d1, d2).astype(I32)
        return carry

    lax.fori_loop(0, n // tm, place, 0)
    trow = lax.broadcasted_iota(I32, (meta_ref.shape[0], 1), 0).astype(F32)
    texp = jnp.sum(jnp.where((tile_end <= trow) & (lane_i < N_EXPERTS), 1.0, 0.0), axis=-1, keepdims=True)
    texp = jnp.minimum(texp, N_EXPERTS - 1.0)
    used = pick(N_EXPERTS - 1.0, tile_end)
    diag = lax.broadcasted_iota(I32, (meta_ref.shape[0], LANES), 0) == lane_i
    end_rows = jnp.sum(jnp.where(diag, tile_end, 0.0), axis=-1, keepdims=True)
    ntile_rows = jnp.sum(jnp.where(diag, ntile, 0.0), axis=-1, keepdims=True)
    meta_ref[...] = jnp.where(lane_i == 2, end_rows, jnp.where(lane_i == 3, ntile_rows, two_lanes(
        texp, jnp.broadcast_to(used, texp.shape)))).astype(I32)


def _rank(route):
    n = route.shape[0]
    return pl.pallas_call(
        _rank_kernel,
        out_shape=[jax.ShapeDtypeStruct((n, LANES), I32), jax.ShapeDtypeStruct((LANES, LANES), I32)],
        scratch_shapes=[pltpu.VMEM((n, LANES), F32)],
        compiler_params=pltpu.CompilerParams(vmem_limit_bytes=VMEM_LIMIT),
        name="moe_rank",
    )(route)


def _dispatch_kernel(dest_ref, ends_ref, h_ref, xs_ref, zero_scr, sem, zsem):
    i = pl.program_id(0)
    tm = h_ref.shape[0]
    tb = zero_scr.shape[0]

    @pl.when(i == 0)
    def _():
        zero_scr[...] = jnp.zeros(zero_scr.shape, F32)

        def zero_copy(e):
            r0 = pl.multiple_of((ends_ref[e] - 1) * tb, tb)
            return pltpu.make_async_copy(zero_scr, xs_ref.at[pl.ds(r0, tb)], zsem)

        def start(e, carry):
            @pl.when(ends_ref[N_EXPERTS + e] > 0)
            def _():
                zero_copy(e).start()
            return carry

        def wait(e, carry):
            @pl.when(ends_ref[N_EXPERTS + e] > 0)
            def _():
                zero_copy(e).wait()
            return carry

        def tail_copy(t):
            return pltpu.make_async_copy(zero_scr, xs_ref.at[pl.ds(pl.multiple_of(t * tb, tb), tb)], zsem)

        def tail_start(t, carry):
            tail_copy(t).start()
            return carry

        def tail_wait(t, carry):
            tail_copy(t).wait()
            return carry

        used = ends_ref[2 * N_EXPERTS]
        lax.fori_loop(0, N_EXPERTS, start, 0)
        lax.fori_loop(used, xs_ref.shape[0] // tb, tail_start, 0)
        lax.fori_loop(0, N_EXPERTS, wait, 0)
        lax.fori_loop(used, xs_ref.shape[0] // tb, tail_wait, 0)

    def issue(r, carry):
        p = (i * tm + r) * TOP_K
        for k in range(TOP_K):
            pltpu.make_async_copy(h_ref.at[pl.ds(r, 1)], xs_ref.at[pl.ds(dest_ref[p + k], 1)], sem).start()
        return carry

    lax.fori_loop(0, tm, issue, 0)
    for k in range(TOP_K):
        pltpu.make_async_copy(h_ref, xs_ref.at[pl.ds(0, tm)], sem).wait()


def _dispatch(dest_flat, ends_flat, h, cap):
    n, d = h.shape
    tm = DISPATCH_TM
    return pl.pallas_call(
        _dispatch_kernel,
        grid_spec=pltpu.PrefetchScalarGridSpec(
            num_scalar_prefetch=2, grid=(n // tm,),
            in_specs=[pl.BlockSpec((tm, d), lambda i, dst, ends: (i, 0))],
            out_specs=pl.BlockSpec(memory_space=pl.ANY),
            scratch_shapes=[pltpu.VMEM((MOE_TB, d), F32), pltpu.SemaphoreType.DMA(()), pltpu.SemaphoreType.DMA(())]),
        out_shape=jax.ShapeDtypeStruct((cap, d), F32),
        compiler_params=_cparams(("arbitrary",)),
        name="moe_dispatch",
    )(dest_flat, ends_flat, h)


def _ffn_kernel(meta_ref, x_ref, wg_ref, wu_ref, wd_ref, y_ref, wgb, wub, wdb):
    t = pl.program_id(0)
    ntile = pl.num_programs(0)
    e = meta_ref[t]
    e_prev = meta_ref[jnp.maximum(t - 1, 0)]
    active = t < meta_ref[ntile]

    @pl.when(active & ((t == 0) | (e != e_prev)))
    def _():
        wgb[...] = wg_ref[0].astype(BF16)
        wub[...] = wu_ref[0].astype(BF16)
        wdb[...] = wd_ref[0].astype(BF16)

    @pl.when(active)
    def _():
        x = x_ref[...].astype(BF16)
        gate = jnp.dot(x, wgb[...], preferred_element_type=F32)
        up = jnp.dot(x, wub[...], preferred_element_type=F32)
        act = (gate * jax.nn.sigmoid(gate)) * up
        y_ref[...] = jnp.dot(act.astype(BF16), wdb[...], preferred_element_type=F32)

    @pl.when(jnp.logical_not(active))
    def _():
        y_ref[...] = jnp.zeros(y_ref.shape, F32)


def _ffn(meta_flat, xs, wg, wu, wd):
    cap, d = xs.shape
    ff = wg.shape[2]
    tb = MOE_TB
    ntile = cap // tb
    return pl.pallas_call(
        _ffn_kernel,
        grid_spec=pltpu.PrefetchScalarGridSpec(
            num_scalar_prefetch=1, grid=(ntile,),
            in_specs=[pl.BlockSpec((tb, d), lambda t, m: (jnp.minimum(t, m[ntile] - 1), 0)),
                      pl.BlockSpec((1, d, ff), lambda t, m: (m[t], 0, 0)),
                      pl.BlockSpec((1, d, ff), lambda t, m: (m[t], 0, 0)),
                      pl.BlockSpec((1, ff, d), lambda t, m: (m[t], 0, 0))],
            out_specs=pl.BlockSpec((tb, d), lambda t, m: (t, 0)),
            scratch_shapes=[pltpu.VMEM((d, ff), BF16), pltpu.VMEM((d, ff), BF16), pltpu.VMEM((ff, d), BF16)]),
        out_shape=jax.ShapeDtypeStruct((cap, d), F32),
        compiler_params=_cparams(("arbitrary",)),
        name="moe_ffn",
    )(meta_flat, xs, wg, wu, wd)


def _combine_kernel(dest_ref, y_ref, x1_ref, route_ref, g2_ref, nf_ref, o_ref, ybuf, sem):
    i = pl.program_id(0)
    nstep = pl.num_programs(0)
    tm = x1_ref.shape[0]

    def issue(tile, slot):
        def body(r, carry):
            p = (tile * tm + r) * TOP_K
            for k in range(TOP_K):
                pltpu.make_async_copy(y_ref.at[pl.ds(dest_ref[p + k], 1)], ybuf.at[slot, k, pl.ds(r, 1)],
                                      sem.at[slot]).start()
            return carry

        lax.fori_loop(0, tm, body, 0)

    @pl.when(i == 0)
    def _():
        issue(0, 0)

    @pl.when(i + 1 < nstep)
    def _():
        issue(i + 1, (i + 1) % 2)

    slot = i % 2
    for k in range(TOP_K):
        pltpu.make_async_copy(y_ref.at[pl.ds(0, tm)], ybuf.at[slot, k], sem.at[slot]).wait()
    rt = route_ref[...]
    moe = rt[:, 2:3] * ybuf[slot, 0] + rt[:, 3:4] * ybuf[slot, 1]
    xo = x1_ref[...] + g2_ref[0, 0] * moe
    o_ref[...] = xo * lax.rsqrt(jnp.mean(xo * xo, axis=-1, keepdims=True) + EPS) * nf_ref[...]


def _combine(dest_flat, y, x1, route, mod4, nf, seq):
    n, d = x1.shape
    tm = COMB_TM
    tiles_per_seq = seq // tm
    return pl.pallas_call(
        _combine_kernel,
        grid_spec=pltpu.PrefetchScalarGridSpec(
            num_scalar_prefetch=1, grid=(n // tm,),
            in_specs=[pl.BlockSpec(memory_space=pl.ANY),
                      pl.BlockSpec((tm, d), lambda i, dst: (i, 0)),
                      pl.BlockSpec((tm, LANES), lambda i, dst: (i, 0)),
                      pl.BlockSpec((1, 1, 1, d), lambda i, dst: (i // tiles_per_seq, 5, 0, 0)),
                      pl.BlockSpec((1, d), lambda i, dst: (0, 0))],
            out_specs=pl.BlockSpec((tm, d), lambda i, dst: (i, 0)),
            scratch_shapes=[pltpu.VMEM((2, TOP_K, tm, d), F32), pltpu.SemaphoreType.DMA((2,))]),
        out_shape=jax.ShapeDtypeStruct((n, d), F32),
        compiler_params=_cparams(("arbitrary",)),
        name="moe_combine",
    )(dest_flat, y, x1, route, mod4, nf)


def _alibi_slopes():
    n = NSA_HEADS
    full = jnp.asarray(2.0 ** (-8.0 * np.arange(1, n + 1) / n), dtype=F32)
    pieces, rest = [], full
    for _ in range(3):
        piece = rest.astype(BF16).astype(F32)
        pieces.append(piece)
        rest = rest - piece
    return jnp.concatenate([full] + pieces)


def _layer(x, c, w_ada, b_ada, norm1_w, w_in, cmp_pos_k, cmp_w1_k, cmp_w2_k, cmp_pos_v, cmp_w1_v, cmp_w2_v,
           gla_w_gate2, gla_b_gate, gla_norm_w, w_out, norm2_w, w_rg, b_rg, w_re, b_re, w_eg, w_eu, w_ed):
    bsz, s, d = x.shape
    dh = NSA_HEAD_DIM
    mod4 = _adaln(c, w_ada, b_ada).reshape(bsz, 6, 1, d)

    o_gate = NSA_COLS
    o_gla = o_gate + NSA_GATE_COLS
    o_lr = o_gla + 2 * GLA_HEADS * GLA_DK + 2 * GLA_HEADS * GLA_DV
    w_nsa = w_in[:, :NSA_COLS].astype(BF16)
    w_gla = jnp.concatenate(
        [w_in[:, o_gla:o_lr], w_in[:, o_gate:o_gla], w_in[:, o_lr:],
         jnp.zeros((d, LANES - NSA_GATE_COLS - GLA_GATE_RANK), F32)], axis=1).astype(BF16)
    nw1 = norm1_w.reshape(1, d)
    proj_nsa = _inproj(x, mod4, nw1, w_nsa, sc_idx=1, sh_idx=0, tn=INPROJ_TN_NSA, head_major=True)
    proj_gla = _inproj(x, mod4, nw1, w_gla, sc_idx=1, sh_idx=0, tn=INPROJ_TN_GLA, head_major=False)

    nkv = 2 * NSA_KV_HEADS
    chunk_w = CMP_STRIDE * dh
    kv_chunks = proj_nsa[:, NSA_HEADS:NSA_HEADS + nkv].reshape(bsz, nkv, s // CMP_STRIDE, chunk_w)
    pos = jnp.stack([cmp_pos_k, cmp_pos_v]).reshape(2, 2, chunk_w)
    kvc = _compress(kv_chunks, pos, jnp.stack([cmp_w1_k, cmp_w1_v]), jnp.stack([cmp_w2_k, cmp_w2_v]))
    o_nsa = _nsa(_alibi_slopes(), proj_nsa, kvc, proj_gla)

    wg_pad = jnp.zeros((LANES, GLA_HEADS * GLA_DK), F32).at[
        NSA_GATE_COLS:NSA_GATE_COLS + GLA_GATE_RANK].set(gla_w_gate2)
    o_gla_out = _gla(proj_gla, wg_pad, gla_b_gate.reshape(1, -1), gla_norm_w.reshape(1, -1))

    wr = jnp.concatenate([w_re, w_rg, jnp.zeros((d, LANES - N_EXPERTS - N_GROUPS), F32)], axis=1)
    br = jnp.concatenate([b_re, b_rg, jnp.zeros((LANES - N_EXPERTS - N_GROUPS,), F32)]).reshape(1, LANES)
    wr_hi = wr.astype(BF16)
    wr_cat = jnp.concatenate([wr_hi, (wr - wr_hi.astype(F32)).astype(BF16)], axis=1)
    x1, h2, route = _outproj(o_nsa, o_gla_out, x, w_out.astype(BF16), mod4, norm2_w.reshape(1, d), wr_cat, br)

    n = bsz * s
    npair = n * TOP_K
    cap = npair + N_EXPERTS * MOE_TB
    ntile = cap // MOE_TB
    route2 = route.reshape(n, LANES)
    dest, meta = _rank(route2)
    dest_flat = dest[:, :TOP_K].reshape(npair)
    meta_flat = jnp.concatenate([meta[:ntile, 0], meta[:1, 1]])
    ends_flat = jnp.concatenate([meta[:N_EXPERTS, 2], meta[:N_EXPERTS, 3], meta[:1, 1]])
    xs = _dispatch(dest_flat, ends_flat, h2.reshape(n, d), cap)
    y = _ffn(meta_flat, xs, w_eg, w_eu, w_ed)
    return x1.reshape(n, d), y, dest_flat, route2, mod4


def kernel(x, c, w_ada, b_ada, norm1_w, w_in, cmp_pos_k, cmp_w1_k, cmp_w2_k, cmp_pos_v, cmp_w1_v, cmp_w2_v,
           gla_w_gate2, gla_b_gate, gla_norm_w, w_out, norm2_w, w_router_group, b_router_group, w_router_expert,
           b_router_expert, w_expert_gate, w_expert_up, w_expert_down, norm_f_w):
    bsz, s, d = x.shape
    assert w_ada.shape[0] == 1, "single layer"
    x1, y, dest_flat, route2, mod4 = _layer(
        x, c, w_ada[0], b_ada[0], norm1_w[0], w_in[0], cmp_pos_k[0], cmp_w1_k[0], cmp_w2_k[0], cmp_pos_v[0],
        cmp_w1_v[0], cmp_w2_v[0], gla_w_gate2[0], gla_b_gate[0], gla_norm_w[0], w_out[0], norm2_w[0],
        w_router_group[0], b_router_group[0], w_router_expert[0], b_router_expert[0],
        w_expert_gate[0], w_expert_up[0], w_expert_down[0])
    out = _combine(dest_flat, y, x1, route2, mod4, norm_f_w.reshape(1, d), s)
    return out.reshape(bsz, s, d)
```

```python
import numpy as np
import jax
import jax.numpy as jnp
from jax import lax
from jax.experimental import pallas as pl
from jax.experimental.pallas import tpu as pltpu

F32 = jnp.float32
BF16 = jnp.bfloat16
I32 = jnp.int32
HI = lax.Precision.HIGHEST

D_MODEL = 2048
NSA_HEAD_DIM = 64
NSA_HEADS = 16
NSA_KV_HEADS = 4
NSA_Q_PER_KV = 4
CMP_BLOCK = 32
CMP_STRIDE = 16
SEL_BLOCK = 64
N_SEL = 16
WINDOW = 512
N_BRANCH = 3
GLA_HEADS = 4
GLA_DV = 256
GLA_DK = 128
GLA_GATE_RANK = 16
GLA_GATE_NORM = 16.0
GLA_CHUNK = 64
GLA_SUB = 16
N_GROUPS = 4
EXPERTS_PER_GROUP = 8
N_EXPERTS = 32
TOP_K = 2
EXPERT_FF = 512
EPS = 1e-6
NEG = -1e30
FORCE = 1e30
MASKED = 2.0 ** 100

NSA_Q_COLS = NSA_HEADS * NSA_HEAD_DIM
NSA_KV_COLS = 2 * N_BRANCH * NSA_KV_HEADS * NSA_HEAD_DIM
NSA_GATE_COLS = N_BRANCH * NSA_HEADS
NSA_COLS = NSA_Q_COLS + NSA_KV_COLS
NSA_SLOTS = NSA_COLS // NSA_HEAD_DIM
GLA_Q_OFF = 0
GLA_K_OFF = GLA_HEADS * GLA_DK
GLA_V_OFF = 2 * GLA_HEADS * GLA_DK
GLA_OG_OFF = GLA_V_OFF + GLA_HEADS * GLA_DV
GLA_MISC_OFF = GLA_OG_OFF + GLA_HEADS * GLA_DV
LANES = 128
GLA_COLS = GLA_MISC_OFF + LANES

VMEM_LIMIT = 56 * 1024 * 1024

ADA_TN = 768
INPROJ_TM = 256
INPROJ_TN = 512
NSA_TQ = 256
NSA_TK = 256
GLA_TS = 512
OUT_TM = 256
RANK_TM = 256
MOE_TB = 256
DISPATCH_TM = 256
COMB_TM = 256


def _cparams(sem):
    return pltpu.CompilerParams(dimension_semantics=sem, vmem_limit_bytes=VMEM_LIMIT)


def _adaln_kernel(ct_ref, w_ref, b_ref, o_ref, s_scr):
    nb = ct_ref.shape[1]
    kdim, tn = w_ref.shape

    @pl.when(pl.program_id(0) == 0)
    def _():
        ct = ct_ref[...]
        s = ct * jax.nn.sigmoid(ct)
        for b in range(nb):
            s_scr[b] = jnp.broadcast_to(s[:, b:b + 1], (kdim, LANES))

    def body(k, accs):
        r = pl.multiple_of(k * 8, 8)
        w8 = w_ref[pl.ds(r, 8), :]
        out = []
        for b, acc in enumerate(accs):
            s8 = s_scr[b, pl.ds(r, 8), :]
            out.append(acc + w8 * jnp.concatenate([s8] * (tn // LANES), axis=1))
        return tuple(out)

    accs = lax.fori_loop(0, kdim // 8, body, tuple(jnp.zeros((8, tn), F32) for _ in range(nb)), unroll=2)
    bias = b_ref[...]
    for b, acc in enumerate(accs):
        o_ref[b:b + 1, :] = jnp.sum(acc, axis=0, keepdims=True) + bias


def _adaln(c, w, b):
    nb, d = c.shape
    n = w.shape[1]
    return pl.pallas_call(
        _adaln_kernel,
        grid=(n // ADA_TN,),
        in_specs=[pl.BlockSpec((d, nb), lambda j: (0, 0)),
                  pl.BlockSpec((d, ADA_TN), lambda j: (0, j)),
                  pl.BlockSpec((1, ADA_TN), lambda j: (0, j))],
        out_specs=pl.BlockSpec((nb, ADA_TN), lambda j: (0, j)),
        out_shape=jax.ShapeDtypeStruct((nb, n), F32),
        scratch_shapes=[pltpu.VMEM((nb, d, LANES), F32)],
        compiler_params=_cparams(("arbitrary",)),
        name="adaln",
    )(c.T, w, b.reshape(1, n))


def _modulated_norm(x, nw, sc, sh):
    ms = jnp.mean(x * x, axis=-1, keepdims=True)
    h = x * lax.rsqrt(ms + EPS) * nw
    return h * (1.0 + sc) + sh


def _inproj_kernel(x_ref, sc_ref, sh_ref, nw_ref, wn_ref, wg_ref, on_ref, og_ref):
    h = _modulated_norm(x_ref[0], nw_ref[...], sc_ref[0, 0], sh_ref[0, 0]).astype(BF16)
    dh = NSA_HEAD_DIM
    tn = INPROJ_TN
    for c in range(wn_ref.shape[1] // tn):
        acc = jnp.dot(h, wn_ref[:, c * tn:(c + 1) * tn], preferred_element_type=F32)
        for u in range(tn // dh):
            on_ref[0, c * (tn // dh) + u] = acc[:, u * dh:(u + 1) * dh]
    og_ref[0] = jnp.dot(h, wg_ref[...], preferred_element_type=F32)


def _inproj(x, mod4, nw, w_nsa, w_gla, *, sc_idx, sh_idx):
    bsz, s, d = x.shape
    n_nsa, n_gla = w_nsa.shape[1], w_gla.shape[1]
    tm = INPROJ_TM
    dh = NSA_HEAD_DIM
    return pl.pallas_call(
        _inproj_kernel,
        grid=(bsz, s // tm),
        in_specs=[pl.BlockSpec((1, tm, d), lambda b, i: (b, i, 0)),
                  pl.BlockSpec((1, 1, 1, d), lambda b, i: (b, sc_idx, 0, 0)),
                  pl.BlockSpec((1, 1, 1, d), lambda b, i: (b, sh_idx, 0, 0)),
                  pl.BlockSpec((1, d), lambda b, i: (0, 0)),
                  pl.BlockSpec((d, n_nsa), lambda b, i: (0, 0), pipeline_mode=pl.Buffered(1)),
                  pl.BlockSpec((d, n_gla), lambda b, i: (0, 0), pipeline_mode=pl.Buffered(1))],
        out_specs=[pl.BlockSpec((1, n_nsa // dh, tm, dh), lambda b, i: (b, 0, i, 0)),
                   pl.BlockSpec((1, tm, n_gla), lambda b, i: (b, i, 0))],
        out_shape=[jax.ShapeDtypeStruct((bsz, n_nsa // dh, s, dh), F32),
                   jax.ShapeDtypeStruct((bsz, s, n_gla), F32)],
        compiler_params=_cparams(("parallel", "parallel")),
        name="inproj",
    )(x, mod4, mod4, nw, w_nsa, w_gla)


def _compress_kernel(a_ref, pos_ref, w1_ref, w2_ref, o_ref):
    nch = a_ref.shape[2] // CMP_STRIDE
    a = jnp.concatenate([a_ref[0, 0, pl.ds(t, nch, stride=CMP_STRIDE), :] for t in range(CMP_STRIDE)], axis=1)
    pos = pos_ref[0]
    half = a.shape[1]
    y1 = jnp.dot(a + pos[0:1], w1_ref[0, :half, :], precision=HI, preferred_element_type=F32)
    y2 = jnp.dot(a + pos[1:2], w1_ref[0, half:, :], precision=HI, preferred_element_type=F32)
    nrow = a.shape[0]
    h = y1 + pltpu.roll(y2, nrow - 1, axis=0)
    out = jnp.dot(jax.nn.gelu(h), w2_ref[0], precision=HI, preferred_element_type=F32)
    row = lax.broadcasted_iota(I32, out.shape, 0)
    o_ref[0, 0] = jnp.where(row < nrow - 1, out, 0.0)


def _compress(proj_nsa, pos, w1, w2):
    bsz, _, s, dh = proj_nsa.shape
    nslot = 2 * NSA_KV_HEADS
    nch = s // CMP_STRIDE
    cw = CMP_STRIDE * dh
    return pl.pallas_call(
        _compress_kernel,
        grid=(bsz, nslot),
        in_specs=[pl.BlockSpec((1, 1, s, dh), lambda b, t: (b, NSA_HEADS + t, 0, 0)),
                  pl.BlockSpec((1, 2, cw), lambda b, t: (t // NSA_KV_HEADS, 0, 0)),
                  pl.BlockSpec((1, 2 * cw, dh), lambda b, t: (t // NSA_KV_HEADS, 0, 0)),
                  pl.BlockSpec((1, dh, dh), lambda b, t: (t // NSA_KV_HEADS, 0, 0))],
        out_specs=pl.BlockSpec((1, 1, nch, dh), lambda b, t: (b, t, 0, 0)),
        out_shape=jax.ShapeDtypeStruct((bsz, nslot, nch, dh), F32),
        compiler_params=_cparams(("parallel", "parallel")),
        name="nsa_compress",
    )(proj_nsa, pos, w1, w2)


def _hi_lo(x):
    hi = x.astype(BF16)
    return hi, (x - hi.astype(F32)).astype(BF16)


def _nt_dot(a, b, **kw):
    return lax.dot_general(a, b, (((1,), (1,)), ((), ())), preferred_element_type=F32, **kw)


def _nsa_kernel(slopes_ref, q_ref, kc_ref, vc_ref, ks_ref, vs_ref, kw_ref, vw_ref, gate_ref, o_ref,
                ksb, vst, kwb, vwt, q4_scr, *scr):
    g = pl.program_id(1)
    qi = pl.program_id(2)
    tq_n = q_ref.shape[2]
    dh = NSA_HEAD_DIM
    nr = NSA_Q_PER_KV
    seq = ks_ref.shape[2]
    tk_n = NSA_TK
    nb = seq // SEL_BLOCK

    @pl.when(qi == 0)
    def _():
        row = lax.broadcasted_iota(I32, (seq, dh), 0)
        lane = lax.broadcasted_iota(I32, (seq, dh), 1)
        blk = row // SEL_BLOCK
        pos = jnp.where((lane >= nb) & (lane < nb + 3), (blk * SEL_BLOCK).astype(F32),
                        jnp.where((lane >= nb + 3) & (lane < nb + 6), (row % SEL_BLOCK).astype(F32), 0.0))
        ksb[...] = jnp.concatenate([jnp.where(lane == blk, -MASKED, pos), ks_ref[0, 0]], axis=1).astype(BF16)
        kwb[...] = jnp.concatenate([pos, kw_ref[0, 0]], axis=1).astype(BF16)
        for c in range(seq // tk_n):
            rows = slice(c * tk_n, (c + 1) * tk_n)
            for src, dst in ((vs_ref, vst), (vw_ref, vwt)):
                v = src[0, 0, rows, :]
                dst[c] = jnp.concatenate([v, v], axis=1).T[:dh].astype(BF16)

    t0 = qi * tq_n
    tq = t0 + lax.broadcasted_iota(I32, (1, tq_n), 1)
    tq_f = tq.astype(F32)
    slopes = [slopes_ref[g * nr + r] for r in range(nr)]
    scale = dh ** -0.5
    q_t = []
    for pair in range(nr // 2):
        both = jnp.concatenate([q_ref[0, 2 * pair], q_ref[0, 2 * pair + 1]], axis=1).T * scale
        q_t += [both[:dh], both[dh:]]

    ncp = kc_ref.shape[2]
    kc = kc_ref[0, 0]
    vc = vc_ref[0, 0]
    vc_t = jnp.concatenate([vc, vc], axis=1).T[:dh].astype(BF16)
    n_sub = lax.broadcasted_iota(I32, (ncp, 1), 0)
    blk_end = n_sub * CMP_STRIDE + (CMP_BLOCK - 1)
    center = n_sub.astype(F32) * CMP_STRIDE + (CMP_BLOCK - 1) / 2.0
    valid_c = blk_end <= tq
    dist_c = tq_f - center
    psum = jnp.zeros((ncp, tq_n), F32)
    o_c = []
    kc_hi, kc_lo = _hi_lo(kc)
    kc_cat = jnp.concatenate([kc_hi, kc_lo, kc_hi, jnp.zeros_like(kc_hi)], axis=1)
    for r in range(nr):
        q_hi, q_lo = _hi_lo(q_t[r])
        q_cat = jnp.concatenate([q_hi, q_hi, q_lo, jnp.zeros_like(q_hi)], axis=0)
        s = jnp.dot(kc_cat, q_cat, preferred_element_type=F32)
        s = jnp.where(valid_c, s - slopes[r] * dist_c, NEG)
        e = jnp.exp(s - jnp.max(s, axis=0, keepdims=True))
        p = jnp.where(valid_c, e / jnp.sum(e, axis=0, keepdims=True), 0.0)
        psum = psum + p
        o_c.append(jnp.dot(vc_t, p.astype(BF16), preferred_element_type=F32))

    rowj = lax.broadcasted_iota(I32, (LANES, ncp), 0) * SEL_BLOCK
    coln = lax.broadcasted_iota(I32, (LANES, ncp), 1) * CMP_STRIDE
    overlap = jnp.where((coln < rowj + SEL_BLOCK) & (coln + CMP_BLOCK > rowj)
                        & (coln < (ncp - 1) * CMP_STRIDE) & (rowj < nb * SEL_BLOCK), 1.0, 0.0)
    ov = overlap.astype(BF16)
    imp = jnp.dot(jnp.concatenate([ov, ov], axis=1), jnp.concatenate(_hi_lo(psum), axis=0),
                  preferred_element_type=F32)[:nb]
    j_sub = lax.broadcasted_iota(I32, (nb, 1), 0)
    qblk = tq // SEL_BLOCK
    forced = (j_sub == 0) | (j_sub == qblk) | (j_sub == qblk - 1)
    imp = jnp.where(forced, FORCE, jnp.where(j_sub <= qblk, imp, NEG))
    cnt = jnp.zeros((nb, tq_n), F32)
    for i in range(nb):
        ci = imp[i:i + 1, :]
        tie = jnp.where(j_sub > i, 1.0, 0.0)
        cnt = cnt + jnp.where(ci > imp, 1.0, jnp.where(ci == imp, tie, 0.0))
    notsel = jnp.where(cnt < float(min(N_SEL, nb)), 0.0, 1.0)

    sub_h = lax.broadcasted_iota(I32, (dh - nb, 1), 0)
    for r in range(nr):
        scol = jnp.zeros((dh - nb, 1), F32)
        for i in range(3):
            piece = slopes_ref[(i + 1) * NSA_HEADS + g * nr + r]
            scol = jnp.where((sub_h == i) | (sub_h == 3 + i), piece, scol)
        q4_scr[:, r * tq_n:(r + 1) * tq_n] = jnp.concatenate(
            [notsel, jnp.broadcast_to(scol, (dh - nb, tq_n)), q_t[r]], axis=0).astype(BF16)

    nwt = WINDOW // tk_n
    nbuf = nwt + 2
    stats_s, stats_w = scr[0:3], scr[3:6]
    s_buf, p_buf, a_buf = scr[6:6 + nbuf], scr[6 + nbuf:6 + 2 * nbuf], scr[6 + 2 * nbuf:6 + 3 * nbuf]
    for m_ref, l_ref, acc_ref in (stats_s, stats_w):
        m_ref[...] = jnp.full(m_ref.shape, NEG, F32)
        l_ref[...] = jnp.zeros(l_ref.shape, F32)
        acc_ref[...] = jnp.zeros(acc_ref.shape, F32)
    key_i = lax.broadcasted_iota(I32, (tk_n, LANES), 0)
    qry_j = lax.broadcasted_iota(I32, (tk_n, LANES), 1)

    def scores(k_ref, kt, buf):
        k_tile = k_ref[pl.ds(pl.multiple_of(kt * tk_n, tk_n), tk_n), :]
        s_buf[buf][...] = jnp.dot(k_tile, q4_scr[...], preferred_element_type=F32)

    def softmax(buf, keep_fn, stats):
        m_ref, l_ref, _ = stats
        for cb in range(nr * tq_n // LANES):
            cols = slice(cb * LANES, (cb + 1) * LANES)
            s = s_buf[buf][:, cols]
            if keep_fn is not None:
                s = jnp.where(keep_fn(qry_j + (cb * LANES) % tq_n), s, -MASKED)
            m_prev = m_ref[:, cols]
            m_new = jnp.maximum(m_prev, jnp.max(s, axis=0, keepdims=True))
            alpha = jnp.exp(m_prev - m_new)
            p = jnp.exp(s - m_new)
            l_ref[:, cols] = alpha * l_ref[:, cols] + jnp.sum(p, axis=0, keepdims=True)
            m_ref[:, cols] = m_new
            a_buf[buf][:, cols] = alpha
            p_buf[buf][:, cols] = p.astype(BF16)

    def values(vt_ref, kt, buf, stats):
        acc_ref = stats[2]
        pv = jnp.dot(vt_ref[kt], p_buf[buf][...], preferred_element_type=F32)
        acc_ref[...] = acc_ref[...] * a_buf[buf][...] + pv

    def batch(jobs):
        for buf, (k_ref, _, kt, _, _) in enumerate(jobs):
            scores(k_ref, kt, buf)
        for buf, (_, vt_ref, kt, keep_fn, stats) in enumerate(jobs):
            softmax(buf, keep_fn, stats)
            values(vt_ref, kt, buf, stats)

    def sel_pair(i, carry):
        batch([(ksb, vst, 2 * i, None, stats_s), (ksb, vst, 2 * i + 1, None, stats_s)])
        return carry

    lax.fori_loop(0, qi // 2, sel_pair, 0)

    @pl.when(qi % 2 == 1)
    def _():
        batch([(ksb, vst, qi - 1, None, stats_s)])

    def causal(j):
        return key_i <= j

    jobs = []
    for back in range(nwt, 0, -1):
        off = jnp.where(qi >= back, 0, tk_n)
        keep_fn = (lambda j, off=off: key_i - off > j) if back == nwt else (lambda j, off=off: key_i >= off)
        jobs.append((kwb, vwt, jnp.maximum(qi - back, 0), keep_fn, stats_w))
    jobs.append((kwb, vwt, qi, causal, stats_w))
    jobs.append((ksb, vst, qi, causal, stats_s))
    batch(jobs)

    gsel = jnp.where(lax.broadcasted_iota(I32, (LANES, LANES), 0)
                     == lax.broadcasted_iota(I32, (LANES, LANES), 1) + g * (nr * N_BRANCH), 1.0, 0.0)
    gs = gsel.astype(BF16)
    gates = jax.nn.sigmoid(jnp.dot(jnp.concatenate(_hi_lo(gate_ref[0]), axis=1), jnp.concatenate([gs, gs], axis=0),
                                   preferred_element_type=F32)).T
    (_, l_s, acc_s), (_, l_w, acc_w) = stats_s, stats_w
    for pair in range(nr // 2):
        o_t = []
        for r in (2 * pair, 2 * pair + 1):
            c0 = r * N_BRANCH
            cols = slice(r * tq_n, (r + 1) * tq_n)
            o_t.append(gates[c0:c0 + 1, :] * o_c[r] + (gates[c0 + 1:c0 + 2, :] / l_s[:, cols]) * acc_s[:, cols]
                       + (gates[c0 + 2:c0 + 3, :] / l_w[:, cols]) * acc_w[:, cols])
        o_ref[0, :, pair * LANES:(pair + 1) * LANES] = jnp.concatenate(o_t, axis=0).T


def _nsa(slopes, proj_nsa, kvc, proj_gla):
    bsz, _, s, dh = proj_nsa.shape
    g_n, nr = NSA_KV_HEADS, NSA_Q_PER_KV
    tq = NSA_TQ
    tk = NSA_TK
    assert tq == tk and WINDOW % tk == 0 and 2 * dh == LANES
    ncp = kvc.shape[2]
    kv0 = NSA_HEADS
    nq = nr * tq
    nbuf = WINDOW // tk + 2

    def kv_spec(i):
        return pl.BlockSpec((1, 1, s, dh), lambda b, g, q, i=i: (b, kv0 + i * g_n + g, 0, 0))

    return pl.pallas_call(
        _nsa_kernel,
        grid=(bsz, g_n, s // tq),
        in_specs=[pl.BlockSpec(memory_space=pltpu.SMEM),
                  pl.BlockSpec((1, nr, tq, dh), lambda b, g, q: (b, g, q, 0)),
                  pl.BlockSpec((1, 1, ncp, dh), lambda b, g, q: (b, g, 0, 0)),
                  pl.BlockSpec((1, 1, ncp, dh), lambda b, g, q: (b, g_n + g, 0, 0)),
                  kv_spec(2), kv_spec(3), kv_spec(4), kv_spec(5),
                  pl.BlockSpec((1, tq, LANES), lambda b, g, q: (b, q, GLA_MISC_OFF // LANES))],
        out_specs=pl.BlockSpec((1, tq, nr * dh), lambda b, g, q: (b, q, g)),
        out_shape=jax.ShapeDtypeStruct((bsz, s, NSA_HEADS * dh), F32),
        scratch_shapes=[pltpu.VMEM((s, LANES), BF16), pltpu.VMEM((s // tk, dh, tk), BF16)] * 2
        + [pltpu.VMEM((LANES, nq), BF16)]
        + [pltpu.VMEM((1, nq), F32), pltpu.VMEM((1, nq), F32), pltpu.VMEM((dh, nq), F32)] * 2
        + [pltpu.VMEM((tk, nq), F32)] * nbuf + [pltpu.VMEM((tk, nq), BF16)] * nbuf + [pltpu.VMEM((1, nq), F32)] * nbuf,
        compiler_params=_cparams(("parallel", "parallel", "arbitrary")),
        name="nsa_attention",
    )(slopes, proj_nsa, kvc, kvc, proj_nsa, proj_nsa, proj_nsa, proj_nsa, proj_gla)


def _gla_kernel(q_ref, k_ref, v_ref, og_ref, lr_ref, wg_ref, bg_ref, nw_ref, o_ref, st_scr, la_scr, b_scr):
    rows_n = q_ref.shape[1]
    c_n, sub = GLA_CHUNK, GLA_SUB
    nh, dk, dv = GLA_HEADS, GLA_DK, GLA_DV

    @pl.when(pl.program_id(1) == 0)
    def _():
        st_scr[...] = jnp.zeros(st_scr.shape, F32)

    z = jnp.dot(lr_ref[0], wg_ref[...], precision=HI, preferred_element_type=F32) + bg_ref[...]
    la_scr[...] = (jnp.minimum(z, 0.0) - jnp.log1p(jnp.exp(-jnp.abs(z)))) * (1.0 / GLA_GATE_NORM)
    tril = jnp.where(lax.broadcasted_iota(I32, (c_n, c_n), 0) >= lax.broadcasted_iota(I32, (c_n, c_n), 1), 1.0, 0.0)
    row_c = lax.broadcasted_iota(I32, (c_n, 1), 0)
    row_s = lax.broadcasted_iota(I32, (sub, 1), 0)
    lane_c = lax.broadcasted_iota(I32, (1, c_n), 1)
    nw = nw_ref[...]
    hk = [slice(h * dk, (h + 1) * dk) for h in range(nh)]
    hv = [slice(h * dv, (h + 1) * dv) for h in range(nh)]

    def chunk(c, carry):
        r0 = pl.multiple_of(c * c_n, c_n)
        rows = pl.ds(r0, c_n)
        qc = q_ref[0, rows, :] * (dk ** -0.5)
        kc = k_ref[0, rows, :]
        vc = [v_ref[0, rows, hv[h]].astype(BF16) for h in range(nh)]
        b = jnp.dot(tril, la_scr[rows, :], precision=HI, preferred_element_type=F32)
        b_scr[...] = b
        st = [st_scr[h] for h in range(nh)]
        q_e = (qc * jnp.exp(b)).astype(BF16)
        o = [_nt_dot(q_e[:, hk[h]], st[h].astype(BF16)) for h in range(nh)]
        strips = [[] for _ in range(nh)]
        for blk in range(c_n // sub):
            lo = blk * sub
            q_i = qc[lo:lo + sub]
            b_i = b[lo:lo + sub]
            a = [jnp.zeros((sub, c_n), F32) for _ in range(nh)]
            if blk > 0:
                b_r = b_scr[lo - 1:lo, :]
                q_d = (q_i * jnp.exp(b_i - b_r)).astype(BF16)
                k_d = (kc * jnp.exp(jnp.where(row_c < lo, b_r - b, -jnp.inf))).astype(BF16)
                a = [_nt_dot(q_d[:, hk[h]], k_d[:, hk[h]]) for h in range(nh)]
            for j in range(sub):
                b_j = b_scr[lo + j:lo + j + 1, :]
                k_j = k_ref[0, pl.ds(r0 + lo + j, 1), :]
                prod = q_i * k_j * jnp.exp(jnp.where(row_s >= j, b_i - b_j, -jnp.inf))
                for h in range(nh):
                    col = jnp.sum(prod[:, hk[h]], axis=-1, keepdims=True)
                    a[h] = jnp.where(lane_c == lo + j, col, a[h]) if blk == 0 else (
                        a[h] + jnp.where(lane_c == lo + j, col, 0.0))
            for h in range(nh):
                strips[h].append(a[h])
        for h in range(nh):
            attn = jnp.concatenate(strips[h], axis=0)
            o[h] = o[h] + jnp.dot(attn.astype(BF16), vc[h], preferred_element_type=F32)
        b_last = b_scr[c_n - 1:c_n, :]
        k_dec = (kc * jnp.exp(b_last - b)).astype(BF16)
        decay = jnp.exp(b_last)
        for h in range(nh):
            st_scr[h] = st[h] * decay[:, hk[h]] + lax.dot_general(
                vc[h], k_dec[:, hk[h]], (((0,), (0,)), ((), ())), preferred_element_type=F32)
        for h in range(nh):
            og = og_ref[0, rows, hv[h]]
            on = o[h] * lax.rsqrt(jnp.mean(o[h] * o[h], axis=-1, keepdims=True) + EPS) * nw
            o_ref[0, rows, hv[h]] = on * (og * jax.nn.sigmoid(og))
        return carry

    lax.fori_loop(0, rows_n // c_n, chunk, 0)


def _gla(proj_gla, wg_pad, bg, nw):
    bsz, s, _ = proj_gla.shape
    nh, dk, dv = GLA_HEADS, GLA_DK, GLA_DV
    ts = GLA_TS
    wk, wv = nh * dk, nh * dv
    return pl.pallas_call(
        _gla_kernel,
        grid=(bsz, s // ts),
        in_specs=[pl.BlockSpec((1, ts, wk), lambda b, i: (b, i, GLA_Q_OFF // wk)),
                  pl.BlockSpec((1, ts, wk), lambda b, i: (b, i, GLA_K_OFF // wk)),
                  pl.BlockSpec((1, ts, wv), lambda b, i: (b, i, GLA_V_OFF // wv)),
                  pl.BlockSpec((1, ts, wv), lambda b, i: (b, i, GLA_OG_OFF // wv)),
                  pl.BlockSpec((1, ts, LANES), lambda b, i: (b, i, GLA_MISC_OFF // LANES)),
                  pl.BlockSpec((LANES, wk), lambda b, i: (0, 0)),
                  pl.BlockSpec((1, wk), lambda b, i: (0, 0)),
                  pl.BlockSpec((1, dv), lambda b, i: (0, 0))],
        out_specs=pl.BlockSpec((1, ts, wv), lambda b, i: (b, i, 0)),
        out_shape=jax.ShapeDtypeStruct((bsz, s, wv), F32),
        scratch_shapes=[pltpu.VMEM((nh, dv, dk), F32), pltpu.VMEM((ts, wk), F32), pltpu.VMEM((GLA_CHUNK, wk), F32)],
        compiler_params=_cparams(("parallel", "arbitrary")),
        name="gla",
    )(proj_gla, proj_gla, proj_gla, proj_gla, proj_gla, wg_pad, bg, nw)


def _outproj_kernel(nsa_ref, gla_ref, x_ref, wo_ref, g1_ref, sc_ref, sh_ref, nw_ref, wr_ref, br_ref,
                    x1_ref, h_ref, route_ref):
    half = nsa_ref.shape[2]
    acc = jnp.dot(nsa_ref[0].astype(BF16), wo_ref[:half, :], preferred_element_type=F32)
    acc = acc + jnp.dot(gla_ref[0].astype(BF16), wo_ref[half:, :], preferred_element_type=F32)
    x1 = x_ref[0] + g1_ref[0, 0] * acc
    x1_ref[0] = x1
    h = _modulated_norm(x1, nw_ref[...], sc_ref[0, 0], sh_ref[0, 0])
    h_ref[0] = h
    h_hi = h.astype(BF16)
    h_lo = (h - h_hi.astype(F32)).astype(BF16)
    t = jnp.dot(h_hi, wr_ref[...], preferred_element_type=F32)
    logits = (t[:, :LANES] + t[:, LANES:] + jnp.dot(h_lo, wr_ref[:, :LANES], preferred_element_type=F32)
              + br_ref[...])
    lane = lax.broadcasted_iota(I32, (1, LANES), 1)
    ninf = -jnp.inf
    is_g = (lane >= N_EXPERTS) & (lane < N_EXPERTS + N_GROUPS)
    gl = jnp.where(is_g, logits, ninf)
    ge = jnp.exp(gl - jnp.max(gl, axis=-1, keepdims=True))
    gp = ge / jnp.sum(ge, axis=-1, keepdims=True)
    gp_max = jnp.max(gp, axis=-1, keepdims=True)
    grp = jnp.min(jnp.where((gp == gp_max) & is_g, lane - N_EXPERTS, LANES), axis=-1, keepdims=True)
    in_grp = (lane // EXPERTS_PER_GROUP == grp) & (lane < N_EXPERTS)
    el = jnp.where(in_grp, logits, ninf)
    v1 = jnp.max(el, axis=-1, keepdims=True)
    i1 = jnp.min(jnp.where(el == v1, lane, LANES), axis=-1, keepdims=True)
    el2 = jnp.where(lane == i1, ninf, el)
    v2 = jnp.max(el2, axis=-1, keepdims=True)
    i2 = jnp.min(jnp.where(el2 == v2, lane, LANES), axis=-1, keepdims=True)
    e2 = jnp.exp(v2 - v1)
    den = 1.0 + e2
    w1 = gp_max * (1.0 / den)
    w2 = gp_max * (e2 / den)
    route_ref[0] = jnp.where(lane == 0, i1.astype(F32), jnp.where(lane == 1, i2.astype(F32), jnp.where(
        lane == 2, w1, jnp.where(lane == 3, w2, 0.0))))


def _outproj(o_nsa, o_gla, x, wo, mod4, nw, wr, br):
    bsz, s, d = x.shape
    tm = OUT_TM
    half = o_nsa.shape[2]

    def mod_spec(idx):
        return pl.BlockSpec((1, 1, 1, d), lambda b, i: (b, idx, 0, 0))

    row = lambda w: pl.BlockSpec((1, tm, w), lambda b, i: (b, i, 0))
    return pl.pallas_call(
        _outproj_kernel,
        grid=(bsz, s // tm),
        in_specs=[row(half), row(half), row(d),
                  pl.BlockSpec((2 * half, d), lambda b, i: (0, 0)),
                  mod_spec(2), mod_spec(4), mod_spec(3),
                  pl.BlockSpec((1, d), lambda b, i: (0, 0)),
                  pl.BlockSpec((d, 2 * LANES), lambda b, i: (0, 0)),
                  pl.BlockSpec((1, LANES), lambda b, i: (0, 0))],
        out_specs=[row(d), row(d), row(LANES)],
        out_shape=[jax.ShapeDtypeStruct((bsz, s, d), F32), jax.ShapeDtypeStruct((bsz, s, d), F32),
                   jax.ShapeDtypeStruct((bsz, s, LANES), F32)],
        compiler_params=_cparams(("parallel", "parallel")),
        name="outproj_router",
    )(o_nsa, o_gla, x, wo, mod4, mod4, mod4, nw, wr, br)


def _rank_kernel(route_ref, dest_ref, meta_ref, rank_scr):
    n = route_ref.shape[0]
    tm = RANK_TM
    lane_i = lax.broadcasted_iota(I32, (1, LANES), 1)
    lane = lane_i.astype(F32)
    strict = jnp.where(lax.broadcasted_iota(I32, (tm, tm), 0) > lax.broadcasted_iota(I32, (tm, tm), 1),
                       1.0, 0.0).astype(BF16)

    def two_lanes(a, b):
        return jnp.where(lane_i == 0, a, jnp.where(lane_i == 1, b, 0.0))

    def pick(e, table):
        return jnp.sum(jnp.where(lane == e, table, 0.0), axis=-1, keepdims=True)

    def count(i, seen):
        r0 = pl.multiple_of(i * tm, tm)
        rt = route_ref[pl.ds(r0, tm), :]
        e1, e2 = rt[:, 0:1], rt[:, 1:2]
        member = jnp.where(lane == e1, 1.0, jnp.where(lane == e2, 1.0, 0.0))
        before = jnp.dot(strict, member.astype(BF16), preferred_element_type=F32) + seen
        rank_scr[pl.ds(r0, tm), :] = two_lanes(pick(e1, before), pick(e2, before))
        return seen + jnp.sum(member, axis=0, keepdims=True)

    counts = lax.fori_loop(0, n // tm, count, jnp.zeros((1, LANES), F32))
    ntile = jnp.floor((counts + (MOE_TB - 1)) * (1.0 / MOE_TB))
    incl = jnp.where(lax.broadcasted_iota(I32, (LANES, LANES), 0) <= lax.broadcasted_iota(I32, (LANES, LANES), 1),
                     1.0, 0.0).astype(BF16)
    tile_end = jnp.dot(jnp.broadcast_to(ntile, (8, LANES)).astype(BF16), incl,
                       preferred_element_type=F32)[0:1]
    row_start = (tile_end - ntile) * MOE_TB

    def place(i, carry):
        r0 = pl.multiple_of(i * tm, tm)
        rt = route_ref[pl.ds(r0, tm), :]
        rk = rank_scr[pl.ds(r0, tm), :]
        d1 = pick(rt[:, 0:1], row_start) + rk[:, 0:1]
        d2 = pick(rt[:, 1:2], row_start) + rk[:, 1:2]
        dest_ref[pl.ds(r0, tm), :] = two_lanes(d1, d2).astype(I32)
        return carry

    lax.fori_loop(0, n // tm, place, 0)
    trow = lax.broadcasted_iota(I32, (meta_ref.shape[0], 1), 0).astype(F32)
    texp = jnp.sum(jnp.where((tile_end <= trow) & (lane_i < N_EXPERTS), 1.0, 0.0), axis=-1, keepdims=True)
    texp = jnp.minimum(texp, N_EXPERTS - 1.0)
    used = pick(N_EXPERTS - 1.0, tile_end)
    diag = lax.broadcasted_iota(I32, (meta_ref.shape[0], LANES), 0) == lane_i
    end_rows = jnp.sum(jnp.where(diag, tile_end, 0.0), axis=-1, keepdims=True)
    ntile_rows = jnp.sum(jnp.where(diag, ntile, 0.0), axis=-1, keepdims=True)
    meta_ref[...] = jnp.where(lane_i == 2, end_rows, jnp.where(lane_i == 3, ntile_rows, two_lanes(
        texp, jnp.broadcast_to(used, texp.shape)))).astype(I32)


def _rank(route):
    n = route.shape[0]
    return pl.pallas_call(
        _rank_kernel,
        out_shape=[jax.ShapeDtypeStruct((n, LANES), I32), jax.ShapeDtypeStruct((LANES, LANES), I32)],
        scratch_shapes=[pltpu.VMEM((n, LANES), F32)],
        compiler_params=pltpu.CompilerParams(vmem_limit_bytes=VMEM_LIMIT),
        name="moe_rank",
    )(route)


def _dispatch_kernel(dest_ref, ends_ref, h_ref, xs_ref, zero_scr, sem, zsem):
    i = pl.program_id(0)
    tm = h_ref.shape[0]
    tb = zero_scr.shape[0]

    @pl.when(i == 0)
    def _():
        zero_scr[...] = jnp.zeros(zero_scr.shape, F32)

        def zero_copy(e):
            r0 = pl.multiple_of((ends_ref[e] - 1) * tb, tb)
            return pltpu.make_async_copy(zero_scr, xs_ref.at[pl.ds(r0, tb)], zsem)

        def start(e, carry):
            @pl.when(ends_ref[N_EXPERTS + e] > 0)
            def _():
                zero_copy(e).start()
            return carry

        def wait(e, carry):
            @pl.when(ends_ref[N_EXPERTS + e] > 0)
            def _():
                zero_copy(e).wait()
            return carry

        def tail_copy(t):
            return pltpu.make_async_copy(zero_scr, xs_ref.at[pl.ds(pl.multiple_of(t * tb, tb), tb)], zsem)

        def tail_start(t, carry):
            tail_copy(t).start()
            return carry

        def tail_wait(t, carry):
            tail_copy(t).wait()
            return carry

        used = ends_ref[2 * N_EXPERTS]
        lax.fori_loop(0, N_EXPERTS, start, 0)
        lax.fori_loop(used, xs_ref.shape[0] // tb, tail_start, 0)
        lax.fori_loop(0, N_EXPERTS, wait, 0)
        lax.fori_loop(used, xs_ref.shape[0] // tb, tail_wait, 0)

    def issue(r, carry):
        p = (i * tm + r) * TOP_K
        for k in range(TOP_K):
            pltpu.make_async_copy(h_ref.at[pl.ds(r, 1)], xs_ref.at[pl.ds(dest_ref[p + k], 1)], sem).start()
        return carry

    lax.fori_loop(0, tm, issue, 0)
    for k in range(TOP_K):
        pltpu.make_async_copy(h_ref, xs_ref.at[pl.ds(0, tm)], sem).wait()


def _dispatch(dest_flat, ends_flat, h, cap):
    n, d = h.shape
    tm = DISPATCH_TM
    return pl.pallas_call(
        _dispatch_kernel,
        grid_spec=pltpu.PrefetchScalarGridSpec(
            num_scalar_prefetch=2, grid=(n // tm,),
            in_specs=[pl.BlockSpec((tm, d), lambda i, dst, ends: (i, 0))],
            out_specs=pl.BlockSpec(memory_space=pl.ANY),
            scratch_shapes=[pltpu.VMEM((MOE_TB, d), F32), pltpu.SemaphoreType.DMA(()), pltpu.SemaphoreType.DMA(())]),
        out_shape=jax.ShapeDtypeStruct((cap, d), F32),
        compiler_params=_cparams(("arbitrary",)),
        name="moe_dispatch",
    )(dest_flat, ends_flat, h)


def _ffn_kernel(meta_ref, x_ref, wg_ref, wu_ref, wd_ref, y_ref, wgb, wub, wdb):
    t = pl.program_id(0)
    ntile = pl.num_programs(0)
    e = meta_ref[t]
    e_prev = meta_ref[jnp.maximum(t - 1, 0)]
    active = t < meta_ref[ntile]

    @pl.when(active & ((t == 0) | (e != e_prev)))
    def _():
        wgb[...] = wg_ref[0].astype(BF16)
        wub[...] = wu_ref[0].astype(BF16)
        wdb[...] = wd_ref[0].astype(BF16)

    @pl.when(active)
    def _():
        x = x_ref[...].astype(BF16)
        gate = jnp.dot(x, wgb[...], preferred_element_type=F32)
        up = jnp.dot(x, wub[...], preferred_element_type=F32)
        act = (gate * jax.nn.sigmoid(gate)) * up
        y_ref[...] = jnp.dot(act.astype(BF16), wdb[...], preferred_element_type=F32)

    @pl.when(jnp.logical_not(active))
    def _():
        y_ref[...] = jnp.zeros(y_ref.shape, F32)


def _ffn(meta_flat, xs, wg, wu, wd):
    cap, d = xs.shape
    ff = wg.shape[2]
    tb = MOE_TB
    ntile = cap // tb
    return pl.pallas_call(
        _ffn_kernel,
        grid_spec=pltpu.PrefetchScalarGridSpec(
            num_scalar_prefetch=1, grid=(ntile,),
            in_specs=[pl.BlockSpec((tb, d), lambda t, m: (jnp.minimum(t, m[ntile] - 1), 0)),
                      pl.BlockSpec((1, d, ff), lambda t, m: (m[t], 0, 0)),
                      pl.BlockSpec((1, d, ff), lambda t, m: (m[t], 0, 0)),
                      pl.BlockSpec((1, ff, d), lambda t, m: (m[t], 0, 0))],
            out_specs=pl.BlockSpec((tb, d), lambda t, m: (t, 0)),
            scratch_shapes=[pltpu.VMEM((d, ff), BF16), pltpu.VMEM((d, ff), BF16), pltpu.VMEM((ff, d), BF16)]),
        out_shape=jax.ShapeDtypeStruct((cap, d), F32),
        compiler_params=_cparams(("arbitrary",)),
        name="moe_ffn",
    )(meta_flat, xs, wg, wu, wd)


def _combine_kernel(dest_ref, y_ref, x1_ref, route_ref, g2_ref, nf_ref, o_ref, ybuf, sem):
    i = pl.program_id(0)
    nstep = pl.num_programs(0)
    tm = x1_ref.shape[0]

    def issue(tile, slot):
        def body(r, carry):
            p = (tile * tm + r) * TOP_K
            for k in range(TOP_K):
                pltpu.make_async_copy(y_ref.at[pl.ds(dest_ref[p + k], 1)], ybuf.at[slot, k, pl.ds(r, 1)],
                                      sem.at[slot]).start()
            return carry

        lax.fori_loop(0, tm, body, 0)

    @pl.when(i == 0)
    def _():
        issue(0, 0)

    @pl.when(i + 1 < nstep)
    def _():
        issue(i + 1, (i + 1) % 2)

    slot = i % 2
    for k in range(TOP_K):
        pltpu.make_async_copy(y_ref.at[pl.ds(0, tm)], ybuf.at[slot, k], sem.at[slot]).wait()
    rt = route_ref[...]
    moe = rt[:, 2:3] * ybuf[slot, 0] + rt[:, 3:4] * ybuf[slot, 1]
    xo = x1_ref[...] + g2_ref[0, 0] * moe
    o_ref[...] = xo * lax.rsqrt(jnp.mean(xo * xo, axis=-1, keepdims=True) + EPS) * nf_ref[...]


def _combine(dest_flat, y, x1, route, mod4, nf, seq):
    n, d = x1.shape
    tm = COMB_TM
    tiles_per_seq = seq // tm
    return pl.pallas_call(
        _combine_kernel,
        grid_spec=pltpu.PrefetchScalarGridSpec(
            num_scalar_prefetch=1, grid=(n // tm,),
            in_specs=[pl.BlockSpec(memory_space=pl.ANY),
                      pl.BlockSpec((tm, d), lambda i, dst: (i, 0)),
                      pl.BlockSpec((tm, LANES), lambda i, dst: (i, 0)),
                      pl.BlockSpec((1, 1, 1, d), lambda i, dst: (i // tiles_per_seq, 5, 0, 0)),
                      pl.BlockSpec((1, d), lambda i, dst: (0, 0))],
            out_specs=pl.BlockSpec((tm, d), lambda i, dst: (i, 0)),
            scratch_shapes=[pltpu.VMEM((2, TOP_K, tm, d), F32), pltpu.SemaphoreType.DMA((2,))]),
        out_shape=jax.ShapeDtypeStruct((n, d), F32),
        compiler_params=_cparams(("arbitrary",)),
        name="moe_combine",
    )(dest_flat, y, x1, route, mod4, nf)


def _alibi_slopes():
    n = NSA_HEADS
    full = jnp.asarray(2.0 ** (-8.0 * np.arange(1, n + 1) / n), dtype=F32)
    pieces, rest = [], full
    for _ in range(3):
        piece = rest.astype(BF16).astype(F32)
        pieces.append(piece)
        rest = rest - piece
    return jnp.concatenate([full] + pieces)


def _layer(x, c, w_ada, b_ada, norm1_w, w_in, cmp_pos_k, cmp_w1_k, cmp_w2_k, cmp_pos_v, cmp_w1_v, cmp_w2_v,
           gla_w_gate2, gla_b_gate, gla_norm_w, w_out, norm2_w, w_rg, b_rg, w_re, b_re, w_eg, w_eu, w_ed):
    bsz, s, d = x.shape
    dh = NSA_HEAD_DIM
    mod4 = _adaln(c, w_ada, b_ada).reshape(bsz, 6, 1, d)

    o_gate = NSA_COLS
    o_gla = o_gate + NSA_GATE_COLS
    o_lr = o_gla + 2 * GLA_HEADS * GLA_DK + 2 * GLA_HEADS * GLA_DV
    w_bf = w_in.astype(BF16)
    w_nsa = w_bf[:, :NSA_COLS]
    w_gla = jnp.concatenate(
        [w_bf[:, o_gla:o_lr], w_bf[:, o_gate:o_gla], w_bf[:, o_lr:],
         jnp.zeros((d, LANES - NSA_GATE_COLS - GLA_GATE_RANK), BF16)], axis=1)
    nw1 = norm1_w.reshape(1, d)
    proj_nsa, proj_gla = _inproj(x, mod4, nw1, w_nsa, w_gla, sc_idx=1, sh_idx=0)

    pos = jnp.stack([cmp_pos_k, cmp_pos_v]).reshape(2, 2, CMP_STRIDE * dh)
    kvc = _compress(proj_nsa, pos, jnp.stack([cmp_w1_k, cmp_w1_v]), jnp.stack([cmp_w2_k, cmp_w2_v]))
    o_nsa = _nsa(_alibi_slopes(), proj_nsa, kvc, proj_gla)

    wg_pad = jnp.zeros((LANES, GLA_HEADS * GLA_DK), F32).at[
        NSA_GATE_COLS:NSA_GATE_COLS + GLA_GATE_RANK].set(gla_w_gate2)
    o_gla_out = _gla(proj_gla, wg_pad, gla_b_gate.reshape(1, -1), gla_norm_w.reshape(1, -1))

    wr = jnp.concatenate([w_re, w_rg, jnp.zeros((d, LANES - N_EXPERTS - N_GROUPS), F32)], axis=1)
    br = jnp.concatenate([b_re, b_rg, jnp.zeros((LANES - N_EXPERTS - N_GROUPS,), F32)]).reshape(1, LANES)
    wr_hi = wr.astype(BF16)
    wr_cat = jnp.concatenate([wr_hi, (wr - wr_hi.astype(F32)).astype(BF16)], axis=1)
    x1, h2, route = _outproj(o_nsa, o_gla_out, x, w_out.astype(BF16), mod4, norm2_w.reshape(1, d), wr_cat, br)

    n = bsz * s
    npair = n * TOP_K
    cap = npair + N_EXPERTS * MOE_TB
    ntile = cap // MOE_TB
    route2 = route.reshape(n, LANES)
    dest, meta = _rank(route2)
    dest_flat = dest[:, :TOP_K].reshape(npair)
    meta_flat = jnp.concatenate([meta[:ntile, 0], meta[:1, 1]])
    ends_flat = jnp.concatenate([meta[:N_EXPERTS, 2], meta[:N_EXPERTS, 3], meta[:1, 1]])
    xs = _dispatch(dest_flat, ends_flat, h2.reshape(n, d), cap)
    y = _ffn(meta_flat, xs, w_eg, w_eu, w_ed)
    return x1.reshape(n, d), y, dest_flat, route2, mod4


def kernel(x, c, w_ada, b_ada, norm1_w, w_in, cmp_pos_k, cmp_w1_k, cmp_w2_k, cmp_pos_v, cmp_w1_v, cmp_w2_v,
           gla_w_gate2, gla_b_gate, gla_norm_w, w_out, norm2_w, w_router_group, b_router_group, w_router_expert,
           b_router_expert, w_expert_gate, w_expert_up, w_expert_down, norm_f_w):
    bsz, s, d = x.shape
    assert w_ada.shape[0] == 1, "single layer"
    x1, y, dest_flat, route2, mod4 = _layer(
        x, c, w_ada[0], b_ada[0], norm1_w[0], w_in[0], cmp_pos_k[0], cmp_w1_k[0], cmp_w2_k[0], cmp_pos_v[0],
        cmp_w1_v[0], cmp_w2_v[0], gla_w_gate2[0], gla_b_gate[0], gla_norm_w[0], w_out[0], norm2_w[0],
        w_router_group[0], b_router_group[0], w_router_expert[0], b_router_expert[0],
        w_expert_gate[0], w_expert_up[0], w_expert_down[0])
    out = _combine(dest_flat, y, x1, route2, mod4, norm_f_w.reshape(1, d), s)
    return out.reshape(bsz, s, d)
```

```python
import functools

import numpy as np
import jax
import jax.numpy as jnp
from jax import lax
from jax.experimental import pallas as pl
from jax.experimental.pallas import tpu as pltpu

F32 = jnp.float32
BF16 = jnp.bfloat16
I32 = jnp.int32
HI = lax.Precision.HIGHEST

D_MODEL = 2048
NSA_HEAD_DIM = 64
NSA_HEADS = 16
NSA_KV_HEADS = 4
NSA_Q_PER_KV = 4
CMP_BLOCK = 32
CMP_STRIDE = 16
SEL_BLOCK = 64
N_SEL = 16
WINDOW = 512
N_BRANCH = 3
GLA_HEADS = 4
GLA_DV = 256
GLA_DK = 128
GLA_GATE_RANK = 16
GLA_GATE_NORM = 16.0
GLA_CHUNK = 64
GLA_SUB = 16
N_GROUPS = 4
EXPERTS_PER_GROUP = 8
N_EXPERTS = 32
TOP_K = 2
EXPERT_FF = 512
EPS = 1e-6
NEG = -1e30
FORCE = 1e30
LOG2E = 1.4426950408889634
MASKED = 2.0 ** 100

NSA_Q_COLS = NSA_HEADS * NSA_HEAD_DIM
NSA_KV_COLS = 2 * N_BRANCH * NSA_KV_HEADS * NSA_HEAD_DIM
NSA_GATE_COLS = N_BRANCH * NSA_HEADS
NSA_COLS = NSA_Q_COLS + NSA_KV_COLS
NSA_SLOTS = NSA_COLS // NSA_HEAD_DIM
GLA_Q_OFF = 0
GLA_K_OFF = GLA_HEADS * GLA_DK
GLA_V_OFF = 2 * GLA_HEADS * GLA_DK
GLA_OG_OFF = GLA_V_OFF + GLA_HEADS * GLA_DV
GLA_MISC_OFF = GLA_OG_OFF + GLA_HEADS * GLA_DV
LANES = 128
GLA_COLS = GLA_MISC_OFF + LANES

VMEM_LIMIT = 56 * 1024 * 1024

ADA_TN = 768
PREP_TR = 256
INPROJ_TM = 256
INPROJ_TN = 512
NSA_TQ = 256
NSA_TK = 256
GLA_TS = 512
OUT_TM = 256
RANK_TM = 256
MOE_TB = 256
DISPATCH_TM = 256
COMB_TM = 256


def _cparams(sem):
    return pltpu.CompilerParams(dimension_semantics=sem, vmem_limit_bytes=VMEM_LIMIT)


def _adaln_kernel(ct_ref, w_ref, b_ref, o_ref, s_scr):
    nb = ct_ref.shape[1]
    kdim, tn = w_ref.shape

    @pl.when(pl.program_id(0) == 0)
    def _():
        ct = ct_ref[...]
        s = ct * jax.nn.sigmoid(ct)
        for b in range(nb):
            s_scr[b] = jnp.broadcast_to(s[:, b:b + 1], (kdim, LANES))

    def body(k, accs):
        r = pl.multiple_of(k * 8, 8)
        w8 = w_ref[pl.ds(r, 8), :]
        out = []
        for b, acc in enumerate(accs):
            s8 = s_scr[b, pl.ds(r, 8), :]
            out.append(acc + w8 * jnp.concatenate([s8] * (tn // LANES), axis=1))
        return tuple(out)

    accs = lax.fori_loop(0, kdim // 8, body, tuple(jnp.zeros((8, tn), F32) for _ in range(nb)), unroll=2)
    bias = b_ref[...]
    for b, acc in enumerate(accs):
        o_ref[b:b + 1, :] = jnp.sum(acc, axis=0, keepdims=True) + bias


def _adaln(c, w, b):
    nb, d = c.shape
    n = w.shape[1]
    return pl.pallas_call(
        _adaln_kernel,
        grid=(n // ADA_TN,),
        in_specs=[pl.BlockSpec((d, nb), lambda j: (0, 0)),
                  pl.BlockSpec((d, ADA_TN), lambda j: (0, j)),
                  pl.BlockSpec((1, ADA_TN), lambda j: (0, j))],
        out_specs=pl.BlockSpec((nb, ADA_TN), lambda j: (0, j)),
        out_shape=jax.ShapeDtypeStruct((nb, n), F32),
        scratch_shapes=[pltpu.VMEM((nb, d, LANES), F32)],
        compiler_params=_cparams(("arbitrary",)),
        name="adaln",
    )(c.T, w, b.reshape(1, n))


def _modulated_norm(x, nw, sc, sh):
    ms = jnp.mean(x * x, axis=-1, keepdims=True)
    h = x * lax.rsqrt(ms + EPS) * nw
    return h * (1.0 + sc) + sh


def _prep_w_in_kernel(w_ref, wn_ref, wg_ref, *, o_gate, o_gla, o_lr):
    w = w_ref[...]
    wn_ref[...] = w[:, :o_gate].astype(BF16)
    pad = wg_ref.shape[1] - (w.shape[1] - o_gate)
    wg_ref[...] = jnp.concatenate(
        [w[:, o_gla:o_lr], w[:, o_gate:o_gla], w[:, o_lr:], jnp.zeros((w.shape[0], pad), F32)], axis=1).astype(BF16)


def _prep_w_in(w_in, o_gate, o_gla, o_lr):
    d, n = w_in.shape
    tr = PREP_TR
    return pl.pallas_call(
        functools.partial(_prep_w_in_kernel, o_gate=o_gate, o_gla=o_gla, o_lr=o_lr),
        grid=(d // tr,),
        in_specs=[pl.BlockSpec((tr, n), lambda i: (i, 0))],
        out_specs=[pl.BlockSpec((tr, NSA_COLS), lambda i: (i, 0)), pl.BlockSpec((tr, GLA_COLS), lambda i: (i, 0))],
        out_shape=[jax.ShapeDtypeStruct((d, NSA_COLS), BF16), jax.ShapeDtypeStruct((d, GLA_COLS), BF16)],
        compiler_params=_cparams(("parallel",)),
        name="prep_w_in",
    )(w_in)


def _inproj_kernel(x_ref, sc_ref, sh_ref, nw_ref, wn_ref, wg_ref, on_ref, og_ref):
    h = _modulated_norm(x_ref[0], nw_ref[...], sc_ref[0, 0], sh_ref[0, 0]).astype(BF16)
    dh = NSA_HEAD_DIM
    tn = INPROJ_TN
    for c in range(wn_ref.shape[1] // tn):
        acc = jnp.dot(h, wn_ref[:, c * tn:(c + 1) * tn], preferred_element_type=F32)
        for u in range(tn // dh):
            on_ref[0, c * (tn // dh) + u] = acc[:, u * dh:(u + 1) * dh]
    og_ref[0] = jnp.dot(h, wg_ref[...], preferred_element_type=F32)


def _inproj(x, mod4, nw, w_nsa, w_gla, *, sc_idx, sh_idx):
    bsz, s, d = x.shape
    n_nsa, n_gla = w_nsa.shape[1], w_gla.shape[1]
    tm = INPROJ_TM
    dh = NSA_HEAD_DIM
    return pl.pallas_call(
        _inproj_kernel,
        grid=(bsz, s // tm),
        in_specs=[pl.BlockSpec((1, tm, d), lambda b, i: (b, i, 0)),
                  pl.BlockSpec((1, 1, 1, d), lambda b, i: (b, sc_idx, 0, 0)),
                  pl.BlockSpec((1, 1, 1, d), lambda b, i: (b, sh_idx, 0, 0)),
                  pl.BlockSpec((1, d), lambda b, i: (0, 0)),
                  pl.BlockSpec((d, n_nsa), lambda b, i: (0, 0), pipeline_mode=pl.Buffered(1)),
                  pl.BlockSpec((d, n_gla), lambda b, i: (0, 0), pipeline_mode=pl.Buffered(1))],
        out_specs=[pl.BlockSpec((1, n_nsa // dh, tm, dh), lambda b, i: (b, 0, i, 0)),
                   pl.BlockSpec((1, tm, n_gla), lambda b, i: (b, i, 0))],
        out_shape=[jax.ShapeDtypeStruct((bsz, n_nsa // dh, s, dh), F32),
                   jax.ShapeDtypeStruct((bsz, s, n_gla), F32)],
        compiler_params=_cparams(("parallel", "parallel")),
        name="inproj",
    )(x, mod4, mod4, nw, w_nsa, w_gla)


def _compress_kernel(a_ref, pos_ref, w1_ref, w2_ref, o_ref):
    nch = a_ref.shape[2] // CMP_STRIDE
    a = jnp.concatenate([a_ref[0, 0, pl.ds(t, nch, stride=CMP_STRIDE), :] for t in range(CMP_STRIDE)], axis=1)
    pos = pos_ref[0]
    half = a.shape[1]
    y1 = jnp.dot(a + pos[0:1], w1_ref[0, :half, :], precision=HI, preferred_element_type=F32)
    y2 = jnp.dot(a + pos[1:2], w1_ref[0, half:, :], precision=HI, preferred_element_type=F32)
    nrow = a.shape[0]
    h = y1 + pltpu.roll(y2, nrow - 1, axis=0)
    out = jnp.dot(jax.nn.gelu(h), w2_ref[0], precision=HI, preferred_element_type=F32)
    row = lax.broadcasted_iota(I32, out.shape, 0)
    o_ref[0, 0] = jnp.where(row < nrow - 1, out, 0.0)


def _compress(proj_nsa, pos, w1, w2):
    bsz, _, s, dh = proj_nsa.shape
    nslot = 2 * NSA_KV_HEADS
    nch = s // CMP_STRIDE
    cw = CMP_STRIDE * dh
    return pl.pallas_call(
        _compress_kernel,
        grid=(bsz, nslot),
        in_specs=[pl.BlockSpec((1, 1, s, dh), lambda b, t: (b, NSA_HEADS + t, 0, 0)),
                  pl.BlockSpec((1, 2, cw), lambda b, t: (t // NSA_KV_HEADS, 0, 0)),
                  pl.BlockSpec((1, 2 * cw, dh), lambda b, t: (t // NSA_KV_HEADS, 0, 0)),
                  pl.BlockSpec((1, dh, dh), lambda b, t: (t // NSA_KV_HEADS, 0, 0))],
        out_specs=pl.BlockSpec((1, 1, nch, dh), lambda b, t: (b, t, 0, 0)),
        out_shape=jax.ShapeDtypeStruct((bsz, nslot, nch, dh), F32),
        compiler_params=_cparams(("parallel", "parallel")),
        name="nsa_compress",
    )(proj_nsa, pos, w1, w2)


def _hi_lo(x):
    hi = x.astype(BF16)
    return hi, (x - hi.astype(F32)).astype(BF16)


def _nt_dot(a, b, **kw):
    return lax.dot_general(a, b, (((1,), (1,)), ((), ())), preferred_element_type=F32, **kw)


def _nsa_kernel(slopes_ref, q_ref, kc_ref, vc_ref, ks_ref, vs_ref, kw_ref, vw_ref, gate_ref, o_ref,
                ksb, vst, kwb, vwt, q4_scr, *scr):
    g = pl.program_id(1)
    qi = pl.program_id(2)
    tq_n = q_ref.shape[2]
    dh = NSA_HEAD_DIM
    nr = NSA_Q_PER_KV
    seq = ks_ref.shape[2]
    tk_n = NSA_TK
    nb = seq // SEL_BLOCK

    @pl.when(qi == 0)
    def _():
        row = lax.broadcasted_iota(I32, (seq, dh), 0)
        lane = lax.broadcasted_iota(I32, (seq, dh), 1)
        blk = row // SEL_BLOCK
        pos = jnp.where((lane >= nb) & (lane < nb + 3), (blk * SEL_BLOCK).astype(F32),
                        jnp.where((lane >= nb + 3) & (lane < nb + 6), (row % SEL_BLOCK).astype(F32), 0.0))
        ksb[...] = jnp.concatenate([jnp.where(lane == blk, -MASKED, pos), ks_ref[0, 0]], axis=1).astype(BF16)
        kwb[...] = jnp.concatenate([pos, kw_ref[0, 0]], axis=1).astype(BF16)
        for c in range(seq // tk_n):
            rows = slice(c * tk_n, (c + 1) * tk_n)
            for src, dst in ((vs_ref, vst), (vw_ref, vwt)):
                v = src[0, 0, rows, :]
                dst[c] = jnp.concatenate([v, v], axis=1).T[:dh].astype(BF16)

    t0 = qi * tq_n
    tq = t0 + lax.broadcasted_iota(I32, (1, tq_n), 1)
    tq_f = tq.astype(F32)
    slopes = [slopes_ref[g * nr + r] for r in range(nr)]
    scale = dh ** -0.5
    q_t = []
    for pair in range(nr // 2):
        both = jnp.concatenate([q_ref[0, 2 * pair], q_ref[0, 2 * pair + 1]], axis=1).T * scale
        q_t += [both[:dh], both[dh:]]

    ncp = kc_ref.shape[2]
    kc = kc_ref[0, 0]
    vc = vc_ref[0, 0]
    vc_t = jnp.concatenate([vc, vc], axis=1).T[:dh].astype(BF16)
    n_sub = lax.broadcasted_iota(I32, (ncp, 1), 0)
    blk_end = n_sub * CMP_STRIDE + (CMP_BLOCK - 1)
    center = n_sub.astype(F32) * CMP_STRIDE + (CMP_BLOCK - 1) / 2.0
    valid_c = blk_end <= tq
    dist_c = tq_f - center
    psum = jnp.zeros((ncp, tq_n), F32)
    o_c = []
    kc_hi, kc_lo = _hi_lo(kc)
    kc_cat = jnp.concatenate([kc_hi, kc_lo, kc_hi, jnp.zeros_like(kc_hi)], axis=1)
    for r in range(nr):
        q_hi, q_lo = _hi_lo(q_t[r])
        q_cat = jnp.concatenate([q_hi, q_hi, q_lo, jnp.zeros_like(q_hi)], axis=0)
        s = jnp.dot(kc_cat, q_cat, preferred_element_type=F32)
        s = jnp.where(valid_c, s - slopes[r] * dist_c, NEG)
        e = jnp.exp(s - jnp.max(s, axis=0, keepdims=True))
        p = jnp.where(valid_c, e / jnp.sum(e, axis=0, keepdims=True), 0.0)
        psum = psum + p
        o_c.append(jnp.dot(vc_t, p.astype(BF16), preferred_element_type=F32))

    rowj = lax.broadcasted_iota(I32, (LANES, ncp), 0) * SEL_BLOCK
    coln = lax.broadcasted_iota(I32, (LANES, ncp), 1) * CMP_STRIDE
    overlap = jnp.where((coln < rowj + SEL_BLOCK) & (coln + CMP_BLOCK > rowj)
                        & (coln < (ncp - 1) * CMP_STRIDE) & (rowj < nb * SEL_BLOCK), 1.0, 0.0)
    ov = overlap.astype(BF16)
    imp = jnp.dot(jnp.concatenate([ov, ov], axis=1), jnp.concatenate(_hi_lo(psum), axis=0),
                  preferred_element_type=F32)[:nb]
    j_sub = lax.broadcasted_iota(I32, (nb, 1), 0)
    qblk = tq // SEL_BLOCK
    forced = (j_sub == 0) | (j_sub == qblk) | (j_sub == qblk - 1)
    imp = jnp.where(forced, FORCE, jnp.where(j_sub <= qblk, imp, NEG))
    cnt = jnp.zeros((nb, tq_n), F32)
    for i in range(nb):
        ci = imp[i:i + 1, :]
        tie = jnp.where(j_sub > i, 1.0, 0.0)
        cnt = cnt + jnp.where(ci > imp, 1.0, jnp.where(ci == imp, tie, 0.0))
    notsel = jnp.where(cnt < float(min(N_SEL, nb)), 0.0, 1.0)

    sub_h = lax.broadcasted_iota(I32, (dh - nb, 1), 0)
    for r in range(nr):
        scol = jnp.zeros((dh - nb, 1), F32)
        for i in range(3):
            piece = slopes_ref[(i + 1) * NSA_HEADS + g * nr + r]
            scol = jnp.where((sub_h == i) | (sub_h == 3 + i), piece, scol)
        q4_scr[:, r * tq_n:(r + 1) * tq_n] = jnp.concatenate(
            [notsel, jnp.broadcast_to(scol, (dh - nb, tq_n)), q_t[r] * LOG2E], axis=0).astype(BF16)

    nwt = WINDOW // tk_n
    nbuf = nwt + 2
    stats_s, stats_w = scr[0:3], scr[3:6]
    s_buf, p_buf, a_buf = scr[6:6 + nbuf], scr[6 + nbuf:6 + 2 * nbuf], scr[6 + 2 * nbuf:6 + 3 * nbuf]
    for m_ref, l_ref, acc_ref in (stats_s, stats_w):
        m_ref[...] = jnp.full(m_ref.shape, NEG, F32)
        l_ref[...] = jnp.zeros(l_ref.shape, F32)
        acc_ref[...] = jnp.zeros(acc_ref.shape, F32)
    key_i = lax.broadcasted_iota(I32, (tk_n, LANES), 0)
    qry_j = lax.broadcasted_iota(I32, (tk_n, LANES), 1)

    def scores(k_ref, kt, buf):
        k_tile = k_ref[pl.ds(pl.multiple_of(kt * tk_n, tk_n), tk_n), :]
        s_buf[buf][...] = jnp.dot(k_tile, q4_scr[...], preferred_element_type=F32)

    def softmax(buf, keep_fn, stats):
        m_ref, l_ref, _ = stats
        for cb in range(nr * tq_n // LANES):
            cols = slice(cb * LANES, (cb + 1) * LANES)
            s = s_buf[buf][:, cols]
            if keep_fn is not None:
                s = jnp.where(keep_fn(qry_j + (cb * LANES) % tq_n), s, -MASKED)
            m_prev = m_ref[:, cols]
            m_new = jnp.maximum(m_prev, jnp.max(s, axis=0, keepdims=True))
            alpha = jnp.exp2(m_prev - m_new)
            p = jnp.exp2(s - m_new)
            l_ref[:, cols] = alpha * l_ref[:, cols] + jnp.sum(p, axis=0, keepdims=True)
            m_ref[:, cols] = m_new
            a_buf[buf][:, cols] = alpha
            p_buf[buf][:, cols] = p.astype(BF16)

    def values(vt_ref, kt, buf, stats):
        acc_ref = stats[2]
        pv = jnp.dot(vt_ref[kt], p_buf[buf][...], preferred_element_type=F32)
        acc_ref[...] = acc_ref[...] * a_buf[buf][...] + pv

    def batch(jobs):
        for buf, (k_ref, _, kt, _, _) in enumerate(jobs):
            scores(k_ref, kt, buf)
        for buf, (_, vt_ref, kt, keep_fn, stats) in enumerate(jobs):
            softmax(buf, keep_fn, stats)
            values(vt_ref, kt, buf, stats)

    size = nbuf
    while size >= 1:
        @pl.when(qi & size != 0)
        def _(size=size):
            base = qi & ~(2 * size - 1)
            batch([(ksb, vst, base + t, None, stats_s) for t in range(size)])
        size //= 2

    def causal(j):
        return key_i <= j

    jobs = []
    for back in range(nwt, 0, -1):
        off = jnp.where(qi >= back, 0, tk_n)
        keep_fn = (lambda j, off=off: key_i - off > j) if back == nwt else (lambda j, off=off: key_i >= off)
        jobs.append((kwb, vwt, jnp.maximum(qi - back, 0), keep_fn, stats_w))
    jobs.append((kwb, vwt, qi, causal, stats_w))
    jobs.append((ksb, vst, qi, causal, stats_s))
    batch(jobs)

    gsel = jnp.where(lax.broadcasted_iota(I32, (LANES, LANES), 0)
                     == lax.broadcasted_iota(I32, (LANES, LANES), 1) + g * (nr * N_BRANCH), 1.0, 0.0)
    gs = gsel.astype(BF16)
    gates = jax.nn.sigmoid(jnp.dot(jnp.concatenate(_hi_lo(gate_ref[0]), axis=1), jnp.concatenate([gs, gs], axis=0),
                                   preferred_element_type=F32)).T
    (_, l_s, acc_s), (_, l_w, acc_w) = stats_s, stats_w
    for pair in range(nr // 2):
        o_t = []
        for r in (2 * pair, 2 * pair + 1):
            c0 = r * N_BRANCH
            cols = slice(r * tq_n, (r + 1) * tq_n)
            o_t.append(gates[c0:c0 + 1, :] * o_c[r] + (gates[c0 + 1:c0 + 2, :] / l_s[:, cols]) * acc_s[:, cols]
                       + (gates[c0 + 2:c0 + 3, :] / l_w[:, cols]) * acc_w[:, cols])
        o_ref[0, :, pair * LANES:(pair + 1) * LANES] = jnp.concatenate(o_t, axis=0).T


def _nsa(slopes, proj_nsa, kvc, proj_gla):
    bsz, _, s, dh = proj_nsa.shape
    g_n, nr = NSA_KV_HEADS, NSA_Q_PER_KV
    tq = NSA_TQ
    tk = NSA_TK
    assert tq == tk and WINDOW % tk == 0 and 2 * dh == LANES
    ncp = kvc.shape[2]
    kv0 = NSA_HEADS
    nq = nr * tq
    nbuf = WINDOW // tk + 2
    assert nbuf & (nbuf - 1) == 0 and s // tq <= 2 * nbuf

    def kv_spec(i):
        return pl.BlockSpec((1, 1, s, dh), lambda b, g, q, i=i: (b, kv0 + i * g_n + g, 0, 0))

    return pl.pallas_call(
        _nsa_kernel,
        grid=(bsz, g_n, s // tq),
        in_specs=[pl.BlockSpec(memory_space=pltpu.SMEM),
                  pl.BlockSpec((1, nr, tq, dh), lambda b, g, q: (b, g, q, 0)),
                  pl.BlockSpec((1, 1, ncp, dh), lambda b, g, q: (b, g, 0, 0)),
                  pl.BlockSpec((1, 1, ncp, dh), lambda b, g, q: (b, g_n + g, 0, 0)),
                  kv_spec(2), kv_spec(3), kv_spec(4), kv_spec(5),
                  pl.BlockSpec((1, tq, LANES), lambda b, g, q: (b, q, GLA_MISC_OFF // LANES))],
        out_specs=pl.BlockSpec((1, tq, nr * dh), lambda b, g, q: (b, q, g)),
        out_shape=jax.ShapeDtypeStruct((bsz, s, NSA_HEADS * dh), F32),
        scratch_shapes=[pltpu.VMEM((s, LANES), BF16), pltpu.VMEM((s // tk, dh, tk), BF16)] * 2
        + [pltpu.VMEM((LANES, nq), BF16)]
        + [pltpu.VMEM((1, nq), F32), pltpu.VMEM((1, nq), F32), pltpu.VMEM((dh, nq), F32)] * 2
        + [pltpu.VMEM((tk, nq), F32)] * nbuf + [pltpu.VMEM((tk, nq), BF16)] * nbuf + [pltpu.VMEM((1, nq), F32)] * nbuf,
        compiler_params=_cparams(("parallel", "parallel", "arbitrary")),
        name="nsa_attention",
    )(slopes, proj_nsa, kvc, kvc, proj_nsa, proj_nsa, proj_nsa, proj_nsa, proj_gla)


def _gla_kernel(q_ref, k_ref, v_ref, og_ref, lr_ref, wg_ref, bg_ref, nw_ref, o_ref, st_scr, la_scr, b_scr):
    rows_n = q_ref.shape[1]
    c_n, sub = GLA_CHUNK, GLA_SUB
    nh, dk, dv = GLA_HEADS, GLA_DK, GLA_DV

    @pl.when(pl.program_id(1) == 0)
    def _():
        st_scr[...] = jnp.zeros(st_scr.shape, F32)

    z = jnp.dot(lr_ref[0], wg_ref[...], precision=HI, preferred_element_type=F32) + bg_ref[...]
    la_scr[...] = (jnp.minimum(z, 0.0) - jnp.log1p(jnp.exp(-jnp.abs(z)))) * (1.0 / GLA_GATE_NORM)
    tril = jnp.where(lax.broadcasted_iota(I32, (c_n, c_n), 0) >= lax.broadcasted_iota(I32, (c_n, c_n), 1), 1.0, 0.0)
    row_c = lax.broadcasted_iota(I32, (c_n, 1), 0)
    row_s = lax.broadcasted_iota(I32, (sub, 1), 0)
    lane_c = lax.broadcasted_iota(I32, (1, c_n), 1)
    nw = nw_ref[...]
    hk = [slice(h * dk, (h + 1) * dk) for h in range(nh)]
    hv = [slice(h * dv, (h + 1) * dv) for h in range(nh)]

    def chunk(c, carry):
        r0 = pl.multiple_of(c * c_n, c_n)
        rows = pl.ds(r0, c_n)
        qc = q_ref[0, rows, :] * (dk ** -0.5)
        kc = k_ref[0, rows, :]
        vc = [v_ref[0, rows, hv[h]].astype(BF16) for h in range(nh)]
        b = jnp.dot(tril, la_scr[rows, :], precision=HI, preferred_element_type=F32)
        b_scr[...] = b
        st = [st_scr[h] for h in range(nh)]
        q_e = (qc * jnp.exp(b)).astype(BF16)
        o = [_nt_dot(q_e[:, hk[h]], st[h].astype(BF16)) for h in range(nh)]
        strips = [[] for _ in range(nh)]
        for blk in range(c_n // sub):
            lo = blk * sub
            q_i = qc[lo:lo + sub]
            b_i = b[lo:lo + sub]
            a = [jnp.zeros((sub, c_n), F32) for _ in range(nh)]
            if blk > 0:
                b_r = b_scr[lo - 1:lo, :]
                q_d = (q_i * jnp.exp(b_i - b_r)).astype(BF16)
                k_d = (kc * jnp.exp(jnp.where(row_c < lo, b_r - b, -jnp.inf))).astype(BF16)
                a = [_nt_dot(q_d[:, hk[h]], k_d[:, hk[h]]) for h in range(nh)]
            for j in range(sub):
                b_j = b_scr[lo + j:lo + j + 1, :]
                k_j = k_ref[0, pl.ds(r0 + lo + j, 1), :]
                prod = q_i * k_j * jnp.exp(jnp.where(row_s >= j, b_i - b_j, -jnp.inf))
                for h in range(nh):
                    col = jnp.sum(prod[:, hk[h]], axis=-1, keepdims=True)
                    a[h] = jnp.where(lane_c == lo + j, col, a[h]) if blk == 0 else (
                        a[h] + jnp.where(lane_c == lo + j, col, 0.0))
            for h in range(nh):
                strips[h].append(a[h])
        for h in range(nh):
            attn = jnp.concatenate(strips[h], axis=0)
            o[h] = o[h] + jnp.dot(attn.astype(BF16), vc[h], preferred_element_type=F32)
        b_last = b_scr[c_n - 1:c_n, :]
        k_dec = (kc * jnp.exp(b_last - b)).astype(BF16)
        decay = jnp.exp(b_last)
        for h in range(nh):
            st_scr[h] = st[h] * decay[:, hk[h]] + lax.dot_general(
                vc[h], k_dec[:, hk[h]], (((0,), (0,)), ((), ())), preferred_element_type=F32)
        for h in range(nh):
            og = og_ref[0, rows, hv[h]]
            on = o[h] * lax.rsqrt(jnp.mean(o[h] * o[h], axis=-1, keepdims=True) + EPS) * nw
            o_ref[0, rows, hv[h]] = on * (og * jax.nn.sigmoid(og))
        return carry

    lax.fori_loop(0, rows_n // c_n, chunk, 0)


def _gla(proj_gla, wg_pad, bg, nw):
    bsz, s, _ = proj_gla.shape
    nh, dk, dv = GLA_HEADS, GLA_DK, GLA_DV
    ts = GLA_TS
    wk, wv = nh * dk, nh * dv
    return pl.pallas_call(
        _gla_kernel,
        grid=(bsz, s // ts),
        in_specs=[pl.BlockSpec((1, ts, wk), lambda b, i: (b, i, GLA_Q_OFF // wk)),
                  pl.BlockSpec((1, ts, wk), lambda b, i: (b, i, GLA_K_OFF // wk)),
                  pl.BlockSpec((1, ts, wv), lambda b, i: (b, i, GLA_V_OFF // wv)),
                  pl.BlockSpec((1, ts, wv), lambda b, i: (b, i, GLA_OG_OFF // wv)),
                  pl.BlockSpec((1, ts, LANES), lambda b, i: (b, i, GLA_MISC_OFF // LANES)),
                  pl.BlockSpec((LANES, wk), lambda b, i: (0, 0)),
                  pl.BlockSpec((1, wk), lambda b, i: (0, 0)),
                  pl.BlockSpec((1, dv), lambda b, i: (0, 0))],
        out_specs=pl.BlockSpec((1, ts, wv), lambda b, i: (b, i, 0)),
        out_shape=jax.ShapeDtypeStruct((bsz, s, wv), F32),
        scratch_shapes=[pltpu.VMEM((nh, dv, dk), F32), pltpu.VMEM((ts, wk), F32), pltpu.VMEM((GLA_CHUNK, wk), F32)],
        compiler_params=_cparams(("parallel", "arbitrary")),
        name="gla",
    )(proj_gla, proj_gla, proj_gla, proj_gla, proj_gla, wg_pad, bg, nw)


def _outproj_kernel(nsa_ref, gla_ref, x_ref, wo_ref, g1_ref, sc_ref, sh_ref, nw_ref, wr_ref, br_ref,
                    x1_ref, h_ref, route_ref):
    half = nsa_ref.shape[2]
    acc = jnp.dot(nsa_ref[0].astype(BF16), wo_ref[:half, :], preferred_element_type=F32)
    acc = acc + jnp.dot(gla_ref[0].astype(BF16), wo_ref[half:, :], preferred_element_type=F32)
    x1 = x_ref[0] + g1_ref[0, 0] * acc
    x1_ref[0] = x1
    h = _modulated_norm(x1, nw_ref[...], sc_ref[0, 0], sh_ref[0, 0])
    h_ref[0] = h
    h_hi = h.astype(BF16)
    h_lo = (h - h_hi.astype(F32)).astype(BF16)
    t = jnp.dot(h_hi, wr_ref[...], preferred_element_type=F32)
    logits = (t[:, :LANES] + t[:, LANES:] + jnp.dot(h_lo, wr_ref[:, :LANES], preferred_element_type=F32)
              + br_ref[...])
    lane = lax.broadcasted_iota(I32, (1, LANES), 1)
    ninf = -jnp.inf
    is_g = (lane >= N_EXPERTS) & (lane < N_EXPERTS + N_GROUPS)
    gl = jnp.where(is_g, logits, ninf)
    ge = jnp.exp(gl - jnp.max(gl, axis=-1, keepdims=True))
    gp = ge / jnp.sum(ge, axis=-1, keepdims=True)
    gp_max = jnp.max(gp, axis=-1, keepdims=True)
    grp = jnp.min(jnp.where((gp == gp_max) & is_g, lane - N_EXPERTS, LANES), axis=-1, keepdims=True)
    in_grp = (lane // EXPERTS_PER_GROUP == grp) & (lane < N_EXPERTS)
    el = jnp.where(in_grp, logits, ninf)
    v1 = jnp.max(el, axis=-1, keepdims=True)
    i1 = jnp.min(jnp.where(el == v1, lane, LANES), axis=-1, keepdims=True)
    el2 = jnp.where(lane == i1, ninf, el)
    v2 = jnp.max(el2, axis=-1, keepdims=True)
    i2 = jnp.min(jnp.where(el2 == v2, lane, LANES), axis=-1, keepdims=True)
    e2 = jnp.exp(v2 - v1)
    den = 1.0 + e2
    w1 = gp_max * (1.0 / den)
    w2 = gp_max * (e2 / den)
    route_ref[0] = jnp.where(lane == 0, i1.astype(F32), jnp.where(lane == 1, i2.astype(F32), jnp.where(
        lane == 2, w1, jnp.where(lane == 3, w2, 0.0))))


def _outproj(o_nsa, o_gla, x, wo, mod4, nw, wr, br):
    bsz, s, d = x.shape
    tm = OUT_TM
    half = o_nsa.shape[2]

    def mod_spec(idx):
        return pl.BlockSpec((1, 1, 1, d), lambda b, i: (b, idx, 0, 0))

    row = lambda w: pl.BlockSpec((1, tm, w), lambda b, i: (b, i, 0))
    return pl.pallas_call(
        _outproj_kernel,
        grid=(bsz, s // tm),
        in_specs=[row(half), row(half), row(d),
                  pl.BlockSpec((2 * half, d), lambda b, i: (0, 0)),
                  mod_spec(2), mod_spec(4), mod_spec(3),
                  pl.BlockSpec((1, d), lambda b, i: (0, 0)),
                  pl.BlockSpec((d, 2 * LANES), lambda b, i: (0, 0)),
                  pl.BlockSpec((1, LANES), lambda b, i: (0, 0))],
        out_specs=[row(d), row(d), row(LANES)],
        out_shape=[jax.ShapeDtypeStruct((bsz, s, d), F32), jax.ShapeDtypeStruct((bsz, s, d), F32),
                   jax.ShapeDtypeStruct((bsz, s, LANES), F32)],
        compiler_params=_cparams(("parallel", "parallel")),
        name="outproj_router",
    )(o_nsa, o_gla, x, wo, mod4, mod4, mod4, nw, wr, br)


def _rank_kernel(route_ref, dest_ref, meta_ref, rank_scr):
    n = route_ref.shape[0]
    tm = RANK_TM
    lane_i = lax.broadcasted_iota(I32, (1, LANES), 1)
    lane = lane_i.astype(F32)
    strict = jnp.where(lax.broadcasted_iota(I32, (tm, tm), 0) > lax.broadcasted_iota(I32, (tm, tm), 1),
                       1.0, 0.0).astype(BF16)

    def two_lanes(a, b):
        return jnp.where(lane_i == 0, a, jnp.where(lane_i == 1, b, 0.0))

    def pick(e, table):
        return jnp.sum(jnp.where(lane == e, table, 0.0), axis=-1, keepdims=True)

    def count(i, seen):
        r0 = pl.multiple_of(i * tm, tm)
        rt = route_ref[pl.ds(r0, tm), :]
        e1, e2 = rt[:, 0:1], rt[:, 1:2]
        member = jnp.where(lane == e1, 1.0, jnp.where(lane == e2, 1.0, 0.0))
        before = jnp.dot(strict, member.astype(BF16), preferred_element_type=F32) + seen
        rank_scr[pl.ds(r0, tm), :] = two_lanes(pick(e1, before), pick(e2, before))
        return seen + jnp.sum(member, axis=0, keepdims=True)

    counts = lax.fori_loop(0, n // tm, count, jnp.zeros((1, LANES), F32))
    ntile = jnp.floor((counts + (MOE_TB - 1)) * (1.0 / MOE_TB))
    incl = jnp.where(lax.broadcasted_iota(I32, (LANES, LANES), 0) <= lax.broadcasted_iota(I32, (LANES, LANES), 1),
                     1.0, 0.0).astype(BF16)
    tile_end = jnp.dot(jnp.broadcast_to(ntile, (8, LANES)).astype(BF16), incl,
                       preferred_element_type=F32)[0:1]
    row_start = (tile_end - ntile) * MOE_TB

    def place(i, carry):
        r0 = pl.multiple_of(i * tm, tm)
        rt = route_ref[pl.ds(r0, tm), :]
        rk = rank_scr[pl.ds(r0, tm), :]
        d1 = pick(rt[:, 0:1], row_start) + rk[:, 0:1]
        d2 = pick(rt[:, 1:2], row_start) + rk[:, 1:2]
        dest_ref[pl.ds(r0, tm), :] = two_lanes(d1, d2).astype(I32)
        return carry

    lax.fori_loop(0, n // tm, place, 0)
    trow = lax.broadcasted_iota(I32, (meta_ref.shape[0], 1), 0).astype(F32)
    texp = jnp.sum(jnp.where((tile_end <= trow) & (lane_i < N_EXPERTS), 1.0, 0.0), axis=-1, keepdims=True)
    texp = jnp.minimum(texp, N_EXPERTS - 1.0)
    used = pick(N_EXPERTS - 1.0, tile_end)
    diag = lax.broadcasted_iota(I32, (meta_ref.shape[0], LANES), 0) == lane_i
    end_rows = jnp.sum(jnp.where(diag, tile_end, 0.0), axis=-1, keepdims=True)
    ntile_rows = jnp.sum(jnp.where(diag, ntile, 0.0), axis=-1, keepdims=True)
    meta_ref[...] = jnp.where(lane_i == 2, end_rows, jnp.where(lane_i == 3, ntile_rows, two_lanes(
        texp, jnp.broadcast_to(used, texp.shape)))).astype(I32)


def _rank(route):
    n = route.shape[0]
    return pl.pallas_call(
        _rank_kernel,
        out_shape=[jax.ShapeDtypeStruct((n, LANES), I32), jax.ShapeDtypeStruct((LANES, LANES), I32)],
        scratch_shapes=[pltpu.VMEM((n, LANES), F32)],
        compiler_params=pltpu.CompilerParams(vmem_limit_bytes=VMEM_LIMIT),
        name="moe_rank",
    )(route)


def _dispatch_kernel(dest_ref, ends_ref, h_ref, xs_ref, zero_scr, sem, zsem):
    i = pl.program_id(0)
    tm = h_ref.shape[0]
    tb = zero_scr.shape[0]

    @pl.when(i == 0)
    def _():
        zero_scr[...] = jnp.zeros(zero_scr.shape, F32)

        def zero_copy(e):
            r0 = pl.multiple_of((ends_ref[e] - 1) * tb, tb)
            return pltpu.make_async_copy(zero_scr, xs_ref.at[pl.ds(r0, tb)], zsem)

        def start(e, carry):
            @pl.when(ends_ref[N_EXPERTS + e] > 0)
            def _():
                zero_copy(e).start()
            return carry

        def wait(e, carry):
            @pl.when(ends_ref[N_EXPERTS + e] > 0)
            def _():
                zero_copy(e).wait()
            return carry

        def tail_copy(t):
            return pltpu.make_async_copy(zero_scr, xs_ref.at[pl.ds(pl.multiple_of(t * tb, tb), tb)], zsem)

        def tail_start(t, carry):
            tail_copy(t).start()
            return carry

        def tail_wait(t, carry):
            tail_copy(t).wait()
            return carry

        used = ends_ref[2 * N_EXPERTS]
        lax.fori_loop(0, N_EXPERTS, start, 0)
        lax.fori_loop(used, xs_ref.shape[0] // tb, tail_start, 0)
        lax.fori_loop(0, N_EXPERTS, wait, 0)
        lax.fori_loop(used, xs_ref.shape[0] // tb, tail_wait, 0)

    def issue(r, carry):
        p = (i * tm + r) * TOP_K
        for k in range(TOP_K):
            pltpu.make_async_copy(h_ref.at[pl.ds(r, 1)], xs_ref.at[pl.ds(dest_ref[p + k], 1)], sem).start()
        return carry

    lax.fori_loop(0, tm, issue, 0)
    for k in range(TOP_K):
        pltpu.make_async_copy(h_ref, xs_ref.at[pl.ds(0, tm)], sem).wait()


def _dispatch(dest_flat, ends_flat, h, cap):
    n, d = h.shape
    tm = DISPATCH_TM
    return pl.pallas_call(
        _dispatch_kernel,
        grid_spec=pltpu.PrefetchScalarGridSpec(
            num_scalar_prefetch=2, grid=(n // tm,),
            in_specs=[pl.BlockSpec((tm, d), lambda i, dst, ends: (i, 0))],
            out_specs=pl.BlockSpec(memory_space=pl.ANY),
            scratch_shapes=[pltpu.VMEM((MOE_TB, d), F32), pltpu.SemaphoreType.DMA(()), pltpu.SemaphoreType.DMA(())]),
        out_shape=jax.ShapeDtypeStruct((cap, d), F32),
        compiler_params=_cparams(("arbitrary",)),
        name="moe_dispatch",
    )(dest_flat, ends_flat, h)


def _ffn_kernel(meta_ref, x_ref, wg_ref, wu_ref, wd_ref, y_ref, wgb, wub, wdb):
    t = pl.program_id(0)
    ntile = pl.num_programs(0)
    e = meta_ref[t]
    e_prev = meta_ref[jnp.maximum(t - 1, 0)]
    active = t < meta_ref[ntile]

    @pl.when(active & ((t == 0) | (e != e_prev)))
    def _():
        wgb[...] = wg_ref[0].astype(BF16)
        wub[...] = wu_ref[0].astype(BF16)
        wdb[...] = wd_ref[0].astype(BF16)

    @pl.when(active)
    def _():
        x = x_ref[...].astype(BF16)
        gate = jnp.dot(x, wgb[...], preferred_element_type=F32)
        up = jnp.dot(x, wub[...], preferred_element_type=F32)
        act = (gate * jax.nn.sigmoid(gate)) * up
        y_ref[...] = jnp.dot(act.astype(BF16), wdb[...], preferred_element_type=F32)

    @pl.when(jnp.logical_not(active))
    def _():
        y_ref[...] = jnp.zeros(y_ref.shape, F32)


def _ffn(meta_flat, xs, wg, wu, wd):
    cap, d = xs.shape
    ff = wg.shape[2]
    tb = MOE_TB
    ntile = cap // tb
    return pl.pallas_call(
        _ffn_kernel,
        grid_spec=pltpu.PrefetchScalarGridSpec(
            num_scalar_prefetch=1, grid=(ntile,),
            in_specs=[pl.BlockSpec((tb, d), lambda t, m: (jnp.minimum(t, m[ntile] - 1), 0)),
                      pl.BlockSpec((1, d, ff), lambda t, m: (m[t], 0, 0)),
                      pl.BlockSpec((1, d, ff), lambda t, m: (m[t], 0, 0)),
                      pl.BlockSpec((1, ff, d), lambda t, m: (m[t], 0, 0))],
            out_specs=pl.BlockSpec((tb, d), lambda t, m: (t, 0)),
            scratch_shapes=[pltpu.VMEM((d, ff), BF16), pltpu.VMEM((d, ff), BF16), pltpu.VMEM((ff, d), BF16)]),
        out_shape=jax.ShapeDtypeStruct((cap, d), F32),
        compiler_params=_cparams(("arbitrary",)),
        name="moe_ffn",
    )(meta_flat, xs, wg, wu, wd)


def _combine_kernel(dest_ref, y_ref, x1_ref, route_ref, g2_ref, nf_ref, o_ref, ybuf, sem):
    i = pl.program_id(0)
    nstep = pl.num_programs(0)
    tm = x1_ref.shape[0]

    def issue(tile, slot):
        def body(r, carry):
            p = (tile * tm + r) * TOP_K
            for k in range(TOP_K):
                pltpu.make_async_copy(y_ref.at[pl.ds(dest_ref[p + k], 1)], ybuf.at[slot, k, pl.ds(r, 1)],
                                      sem.at[slot]).start()
            return carry

        lax.fori_loop(0, tm, body, 0)

    @pl.when(i == 0)
    def _():
        issue(0, 0)

    @pl.when(i + 1 < nstep)
    def _():
        issue(i + 1, (i + 1) % 2)

    slot = i % 2
    for k in range(TOP_K):
        pltpu.make_async_copy(y_ref.at[pl.ds(0, tm)], ybuf.at[slot, k], sem.at[slot]).wait()
    rt = route_ref[...]
    moe = rt[:, 2:3] * ybuf[slot, 0] + rt[:, 3:4] * ybuf[slot, 1]
    xo = x1_ref[...] + g2_ref[0, 0] * moe
    o_ref[...] = xo * lax.rsqrt(jnp.mean(xo * xo, axis=-1, keepdims=True) + EPS) * nf_ref[...]


def _combine(dest_flat, y, x1, route, mod4, nf, seq):
    n, d = x1.shape
    tm = COMB_TM
    tiles_per_seq = seq // tm
    return pl.pallas_call(
        _combine_kernel,
        grid_spec=pltpu.PrefetchScalarGridSpec(
            num_scalar_prefetch=1, grid=(n // tm,),
            in_specs=[pl.BlockSpec(memory_space=pl.ANY),
                      pl.BlockSpec((tm, d), lambda i, dst: (i, 0)),
                      pl.BlockSpec((tm, LANES), lambda i, dst: (i, 0)),
                      pl.BlockSpec((1, 1, 1, d), lambda i, dst: (i // tiles_per_seq, 5, 0, 0)),
                      pl.BlockSpec((1, d), lambda i, dst: (0, 0))],
            out_specs=pl.BlockSpec((tm, d), lambda i, dst: (i, 0)),
            scratch_shapes=[pltpu.VMEM((2, TOP_K, tm, d), F32), pltpu.SemaphoreType.DMA((2,))]),
        out_shape=jax.ShapeDtypeStruct((n, d), F32),
        compiler_params=_cparams(("arbitrary",)),
        name="moe_combine",
    )(dest_flat, y, x1, route, mod4, nf)


def _alibi_slopes():
    n = NSA_HEADS
    full = jnp.asarray(2.0 ** (-8.0 * np.arange(1, n + 1) / n), dtype=F32)
    pieces, rest = [], full * LOG2E
    for _ in range(3):
        piece = rest.astype(BF16).astype(F32)
        pieces.append(piece)
        rest = rest - piece
    return jnp.concatenate([full] + pieces)


def _layer(x, c, w_ada, b_ada, norm1_w, w_in, cmp_pos_k, cmp_w1_k, cmp_w2_k, cmp_pos_v, cmp_w1_v, cmp_w2_v,
           gla_w_gate2, gla_b_gate, gla_norm_w, w_out, norm2_w, w_rg, b_rg, w_re, b_re, w_eg, w_eu, w_ed):
    bsz, s, d = x.shape
    dh = NSA_HEAD_DIM
    mod4 = _adaln(c, w_ada, b_ada).reshape(bsz, 6, 1, d)

    o_gate = NSA_COLS
    o_gla = o_gate + NSA_GATE_COLS
    o_lr = o_gla + 2 * GLA_HEADS * GLA_DK + 2 * GLA_HEADS * GLA_DV
    w_nsa, w_gla = _prep_w_in(w_in, o_gate, o_gla, o_lr)
    nw1 = norm1_w.reshape(1, d)
    proj_nsa, proj_gla = _inproj(x, mod4, nw1, w_nsa, w_gla, sc_idx=1, sh_idx=0)

    pos = jnp.stack([cmp_pos_k, cmp_pos_v]).reshape(2, 2, CMP_STRIDE * dh)
    kvc = _compress(proj_nsa, pos, jnp.stack([cmp_w1_k, cmp_w1_v]), jnp.stack([cmp_w2_k, cmp_w2_v]))
    o_nsa = _nsa(_alibi_slopes(), proj_nsa, kvc, proj_gla)

    wg_pad = jnp.zeros((LANES, GLA_HEADS * GLA_DK), F32).at[
        NSA_GATE_COLS:NSA_GATE_COLS + GLA_GATE_RANK].set(gla_w_gate2)
    o_gla_out = _gla(proj_gla, wg_pad, gla_b_gate.reshape(1, -1), gla_norm_w.reshape(1, -1))

    wr = jnp.concatenate([w_re, w_rg, jnp.zeros((d, LANES - N_EXPERTS - N_GROUPS), F32)], axis=1)
    br = jnp.concatenate([b_re, b_rg, jnp.zeros((LANES - N_EXPERTS - N_GROUPS,), F32)]).reshape(1, LANES)
    wr_hi = wr.astype(BF16)
    wr_cat = jnp.concatenate([wr_hi, (wr - wr_hi.astype(F32)).astype(BF16)], axis=1)
    x1, h2, route = _outproj(o_nsa, o_gla_out, x, w_out.astype(BF16), mod4, norm2_w.reshape(1, d), wr_cat, br)

    n = bsz * s
    npair = n * TOP_K
    cap = npair + N_EXPERTS * MOE_TB
    ntile = cap // MOE_TB
    route2 = route.reshape(n, LANES)
    dest, meta = _rank(route2)
    dest_flat = dest[:, :TOP_K].reshape(npair)
    meta_flat = jnp.concatenate([meta[:ntile, 0], meta[:1, 1]])
    ends_flat = jnp.concatenate([meta[:N_EXPERTS, 2], meta[:N_EXPERTS, 3], meta[:1, 1]])
    xs = _dispatch(dest_flat, ends_flat, h2.reshape(n, d), cap)
    y = _ffn(meta_flat, xs, w_eg, w_eu, w_ed)
    return x1.reshape(n, d), y, dest_flat, route2, mod4


def kernel(x, c, w_ada, b_ada, norm1_w, w_in, cmp_pos_k, cmp_w1_k, cmp_w2_k, cmp_pos_v, cmp_w1_v, cmp_w2_v,
           gla_w_gate2, gla_b_gate, gla_norm_w, w_out, norm2_w, w_router_group, b_router_group, w_router_expert,
           b_router_expert, w_expert_gate, w_expert_up, w_expert_down, norm_f_w):
    bsz, s, d = x.shape
    assert w_ada.shape[0] == 1, "single layer"
    x1, y, dest_flat, route2, mod4 = _layer(
        x, c, w_ada[0], b_ada[0], norm1_w[0], w_in[0], cmp_pos_k[0], cmp_w1_k[0], cmp_w2_k[0], cmp_pos_v[0],
        cmp_w1_v[0], cmp_w2_v[0], gla_w_gate2[0], gla_b_gate[0], gla_norm_w[0], w_out[0], norm2_w[0],
        w_router_group[0], b_router_group[0], w_router_expert[0], b_router_expert[0],
        w_expert_gate[0], w_expert_up[0], w_expert_down[0])
    out = _combine(dest_flat, y, x1, route2, mod4, norm_f_w.reshape(1, d), s)
    return out.reshape(bsz, s, d)
```

```python
import functools

import numpy as np
import jax
import jax.numpy as jnp
from jax import lax
from jax.experimental import pallas as pl
from jax.experimental.pallas import tpu as pltpu

F32 = jnp.float32
BF16 = jnp.bfloat16
I32 = jnp.int32
HI = lax.Precision.HIGHEST

D_MODEL = 2048
NSA_HEAD_DIM = 64
NSA_HEADS = 16
NSA_KV_HEADS = 4
NSA_Q_PER_KV = 4
CMP_BLOCK = 32
CMP_STRIDE = 16
SEL_BLOCK = 64
N_SEL = 16
WINDOW = 512
N_BRANCH = 3
GLA_HEADS = 4
GLA_DV = 256
GLA_DK = 128
GLA_GATE_RANK = 16
GLA_GATE_NORM = 16.0
GLA_CHUNK = 64
GLA_SUB = 16
N_GROUPS = 4
EXPERTS_PER_GROUP = 8
N_EXPERTS = 32
TOP_K = 2
EXPERT_FF = 512
EPS = 1e-6
NEG = -1e30
FORCE = 1e30
LOG2E = 1.4426950408889634
MASKED = 2.0 ** 100

NSA_Q_COLS = NSA_HEADS * NSA_HEAD_DIM
NSA_KV_COLS = 2 * N_BRANCH * NSA_KV_HEADS * NSA_HEAD_DIM
NSA_GATE_COLS = N_BRANCH * NSA_HEADS
NSA_COLS = NSA_Q_COLS + NSA_KV_COLS
NSA_SLOTS = NSA_COLS // NSA_HEAD_DIM
GLA_Q_OFF = 0
GLA_K_OFF = GLA_HEADS * GLA_DK
GLA_V_OFF = 2 * GLA_HEADS * GLA_DK
GLA_OG_OFF = GLA_V_OFF + GLA_HEADS * GLA_DV
GLA_MISC_OFF = GLA_OG_OFF + GLA_HEADS * GLA_DV
LANES = 128
GLA_COLS = GLA_MISC_OFF + LANES

VMEM_LIMIT = 56 * 1024 * 1024

ADA_TN = 768
PREP_TR = 256
PREP_STEP = 512
INPROJ_TM = 256
INPROJ_TN = 512
NSA_TQ = 256
NSA_TK = 256
GLA_TS = 512
OUT_TM = 256
RANK_TM = 256
MOE_TB = 256
DISPATCH_TM = 256
COMB_TM = 256
DMA_UNROLL = 8


def _cparams(sem):
    return pltpu.CompilerParams(dimension_semantics=sem, vmem_limit_bytes=VMEM_LIMIT)


def _adaln_kernel(ct_ref, w_ref, b_ref, o_ref, s_scr):
    nb = ct_ref.shape[1]
    kdim, tn = w_ref.shape

    @pl.when(pl.program_id(0) == 0)
    def _():
        ct = ct_ref[...]
        s = ct * jax.nn.sigmoid(ct)
        for b in range(nb):
            s_scr[b] = jnp.broadcast_to(s[:, b:b + 1], (kdim, LANES))

    def body(k, accs):
        r = pl.multiple_of(k * 8, 8)
        w8 = w_ref[pl.ds(r, 8), :]
        out = []
        for b, acc in enumerate(accs):
            s8 = s_scr[b, pl.ds(r, 8), :]
            out.append(acc + w8 * jnp.concatenate([s8] * (tn // LANES), axis=1))
        return tuple(out)

    accs = lax.fori_loop(0, kdim // 8, body, tuple(jnp.zeros((8, tn), F32) for _ in range(nb)), unroll=2)
    bias = b_ref[...]
    for b, acc in enumerate(accs):
        o_ref[b:b + 1, :] = jnp.sum(acc, axis=0, keepdims=True) + bias


def _adaln(c, w, b):
    nb, d = c.shape
    n = w.shape[1]
    return pl.pallas_call(
        _adaln_kernel,
        grid=(n // ADA_TN,),
        in_specs=[pl.BlockSpec((d, nb), lambda j: (0, 0)),
                  pl.BlockSpec((d, ADA_TN), lambda j: (0, j)),
                  pl.BlockSpec((1, ADA_TN), lambda j: (0, j))],
        out_specs=pl.BlockSpec((nb, ADA_TN), lambda j: (0, j)),
        out_shape=jax.ShapeDtypeStruct((nb, n), F32),
        scratch_shapes=[pltpu.VMEM((nb, d, LANES), F32)],
        compiler_params=_cparams(("arbitrary",)),
        name="adaln",
    )(c.T, w, b.reshape(1, n))


def _modulated_norm(x, nw, sc, sh):
    ms = jnp.mean(x * x, axis=-1, keepdims=True)
    h = x * lax.rsqrt(ms + EPS) * nw
    return h * (1.0 + sc) + sh


def _prep_w_in_kernel(wt_ref, wn_ref, wg_ref, *, o_gate, o_gla, o_lr):
    tr = wt_ref.shape[1]
    step = PREP_STEP

    def put(dst, c0, rows):
        dst[:, c0:c0 + rows.shape[0]] = rows.T.astype(BF16)

    for c in range(0, o_gate, step):
        put(wn_ref, c, wt_ref[c:c + step, :])
    for c in range(0, o_lr - o_gla, step):
        put(wg_ref, c, wt_ref[o_gla + c:o_gla + c + step, :])
    pad = wg_ref.shape[1] - (wt_ref.shape[0] - o_gate)
    misc = jnp.concatenate([wt_ref[o_gate:o_gla, :], wt_ref[o_lr:, :], jnp.zeros((pad, tr), F32)], axis=0)
    put(wg_ref, o_lr - o_gla, misc)


def _prep_w_in(w_in_t, o_gate, o_gla, o_lr):
    n, d = w_in_t.shape
    tr = PREP_TR
    assert o_gate % PREP_STEP == 0 and (o_lr - o_gla) % PREP_STEP == 0
    return pl.pallas_call(
        functools.partial(_prep_w_in_kernel, o_gate=o_gate, o_gla=o_gla, o_lr=o_lr),
        grid=(d // tr,),
        in_specs=[pl.BlockSpec((n, tr), lambda i: (0, i))],
        out_specs=[pl.BlockSpec((tr, NSA_COLS), lambda i: (i, 0)), pl.BlockSpec((tr, GLA_COLS), lambda i: (i, 0))],
        out_shape=[jax.ShapeDtypeStruct((d, NSA_COLS), BF16), jax.ShapeDtypeStruct((d, GLA_COLS), BF16)],
        compiler_params=_cparams(("parallel",)),
        name="prep_w_in",
    )(w_in_t)


def _inproj_kernel(x_ref, sc_ref, sh_ref, nw_ref, wn_ref, wg_ref, on_ref, og_ref):
    h = _modulated_norm(x_ref[0], nw_ref[...], sc_ref[0, 0], sh_ref[0, 0]).astype(BF16)
    dh = NSA_HEAD_DIM
    tn = INPROJ_TN
    for c in range(wn_ref.shape[1] // tn):
        acc = jnp.dot(h, wn_ref[:, c * tn:(c + 1) * tn], preferred_element_type=F32)
        for u in range(tn // dh):
            on_ref[0, c * (tn // dh) + u] = acc[:, u * dh:(u + 1) * dh]
    og_ref[0] = jnp.dot(h, wg_ref[...], preferred_element_type=F32)


def _inproj(x, mod4, nw, w_nsa, w_gla, *, sc_idx, sh_idx):
    bsz, s, d = x.shape
    n_nsa, n_gla = w_nsa.shape[1], w_gla.shape[1]
    tm = INPROJ_TM
    dh = NSA_HEAD_DIM
    return pl.pallas_call(
        _inproj_kernel,
        grid=(bsz, s // tm),
        in_specs=[pl.BlockSpec((1, tm, d), lambda b, i: (b, i, 0)),
                  pl.BlockSpec((1, 1, 1, d), lambda b, i: (b, sc_idx, 0, 0)),
                  pl.BlockSpec((1, 1, 1, d), lambda b, i: (b, sh_idx, 0, 0)),
                  pl.BlockSpec((1, d), lambda b, i: (0, 0)),
                  pl.BlockSpec((d, n_nsa), lambda b, i: (0, 0), pipeline_mode=pl.Buffered(1)),
                  pl.BlockSpec((d, n_gla), lambda b, i: (0, 0), pipeline_mode=pl.Buffered(1))],
        out_specs=[pl.BlockSpec((1, n_nsa // dh, tm, dh), lambda b, i: (b, 0, i, 0)),
                   pl.BlockSpec((1, tm, n_gla), lambda b, i: (b, i, 0))],
        out_shape=[jax.ShapeDtypeStruct((bsz, n_nsa // dh, s, dh), F32),
                   jax.ShapeDtypeStruct((bsz, s, n_gla), F32)],
        compiler_params=_cparams(("parallel", "parallel")),
        name="inproj",
    )(x, mod4, mod4, nw, w_nsa, w_gla)


def _compress_kernel(a_ref, pos_ref, w1_ref, w2_ref, o_ref):
    nch = a_ref.shape[2] // CMP_STRIDE
    a = jnp.concatenate([a_ref[0, 0, pl.ds(t, nch, stride=CMP_STRIDE), :] for t in range(CMP_STRIDE)], axis=1)
    pos = pos_ref[0]
    half = a.shape[1]
    y1 = jnp.dot(a + pos[0:1], w1_ref[0, :half, :], precision=HI, preferred_element_type=F32)
    y2 = jnp.dot(a + pos[1:2], w1_ref[0, half:, :], precision=HI, preferred_element_type=F32)
    nrow = a.shape[0]
    h = y1 + pltpu.roll(y2, nrow - 1, axis=0)
    out = jnp.dot(jax.nn.gelu(h), w2_ref[0], precision=HI, preferred_element_type=F32)
    row = lax.broadcasted_iota(I32, out.shape, 0)
    o_ref[0, 0] = jnp.where(row < nrow - 1, out, 0.0)


def _compress(proj_nsa, pos, w1, w2):
    bsz, _, s, dh = proj_nsa.shape
    nslot = 2 * NSA_KV_HEADS
    nch = s // CMP_STRIDE
    cw = CMP_STRIDE * dh
    return pl.pallas_call(
        _compress_kernel,
        grid=(bsz, nslot),
        in_specs=[pl.BlockSpec((1, 1, s, dh), lambda b, t: (b, NSA_HEADS + t, 0, 0)),
                  pl.BlockSpec((1, 2, cw), lambda b, t: (t // NSA_KV_HEADS, 0, 0)),
                  pl.BlockSpec((1, 2 * cw, dh), lambda b, t: (t // NSA_KV_HEADS, 0, 0)),
                  pl.BlockSpec((1, dh, dh), lambda b, t: (t // NSA_KV_HEADS, 0, 0))],
        out_specs=pl.BlockSpec((1, 1, nch, dh), lambda b, t: (b, t, 0, 0)),
        out_shape=jax.ShapeDtypeStruct((bsz, nslot, nch, dh), F32),
        compiler_params=_cparams(("parallel", "parallel")),
        name="nsa_compress",
    )(proj_nsa, pos, w1, w2)


def _hi_lo(x):
    hi = x.astype(BF16)
    return hi, (x - hi.astype(F32)).astype(BF16)


def _nt_dot(a, b, **kw):
    return lax.dot_general(a, b, (((1,), (1,)), ((), ())), preferred_element_type=F32, **kw)


def _nsa_kernel(slopes_ref, q_ref, kc_ref, vc_ref, ks_ref, vs_ref, kw_ref, vw_ref, gate_ref, o_ref,
                ksb, vst, kwb, vwt, q4_scr, *scr):
    g = pl.program_id(1)
    qi = pl.program_id(2)
    tq_n = q_ref.shape[2]
    dh = NSA_HEAD_DIM
    nr = NSA_Q_PER_KV
    seq = ks_ref.shape[2]
    tk_n = NSA_TK
    nb = seq // SEL_BLOCK

    @pl.when(qi == 0)
    def _():
        row = lax.broadcasted_iota(I32, (seq, dh), 0)
        lane = lax.broadcasted_iota(I32, (seq, dh), 1)
        blk = row // SEL_BLOCK
        pos = jnp.where((lane >= nb) & (lane < nb + 3), (blk * SEL_BLOCK).astype(F32),
                        jnp.where((lane >= nb + 3) & (lane < nb + 6), (row % SEL_BLOCK).astype(F32), 0.0))
        ksb[...] = jnp.concatenate([jnp.where(lane == blk, -MASKED, pos), ks_ref[0, 0]], axis=1).astype(BF16)
        kwb[...] = jnp.concatenate([pos, kw_ref[0, 0]], axis=1).astype(BF16)
        for c in range(seq // tk_n):
            rows = slice(c * tk_n, (c + 1) * tk_n)
            for src, dst in ((vs_ref, vst), (vw_ref, vwt)):
                v = src[0, 0, rows, :]
                dst[c] = jnp.concatenate([v, v], axis=1).T[:dh].astype(BF16)

    t0 = qi * tq_n
    tq = t0 + lax.broadcasted_iota(I32, (1, tq_n), 1)
    tq_f = tq.astype(F32)
    slopes = [slopes_ref[g * nr + r] for r in range(nr)]
    scale = dh ** -0.5
    q_t = []
    for pair in range(nr // 2):
        both = jnp.concatenate([q_ref[0, 2 * pair], q_ref[0, 2 * pair + 1]], axis=1).T * scale
        q_t += [both[:dh], both[dh:]]

    ncp = kc_ref.shape[2]
    kc = kc_ref[0, 0]
    vc = vc_ref[0, 0]
    vc_t = jnp.concatenate([vc, vc], axis=1).T[:dh].astype(BF16)
    n_sub = lax.broadcasted_iota(I32, (ncp, 1), 0)
    blk_end = n_sub * CMP_STRIDE + (CMP_BLOCK - 1)
    center = n_sub.astype(F32) * CMP_STRIDE + (CMP_BLOCK - 1) / 2.0
    valid_c = blk_end <= tq
    dist_c = tq_f - center
    psum = jnp.zeros((ncp, tq_n), F32)
    o_c = []
    kc_hi, kc_lo = _hi_lo(kc)
    kc_cat = jnp.concatenate([kc_hi, kc_lo, kc_hi, jnp.zeros_like(kc_hi)], axis=1)
    for r in range(nr):
        q_hi, q_lo = _hi_lo(q_t[r])
        q_cat = jnp.concatenate([q_hi, q_hi, q_lo, jnp.zeros_like(q_hi)], axis=0)
        s = jnp.dot(kc_cat, q_cat, preferred_element_type=F32)
        s = jnp.where(valid_c, s - slopes[r] * dist_c, NEG)
        e = jnp.exp(s - jnp.max(s, axis=0, keepdims=True))
        p = jnp.where(valid_c, e / jnp.sum(e, axis=0, keepdims=True), 0.0)
        psum = psum + p
        o_c.append(jnp.dot(vc_t, p.astype(BF16), preferred_element_type=F32))

    rowj = lax.broadcasted_iota(I32, (LANES, ncp), 0) * SEL_BLOCK
    coln = lax.broadcasted_iota(I32, (LANES, ncp), 1) * CMP_STRIDE
    overlap = jnp.where((coln < rowj + SEL_BLOCK) & (coln + CMP_BLOCK > rowj)
                        & (coln < (ncp - 1) * CMP_STRIDE) & (rowj < nb * SEL_BLOCK), 1.0, 0.0)
    ov = overlap.astype(BF16)
    imp = jnp.dot(jnp.concatenate([ov, ov], axis=1), jnp.concatenate(_hi_lo(psum), axis=0),
                  preferred_element_type=F32)[:nb]
    j_sub = lax.broadcasted_iota(I32, (nb, 1), 0)
    qblk = tq // SEL_BLOCK
    forced = (j_sub == 0) | (j_sub == qblk) | (j_sub == qblk - 1)
    imp = jnp.where(forced, FORCE, jnp.where(j_sub <= qblk, imp, NEG))
    cnt = jnp.zeros((nb, tq_n), F32)
    for i in range(nb):
        ci = imp[i:i + 1, :]
        tie = jnp.where(j_sub > i, 1.0, 0.0)
        cnt = cnt + jnp.where(ci > imp, 1.0, jnp.where(ci == imp, tie, 0.0))
    notsel = jnp.where(cnt < float(min(N_SEL, nb)), 0.0, 1.0)

    sub_h = lax.broadcasted_iota(I32, (dh - nb, 1), 0)
    for r in range(nr):
        scol = jnp.zeros((dh - nb, 1), F32)
        for i in range(3):
            piece = slopes_ref[(i + 1) * NSA_HEADS + g * nr + r]
            scol = jnp.where((sub_h == i) | (sub_h == 3 + i), piece, scol)
        q4_scr[:, r * tq_n:(r + 1) * tq_n] = jnp.concatenate(
            [notsel, jnp.broadcast_to(scol, (dh - nb, tq_n)), q_t[r] * LOG2E], axis=0).astype(BF16)

    nwt = WINDOW // tk_n
    nbuf = nwt + 2
    stats_s, stats_w = scr[0:3], scr[3:6]
    s_buf, p_buf, a_buf = scr[6:6 + nbuf], scr[6 + nbuf:6 + 2 * nbuf], scr[6 + 2 * nbuf:6 + 3 * nbuf]
    for m_ref, l_ref, acc_ref in (stats_s, stats_w):
        m_ref[...] = jnp.full(m_ref.shape, NEG, F32)
        l_ref[...] = jnp.zeros(l_ref.shape, F32)
        acc_ref[...] = jnp.zeros(acc_ref.shape, F32)
    key_i = lax.broadcasted_iota(I32, (tk_n, LANES), 0)
    qry_j = lax.broadcasted_iota(I32, (tk_n, LANES), 1)

    def scores(k_ref, kt, buf):
        k_tile = k_ref[pl.ds(pl.multiple_of(kt * tk_n, tk_n), tk_n), :]
        s_buf[buf][...] = jnp.dot(k_tile, q4_scr[...], preferred_element_type=F32)

    def softmax(buf, keep_fn, stats):
        m_ref, l_ref, _ = stats
        for cb in range(nr * tq_n // LANES):
            cols = slice(cb * LANES, (cb + 1) * LANES)
            s = s_buf[buf][:, cols]
            if keep_fn is not None:
                s = jnp.where(keep_fn(qry_j + (cb * LANES) % tq_n), s, -MASKED)
            m_prev = m_ref[:, cols]
            m_new = jnp.maximum(m_prev, jnp.max(s, axis=0, keepdims=True))
            alpha = jnp.exp2(m_prev - m_new)
            p = jnp.exp2(s - m_new)
            l_ref[:, cols] = alpha * l_ref[:, cols] + jnp.sum(p, axis=0, keepdims=True)
            m_ref[:, cols] = m_new
            a_buf[buf][:, cols] = alpha
            p_buf[buf][:, cols] = p.astype(BF16)

    def values(vt_ref, kt, buf, stats):
        acc_ref = stats[2]
        pv = jnp.dot(vt_ref[kt], p_buf[buf][...], preferred_element_type=F32)
        acc_ref[...] = acc_ref[...] * a_buf[buf][...] + pv

    def batch(jobs):
        for buf, (k_ref, _, kt, _, _) in enumerate(jobs):
            scores(k_ref, kt, buf)
        for buf, (_, vt_ref, kt, keep_fn, stats) in enumerate(jobs):
            softmax(buf, keep_fn, stats)
            values(vt_ref, kt, buf, stats)

    size = nbuf
    while size >= 1:
        @pl.when(qi & size != 0)
        def _(size=size):
            base = qi & ~(2 * size - 1)
            batch([(ksb, vst, base + t, None, stats_s) for t in range(size)])
        size //= 2

    def causal(j):
        return key_i <= j

    jobs = []
    for back in range(nwt, 0, -1):
        off = jnp.where(qi >= back, 0, tk_n)
        keep_fn = (lambda j, off=off: key_i - off > j) if back == nwt else (lambda j, off=off: key_i >= off)
        jobs.append((kwb, vwt, jnp.maximum(qi - back, 0), keep_fn, stats_w))
    jobs.append((kwb, vwt, qi, causal, stats_w))
    jobs.append((ksb, vst, qi, causal, stats_s))
    batch(jobs)

    gsel = jnp.where(lax.broadcasted_iota(I32, (LANES, LANES), 0)
                     == lax.broadcasted_iota(I32, (LANES, LANES), 1) + g * (nr * N_BRANCH), 1.0, 0.0)
    gs = gsel.astype(BF16)
    gates = jax.nn.sigmoid(jnp.dot(jnp.concatenate(_hi_lo(gate_ref[0]), axis=1), jnp.concatenate([gs, gs], axis=0),
                                   preferred_element_type=F32)).T
    (_, l_s, acc_s), (_, l_w, acc_w) = stats_s, stats_w
    for pair in range(nr // 2):
        o_t = []
        for r in (2 * pair, 2 * pair + 1):
            c0 = r * N_BRANCH
            cols = slice(r * tq_n, (r + 1) * tq_n)
            o_t.append(gates[c0:c0 + 1, :] * o_c[r] + (gates[c0 + 1:c0 + 2, :] / l_s[:, cols]) * acc_s[:, cols]
                       + (gates[c0 + 2:c0 + 3, :] / l_w[:, cols]) * acc_w[:, cols])
        o_ref[0, :, pair * LANES:(pair + 1) * LANES] = jnp.concatenate(o_t, axis=0).T


def _nsa(slopes, proj_nsa, kvc, proj_gla):
    bsz, _, s, dh = proj_nsa.shape
    g_n, nr = NSA_KV_HEADS, NSA_Q_PER_KV
    tq = NSA_TQ
    tk = NSA_TK
    assert tq == tk and WINDOW % tk == 0 and 2 * dh == LANES
    ncp = kvc.shape[2]
    kv0 = NSA_HEADS
    nq = nr * tq
    nbuf = WINDOW // tk + 2
    assert nbuf & (nbuf - 1) == 0 and s // tq <= 2 * nbuf

    def kv_spec(i):
        return pl.BlockSpec((1, 1, s, dh), lambda b, g, q, i=i: (b, kv0 + i * g_n + g, 0, 0))

    return pl.pallas_call(
        _nsa_kernel,
        grid=(bsz, g_n, s // tq),
        in_specs=[pl.BlockSpec(memory_space=pltpu.SMEM),
                  pl.BlockSpec((1, nr, tq, dh), lambda b, g, q: (b, g, q, 0)),
                  pl.BlockSpec((1, 1, ncp, dh), lambda b, g, q: (b, g, 0, 0)),
                  pl.BlockSpec((1, 1, ncp, dh), lambda b, g, q: (b, g_n + g, 0, 0)),
                  kv_spec(2), kv_spec(3), kv_spec(4), kv_spec(5),
                  pl.BlockSpec((1, tq, LANES), lambda b, g, q: (b, q, GLA_MISC_OFF // LANES))],
        out_specs=pl.BlockSpec((1, tq, nr * dh), lambda b, g, q: (b, q, g)),
        out_shape=jax.ShapeDtypeStruct((bsz, s, NSA_HEADS * dh), F32),
        scratch_shapes=[pltpu.VMEM((s, LANES), BF16), pltpu.VMEM((s // tk, dh, tk), BF16)] * 2
        + [pltpu.VMEM((LANES, nq), BF16)]
        + [pltpu.VMEM((1, nq), F32), pltpu.VMEM((1, nq), F32), pltpu.VMEM((dh, nq), F32)] * 2
        + [pltpu.VMEM((tk, nq), F32)] * nbuf + [pltpu.VMEM((tk, nq), BF16)] * nbuf + [pltpu.VMEM((1, nq), F32)] * nbuf,
        compiler_params=_cparams(("parallel", "parallel", "arbitrary")),
        name="nsa_attention",
    )(slopes, proj_nsa, kvc, kvc, proj_nsa, proj_nsa, proj_nsa, proj_nsa, proj_gla)


def _gla_kernel(q_ref, k_ref, v_ref, og_ref, lr_ref, wg_ref, bg_ref, nw_ref, o_ref, st_scr, la_scr, b_scr):
    rows_n = q_ref.shape[1]
    c_n, sub = GLA_CHUNK, GLA_SUB
    nh, dk, dv = GLA_HEADS, GLA_DK, GLA_DV

    @pl.when(pl.program_id(1) == 0)
    def _():
        st_scr[...] = jnp.zeros(st_scr.shape, F32)

    z = jnp.dot(lr_ref[0], wg_ref[...], precision=HI, preferred_element_type=F32) + bg_ref[...]
    la_scr[...] = (jnp.minimum(z, 0.0) - jnp.log1p(jnp.exp(-jnp.abs(z)))) * (1.0 / GLA_GATE_NORM)
    tril = jnp.where(lax.broadcasted_iota(I32, (c_n, c_n), 0) >= lax.broadcasted_iota(I32, (c_n, c_n), 1), 1.0, 0.0)
    row_c = lax.broadcasted_iota(I32, (c_n, 1), 0)
    row_s = lax.broadcasted_iota(I32, (sub, 1), 0)
    lane_c = lax.broadcasted_iota(I32, (1, c_n), 1)
    nw = nw_ref[...]
    hk = [slice(h * dk, (h + 1) * dk) for h in range(nh)]
    hv = [slice(h * dv, (h + 1) * dv) for h in range(nh)]

    def chunk(c, carry):
        r0 = pl.multiple_of(c * c_n, c_n)
        rows = pl.ds(r0, c_n)
        qc = q_ref[0, rows, :] * (dk ** -0.5)
        kc = k_ref[0, rows, :]
        vc = [v_ref[0, rows, hv[h]].astype(BF16) for h in range(nh)]
        b = jnp.dot(tril, la_scr[rows, :], precision=HI, preferred_element_type=F32)
        b_scr[...] = b
        st = [st_scr[h] for h in range(nh)]
        q_e = (qc * jnp.exp(b)).astype(BF16)
        o = [_nt_dot(q_e[:, hk[h]], st[h].astype(BF16)) for h in range(nh)]
        strips = [[] for _ in range(nh)]
        for blk in range(c_n // sub):
            lo = blk * sub
            q_i = qc[lo:lo + sub]
            b_i = b[lo:lo + sub]
            a = [jnp.zeros((sub, c_n), F32) for _ in range(nh)]
            if blk > 0:
                b_r = b_scr[lo - 1:lo, :]
                q_d = (q_i * jnp.exp(b_i - b_r)).astype(BF16)
                k_d = (kc * jnp.exp(jnp.where(row_c < lo, b_r - b, -jnp.inf))).astype(BF16)
                a = [_nt_dot(q_d[:, hk[h]], k_d[:, hk[h]]) for h in range(nh)]
            for j in range(sub):
                b_j = b_scr[lo + j:lo + j + 1, :]
                k_j = k_ref[0, pl.ds(r0 + lo + j, 1), :]
                prod = q_i * k_j * jnp.exp(jnp.where(row_s >= j, b_i - b_j, -jnp.inf))
                for h in range(nh):
                    col = jnp.sum(prod[:, hk[h]], axis=-1, keepdims=True)
                    a[h] = jnp.where(lane_c == lo + j, col, a[h]) if blk == 0 else (
                        a[h] + jnp.where(lane_c == lo + j, col, 0.0))
            for h in range(nh):
                strips[h].append(a[h])
        for h in range(nh):
            attn = jnp.concatenate(strips[h], axis=0)
            o[h] = o[h] + jnp.dot(attn.astype(BF16), vc[h], preferred_element_type=F32)
        b_last = b_scr[c_n - 1:c_n, :]
        k_dec = (kc * jnp.exp(b_last - b)).astype(BF16)
        decay = jnp.exp(b_last)
        for h in range(nh):
            st_scr[h] = st[h] * decay[:, hk[h]] + lax.dot_general(
                vc[h], k_dec[:, hk[h]], (((0,), (0,)), ((), ())), preferred_element_type=F32)
        for h in range(nh):
            og = og_ref[0, rows, hv[h]]
            on = o[h] * lax.rsqrt(jnp.mean(o[h] * o[h], axis=-1, keepdims=True) + EPS) * nw
            o_ref[0, rows, hv[h]] = on * (og * jax.nn.sigmoid(og))
        return carry

    lax.fori_loop(0, rows_n // c_n, chunk, 0)


def _gla(proj_gla, wg_pad, bg, nw):
    bsz, s, _ = proj_gla.shape
    nh, dk, dv = GLA_HEADS, GLA_DK, GLA_DV
    ts = GLA_TS
    wk, wv = nh * dk, nh * dv
    return pl.pallas_call(
        _gla_kernel,
        grid=(bsz, s // ts),
        in_specs=[pl.BlockSpec((1, ts, wk), lambda b, i: (b, i, GLA_Q_OFF // wk)),
                  pl.BlockSpec((1, ts, wk), lambda b, i: (b, i, GLA_K_OFF // wk)),
                  pl.BlockSpec((1, ts, wv), lambda b, i: (b, i, GLA_V_OFF // wv)),
                  pl.BlockSpec((1, ts, wv), lambda b, i: (b, i, GLA_OG_OFF // wv)),
                  pl.BlockSpec((1, ts, LANES), lambda b, i: (b, i, GLA_MISC_OFF // LANES)),
                  pl.BlockSpec((LANES, wk), lambda b, i: (0, 0)),
                  pl.BlockSpec((1, wk), lambda b, i: (0, 0)),
                  pl.BlockSpec((1, dv), lambda b, i: (0, 0))],
        out_specs=pl.BlockSpec((1, ts, wv), lambda b, i: (b, i, 0)),
        out_shape=jax.ShapeDtypeStruct((bsz, s, wv), F32),
        scratch_shapes=[pltpu.VMEM((nh, dv, dk), F32), pltpu.VMEM((ts, wk), F32), pltpu.VMEM((GLA_CHUNK, wk), F32)],
        compiler_params=_cparams(("parallel", "arbitrary")),
        name="gla",
    )(proj_gla, proj_gla, proj_gla, proj_gla, proj_gla, wg_pad, bg, nw)


def _outproj_kernel(nsa_ref, gla_ref, x_ref, wo_ref, g1_ref, sc_ref, sh_ref, nw_ref, wr_ref, br_ref,
                    x1_ref, h_ref, route_ref):
    half = nsa_ref.shape[2]
    acc = jnp.dot(nsa_ref[0].astype(BF16), wo_ref[:half, :], preferred_element_type=F32)
    acc = acc + jnp.dot(gla_ref[0].astype(BF16), wo_ref[half:, :], preferred_element_type=F32)
    x1 = x_ref[0] + g1_ref[0, 0] * acc
    x1_ref[0] = x1
    h = _modulated_norm(x1, nw_ref[...], sc_ref[0, 0], sh_ref[0, 0])
    h_ref[0] = h
    h_hi = h.astype(BF16)
    h_lo = (h - h_hi.astype(F32)).astype(BF16)
    t = jnp.dot(h_hi, wr_ref[...], preferred_element_type=F32)
    logits = (t[:, :LANES] + t[:, LANES:] + jnp.dot(h_lo, wr_ref[:, :LANES], preferred_element_type=F32)
              + br_ref[...])
    lane = lax.broadcasted_iota(I32, (1, LANES), 1)
    ninf = -jnp.inf
    is_g = (lane >= N_EXPERTS) & (lane < N_EXPERTS + N_GROUPS)
    gl = jnp.where(is_g, logits, ninf)
    ge = jnp.exp(gl - jnp.max(gl, axis=-1, keepdims=True))
    gp = ge / jnp.sum(ge, axis=-1, keepdims=True)
    gp_max = jnp.max(gp, axis=-1, keepdims=True)
    grp = jnp.min(jnp.where((gp == gp_max) & is_g, lane - N_EXPERTS, LANES), axis=-1, keepdims=True)
    in_grp = (lane // EXPERTS_PER_GROUP == grp) & (lane < N_EXPERTS)
    el = jnp.where(in_grp, logits, ninf)
    v1 = jnp.max(el, axis=-1, keepdims=True)
    i1 = jnp.min(jnp.where(el == v1, lane, LANES), axis=-1, keepdims=True)
    el2 = jnp.where(lane == i1, ninf, el)
    v2 = jnp.max(el2, axis=-1, keepdims=True)
    i2 = jnp.min(jnp.where(el2 == v2, lane, LANES), axis=-1, keepdims=True)
    e2 = jnp.exp(v2 - v1)
    den = 1.0 + e2
    w1 = gp_max * (1.0 / den)
    w2 = gp_max * (e2 / den)
    route_ref[0] = jnp.where(lane == 0, i1.astype(F32), jnp.where(lane == 1, i2.astype(F32), jnp.where(
        lane == 2, w1, jnp.where(lane == 3, w2, 0.0))))


def _outproj(o_nsa, o_gla, x, wo, mod4, nw, wr, br):
    bsz, s, d = x.shape
    tm = OUT_TM
    half = o_nsa.shape[2]

    def mod_spec(idx):
        return pl.BlockSpec((1, 1, 1, d), lambda b, i: (b, idx, 0, 0))

    row = lambda w: pl.BlockSpec((1, tm, w), lambda b, i: (b, i, 0))
    return pl.pallas_call(
        _outproj_kernel,
        grid=(bsz, s // tm),
        in_specs=[row(half), row(half), row(d),
                  pl.BlockSpec((2 * half, d), lambda b, i: (0, 0)),
                  mod_spec(2), mod_spec(4), mod_spec(3),
                  pl.BlockSpec((1, d), lambda b, i: (0, 0)),
                  pl.BlockSpec((d, 2 * LANES), lambda b, i: (0, 0)),
                  pl.BlockSpec((1, LANES), lambda b, i: (0, 0))],
        out_specs=[row(d), row(d), row(LANES)],
        out_shape=[jax.ShapeDtypeStruct((bsz, s, d), F32), jax.ShapeDtypeStruct((bsz, s, d), F32),
                   jax.ShapeDtypeStruct((bsz, s, LANES), F32)],
        compiler_params=_cparams(("parallel", "parallel")),
        name="outproj_router",
    )(o_nsa, o_gla, x, wo, mod4, mod4, mod4, nw, wr, br)


def _rank_kernel(route_ref, dest_ref, meta_ref, rank_scr):
    n = route_ref.shape[0]
    tm = RANK_TM
    lane_i = lax.broadcasted_iota(I32, (1, LANES), 1)
    lane = lane_i.astype(F32)
    strict = jnp.where(lax.broadcasted_iota(I32, (tm, tm), 0) > lax.broadcasted_iota(I32, (tm, tm), 1),
                       1.0, 0.0).astype(BF16)

    def two_lanes(a, b):
        return jnp.where(lane_i == 0, a, jnp.where(lane_i == 1, b, 0.0))

    def pick(e, table):
        return jnp.sum(jnp.where(lane == e, table, 0.0), axis=-1, keepdims=True)

    def count(i, seen):
        r0 = pl.multiple_of(i * tm, tm)
        rt = route_ref[pl.ds(r0, tm), :]
        e1, e2 = rt[:, 0:1], rt[:, 1:2]
        member = jnp.where(lane == e1, 1.0, jnp.where(lane == e2, 1.0, 0.0))
        before = jnp.dot(strict, member.astype(BF16), preferred_element_type=F32) + seen
        rank_scr[pl.ds(r0, tm), :] = two_lanes(pick(e1, before), pick(e2, before))
        return seen + jnp.sum(member, axis=0, keepdims=True)

    counts = lax.fori_loop(0, n // tm, count, jnp.zeros((1, LANES), F32))
    ntile = jnp.floor((counts + (MOE_TB - 1)) * (1.0 / MOE_TB))
    incl = jnp.where(lax.broadcasted_iota(I32, (LANES, LANES), 0) <= lax.broadcasted_iota(I32, (LANES, LANES), 1),
                     1.0, 0.0).astype(BF16)
    tile_end = jnp.dot(jnp.broadcast_to(ntile, (8, LANES)).astype(BF16), incl,
                       preferred_element_type=F32)[0:1]
    row_start = (tile_end - ntile) * MOE_TB

    def place(i, carry):
        r0 = pl.multiple_of(i * tm, tm)
        rt = route_ref[pl.ds(r0, tm), :]
        rk = rank_scr[pl.ds(r0, tm), :]
        d1 = pick(rt[:, 0:1], row_start) + rk[:, 0:1]
        d2 = pick(rt[:, 1:2], row_start) + rk[:, 1:2]
        dest_ref[pl.ds(r0, tm), :] = two_lanes(d1, d2).astype(I32)
        return carry

    lax.fori_loop(0, n // tm, place, 0)
    trow = lax.broadcasted_iota(I32, (meta_ref.shape[0], 1), 0).astype(F32)
    texp = jnp.sum(jnp.where((tile_end <= trow) & (lane_i < N_EXPERTS), 1.0, 0.0), axis=-1, keepdims=True)
    texp = jnp.minimum(texp, N_EXPERTS - 1.0)
    used = pick(N_EXPERTS - 1.0, tile_end)
    diag = lax.broadcasted_iota(I32, (meta_ref.shape[0], LANES), 0) == lane_i
    end_rows = jnp.sum(jnp.where(diag, tile_end, 0.0), axis=-1, keepdims=True)
    ntile_rows = jnp.sum(jnp.where(diag, ntile, 0.0), axis=-1, keepdims=True)
    meta_ref[...] = jnp.where(lane_i == 2, end_rows, jnp.where(lane_i == 3, ntile_rows, two_lanes(
        texp, jnp.broadcast_to(used, texp.shape)))).astype(I32)


def _rank(route):
    n = route.shape[0]
    return pl.pallas_call(
        _rank_kernel,
        out_shape=[jax.ShapeDtypeStruct((n, LANES), I32), jax.ShapeDtypeStruct((LANES, LANES), I32)],
        scratch_shapes=[pltpu.VMEM((n, LANES), F32)],
        compiler_params=pltpu.CompilerParams(vmem_limit_bytes=VMEM_LIMIT),
        name="moe_rank",
    )(route)


def _dispatch_kernel(dest_ref, ends_ref, h_ref, xs_ref, zero_scr, sem, zsem):
    i = pl.program_id(0)
    tm = h_ref.shape[0]
    tb = zero_scr.shape[0]

    @pl.when(i == 0)
    def _():
        zero_scr[...] = jnp.zeros(zero_scr.shape, F32)

        def zero_copy(e):
            r0 = pl.multiple_of((ends_ref[e] - 1) * tb, tb)
            return pltpu.make_async_copy(zero_scr, xs_ref.at[pl.ds(r0, tb)], zsem)

        def start(e, carry):
            @pl.when(ends_ref[N_EXPERTS + e] > 0)
            def _():
                zero_copy(e).start()
            return carry

        def wait(e, carry):
            @pl.when(ends_ref[N_EXPERTS + e] > 0)
            def _():
                zero_copy(e).wait()
            return carry

        def tail_copy(t):
            return pltpu.make_async_copy(zero_scr, xs_ref.at[pl.ds(pl.multiple_of(t * tb, tb), tb)], zsem)

        def tail_start(t, carry):
            tail_copy(t).start()
            return carry

        def tail_wait(t, carry):
            tail_copy(t).wait()
            return carry

        used = ends_ref[2 * N_EXPERTS]
        lax.fori_loop(0, N_EXPERTS, start, 0)
        lax.fori_loop(used, xs_ref.shape[0] // tb, tail_start, 0)
        lax.fori_loop(0, N_EXPERTS, wait, 0)
        lax.fori_loop(used, xs_ref.shape[0] // tb, tail_wait, 0)

    def issue(r, carry):
        p = (i * tm + r) * TOP_K
        for k in range(TOP_K):
            pltpu.make_async_copy(h_ref.at[pl.ds(r, 1)], xs_ref.at[pl.ds(dest_ref[p + k], 1)], sem).start()
        return carry

    lax.fori_loop(0, tm, issue, 0, unroll=DMA_UNROLL)
    for k in range(TOP_K):
        pltpu.make_async_copy(h_ref, xs_ref.at[pl.ds(0, tm)], sem).wait()


def _dispatch(dest_flat, ends_flat, h, cap):
    n, d = h.shape
    tm = DISPATCH_TM
    return pl.pallas_call(
        _dispatch_kernel,
        grid_spec=pltpu.PrefetchScalarGridSpec(
            num_scalar_prefetch=2, grid=(n // tm,),
            in_specs=[pl.BlockSpec((tm, d), lambda i, dst, ends: (i, 0))],
            out_specs=pl.BlockSpec(memory_space=pl.ANY),
            scratch_shapes=[pltpu.VMEM((MOE_TB, d), F32), pltpu.SemaphoreType.DMA(()), pltpu.SemaphoreType.DMA(())]),
        out_shape=jax.ShapeDtypeStruct((cap, d), F32),
        compiler_params=_cparams(("arbitrary",)),
        name="moe_dispatch",
    )(dest_flat, ends_flat, h)


def _ffn_kernel(meta_ref, x_ref, wg_ref, wu_ref, wd_ref, y_ref, wgb, wub, wdb):
    t = pl.program_id(0)
    ntile = pl.num_programs(0)
    e = meta_ref[t]
    e_prev = meta_ref[jnp.maximum(t - 1, 0)]
    active = t < meta_ref[ntile]

    @pl.when(active & ((t == 0) | (e != e_prev)))
    def _():
        wgb[...] = wg_ref[0].astype(BF16)
        wub[...] = wu_ref[0].astype(BF16)
        wdb[...] = wd_ref[0].astype(BF16)

    @pl.when(active)
    def _():
        x = x_ref[...].astype(BF16)
        gate = jnp.dot(x, wgb[...], preferred_element_type=F32)
        up = jnp.dot(x, wub[...], preferred_element_type=F32)
        act = (gate * jax.nn.sigmoid(gate)) * up
        y_ref[...] = jnp.dot(act.astype(BF16), wdb[...], preferred_element_type=F32)

    @pl.when(jnp.logical_not(active))
    def _():
        y_ref[...] = jnp.zeros(y_ref.shape, F32)


def _ffn(meta_flat, xs, wg, wu, wd):
    cap, d = xs.shape
    ff = wg.shape[2]
    tb = MOE_TB
    ntile = cap // tb
    return pl.pallas_call(
        _ffn_kernel,
        grid_spec=pltpu.PrefetchScalarGridSpec(
            num_scalar_prefetch=1, grid=(ntile,),
            in_specs=[pl.BlockSpec((tb, d), lambda t, m: (jnp.minimum(t, m[ntile] - 1), 0)),
                      pl.BlockSpec((1, d, ff), lambda t, m: (m[t], 0, 0)),
                      pl.BlockSpec((1, d, ff), lambda t, m: (m[t], 0, 0)),
                      pl.BlockSpec((1, ff, d), lambda t, m: (m[t], 0, 0))],
            out_specs=pl.BlockSpec((tb, d), lambda t, m: (t, 0)),
            scratch_shapes=[pltpu.VMEM((d, ff), BF16), pltpu.VMEM((d, ff), BF16), pltpu.VMEM((ff, d), BF16)]),
        out_shape=jax.ShapeDtypeStruct((cap, d), F32),
        compiler_params=_cparams(("arbitrary",)),
        name="moe_ffn",
    )(meta_flat, xs, wg, wu, wd)


def _combine_kernel(dest_ref, y_ref, x1_ref, route_ref, g2_ref, nf_ref, o_ref, ybuf, sem):
    i = pl.program_id(0)
    nstep = pl.num_programs(0)
    tm = x1_ref.shape[0]

    def issue(tile, slot):
        def body(r, carry):
            p = (tile * tm + r) * TOP_K
            for k in range(TOP_K):
                pltpu.make_async_copy(y_ref.at[pl.ds(dest_ref[p + k], 1)], ybuf.at[slot, k, pl.ds(r, 1)],
                                      sem.at[slot]).start()
            return carry

        lax.fori_loop(0, tm, body, 0, unroll=DMA_UNROLL)

    @pl.when(i == 0)
    def _():
        issue(0, 0)

    @pl.when(i + 1 < nstep)
    def _():
        issue(i + 1, (i + 1) % 2)

    slot = i % 2
    for k in range(TOP_K):
        pltpu.make_async_copy(y_ref.at[pl.ds(0, tm)], ybuf.at[slot, k], sem.at[slot]).wait()
    rt = route_ref[...]
    moe = rt[:, 2:3] * ybuf[slot, 0] + rt[:, 3:4] * ybuf[slot, 1]
    xo = x1_ref[...] + g2_ref[0, 0] * moe
    o_ref[...] = xo * lax.rsqrt(jnp.mean(xo * xo, axis=-1, keepdims=True) + EPS) * nf_ref[...]


def _combine(dest_flat, y, x1, route, mod4, nf, seq):
    n, d = x1.shape
    tm = COMB_TM
    tiles_per_seq = seq // tm
    return pl.pallas_call(
        _combine_kernel,
        grid_spec=pltpu.PrefetchScalarGridSpec(
            num_scalar_prefetch=1, grid=(n // tm,),
            in_specs=[pl.BlockSpec(memory_space=pl.ANY),
                      pl.BlockSpec((tm, d), lambda i, dst: (i, 0)),
                      pl.BlockSpec((tm, LANES), lambda i, dst: (i, 0)),
                      pl.BlockSpec((1, 1, 1, d), lambda i, dst: (i // tiles_per_seq, 5, 0, 0)),
                      pl.BlockSpec((1, d), lambda i, dst: (0, 0))],
            out_specs=pl.BlockSpec((tm, d), lambda i, dst: (i, 0)),
            scratch_shapes=[pltpu.VMEM((2, TOP_K, tm, d), F32), pltpu.SemaphoreType.DMA((2,))]),
        out_shape=jax.ShapeDtypeStruct((n, d), F32),
        compiler_params=_cparams(("arbitrary",)),
        name="moe_combine",
    )(dest_flat, y, x1, route, mod4, nf)


def _alibi_slopes():
    n = NSA_HEADS
    full = jnp.asarray(2.0 ** (-8.0 * np.arange(1, n + 1) / n), dtype=F32)
    pieces, rest = [], full * LOG2E
    for _ in range(3):
        piece = rest.astype(BF16).astype(F32)
        pieces.append(piece)
        rest = rest - piece
    return jnp.concatenate([full] + pieces)


def _layer(x, c, w_ada, b_ada, norm1_w, w_in, cmp_pos_k, cmp_w1_k, cmp_w2_k, cmp_pos_v, cmp_w1_v, cmp_w2_v,
           gla_w_gate2, gla_b_gate, gla_norm_w, w_out, norm2_w, w_rg, b_rg, w_re, b_re, w_eg, w_eu, w_ed):
    bsz, s, d = x.shape
    dh = NSA_HEAD_DIM
    mod4 = _adaln(c, w_ada, b_ada).reshape(bsz, 6, 1, d)

    o_gate = NSA_COLS
    o_gla = o_gate + NSA_GATE_COLS
    o_lr = o_gla + 2 * GLA_HEADS * GLA_DK + 2 * GLA_HEADS * GLA_DV
    w_nsa, w_gla = _prep_w_in(w_in.T, o_gate, o_gla, o_lr)
    nw1 = norm1_w.reshape(1, d)
    proj_nsa, proj_gla = _inproj(x, mod4, nw1, w_nsa, w_gla, sc_idx=1, sh_idx=0)

    pos = jnp.stack([cmp_pos_k, cmp_pos_v]).reshape(2, 2, CMP_STRIDE * dh)
    kvc = _compress(proj_nsa, pos, jnp.stack([cmp_w1_k, cmp_w1_v]), jnp.stack([cmp_w2_k, cmp_w2_v]))
    o_nsa = _nsa(_alibi_slopes(), proj_nsa, kvc, proj_gla)

    wg_pad = jnp.zeros((LANES, GLA_HEADS * GLA_DK), F32).at[
        NSA_GATE_COLS:NSA_GATE_COLS + GLA_GATE_RANK].set(gla_w_gate2)
    o_gla_out = _gla(proj_gla, wg_pad, gla_b_gate.reshape(1, -1), gla_norm_w.reshape(1, -1))

    wr = jnp.concatenate([w_re, w_rg, jnp.zeros((d, LANES - N_EXPERTS - N_GROUPS), F32)], axis=1)
    br = jnp.concatenate([b_re, b_rg, jnp.zeros((LANES - N_EXPERTS - N_GROUPS,), F32)]).reshape(1, LANES)
    wr_hi = wr.astype(BF16)
    wr_cat = jnp.concatenate([wr_hi, (wr - wr_hi.astype(F32)).astype(BF16)], axis=1)
    x1, h2, route = _outproj(o_nsa, o_gla_out, x, w_out.astype(BF16), mod4, norm2_w.reshape(1, d), wr_cat, br)

    n = bsz * s
    npair = n * TOP_K
    cap = npair + N_EXPERTS * MOE_TB
    ntile = cap // MOE_TB
    route2 = route.reshape(n, LANES)
    dest, meta = _rank(route2)
    dest_flat = dest[:, :TOP_K].reshape(npair)
    meta_flat = jnp.concatenate([meta[:ntile, 0], meta[:1, 1]])
    ends_flat = jnp.concatenate([meta[:N_EXPERTS, 2], meta[:N_EXPERTS, 3], meta[:1, 1]])
    xs = _dispatch(dest_flat, ends_flat, h2.reshape(n, d), cap)
    y = _ffn(meta_flat, xs, w_eg, w_eu, w_ed)
    return x1.reshape(n, d), y, dest_flat, route2, mod4


def kernel(x, c, w_ada, b_ada, norm1_w, w_in, cmp_pos_k, cmp_w1_k, cmp_w2_k, cmp_pos_v, cmp_w1_v, cmp_w2_v,
           gla_w_gate2, gla_b_gate, gla_norm_w, w_out, norm2_w, w_router_group, b_router_group, w_router_expert,
           b_router_expert, w_expert_gate, w_expert_up, w_expert_down, norm_f_w):
    bsz, s, d = x.shape
    assert w_ada.shape[0] == 1, "single layer"
    x1, y, dest_flat, route2, mod4 = _layer(
        x, c, w_ada[0], b_ada[0], norm1_w[0], w_in[0], cmp_pos_k[0], cmp_w1_k[0], cmp_w2_k[0], cmp_pos_v[0],
        cmp_w1_v[0], cmp_w2_v[0], gla_w_gate2[0], gla_b_gate[0], gla_norm_w[0], w_out[0], norm2_w[0],
        w_router_group[0], b_router_group[0], w_router_expert[0], b_router_expert[0],
        w_expert_gate[0], w_expert_up[0], w_expert_down[0])
    out = _combine(dest_flat, y, x1, route2, mod4, norm_f_w.reshape(1, d), s)
    return out.reshape(bsz, s, d)
```

```python
import functools

import numpy as np
import jax
import jax.numpy as jnp
from jax import lax
from jax.experimental import pallas as pl
from jax.experimental.pallas import tpu as pltpu

F32 = jnp.float32
BF16 = jnp.bfloat16
I32 = jnp.int32
HI = lax.Precision.HIGHEST

D_MODEL = 2048
NSA_HEAD_DIM = 64
NSA_HEADS = 16
NSA_KV_HEADS = 4
NSA_Q_PER_KV = 4
CMP_BLOCK = 32
CMP_STRIDE = 16
SEL_BLOCK = 64
N_SEL = 16
WINDOW = 512
N_BRANCH = 3
GLA_HEADS = 4
GLA_DV = 256
GLA_DK = 128
GLA_GATE_RANK = 16
GLA_GATE_NORM = 16.0
GLA_CHUNK = 64
GLA_SUB = 16
N_GROUPS = 4
EXPERTS_PER_GROUP = 8
N_EXPERTS = 32
TOP_K = 2
EXPERT_FF = 512
EPS = 1e-6
NEG = -1e30
FORCE = 1e30
LOG2E = 1.4426950408889634
MASKED = 2.0 ** 100

NSA_Q_COLS = NSA_HEADS * NSA_HEAD_DIM
NSA_KV_COLS = 2 * N_BRANCH * NSA_KV_HEADS * NSA_HEAD_DIM
NSA_GATE_COLS = N_BRANCH * NSA_HEADS
NSA_COLS = NSA_Q_COLS + NSA_KV_COLS
NSA_SLOTS = NSA_COLS // NSA_HEAD_DIM
GLA_Q_OFF = 0
GLA_K_OFF = GLA_HEADS * GLA_DK
GLA_V_OFF = 2 * GLA_HEADS * GLA_DK
GLA_OG_OFF = GLA_V_OFF + GLA_HEADS * GLA_DV
GLA_MISC_OFF = GLA_OG_OFF + GLA_HEADS * GLA_DV
LANES = 128
GLA_COLS = GLA_MISC_OFF + LANES

VMEM_LIMIT = 56 * 1024 * 1024

ADA_TN = 768
PREP_TR = 256
PREP_STEP = 512
INPROJ_TM = 256
INPROJ_TN = 512
NSA_TQ = 256
NSA_TK = 256
GLA_TS = 512
OUT_TM = 256
RANK_TM = 256
MOE_TB = 256
DISPATCH_TM = 256
COMB_TM = 256
DMA_UNROLL = 8


def _cparams(sem):
    return pltpu.CompilerParams(dimension_semantics=sem, vmem_limit_bytes=VMEM_LIMIT)


def _adaln_kernel(ct_ref, w_ref, b_ref, o_ref, s_scr):
    nb = ct_ref.shape[1]
    kdim, tn = w_ref.shape

    @pl.when(pl.program_id(0) == 0)
    def _():
        ct = ct_ref[...]
        s = ct * jax.nn.sigmoid(ct)
        for b in range(nb):
            s_scr[b] = jnp.broadcast_to(s[:, b:b + 1], (kdim, LANES))

    def body(k, accs):
        r = pl.multiple_of(k * 8, 8)
        w8 = w_ref[pl.ds(r, 8), :]
        out = []
        for b, acc in enumerate(accs):
            s8 = s_scr[b, pl.ds(r, 8), :]
            out.append(acc + w8 * jnp.concatenate([s8] * (tn // LANES), axis=1))
        return tuple(out)

    accs = lax.fori_loop(0, kdim // 8, body, tuple(jnp.zeros((8, tn), F32) for _ in range(nb)), unroll=2)
    bias = b_ref[...]
    for b, acc in enumerate(accs):
        o_ref[b:b + 1, :] = jnp.sum(acc, axis=0, keepdims=True) + bias


def _adaln(c, w, b):
    nb, d = c.shape
    n = w.shape[1]
    return pl.pallas_call(
        _adaln_kernel,
        grid=(n // ADA_TN,),
        in_specs=[pl.BlockSpec((d, nb), lambda j: (0, 0)),
                  pl.BlockSpec((d, ADA_TN), lambda j: (0, j)),
                  pl.BlockSpec((1, ADA_TN), lambda j: (0, j))],
        out_specs=pl.BlockSpec((nb, ADA_TN), lambda j: (0, j)),
        out_shape=jax.ShapeDtypeStruct((nb, n), F32),
        scratch_shapes=[pltpu.VMEM((nb, d, LANES), F32)],
        compiler_params=_cparams(("arbitrary",)),
        name="adaln",
    )(c.T, w, b.reshape(1, n))


def _modulated_norm(x, nw, sc, sh):
    ms = jnp.mean(x * x, axis=-1, keepdims=True)
    h = x * lax.rsqrt(ms + EPS) * nw
    return h * (1.0 + sc) + sh


def _prep_w_in_kernel(wt_ref, wn_ref, wg_ref, *, o_gate, o_gla, o_lr):
    tr = wt_ref.shape[1]
    step = PREP_STEP

    def put(dst, c0, rows):
        dst[:, c0:c0 + rows.shape[0]] = rows.T.astype(BF16)

    for c in range(0, o_gate, step):
        put(wn_ref, c, wt_ref[c:c + step, :])
    for c in range(0, o_lr - o_gla, step):
        put(wg_ref, c, wt_ref[o_gla + c:o_gla + c + step, :])
    pad = wg_ref.shape[1] - (wt_ref.shape[0] - o_gate)
    misc = jnp.concatenate([wt_ref[o_gate:o_gla, :], wt_ref[o_lr:, :], jnp.zeros((pad, tr), F32)], axis=0)
    put(wg_ref, o_lr - o_gla, misc)


def _prep_w_in(w_in_t, o_gate, o_gla, o_lr):
    n, d = w_in_t.shape
    tr = PREP_TR
    assert o_gate % PREP_STEP == 0 and (o_lr - o_gla) % PREP_STEP == 0
    return pl.pallas_call(
        functools.partial(_prep_w_in_kernel, o_gate=o_gate, o_gla=o_gla, o_lr=o_lr),
        grid=(d // tr,),
        in_specs=[pl.BlockSpec((n, tr), lambda i: (0, i))],
        out_specs=[pl.BlockSpec((tr, NSA_COLS), lambda i: (i, 0)), pl.BlockSpec((tr, GLA_COLS), lambda i: (i, 0))],
        out_shape=[jax.ShapeDtypeStruct((d, NSA_COLS), BF16), jax.ShapeDtypeStruct((d, GLA_COLS), BF16)],
        compiler_params=_cparams(("parallel",)),
        name="prep_w_in",
    )(w_in_t)


def _inproj_kernel(x_ref, sc_ref, sh_ref, nw_ref, wn_ref, wg_ref, on_ref, og_ref):
    h = _modulated_norm(x_ref[0], nw_ref[...], sc_ref[0, 0], sh_ref[0, 0]).astype(BF16)
    dh = NSA_HEAD_DIM
    tn = INPROJ_TN
    for c in range(wn_ref.shape[1] // tn):
        acc = jnp.dot(h, wn_ref[:, c * tn:(c + 1) * tn], preferred_element_type=F32)
        for u in range(tn // dh):
            on_ref[0, c * (tn // dh) + u] = acc[:, u * dh:(u + 1) * dh]
    og_ref[0] = jnp.dot(h, wg_ref[...], preferred_element_type=F32)


def _inproj(x, mod4, nw, w_nsa, w_gla, *, sc_idx, sh_idx):
    bsz, s, d = x.shape
    n_nsa, n_gla = w_nsa.shape[1], w_gla.shape[1]
    tm = INPROJ_TM
    dh = NSA_HEAD_DIM
    return pl.pallas_call(
        _inproj_kernel,
        grid=(bsz, s // tm),
        in_specs=[pl.BlockSpec((1, tm, d), lambda b, i: (b, i, 0)),
                  pl.BlockSpec((1, 1, 1, d), lambda b, i: (b, sc_idx, 0, 0)),
                  pl.BlockSpec((1, 1, 1, d), lambda b, i: (b, sh_idx, 0, 0)),
                  pl.BlockSpec((1, d), lambda b, i: (0, 0)),
                  pl.BlockSpec((d, n_nsa), lambda b, i: (0, 0), pipeline_mode=pl.Buffered(1)),
                  pl.BlockSpec((d, n_gla), lambda b, i: (0, 0), pipeline_mode=pl.Buffered(1))],
        out_specs=[pl.BlockSpec((1, n_nsa // dh, tm, dh), lambda b, i: (b, 0, i, 0)),
                   pl.BlockSpec((1, tm, n_gla), lambda b, i: (b, i, 0))],
        out_shape=[jax.ShapeDtypeStruct((bsz, n_nsa // dh, s, dh), F32),
                   jax.ShapeDtypeStruct((bsz, s, n_gla), F32)],
        compiler_params=_cparams(("parallel", "parallel")),
        name="inproj",
    )(x, mod4, mod4, nw, w_nsa, w_gla)


def _compress_kernel(a_ref, pos_ref, w1_ref, w2_ref, o_ref):
    nch = a_ref.shape[2] // CMP_STRIDE
    a = jnp.concatenate([a_ref[0, 0, pl.ds(t, nch, stride=CMP_STRIDE), :] for t in range(CMP_STRIDE)], axis=1)
    pos = pos_ref[0]
    half = a.shape[1]
    y1 = jnp.dot(a + pos[0:1], w1_ref[0, :half, :], precision=HI, preferred_element_type=F32)
    y2 = jnp.dot(a + pos[1:2], w1_ref[0, half:, :], precision=HI, preferred_element_type=F32)
    nrow = a.shape[0]
    h = y1 + pltpu.roll(y2, nrow - 1, axis=0)
    out = jnp.dot(jax.nn.gelu(h), w2_ref[0], precision=HI, preferred_element_type=F32)
    row = lax.broadcasted_iota(I32, out.shape, 0)
    o_ref[0, 0] = jnp.where(row < nrow - 1, out, 0.0)


def _compress(proj_nsa, pos, w1, w2):
    bsz, _, s, dh = proj_nsa.shape
    nslot = 2 * NSA_KV_HEADS
    nch = s // CMP_STRIDE
    cw = CMP_STRIDE * dh
    return pl.pallas_call(
        _compress_kernel,
        grid=(bsz, nslot),
        in_specs=[pl.BlockSpec((1, 1, s, dh), lambda b, t: (b, NSA_HEADS + t, 0, 0)),
                  pl.BlockSpec((1, 2, cw), lambda b, t: (t // NSA_KV_HEADS, 0, 0)),
                  pl.BlockSpec((1, 2 * cw, dh), lambda b, t: (t // NSA_KV_HEADS, 0, 0)),
                  pl.BlockSpec((1, dh, dh), lambda b, t: (t // NSA_KV_HEADS, 0, 0))],
        out_specs=pl.BlockSpec((1, 1, nch, dh), lambda b, t: (b, t, 0, 0)),
        out_shape=jax.ShapeDtypeStruct((bsz, nslot, nch, dh), F32),
        compiler_params=_cparams(("parallel", "parallel")),
        name="nsa_compress",
    )(proj_nsa, pos, w1, w2)


def _hi_lo(x):
    hi = x.astype(BF16)
    return hi, (x - hi.astype(F32)).astype(BF16)


def _nt_dot(a, b, **kw):
    return lax.dot_general(a, b, (((1,), (1,)), ((), ())), preferred_element_type=F32, **kw)


def _nsa_kernel(slopes_ref, q_ref, kc_ref, vc_ref, ks_ref, vs_ref, kw_ref, vw_ref, gate_ref, o_ref,
                ksb, vst, kwb, vwt, q4_scr, notsel_scr, *scr):
    g = pl.program_id(1)
    qi = pl.program_id(2)
    tq_n = q_ref.shape[2]
    dh = NSA_HEAD_DIM
    nr = NSA_Q_PER_KV
    seq = ks_ref.shape[2]
    tk_n = NSA_TK
    nb = seq // SEL_BLOCK

    @pl.when(qi == 0)
    def _():
        row = lax.broadcasted_iota(I32, (seq, dh), 0)
        lane = lax.broadcasted_iota(I32, (seq, dh), 1)
        blk = row // SEL_BLOCK
        pos = jnp.where((lane >= nb) & (lane < nb + 3), (blk * SEL_BLOCK).astype(F32),
                        jnp.where((lane >= nb + 3) & (lane < nb + 6), (row % SEL_BLOCK).astype(F32), 0.0))
        ksb[...] = jnp.concatenate([jnp.where(lane == blk, -MASKED, pos), ks_ref[0, 0]], axis=1).astype(BF16)
        kwb[...] = jnp.concatenate([pos, kw_ref[0, 0]], axis=1).astype(BF16)
        for c in range(seq // tk_n):
            rows = slice(c * tk_n, (c + 1) * tk_n)
            for src, dst in ((vs_ref, vst), (vw_ref, vwt)):
                v = src[0, 0, rows, :]
                dst[c] = jnp.concatenate([v, v], axis=1).T[:dh].astype(BF16)

    t0 = qi * tq_n
    tq = t0 + lax.broadcasted_iota(I32, (1, tq_n), 1)
    tq_f = tq.astype(F32)
    slopes = [slopes_ref[g * nr + r] for r in range(nr)]
    scale = dh ** -0.5
    q_t = []
    for pair in range(nr // 2):
        both = jnp.concatenate([q_ref[0, 2 * pair], q_ref[0, 2 * pair + 1]], axis=1).T * scale
        q_t += [both[:dh], both[dh:]]

    ncp = kc_ref.shape[2]
    kc = kc_ref[0, 0]
    vc = vc_ref[0, 0]
    vc_t = jnp.concatenate([vc, vc], axis=1).T[:dh].astype(BF16)
    n_sub = lax.broadcasted_iota(I32, (ncp, 1), 0)
    blk_end = n_sub * CMP_STRIDE + (CMP_BLOCK - 1)
    center = n_sub.astype(F32) * CMP_STRIDE + (CMP_BLOCK - 1) / 2.0
    valid_c = blk_end <= tq
    dist_c = tq_f - center
    psum = jnp.zeros((ncp, tq_n), F32)
    o_c = []
    kc_hi, kc_lo = _hi_lo(kc)
    kc_cat = jnp.concatenate([kc_hi, kc_lo, kc_hi, jnp.zeros_like(kc_hi)], axis=1)
    for r in range(nr):
        q_hi, q_lo = _hi_lo(q_t[r])
        q_cat = jnp.concatenate([q_hi, q_hi, q_lo, jnp.zeros_like(q_hi)], axis=0)
        s = jnp.dot(kc_cat, q_cat, preferred_element_type=F32)
        s = jnp.where(valid_c, s - slopes[r] * dist_c, NEG)
        e = jnp.exp(s - jnp.max(s, axis=0, keepdims=True))
        p = jnp.where(valid_c, e / jnp.sum(e, axis=0, keepdims=True), 0.0)
        psum = psum + p
        o_c.append(jnp.dot(vc_t, p.astype(BF16), preferred_element_type=F32))

    n_sel = min(N_SEL, nb)
    notsel_scr[...] = jnp.zeros(notsel_scr.shape, F32)

    @pl.when((qi + 1) * tq_n > n_sel * SEL_BLOCK)
    def _():
        rowj = lax.broadcasted_iota(I32, (LANES, ncp), 0) * SEL_BLOCK
        coln = lax.broadcasted_iota(I32, (LANES, ncp), 1) * CMP_STRIDE
        overlap = jnp.where((coln < rowj + SEL_BLOCK) & (coln + CMP_BLOCK > rowj)
                            & (coln < (ncp - 1) * CMP_STRIDE) & (rowj < nb * SEL_BLOCK), 1.0, 0.0)
        ov = overlap.astype(BF16)
        imp = jnp.dot(jnp.concatenate([ov, ov], axis=1), jnp.concatenate(_hi_lo(psum), axis=0),
                      preferred_element_type=F32)[:nb]
        j_sub = lax.broadcasted_iota(I32, (nb, 1), 0)
        qblk = tq // SEL_BLOCK
        forced = (j_sub == 0) | (j_sub == qblk) | (j_sub == qblk - 1)
        imp = jnp.where(forced, FORCE, jnp.where(j_sub <= qblk, imp, NEG))
        cnt = jnp.zeros((nb, tq_n), F32)
        for i in range(nb):
            ci = imp[i:i + 1, :]
            tie = jnp.where(j_sub > i, 1.0, 0.0)
            cnt = cnt + jnp.where(ci > imp, 1.0, jnp.where(ci == imp, tie, 0.0))
        notsel_scr[...] = jnp.where(cnt < float(n_sel), 0.0, 1.0)

    notsel = notsel_scr[...]

    sub_h = lax.broadcasted_iota(I32, (dh - nb, 1), 0)
    for r in range(nr):
        scol = jnp.zeros((dh - nb, 1), F32)
        for i in range(3):
            piece = slopes_ref[(i + 1) * NSA_HEADS + g * nr + r]
            scol = jnp.where((sub_h == i) | (sub_h == 3 + i), piece, scol)
        q4_scr[:, r * tq_n:(r + 1) * tq_n] = jnp.concatenate(
            [notsel, jnp.broadcast_to(scol, (dh - nb, tq_n)), q_t[r] * LOG2E], axis=0).astype(BF16)

    nwt = WINDOW // tk_n
    nbuf = nwt + 2
    stats_s, stats_w = scr[0:3], scr[3:6]
    s_buf, p_buf, a_buf = scr[6:6 + nbuf], scr[6 + nbuf:6 + 2 * nbuf], scr[6 + 2 * nbuf:6 + 3 * nbuf]
    for m_ref, l_ref, acc_ref in (stats_s, stats_w):
        m_ref[...] = jnp.full(m_ref.shape, NEG, F32)
        l_ref[...] = jnp.zeros(l_ref.shape, F32)
        acc_ref[...] = jnp.zeros(acc_ref.shape, F32)
    key_i = lax.broadcasted_iota(I32, (tk_n, LANES), 0)
    qry_j = lax.broadcasted_iota(I32, (tk_n, LANES), 1)

    def scores(k_ref, kt, buf):
        k_tile = k_ref[kt * tk_n:(kt + 1) * tk_n, :]
        s_buf[buf][...] = jnp.dot(k_tile, q4_scr[...], preferred_element_type=F32)

    def softmax(buf, mode, stats):
        m_ref, l_ref, _ = stats
        for cb in range(nr * tq_n // LANES):
            cols = slice(cb * LANES, (cb + 1) * LANES)
            s = s_buf[buf][:, cols]
            if mode is not None:
                j = qry_j + (cb * LANES) % tq_n
                s = jnp.where(key_i <= j if mode == "causal" else key_i > j, s, -MASKED)
            m_prev = m_ref[:, cols]
            m_new = jnp.maximum(m_prev, jnp.max(s, axis=0, keepdims=True))
            alpha = jnp.exp2(m_prev - m_new)
            p = jnp.exp2(s - m_new)
            l_ref[:, cols] = alpha * l_ref[:, cols] + jnp.sum(p, axis=0, keepdims=True)
            m_ref[:, cols] = m_new
            a_buf[buf][:, cols] = alpha
            p_buf[buf][:, cols] = p.astype(BF16)

    def values(vt_ref, kt, buf, stats):
        acc_ref = stats[2]
        pv = jnp.dot(vt_ref[kt], p_buf[buf][...], preferred_element_type=F32)
        acc_ref[...] = acc_ref[...] * a_buf[buf][...] + pv

    def batch(jobs):
        for i in range(min(nbuf, len(jobs))):
            scores(jobs[i][0], jobs[i][2], i)
        for i, (_, vt_ref, kt, mode, stats) in enumerate(jobs):
            softmax(i % nbuf, mode, stats)
            values(vt_ref, kt, i % nbuf, stats)
            if i + nbuf < len(jobs):
                scores(jobs[i + nbuf][0], jobs[i + nbuf][2], i % nbuf)

    for k in range(seq // tq_n):
        @pl.when(qi == k)
        def _(k=k):
            jobs = [(ksb, vst, t, None, stats_s) for t in range(k)]
            jobs += [(kwb, vwt, k - back, "band" if back == nwt else None, stats_w)
                     for back in range(min(nwt, k), 0, -1)]
            jobs += [(kwb, vwt, k, "causal", stats_w), (ksb, vst, k, "causal", stats_s)]
            batch(jobs)

    gsel = jnp.where(lax.broadcasted_iota(I32, (LANES, LANES), 0)
                     == lax.broadcasted_iota(I32, (LANES, LANES), 1) + g * (nr * N_BRANCH), 1.0, 0.0)
    gs = gsel.astype(BF16)
    gates = jax.nn.sigmoid(jnp.dot(jnp.concatenate(_hi_lo(gate_ref[0]), axis=1), jnp.concatenate([gs, gs], axis=0),
                                   preferred_element_type=F32)).T
    (_, l_s, acc_s), (_, l_w, acc_w) = stats_s, stats_w
    for pair in range(nr // 2):
        o_t = []
        for r in (2 * pair, 2 * pair + 1):
            c0 = r * N_BRANCH
            cols = slice(r * tq_n, (r + 1) * tq_n)
            o_t.append(gates[c0:c0 + 1, :] * o_c[r] + (gates[c0 + 1:c0 + 2, :] / l_s[:, cols]) * acc_s[:, cols]
                       + (gates[c0 + 2:c0 + 3, :] / l_w[:, cols]) * acc_w[:, cols])
        o_ref[0, :, pair * LANES:(pair + 1) * LANES] = jnp.concatenate(o_t, axis=0).T


def _nsa(slopes, proj_nsa, kvc, proj_gla):
    bsz, _, s, dh = proj_nsa.shape
    g_n, nr = NSA_KV_HEADS, NSA_Q_PER_KV
    tq = NSA_TQ
    tk = NSA_TK
    assert tq == tk and WINDOW % tk == 0 and 2 * dh == LANES
    ncp = kvc.shape[2]
    kv0 = NSA_HEADS
    nq = nr * tq
    nbuf = WINDOW // tk + 2

    def kv_spec(i):
        return pl.BlockSpec((1, 1, s, dh), lambda b, g, q, i=i: (b, kv0 + i * g_n + g, 0, 0))

    return pl.pallas_call(
        _nsa_kernel,
        grid=(bsz, g_n, s // tq),
        in_specs=[pl.BlockSpec(memory_space=pltpu.SMEM),
                  pl.BlockSpec((1, nr, tq, dh), lambda b, g, q: (b, g, q, 0)),
                  pl.BlockSpec((1, 1, ncp, dh), lambda b, g, q: (b, g, 0, 0)),
                  pl.BlockSpec((1, 1, ncp, dh), lambda b, g, q: (b, g_n + g, 0, 0)),
                  kv_spec(2), kv_spec(3), kv_spec(4), kv_spec(5),
                  pl.BlockSpec((1, tq, LANES), lambda b, g, q: (b, q, GLA_MISC_OFF // LANES))],
        out_specs=pl.BlockSpec((1, tq, nr * dh), lambda b, g, q: (b, q, g)),
        out_shape=jax.ShapeDtypeStruct((bsz, s, NSA_HEADS * dh), F32),
        scratch_shapes=[pltpu.VMEM((s, LANES), BF16), pltpu.VMEM((s // tk, dh, tk), BF16)] * 2
        + [pltpu.VMEM((LANES, nq), BF16), pltpu.VMEM((s // SEL_BLOCK, tq), F32)]
        + [pltpu.VMEM((1, nq), F32), pltpu.VMEM((1, nq), F32), pltpu.VMEM((dh, nq), F32)] * 2
        + [pltpu.VMEM((tk, nq), F32)] * nbuf + [pltpu.VMEM((tk, nq), BF16)] * nbuf + [pltpu.VMEM((1, nq), F32)] * nbuf,
        compiler_params=_cparams(("parallel", "parallel", "arbitrary")),
        name="nsa_attention",
    )(slopes, proj_nsa, kvc, kvc, proj_nsa, proj_nsa, proj_nsa, proj_nsa, proj_gla)


def _gla_kernel(q_ref, k_ref, v_ref, og_ref, lr_ref, wg_ref, bg_ref, nw_ref, o_ref, st_scr, la_scr, b_scr):
    rows_n = q_ref.shape[1]
    c_n, sub = GLA_CHUNK, GLA_SUB
    nh, dk, dv = GLA_HEADS, GLA_DK, GLA_DV

    @pl.when(pl.program_id(1) == 0)
    def _():
        st_scr[...] = jnp.zeros(st_scr.shape, F32)

    z = jnp.dot(lr_ref[0], wg_ref[...], precision=HI, preferred_element_type=F32) + bg_ref[...]
    la_scr[...] = (jnp.minimum(z, 0.0) - jnp.log1p(jnp.exp(-jnp.abs(z)))) * (1.0 / GLA_GATE_NORM)
    tril = jnp.where(lax.broadcasted_iota(I32, (c_n, c_n), 0) >= lax.broadcasted_iota(I32, (c_n, c_n), 1), 1.0, 0.0)
    row_c = lax.broadcasted_iota(I32, (c_n, 1), 0)
    row_s = lax.broadcasted_iota(I32, (sub, 1), 0)
    lane_c = lax.broadcasted_iota(I32, (1, c_n), 1)
    nw = nw_ref[...]
    hk = [slice(h * dk, (h + 1) * dk) for h in range(nh)]
    hv = [slice(h * dv, (h + 1) * dv) for h in range(nh)]

    def chunk(c, carry):
        r0 = pl.multiple_of(c * c_n, c_n)
        rows = pl.ds(r0, c_n)
        qc = q_ref[0, rows, :] * (dk ** -0.5)
        kc = k_ref[0, rows, :]
        vc = [v_ref[0, rows, hv[h]].astype(BF16) for h in range(nh)]
        b = jnp.dot(tril, la_scr[rows, :], precision=HI, preferred_element_type=F32)
        b_scr[...] = b
        st = [st_scr[h] for h in range(nh)]
        q_e = (qc * jnp.exp(b)).astype(BF16)
        o = [_nt_dot(q_e[:, hk[h]], st[h].astype(BF16)) for h in range(nh)]
        strips = [[] for _ in range(nh)]
        for blk in range(c_n // sub):
            lo = blk * sub
            q_i = qc[lo:lo + sub]
            b_i = b[lo:lo + sub]
            a = [jnp.zeros((sub, c_n), F32) for _ in range(nh)]
            if blk > 0:
                b_r = b_scr[lo - 1:lo, :]
                q_d = (q_i * jnp.exp(b_i - b_r)).astype(BF16)
                k_d = (kc * jnp.exp(jnp.where(row_c < lo, b_r - b, -jnp.inf))).astype(BF16)
                a = [_nt_dot(q_d[:, hk[h]], k_d[:, hk[h]]) for h in range(nh)]
            for j in range(sub):
                b_j = b_scr[lo + j:lo + j + 1, :]
                k_j = k_ref[0, pl.ds(r0 + lo + j, 1), :]
                prod = q_i * k_j * jnp.exp(jnp.where(row_s >= j, b_i - b_j, -jnp.inf))
                for h in range(nh):
                    col = jnp.sum(prod[:, hk[h]], axis=-1, keepdims=True)
                    a[h] = jnp.where(lane_c == lo + j, col, a[h]) if blk == 0 else (
                        a[h] + jnp.where(lane_c == lo + j, col, 0.0))
            for h in range(nh):
                strips[h].append(a[h])
        for h in range(nh):
            attn = jnp.concatenate(strips[h], axis=0)
            o[h] = o[h] + jnp.dot(attn.astype(BF16), vc[h], preferred_element_type=F32)
        b_last = b_scr[c_n - 1:c_n, :]
        k_dec = (kc * jnp.exp(b_last - b)).astype(BF16)
        decay = jnp.exp(b_last)
        for h in range(nh):
            st_scr[h] = st[h] * decay[:, hk[h]] + lax.dot_general(
                vc[h], k_dec[:, hk[h]], (((0,), (0,)), ((), ())), preferred_element_type=F32)
        for h in range(nh):
            og = og_ref[0, rows, hv[h]]
            on = o[h] * lax.rsqrt(jnp.mean(o[h] * o[h], axis=-1, keepdims=True) + EPS) * nw
            o_ref[0, rows, hv[h]] = on * (og * jax.nn.sigmoid(og))
        return carry

    lax.fori_loop(0, rows_n // c_n, chunk, 0)


def _gla(proj_gla, wg_pad, bg, nw):
    bsz, s, _ = proj_gla.shape
    nh, dk, dv = GLA_HEADS, GLA_DK, GLA_DV
    ts = GLA_TS
    wk, wv = nh * dk, nh * dv
    return pl.pallas_call(
        _gla_kernel,
        grid=(bsz, s // ts),
        in_specs=[pl.BlockSpec((1, ts, wk), lambda b, i: (b, i, GLA_Q_OFF // wk)),
                  pl.BlockSpec((1, ts, wk), lambda b, i: (b, i, GLA_K_OFF // wk)),
                  pl.BlockSpec((1, ts, wv), lambda b, i: (b, i, GLA_V_OFF // wv)),
                  pl.BlockSpec((1, ts, wv), lambda b, i: (b, i, GLA_OG_OFF // wv)),
                  pl.BlockSpec((1, ts, LANES), lambda b, i: (b, i, GLA_MISC_OFF // LANES)),
                  pl.BlockSpec((LANES, wk), lambda b, i: (0, 0)),
                  pl.BlockSpec((1, wk), lambda b, i: (0, 0)),
                  pl.BlockSpec((1, dv), lambda b, i: (0, 0))],
        out_specs=pl.BlockSpec((1, ts, wv), lambda b, i: (b, i, 0)),
        out_shape=jax.ShapeDtypeStruct((bsz, s, wv), F32),
        scratch_shapes=[pltpu.VMEM((nh, dv, dk), F32), pltpu.VMEM((ts, wk), F32), pltpu.VMEM((GLA_CHUNK, wk), F32)],
        compiler_params=_cparams(("parallel", "arbitrary")),
        name="gla",
    )(proj_gla, proj_gla, proj_gla, proj_gla, proj_gla, wg_pad, bg, nw)


def _outproj_kernel(nsa_ref, gla_ref, x_ref, wo_ref, g1_ref, sc_ref, sh_ref, nw_ref, wr_ref, br_ref,
                    x1_ref, h_ref, route_ref):
    half = nsa_ref.shape[2]
    acc = jnp.dot(nsa_ref[0].astype(BF16), wo_ref[:half, :], preferred_element_type=F32)
    acc = acc + jnp.dot(gla_ref[0].astype(BF16), wo_ref[half:, :], preferred_element_type=F32)
    x1 = x_ref[0] + g1_ref[0, 0] * acc
    x1_ref[0] = x1
    h = _modulated_norm(x1, nw_ref[...], sc_ref[0, 0], sh_ref[0, 0])
    h_ref[0] = h
    h_hi = h.astype(BF16)
    h_lo = (h - h_hi.astype(F32)).astype(BF16)
    t = jnp.dot(h_hi, wr_ref[...], preferred_element_type=F32)
    logits = (t[:, :LANES] + t[:, LANES:] + jnp.dot(h_lo, wr_ref[:, :LANES], preferred_element_type=F32)
              + br_ref[...])
    lane = lax.broadcasted_iota(I32, (1, LANES), 1)
    ninf = -jnp.inf
    is_g = (lane >= N_EXPERTS) & (lane < N_EXPERTS + N_GROUPS)
    gl = jnp.where(is_g, logits, ninf)
    ge = jnp.exp(gl - jnp.max(gl, axis=-1, keepdims=True))
    gp = ge / jnp.sum(ge, axis=-1, keepdims=True)
    gp_max = jnp.max(gp, axis=-1, keepdims=True)
    grp = jnp.min(jnp.where((gp == gp_max) & is_g, lane - N_EXPERTS, LANES), axis=-1, keepdims=True)
    in_grp = (lane // EXPERTS_PER_GROUP == grp) & (lane < N_EXPERTS)
    el = jnp.where(in_grp, logits, ninf)
    v1 = jnp.max(el, axis=-1, keepdims=True)
    i1 = jnp.min(jnp.where(el == v1, lane, LANES), axis=-1, keepdims=True)
    el2 = jnp.where(lane == i1, ninf, el)
    v2 = jnp.max(el2, axis=-1, keepdims=True)
    i2 = jnp.min(jnp.where(el2 == v2, lane, LANES), axis=-1, keepdims=True)
    e2 = jnp.exp(v2 - v1)
    den = 1.0 + e2
    w1 = gp_max * (1.0 / den)
    w2 = gp_max * (e2 / den)
    route_ref[0] = jnp.where(lane == 0, i1.astype(F32), jnp.where(lane == 1, i2.astype(F32), jnp.where(
        lane == 2, w1, jnp.where(lane == 3, w2, 0.0))))


def _outproj(o_nsa, o_gla, x, wo, mod4, nw, wr, br):
    bsz, s, d = x.shape
    tm = OUT_TM
    half = o_nsa.shape[2]

    def mod_spec(idx):
        return pl.BlockSpec((1, 1, 1, d), lambda b, i: (b, idx, 0, 0))

    row = lambda w: pl.BlockSpec((1, tm, w), lambda b, i: (b, i, 0))
    return pl.pallas_call(
        _outproj_kernel,
        grid=(bsz, s // tm),
        in_specs=[row(half), row(half), row(d),
                  pl.BlockSpec((2 * half, d), lambda b, i: (0, 0)),
                  mod_spec(2), mod_spec(4), mod_spec(3),
                  pl.BlockSpec((1, d), lambda b, i: (0, 0)),
                  pl.BlockSpec((d, 2 * LANES), lambda b, i: (0, 0)),
                  pl.BlockSpec((1, LANES), lambda b, i: (0, 0))],
        out_specs=[row(d), row(d), row(LANES)],
        out_shape=[jax.ShapeDtypeStruct((bsz, s, d), F32), jax.ShapeDtypeStruct((bsz, s, d), F32),
                   jax.ShapeDtypeStruct((bsz, s, LANES), F32)],
        compiler_params=_cparams(("parallel", "parallel")),
        name="outproj_router",
    )(o_nsa, o_gla, x, wo, mod4, mod4, mod4, nw, wr, br)


def _rank_kernel(route_ref, dest_ref, meta_ref, rank_scr):
    n = route_ref.shape[0]
    tm = RANK_TM
    lane_i = lax.broadcasted_iota(I32, (1, LANES), 1)
    lane = lane_i.astype(F32)
    strict = jnp.where(lax.broadcasted_iota(I32, (tm, tm), 0) > lax.broadcasted_iota(I32, (tm, tm), 1),
                       1.0, 0.0).astype(BF16)

    def two_lanes(a, b):
        return jnp.where(lane_i == 0, a, jnp.where(lane_i == 1, b, 0.0))

    def pick(e, table):
        return jnp.sum(jnp.where(lane == e, table, 0.0), axis=-1, keepdims=True)

    def count(i, seen):
        r0 = pl.multiple_of(i * tm, tm)
        rt = route_ref[pl.ds(r0, tm), :]
        e1, e2 = rt[:, 0:1], rt[:, 1:2]
        member = jnp.where(lane == e1, 1.0, jnp.where(lane == e2, 1.0, 0.0))
        before = jnp.dot(strict, member.astype(BF16), preferred_element_type=F32) + seen
        rank_scr[pl.ds(r0, tm), :] = two_lanes(pick(e1, before), pick(e2, before))
        return seen + jnp.sum(member, axis=0, keepdims=True)

    counts = lax.fori_loop(0, n // tm, count, jnp.zeros((1, LANES), F32))
    ntile = jnp.floor((counts + (MOE_TB - 1)) * (1.0 / MOE_TB))
    incl = jnp.where(lax.broadcasted_iota(I32, (LANES, LANES), 0) <= lax.broadcasted_iota(I32, (LANES, LANES), 1),
                     1.0, 0.0).astype(BF16)
    tile_end = jnp.dot(jnp.broadcast_to(ntile, (8, LANES)).astype(BF16), incl,
                       preferred_element_type=F32)[0:1]
    row_start = (tile_end - ntile) * MOE_TB

    def place(i, carry):
        r0 = pl.multiple_of(i * tm, tm)
        rt = route_ref[pl.ds(r0, tm), :]
        rk = rank_scr[pl.ds(r0, tm), :]
        d1 = pick(rt[:, 0:1], row_start) + rk[:, 0:1]
        d2 = pick(rt[:, 1:2], row_start) + rk[:, 1:2]
        dest_ref[pl.ds(r0, tm), :] = two_lanes(d1, d2).astype(I32)
        return carry

    lax.fori_loop(0, n // tm, place, 0)
    trow = lax.broadcasted_iota(I32, (meta_ref.shape[0], 1), 0).astype(F32)
    texp = jnp.sum(jnp.where((tile_end <= trow) & (lane_i < N_EXPERTS), 1.0, 0.0), axis=-1, keepdims=True)
    texp = jnp.minimum(texp, N_EXPERTS - 1.0)
    used = pick(N_EXPERTS - 1.0, tile_end)
    diag = lax.broadcasted_iota(I32, (meta_ref.shape[0], LANES), 0) == lane_i
    end_rows = jnp.sum(jnp.where(diag, tile_end, 0.0), axis=-1, keepdims=True)
    ntile_rows = jnp.sum(jnp.where(diag, ntile, 0.0), axis=-1, keepdims=True)
    meta_ref[...] = jnp.where(lane_i == 2, end_rows, jnp.where(lane_i == 3, ntile_rows, two_lanes(
        texp, jnp.broadcast_to(used, texp.shape)))).astype(I32)


def _rank(route):
    n = route.shape[0]
    return pl.pallas_call(
        _rank_kernel,
        out_shape=[jax.ShapeDtypeStruct((n, LANES), I32), jax.ShapeDtypeStruct((LANES, LANES), I32)],
        scratch_shapes=[pltpu.VMEM((n, LANES), F32)],
        compiler_params=pltpu.CompilerParams(vmem_limit_bytes=VMEM_LIMIT),
        name="moe_rank",
    )(route)


def _dispatch_kernel(dest_ref, ends_ref, h_ref, xs_ref, zero_scr, sem, zsem):
    i = pl.program_id(0)
    tm = h_ref.shape[0]
    tb = zero_scr.shape[0]

    @pl.when(i == 0)
    def _():
        zero_scr[...] = jnp.zeros(zero_scr.shape, F32)

        def zero_copy(e):
            r0 = pl.multiple_of((ends_ref[e] - 1) * tb, tb)
            return pltpu.make_async_copy(zero_scr, xs_ref.at[pl.ds(r0, tb)], zsem)

        def start(e, carry):
            @pl.when(ends_ref[N_EXPERTS + e] > 0)
            def _():
                zero_copy(e).start()
            return carry

        def wait(e, carry):
            @pl.when(ends_ref[N_EXPERTS + e] > 0)
            def _():
                zero_copy(e).wait()
            return carry

        def tail_copy(t):
            return pltpu.make_async_copy(zero_scr, xs_ref.at[pl.ds(pl.multiple_of(t * tb, tb), tb)], zsem)

        def tail_start(t, carry):
            tail_copy(t).start()
            return carry

        def tail_wait(t, carry):
            tail_copy(t).wait()
            return carry

        used = ends_ref[2 * N_EXPERTS]
        lax.fori_loop(0, N_EXPERTS, start, 0)
        lax.fori_loop(used, xs_ref.shape[0] // tb, tail_start, 0)
        lax.fori_loop(0, N_EXPERTS, wait, 0)
        lax.fori_loop(used, xs_ref.shape[0] // tb, tail_wait, 0)

    def issue(r, carry):
        p = (i * tm + r) * TOP_K
        for k in range(TOP_K):
            pltpu.make_async_copy(h_ref.at[pl.ds(r, 1)], xs_ref.at[pl.ds(dest_ref[p + k], 1)], sem).start()
        return carry

    lax.fori_loop(0, tm, issue, 0, unroll=DMA_UNROLL)
    for k in range(TOP_K):
        pltpu.make_async_copy(h_ref, xs_ref.at[pl.ds(0, tm)], sem).wait()


def _dispatch(dest_flat, ends_flat, h, cap):
    n, d = h.shape
    tm = DISPATCH_TM
    return pl.pallas_call(
        _dispatch_kernel,
        grid_spec=pltpu.PrefetchScalarGridSpec(
            num_scalar_prefetch=2, grid=(n // tm,),
            in_specs=[pl.BlockSpec((tm, d), lambda i, dst, ends: (i, 0))],
            out_specs=pl.BlockSpec(memory_space=pl.ANY),
            scratch_shapes=[pltpu.VMEM((MOE_TB, d), F32), pltpu.SemaphoreType.DMA(()), pltpu.SemaphoreType.DMA(())]),
        out_shape=jax.ShapeDtypeStruct((cap, d), F32),
        compiler_params=_cparams(("arbitrary",)),
        name="moe_dispatch",
    )(dest_flat, ends_flat, h)


def _ffn_kernel(meta_ref, x_ref, wg_ref, wu_ref, wd_ref, y_ref, wgb, wub, wdb):
    t = pl.program_id(0)
    ntile = pl.num_programs(0)
    e = meta_ref[t]
    e_prev = meta_ref[jnp.maximum(t - 1, 0)]
    active = t < meta_ref[ntile]

    @pl.when(active & ((t == 0) | (e != e_prev)))
    def _():
        wgb[...] = wg_ref[0].astype(BF16)
        wub[...] = wu_ref[0].astype(BF16)
        wdb[...] = wd_ref[0].astype(BF16)

    @pl.when(active)
    def _():
        x = x_ref[...].astype(BF16)
        gate = jnp.dot(x, wgb[...], preferred_element_type=F32)
        up = jnp.dot(x, wub[...], preferred_element_type=F32)
        act = (gate * jax.nn.sigmoid(gate)) * up
        y_ref[...] = jnp.dot(act.astype(BF16), wdb[...], preferred_element_type=F32)

    @pl.when(jnp.logical_not(active))
    def _():
        y_ref[...] = jnp.zeros(y_ref.shape, F32)


def _ffn(meta_flat, xs, wg, wu, wd):
    cap, d = xs.shape
    ff = wg.shape[2]
    tb = MOE_TB
    ntile = cap // tb
    return pl.pallas_call(
        _ffn_kernel,
        grid_spec=pltpu.PrefetchScalarGridSpec(
            num_scalar_prefetch=1, grid=(ntile,),
            in_specs=[pl.BlockSpec((tb, d), lambda t, m: (jnp.minimum(t, m[ntile] - 1), 0)),
                      pl.BlockSpec((1, d, ff), lambda t, m: (m[t], 0, 0)),
                      pl.BlockSpec((1, d, ff), lambda t, m: (m[t], 0, 0)),
                      pl.BlockSpec((1, ff, d), lambda t, m: (m[t], 0, 0))],
            out_specs=pl.BlockSpec((tb, d), lambda t, m: (t, 0)),
            scratch_shapes=[pltpu.VMEM((d, ff), BF16), pltpu.VMEM((d, ff), BF16), pltpu.VMEM((ff, d), BF16)]),
        out_shape=jax.ShapeDtypeStruct((cap, d), F32),
        compiler_params=_cparams(("arbitrary",)),
        name="moe_ffn",
    )(meta_flat, xs, wg, wu, wd)


def _combine_kernel(dest_ref, y_ref, x1_ref, route_ref, g2_ref, nf_ref, o_ref, ybuf, sem):
    i = pl.program_id(0)
    nstep = pl.num_programs(0)
    tm = x1_ref.shape[0]

    def issue(tile, slot):
        def body(r, carry):
            p = (tile * tm + r) * TOP_K
            for k in range(TOP_K):
                pltpu.make_async_copy(y_ref.at[pl.ds(dest_ref[p + k], 1)], ybuf.at[slot, k, pl.ds(r, 1)],
                                      sem.at[slot]).start()
            return carry

        lax.fori_loop(0, tm, body, 0, unroll=DMA_UNROLL)

    @pl.when(i == 0)
    def _():
        issue(0, 0)

    @pl.when(i + 1 < nstep)
    def _():
        issue(i + 1, (i + 1) % 2)

    slot = i % 2
    for k in range(TOP_K):
        pltpu.make_async_copy(y_ref.at[pl.ds(0, tm)], ybuf.at[slot, k], sem.at[slot]).wait()
    rt = route_ref[...]
    moe = rt[:, 2:3] * ybuf[slot, 0] + rt[:, 3:4] * ybuf[slot, 1]
    xo = x1_ref[...] + g2_ref[0, 0] * moe
    o_ref[...] = xo * lax.rsqrt(jnp.mean(xo * xo, axis=-1, keepdims=True) + EPS) * nf_ref[...]


def _combine(dest_flat, y, x1, route, mod4, nf, seq):
    n, d = x1.shape
    tm = COMB_TM
    tiles_per_seq = seq // tm
    return pl.pallas_call(
        _combine_kernel,
        grid_spec=pltpu.PrefetchScalarGridSpec(
            num_scalar_prefetch=1, grid=(n // tm,),
            in_specs=[pl.BlockSpec(memory_space=pl.ANY),
                      pl.BlockSpec((tm, d), lambda i, dst: (i, 0)),
                      pl.BlockSpec((tm, LANES), lambda i, dst: (i, 0)),
                      pl.BlockSpec((1, 1, 1, d), lambda i, dst: (i // tiles_per_seq, 5, 0, 0)),
                      pl.BlockSpec((1, d), lambda i, dst: (0, 0))],
            out_specs=pl.BlockSpec((tm, d), lambda i, dst: (i, 0)),
            scratch_shapes=[pltpu.VMEM((2, TOP_K, tm, d), F32), pltpu.SemaphoreType.DMA((2,))]),
        out_shape=jax.ShapeDtypeStruct((n, d), F32),
        compiler_params=_cparams(("arbitrary",)),
        name="moe_combine",
    )(dest_flat, y, x1, route, mod4, nf)


def _alibi_slopes():
    n = NSA_HEADS
    full = jnp.asarray(2.0 ** (-8.0 * np.arange(1, n + 1) / n), dtype=F32)
    pieces, rest = [], full * LOG2E
    for _ in range(3):
        piece = rest.astype(BF16).astype(F32)
        pieces.append(piece)
        rest = rest - piece
    return jnp.concatenate([full] + pieces)


def _layer(x, c, w_ada, b_ada, norm1_w, w_in, cmp_pos_k, cmp_w1_k, cmp_w2_k, cmp_pos_v, cmp_w1_v, cmp_w2_v,
           gla_w_gate2, gla_b_gate, gla_norm_w, w_out, norm2_w, w_rg, b_rg, w_re, b_re, w_eg, w_eu, w_ed):
    bsz, s, d = x.shape
    dh = NSA_HEAD_DIM
    mod4 = _adaln(c, w_ada, b_ada).reshape(bsz, 6, 1, d)

    o_gate = NSA_COLS
    o_gla = o_gate + NSA_GATE_COLS
    o_lr = o_gla + 2 * GLA_HEADS * GLA_DK + 2 * GLA_HEADS * GLA_DV
    w_nsa, w_gla = _prep_w_in(w_in.T, o_gate, o_gla, o_lr)
    nw1 = norm1_w.reshape(1, d)
    proj_nsa, proj_gla = _inproj(x, mod4, nw1, w_nsa, w_gla, sc_idx=1, sh_idx=0)

    pos = jnp.stack([cmp_pos_k, cmp_pos_v]).reshape(2, 2, CMP_STRIDE * dh)
    kvc = _compress(proj_nsa, pos, jnp.stack([cmp_w1_k, cmp_w1_v]), jnp.stack([cmp_w2_k, cmp_w2_v]))
    o_nsa = _nsa(_alibi_slopes(), proj_nsa, kvc, proj_gla)

    wg_pad = jnp.zeros((LANES, GLA_HEADS * GLA_DK), F32).at[
        NSA_GATE_COLS:NSA_GATE_COLS + GLA_GATE_RANK].set(gla_w_gate2)
    o_gla_out = _gla(proj_gla, wg_pad, gla_b_gate.reshape(1, -1), gla_norm_w.reshape(1, -1))

    wr = jnp.concatenate([w_re, w_rg, jnp.zeros((d, LANES - N_EXPERTS - N_GROUPS), F32)], axis=1)
    br = jnp.concatenate([b_re, b_rg, jnp.zeros((LANES - N_EXPERTS - N_GROUPS,), F32)]).reshape(1, LANES)
    wr_hi = wr.astype(BF16)
    wr_cat = jnp.concatenate([wr_hi, (wr - wr_hi.astype(F32)).astype(BF16)], axis=1)
    x1, h2, route = _outproj(o_nsa, o_gla_out, x, w_out.astype(BF16), mod4, norm2_w.reshape(1, d), wr_cat, br)

    n = bsz * s
    npair = n * TOP_K
    cap = npair + N_EXPERTS * MOE_TB
    ntile = cap // MOE_TB
    route2 = route.reshape(n, LANES)
    dest, meta = _rank(route2)
    dest_flat = dest[:, :TOP_K].reshape(npair)
    meta_flat = jnp.concatenate([meta[:ntile, 0], meta[:1, 1]])
    ends_flat = jnp.concatenate([meta[:N_EXPERTS, 2], meta[:N_EXPERTS, 3], meta[:1, 1]])
    xs = _dispatch(dest_flat, ends_flat, h2.reshape(n, d), cap)
    y = _ffn(meta_flat, xs, w_eg, w_eu, w_ed)
    return x1.reshape(n, d), y, dest_flat, route2, mod4


def kernel(x, c, w_ada, b_ada, norm1_w, w_in, cmp_pos_k, cmp_w1_k, cmp_w2_k, cmp_pos_v, cmp_w1_v, cmp_w2_v,
           gla_w_gate2, gla_b_gate, gla_norm_w, w_out, norm2_w, w_router_group, b_router_group, w_router_expert,
           b_router_expert, w_expert_gate, w_expert_up, w_expert_down, norm_f_w):
    bsz, s, d = x.shape
    assert w_ada.shape[0] == 1, "single layer"
    x1, y, dest_flat, route2, mod4 = _layer(
        x, c, w_ada[0], b_ada[0], norm1_w[0], w_in[0], cmp_pos_k[0], cmp_w1_k[0], cmp_w2_k[0], cmp_pos_v[0],
        cmp_w1_v[0], cmp_w2_v[0], gla_w_gate2[0], gla_b_gate[0], gla_norm_w[0], w_out[0], norm2_w[0],
        w_router_group[0], b_router_group[0], w_router_expert[0], b_router_expert[0],
        w_expert_gate[0], w_expert_up[0], w_expert_down[0])
    out = _combine(dest_flat, y, x1, route2, mod4, norm_f_w.reshape(1, d), s)
    return out.reshape(bsz, s, d)
```

```python
import functools

import numpy as np
import jax
import jax.numpy as jnp
from jax import lax
from jax.experimental import pallas as pl
from jax.experimental.pallas import tpu as pltpu

F32 = jnp.float32
BF16 = jnp.bfloat16
I32 = jnp.int32
HI = lax.Precision.HIGHEST

D_MODEL = 2048
NSA_HEAD_DIM = 64
NSA_HEADS = 16
NSA_KV_HEADS = 4
NSA_Q_PER_KV = 4
CMP_BLOCK = 32
CMP_STRIDE = 16
SEL_BLOCK = 64
N_SEL = 16
WINDOW = 512
N_BRANCH = 3
GLA_HEADS = 4
GLA_DV = 256
GLA_DK = 128
GLA_GATE_RANK = 16
GLA_GATE_NORM = 16.0
GLA_CHUNK = 64
GLA_SUB = 16
N_GROUPS = 4
EXPERTS_PER_GROUP = 8
N_EXPERTS = 32
TOP_K = 2
EXPERT_FF = 512
EPS = 1e-6
NEG = -1e30
FORCE = 1e30
LOG2E = 1.4426950408889634
MASKED = 2.0 ** 100

NSA_Q_COLS = NSA_HEADS * NSA_HEAD_DIM
NSA_KV_COLS = 2 * N_BRANCH * NSA_KV_HEADS * NSA_HEAD_DIM
NSA_GATE_COLS = N_BRANCH * NSA_HEADS
NSA_COLS = NSA_Q_COLS + NSA_KV_COLS
NSA_SLOTS = NSA_COLS // NSA_HEAD_DIM
GLA_Q_OFF = 0
GLA_K_OFF = GLA_HEADS * GLA_DK
GLA_V_OFF = 2 * GLA_HEADS * GLA_DK
GLA_OG_OFF = GLA_V_OFF + GLA_HEADS * GLA_DV
GLA_MISC_OFF = GLA_OG_OFF + GLA_HEADS * GLA_DV
LANES = 128
GLA_COLS = GLA_MISC_OFF + LANES

VMEM_LIMIT = 56 * 1024 * 1024

ADA_TN = 768
PREP_TR = 256
PREP_STEP = 512
INPROJ_TM = 256
INPROJ_TN = 512
NSA_TQ = 256
NSA_TK = 256
GLA_TS = 512
OUT_TM = 256
RANK_TM = 256
MOE_TB = 256
DISPATCH_TM = 256
COMB_TM = 256
DMA_UNROLL = 8


def _cparams(sem):
    return pltpu.CompilerParams(dimension_semantics=sem, vmem_limit_bytes=VMEM_LIMIT)


def _adaln_kernel(ct_ref, w_ref, b_ref, o_ref, s_scr):
    nb = ct_ref.shape[1]
    kdim, tn = w_ref.shape

    @pl.when(pl.program_id(0) == 0)
    def _():
        ct = ct_ref[...]
        s = ct * jax.nn.sigmoid(ct)
        for b in range(nb):
            s_scr[b] = jnp.broadcast_to(s[:, b:b + 1], (kdim, LANES))

    def body(k, accs):
        r = pl.multiple_of(k * 8, 8)
        w8 = w_ref[pl.ds(r, 8), :]
        out = []
        for b, acc in enumerate(accs):
            s8 = s_scr[b, pl.ds(r, 8), :]
            out.append(acc + w8 * jnp.concatenate([s8] * (tn // LANES), axis=1))
        return tuple(out)

    accs = lax.fori_loop(0, kdim // 8, body, tuple(jnp.zeros((8, tn), F32) for _ in range(nb)), unroll=2)
    bias = b_ref[...]
    for b, acc in enumerate(accs):
        o_ref[b:b + 1, :] = jnp.sum(acc, axis=0, keepdims=True) + bias


def _adaln(c, w, b):
    nb, d = c.shape
    n = w.shape[1]
    return pl.pallas_call(
        _adaln_kernel,
        grid=(n // ADA_TN,),
        in_specs=[pl.BlockSpec((d, nb), lambda j: (0, 0)),
                  pl.BlockSpec((d, ADA_TN), lambda j: (0, j)),
                  pl.BlockSpec((1, ADA_TN), lambda j: (0, j))],
        out_specs=pl.BlockSpec((nb, ADA_TN), lambda j: (0, j)),
        out_shape=jax.ShapeDtypeStruct((nb, n), F32),
        scratch_shapes=[pltpu.VMEM((nb, d, LANES), F32)],
        compiler_params=_cparams(("arbitrary",)),
        name="adaln",
    )(c.T, w, b.reshape(1, n))


def _modulated_norm(x, nw, sc, sh):
    ms = jnp.mean(x * x, axis=-1, keepdims=True)
    h = x * lax.rsqrt(ms + EPS) * nw
    return h * (1.0 + sc) + sh


def _prep_w_in_kernel(wt_ref, wn_ref, wg_ref, *, o_gate, o_gla, o_lr):
    tr = wt_ref.shape[1]
    step = PREP_STEP

    def put(dst, c0, rows):
        dst[:, c0:c0 + rows.shape[0]] = rows.T.astype(BF16)

    for c in range(0, o_gate, step):
        put(wn_ref, c, wt_ref[c:c + step, :])
    for c in range(0, o_lr - o_gla, step):
        put(wg_ref, c, wt_ref[o_gla + c:o_gla + c + step, :])
    pad = wg_ref.shape[1] - (wt_ref.shape[0] - o_gate)
    misc = jnp.concatenate([wt_ref[o_gate:o_gla, :], wt_ref[o_lr:, :], jnp.zeros((pad, tr), F32)], axis=0)
    put(wg_ref, o_lr - o_gla, misc)


def _prep_w_in(w_in_t, o_gate, o_gla, o_lr):
    n, d = w_in_t.shape
    tr = PREP_TR
    assert o_gate % PREP_STEP == 0 and (o_lr - o_gla) % PREP_STEP == 0
    return pl.pallas_call(
        functools.partial(_prep_w_in_kernel, o_gate=o_gate, o_gla=o_gla, o_lr=o_lr),
        grid=(d // tr,),
        in_specs=[pl.BlockSpec((n, tr), lambda i: (0, i))],
        out_specs=[pl.BlockSpec((tr, NSA_COLS), lambda i: (i, 0)), pl.BlockSpec((tr, GLA_COLS), lambda i: (i, 0))],
        out_shape=[jax.ShapeDtypeStruct((d, NSA_COLS), BF16), jax.ShapeDtypeStruct((d, GLA_COLS), BF16)],
        compiler_params=_cparams(("parallel",)),
        name="prep_w_in",
    )(w_in_t)


def _inproj_kernel(x_ref, sc_ref, sh_ref, nw_ref, wn_ref, wg_ref, on_ref, og_ref):
    h = _modulated_norm(x_ref[0], nw_ref[...], sc_ref[0, 0], sh_ref[0, 0]).astype(BF16)
    dh = NSA_HEAD_DIM
    tn = INPROJ_TN
    for c in range(wn_ref.shape[1] // tn):
        acc = jnp.dot(h, wn_ref[:, c * tn:(c + 1) * tn], preferred_element_type=F32)
        for u in range(tn // dh):
            on_ref[0, c * (tn // dh) + u] = acc[:, u * dh:(u + 1) * dh]
    og_ref[0] = jnp.dot(h, wg_ref[...], preferred_element_type=F32)


def _inproj(x, mod4, nw, w_nsa, w_gla, *, sc_idx, sh_idx):
    bsz, s, d = x.shape
    n_nsa, n_gla = w_nsa.shape[1], w_gla.shape[1]
    tm = INPROJ_TM
    dh = NSA_HEAD_DIM
    return pl.pallas_call(
        _inproj_kernel,
        grid=(bsz, s // tm),
        in_specs=[pl.BlockSpec((1, tm, d), lambda b, i: (b, i, 0)),
                  pl.BlockSpec((1, 1, 1, d), lambda b, i: (b, sc_idx, 0, 0)),
                  pl.BlockSpec((1, 1, 1, d), lambda b, i: (b, sh_idx, 0, 0)),
                  pl.BlockSpec((1, d), lambda b, i: (0, 0)),
                  pl.BlockSpec((d, n_nsa), lambda b, i: (0, 0), pipeline_mode=pl.Buffered(1)),
                  pl.BlockSpec((d, n_gla), lambda b, i: (0, 0), pipeline_mode=pl.Buffered(1))],
        out_specs=[pl.BlockSpec((1, n_nsa // dh, tm, dh), lambda b, i: (b, 0, i, 0)),
                   pl.BlockSpec((1, tm, n_gla), lambda b, i: (b, i, 0))],
        out_shape=[jax.ShapeDtypeStruct((bsz, n_nsa // dh, s, dh), F32),
                   jax.ShapeDtypeStruct((bsz, s, n_gla), F32)],
        compiler_params=_cparams(("parallel", "parallel")),
        name="inproj",
    )(x, mod4, mod4, nw, w_nsa, w_gla)


def _hi_lo(x):
    hi = x.astype(BF16)
    return hi, (x - hi.astype(F32)).astype(BF16)


def _dot_hi_lo(x, w):
    x_hi, x_lo = _hi_lo(x)
    w_hi, w_lo = _hi_lo(w)
    return jnp.dot(jnp.concatenate([x_hi, x_lo, x_hi], axis=1), jnp.concatenate([w_hi, w_hi, w_lo], axis=0),
                   preferred_element_type=F32)


def _compress_kernel(a_ref, pos_ref, w1_ref, w2_ref, o_ref):
    nch = a_ref.shape[2] // CMP_STRIDE
    a = jnp.concatenate([a_ref[0, 0, pl.ds(t, nch, stride=CMP_STRIDE), :] for t in range(CMP_STRIDE)], axis=1)
    pos = pos_ref[0]
    half = a.shape[1]
    y1 = _dot_hi_lo(a + pos[0:1], w1_ref[0, :half, :])
    y2 = _dot_hi_lo(a + pos[1:2], w1_ref[0, half:, :])
    nrow = a.shape[0]
    h = y1 + pltpu.roll(y2, nrow - 1, axis=0)
    out = _dot_hi_lo(jax.nn.gelu(h), w2_ref[0])
    row = lax.broadcasted_iota(I32, out.shape, 0)
    o_ref[0, 0] = jnp.where(row < nrow - 1, out, 0.0)


def _compress(proj_nsa, pos, w1, w2):
    bsz, _, s, dh = proj_nsa.shape
    nslot = 2 * NSA_KV_HEADS
    nch = s // CMP_STRIDE
    cw = CMP_STRIDE * dh
    return pl.pallas_call(
        _compress_kernel,
        grid=(bsz, nslot),
        in_specs=[pl.BlockSpec((1, 1, s, dh), lambda b, t: (b, NSA_HEADS + t, 0, 0)),
                  pl.BlockSpec((1, 2, cw), lambda b, t: (t // NSA_KV_HEADS, 0, 0)),
                  pl.BlockSpec((1, 2 * cw, dh), lambda b, t: (t // NSA_KV_HEADS, 0, 0)),
                  pl.BlockSpec((1, dh, dh), lambda b, t: (t // NSA_KV_HEADS, 0, 0))],
        out_specs=pl.BlockSpec((1, 1, nch, dh), lambda b, t: (b, t, 0, 0)),
        out_shape=jax.ShapeDtypeStruct((bsz, nslot, nch, dh), F32),
        compiler_params=_cparams(("parallel", "parallel")),
        name="nsa_compress",
    )(proj_nsa, pos, w1, w2)


def _nt_dot(a, b, **kw):
    return lax.dot_general(a, b, (((1,), (1,)), ((), ())), preferred_element_type=F32, **kw)


def _nsa_kernel(slopes_ref, q_ref, kc_ref, vc_ref, ks_ref, vs_ref, kw_ref, vw_ref, gate_ref, o_ref,
                ksb, vst, kwb, vwt, q4_scr, notsel_scr, *scr):
    g = pl.program_id(1)
    qi = pl.program_id(2)
    tq_n = q_ref.shape[2]
    dh = NSA_HEAD_DIM
    nr = NSA_Q_PER_KV
    seq = ks_ref.shape[2]
    tk_n = NSA_TK
    nb = seq // SEL_BLOCK

    @pl.when(qi == 0)
    def _():
        row = lax.broadcasted_iota(I32, (seq, dh), 0)
        lane = lax.broadcasted_iota(I32, (seq, dh), 1)
        blk = row // SEL_BLOCK
        pos = jnp.where((lane >= nb) & (lane < nb + 3), (blk * SEL_BLOCK).astype(F32),
                        jnp.where((lane >= nb + 3) & (lane < nb + 6), (row % SEL_BLOCK).astype(F32), 0.0))
        ksb[...] = jnp.concatenate([jnp.where(lane == blk, -MASKED, pos), ks_ref[0, 0]], axis=1).astype(BF16)
        kwb[...] = jnp.concatenate([pos, kw_ref[0, 0]], axis=1).astype(BF16)
        for c in range(seq // tk_n):
            rows = slice(c * tk_n, (c + 1) * tk_n)
            for src, dst in ((vs_ref, vst), (vw_ref, vwt)):
                v = src[0, 0, rows, :]
                dst[c] = jnp.concatenate([v, v], axis=1).T[:dh].astype(BF16)

    t0 = qi * tq_n
    tq = t0 + lax.broadcasted_iota(I32, (1, tq_n), 1)
    slopes = [slopes_ref[g * nr + r] for r in range(nr)]
    scale = dh ** -0.5
    q_t = []
    for pair in range(nr // 2):
        both = jnp.concatenate([q_ref[0, 2 * pair], q_ref[0, 2 * pair + 1]], axis=1).T * scale
        q_t += [both[:dh], both[dh:]]

    ncp = kc_ref.shape[2]
    kc = kc_ref[0, 0]
    vc = vc_ref[0, 0]
    vc_t = jnp.concatenate([vc, vc], axis=1).T[:dh].astype(BF16)
    n_sub = lax.broadcasted_iota(I32, (ncp, 1), 0)
    blk_end = n_sub * CMP_STRIDE + (CMP_BLOCK - 1)
    center = n_sub.astype(F32) * CMP_STRIDE + (CMP_BLOCK - 1) / 2.0
    kc_hi, kc_lo = _hi_lo(kc)
    kc_cat = jnp.concatenate([kc_hi, kc_lo, kc_hi, jnp.zeros_like(kc_hi)], axis=1)
    q_hi, q_lo = _hi_lo(jnp.concatenate(q_t, axis=1))
    q_cat = jnp.concatenate([q_hi, q_hi, q_lo, jnp.zeros_like(q_hi)], axis=0)
    s = jnp.dot(kc_cat, q_cat, preferred_element_type=F32)
    tq_all = jnp.concatenate([tq] * nr, axis=1)
    slope_all = jnp.concatenate([jnp.full((1, tq_n), slopes[r], F32) for r in range(nr)], axis=1)
    valid_all = blk_end <= tq_all
    s = jnp.where(valid_all, s - slope_all * (tq_all.astype(F32) - center), NEG)
    e = jnp.exp(s - jnp.max(s, axis=0, keepdims=True))
    p = jnp.where(valid_all, e / jnp.sum(e, axis=0, keepdims=True), 0.0)
    psum = sum(p[:, r * tq_n:(r + 1) * tq_n] for r in range(nr))
    o_c_all = jnp.dot(vc_t, p.astype(BF16), preferred_element_type=F32)
    o_c = [o_c_all[:, r * tq_n:(r + 1) * tq_n] for r in range(nr)]

    n_sel = min(N_SEL, nb)
    notsel_scr[...] = jnp.zeros(notsel_scr.shape, F32)

    @pl.when((qi + 1) * tq_n > n_sel * SEL_BLOCK)
    def _():
        rowj = lax.broadcasted_iota(I32, (LANES, ncp), 0) * SEL_BLOCK
        coln = lax.broadcasted_iota(I32, (LANES, ncp), 1) * CMP_STRIDE
        overlap = jnp.where((coln < rowj + SEL_BLOCK) & (coln + CMP_BLOCK > rowj)
                            & (coln < (ncp - 1) * CMP_STRIDE) & (rowj < nb * SEL_BLOCK), 1.0, 0.0)
        ov = overlap.astype(BF16)
        imp = jnp.dot(jnp.concatenate([ov, ov], axis=1), jnp.concatenate(_hi_lo(psum), axis=0),
                      preferred_element_type=F32)[:nb]
        j_sub = lax.broadcasted_iota(I32, (nb, 1), 0)
        qblk = tq // SEL_BLOCK
        forced = (j_sub == 0) | (j_sub == qblk) | (j_sub == qblk - 1)
        imp = jnp.where(forced, FORCE, jnp.where(j_sub <= qblk, imp, NEG))
        cnt = jnp.zeros((nb, tq_n), F32)
        for i in range(nb):
            ci = imp[i:i + 1, :]
            tie = jnp.where(j_sub > i, 1.0, 0.0)
            cnt = cnt + jnp.where(ci > imp, 1.0, jnp.where(ci == imp, tie, 0.0))
        notsel_scr[...] = jnp.where(cnt < float(n_sel), 0.0, 1.0)

    notsel = notsel_scr[...]

    sub_h = lax.broadcasted_iota(I32, (dh - nb, 1), 0)
    for r in range(nr):
        scol = jnp.zeros((dh - nb, 1), F32)
        for i in range(3):
            piece = slopes_ref[(i + 1) * NSA_HEADS + g * nr + r]
            scol = jnp.where((sub_h == i) | (sub_h == 3 + i), piece, scol)
        q4_scr[:, r * tq_n:(r + 1) * tq_n] = jnp.concatenate(
            [notsel, jnp.broadcast_to(scol, (dh - nb, tq_n)), q_t[r] * LOG2E], axis=0).astype(BF16)

    nwt = WINDOW // tk_n
    nbuf = nwt + 2
    stats_s, stats_w = scr[0:3], scr[3:6]
    s_buf, p_buf, a_buf = scr[6:6 + nbuf], scr[6 + nbuf:6 + 2 * nbuf], scr[6 + 2 * nbuf:6 + 3 * nbuf]
    for m_ref, l_ref, acc_ref in (stats_s, stats_w):
        m_ref[...] = jnp.full(m_ref.shape, NEG, F32)
        l_ref[...] = jnp.zeros(l_ref.shape, F32)
        acc_ref[...] = jnp.zeros(acc_ref.shape, F32)
    key_i = lax.broadcasted_iota(I32, (tk_n, LANES), 0)
    qry_j = lax.broadcasted_iota(I32, (tk_n, LANES), 1)

    def scores(k_ref, kt, buf):
        k_tile = k_ref[kt * tk_n:(kt + 1) * tk_n, :]
        s_buf[buf][...] = jnp.dot(k_tile, q4_scr[...], preferred_element_type=F32)

    def softmax(buf, mode, stats):
        m_ref, l_ref, _ = stats
        for cb in range(nr * tq_n // LANES):
            cols = slice(cb * LANES, (cb + 1) * LANES)
            s = s_buf[buf][:, cols]
            if mode is not None:
                j = qry_j + (cb * LANES) % tq_n
                s = jnp.where(key_i <= j if mode == "causal" else key_i > j, s, -MASKED)
            m_prev = m_ref[:, cols]
            m_new = jnp.maximum(m_prev, jnp.max(s, axis=0, keepdims=True))
            alpha = jnp.exp2(m_prev - m_new)
            p = jnp.exp2(s - m_new)
            l_ref[:, cols] = alpha * l_ref[:, cols] + jnp.sum(p, axis=0, keepdims=True)
            m_ref[:, cols] = m_new
            a_buf[buf][:, cols] = alpha
            p_buf[buf][:, cols] = p.astype(BF16)

    def values(vt_ref, kt, buf, stats):
        acc_ref = stats[2]
        pv = jnp.dot(vt_ref[kt], p_buf[buf][...], preferred_element_type=F32)
        acc_ref[...] = acc_ref[...] * a_buf[buf][...] + pv

    def batch(jobs):
        for i in range(min(nbuf, len(jobs))):
            scores(jobs[i][0], jobs[i][2], i)
        for i, (_, vt_ref, kt, mode, stats) in enumerate(jobs):
            softmax(i % nbuf, mode, stats)
            values(vt_ref, kt, i % nbuf, stats)
            if i + nbuf < len(jobs):
                scores(jobs[i + nbuf][0], jobs[i + nbuf][2], i % nbuf)

    for k in range(seq // tq_n):
        @pl.when(qi == k)
        def _(k=k):
            jobs = [(ksb, vst, t, None, stats_s) for t in range(k)]
            jobs += [(kwb, vwt, k - back, "band" if back == nwt else None, stats_w)
                     for back in range(min(nwt, k), 0, -1)]
            jobs += [(kwb, vwt, k, "causal", stats_w), (ksb, vst, k, "causal", stats_s)]
            batch(jobs)

    gsel = jnp.where(lax.broadcasted_iota(I32, (LANES, LANES), 0)
                     == lax.broadcasted_iota(I32, (LANES, LANES), 1) + g * (nr * N_BRANCH), 1.0, 0.0)
    gs = gsel.astype(BF16)
    gates = jax.nn.sigmoid(jnp.dot(jnp.concatenate(_hi_lo(gate_ref[0]), axis=1), jnp.concatenate([gs, gs], axis=0),
                                   preferred_element_type=F32)).T
    (_, l_s, acc_s), (_, l_w, acc_w) = stats_s, stats_w
    for pair in range(nr // 2):
        o_t = []
        for r in (2 * pair, 2 * pair + 1):
            c0 = r * N_BRANCH
            cols = slice(r * tq_n, (r + 1) * tq_n)
            o_t.append(gates[c0:c0 + 1, :] * o_c[r] + (gates[c0 + 1:c0 + 2, :] / l_s[:, cols]) * acc_s[:, cols]
                       + (gates[c0 + 2:c0 + 3, :] / l_w[:, cols]) * acc_w[:, cols])
        o_ref[0, :, pair * LANES:(pair + 1) * LANES] = jnp.concatenate(o_t, axis=0).T


def _nsa(slopes, proj_nsa, kvc, proj_gla):
    bsz, _, s, dh = proj_nsa.shape
    g_n, nr = NSA_KV_HEADS, NSA_Q_PER_KV
    tq = NSA_TQ
    tk = NSA_TK
    assert tq == tk and WINDOW % tk == 0 and 2 * dh == LANES
    ncp = kvc.shape[2]
    kv0 = NSA_HEADS
    nq = nr * tq
    nbuf = WINDOW // tk + 2

    def kv_spec(i):
        return pl.BlockSpec((1, 1, s, dh), lambda b, g, q, i=i: (b, kv0 + i * g_n + g, 0, 0))

    return pl.pallas_call(
        _nsa_kernel,
        grid=(bsz, g_n, s // tq),
        in_specs=[pl.BlockSpec(memory_space=pltpu.SMEM),
                  pl.BlockSpec((1, nr, tq, dh), lambda b, g, q: (b, g, q, 0)),
                  pl.BlockSpec((1, 1, ncp, dh), lambda b, g, q: (b, g, 0, 0)),
                  pl.BlockSpec((1, 1, ncp, dh), lambda b, g, q: (b, g_n + g, 0, 0)),
                  kv_spec(2), kv_spec(3), kv_spec(4), kv_spec(5),
                  pl.BlockSpec((1, tq, LANES), lambda b, g, q: (b, q, GLA_MISC_OFF // LANES))],
        out_specs=pl.BlockSpec((1, tq, nr * dh), lambda b, g, q: (b, q, g)),
        out_shape=jax.ShapeDtypeStruct((bsz, s, NSA_HEADS * dh), F32),
        scratch_shapes=[pltpu.VMEM((s, LANES), BF16), pltpu.VMEM((s // tk, dh, tk), BF16)] * 2
        + [pltpu.VMEM((LANES, nq), BF16), pltpu.VMEM((s // SEL_BLOCK, tq), F32)]
        + [pltpu.VMEM((1, nq), F32), pltpu.VMEM((1, nq), F32), pltpu.VMEM((dh, nq), F32)] * 2
        + [pltpu.VMEM((tk, nq), F32)] * nbuf + [pltpu.VMEM((tk, nq), BF16)] * nbuf + [pltpu.VMEM((1, nq), F32)] * nbuf,
        compiler_params=_cparams(("parallel", "parallel", "arbitrary")),
        name="nsa_attention",
    )(slopes, proj_nsa, kvc, kvc, proj_nsa, proj_nsa, proj_nsa, proj_nsa, proj_gla)


def _gla_kernel(q_ref, k_ref, v_ref, og_ref, lr_ref, wg_ref, bg_ref, nw_ref, o_ref, st_scr, la_scr, b_scr):
    rows_n = q_ref.shape[1]
    c_n, sub = GLA_CHUNK, GLA_SUB
    nh, dk, dv = GLA_HEADS, GLA_DK, GLA_DV

    @pl.when(pl.program_id(1) == 0)
    def _():
        st_scr[...] = jnp.zeros(st_scr.shape, F32)

    z = jnp.dot(lr_ref[0], wg_ref[...], precision=HI, preferred_element_type=F32) + bg_ref[...]
    la_scr[...] = (jnp.minimum(z, 0.0) - jnp.log1p(jnp.exp(-jnp.abs(z)))) * (1.0 / GLA_GATE_NORM)
    tril = jnp.where(lax.broadcasted_iota(I32, (c_n, c_n), 0) >= lax.broadcasted_iota(I32, (c_n, c_n), 1), 1.0, 0.0)
    row_c = lax.broadcasted_iota(I32, (c_n, 1), 0)
    row_s = lax.broadcasted_iota(I32, (sub, 1), 0)
    lane_c = lax.broadcasted_iota(I32, (1, c_n), 1)
    nw = nw_ref[...]
    hk = [slice(h * dk, (h + 1) * dk) for h in range(nh)]
    hv = [slice(h * dv, (h + 1) * dv) for h in range(nh)]

    def chunk(c, carry):
        r0 = pl.multiple_of(c * c_n, c_n)
        rows = pl.ds(r0, c_n)
        qc = q_ref[0, rows, :] * (dk ** -0.5)
        kc = k_ref[0, rows, :]
        vc = [v_ref[0, rows, hv[h]].astype(BF16) for h in range(nh)]
        b = jnp.dot(tril, la_scr[rows, :], precision=HI, preferred_element_type=F32)
        b_scr[...] = b
        st = [st_scr[h] for h in range(nh)]
        q_e = (qc * jnp.exp(b)).astype(BF16)
        o = [_nt_dot(q_e[:, hk[h]], st[h].astype(BF16)) for h in range(nh)]
        strips = [[] for _ in range(nh)]
        for blk in range(c_n // sub):
            lo = blk * sub
            q_i = qc[lo:lo + sub]
            b_i = b[lo:lo + sub]
            a = [jnp.zeros((sub, c_n), F32) for _ in range(nh)]
            if blk > 0:
                b_r = b_scr[lo - 1:lo, :]
                q_d = (q_i * jnp.exp(b_i - b_r)).astype(BF16)
                k_d = (kc * jnp.exp(jnp.where(row_c < lo, b_r - b, -jnp.inf))).astype(BF16)
                a = [_nt_dot(q_d[:, hk[h]], k_d[:, hk[h]]) for h in range(nh)]
            for j in range(sub):
                b_j = b_scr[lo + j:lo + j + 1, :]
                k_j = k_ref[0, pl.ds(r0 + lo + j, 1), :]
                prod = q_i * k_j * jnp.exp(jnp.where(row_s >= j, b_i - b_j, -jnp.inf))
                for h in range(nh):
                    col = jnp.sum(prod[:, hk[h]], axis=-1, keepdims=True)
                    a[h] = jnp.where(lane_c == lo + j, col, a[h]) if blk == 0 else (
                        a[h] + jnp.where(lane_c == lo + j, col, 0.0))
            for h in range(nh):
                strips[h].append(a[h])
        for h in range(nh):
            attn = jnp.concatenate(strips[h], axis=0)
            o[h] = o[h] + jnp.dot(attn.astype(BF16), vc[h], preferred_element_type=F32)
        b_last = b_scr[c_n - 1:c_n, :]
        k_dec = (kc * jnp.exp(b_last - b)).astype(BF16)
        decay = jnp.exp(b_last)
        for h in range(nh):
            st_scr[h] = st[h] * decay[:, hk[h]] + lax.dot_general(
                vc[h], k_dec[:, hk[h]], (((0,), (0,)), ((), ())), preferred_element_type=F32)
        for h in range(nh):
            og = og_ref[0, rows, hv[h]]
            on = o[h] * lax.rsqrt(jnp.mean(o[h] * o[h], axis=-1, keepdims=True) + EPS) * nw
            o_ref[0, rows, hv[h]] = on * (og * jax.nn.sigmoid(og))
        return carry

    lax.fori_loop(0, rows_n // c_n, chunk, 0)


def _gla(proj_gla, wg_pad, bg, nw):
    bsz, s, _ = proj_gla.shape
    nh, dk, dv = GLA_HEADS, GLA_DK, GLA_DV
    ts = GLA_TS
    wk, wv = nh * dk, nh * dv
    return pl.pallas_call(
        _gla_kernel,
        grid=(bsz, s // ts),
        in_specs=[pl.BlockSpec((1, ts, wk), lambda b, i: (b, i, GLA_Q_OFF // wk)),
                  pl.BlockSpec((1, ts, wk), lambda b, i: (b, i, GLA_K_OFF // wk)),
                  pl.BlockSpec((1, ts, wv), lambda b, i: (b, i, GLA_V_OFF // wv)),
                  pl.BlockSpec((1, ts, wv), lambda b, i: (b, i, GLA_OG_OFF // wv)),
                  pl.BlockSpec((1, ts, LANES), lambda b, i: (b, i, GLA_MISC_OFF // LANES)),
                  pl.BlockSpec((LANES, wk), lambda b, i: (0, 0)),
                  pl.BlockSpec((1, wk), lambda b, i: (0, 0)),
                  pl.BlockSpec((1, dv), lambda b, i: (0, 0))],
        out_specs=pl.BlockSpec((1, ts, wv), lambda b, i: (b, i, 0)),
        out_shape=jax.ShapeDtypeStruct((bsz, s, wv), F32),
        scratch_shapes=[pltpu.VMEM((nh, dv, dk), F32), pltpu.VMEM((ts, wk), F32), pltpu.VMEM((GLA_CHUNK, wk), F32)],
        compiler_params=_cparams(("parallel", "arbitrary")),
        name="gla",
    )(proj_gla, proj_gla, proj_gla, proj_gla, proj_gla, wg_pad, bg, nw)


def _outproj_kernel(nsa_ref, gla_ref, x_ref, wo_ref, g1_ref, sc_ref, sh_ref, nw_ref, wr_ref, br_ref,
                    x1_ref, h_ref, route_ref):
    half = nsa_ref.shape[2]
    acc = jnp.dot(nsa_ref[0].astype(BF16), wo_ref[:half, :], preferred_element_type=F32)
    acc = acc + jnp.dot(gla_ref[0].astype(BF16), wo_ref[half:, :], preferred_element_type=F32)
    x1 = x_ref[0] + g1_ref[0, 0] * acc
    x1_ref[0] = x1
    h = _modulated_norm(x1, nw_ref[...], sc_ref[0, 0], sh_ref[0, 0])
    h_ref[0] = h
    h_hi = h.astype(BF16)
    h_lo = (h - h_hi.astype(F32)).astype(BF16)
    t = jnp.dot(h_hi, wr_ref[...], preferred_element_type=F32)
    logits = (t[:, :LANES] + t[:, LANES:] + jnp.dot(h_lo, wr_ref[:, :LANES], preferred_element_type=F32)
              + br_ref[...])
    lane = lax.broadcasted_iota(I32, (1, LANES), 1)
    ninf = -jnp.inf
    is_g = (lane >= N_EXPERTS) & (lane < N_EXPERTS + N_GROUPS)
    gl = jnp.where(is_g, logits, ninf)
    ge = jnp.exp(gl - jnp.max(gl, axis=-1, keepdims=True))
    gp = ge / jnp.sum(ge, axis=-1, keepdims=True)
    gp_max = jnp.max(gp, axis=-1, keepdims=True)
    grp = jnp.min(jnp.where((gp == gp_max) & is_g, lane - N_EXPERTS, LANES), axis=-1, keepdims=True)
    in_grp = (lane // EXPERTS_PER_GROUP == grp) & (lane < N_EXPERTS)
    el = jnp.where(in_grp, logits, ninf)
    v1 = jnp.max(el, axis=-1, keepdims=True)
    i1 = jnp.min(jnp.where(el == v1, lane, LANES), axis=-1, keepdims=True)
    el2 = jnp.where(lane == i1, ninf, el)
    v2 = jnp.max(el2, axis=-1, keepdims=True)
    i2 = jnp.min(jnp.where(el2 == v2, lane, LANES), axis=-1, keepdims=True)
    e2 = jnp.exp(v2 - v1)
    den = 1.0 + e2
    w1 = gp_max * (1.0 / den)
    w2 = gp_max * (e2 / den)
    route_ref[0] = jnp.where(lane == 0, i1.astype(F32), jnp.where(lane == 1, i2.astype(F32), jnp.where(
        lane == 2, w1, jnp.where(lane == 3, w2, 0.0))))


def _outproj(o_nsa, o_gla, x, wo, mod4, nw, wr, br):
    bsz, s, d = x.shape
    tm = OUT_TM
    half = o_nsa.shape[2]

    def mod_spec(idx):
        return pl.BlockSpec((1, 1, 1, d), lambda b, i: (b, idx, 0, 0))

    row = lambda w: pl.BlockSpec((1, tm, w), lambda b, i: (b, i, 0))
    return pl.pallas_call(
        _outproj_kernel,
        grid=(bsz, s // tm),
        in_specs=[row(half), row(half), row(d),
                  pl.BlockSpec((2 * half, d), lambda b, i: (0, 0)),
                  mod_spec(2), mod_spec(4), mod_spec(3),
                  pl.BlockSpec((1, d), lambda b, i: (0, 0)),
                  pl.BlockSpec((d, 2 * LANES), lambda b, i: (0, 0)),
                  pl.BlockSpec((1, LANES), lambda b, i: (0, 0))],
        out_specs=[row(d), row(d), row(LANES)],
        out_shape=[jax.ShapeDtypeStruct((bsz, s, d), F32), jax.ShapeDtypeStruct((bsz, s, d), F32),
                   jax.ShapeDtypeStruct((bsz, s, LANES), F32)],
        compiler_params=_cparams(("parallel", "parallel")),
        name="outproj_router",
    )(o_nsa, o_gla, x, wo, mod4, mod4, mod4, nw, wr, br)


def _rank_kernel(route_ref, dest_ref, meta_ref, rank_scr):
    n = route_ref.shape[0]
    tm = RANK_TM
    lane_i = lax.broadcasted_iota(I32, (1, LANES), 1)
    lane = lane_i.astype(F32)
    strict = jnp.where(lax.broadcasted_iota(I32, (tm, tm), 0) > lax.broadcasted_iota(I32, (tm, tm), 1),
                       1.0, 0.0).astype(BF16)

    def two_lanes(a, b):
        return jnp.where(lane_i == 0, a, jnp.where(lane_i == 1, b, 0.0))

    def pick(e, table):
        return jnp.sum(jnp.where(lane == e, table, 0.0), axis=-1, keepdims=True)

    def count(i, seen):
        r0 = pl.multiple_of(i * tm, tm)
        rt = route_ref[pl.ds(r0, tm), :]
        e1, e2 = rt[:, 0:1], rt[:, 1:2]
        member = jnp.where(lane == e1, 1.0, jnp.where(lane == e2, 1.0, 0.0))
        before = jnp.dot(strict, member.astype(BF16), preferred_element_type=F32) + seen
        rank_scr[pl.ds(r0, tm), :] = two_lanes(pick(e1, before), pick(e2, before))
        return seen + jnp.sum(member, axis=0, keepdims=True)

    counts = lax.fori_loop(0, n // tm, count, jnp.zeros((1, LANES), F32))
    ntile = jnp.floor((counts + (MOE_TB - 1)) * (1.0 / MOE_TB))
    incl = jnp.where(lax.broadcasted_iota(I32, (LANES, LANES), 0) <= lax.broadcasted_iota(I32, (LANES, LANES), 1),
                     1.0, 0.0).astype(BF16)
    tile_end = jnp.dot(jnp.broadcast_to(ntile, (8, LANES)).astype(BF16), incl,
                       preferred_element_type=F32)[0:1]
    row_start = (tile_end - ntile) * MOE_TB

    def place(i, carry):
        r0 = pl.multiple_of(i * tm, tm)
        rt = route_ref[pl.ds(r0, tm), :]
        rk = rank_scr[pl.ds(r0, tm), :]
        d1 = pick(rt[:, 0:1], row_start) + rk[:, 0:1]
        d2 = pick(rt[:, 1:2], row_start) + rk[:, 1:2]
        dest_ref[pl.ds(r0, tm), :] = two_lanes(d1, d2).astype(I32)
        return carry

    lax.fori_loop(0, n // tm, place, 0)
    trow = lax.broadcasted_iota(I32, (meta_ref.shape[0], 1), 0).astype(F32)
    texp = jnp.sum(jnp.where((tile_end <= trow) & (lane_i < N_EXPERTS), 1.0, 0.0), axis=-1, keepdims=True)
    texp = jnp.minimum(texp, N_EXPERTS - 1.0)
    used = pick(N_EXPERTS - 1.0, tile_end)
    diag = lax.broadcasted_iota(I32, (meta_ref.shape[0], LANES), 0) == lane_i
    end_rows = jnp.sum(jnp.where(diag, tile_end, 0.0), axis=-1, keepdims=True)
    ntile_rows = jnp.sum(jnp.where(diag, ntile, 0.0), axis=-1, keepdims=True)
    meta_ref[...] = jnp.where(lane_i == 2, end_rows, jnp.where(lane_i == 3, ntile_rows, two_lanes(
        texp, jnp.broadcast_to(used, texp.shape)))).astype(I32)


def _rank(route):
    n = route.shape[0]
    return pl.pallas_call(
        _rank_kernel,
        out_shape=[jax.ShapeDtypeStruct((n, LANES), I32), jax.ShapeDtypeStruct((LANES, LANES), I32)],
        scratch_shapes=[pltpu.VMEM((n, LANES), F32)],
        compiler_params=pltpu.CompilerParams(vmem_limit_bytes=VMEM_LIMIT),
        name="moe_rank",
    )(route)


def _dispatch_kernel(dest_ref, ends_ref, h_ref, xs_ref, zero_scr, sem, zsem):
    i = pl.program_id(0)
    tm = h_ref.shape[0]
    tb = zero_scr.shape[0]

    @pl.when(i == 0)
    def _():
        zero_scr[...] = jnp.zeros(zero_scr.shape, F32)

        def zero_copy(e):
            r0 = pl.multiple_of((ends_ref[e] - 1) * tb, tb)
            return pltpu.make_async_copy(zero_scr, xs_ref.at[pl.ds(r0, tb)], zsem)

        def start(e, carry):
            @pl.when(ends_ref[N_EXPERTS + e] > 0)
            def _():
                zero_copy(e).start()
            return carry

        def wait(e, carry):
            @pl.when(ends_ref[N_EXPERTS + e] > 0)
            def _():
                zero_copy(e).wait()
            return carry

        def tail_copy(t):
            return pltpu.make_async_copy(zero_scr, xs_ref.at[pl.ds(pl.multiple_of(t * tb, tb), tb)], zsem)

        def tail_start(t, carry):
            tail_copy(t).start()
            return carry

        def tail_wait(t, carry):
            tail_copy(t).wait()
            return carry

        used = ends_ref[2 * N_EXPERTS]
        lax.fori_loop(0, N_EXPERTS, start, 0)
        lax.fori_loop(used, xs_ref.shape[0] // tb, tail_start, 0)
        lax.fori_loop(0, N_EXPERTS, wait, 0)
        lax.fori_loop(used, xs_ref.shape[0] // tb, tail_wait, 0)

    def issue(r, carry):
        p = (i * tm + r) * TOP_K
        for k in range(TOP_K):
            pltpu.make_async_copy(h_ref.at[pl.ds(r, 1)], xs_ref.at[pl.ds(dest_ref[p + k], 1)], sem).start()
        return carry

    lax.fori_loop(0, tm, issue, 0, unroll=DMA_UNROLL)
    for k in range(TOP_K):
        pltpu.make_async_copy(h_ref, xs_ref.at[pl.ds(0, tm)], sem).wait()


def _dispatch(dest_flat, ends_flat, h, cap):
    n, d = h.shape
    tm = DISPATCH_TM
    return pl.pallas_call(
        _dispatch_kernel,
        grid_spec=pltpu.PrefetchScalarGridSpec(
            num_scalar_prefetch=2, grid=(n // tm,),
            in_specs=[pl.BlockSpec((tm, d), lambda i, dst, ends: (i, 0))],
            out_specs=pl.BlockSpec(memory_space=pl.ANY),
            scratch_shapes=[pltpu.VMEM((MOE_TB, d), F32), pltpu.SemaphoreType.DMA(()), pltpu.SemaphoreType.DMA(())]),
        out_shape=jax.ShapeDtypeStruct((cap, d), F32),
        compiler_params=_cparams(("arbitrary",)),
        name="moe_dispatch",
    )(dest_flat, ends_flat, h)


def _ffn_kernel(meta_ref, x_ref, wg_ref, wu_ref, wd_ref, y_ref, wgb, wub, wdb):
    t = pl.program_id(0)
    ntile = pl.num_programs(0)
    e = meta_ref[t]
    e_prev = meta_ref[jnp.maximum(t - 1, 0)]
    active = t < meta_ref[ntile]

    @pl.when(active & ((t == 0) | (e != e_prev)))
    def _():
        wgb[...] = wg_ref[0].astype(BF16)
        wub[...] = wu_ref[0].astype(BF16)
        wdb[...] = wd_ref[0].astype(BF16)

    @pl.when(active)
    def _():
        x = x_ref[...].astype(BF16)
        gate = jnp.dot(x, wgb[...], preferred_element_type=F32)
        up = jnp.dot(x, wub[...], preferred_element_type=F32)
        act = (gate * jax.nn.sigmoid(gate)) * up
        y_ref[...] = jnp.dot(act.astype(BF16), wdb[...], preferred_element_type=F32)

    @pl.when(jnp.logical_not(active))
    def _():
        y_ref[...] = jnp.zeros(y_ref.shape, F32)


def _ffn(meta_flat, xs, wg, wu, wd):
    cap, d = xs.shape
    ff = wg.shape[2]
    tb = MOE_TB
    ntile = cap // tb
    return pl.pallas_call(
        _ffn_kernel,
        grid_spec=pltpu.PrefetchScalarGridSpec(
            num_scalar_prefetch=1, grid=(ntile,),
            in_specs=[pl.BlockSpec((tb, d), lambda t, m: (jnp.minimum(t, m[ntile] - 1), 0)),
                      pl.BlockSpec((1, d, ff), lambda t, m: (m[t], 0, 0)),
                      pl.BlockSpec((1, d, ff), lambda t, m: (m[t], 0, 0)),
                      pl.BlockSpec((1, ff, d), lambda t, m: (m[t], 0, 0))],
            out_specs=pl.BlockSpec((tb, d), lambda t, m: (t, 0)),
            scratch_shapes=[pltpu.VMEM((d, ff), BF16), pltpu.VMEM((d, ff), BF16), pltpu.VMEM((ff, d), BF16)]),
        out_shape=jax.ShapeDtypeStruct((cap, d), F32),
        compiler_params=_cparams(("arbitrary",)),
        name="moe_ffn",
    )(meta_flat, xs, wg, wu, wd)


def _combine_kernel(dest_ref, y_ref, x1_ref, route_ref, g2_ref, nf_ref, o_ref, ybuf, sem):
    i = pl.program_id(0)
    nstep = pl.num_programs(0)
    tm = x1_ref.shape[0]

    def issue(tile, slot):
        def body(r, carry):
            p = (tile * tm + r) * TOP_K
            for k in range(TOP_K):
                pltpu.make_async_copy(y_ref.at[pl.ds(dest_ref[p + k], 1)], ybuf.at[slot, k, pl.ds(r, 1)],
                                      sem.at[slot]).start()
            return carry

        lax.fori_loop(0, tm, body, 0, unroll=DMA_UNROLL)

    @pl.when(i == 0)
    def _():
        issue(0, 0)

    @pl.when(i + 1 < nstep)
    def _():
        issue(i + 1, (i + 1) % 2)

    slot = i % 2
    for k in range(TOP_K):
        pltpu.make_async_copy(y_ref.at[pl.ds(0, tm)], ybuf.at[slot, k], sem.at[slot]).wait()
    rt = route_ref[...]
    moe = rt[:, 2:3] * ybuf[slot, 0] + rt[:, 3:4] * ybuf[slot, 1]
    xo = x1_ref[...] + g2_ref[0, 0] * moe
    o_ref[...] = xo * lax.rsqrt(jnp.mean(xo * xo, axis=-1, keepdims=True) + EPS) * nf_ref[...]


def _combine(dest_flat, y, x1, route, mod4, nf, seq):
    n, d = x1.shape
    tm = COMB_TM
    tiles_per_seq = seq // tm
    return pl.pallas_call(
        _combine_kernel,
        grid_spec=pltpu.PrefetchScalarGridSpec(
            num_scalar_prefetch=1, grid=(n // tm,),
            in_specs=[pl.BlockSpec(memory_space=pl.ANY),
                      pl.BlockSpec((tm, d), lambda i, dst: (i, 0)),
                      pl.BlockSpec((tm, LANES), lambda i, dst: (i, 0)),
                      pl.BlockSpec((1, 1, 1, d), lambda i, dst: (i // tiles_per_seq, 5, 0, 0)),
                      pl.BlockSpec((1, d), lambda i, dst: (0, 0))],
            out_specs=pl.BlockSpec((tm, d), lambda i, dst: (i, 0)),
            scratch_shapes=[pltpu.VMEM((2, TOP_K, tm, d), F32), pltpu.SemaphoreType.DMA((2,))]),
        out_shape=jax.ShapeDtypeStruct((n, d), F32),
        compiler_params=_cparams(("arbitrary",)),
        name="moe_combine",
    )(dest_flat, y, x1, route, mod4, nf)


def _alibi_slopes():
    n = NSA_HEADS
    full = jnp.asarray(2.0 ** (-8.0 * np.arange(1, n + 1) / n), dtype=F32)
    pieces, rest = [], full * LOG2E
    for _ in range(3):
        piece = rest.astype(BF16).astype(F32)
        pieces.append(piece)
        rest = rest - piece
    return jnp.concatenate([full] + pieces)


def _layer(x, c, w_ada, b_ada, norm1_w, w_in, cmp_pos_k, cmp_w1_k, cmp_w2_k, cmp_pos_v, cmp_w1_v, cmp_w2_v,
           gla_w_gate2, gla_b_gate, gla_norm_w, w_out, norm2_w, w_rg, b_rg, w_re, b_re, w_eg, w_eu, w_ed):
    bsz, s, d = x.shape
    dh = NSA_HEAD_DIM
    mod4 = _adaln(c, w_ada, b_ada).reshape(bsz, 6, 1, d)

    o_gate = NSA_COLS
    o_gla = o_gate + NSA_GATE_COLS
    o_lr = o_gla + 2 * GLA_HEADS * GLA_DK + 2 * GLA_HEADS * GLA_DV
    w_nsa, w_gla = _prep_w_in(w_in.T, o_gate, o_gla, o_lr)
    nw1 = norm1_w.reshape(1, d)
    proj_nsa, proj_gla = _inproj(x, mod4, nw1, w_nsa, w_gla, sc_idx=1, sh_idx=0)

    pos = jnp.stack([cmp_pos_k, cmp_pos_v]).reshape(2, 2, CMP_STRIDE * dh)
    kvc = _compress(proj_nsa, pos, jnp.stack([cmp_w1_k, cmp_w1_v]), jnp.stack([cmp_w2_k, cmp_w2_v]))
    o_nsa = _nsa(_alibi_slopes(), proj_nsa, kvc, proj_gla)

    wg_pad = jnp.zeros((LANES, GLA_HEADS * GLA_DK), F32).at[
        NSA_GATE_COLS:NSA_GATE_COLS + GLA_GATE_RANK].set(gla_w_gate2)
    o_gla_out = _gla(proj_gla, wg_pad, gla_b_gate.reshape(1, -1), gla_norm_w.reshape(1, -1))

    wr = jnp.concatenate([w_re, w_rg, jnp.zeros((d, LANES - N_EXPERTS - N_GROUPS), F32)], axis=1)
    br = jnp.concatenate([b_re, b_rg, jnp.zeros((LANES - N_EXPERTS - N_GROUPS,), F32)]).reshape(1, LANES)
    wr_hi = wr.astype(BF16)
    wr_cat = jnp.concatenate([wr_hi, (wr - wr_hi.astype(F32)).astype(BF16)], axis=1)
    x1, h2, route = _outproj(o_nsa, o_gla_out, x, w_out.astype(BF16), mod4, norm2_w.reshape(1, d), wr_cat, br)

    n = bsz * s
    npair = n * TOP_K
    cap = npair + N_EXPERTS * MOE_TB
    ntile = cap // MOE_TB
    route2 = route.reshape(n, LANES)
    dest, meta = _rank(route2)
    dest_flat = dest[:, :TOP_K].reshape(npair)
    meta_flat = jnp.concatenate([meta[:ntile, 0], meta[:1, 1]])
    ends_flat = jnp.concatenate([meta[:N_EXPERTS, 2], meta[:N_EXPERTS, 3], meta[:1, 1]])
    xs = _dispatch(dest_flat, ends_flat, h2.reshape(n, d), cap)
    y = _ffn(meta_flat, xs, w_eg, w_eu, w_ed)
    return x1.reshape(n, d), y, dest_flat, route2, mod4


def kernel(x, c, w_ada, b_ada, norm1_w, w_in, cmp_pos_k, cmp_w1_k, cmp_w2_k, cmp_pos_v, cmp_w1_v, cmp_w2_v,
           gla_w_gate2, gla_b_gate, gla_norm_w, w_out, norm2_w, w_router_group, b_router_group, w_router_expert,
           b_router_expert, w_expert_gate, w_expert_up, w_expert_down, norm_f_w):
    bsz, s, d = x.shape
    assert w_ada.shape[0] == 1, "single layer"
    x1, y, dest_flat, route2, mod4 = _layer(
        x, c, w_ada[0], b_ada[0], norm1_w[0], w_in[0], cmp_pos_k[0], cmp_w1_k[0], cmp_w2_k[0], cmp_pos_v[0],
        cmp_w1_v[0], cmp_w2_v[0], gla_w_gate2[0], gla_b_gate[0], gla_norm_w[0], w_out[0], norm2_w[0],
        w_router_group[0], b_router_group[0], w_router_expert[0], b_router_expert[0],
        w_expert_gate[0], w_expert_up[0], w_expert_down[0])
    out = _combine(dest_flat, y, x1, route2, mod4, norm_f_w.reshape(1, d), s)
    return out.reshape(bsz, s, d)
```

```python
import functools

import numpy as np
import jax
import jax.numpy as jnp
from jax import lax
from jax.experimental import pallas as pl
from jax.experimental.pallas import tpu as pltpu

F32 = jnp.float32
BF16 = jnp.bfloat16
I32 = jnp.int32
HI = lax.Precision.HIGHEST

D_MODEL = 2048
NSA_HEAD_DIM = 64
NSA_HEADS = 16
NSA_KV_HEADS = 4
NSA_Q_PER_KV = 4
CMP_BLOCK = 32
CMP_STRIDE = 16
SEL_BLOCK = 64
N_SEL = 16
WINDOW = 512
N_BRANCH = 3
GLA_HEADS = 4
GLA_DV = 256
GLA_DK = 128
GLA_GATE_RANK = 16
GLA_GATE_NORM = 16.0
GLA_CHUNK = 64
GLA_SUB = 16
N_GROUPS = 4
EXPERTS_PER_GROUP = 8
N_EXPERTS = 32
TOP_K = 2
EXPERT_FF = 512
EPS = 1e-6
NEG = -1e30
FORCE = 1e30
LOG2E = 1.4426950408889634
MASKED = 2.0 ** 100

NSA_Q_COLS = NSA_HEADS * NSA_HEAD_DIM
NSA_KV_COLS = 2 * N_BRANCH * NSA_KV_HEADS * NSA_HEAD_DIM
NSA_GATE_COLS = N_BRANCH * NSA_HEADS
NSA_COLS = NSA_Q_COLS + NSA_KV_COLS
NSA_SLOTS = NSA_COLS // NSA_HEAD_DIM
GLA_Q_OFF = 0
GLA_K_OFF = GLA_HEADS * GLA_DK
GLA_V_OFF = 2 * GLA_HEADS * GLA_DK
GLA_OG_OFF = GLA_V_OFF + GLA_HEADS * GLA_DV
GLA_MISC_OFF = GLA_OG_OFF + GLA_HEADS * GLA_DV
LANES = 128
GLA_COLS = GLA_MISC_OFF + LANES

VMEM_LIMIT = 56 * 1024 * 1024

ADA_TN = 768
PREP_TR = 256
PREP_STEP = 512
INPROJ_TM = 256
INPROJ_TN = 512
NSA_TQ = 256
NSA_TK = 256
GLA_TS = 512
OUT_TM = 256
RANK_TM = 256
MOE_TB = 256
DISPATCH_TM = 256
COMB_TM = 256
DMA_UNROLL = 8


def _cparams(sem):
    return pltpu.CompilerParams(dimension_semantics=sem, vmem_limit_bytes=VMEM_LIMIT)


def _adaln_kernel(ct_ref, w_ref, b_ref, o_ref, s_scr):
    nb = ct_ref.shape[1]
    kdim, tn = w_ref.shape

    @pl.when(pl.program_id(0) == 0)
    def _():
        ct = ct_ref[...]
        s = ct * jax.nn.sigmoid(ct)
        for b in range(nb):
            s_scr[b] = jnp.broadcast_to(s[:, b:b + 1], (kdim, LANES))

    def body(k, accs):
        r = pl.multiple_of(k * 8, 8)
        w8 = w_ref[pl.ds(r, 8), :]
        out = []
        for b, acc in enumerate(accs):
            s8 = s_scr[b, pl.ds(r, 8), :]
            out.append(acc + w8 * jnp.concatenate([s8] * (tn // LANES), axis=1))
        return tuple(out)

    accs = lax.fori_loop(0, kdim // 8, body, tuple(jnp.zeros((8, tn), F32) for _ in range(nb)), unroll=2)
    bias = b_ref[...]
    for b, acc in enumerate(accs):
        o_ref[b:b + 1, :] = jnp.sum(acc, axis=0, keepdims=True) + bias


def _adaln(c, w, b):
    nb, d = c.shape
    n = w.shape[1]
    return pl.pallas_call(
        _adaln_kernel,
        grid=(n // ADA_TN,),
        in_specs=[pl.BlockSpec((d, nb), lambda j: (0, 0)),
                  pl.BlockSpec((d, ADA_TN), lambda j: (0, j)),
                  pl.BlockSpec((1, ADA_TN), lambda j: (0, j))],
        out_specs=pl.BlockSpec((nb, ADA_TN), lambda j: (0, j)),
        out_shape=jax.ShapeDtypeStruct((nb, n), F32),
        scratch_shapes=[pltpu.VMEM((nb, d, LANES), F32)],
        compiler_params=_cparams(("arbitrary",)),
        name="adaln",
    )(c.T, w, b.reshape(1, n))


def _modulated_norm(x, nw, sc, sh):
    ms = jnp.mean(x * x, axis=-1, keepdims=True)
    h = x * lax.rsqrt(ms + EPS) * nw
    return h * (1.0 + sc) + sh


def _prep_w_in_kernel(wt_ref, wn_ref, wg_ref, *, o_gate, o_gla, o_lr):
    tr = wt_ref.shape[1]
    step = PREP_STEP

    def put(dst, c0, rows):
        dst[:, c0:c0 + rows.shape[0]] = rows.T.astype(BF16)

    for c in range(0, o_gate, step):
        put(wn_ref, c, wt_ref[c:c + step, :])
    for c in range(0, o_lr - o_gla, step):
        put(wg_ref, c, wt_ref[o_gla + c:o_gla + c + step, :])
    pad = wg_ref.shape[1] - (wt_ref.shape[0] - o_gate)
    misc = jnp.concatenate([wt_ref[o_gate:o_gla, :], wt_ref[o_lr:, :], jnp.zeros((pad, tr), F32)], axis=0)
    put(wg_ref, o_lr - o_gla, misc)


def _prep_w_in(w_in_t, o_gate, o_gla, o_lr):
    n, d = w_in_t.shape
    tr = PREP_TR
    assert o_gate % PREP_STEP == 0 and (o_lr - o_gla) % PREP_STEP == 0
    return pl.pallas_call(
        functools.partial(_prep_w_in_kernel, o_gate=o_gate, o_gla=o_gla, o_lr=o_lr),
        grid=(d // tr,),
        in_specs=[pl.BlockSpec((n, tr), lambda i: (0, i))],
        out_specs=[pl.BlockSpec((tr, NSA_COLS), lambda i: (i, 0)), pl.BlockSpec((tr, GLA_COLS), lambda i: (i, 0))],
        out_shape=[jax.ShapeDtypeStruct((d, NSA_COLS), BF16), jax.ShapeDtypeStruct((d, GLA_COLS), BF16)],
        compiler_params=_cparams(("parallel",)),
        name="prep_w_in",
    )(w_in_t)


def _inproj_kernel(x_ref, sc_ref, sh_ref, nw_ref, wn_ref, wg_ref, on_ref, og_ref):
    h = _modulated_norm(x_ref[0], nw_ref[...], sc_ref[0, 0], sh_ref[0, 0]).astype(BF16)
    dh = NSA_HEAD_DIM
    tn = INPROJ_TN
    for c in range(wn_ref.shape[1] // tn):
        acc = jnp.dot(h, wn_ref[:, c * tn:(c + 1) * tn], preferred_element_type=F32)
        for u in range(tn // dh):
            on_ref[0, c * (tn // dh) + u] = acc[:, u * dh:(u + 1) * dh]
    og_ref[0] = jnp.dot(h, wg_ref[...], preferred_element_type=F32)


def _inproj(x, mod4, nw, w_nsa, w_gla, *, sc_idx, sh_idx):
    bsz, s, d = x.shape
    n_nsa, n_gla = w_nsa.shape[1], w_gla.shape[1]
    tm = INPROJ_TM
    dh = NSA_HEAD_DIM
    return pl.pallas_call(
        _inproj_kernel,
        grid=(bsz, s // tm),
        in_specs=[pl.BlockSpec((1, tm, d), lambda b, i: (b, i, 0)),
                  pl.BlockSpec((1, 1, 1, d), lambda b, i: (b, sc_idx, 0, 0)),
                  pl.BlockSpec((1, 1, 1, d), lambda b, i: (b, sh_idx, 0, 0)),
                  pl.BlockSpec((1, d), lambda b, i: (0, 0)),
                  pl.BlockSpec((d, n_nsa), lambda b, i: (0, 0), pipeline_mode=pl.Buffered(1)),
                  pl.BlockSpec((d, n_gla), lambda b, i: (0, 0), pipeline_mode=pl.Buffered(1))],
        out_specs=[pl.BlockSpec((1, n_nsa // dh, tm, dh), lambda b, i: (b, 0, i, 0)),
                   pl.BlockSpec((1, tm, n_gla), lambda b, i: (b, i, 0))],
        out_shape=[jax.ShapeDtypeStruct((bsz, n_nsa // dh, s, dh), F32),
                   jax.ShapeDtypeStruct((bsz, s, n_gla), F32)],
        compiler_params=_cparams(("parallel", "parallel")),
        name="inproj",
    )(x, mod4, mod4, nw, w_nsa, w_gla)


def _hi_lo(x):
    hi = x.astype(BF16)
    return hi, (x - hi.astype(F32)).astype(BF16)


def _dot_hi_lo(x, w):
    x_hi, x_lo = _hi_lo(x)
    w_hi, w_lo = _hi_lo(w)
    return jnp.dot(jnp.concatenate([x_hi, x_lo, x_hi], axis=1), jnp.concatenate([w_hi, w_hi, w_lo], axis=0),
                   preferred_element_type=F32)


def _compress_kernel(a_ref, pos_ref, w1_ref, w2_ref, o_ref):
    nch = a_ref.shape[2] // CMP_STRIDE
    a = jnp.concatenate([a_ref[0, 0, pl.ds(t, nch, stride=CMP_STRIDE), :] for t in range(CMP_STRIDE)], axis=1)
    pos = pos_ref[0]
    half = a.shape[1]
    y1 = _dot_hi_lo(a + pos[0:1], w1_ref[0, :half, :])
    y2 = _dot_hi_lo(a + pos[1:2], w1_ref[0, half:, :])
    nrow = a.shape[0]
    h = y1 + pltpu.roll(y2, nrow - 1, axis=0)
    out = _dot_hi_lo(jax.nn.gelu(h), w2_ref[0])
    row = lax.broadcasted_iota(I32, out.shape, 0)
    o_ref[0, 0] = jnp.where(row < nrow - 1, out, 0.0)


def _compress(proj_nsa, pos, w1, w2):
    bsz, _, s, dh = proj_nsa.shape
    nslot = 2 * NSA_KV_HEADS
    nch = s // CMP_STRIDE
    cw = CMP_STRIDE * dh
    return pl.pallas_call(
        _compress_kernel,
        grid=(bsz, nslot),
        in_specs=[pl.BlockSpec((1, 1, s, dh), lambda b, t: (b, NSA_HEADS + t, 0, 0)),
                  pl.BlockSpec((1, 2, cw), lambda b, t: (t // NSA_KV_HEADS, 0, 0)),
                  pl.BlockSpec((1, 2 * cw, dh), lambda b, t: (t // NSA_KV_HEADS, 0, 0)),
                  pl.BlockSpec((1, dh, dh), lambda b, t: (t // NSA_KV_HEADS, 0, 0))],
        out_specs=pl.BlockSpec((1, 1, nch, dh), lambda b, t: (b, t, 0, 0)),
        out_shape=jax.ShapeDtypeStruct((bsz, nslot, nch, dh), F32),
        compiler_params=_cparams(("parallel", "parallel")),
        name="nsa_compress",
    )(proj_nsa, pos, w1, w2)


def _nt_dot(a, b, **kw):
    return lax.dot_general(a, b, (((1,), (1,)), ((), ())), preferred_element_type=F32, **kw)


def _nsa_kernel(slopes_ref, q_ref, kc_ref, vc_ref, ks_ref, vs_ref, kw_ref, vw_ref, gate_ref, o_ref,
                ksb, vst, kwb, vwt, q4_scr, notsel_scr, *scr):
    g = pl.program_id(1)
    qi = pl.program_id(2)
    tq_n = q_ref.shape[2]
    dh = NSA_HEAD_DIM
    nr = NSA_Q_PER_KV
    seq = ks_ref.shape[2]
    tk_n = NSA_TK
    nb = seq // SEL_BLOCK

    @pl.when(qi == 0)
    def _():
        row = lax.broadcasted_iota(I32, (seq, dh), 0)
        lane = lax.broadcasted_iota(I32, (seq, dh), 1)
        blk = row // SEL_BLOCK
        pos = jnp.where((lane >= nb) & (lane < nb + 3), (blk * SEL_BLOCK).astype(F32),
                        jnp.where((lane >= nb + 3) & (lane < nb + 6), (row % SEL_BLOCK).astype(F32), 0.0))
        ksb[...] = jnp.concatenate([jnp.where(lane == blk, -MASKED, pos), ks_ref[0, 0]], axis=1).astype(BF16)
        kwb[...] = jnp.concatenate([pos, kw_ref[0, 0]], axis=1).astype(BF16)
        for c in range(seq // tk_n):
            rows = slice(c * tk_n, (c + 1) * tk_n)
            for src, dst in ((vs_ref, vst), (vw_ref, vwt)):
                v = src[0, 0, rows, :]
                dst[c] = jnp.concatenate([v, v], axis=1).T[:dh].astype(BF16)

    t0 = qi * tq_n
    tq = t0 + lax.broadcasted_iota(I32, (1, tq_n), 1)
    slopes = [slopes_ref[g * nr + r] for r in range(nr)]
    scale = dh ** -0.5
    q_t = []
    for pair in range(nr // 2):
        both = jnp.concatenate([q_ref[0, 2 * pair], q_ref[0, 2 * pair + 1]], axis=1).T * scale
        q_t += [both[:dh], both[dh:]]

    ncp = kc_ref.shape[2]
    kc = kc_ref[0, 0]
    vc = vc_ref[0, 0]
    vc_t = jnp.concatenate([vc, vc], axis=1).T[:dh].astype(BF16)
    n_sub = lax.broadcasted_iota(I32, (ncp, 1), 0)
    blk_end = n_sub * CMP_STRIDE + (CMP_BLOCK - 1)
    center = n_sub.astype(F32) * CMP_STRIDE + (CMP_BLOCK - 1) / 2.0
    kc_hi, kc_lo = _hi_lo(kc)
    kc_cat = jnp.concatenate([kc_hi, kc_lo, kc_hi, jnp.zeros_like(kc_hi)], axis=1)
    q_hi, q_lo = _hi_lo(jnp.concatenate(q_t, axis=1))
    q_cat = jnp.concatenate([q_hi, q_hi, q_lo, jnp.zeros_like(q_hi)], axis=0)
    s = jnp.dot(kc_cat, q_cat, preferred_element_type=F32)
    tq_all = jnp.concatenate([tq] * nr, axis=1)
    slope_all = jnp.concatenate([jnp.full((1, tq_n), slopes[r], F32) for r in range(nr)], axis=1)
    valid_all = blk_end <= tq_all
    s = jnp.where(valid_all, s - slope_all * (tq_all.astype(F32) - center), NEG)
    e = jnp.exp(s - jnp.max(s, axis=0, keepdims=True))
    p = jnp.where(valid_all, e / jnp.sum(e, axis=0, keepdims=True), 0.0)
    psum = sum(p[:, r * tq_n:(r + 1) * tq_n] for r in range(nr))
    o_c_all = jnp.dot(vc_t, p.astype(BF16), preferred_element_type=F32)
    o_c = [o_c_all[:, r * tq_n:(r + 1) * tq_n] for r in range(nr)]

    n_sel = min(N_SEL, nb)
    notsel_scr[...] = jnp.zeros(notsel_scr.shape, F32)

    @pl.when((qi + 1) * tq_n > n_sel * SEL_BLOCK)
    def _():
        rowj = lax.broadcasted_iota(I32, (LANES, ncp), 0) * SEL_BLOCK
        coln = lax.broadcasted_iota(I32, (LANES, ncp), 1) * CMP_STRIDE
        overlap = jnp.where((coln < rowj + SEL_BLOCK) & (coln + CMP_BLOCK > rowj)
                            & (coln < (ncp - 1) * CMP_STRIDE) & (rowj < nb * SEL_BLOCK), 1.0, 0.0)
        ov = overlap.astype(BF16)
        imp = jnp.dot(jnp.concatenate([ov, ov], axis=1), jnp.concatenate(_hi_lo(psum), axis=0),
                      preferred_element_type=F32)[:nb]
        j_sub = lax.broadcasted_iota(I32, (nb, 1), 0)
        qblk = tq // SEL_BLOCK
        forced = (j_sub == 0) | (j_sub == qblk) | (j_sub == qblk - 1)
        imp = jnp.where(forced, FORCE, jnp.where(j_sub <= qblk, imp, NEG))
        cnt = jnp.zeros((nb, tq_n), F32)
        for i in range(nb):
            ci = imp[i:i + 1, :]
            tie = jnp.where(j_sub > i, 1.0, 0.0)
            cnt = cnt + jnp.where(ci > imp, 1.0, jnp.where(ci == imp, tie, 0.0))
        notsel_scr[...] = jnp.where(cnt < float(n_sel), 0.0, 1.0)

    notsel = notsel_scr[...]

    sub_h = lax.broadcasted_iota(I32, (dh - nb, 1), 0)
    for r in range(nr):
        scol = jnp.zeros((dh - nb, 1), F32)
        for i in range(3):
            piece = slopes_ref[(i + 1) * NSA_HEADS + g * nr + r]
            scol = jnp.where((sub_h == i) | (sub_h == 3 + i), piece, scol)
        q4_scr[:, r * tq_n:(r + 1) * tq_n] = jnp.concatenate(
            [notsel, jnp.broadcast_to(scol, (dh - nb, tq_n)), q_t[r] * LOG2E], axis=0).astype(BF16)

    nwt = WINDOW // tk_n
    nbuf = nwt + 2
    stats_s, stats_w = scr[0:3], scr[3:6]
    s_buf, p_buf, a_buf = scr[6:6 + nbuf], scr[6 + nbuf:6 + 2 * nbuf], scr[6 + 2 * nbuf:6 + 3 * nbuf]
    for m_ref, l_ref, acc_ref in (stats_s, stats_w):
        m_ref[...] = jnp.full(m_ref.shape, NEG, F32)
        l_ref[...] = jnp.zeros(l_ref.shape, F32)
        acc_ref[...] = jnp.zeros(acc_ref.shape, F32)
    key_i = lax.broadcasted_iota(I32, (tk_n, LANES), 0)
    qry_j = lax.broadcasted_iota(I32, (tk_n, LANES), 1)

    def scores(k_ref, kt, buf):
        k_tile = k_ref[kt * tk_n:(kt + 1) * tk_n, :]
        s_buf[buf][...] = jnp.dot(k_tile, q4_scr[...], preferred_element_type=F32)

    def softmax(buf, mode, stats):
        m_ref, l_ref, _ = stats
        for cb in range(nr * tq_n // LANES):
            cols = slice(cb * LANES, (cb + 1) * LANES)
            s = s_buf[buf][:, cols]
            if mode is not None:
                j = qry_j + (cb * LANES) % tq_n
                s = jnp.where(key_i <= j if mode == "causal" else key_i > j, s, -MASKED)
            m_prev = m_ref[:, cols]
            m_new = jnp.maximum(m_prev, jnp.max(s, axis=0, keepdims=True))
            alpha = jnp.exp2(m_prev - m_new)
            p = jnp.exp2(s - m_new)
            l_ref[:, cols] = alpha * l_ref[:, cols] + jnp.sum(p, axis=0, keepdims=True)
            m_ref[:, cols] = m_new
            a_buf[buf][:, cols] = alpha
            p_buf[buf][:, cols] = p.astype(BF16)

    def values(vt_ref, kt, buf, stats):
        acc_ref = stats[2]
        pv = jnp.dot(vt_ref[kt], p_buf[buf][...], preferred_element_type=F32)
        acc_ref[...] = acc_ref[...] * a_buf[buf][...] + pv

    def batch(jobs):
        for i in range(min(nbuf, len(jobs))):
            scores(jobs[i][0], jobs[i][2], i)
        for i, (_, vt_ref, kt, mode, stats) in enumerate(jobs):
            softmax(i % nbuf, mode, stats)
            values(vt_ref, kt, i % nbuf, stats)
            if i + nbuf < len(jobs):
                scores(jobs[i + nbuf][0], jobs[i + nbuf][2], i % nbuf)

    for k in range(seq // tq_n):
        @pl.when(qi == k)
        def _(k=k):
            jobs = [(ksb, vst, t, None, stats_s) for t in range(k)]
            jobs += [(kwb, vwt, k - back, "band" if back == nwt else None, stats_w)
                     for back in range(min(nwt, k), 0, -1)]
            jobs += [(kwb, vwt, k, "causal", stats_w), (ksb, vst, k, "causal", stats_s)]
            batch(jobs)

    gsel = jnp.where(lax.broadcasted_iota(I32, (LANES, LANES), 0)
                     == lax.broadcasted_iota(I32, (LANES, LANES), 1) + g * (nr * N_BRANCH), 1.0, 0.0)
    gs = gsel.astype(BF16)
    gates = jax.nn.sigmoid(jnp.dot(jnp.concatenate(_hi_lo(gate_ref[0]), axis=1), jnp.concatenate([gs, gs], axis=0),
                                   preferred_element_type=F32)).T
    (_, l_s, acc_s), (_, l_w, acc_w) = stats_s, stats_w
    for pair in range(nr // 2):
        o_t = []
        for r in (2 * pair, 2 * pair + 1):
            c0 = r * N_BRANCH
            cols = slice(r * tq_n, (r + 1) * tq_n)
            o_t.append(gates[c0:c0 + 1, :] * o_c[r] + (gates[c0 + 1:c0 + 2, :] / l_s[:, cols]) * acc_s[:, cols]
                       + (gates[c0 + 2:c0 + 3, :] / l_w[:, cols]) * acc_w[:, cols])
        o_ref[0, :, pair * LANES:(pair + 1) * LANES] = jnp.concatenate(o_t, axis=0).T


def _nsa(slopes, proj_nsa, kvc, proj_gla):
    bsz, _, s, dh = proj_nsa.shape
    g_n, nr = NSA_KV_HEADS, NSA_Q_PER_KV
    tq = NSA_TQ
    tk = NSA_TK
    assert tq == tk and WINDOW % tk == 0 and 2 * dh == LANES
    ncp = kvc.shape[2]
    kv0 = NSA_HEADS
    nq = nr * tq
    nbuf = WINDOW // tk + 2

    def kv_spec(i):
        return pl.BlockSpec((1, 1, s, dh), lambda b, g, q, i=i: (b, kv0 + i * g_n + g, 0, 0))

    return pl.pallas_call(
        _nsa_kernel,
        grid=(bsz, g_n, s // tq),
        in_specs=[pl.BlockSpec(memory_space=pltpu.SMEM),
                  pl.BlockSpec((1, nr, tq, dh), lambda b, g, q: (b, g, q, 0)),
                  pl.BlockSpec((1, 1, ncp, dh), lambda b, g, q: (b, g, 0, 0)),
                  pl.BlockSpec((1, 1, ncp, dh), lambda b, g, q: (b, g_n + g, 0, 0)),
                  kv_spec(2), kv_spec(3), kv_spec(4), kv_spec(5),
                  pl.BlockSpec((1, tq, LANES), lambda b, g, q: (b, q, GLA_MISC_OFF // LANES))],
        out_specs=pl.BlockSpec((1, tq, nr * dh), lambda b, g, q: (b, q, g)),
        out_shape=jax.ShapeDtypeStruct((bsz, s, NSA_HEADS * dh), F32),
        scratch_shapes=[pltpu.VMEM((s, LANES), BF16), pltpu.VMEM((s // tk, dh, tk), BF16)] * 2
        + [pltpu.VMEM((LANES, nq), BF16), pltpu.VMEM((s // SEL_BLOCK, tq), F32)]
        + [pltpu.VMEM((1, nq), F32), pltpu.VMEM((1, nq), F32), pltpu.VMEM((dh, nq), F32)] * 2
        + [pltpu.VMEM((tk, nq), F32)] * nbuf + [pltpu.VMEM((tk, nq), BF16)] * nbuf + [pltpu.VMEM((1, nq), F32)] * nbuf,
        compiler_params=_cparams(("parallel", "parallel", "arbitrary")),
        name="nsa_attention",
    )(slopes, proj_nsa, kvc, kvc, proj_nsa, proj_nsa, proj_nsa, proj_nsa, proj_gla)


def _gla_kernel(q_ref, k_ref, v_ref, og_ref, lr_ref, wg_ref, bg_ref, nw_ref, o_ref, st_scr, la_scr, b_scr):
    rows_n = q_ref.shape[1]
    c_n, sub = GLA_CHUNK, GLA_SUB
    nh, dk, dv = GLA_HEADS, GLA_DK, GLA_DV

    @pl.when(pl.program_id(1) == 0)
    def _():
        st_scr[...] = jnp.zeros(st_scr.shape, F32)

    z = jnp.dot(lr_ref[0], wg_ref[...], precision=HI, preferred_element_type=F32) + bg_ref[...]
    la_scr[...] = (jnp.minimum(z, 0.0) - jnp.log1p(jnp.exp(-jnp.abs(z)))) * (1.0 / GLA_GATE_NORM)
    tril = jnp.where(lax.broadcasted_iota(I32, (c_n, c_n), 0) >= lax.broadcasted_iota(I32, (c_n, c_n), 1), 1.0, 0.0)
    row_c = lax.broadcasted_iota(I32, (c_n, 1), 0)
    row_s = lax.broadcasted_iota(I32, (sub, 1), 0)
    lane_c = lax.broadcasted_iota(I32, (1, c_n), 1)
    nw = nw_ref[...]
    hk = [slice(h * dk, (h + 1) * dk) for h in range(nh)]
    hv = [slice(h * dv, (h + 1) * dv) for h in range(nh)]

    def chunk(c, carry):
        r0 = pl.multiple_of(c * c_n, c_n)
        rows = pl.ds(r0, c_n)
        qc = q_ref[0, rows, :] * (dk ** -0.5)
        kc = k_ref[0, rows, :]
        vc = [v_ref[0, rows, hv[h]].astype(BF16) for h in range(nh)]
        b = jnp.dot(tril, la_scr[rows, :], precision=HI, preferred_element_type=F32)
        b_scr[...] = b
        st = [st_scr[h] for h in range(nh)]
        q_e = (qc * jnp.exp(b)).astype(BF16)
        o = [_nt_dot(q_e[:, hk[h]], st[h].astype(BF16)) for h in range(nh)]
        strips = [[] for _ in range(nh)]
        for blk in range(c_n // sub):
            lo = blk * sub
            q_i = qc[lo:lo + sub]
            b_i = b[lo:lo + sub]
            a = [jnp.zeros((sub, c_n), F32) for _ in range(nh)]
            if blk > 0:
                b_r = b_scr[lo - 1:lo, :]
                q_d = (q_i * jnp.exp(b_i - b_r)).astype(BF16)
                k_d = (kc * jnp.exp(jnp.where(row_c < lo, b_r - b, -jnp.inf))).astype(BF16)
                a = [_nt_dot(q_d[:, hk[h]], k_d[:, hk[h]]) for h in range(nh)]
            for j in range(sub):
                b_j = b_scr[lo + j:lo + j + 1, :]
                k_j = k_ref[0, pl.ds(r0 + lo + j, 1), :]
                prod = q_i * k_j * jnp.exp(jnp.where(row_s >= j, b_i - b_j, -jnp.inf))
                for h in range(nh):
                    col = jnp.sum(prod[:, hk[h]], axis=-1, keepdims=True)
                    a[h] = jnp.where(lane_c == lo + j, col, a[h]) if blk == 0 else (
                        a[h] + jnp.where(lane_c == lo + j, col, 0.0))
            for h in range(nh):
                strips[h].append(a[h])
        for h in range(nh):
            attn = jnp.concatenate(strips[h], axis=0)
            o[h] = o[h] + jnp.dot(attn.astype(BF16), vc[h], preferred_element_type=F32)
        b_last = b_scr[c_n - 1:c_n, :]
        k_dec = (kc * jnp.exp(b_last - b)).astype(BF16)
        decay = jnp.exp(b_last)
        for h in range(nh):
            st_scr[h] = st[h] * decay[:, hk[h]] + lax.dot_general(
                vc[h], k_dec[:, hk[h]], (((0,), (0,)), ((), ())), preferred_element_type=F32)
        for h in range(nh):
            og = og_ref[0, rows, hv[h]]
            on = o[h] * lax.rsqrt(jnp.mean(o[h] * o[h], axis=-1, keepdims=True) + EPS) * nw
            o_ref[0, rows, hv[h]] = on * (og * jax.nn.sigmoid(og))
        return carry

    lax.fori_loop(0, rows_n // c_n, chunk, 0)


def _gla(proj_gla, wg_pad, bg, nw):
    bsz, s, _ = proj_gla.shape
    nh, dk, dv = GLA_HEADS, GLA_DK, GLA_DV
    ts = GLA_TS
    wk, wv = nh * dk, nh * dv
    return pl.pallas_call(
        _gla_kernel,
        grid=(bsz, s // ts),
        in_specs=[pl.BlockSpec((1, ts, wk), lambda b, i: (b, i, GLA_Q_OFF // wk)),
                  pl.BlockSpec((1, ts, wk), lambda b, i: (b, i, GLA_K_OFF // wk)),
                  pl.BlockSpec((1, ts, wv), lambda b, i: (b, i, GLA_V_OFF // wv)),
                  pl.BlockSpec((1, ts, wv), lambda b, i: (b, i, GLA_OG_OFF // wv)),
                  pl.BlockSpec((1, ts, LANES), lambda b, i: (b, i, GLA_MISC_OFF // LANES)),
                  pl.BlockSpec((LANES, wk), lambda b, i: (0, 0)),
                  pl.BlockSpec((1, wk), lambda b, i: (0, 0)),
                  pl.BlockSpec((1, dv), lambda b, i: (0, 0))],
        out_specs=pl.BlockSpec((1, ts, wv), lambda b, i: (b, i, 0)),
        out_shape=jax.ShapeDtypeStruct((bsz, s, wv), F32),
        scratch_shapes=[pltpu.VMEM((nh, dv, dk), F32), pltpu.VMEM((ts, wk), F32), pltpu.VMEM((GLA_CHUNK, wk), F32)],
        compiler_params=_cparams(("parallel", "arbitrary")),
        name="gla",
    )(proj_gla, proj_gla, proj_gla, proj_gla, proj_gla, wg_pad, bg, nw)


def _outproj_kernel(nsa_ref, gla_ref, x_ref, wo_ref, g1_ref, sc_ref, sh_ref, nw_ref, wr_ref, br_ref,
                    x1_ref, h_ref, route_ref):
    half = nsa_ref.shape[2]
    acc = jnp.dot(nsa_ref[0].astype(BF16), wo_ref[:half, :], preferred_element_type=F32)
    acc = acc + jnp.dot(gla_ref[0].astype(BF16), wo_ref[half:, :], preferred_element_type=F32)
    x1 = x_ref[0] + g1_ref[0, 0] * acc
    x1_ref[0] = x1
    h = _modulated_norm(x1, nw_ref[...], sc_ref[0, 0], sh_ref[0, 0])
    h_ref[0] = h
    h_hi = h.astype(BF16)
    h_lo = (h - h_hi.astype(F32)).astype(BF16)
    t = jnp.dot(h_hi, wr_ref[...], preferred_element_type=F32)
    logits = (t[:, :LANES] + t[:, LANES:] + jnp.dot(h_lo, wr_ref[:, :LANES], preferred_element_type=F32)
              + br_ref[...])
    lane = lax.broadcasted_iota(I32, (1, LANES), 1)
    ninf = -jnp.inf
    is_g = (lane >= N_EXPERTS) & (lane < N_EXPERTS + N_GROUPS)
    gl = jnp.where(is_g, logits, ninf)
    ge = jnp.exp(gl - jnp.max(gl, axis=-1, keepdims=True))
    gp = ge / jnp.sum(ge, axis=-1, keepdims=True)
    gp_max = jnp.max(gp, axis=-1, keepdims=True)
    grp = jnp.min(jnp.where((gp == gp_max) & is_g, lane - N_EXPERTS, LANES), axis=-1, keepdims=True)
    in_grp = (lane // EXPERTS_PER_GROUP == grp) & (lane < N_EXPERTS)
    el = jnp.where(in_grp, logits, ninf)
    v1 = jnp.max(el, axis=-1, keepdims=True)
    i1 = jnp.min(jnp.where(el == v1, lane, LANES), axis=-1, keepdims=True)
    el2 = jnp.where(lane == i1, ninf, el)
    v2 = jnp.max(el2, axis=-1, keepdims=True)
    i2 = jnp.min(jnp.where(el2 == v2, lane, LANES), axis=-1, keepdims=True)
    e2 = jnp.exp(v2 - v1)
    den = 1.0 + e2
    w1 = gp_max * (1.0 / den)
    w2 = gp_max * (e2 / den)
    route_ref[0] = jnp.where(lane == 0, i1.astype(F32), jnp.where(lane == 1, i2.astype(F32), jnp.where(
        lane == 2, w1, jnp.where(lane == 3, w2, 0.0))))


def _outproj(o_nsa, o_gla, x, wo, mod4, nw, wr, br):
    bsz, s, d = x.shape
    tm = OUT_TM
    half = o_nsa.shape[2]

    def mod_spec(idx):
        return pl.BlockSpec((1, 1, 1, d), lambda b, i: (b, idx, 0, 0))

    row = lambda w: pl.BlockSpec((1, tm, w), lambda b, i: (b, i, 0))
    return pl.pallas_call(
        _outproj_kernel,
        grid=(bsz, s // tm),
        in_specs=[row(half), row(half), row(d),
                  pl.BlockSpec((2 * half, d), lambda b, i: (0, 0)),
                  mod_spec(2), mod_spec(4), mod_spec(3),
                  pl.BlockSpec((1, d), lambda b, i: (0, 0)),
                  pl.BlockSpec((d, 2 * LANES), lambda b, i: (0, 0)),
                  pl.BlockSpec((1, LANES), lambda b, i: (0, 0))],
        out_specs=[row(d), row(d), row(LANES)],
        out_shape=[jax.ShapeDtypeStruct((bsz, s, d), F32), jax.ShapeDtypeStruct((bsz, s, d), F32),
                   jax.ShapeDtypeStruct((bsz, s, LANES), F32)],
        compiler_params=_cparams(("parallel", "parallel")),
        name="outproj_router",
    )(o_nsa, o_gla, x, wo, mod4, mod4, mod4, nw, wr, br)


def _rank_kernel(route_ref, dest_ref, meta_ref, rank_scr):
    n = route_ref.shape[0]
    tm = RANK_TM
    lane_i = lax.broadcasted_iota(I32, (1, LANES), 1)
    lane = lane_i.astype(F32)
    strict = jnp.where(lax.broadcasted_iota(I32, (tm, tm), 0) > lax.broadcasted_iota(I32, (tm, tm), 1),
                       1.0, 0.0).astype(BF16)

    def two_lanes(a, b):
        return jnp.where(lane_i == 0, a, jnp.where(lane_i == 1, b, 0.0))

    def pick(e, table):
        return jnp.sum(jnp.where(lane == e, table, 0.0), axis=-1, keepdims=True)

    def count(i, seen):
        r0 = pl.multiple_of(i * tm, tm)
        rt = route_ref[pl.ds(r0, tm), :]
        e1, e2 = rt[:, 0:1], rt[:, 1:2]
        member = jnp.where(lane == e1, 1.0, jnp.where(lane == e2, 1.0, 0.0))
        before = jnp.dot(strict, member.astype(BF16), preferred_element_type=F32) + seen
        rank_scr[pl.ds(r0, tm), :] = two_lanes(pick(e1, before), pick(e2, before))
        return seen + jnp.sum(member, axis=0, keepdims=True)

    counts = lax.fori_loop(0, n // tm, count, jnp.zeros((1, LANES), F32))
    ntile = jnp.floor((counts + (MOE_TB - 1)) * (1.0 / MOE_TB))
    incl = jnp.where(lax.broadcasted_iota(I32, (LANES, LANES), 0) <= lax.broadcasted_iota(I32, (LANES, LANES), 1),
                     1.0, 0.0).astype(BF16)
    tile_end = jnp.dot(jnp.broadcast_to(ntile, (8, LANES)).astype(BF16), incl,
                       preferred_element_type=F32)[0:1]
    row_start = (tile_end - ntile) * MOE_TB

    def place(i, carry):
        r0 = pl.multiple_of(i * tm, tm)
        rt = route_ref[pl.ds(r0, tm), :]
        rk = rank_scr[pl.ds(r0, tm), :]
        d1 = pick(rt[:, 0:1], row_start) + rk[:, 0:1]
        d2 = pick(rt[:, 1:2], row_start) + rk[:, 1:2]
        dest_ref[pl.ds(r0, tm), :] = two_lanes(d1, d2).astype(I32)
        return carry

    lax.fori_loop(0, n // tm, place, 0)
    trow = lax.broadcasted_iota(I32, (meta_ref.shape[0], 1), 0).astype(F32)
    texp = jnp.sum(jnp.where((tile_end <= trow) & (lane_i < N_EXPERTS), 1.0, 0.0), axis=-1, keepdims=True)
    texp = jnp.minimum(texp, N_EXPERTS - 1.0)
    used = pick(N_EXPERTS - 1.0, tile_end)
    diag = lax.broadcasted_iota(I32, (meta_ref.shape[0], LANES), 0) == lane_i
    end_rows = jnp.sum(jnp.where(diag, tile_end, 0.0), axis=-1, keepdims=True)
    ntile_rows = jnp.sum(jnp.where(diag, ntile, 0.0), axis=-1, keepdims=True)
    meta_ref[...] = jnp.where(lane_i == 2, end_rows, jnp.where(lane_i == 3, ntile_rows, two_lanes(
        texp, jnp.broadcast_to(used, texp.shape)))).astype(I32)


def _rank(route):
    n = route.shape[0]
    return pl.pallas_call(
        _rank_kernel,
        out_shape=[jax.ShapeDtypeStruct((n, LANES), I32), jax.ShapeDtypeStruct((LANES, LANES), I32)],
        scratch_shapes=[pltpu.VMEM((n, LANES), F32)],
        compiler_params=pltpu.CompilerParams(vmem_limit_bytes=VMEM_LIMIT),
        name="moe_rank",
    )(route)


def _dispatch_kernel(dest_ref, ends_ref, h_ref, xs_ref, zero_scr, sem, zsem):
    i = pl.program_id(0)
    tm = h_ref.shape[0]
    tb = zero_scr.shape[0]

    @pl.when(i == 0)
    def _():
        zero_scr[...] = jnp.zeros(zero_scr.shape, F32)

        def zero_copy(e):
            r0 = pl.multiple_of((ends_ref[e] - 1) * tb, tb)
            return pltpu.make_async_copy(zero_scr, xs_ref.at[pl.ds(r0, tb)], zsem)

        def start(e, carry):
            @pl.when(ends_ref[N_EXPERTS + e] > 0)
            def _():
                zero_copy(e).start()
            return carry

        def wait(e, carry):
            @pl.when(ends_ref[N_EXPERTS + e] > 0)
            def _():
                zero_copy(e).wait()
            return carry

        def tail_copy(t):
            return pltpu.make_async_copy(zero_scr, xs_ref.at[pl.ds(pl.multiple_of(t * tb, tb), tb)], zsem)

        def tail_start(t, carry):
            tail_copy(t).start()
            return carry

        def tail_wait(t, carry):
            tail_copy(t).wait()
            return carry

        used = ends_ref[2 * N_EXPERTS]
        lax.fori_loop(0, N_EXPERTS, start, 0)
        lax.fori_loop(used, xs_ref.shape[0] // tb, tail_start, 0)
        lax.fori_loop(0, N_EXPERTS, wait, 0)
        lax.fori_loop(used, xs_ref.shape[0] // tb, tail_wait, 0)

    def issue(r, carry):
        p = (i * tm + r) * TOP_K
        for k in range(TOP_K):
            pltpu.make_async_copy(h_ref.at[pl.ds(r, 1)], xs_ref.at[pl.ds(dest_ref[p + k], 1)], sem).start()
        return carry

    lax.fori_loop(0, tm, issue, 0, unroll=DMA_UNROLL)
    for k in range(TOP_K):
        pltpu.make_async_copy(h_ref, xs_ref.at[pl.ds(0, tm)], sem).wait()


def _dispatch(dest_flat, ends_flat, h, cap):
    n, d = h.shape
    tm = DISPATCH_TM
    return pl.pallas_call(
        _dispatch_kernel,
        grid_spec=pltpu.PrefetchScalarGridSpec(
            num_scalar_prefetch=2, grid=(n // tm,),
            in_specs=[pl.BlockSpec((tm, d), lambda i, dst, ends: (i, 0))],
            out_specs=pl.BlockSpec(memory_space=pl.ANY),
            scratch_shapes=[pltpu.VMEM((MOE_TB, d), F32), pltpu.SemaphoreType.DMA(()), pltpu.SemaphoreType.DMA(())]),
        out_shape=jax.ShapeDtypeStruct((cap, d), F32),
        compiler_params=_cparams(("arbitrary",)),
        name="moe_dispatch",
    )(dest_flat, ends_flat, h)


def _ffn_kernel(meta_ref, x_ref, wg_hbm, wu_hbm, wd_hbm, y_ref, wg_st, wu_st, wd_st, wgb, wub, wdb, wsem):
    t = pl.program_id(0)
    ntile = pl.num_programs(0)
    n_used = meta_ref[ntile]
    e = meta_ref[t]
    e_prev = meta_ref[jnp.maximum(t - 1, 0)]
    active = t < n_used

    def fetch(expert):
        return [pltpu.make_async_copy(src.at[expert], dst, wsem)
                for src, dst in ((wg_hbm, wg_st), (wu_hbm, wu_st), (wd_hbm, wd_st))]

    @pl.when(t == 0)
    def _():
        for cp in fetch(e):
            cp.start()

    @pl.when(active & ((t == 0) | (e != e_prev)))
    def _():
        for cp in fetch(e):
            cp.wait()
        wgb[...] = wg_st[...].astype(BF16)
        wub[...] = wu_st[...].astype(BF16)
        wdb[...] = wd_st[...].astype(BF16)
        end = meta_ref[ntile + 1 + e]

        @pl.when(end < n_used)
        def _():
            for cp in fetch(meta_ref[end]):
                cp.start()

    @pl.when(active)
    def _():
        x = x_ref[...].astype(BF16)
        gate = jnp.dot(x, wgb[...], preferred_element_type=F32)
        up = jnp.dot(x, wub[...], preferred_element_type=F32)
        act = (gate * jax.nn.sigmoid(gate)) * up
        y_ref[...] = jnp.dot(act.astype(BF16), wdb[...], preferred_element_type=F32)

    @pl.when(jnp.logical_not(active))
    def _():
        y_ref[...] = jnp.zeros(y_ref.shape, F32)


def _ffn(meta_flat, xs, wg, wu, wd):
    cap, d = xs.shape
    ff = wg.shape[2]
    tb = MOE_TB
    ntile = cap // tb
    return pl.pallas_call(
        _ffn_kernel,
        grid_spec=pltpu.PrefetchScalarGridSpec(
            num_scalar_prefetch=1, grid=(ntile,),
            in_specs=[pl.BlockSpec((tb, d), lambda t, m: (jnp.minimum(t, m[ntile] - 1), 0)),
                      pl.BlockSpec(memory_space=pl.ANY), pl.BlockSpec(memory_space=pl.ANY),
                      pl.BlockSpec(memory_space=pl.ANY)],
            out_specs=pl.BlockSpec((tb, d), lambda t, m: (t, 0)),
            scratch_shapes=[pltpu.VMEM((d, ff), F32), pltpu.VMEM((d, ff), F32), pltpu.VMEM((ff, d), F32),
                            pltpu.VMEM((d, ff), BF16), pltpu.VMEM((d, ff), BF16), pltpu.VMEM((ff, d), BF16),
                            pltpu.SemaphoreType.DMA(())]),
        out_shape=jax.ShapeDtypeStruct((cap, d), F32),
        compiler_params=_cparams(("arbitrary",)),
        name="moe_ffn",
    )(meta_flat, xs, wg, wu, wd)


def _combine_kernel(dest_ref, y_ref, x1_ref, route_ref, g2_ref, nf_ref, o_ref, ybuf, sem):
    i = pl.program_id(0)
    nstep = pl.num_programs(0)
    tm = x1_ref.shape[0]

    def issue(tile, slot):
        def body(r, carry):
            p = (tile * tm + r) * TOP_K
            for k in range(TOP_K):
                pltpu.make_async_copy(y_ref.at[pl.ds(dest_ref[p + k], 1)], ybuf.at[slot, k, pl.ds(r, 1)],
                                      sem.at[slot]).start()
            return carry

        lax.fori_loop(0, tm, body, 0, unroll=DMA_UNROLL)

    @pl.when(i == 0)
    def _():
        issue(0, 0)

    @pl.when(i + 1 < nstep)
    def _():
        issue(i + 1, (i + 1) % 2)

    slot = i % 2
    for k in range(TOP_K):
        pltpu.make_async_copy(y_ref.at[pl.ds(0, tm)], ybuf.at[slot, k], sem.at[slot]).wait()
    rt = route_ref[...]
    moe = rt[:, 2:3] * ybuf[slot, 0] + rt[:, 3:4] * ybuf[slot, 1]
    xo = x1_ref[...] + g2_ref[0, 0] * moe
    o_ref[...] = xo * lax.rsqrt(jnp.mean(xo * xo, axis=-1, keepdims=True) + EPS) * nf_ref[...]


def _combine(dest_flat, y, x1, route, mod4, nf, seq):
    n, d = x1.shape
    tm = COMB_TM
    tiles_per_seq = seq // tm
    return pl.pallas_call(
        _combine_kernel,
        grid_spec=pltpu.PrefetchScalarGridSpec(
            num_scalar_prefetch=1, grid=(n // tm,),
            in_specs=[pl.BlockSpec(memory_space=pl.ANY),
                      pl.BlockSpec((tm, d), lambda i, dst: (i, 0)),
                      pl.BlockSpec((tm, LANES), lambda i, dst: (i, 0)),
                      pl.BlockSpec((1, 1, 1, d), lambda i, dst: (i // tiles_per_seq, 5, 0, 0)),
                      pl.BlockSpec((1, d), lambda i, dst: (0, 0))],
            out_specs=pl.BlockSpec((tm, d), lambda i, dst: (i, 0)),
            scratch_shapes=[pltpu.VMEM((2, TOP_K, tm, d), F32), pltpu.SemaphoreType.DMA((2,))]),
        out_shape=jax.ShapeDtypeStruct((n, d), F32),
        compiler_params=_cparams(("arbitrary",)),
        name="moe_combine",
    )(dest_flat, y, x1, route, mod4, nf)


def _alibi_slopes():
    n = NSA_HEADS
    full = jnp.asarray(2.0 ** (-8.0 * np.arange(1, n + 1) / n), dtype=F32)
    pieces, rest = [], full * LOG2E
    for _ in range(3):
        piece = rest.astype(BF16).astype(F32)
        pieces.append(piece)
        rest = rest - piece
    return jnp.concatenate([full] + pieces)


def _layer(x, c, w_ada, b_ada, norm1_w, w_in, cmp_pos_k, cmp_w1_k, cmp_w2_k, cmp_pos_v, cmp_w1_v, cmp_w2_v,
           gla_w_gate2, gla_b_gate, gla_norm_w, w_out, norm2_w, w_rg, b_rg, w_re, b_re, w_eg, w_eu, w_ed):
    bsz, s, d = x.shape
    dh = NSA_HEAD_DIM
    mod4 = _adaln(c, w_ada, b_ada).reshape(bsz, 6, 1, d)

    o_gate = NSA_COLS
    o_gla = o_gate + NSA_GATE_COLS
    o_lr = o_gla + 2 * GLA_HEADS * GLA_DK + 2 * GLA_HEADS * GLA_DV
    w_nsa, w_gla = _prep_w_in(w_in.T, o_gate, o_gla, o_lr)
    nw1 = norm1_w.reshape(1, d)
    proj_nsa, proj_gla = _inproj(x, mod4, nw1, w_nsa, w_gla, sc_idx=1, sh_idx=0)

    pos = jnp.stack([cmp_pos_k, cmp_pos_v]).reshape(2, 2, CMP_STRIDE * dh)
    kvc = _compress(proj_nsa, pos, jnp.stack([cmp_w1_k, cmp_w1_v]), jnp.stack([cmp_w2_k, cmp_w2_v]))
    o_nsa = _nsa(_alibi_slopes(), proj_nsa, kvc, proj_gla)

    wg_pad = jnp.zeros((LANES, GLA_HEADS * GLA_DK), F32).at[
        NSA_GATE_COLS:NSA_GATE_COLS + GLA_GATE_RANK].set(gla_w_gate2)
    o_gla_out = _gla(proj_gla, wg_pad, gla_b_gate.reshape(1, -1), gla_norm_w.reshape(1, -1))

    wr = jnp.concatenate([w_re, w_rg, jnp.zeros((d, LANES - N_EXPERTS - N_GROUPS), F32)], axis=1)
    br = jnp.concatenate([b_re, b_rg, jnp.zeros((LANES - N_EXPERTS - N_GROUPS,), F32)]).reshape(1, LANES)
    wr_hi = wr.astype(BF16)
    wr_cat = jnp.concatenate([wr_hi, (wr - wr_hi.astype(F32)).astype(BF16)], axis=1)
    x1, h2, route = _outproj(o_nsa, o_gla_out, x, w_out.astype(BF16), mod4, norm2_w.reshape(1, d), wr_cat, br)

    n = bsz * s
    npair = n * TOP_K
    cap = npair + N_EXPERTS * MOE_TB
    ntile = cap // MOE_TB
    route2 = route.reshape(n, LANES)
    dest, meta = _rank(route2)
    dest_flat = dest[:, :TOP_K].reshape(npair)
    meta_flat = jnp.concatenate([meta[:ntile, 0], meta[:1, 1], meta[:N_EXPERTS, 2]])
    ends_flat = jnp.concatenate([meta[:N_EXPERTS, 2], meta[:N_EXPERTS, 3], meta[:1, 1]])
    xs = _dispatch(dest_flat, ends_flat, h2.reshape(n, d), cap)
    y = _ffn(meta_flat, xs, w_eg, w_eu, w_ed)
    return x1.reshape(n, d), y, dest_flat, route2, mod4


def kernel(x, c, w_ada, b_ada, norm1_w, w_in, cmp_pos_k, cmp_w1_k, cmp_w2_k, cmp_pos_v, cmp_w1_v, cmp_w2_v,
           gla_w_gate2, gla_b_gate, gla_norm_w, w_out, norm2_w, w_router_group, b_router_group, w_router_expert,
           b_router_expert, w_expert_gate, w_expert_up, w_expert_down, norm_f_w):
    bsz, s, d = x.shape
    assert w_ada.shape[0] == 1, "single layer"
    x1, y, dest_flat, route2, mod4 = _layer(
        x, c, w_ada[0], b_ada[0], norm1_w[0], w_in[0], cmp_pos_k[0], cmp_w1_k[0], cmp_w2_k[0], cmp_pos_v[0],
        cmp_w1_v[0], cmp_w2_v[0], gla_w_gate2[0], gla_b_gate[0], gla_norm_w[0], w_out[0], norm2_w[0],
        w_router_group[0], b_router_group[0], w_router_expert[0], b_router_expert[0],
        w_expert_gate[0], w_expert_up[0], w_expert_down[0])
    out = _combine(dest_flat, y, x1, route2, mod4, norm_f_w.reshape(1, d), s)
    return out.reshape(bsz, s, d)
```

```python
import functools

import numpy as np
import jax
import jax.numpy as jnp
from jax import lax
from jax.experimental import pallas as pl
from jax.experimental.pallas import tpu as pltpu

F32 = jnp.float32
BF16 = jnp.bfloat16
I32 = jnp.int32
HI = lax.Precision.HIGHEST

D_MODEL = 2048
NSA_HEAD_DIM = 64
NSA_HEADS = 16
NSA_KV_HEADS = 4
NSA_Q_PER_KV = 4
CMP_BLOCK = 32
CMP_STRIDE = 16
SEL_BLOCK = 64
N_SEL = 16
WINDOW = 512
N_BRANCH = 3
GLA_HEADS = 4
GLA_DV = 256
GLA_DK = 128
GLA_GATE_RANK = 16
GLA_GATE_NORM = 16.0
GLA_CHUNK = 64
GLA_SUB = 16
N_GROUPS = 4
EXPERTS_PER_GROUP = 8
N_EXPERTS = 32
TOP_K = 2
EXPERT_FF = 512
EPS = 1e-6
NEG = -1e30
FORCE = 1e30
LOG2E = 1.4426950408889634
MASKED = 2.0 ** 100

NSA_Q_COLS = NSA_HEADS * NSA_HEAD_DIM
NSA_KV_COLS = 2 * N_BRANCH * NSA_KV_HEADS * NSA_HEAD_DIM
NSA_GATE_COLS = N_BRANCH * NSA_HEADS
NSA_COLS = NSA_Q_COLS + NSA_KV_COLS
NSA_SLOTS = NSA_COLS // NSA_HEAD_DIM
GLA_Q_OFF = 0
GLA_K_OFF = GLA_HEADS * GLA_DK
GLA_V_OFF = 2 * GLA_HEADS * GLA_DK
GLA_OG_OFF = GLA_V_OFF + GLA_HEADS * GLA_DV
GLA_MISC_OFF = GLA_OG_OFF + GLA_HEADS * GLA_DV
LANES = 128
SLAB = D_MODEL // LANES
GLA_COLS = GLA_MISC_OFF + LANES

VMEM_LIMIT = 56 * 1024 * 1024

ADA_TN = 768
PREP_TR = 256
PREP_STEP = 512
INPROJ_TM = 256
INPROJ_TN = 512
NSA_TQ = 256
NSA_TK = 256
GLA_TS = 512
OUT_TM = 256
RANK_TM = 256
MOE_TB = 256
DISPATCH_TM = 256
COMB_TM = 256
DMA_UNROLL = 8


def _cparams(sem):
    return pltpu.CompilerParams(dimension_semantics=sem, vmem_limit_bytes=VMEM_LIMIT)


def _adaln_kernel(ct_ref, w_ref, b_ref, o_ref, s_scr):
    nb = ct_ref.shape[1]
    kdim, tn = w_ref.shape

    @pl.when(pl.program_id(0) == 0)
    def _():
        ct = ct_ref[...]
        s = ct * jax.nn.sigmoid(ct)
        for b in range(nb):
            s_scr[b] = jnp.broadcast_to(s[:, b:b + 1], (kdim, LANES))

    def body(k, accs):
        r = pl.multiple_of(k * 8, 8)
        w8 = w_ref[pl.ds(r, 8), :]
        out = []
        for b, acc in enumerate(accs):
            s8 = s_scr[b, pl.ds(r, 8), :]
            out.append(acc + w8 * jnp.concatenate([s8] * (tn // LANES), axis=1))
        return tuple(out)

    accs = lax.fori_loop(0, kdim // 8, body, tuple(jnp.zeros((8, tn), F32) for _ in range(nb)), unroll=2)
    bias = b_ref[...]
    for b, acc in enumerate(accs):
        o_ref[b:b + 1, :] = jnp.sum(acc, axis=0, keepdims=True) + bias


def _adaln(c, w, b):
    nb, d = c.shape
    n = w.shape[1]
    return pl.pallas_call(
        _adaln_kernel,
        grid=(n // ADA_TN,),
        in_specs=[pl.BlockSpec((d, nb), lambda j: (0, 0)),
                  pl.BlockSpec((d, ADA_TN), lambda j: (0, j)),
                  pl.BlockSpec((1, ADA_TN), lambda j: (0, j))],
        out_specs=pl.BlockSpec((nb, ADA_TN), lambda j: (0, j)),
        out_shape=jax.ShapeDtypeStruct((nb, n), F32),
        scratch_shapes=[pltpu.VMEM((nb, d, LANES), F32)],
        compiler_params=_cparams(("arbitrary",)),
        name="adaln",
    )(c.T, w, b.reshape(1, n))


def _modulated_norm(x, nw, sc, sh):
    ms = jnp.mean(x * x, axis=-1, keepdims=True)
    h = x * lax.rsqrt(ms + EPS) * nw
    return h * (1.0 + sc) + sh


def _prep_w_in_kernel(wt_ref, wn_ref, wg_ref, *, o_gate, o_gla, o_lr):
    tr = wt_ref.shape[1]
    step = PREP_STEP

    def put(dst, c0, rows):
        dst[:, c0:c0 + rows.shape[0]] = rows.T.astype(BF16)

    for c in range(0, o_gate, step):
        put(wn_ref, c, wt_ref[c:c + step, :])
    for c in range(0, o_lr - o_gla, step):
        put(wg_ref, c, wt_ref[o_gla + c:o_gla + c + step, :])
    pad = wg_ref.shape[1] - (wt_ref.shape[0] - o_gate)
    misc = jnp.concatenate([wt_ref[o_gate:o_gla, :], wt_ref[o_lr:, :], jnp.zeros((pad, tr), F32)], axis=0)
    put(wg_ref, o_lr - o_gla, misc)


def _prep_w_in(w_in_t, o_gate, o_gla, o_lr):
    n, d = w_in_t.shape
    tr = PREP_TR
    assert o_gate % PREP_STEP == 0 and (o_lr - o_gla) % PREP_STEP == 0
    return pl.pallas_call(
        functools.partial(_prep_w_in_kernel, o_gate=o_gate, o_gla=o_gla, o_lr=o_lr),
        grid=(d // tr,),
        in_specs=[pl.BlockSpec((n, tr), lambda i: (0, i))],
        out_specs=[pl.BlockSpec((tr, NSA_COLS), lambda i: (i, 0)), pl.BlockSpec((tr, GLA_COLS), lambda i: (i, 0))],
        out_shape=[jax.ShapeDtypeStruct((d, NSA_COLS), BF16), jax.ShapeDtypeStruct((d, GLA_COLS), BF16)],
        compiler_params=_cparams(("parallel",)),
        name="prep_w_in",
    )(w_in_t)


def _inproj_kernel(x_ref, sc_ref, sh_ref, nw_ref, wn_ref, wg_ref, on_ref, og_ref):
    h = _modulated_norm(x_ref[0], nw_ref[...], sc_ref[0, 0], sh_ref[0, 0]).astype(BF16)
    dh = NSA_HEAD_DIM
    tn = INPROJ_TN
    for c in range(wn_ref.shape[1] // tn):
        acc = jnp.dot(h, wn_ref[:, c * tn:(c + 1) * tn], preferred_element_type=F32)
        for u in range(tn // dh):
            on_ref[0, c * (tn // dh) + u] = acc[:, u * dh:(u + 1) * dh]
    og_ref[0] = jnp.dot(h, wg_ref[...], preferred_element_type=F32)


def _inproj(x, mod4, nw, w_nsa, w_gla, *, sc_idx, sh_idx):
    bsz, s, d = x.shape
    n_nsa, n_gla = w_nsa.shape[1], w_gla.shape[1]
    tm = INPROJ_TM
    dh = NSA_HEAD_DIM
    return pl.pallas_call(
        _inproj_kernel,
        grid=(bsz, s // tm),
        in_specs=[pl.BlockSpec((1, tm, d), lambda b, i: (b, i, 0)),
                  pl.BlockSpec((1, 1, 1, d), lambda b, i: (b, sc_idx, 0, 0)),
                  pl.BlockSpec((1, 1, 1, d), lambda b, i: (b, sh_idx, 0, 0)),
                  pl.BlockSpec((1, d), lambda b, i: (0, 0)),
                  pl.BlockSpec((d, n_nsa), lambda b, i: (0, 0), pipeline_mode=pl.Buffered(1)),
                  pl.BlockSpec((d, n_gla), lambda b, i: (0, 0), pipeline_mode=pl.Buffered(1))],
        out_specs=[pl.BlockSpec((1, n_nsa // dh, tm, dh), lambda b, i: (b, 0, i, 0)),
                   pl.BlockSpec((1, tm, n_gla), lambda b, i: (b, i, 0))],
        out_shape=[jax.ShapeDtypeStruct((bsz, n_nsa // dh, s, dh), F32),
                   jax.ShapeDtypeStruct((bsz, s, n_gla), F32)],
        compiler_params=_cparams(("parallel", "parallel")),
        name="inproj",
    )(x, mod4, mod4, nw, w_nsa, w_gla)


def _hi_lo(x):
    hi = x.astype(BF16)
    return hi, (x - hi.astype(F32)).astype(BF16)


def _dot_hi_lo(x, w):
    x_hi, x_lo = _hi_lo(x)
    w_hi, w_lo = _hi_lo(w)
    return jnp.dot(jnp.concatenate([x_hi, x_lo, x_hi], axis=1), jnp.concatenate([w_hi, w_hi, w_lo], axis=0),
                   preferred_element_type=F32)


def _compress_kernel(a_ref, pos_ref, w1_ref, w2_ref, o_ref):
    nch = a_ref.shape[2] // CMP_STRIDE
    a = jnp.concatenate([a_ref[0, 0, pl.ds(t, nch, stride=CMP_STRIDE), :] for t in range(CMP_STRIDE)], axis=1)
    pos = pos_ref[0]
    half = a.shape[1]
    y1 = _dot_hi_lo(a + pos[0:1], w1_ref[0, :half, :])
    y2 = _dot_hi_lo(a + pos[1:2], w1_ref[0, half:, :])
    nrow = a.shape[0]
    h = y1 + pltpu.roll(y2, nrow - 1, axis=0)
    out = _dot_hi_lo(jax.nn.gelu(h), w2_ref[0])
    row = lax.broadcasted_iota(I32, out.shape, 0)
    o_ref[0, 0] = jnp.where(row < nrow - 1, out, 0.0)


def _compress(proj_nsa, pos, w1, w2):
    bsz, _, s, dh = proj_nsa.shape
    nslot = 2 * NSA_KV_HEADS
    nch = s // CMP_STRIDE
    cw = CMP_STRIDE * dh
    return pl.pallas_call(
        _compress_kernel,
        grid=(bsz, nslot),
        in_specs=[pl.BlockSpec((1, 1, s, dh), lambda b, t: (b, NSA_HEADS + t, 0, 0)),
                  pl.BlockSpec((1, 2, cw), lambda b, t: (t // NSA_KV_HEADS, 0, 0)),
                  pl.BlockSpec((1, 2 * cw, dh), lambda b, t: (t // NSA_KV_HEADS, 0, 0)),
                  pl.BlockSpec((1, dh, dh), lambda b, t: (t // NSA_KV_HEADS, 0, 0))],
        out_specs=pl.BlockSpec((1, 1, nch, dh), lambda b, t: (b, t, 0, 0)),
        out_shape=jax.ShapeDtypeStruct((bsz, nslot, nch, dh), F32),
        compiler_params=_cparams(("parallel", "parallel")),
        name="nsa_compress",
    )(proj_nsa, pos, w1, w2)


def _nt_dot(a, b, **kw):
    return lax.dot_general(a, b, (((1,), (1,)), ((), ())), preferred_element_type=F32, **kw)


def _nsa_kernel(slopes_ref, q_ref, kc_ref, vc_ref, ks_ref, vs_ref, kw_ref, vw_ref, gate_ref, o_ref,
                ksb, vst, kwb, vwt, q4_scr, notsel_scr, *scr):
    g = pl.program_id(1)
    qi = pl.program_id(2)
    tq_n = q_ref.shape[2]
    dh = NSA_HEAD_DIM
    nr = NSA_Q_PER_KV
    seq = ks_ref.shape[2]
    tk_n = NSA_TK
    nb = seq // SEL_BLOCK

    @pl.when(qi == 0)
    def _():
        row = lax.broadcasted_iota(I32, (seq, dh), 0)
        lane = lax.broadcasted_iota(I32, (seq, dh), 1)
        blk = row // SEL_BLOCK
        pos = jnp.where((lane >= nb) & (lane < nb + 3), (blk * SEL_BLOCK).astype(F32),
                        jnp.where((lane >= nb + 3) & (lane < nb + 6), (row % SEL_BLOCK).astype(F32), 0.0))
        ksb[...] = jnp.concatenate([jnp.where(lane == blk, -MASKED, pos), ks_ref[0, 0]], axis=1).astype(BF16)
        kwb[...] = jnp.concatenate([pos, kw_ref[0, 0]], axis=1).astype(BF16)
        for c in range(seq // tk_n):
            rows = slice(c * tk_n, (c + 1) * tk_n)
            for src, dst in ((vs_ref, vst), (vw_ref, vwt)):
                v = src[0, 0, rows, :]
                dst[c] = jnp.concatenate([v, v], axis=1).T[:dh].astype(BF16)

    t0 = qi * tq_n
    tq = t0 + lax.broadcasted_iota(I32, (1, tq_n), 1)
    slopes = [slopes_ref[g * nr + r] for r in range(nr)]
    scale = dh ** -0.5
    q_t = []
    for pair in range(nr // 2):
        both = jnp.concatenate([q_ref[0, 2 * pair], q_ref[0, 2 * pair + 1]], axis=1).T * scale
        q_t += [both[:dh], both[dh:]]

    ncp = kc_ref.shape[2]
    kc = kc_ref[0, 0]
    vc = vc_ref[0, 0]
    vc_t = jnp.concatenate([vc, vc], axis=1).T[:dh].astype(BF16)
    n_sub = lax.broadcasted_iota(I32, (ncp, 1), 0)
    blk_end = n_sub * CMP_STRIDE + (CMP_BLOCK - 1)
    center = n_sub.astype(F32) * CMP_STRIDE + (CMP_BLOCK - 1) / 2.0
    kc_hi, kc_lo = _hi_lo(kc)
    kc_cat = jnp.concatenate([kc_hi, kc_lo, kc_hi, jnp.zeros_like(kc_hi)], axis=1)
    q_hi, q_lo = _hi_lo(jnp.concatenate(q_t, axis=1))
    q_cat = jnp.concatenate([q_hi, q_hi, q_lo, jnp.zeros_like(q_hi)], axis=0)
    s = jnp.dot(kc_cat, q_cat, preferred_element_type=F32)
    tq_all = jnp.concatenate([tq] * nr, axis=1)
    slope_all = jnp.concatenate([jnp.full((1, tq_n), slopes[r], F32) for r in range(nr)], axis=1)
    valid_all = blk_end <= tq_all
    s = jnp.where(valid_all, s - slope_all * (tq_all.astype(F32) - center), NEG)
    e = jnp.exp(s - jnp.max(s, axis=0, keepdims=True))
    p = jnp.where(valid_all, e / jnp.sum(e, axis=0, keepdims=True), 0.0)
    psum = sum(p[:, r * tq_n:(r + 1) * tq_n] for r in range(nr))
    o_c_all = jnp.dot(vc_t, p.astype(BF16), preferred_element_type=F32)
    o_c = [o_c_all[:, r * tq_n:(r + 1) * tq_n] for r in range(nr)]

    n_sel = min(N_SEL, nb)
    notsel_scr[...] = jnp.zeros(notsel_scr.shape, F32)

    @pl.when((qi + 1) * tq_n > n_sel * SEL_BLOCK)
    def _():
        rowj = lax.broadcasted_iota(I32, (LANES, ncp), 0) * SEL_BLOCK
        coln = lax.broadcasted_iota(I32, (LANES, ncp), 1) * CMP_STRIDE
        overlap = jnp.where((coln < rowj + SEL_BLOCK) & (coln + CMP_BLOCK > rowj)
                            & (coln < (ncp - 1) * CMP_STRIDE) & (rowj < nb * SEL_BLOCK), 1.0, 0.0)
        ov = overlap.astype(BF16)
        imp = jnp.dot(jnp.concatenate([ov, ov], axis=1), jnp.concatenate(_hi_lo(psum), axis=0),
                      preferred_element_type=F32)[:nb]
        j_sub = lax.broadcasted_iota(I32, (nb, 1), 0)
        qblk = tq // SEL_BLOCK
        forced = (j_sub == 0) | (j_sub == qblk) | (j_sub == qblk - 1)
        imp = jnp.where(forced, FORCE, jnp.where(j_sub <= qblk, imp, NEG))
        cnt = jnp.zeros((nb, tq_n), F32)
        for i in range(nb):
            ci = imp[i:i + 1, :]
            tie = jnp.where(j_sub > i, 1.0, 0.0)
            cnt = cnt + jnp.where(ci > imp, 1.0, jnp.where(ci == imp, tie, 0.0))
        notsel_scr[...] = jnp.where(cnt < float(n_sel), 0.0, 1.0)

    notsel = notsel_scr[...]

    sub_h = lax.broadcasted_iota(I32, (dh - nb, 1), 0)
    for r in range(nr):
        scol = jnp.zeros((dh - nb, 1), F32)
        for i in range(3):
            piece = slopes_ref[(i + 1) * NSA_HEADS + g * nr + r]
            scol = jnp.where((sub_h == i) | (sub_h == 3 + i), piece, scol)
        q4_scr[:, r * tq_n:(r + 1) * tq_n] = jnp.concatenate(
            [notsel, jnp.broadcast_to(scol, (dh - nb, tq_n)), q_t[r] * LOG2E], axis=0).astype(BF16)

    nwt = WINDOW // tk_n
    nbuf = nwt + 2
    stats_s, stats_w = scr[0:3], scr[3:6]
    s_buf, p_buf, a_buf = scr[6:6 + nbuf], scr[6 + nbuf:6 + 2 * nbuf], scr[6 + 2 * nbuf:6 + 3 * nbuf]
    for m_ref, l_ref, acc_ref in (stats_s, stats_w):
        m_ref[...] = jnp.full(m_ref.shape, NEG, F32)
        l_ref[...] = jnp.zeros(l_ref.shape, F32)
        acc_ref[...] = jnp.zeros(acc_ref.shape, F32)
    key_i = lax.broadcasted_iota(I32, (tk_n, LANES), 0)
    qry_j = lax.broadcasted_iota(I32, (tk_n, LANES), 1)

    def scores(k_ref, kt, buf):
        k_tile = k_ref[kt * tk_n:(kt + 1) * tk_n, :]
        s_buf[buf][...] = jnp.dot(k_tile, q4_scr[...], preferred_element_type=F32)

    def softmax(buf, mode, stats):
        m_ref, l_ref, _ = stats
        for cb in range(nr * tq_n // LANES):
            cols = slice(cb * LANES, (cb + 1) * LANES)
            s = s_buf[buf][:, cols]
            if mode is not None:
                j = qry_j + (cb * LANES) % tq_n
                s = jnp.where(key_i <= j if mode == "causal" else key_i > j, s, -MASKED)
            m_prev = m_ref[:, cols]
            m_new = jnp.maximum(m_prev, jnp.max(s, axis=0, keepdims=True))
            alpha = jnp.exp2(m_prev - m_new)
            p = jnp.exp2(s - m_new)
            l_ref[:, cols] = alpha * l_ref[:, cols] + jnp.sum(p, axis=0, keepdims=True)
            m_ref[:, cols] = m_new
            a_buf[buf][:, cols] = alpha
            p_buf[buf][:, cols] = p.astype(BF16)

    def values(vt_ref, kt, buf, stats):
        acc_ref = stats[2]
        pv = jnp.dot(vt_ref[kt], p_buf[buf][...], preferred_element_type=F32)
        acc_ref[...] = acc_ref[...] * a_buf[buf][...] + pv

    def batch(jobs):
        for i in range(min(nbuf, len(jobs))):
            scores(jobs[i][0], jobs[i][2], i)
        for i, (_, vt_ref, kt, mode, stats) in enumerate(jobs):
            softmax(i % nbuf, mode, stats)
            values(vt_ref, kt, i % nbuf, stats)
            if i + nbuf < len(jobs):
                scores(jobs[i + nbuf][0], jobs[i + nbuf][2], i % nbuf)

    for k in range(seq // tq_n):
        @pl.when(qi == k)
        def _(k=k):
            jobs = [(ksb, vst, t, None, stats_s) for t in range(k)]
            jobs += [(kwb, vwt, k - back, "band" if back == nwt else None, stats_w)
                     for back in range(min(nwt, k), 0, -1)]
            jobs += [(kwb, vwt, k, "causal", stats_w), (ksb, vst, k, "causal", stats_s)]
            batch(jobs)

    gsel = jnp.where(lax.broadcasted_iota(I32, (LANES, LANES), 0)
                     == lax.broadcasted_iota(I32, (LANES, LANES), 1) + g * (nr * N_BRANCH), 1.0, 0.0)
    gs = gsel.astype(BF16)
    gates = jax.nn.sigmoid(jnp.dot(jnp.concatenate(_hi_lo(gate_ref[0]), axis=1), jnp.concatenate([gs, gs], axis=0),
                                   preferred_element_type=F32)).T
    (_, l_s, acc_s), (_, l_w, acc_w) = stats_s, stats_w
    for pair in range(nr // 2):
        o_t = []
        for r in (2 * pair, 2 * pair + 1):
            c0 = r * N_BRANCH
            cols = slice(r * tq_n, (r + 1) * tq_n)
            o_t.append(gates[c0:c0 + 1, :] * o_c[r] + (gates[c0 + 1:c0 + 2, :] / l_s[:, cols]) * acc_s[:, cols]
                       + (gates[c0 + 2:c0 + 3, :] / l_w[:, cols]) * acc_w[:, cols])
        o_ref[0, :, pair * LANES:(pair + 1) * LANES] = jnp.concatenate(o_t, axis=0).T


def _nsa(slopes, proj_nsa, kvc, proj_gla):
    bsz, _, s, dh = proj_nsa.shape
    g_n, nr = NSA_KV_HEADS, NSA_Q_PER_KV
    tq = NSA_TQ
    tk = NSA_TK
    assert tq == tk and WINDOW % tk == 0 and 2 * dh == LANES
    ncp = kvc.shape[2]
    kv0 = NSA_HEADS
    nq = nr * tq
    nbuf = WINDOW // tk + 2

    def kv_spec(i):
        return pl.BlockSpec((1, 1, s, dh), lambda b, g, q, i=i: (b, kv0 + i * g_n + g, 0, 0))

    return pl.pallas_call(
        _nsa_kernel,
        grid=(bsz, g_n, s // tq),
        in_specs=[pl.BlockSpec(memory_space=pltpu.SMEM),
                  pl.BlockSpec((1, nr, tq, dh), lambda b, g, q: (b, g, q, 0)),
                  pl.BlockSpec((1, 1, ncp, dh), lambda b, g, q: (b, g, 0, 0)),
                  pl.BlockSpec((1, 1, ncp, dh), lambda b, g, q: (b, g_n + g, 0, 0)),
                  kv_spec(2), kv_spec(3), kv_spec(4), kv_spec(5),
                  pl.BlockSpec((1, tq, LANES), lambda b, g, q: (b, q, GLA_MISC_OFF // LANES))],
        out_specs=pl.BlockSpec((1, tq, nr * dh), lambda b, g, q: (b, q, g)),
        out_shape=jax.ShapeDtypeStruct((bsz, s, NSA_HEADS * dh), F32),
        scratch_shapes=[pltpu.VMEM((s, LANES), BF16), pltpu.VMEM((s // tk, dh, tk), BF16)] * 2
        + [pltpu.VMEM((LANES, nq), BF16), pltpu.VMEM((s // SEL_BLOCK, tq), F32)]
        + [pltpu.VMEM((1, nq), F32), pltpu.VMEM((1, nq), F32), pltpu.VMEM((dh, nq), F32)] * 2
        + [pltpu.VMEM((tk, nq), F32)] * nbuf + [pltpu.VMEM((tk, nq), BF16)] * nbuf + [pltpu.VMEM((1, nq), F32)] * nbuf,
        compiler_params=_cparams(("parallel", "parallel", "arbitrary")),
        name="nsa_attention",
    )(slopes, proj_nsa, kvc, kvc, proj_nsa, proj_nsa, proj_nsa, proj_nsa, proj_gla)


def _gla_kernel(q_ref, k_ref, v_ref, og_ref, lr_ref, wg_ref, bg_ref, nw_ref, o_ref, st_scr, la_scr, b_scr):
    rows_n = q_ref.shape[1]
    c_n, sub = GLA_CHUNK, GLA_SUB
    nh, dk, dv = GLA_HEADS, GLA_DK, GLA_DV

    @pl.when(pl.program_id(1) == 0)
    def _():
        st_scr[...] = jnp.zeros(st_scr.shape, F32)

    z = jnp.dot(lr_ref[0], wg_ref[...], precision=HI, preferred_element_type=F32) + bg_ref[...]
    la_scr[...] = (jnp.minimum(z, 0.0) - jnp.log1p(jnp.exp(-jnp.abs(z)))) * (1.0 / GLA_GATE_NORM)
    tril = jnp.where(lax.broadcasted_iota(I32, (c_n, c_n), 0) >= lax.broadcasted_iota(I32, (c_n, c_n), 1), 1.0, 0.0)
    row_c = lax.broadcasted_iota(I32, (c_n, 1), 0)
    row_s = lax.broadcasted_iota(I32, (sub, 1), 0)
    lane_c = lax.broadcasted_iota(I32, (1, c_n), 1)
    nw = nw_ref[...]
    hk = [slice(h * dk, (h + 1) * dk) for h in range(nh)]
    hv = [slice(h * dv, (h + 1) * dv) for h in range(nh)]

    def chunk(c, carry):
        r0 = pl.multiple_of(c * c_n, c_n)
        rows = pl.ds(r0, c_n)
        qc = q_ref[0, rows, :] * (dk ** -0.5)
        kc = k_ref[0, rows, :]
        vc = [v_ref[0, rows, hv[h]].astype(BF16) for h in range(nh)]
        b = jnp.dot(tril, la_scr[rows, :], precision=HI, preferred_element_type=F32)
        b_scr[...] = b
        st = [st_scr[h] for h in range(nh)]
        q_e = (qc * jnp.exp(b)).astype(BF16)
        o = [_nt_dot(q_e[:, hk[h]], st[h].astype(BF16)) for h in range(nh)]
        strips = [[] for _ in range(nh)]
        for blk in range(c_n // sub):
            lo = blk * sub
            q_i = qc[lo:lo + sub]
            b_i = b[lo:lo + sub]
            a = [jnp.zeros((sub, c_n), F32) for _ in range(nh)]
            if blk > 0:
                b_r = b_scr[lo - 1:lo, :]
                q_d = (q_i * jnp.exp(b_i - b_r)).astype(BF16)
                k_d = (kc * jnp.exp(jnp.where(row_c < lo, b_r - b, -jnp.inf))).astype(BF16)
                a = [_nt_dot(q_d[:, hk[h]], k_d[:, hk[h]]) for h in range(nh)]
            for j in range(sub):
                b_j = b_scr[lo + j:lo + j + 1, :]
                k_j = k_ref[0, pl.ds(r0 + lo + j, 1), :]
                prod = q_i * k_j * jnp.exp(jnp.where(row_s >= j, b_i - b_j, -jnp.inf))
                for h in range(nh):
                    col = jnp.sum(prod[:, hk[h]], axis=-1, keepdims=True)
                    a[h] = jnp.where(lane_c == lo + j, col, a[h]) if blk == 0 else (
                        a[h] + jnp.where(lane_c == lo + j, col, 0.0))
            for h in range(nh):
                strips[h].append(a[h])
        for h in range(nh):
            attn = jnp.concatenate(strips[h], axis=0)
            o[h] = o[h] + jnp.dot(attn.astype(BF16), vc[h], preferred_element_type=F32)
        b_last = b_scr[c_n - 1:c_n, :]
        k_dec = (kc * jnp.exp(b_last - b)).astype(BF16)
        decay = jnp.exp(b_last)
        for h in range(nh):
            st_scr[h] = st[h] * decay[:, hk[h]] + lax.dot_general(
                vc[h], k_dec[:, hk[h]], (((0,), (0,)), ((), ())), preferred_element_type=F32)
        for h in range(nh):
            og = og_ref[0, rows, hv[h]]
            on = o[h] * lax.rsqrt(jnp.mean(o[h] * o[h], axis=-1, keepdims=True) + EPS) * nw
            o_ref[0, rows, hv[h]] = on * (og * jax.nn.sigmoid(og))
        return carry

    lax.fori_loop(0, rows_n // c_n, chunk, 0)


def _gla(proj_gla, wg_pad, bg, nw):
    bsz, s, _ = proj_gla.shape
    nh, dk, dv = GLA_HEADS, GLA_DK, GLA_DV
    ts = GLA_TS
    wk, wv = nh * dk, nh * dv
    return pl.pallas_call(
        _gla_kernel,
        grid=(bsz, s // ts),
        in_specs=[pl.BlockSpec((1, ts, wk), lambda b, i: (b, i, GLA_Q_OFF // wk)),
                  pl.BlockSpec((1, ts, wk), lambda b, i: (b, i, GLA_K_OFF // wk)),
                  pl.BlockSpec((1, ts, wv), lambda b, i: (b, i, GLA_V_OFF // wv)),
                  pl.BlockSpec((1, ts, wv), lambda b, i: (b, i, GLA_OG_OFF // wv)),
                  pl.BlockSpec((1, ts, LANES), lambda b, i: (b, i, GLA_MISC_OFF // LANES)),
                  pl.BlockSpec((LANES, wk), lambda b, i: (0, 0)),
                  pl.BlockSpec((1, wk), lambda b, i: (0, 0)),
                  pl.BlockSpec((1, dv), lambda b, i: (0, 0))],
        out_specs=pl.BlockSpec((1, ts, wv), lambda b, i: (b, i, 0)),
        out_shape=jax.ShapeDtypeStruct((bsz, s, wv), F32),
        scratch_shapes=[pltpu.VMEM((nh, dv, dk), F32), pltpu.VMEM((ts, wk), F32), pltpu.VMEM((GLA_CHUNK, wk), F32)],
        compiler_params=_cparams(("parallel", "arbitrary")),
        name="gla",
    )(proj_gla, proj_gla, proj_gla, proj_gla, proj_gla, wg_pad, bg, nw)


def _slab_store(ref, val):
    nslab = val.shape[1] // LANES
    for s in range(nslab):
        ref[pl.ds(s, val.shape[0], stride=nslab), :] = val[:, s * LANES:(s + 1) * LANES]


def _slab_load(ref, nrow, width):
    nslab = width // LANES
    return jnp.concatenate([ref[pl.ds(s, nrow, stride=nslab), :] for s in range(nslab)], axis=1)


def _outproj_kernel(nsa_ref, gla_ref, x_ref, wo_ref, g1_ref, sc_ref, sh_ref, nw_ref, wr_ref, br_ref,
                    x1_ref, h_ref, route_ref):
    half = nsa_ref.shape[2]
    acc = jnp.dot(nsa_ref[0].astype(BF16), wo_ref[:half, :], preferred_element_type=F32)
    acc = acc + jnp.dot(gla_ref[0].astype(BF16), wo_ref[half:, :], preferred_element_type=F32)
    x1 = x_ref[0] + g1_ref[0, 0] * acc
    x1_ref[0] = x1
    h = _modulated_norm(x1, nw_ref[...], sc_ref[0, 0], sh_ref[0, 0])
    _slab_store(h_ref.at[0], h)
    h_hi = h.astype(BF16)
    h_lo = (h - h_hi.astype(F32)).astype(BF16)
    t = jnp.dot(h_hi, wr_ref[...], preferred_element_type=F32)
    logits = (t[:, :LANES] + t[:, LANES:] + jnp.dot(h_lo, wr_ref[:, :LANES], preferred_element_type=F32)
              + br_ref[...])
    lane = lax.broadcasted_iota(I32, (1, LANES), 1)
    ninf = -jnp.inf
    is_g = (lane >= N_EXPERTS) & (lane < N_EXPERTS + N_GROUPS)
    gl = jnp.where(is_g, logits, ninf)
    ge = jnp.exp(gl - jnp.max(gl, axis=-1, keepdims=True))
    gp = ge / jnp.sum(ge, axis=-1, keepdims=True)
    gp_max = jnp.max(gp, axis=-1, keepdims=True)
    grp = jnp.min(jnp.where((gp == gp_max) & is_g, lane - N_EXPERTS, LANES), axis=-1, keepdims=True)
    in_grp = (lane // EXPERTS_PER_GROUP == grp) & (lane < N_EXPERTS)
    el = jnp.where(in_grp, logits, ninf)
    v1 = jnp.max(el, axis=-1, keepdims=True)
    i1 = jnp.min(jnp.where(el == v1, lane, LANES), axis=-1, keepdims=True)
    el2 = jnp.where(lane == i1, ninf, el)
    v2 = jnp.max(el2, axis=-1, keepdims=True)
    i2 = jnp.min(jnp.where(el2 == v2, lane, LANES), axis=-1, keepdims=True)
    e2 = jnp.exp(v2 - v1)
    den = 1.0 + e2
    w1 = gp_max * (1.0 / den)
    w2 = gp_max * (e2 / den)
    route_ref[0] = jnp.where(lane == 0, i1.astype(F32), jnp.where(lane == 1, i2.astype(F32), jnp.where(
        lane == 2, w1, jnp.where(lane == 3, w2, 0.0))))


def _outproj(o_nsa, o_gla, x, wo, mod4, nw, wr, br):
    bsz, s, d = x.shape
    tm = OUT_TM
    half = o_nsa.shape[2]

    def mod_spec(idx):
        return pl.BlockSpec((1, 1, 1, d), lambda b, i: (b, idx, 0, 0))

    row = lambda w: pl.BlockSpec((1, tm, w), lambda b, i: (b, i, 0))
    return pl.pallas_call(
        _outproj_kernel,
        grid=(bsz, s // tm),
        in_specs=[row(half), row(half), row(d),
                  pl.BlockSpec((2 * half, d), lambda b, i: (0, 0)),
                  mod_spec(2), mod_spec(4), mod_spec(3),
                  pl.BlockSpec((1, d), lambda b, i: (0, 0)),
                  pl.BlockSpec((d, 2 * LANES), lambda b, i: (0, 0)),
                  pl.BlockSpec((1, LANES), lambda b, i: (0, 0))],
        out_specs=[row(d), pl.BlockSpec((1, tm * (d // LANES), LANES), lambda b, i: (b, i, 0)), row(LANES)],
        out_shape=[jax.ShapeDtypeStruct((bsz, s, d), F32), jax.ShapeDtypeStruct((bsz, s * (d // LANES), LANES), F32),
                   jax.ShapeDtypeStruct((bsz, s, LANES), F32)],
        compiler_params=_cparams(("parallel", "parallel")),
        name="outproj_router",
    )(o_nsa, o_gla, x, wo, mod4, mod4, mod4, nw, wr, br)


def _rank_kernel(route_ref, dest_ref, meta_ref, rank_scr):
    n = route_ref.shape[0]
    tm = RANK_TM
    lane_i = lax.broadcasted_iota(I32, (1, LANES), 1)
    lane = lane_i.astype(F32)
    strict = jnp.where(lax.broadcasted_iota(I32, (tm, tm), 0) > lax.broadcasted_iota(I32, (tm, tm), 1),
                       1.0, 0.0).astype(BF16)

    def two_lanes(a, b):
        return jnp.where(lane_i == 0, a, jnp.where(lane_i == 1, b, 0.0))

    def pick(e, table):
        return jnp.sum(jnp.where(lane == e, table, 0.0), axis=-1, keepdims=True)

    def count(i, seen):
        r0 = pl.multiple_of(i * tm, tm)
        rt = route_ref[pl.ds(r0, tm), :]
        e1, e2 = rt[:, 0:1], rt[:, 1:2]
        member = jnp.where(lane == e1, 1.0, jnp.where(lane == e2, 1.0, 0.0))
        before = jnp.dot(strict, member.astype(BF16), preferred_element_type=F32) + seen
        rank_scr[pl.ds(r0, tm), :] = two_lanes(pick(e1, before), pick(e2, before))
        return seen + jnp.sum(member, axis=0, keepdims=True)

    counts = lax.fori_loop(0, n // tm, count, jnp.zeros((1, LANES), F32))
    ntile = jnp.floor((counts + (MOE_TB - 1)) * (1.0 / MOE_TB))
    incl = jnp.where(lax.broadcasted_iota(I32, (LANES, LANES), 0) <= lax.broadcasted_iota(I32, (LANES, LANES), 1),
                     1.0, 0.0).astype(BF16)
    tile_end = jnp.dot(jnp.broadcast_to(ntile, (8, LANES)).astype(BF16), incl,
                       preferred_element_type=F32)[0:1]
    row_start = (tile_end - ntile) * MOE_TB

    def place(i, carry):
        r0 = pl.multiple_of(i * tm, tm)
        rt = route_ref[pl.ds(r0, tm), :]
        rk = rank_scr[pl.ds(r0, tm), :]
        d1 = pick(rt[:, 0:1], row_start) + rk[:, 0:1]
        d2 = pick(rt[:, 1:2], row_start) + rk[:, 1:2]
        dest_ref[pl.ds(r0, tm), :] = two_lanes(d1, d2).astype(I32)
        return carry

    lax.fori_loop(0, n // tm, place, 0)
    trow = lax.broadcasted_iota(I32, (meta_ref.shape[0], 1), 0).astype(F32)
    texp = jnp.sum(jnp.where((tile_end <= trow) & (lane_i < N_EXPERTS), 1.0, 0.0), axis=-1, keepdims=True)
    texp = jnp.minimum(texp, N_EXPERTS - 1.0)
    used = pick(N_EXPERTS - 1.0, tile_end)
    diag = lax.broadcasted_iota(I32, (meta_ref.shape[0], LANES), 0) == lane_i
    end_rows = jnp.sum(jnp.where(diag, tile_end, 0.0), axis=-1, keepdims=True)
    ntile_rows = jnp.sum(jnp.where(diag, ntile, 0.0), axis=-1, keepdims=True)
    meta_ref[...] = jnp.where(lane_i == 2, end_rows, jnp.where(lane_i == 3, ntile_rows, two_lanes(
        texp, jnp.broadcast_to(used, texp.shape)))).astype(I32)


def _rank(route):
    n = route.shape[0]
    return pl.pallas_call(
        _rank_kernel,
        out_shape=[jax.ShapeDtypeStruct((n, LANES), I32), jax.ShapeDtypeStruct((LANES, LANES), I32)],
        scratch_shapes=[pltpu.VMEM((n, LANES), F32)],
        compiler_params=pltpu.CompilerParams(vmem_limit_bytes=VMEM_LIMIT),
        name="moe_rank",
    )(route)


def _dispatch_kernel(dest_ref, ends_ref, h_ref, xs_ref, zero_scr, sem, zsem):
    i = pl.program_id(0)
    tm = h_ref.shape[0] // SLAB
    tb = zero_scr.shape[0]

    @pl.when(i == 0)
    def _():
        zero_scr[...] = jnp.zeros(zero_scr.shape, F32)

        def zero_copy(e):
            r0 = pl.multiple_of((ends_ref[e] - 1) * tb, tb)
            return pltpu.make_async_copy(zero_scr, xs_ref.at[pl.ds(r0, tb)], zsem)

        def start(e, carry):
            @pl.when(ends_ref[N_EXPERTS + e] > 0)
            def _():
                zero_copy(e).start()
            return carry

        def wait(e, carry):
            @pl.when(ends_ref[N_EXPERTS + e] > 0)
            def _():
                zero_copy(e).wait()
            return carry

        def tail_copy(t):
            return pltpu.make_async_copy(zero_scr, xs_ref.at[pl.ds(pl.multiple_of(t * tb, tb), tb)], zsem)

        def tail_start(t, carry):
            tail_copy(t).start()
            return carry

        def tail_wait(t, carry):
            tail_copy(t).wait()
            return carry

        used = ends_ref[2 * N_EXPERTS]
        lax.fori_loop(0, N_EXPERTS, start, 0)
        lax.fori_loop(used, xs_ref.shape[0] // tb, tail_start, 0)
        lax.fori_loop(0, N_EXPERTS, wait, 0)
        lax.fori_loop(used, xs_ref.shape[0] // tb, tail_wait, 0)

    def issue(r, carry):
        p = (i * tm + r) * TOP_K
        src = h_ref.at[pl.ds(pl.multiple_of(r * SLAB, SLAB), SLAB)]
        for k in range(TOP_K):
            dst = xs_ref.at[pl.ds(pl.multiple_of(dest_ref[p + k] * SLAB, SLAB), SLAB)]
            pltpu.make_async_copy(src, dst, sem).start()
        return carry

    lax.fori_loop(0, tm, issue, 0, unroll=DMA_UNROLL)
    for k in range(TOP_K):
        pltpu.make_async_copy(h_ref, xs_ref.at[pl.ds(0, tm * SLAB)], sem).wait()


def _dispatch(dest_flat, ends_flat, h, cap):
    n = h.shape[0] // SLAB
    tm = DISPATCH_TM
    return pl.pallas_call(
        _dispatch_kernel,
        grid_spec=pltpu.PrefetchScalarGridSpec(
            num_scalar_prefetch=2, grid=(n // tm,),
            in_specs=[pl.BlockSpec((tm * SLAB, LANES), lambda i, dst, ends: (i, 0))],
            out_specs=pl.BlockSpec(memory_space=pl.ANY),
            scratch_shapes=[pltpu.VMEM((MOE_TB * SLAB, LANES), F32), pltpu.SemaphoreType.DMA(()),
                            pltpu.SemaphoreType.DMA(())]),
        out_shape=jax.ShapeDtypeStruct((cap * SLAB, LANES), F32),
        compiler_params=_cparams(("arbitrary",)),
        name="moe_dispatch",
    )(dest_flat, ends_flat, h)


def _ffn_kernel(meta_ref, x_ref, wg_hbm, wu_hbm, wd_hbm, y_ref, wg_st, wu_st, wd_st, wgb, wub, wdb, wsem):
    t = pl.program_id(0)
    ntile = pl.num_programs(0)
    n_used = meta_ref[ntile]
    e = meta_ref[t]
    e_prev = meta_ref[jnp.maximum(t - 1, 0)]
    active = t < n_used

    def fetch(expert):
        return [pltpu.make_async_copy(src.at[expert], dst, wsem)
                for src, dst in ((wg_hbm, wg_st), (wu_hbm, wu_st), (wd_hbm, wd_st))]

    @pl.when(t == 0)
    def _():
        for cp in fetch(e):
            cp.start()

    @pl.when(active & ((t == 0) | (e != e_prev)))
    def _():
        for cp in fetch(e):
            cp.wait()
        wgb[...] = wg_st[...].astype(BF16)
        wub[...] = wu_st[...].astype(BF16)
        wdb[...] = wd_st[...].astype(BF16)
        end = meta_ref[ntile + 1 + e]

        @pl.when(end < n_used)
        def _():
            for cp in fetch(meta_ref[end]):
                cp.start()

    @pl.when(active)
    def _():
        x = _slab_load(x_ref, x_ref.shape[0] // SLAB, wgb.shape[0]).astype(BF16)
        gate = jnp.dot(x, wgb[...], preferred_element_type=F32)
        up = jnp.dot(x, wub[...], preferred_element_type=F32)
        act = (gate * jax.nn.sigmoid(gate)) * up
        _slab_store(y_ref, jnp.dot(act.astype(BF16), wdb[...], preferred_element_type=F32))

    @pl.when(jnp.logical_not(active))
    def _():
        y_ref[...] = jnp.zeros(y_ref.shape, F32)


def _ffn(meta_flat, xs, wg, wu, wd):
    _, d, ff = wg.shape
    tb = MOE_TB
    cap = xs.shape[0] // SLAB
    ntile = cap // tb
    return pl.pallas_call(
        _ffn_kernel,
        grid_spec=pltpu.PrefetchScalarGridSpec(
            num_scalar_prefetch=1, grid=(ntile,),
            in_specs=[pl.BlockSpec((tb * SLAB, LANES), lambda t, m: (jnp.minimum(t, m[ntile] - 1), 0)),
                      pl.BlockSpec(memory_space=pl.ANY), pl.BlockSpec(memory_space=pl.ANY),
                      pl.BlockSpec(memory_space=pl.ANY)],
            out_specs=pl.BlockSpec((tb * SLAB, LANES), lambda t, m: (t, 0)),
            scratch_shapes=[pltpu.VMEM((d, ff), F32), pltpu.VMEM((d, ff), F32), pltpu.VMEM((ff, d), F32),
                            pltpu.VMEM((d, ff), BF16), pltpu.VMEM((d, ff), BF16), pltpu.VMEM((ff, d), BF16),
                            pltpu.SemaphoreType.DMA(())]),
        out_shape=jax.ShapeDtypeStruct((cap * SLAB, LANES), F32),
        compiler_params=_cparams(("arbitrary",)),
        name="moe_ffn",
    )(meta_flat, xs, wg, wu, wd)


def _combine_kernel(dest_ref, y_ref, x1_ref, route_ref, g2_ref, nf_ref, o_ref, ybuf, sem):
    i = pl.program_id(0)
    nstep = pl.num_programs(0)
    tm = x1_ref.shape[0]

    def issue(tile, slot):
        def body(r, carry):
            p = (tile * tm + r) * TOP_K
            for k in range(TOP_K):
                src = y_ref.at[pl.ds(pl.multiple_of(dest_ref[p + k] * SLAB, SLAB), SLAB)]
                dst = ybuf.at[slot, k, pl.ds(pl.multiple_of(r * SLAB, SLAB), SLAB)]
                pltpu.make_async_copy(src, dst, sem.at[slot]).start()
            return carry

        lax.fori_loop(0, tm, body, 0, unroll=DMA_UNROLL)

    @pl.when(i == 0)
    def _():
        issue(0, 0)

    @pl.when(i + 1 < nstep)
    def _():
        issue(i + 1, (i + 1) % 2)

    slot = i % 2
    for k in range(TOP_K):
        pltpu.make_async_copy(y_ref.at[pl.ds(0, tm * SLAB)], ybuf.at[slot, k], sem.at[slot]).wait()
    rt = route_ref[...]
    d = x1_ref.shape[1]
    moe = rt[:, 2:3] * _slab_load(ybuf.at[slot, 0], tm, d) + rt[:, 3:4] * _slab_load(ybuf.at[slot, 1], tm, d)
    xo = x1_ref[...] + g2_ref[0, 0] * moe
    o_ref[...] = xo * lax.rsqrt(jnp.mean(xo * xo, axis=-1, keepdims=True) + EPS) * nf_ref[...]


def _combine(dest_flat, y, x1, route, mod4, nf, seq):
    n, d = x1.shape
    tm = COMB_TM
    tiles_per_seq = seq // tm
    return pl.pallas_call(
        _combine_kernel,
        grid_spec=pltpu.PrefetchScalarGridSpec(
            num_scalar_prefetch=1, grid=(n // tm,),
            in_specs=[pl.BlockSpec(memory_space=pl.ANY),
                      pl.BlockSpec((tm, d), lambda i, dst: (i, 0)),
                      pl.BlockSpec((tm, LANES), lambda i, dst: (i, 0)),
                      pl.BlockSpec((1, 1, 1, d), lambda i, dst: (i // tiles_per_seq, 5, 0, 0)),
                      pl.BlockSpec((1, d), lambda i, dst: (0, 0))],
            out_specs=pl.BlockSpec((tm, d), lambda i, dst: (i, 0)),
            scratch_shapes=[pltpu.VMEM((2, TOP_K, tm * SLAB, LANES), F32), pltpu.SemaphoreType.DMA((2,))]),
        out_shape=jax.ShapeDtypeStruct((n, d), F32),
        compiler_params=_cparams(("arbitrary",)),
        name="moe_combine",
    )(dest_flat, y, x1, route, mod4, nf)


def _alibi_slopes():
    n = NSA_HEADS
    full = jnp.asarray(2.0 ** (-8.0 * np.arange(1, n + 1) / n), dtype=F32)
    pieces, rest = [], full * LOG2E
    for _ in range(3):
        piece = rest.astype(BF16).astype(F32)
        pieces.append(piece)
        rest = rest - piece
    return jnp.concatenate([full] + pieces)


def _layer(x, c, w_ada, b_ada, norm1_w, w_in, cmp_pos_k, cmp_w1_k, cmp_w2_k, cmp_pos_v, cmp_w1_v, cmp_w2_v,
           gla_w_gate2, gla_b_gate, gla_norm_w, w_out, norm2_w, w_rg, b_rg, w_re, b_re, w_eg, w_eu, w_ed):
    bsz, s, d = x.shape
    dh = NSA_HEAD_DIM
    mod4 = _adaln(c, w_ada, b_ada).reshape(bsz, 6, 1, d)

    o_gate = NSA_COLS
    o_gla = o_gate + NSA_GATE_COLS
    o_lr = o_gla + 2 * GLA_HEADS * GLA_DK + 2 * GLA_HEADS * GLA_DV
    w_nsa, w_gla = _prep_w_in(w_in.T, o_gate, o_gla, o_lr)
    nw1 = norm1_w.reshape(1, d)
    proj_nsa, proj_gla = _inproj(x, mod4, nw1, w_nsa, w_gla, sc_idx=1, sh_idx=0)

    pos = jnp.stack([cmp_pos_k, cmp_pos_v]).reshape(2, 2, CMP_STRIDE * dh)
    kvc = _compress(proj_nsa, pos, jnp.stack([cmp_w1_k, cmp_w1_v]), jnp.stack([cmp_w2_k, cmp_w2_v]))
    o_nsa = _nsa(_alibi_slopes(), proj_nsa, kvc, proj_gla)

    wg_pad = jnp.zeros((LANES, GLA_HEADS * GLA_DK), F32).at[
        NSA_GATE_COLS:NSA_GATE_COLS + GLA_GATE_RANK].set(gla_w_gate2)
    o_gla_out = _gla(proj_gla, wg_pad, gla_b_gate.reshape(1, -1), gla_norm_w.reshape(1, -1))

    wr = jnp.concatenate([w_re, w_rg, jnp.zeros((d, LANES - N_EXPERTS - N_GROUPS), F32)], axis=1)
    br = jnp.concatenate([b_re, b_rg, jnp.zeros((LANES - N_EXPERTS - N_GROUPS,), F32)]).reshape(1, LANES)
    wr_hi = wr.astype(BF16)
    wr_cat = jnp.concatenate([wr_hi, (wr - wr_hi.astype(F32)).astype(BF16)], axis=1)
    x1, h2, route = _outproj(o_nsa, o_gla_out, x, w_out.astype(BF16), mod4, norm2_w.reshape(1, d), wr_cat, br)

    n = bsz * s
    npair = n * TOP_K
    cap = npair + N_EXPERTS * MOE_TB
    ntile = cap // MOE_TB
    route2 = route.reshape(n, LANES)
    dest, meta = _rank(route2)
    dest_flat = dest[:, :TOP_K].reshape(npair)
    meta_flat = jnp.concatenate([meta[:ntile, 0], meta[:1, 1], meta[:N_EXPERTS, 2]])
    ends_flat = jnp.concatenate([meta[:N_EXPERTS, 2], meta[:N_EXPERTS, 3], meta[:1, 1]])
    xs = _dispatch(dest_flat, ends_flat, h2.reshape(n * SLAB, LANES), cap)
    y = _ffn(meta_flat, xs, w_eg, w_eu, w_ed)
    return x1.reshape(n, d), y, dest_flat, route2, mod4


def kernel(x, c, w_ada, b_ada, norm1_w, w_in, cmp_pos_k, cmp_w1_k, cmp_w2_k, cmp_pos_v, cmp_w1_v, cmp_w2_v,
           gla_w_gate2, gla_b_gate, gla_norm_w, w_out, norm2_w, w_router_group, b_router_group, w_router_expert,
           b_router_expert, w_expert_gate, w_expert_up, w_expert_down, norm_f_w):
    bsz, s, d = x.shape
    assert w_ada.shape[0] == 1, "single layer"
    x1, y, dest_flat, route2, mod4 = _layer(
        x, c, w_ada[0], b_ada[0], norm1_w[0], w_in[0], cmp_pos_k[0], cmp_w1_k[0], cmp_w2_k[0], cmp_pos_v[0],
        cmp_w1_v[0], cmp_w2_v[0], gla_w_gate2[0], gla_b_gate[0], gla_norm_w[0], w_out[0], norm2_w[0],
        w_router_group[0], b_router_group[0], w_router_expert[0], b_router_expert[0],
        w_expert_gate[0], w_expert_up[0], w_expert_down[0])
    out = _combine(dest_flat, y, x1, route2, mod4, norm_f_w.reshape(1, d), s)
    return out.reshape(bsz, s, d)
```

```python
import functools

import numpy as np
import jax
import jax.numpy as jnp
from jax import lax
from jax.experimental import pallas as pl
from jax.experimental.pallas import tpu as pltpu

F32 = jnp.float32
BF16 = jnp.bfloat16
I32 = jnp.int32

D_MODEL = 2048
NSA_HEAD_DIM = 64
NSA_HEADS = 16
NSA_KV_HEADS = 4
NSA_Q_PER_KV = 4
CMP_BLOCK = 32
CMP_STRIDE = 16
SEL_BLOCK = 64
N_SEL = 16
WINDOW = 512
N_BRANCH = 3
GLA_HEADS = 4
GLA_DV = 256
GLA_DK = 128
GLA_GATE_RANK = 16
GLA_GATE_NORM = 16.0
GLA_CHUNK = 64
GLA_SUB = 8
N_GROUPS = 4
EXPERTS_PER_GROUP = 8
N_EXPERTS = 32
TOP_K = 2
EXPERT_FF = 512
EPS = 1e-6
NEG = -1e30
FORCE = 1e30
LOG2E = 1.4426950408889634
MASKED = 2.0 ** 100

NSA_Q_COLS = NSA_HEADS * NSA_HEAD_DIM
NSA_KV_COLS = 2 * N_BRANCH * NSA_KV_HEADS * NSA_HEAD_DIM
NSA_GATE_COLS = N_BRANCH * NSA_HEADS
NSA_COLS = NSA_Q_COLS + NSA_KV_COLS
NSA_SLOTS = NSA_COLS // NSA_HEAD_DIM
GLA_Q_OFF = 0
GLA_K_OFF = GLA_HEADS * GLA_DK
GLA_V_OFF = 2 * GLA_HEADS * GLA_DK
GLA_OG_OFF = GLA_V_OFF + GLA_HEADS * GLA_DV
GLA_MISC_OFF = GLA_OG_OFF + GLA_HEADS * GLA_DV
LANES = 128
GLA_COLS = GLA_MISC_OFF + LANES

VMEM_LIMIT = 56 * 1024 * 1024

ADA_TN = 768
PREP_TR = 256
PREP_STEP = 512
INPROJ_TM = 256
INPROJ_TN = 512
NSA_TQ = 256
NSA_TK = 256
GLA_TS = 512
OUT_TM = 256
RANK_TM = 256
MOE_TB = 256
DISPATCH_TM = 256
COMB_TM = 256
DMA_UNROLL = 8


def _cparams(sem):
    return pltpu.CompilerParams(dimension_semantics=sem, vmem_limit_bytes=VMEM_LIMIT)


def _adaln_kernel(ct_ref, w_ref, b_ref, o_ref, s_scr):
    nb = ct_ref.shape[1]
    kdim, tn = w_ref.shape

    @pl.when(pl.program_id(0) == 0)
    def _():
        ct = ct_ref[...]
        s = ct * jax.nn.sigmoid(ct)
        for b in range(nb):
            s_scr[b] = jnp.broadcast_to(s[:, b:b + 1], (kdim, LANES))

    def body(k, accs):
        r = pl.multiple_of(k * 8, 8)
        w8 = w_ref[pl.ds(r, 8), :]
        out = []
        for b, acc in enumerate(accs):
            s8 = s_scr[b, pl.ds(r, 8), :]
            out.append(acc + w8 * jnp.concatenate([s8] * (tn // LANES), axis=1))
        return tuple(out)

    accs = lax.fori_loop(0, kdim // 8, body, tuple(jnp.zeros((8, tn), F32) for _ in range(nb)), unroll=2)
    bias = b_ref[...]
    for b, acc in enumerate(accs):
        o_ref[b:b + 1, :] = jnp.sum(acc, axis=0, keepdims=True) + bias


def _adaln(c, w, b):
    nb, d = c.shape
    n = w.shape[1]
    return pl.pallas_call(
        _adaln_kernel,
        grid=(n // ADA_TN,),
        in_specs=[pl.BlockSpec((d, nb), lambda j: (0, 0)),
                  pl.BlockSpec((d, ADA_TN), lambda j: (0, j)),
                  pl.BlockSpec((1, ADA_TN), lambda j: (0, j))],
        out_specs=pl.BlockSpec((nb, ADA_TN), lambda j: (0, j)),
        out_shape=jax.ShapeDtypeStruct((nb, n), F32),
        scratch_shapes=[pltpu.VMEM((nb, d, LANES), F32)],
        compiler_params=_cparams(("arbitrary",)),
        name="adaln",
    )(c.T, w, b.reshape(1, n))


def _modulated_norm(x, nw, sc, sh):
    ms = jnp.mean(x * x, axis=-1, keepdims=True)
    h = x * lax.rsqrt(ms + EPS) * nw
    return h * (1.0 + sc) + sh


def _prep_w_in_kernel(wt_ref, wn_ref, wg_ref, *, o_gate, o_gla, o_lr):
    tr = wt_ref.shape[1]
    step = PREP_STEP

    def put(dst, c0, rows):
        dst[:, c0:c0 + rows.shape[0]] = rows.T.astype(BF16)

    for c in range(0, o_gate, step):
        put(wn_ref, c, wt_ref[c:c + step, :])
    for c in range(0, o_lr - o_gla, step):
        put(wg_ref, c, wt_ref[o_gla + c:o_gla + c + step, :])
    pad = wg_ref.shape[1] - (wt_ref.shape[0] - o_gate)
    misc = jnp.concatenate([wt_ref[o_gate:o_gla, :], wt_ref[o_lr:, :], jnp.zeros((pad, tr), F32)], axis=0)
    put(wg_ref, o_lr - o_gla, misc)


def _prep_w_in(w_in_t, o_gate, o_gla, o_lr):
    n, d = w_in_t.shape
    tr = PREP_TR
    assert o_gate % PREP_STEP == 0 and (o_lr - o_gla) % PREP_STEP == 0
    return pl.pallas_call(
        functools.partial(_prep_w_in_kernel, o_gate=o_gate, o_gla=o_gla, o_lr=o_lr),
        grid=(d // tr,),
        in_specs=[pl.BlockSpec((n, tr), lambda i: (0, i))],
        out_specs=[pl.BlockSpec((tr, NSA_COLS), lambda i: (i, 0)), pl.BlockSpec((tr, GLA_COLS), lambda i: (i, 0))],
        out_shape=[jax.ShapeDtypeStruct((d, NSA_COLS), BF16), jax.ShapeDtypeStruct((d, GLA_COLS), BF16)],
        compiler_params=_cparams(("parallel",)),
        name="prep_w_in",
    )(w_in_t)


def _inproj_kernel(x_ref, sc_ref, sh_ref, nw_ref, wn_ref, wg_ref, on_ref, og_ref):
    h = _modulated_norm(x_ref[0], nw_ref[...], sc_ref[0, 0], sh_ref[0, 0]).astype(BF16)
    dh = NSA_HEAD_DIM
    tn = INPROJ_TN
    for c in range(wn_ref.shape[1] // tn):
        acc = jnp.dot(h, wn_ref[:, c * tn:(c + 1) * tn], preferred_element_type=F32)
        for u in range(tn // dh):
            on_ref[0, c * (tn // dh) + u] = acc[:, u * dh:(u + 1) * dh]
    og_ref[0] = jnp.dot(h, wg_ref[...], preferred_element_type=F32)


def _inproj(x, mod4, nw, w_nsa, w_gla, *, sc_idx, sh_idx):
    bsz, s, d = x.shape
    n_nsa, n_gla = w_nsa.shape[1], w_gla.shape[1]
    tm = INPROJ_TM
    dh = NSA_HEAD_DIM
    return pl.pallas_call(
        _inproj_kernel,
        grid=(bsz, s // tm),
        in_specs=[pl.BlockSpec((1, tm, d), lambda b, i: (b, i, 0)),
                  pl.BlockSpec((1, 1, 1, d), lambda b, i: (b, sc_idx, 0, 0)),
                  pl.BlockSpec((1, 1, 1, d), lambda b, i: (b, sh_idx, 0, 0)),
                  pl.BlockSpec((1, d), lambda b, i: (0, 0)),
                  pl.BlockSpec((d, n_nsa), lambda b, i: (0, 0), pipeline_mode=pl.Buffered(1)),
                  pl.BlockSpec((d, n_gla), lambda b, i: (0, 0), pipeline_mode=pl.Buffered(1))],
        out_specs=[pl.BlockSpec((1, n_nsa // dh, tm, dh), lambda b, i: (b, 0, i, 0)),
                   pl.BlockSpec((1, tm, n_gla), lambda b, i: (b, i, 0))],
        out_shape=[jax.ShapeDtypeStruct((bsz, n_nsa // dh, s, dh), F32),
                   jax.ShapeDtypeStruct((bsz, s, n_gla), F32)],
        compiler_params=_cparams(("parallel", "parallel")),
        name="inproj",
    )(x, mod4, mod4, nw, w_nsa, w_gla)


def _hi_lo(x):
    hi = x.astype(BF16)
    return hi, (x - hi.astype(F32)).astype(BF16)


def _dot_hi_lo(x, w):
    x_hi, x_lo = _hi_lo(x)
    w_hi, w_lo = _hi_lo(w)
    return jnp.dot(jnp.concatenate([x_hi, x_lo, x_hi], axis=1), jnp.concatenate([w_hi, w_hi, w_lo], axis=0),
                   preferred_element_type=F32)


def _compress_kernel(a_ref, pos_ref, w1_ref, w2_ref, o_ref):
    ng = a_ref.shape[1]
    nch = a_ref.shape[2] // CMP_STRIDE
    a = jnp.concatenate([
        jnp.concatenate([a_ref[0, g, pl.ds(t, nch, stride=CMP_STRIDE), :] for t in range(CMP_STRIDE)], axis=1)
        for g in range(ng)], axis=0)
    pos = pos_ref[0]
    half = a.shape[1]
    y1 = _dot_hi_lo(a + pos[0:1], w1_ref[0, :half, :])
    y2 = _dot_hi_lo(a + pos[1:2], w1_ref[0, half:, :])
    h = y1 + pltpu.roll(y2, a.shape[0] - 1, axis=0)
    out = _dot_hi_lo(jax.nn.gelu(h), w2_ref[0])
    row = lax.broadcasted_iota(I32, (nch, out.shape[1]), 0)
    for g in range(ng):
        o_ref[0, g] = jnp.where(row < nch - 1, out[g * nch:(g + 1) * nch], 0.0)


def _compress(proj_nsa, pos, w1, w2):
    bsz, _, s, dh = proj_nsa.shape
    ng = NSA_KV_HEADS
    nch = s // CMP_STRIDE
    cw = CMP_STRIDE * dh
    return pl.pallas_call(
        _compress_kernel,
        grid=(bsz, 2),
        in_specs=[pl.BlockSpec((1, ng, s, dh), lambda b, t: (b, NSA_HEADS // ng + t, 0, 0)),
                  pl.BlockSpec((1, 2, cw), lambda b, t: (t, 0, 0)),
                  pl.BlockSpec((1, 2 * cw, dh), lambda b, t: (t, 0, 0)),
                  pl.BlockSpec((1, dh, dh), lambda b, t: (t, 0, 0))],
        out_specs=pl.BlockSpec((1, ng, nch, dh), lambda b, t: (b, t, 0, 0)),
        out_shape=jax.ShapeDtypeStruct((bsz, 2 * ng, nch, dh), F32),
        compiler_params=_cparams(("parallel", "parallel")),
        name="nsa_compress",
    )(proj_nsa, pos, w1, w2)


def _nt_dot(a, b, **kw):
    return lax.dot_general(a, b, (((1,), (1,)), ((), ())), preferred_element_type=F32, **kw)


def _nsa_kernel(slopes_ref, q_ref, kc_ref, vc_ref, ks_ref, vs_ref, kw_ref, vw_ref, gate_ref, o_ref,
                ksb, vst, kwb, vwt, q4_scr, notsel_scr, *scr):
    g = pl.program_id(1)
    qi = pl.program_id(2)
    tq_n = q_ref.shape[2]
    dh = NSA_HEAD_DIM
    nr = NSA_Q_PER_KV
    seq = ks_ref.shape[2]
    tk_n = NSA_TK
    nb = seq // SEL_BLOCK

    @pl.when(qi == 0)
    def _():
        row = lax.broadcasted_iota(I32, (seq, dh), 0)
        lane = lax.broadcasted_iota(I32, (seq, dh), 1)
        blk = row // SEL_BLOCK
        pos = jnp.where((lane >= nb) & (lane < nb + 3), (blk * SEL_BLOCK).astype(F32),
                        jnp.where((lane >= nb + 3) & (lane < nb + 6), (row % SEL_BLOCK).astype(F32), 0.0))
        ksb[...] = jnp.concatenate([jnp.where(lane == blk, -MASKED, pos), ks_ref[0, 0]], axis=1).astype(BF16)
        kwb[...] = jnp.concatenate([pos, kw_ref[0, 0]], axis=1).astype(BF16)
        for c in range(seq // tk_n):
            rows = slice(c * tk_n, (c + 1) * tk_n)
            for src, dst in ((vs_ref, vst), (vw_ref, vwt)):
                v = src[0, 0, rows, :]
                dst[c] = jnp.concatenate([v, v], axis=1).T[:dh].astype(BF16)

    t0 = qi * tq_n
    tq = t0 + lax.broadcasted_iota(I32, (1, tq_n), 1)
    slopes = [slopes_ref[g * nr + r] for r in range(nr)]
    scale = dh ** -0.5
    q_t = []
    for pair in range(nr // 2):
        both = jnp.concatenate([q_ref[0, 2 * pair], q_ref[0, 2 * pair + 1]], axis=1).T * scale
        q_t += [both[:dh], both[dh:]]

    ncp = kc_ref.shape[2]
    kc = kc_ref[0, 0]
    vc = vc_ref[0, 0]
    vc_t = jnp.concatenate([vc, vc], axis=1).T[:dh].astype(BF16)
    n_sub = lax.broadcasted_iota(I32, (ncp, 1), 0)
    blk_end = n_sub * CMP_STRIDE + (CMP_BLOCK - 1)
    center = n_sub.astype(F32) * CMP_STRIDE + (CMP_BLOCK - 1) / 2.0
    kc_hi, kc_lo = _hi_lo(kc)
    kc_cat = jnp.concatenate([kc_hi, kc_lo, kc_hi, jnp.zeros_like(kc_hi)], axis=1)
    q_hi, q_lo = _hi_lo(jnp.concatenate(q_t, axis=1))
    q_cat = jnp.concatenate([q_hi, q_hi, q_lo, jnp.zeros_like(q_hi)], axis=0)
    s = jnp.dot(kc_cat, q_cat, preferred_element_type=F32)
    tq_all = jnp.concatenate([tq] * nr, axis=1)
    slope_all = jnp.concatenate([jnp.full((1, tq_n), slopes[r], F32) for r in range(nr)], axis=1)
    valid_all = blk_end <= tq_all
    s = jnp.where(valid_all, s - slope_all * (tq_all.astype(F32) - center), NEG)
    e = jnp.exp(s - jnp.max(s, axis=0, keepdims=True))
    p = jnp.where(valid_all, e / jnp.sum(e, axis=0, keepdims=True), 0.0)
    psum = sum(p[:, r * tq_n:(r + 1) * tq_n] for r in range(nr))
    o_c_all = jnp.dot(vc_t, p.astype(BF16), preferred_element_type=F32)
    o_c = [o_c_all[:, r * tq_n:(r + 1) * tq_n] for r in range(nr)]

    n_sel = min(N_SEL, nb)
    notsel_scr[...] = jnp.zeros(notsel_scr.shape, F32)

    @pl.when((qi + 1) * tq_n > n_sel * SEL_BLOCK)
    def _():
        rowj = lax.broadcasted_iota(I32, (LANES, ncp), 0) * SEL_BLOCK
        coln = lax.broadcasted_iota(I32, (LANES, ncp), 1) * CMP_STRIDE
        overlap = jnp.where((coln < rowj + SEL_BLOCK) & (coln + CMP_BLOCK > rowj)
                            & (coln < (ncp - 1) * CMP_STRIDE) & (rowj < nb * SEL_BLOCK), 1.0, 0.0)
        ov = overlap.astype(BF16)
        imp = jnp.dot(jnp.concatenate([ov, ov], axis=1), jnp.concatenate(_hi_lo(psum), axis=0),
                      preferred_element_type=F32)[:nb]
        j_sub = lax.broadcasted_iota(I32, (nb, 1), 0)
        qblk = tq // SEL_BLOCK
        forced = (j_sub == 0) | (j_sub == qblk) | (j_sub == qblk - 1)
        imp = jnp.where(forced, FORCE, jnp.where(j_sub <= qblk, imp, NEG))
        cnt = jnp.zeros((nb, tq_n), F32)
        for i in range(nb):
            ci = imp[i:i + 1, :]
            tie = jnp.where(j_sub > i, 1.0, 0.0)
            cnt = cnt + jnp.where(ci > imp, 1.0, jnp.where(ci == imp, tie, 0.0))
        notsel_scr[...] = jnp.where(cnt < float(n_sel), 0.0, 1.0)

    notsel = notsel_scr[...]

    sub_h = lax.broadcasted_iota(I32, (dh - nb, 1), 0)
    for r in range(nr):
        scol = jnp.zeros((dh - nb, 1), F32)
        for i in range(3):
            piece = slopes_ref[(i + 1) * NSA_HEADS + g * nr + r]
            scol = jnp.where((sub_h == i) | (sub_h == 3 + i), piece, scol)
        q4_scr[:, r * tq_n:(r + 1) * tq_n] = jnp.concatenate(
            [notsel, jnp.broadcast_to(scol, (dh - nb, tq_n)), q_t[r] * LOG2E], axis=0).astype(BF16)

    nwt = WINDOW // tk_n
    nbuf = nwt + 2
    stats_s, stats_w = scr[0:3], scr[3:6]
    s_buf, p_buf, a_buf = scr[6:6 + nbuf], scr[6 + nbuf:6 + 2 * nbuf], scr[6 + 2 * nbuf:6 + 3 * nbuf]
    for m_ref, l_ref, acc_ref in (stats_s, stats_w):
        m_ref[...] = jnp.full(m_ref.shape, NEG, F32)
        l_ref[...] = jnp.zeros(l_ref.shape, F32)
        acc_ref[...] = jnp.zeros(acc_ref.shape, F32)
    key_i = lax.broadcasted_iota(I32, (tk_n, LANES), 0)
    qry_j = lax.broadcasted_iota(I32, (tk_n, LANES), 1)

    def scores(k_ref, kt, buf):
        k_tile = k_ref[kt * tk_n:(kt + 1) * tk_n, :]
        s_buf[buf][...] = jnp.dot(k_tile, q4_scr[...], preferred_element_type=F32)

    def softmax(buf, mode, stats):
        m_ref, l_ref, _ = stats
        for cb in range(nr * tq_n // LANES):
            cols = slice(cb * LANES, (cb + 1) * LANES)
            s = s_buf[buf][:, cols]
            if mode is not None:
                j = qry_j + (cb * LANES) % tq_n
                s = jnp.where(key_i <= j if mode == "causal" else key_i > j, s, -MASKED)
            m_prev = m_ref[:, cols]
            m_new = jnp.maximum(m_prev, jnp.max(s, axis=0, keepdims=True))
            alpha = jnp.exp2(m_prev - m_new)
            p = jnp.exp2(s - m_new)
            l_ref[:, cols] = alpha * l_ref[:, cols] + jnp.sum(p, axis=0, keepdims=True)
            m_ref[:, cols] = m_new
            a_buf[buf][:, cols] = alpha
            p_buf[buf][:, cols] = p.astype(BF16)

    def values(vt_ref, kt, buf, stats):
        acc_ref = stats[2]
        pv = jnp.dot(vt_ref[kt], p_buf[buf][...], preferred_element_type=F32)
        acc_ref[...] = acc_ref[...] * a_buf[buf][...] + pv

    def batch(jobs):
        for i in range(min(nbuf, len(jobs))):
            scores(jobs[i][0], jobs[i][2], i)
        for i, (_, vt_ref, kt, mode, stats) in enumerate(jobs):
            softmax(i % nbuf, mode, stats)
            values(vt_ref, kt, i % nbuf, stats)
            if i + nbuf < len(jobs):
                scores(jobs[i + nbuf][0], jobs[i + nbuf][2], i % nbuf)

    for k in range(seq // tq_n):
        @pl.when(qi == k)
        def _(k=k):
            jobs = [(ksb, vst, t, None, stats_s) for t in range(k)]
            jobs += [(kwb, vwt, k - back, "band" if back == nwt else None, stats_w)
                     for back in range(min(nwt, k), 0, -1)]
            jobs += [(kwb, vwt, k, "causal", stats_w), (ksb, vst, k, "causal", stats_s)]
            batch(jobs)

    gsel = jnp.where(lax.broadcasted_iota(I32, (LANES, LANES), 0)
                     == lax.broadcasted_iota(I32, (LANES, LANES), 1) + g * (nr * N_BRANCH), 1.0, 0.0)
    gs = gsel.astype(BF16)
    gates = jax.nn.sigmoid(jnp.dot(jnp.concatenate(_hi_lo(gate_ref[0]), axis=1), jnp.concatenate([gs, gs], axis=0),
                                   preferred_element_type=F32)).T
    (_, l_s, acc_s), (_, l_w, acc_w) = stats_s, stats_w
    for pair in range(nr // 2):
        o_t = []
        for r in (2 * pair, 2 * pair + 1):
            c0 = r * N_BRANCH
            cols = slice(r * tq_n, (r + 1) * tq_n)
            o_t.append(gates[c0:c0 + 1, :] * o_c[r] + (gates[c0 + 1:c0 + 2, :] / l_s[:, cols]) * acc_s[:, cols]
                       + (gates[c0 + 2:c0 + 3, :] / l_w[:, cols]) * acc_w[:, cols])
        o_ref[0, :, pair * LANES:(pair + 1) * LANES] = jnp.concatenate(o_t, axis=0).T


def _nsa(slopes, proj_nsa, kvc, proj_gla):
    bsz, _, s, dh = proj_nsa.shape
    g_n, nr = NSA_KV_HEADS, NSA_Q_PER_KV
    tq = NSA_TQ
    tk = NSA_TK
    assert tq == tk and WINDOW % tk == 0 and 2 * dh == LANES
    ncp = kvc.shape[2]
    kv0 = NSA_HEADS
    nq = nr * tq
    nbuf = WINDOW // tk + 2

    def kv_spec(i):
        return pl.BlockSpec((1, 1, s, dh), lambda b, g, q, i=i: (b, kv0 + i * g_n + g, 0, 0))

    return pl.pallas_call(
        _nsa_kernel,
        grid=(bsz, g_n, s // tq),
        in_specs=[pl.BlockSpec(memory_space=pltpu.SMEM),
                  pl.BlockSpec((1, nr, tq, dh), lambda b, g, q: (b, g, q, 0)),
                  pl.BlockSpec((1, 1, ncp, dh), lambda b, g, q: (b, g, 0, 0)),
                  pl.BlockSpec((1, 1, ncp, dh), lambda b, g, q: (b, g_n + g, 0, 0)),
                  kv_spec(2), kv_spec(3), kv_spec(4), kv_spec(5),
                  pl.BlockSpec((1, tq, LANES), lambda b, g, q: (b, q, GLA_MISC_OFF // LANES))],
        out_specs=pl.BlockSpec((1, tq, nr * dh), lambda b, g, q: (b, q, g)),
        out_shape=jax.ShapeDtypeStruct((bsz, s, NSA_HEADS * dh), F32),
        scratch_shapes=[pltpu.VMEM((s, LANES), BF16), pltpu.VMEM((s // tk, dh, tk), BF16)] * 2
        + [pltpu.VMEM((LANES, nq), BF16), pltpu.VMEM((s // SEL_BLOCK, tq), F32)]
        + [pltpu.VMEM((1, nq), F32), pltpu.VMEM((1, nq), F32), pltpu.VMEM((dh, nq), F32)] * 2
        + [pltpu.VMEM((tk, nq), F32)] * nbuf + [pltpu.VMEM((tk, nq), BF16)] * nbuf + [pltpu.VMEM((1, nq), F32)] * nbuf,
        compiler_params=_cparams(("parallel", "parallel", "arbitrary")),
        name="nsa_attention",
    )(slopes, proj_nsa, kvc, kvc, proj_nsa, proj_nsa, proj_nsa, proj_nsa, proj_gla)


def _gla_kernel(q_ref, k_ref, v_ref, og_ref, lr_ref, wg_ref, bg_ref, nw_ref, o_ref, st_scr, la_scr, b_scr):
    rows_n = q_ref.shape[1]
    c_n, sub = GLA_CHUNK, GLA_SUB
    nh, dk, dv = GLA_HEADS, GLA_DK, GLA_DV

    @pl.when(pl.program_id(1) == 0)
    def _():
        st_scr[...] = jnp.zeros(st_scr.shape, F32)

    z = _dot_hi_lo(lr_ref[0], wg_ref[...]) + bg_ref[...]
    la_scr[...] = (jnp.minimum(z, 0.0) - jnp.log1p(jnp.exp(-jnp.abs(z)))) * (1.0 / GLA_GATE_NORM)
    tril = jnp.where(lax.broadcasted_iota(I32, (c_n, c_n), 0) >= lax.broadcasted_iota(I32, (c_n, c_n), 1), 1.0, 0.0)
    tril3 = jnp.concatenate([tril.astype(BF16)] * 3, axis=1)
    row_c = lax.broadcasted_iota(I32, (c_n, 1), 0)
    row_s = lax.broadcasted_iota(I32, (sub, 1), 0)
    lane_c = lax.broadcasted_iota(I32, (1, c_n), 1)
    nw = nw_ref[...]
    hk = [slice(h * dk, (h + 1) * dk) for h in range(nh)]
    hv = [slice(h * dv, (h + 1) * dv) for h in range(nh)]

    def chunk(c, carry):
        r0 = pl.multiple_of(c * c_n, c_n)
        rows = pl.ds(r0, c_n)
        qc = q_ref[0, rows, :] * (dk ** -0.5)
        kc = k_ref[0, rows, :]
        vc = [v_ref[0, rows, hv[h]].astype(BF16) for h in range(nh)]
        pieces, rest = [], la_scr[rows, :]
        for _ in range(3):
            pieces.append(rest.astype(BF16))
            rest = rest - pieces[-1].astype(F32)
        b = jnp.dot(tril3, jnp.concatenate(pieces, axis=0), preferred_element_type=F32)
        b_scr[...] = b
        st = [st_scr[h] for h in range(nh)]
        q_e = (qc * jnp.exp(b)).astype(BF16)
        o = [_nt_dot(q_e[:, hk[h]], st[h].astype(BF16)) for h in range(nh)]
        strips = [[] for _ in range(nh)]
        for blk in range(c_n // sub):
            lo = blk * sub
            q_i = qc[lo:lo + sub]
            b_i = b[lo:lo + sub]
            a = [jnp.zeros((sub, c_n), F32) for _ in range(nh)]
            if blk > 0:
                b_r = b_scr[lo - 1:lo, :]
                q_d = (q_i * jnp.exp(b_i - b_r)).astype(BF16)
                k_d = (kc * jnp.exp(jnp.where(row_c < lo, b_r - b, -jnp.inf))).astype(BF16)
                a = [_nt_dot(q_d[:, hk[h]], k_d[:, hk[h]]) for h in range(nh)]
            for j in range(sub):
                b_j = b_scr[lo + j:lo + j + 1, :]
                k_j = k_ref[0, pl.ds(r0 + lo + j, 1), :]
                prod = q_i * k_j * jnp.exp(jnp.where(row_s >= j, b_i - b_j, -jnp.inf))
                for h in range(nh):
                    col = jnp.sum(prod[:, hk[h]], axis=-1, keepdims=True)
                    a[h] = jnp.where(lane_c == lo + j, col, a[h]) if blk == 0 else (
                        a[h] + jnp.where(lane_c == lo + j, col, 0.0))
            for h in range(nh):
                strips[h].append(a[h])
        for h in range(nh):
            attn = jnp.concatenate(strips[h], axis=0)
            o[h] = o[h] + jnp.dot(attn.astype(BF16), vc[h], preferred_element_type=F32)
        b_last = b_scr[c_n - 1:c_n, :]
        k_dec = (kc * jnp.exp(b_last - b)).astype(BF16)
        decay = jnp.exp(b_last)
        for h in range(nh):
            st_scr[h] = st[h] * decay[:, hk[h]] + lax.dot_general(
                vc[h], k_dec[:, hk[h]], (((0,), (0,)), ((), ())), preferred_element_type=F32)
        for h in range(nh):
            og = og_ref[0, rows, hv[h]]
            on = o[h] * lax.rsqrt(jnp.mean(o[h] * o[h], axis=-1, keepdims=True) + EPS) * nw
            o_ref[0, rows, hv[h]] = on * (og * jax.nn.sigmoid(og))
        return carry

    lax.fori_loop(0, rows_n // c_n, chunk, 0)


def _gla(proj_gla, wg_pad, bg, nw):
    bsz, s, _ = proj_gla.shape
    nh, dk, dv = GLA_HEADS, GLA_DK, GLA_DV
    ts = GLA_TS
    wk, wv = nh * dk, nh * dv
    return pl.pallas_call(
        _gla_kernel,
        grid=(bsz, s // ts),
        in_specs=[pl.BlockSpec((1, ts, wk), lambda b, i: (b, i, GLA_Q_OFF // wk)),
                  pl.BlockSpec((1, ts, wk), lambda b, i: (b, i, GLA_K_OFF // wk)),
                  pl.BlockSpec((1, ts, wv), lambda b, i: (b, i, GLA_V_OFF // wv)),
                  pl.BlockSpec((1, ts, wv), lambda b, i: (b, i, GLA_OG_OFF // wv)),
                  pl.BlockSpec((1, ts, LANES), lambda b, i: (b, i, GLA_MISC_OFF // LANES)),
                  pl.BlockSpec((LANES, wk), lambda b, i: (0, 0)),
                  pl.BlockSpec((1, wk), lambda b, i: (0, 0)),
                  pl.BlockSpec((1, dv), lambda b, i: (0, 0))],
        out_specs=pl.BlockSpec((1, ts, wv), lambda b, i: (b, i, 0)),
        out_shape=jax.ShapeDtypeStruct((bsz, s, wv), F32),
        scratch_shapes=[pltpu.VMEM((nh, dv, dk), F32), pltpu.VMEM((ts, wk), F32), pltpu.VMEM((GLA_CHUNK, wk), F32)],
        compiler_params=_cparams(("parallel", "arbitrary")),
        name="gla",
    )(proj_gla, proj_gla, proj_gla, proj_gla, proj_gla, wg_pad, bg, nw)


def _outproj_kernel(nsa_ref, gla_ref, x_ref, wo_ref, g1_ref, sc_ref, sh_ref, nw_ref, wr_ref, br_ref,
                    x1_ref, h_ref, route_ref):
    half = nsa_ref.shape[2]
    acc = jnp.dot(nsa_ref[0].astype(BF16), wo_ref[:half, :], preferred_element_type=F32)
    acc = acc + jnp.dot(gla_ref[0].astype(BF16), wo_ref[half:, :], preferred_element_type=F32)
    x1 = x_ref[0] + g1_ref[0, 0] * acc
    x1_ref[0] = x1
    h = _modulated_norm(x1, nw_ref[...], sc_ref[0, 0], sh_ref[0, 0])
    h_ref[0] = h
    h_hi = h.astype(BF16)
    h_lo = (h - h_hi.astype(F32)).astype(BF16)
    t = jnp.dot(h_hi, wr_ref[...], preferred_element_type=F32)
    logits = (t[:, :LANES] + t[:, LANES:] + jnp.dot(h_lo, wr_ref[:, :LANES], preferred_element_type=F32)
              + br_ref[...])
    lane = lax.broadcasted_iota(I32, (1, LANES), 1)
    ninf = -jnp.inf
    is_g = (lane >= N_EXPERTS) & (lane < N_EXPERTS + N_GROUPS)
    gl = jnp.where(is_g, logits, ninf)
    ge = jnp.exp(gl - jnp.max(gl, axis=-1, keepdims=True))
    gp = ge / jnp.sum(ge, axis=-1, keepdims=True)
    gp_max = jnp.max(gp, axis=-1, keepdims=True)
    grp = jnp.min(jnp.where((gp == gp_max) & is_g, lane - N_EXPERTS, LANES), axis=-1, keepdims=True)
    in_grp = (lane // EXPERTS_PER_GROUP == grp) & (lane < N_EXPERTS)
    el = jnp.where(in_grp, logits, ninf)
    v1 = jnp.max(el, axis=-1, keepdims=True)
    i1 = jnp.min(jnp.where(el == v1, lane, LANES), axis=-1, keepdims=True)
    el2 = jnp.where(lane == i1, ninf, el)
    v2 = jnp.max(el2, axis=-1, keepdims=True)
    i2 = jnp.min(jnp.where(el2 == v2, lane, LANES), axis=-1, keepdims=True)
    e2 = jnp.exp(v2 - v1)
    den = 1.0 + e2
    w1 = gp_max * (1.0 / den)
    w2 = gp_max * (e2 / den)
    route_ref[0] = jnp.where(lane == 0, i1.astype(F32), jnp.where(lane == 1, i2.astype(F32), jnp.where(
        lane == 2, w1, jnp.where(lane == 3, w2, 0.0))))


def _outproj(o_nsa, o_gla, x, wo, mod4, nw, wr, br):
    bsz, s, d = x.shape
    tm = OUT_TM
    half = o_nsa.shape[2]

    def mod_spec(idx):
        return pl.BlockSpec((1, 1, 1, d), lambda b, i: (b, idx, 0, 0))

    row = lambda w: pl.BlockSpec((1, tm, w), lambda b, i: (b, i, 0))
    return pl.pallas_call(
        _outproj_kernel,
        grid=(bsz, s // tm),
        in_specs=[row(half), row(half), row(d),
                  pl.BlockSpec((2 * half, d), lambda b, i: (0, 0)),
                  mod_spec(2), mod_spec(4), mod_spec(3),
                  pl.BlockSpec((1, d), lambda b, i: (0, 0)),
                  pl.BlockSpec((d, 2 * LANES), lambda b, i: (0, 0)),
                  pl.BlockSpec((1, LANES), lambda b, i: (0, 0))],
        out_specs=[row(d), row(d), row(LANES)],
        out_shape=[jax.ShapeDtypeStruct((bsz, s, d), F32), jax.ShapeDtypeStruct((bsz, s, d), F32),
                   jax.ShapeDtypeStruct((bsz, s, LANES), F32)],
        compiler_params=_cparams(("parallel", "parallel")),
        name="outproj_router",
    )(o_nsa, o_gla, x, wo, mod4, mod4, mod4, nw, wr, br)


def _rank_kernel(route_ref, dest_ref, meta_ref, rank_scr):
    n = route_ref.shape[0]
    tm = RANK_TM
    lane_i = lax.broadcasted_iota(I32, (1, LANES), 1)
    lane = lane_i.astype(F32)
    strict = jnp.where(lax.broadcasted_iota(I32, (tm, tm), 0) > lax.broadcasted_iota(I32, (tm, tm), 1),
                       1.0, 0.0).astype(BF16)

    def two_lanes(a, b):
        return jnp.where(lane_i == 0, a, jnp.where(lane_i == 1, b, 0.0))

    def pick(e, table):
        return jnp.sum(jnp.where(lane == e, table, 0.0), axis=-1, keepdims=True)

    def count(i, seen):
        r0 = pl.multiple_of(i * tm, tm)
        rt = route_ref[pl.ds(r0, tm), :]
        e1, e2 = rt[:, 0:1], rt[:, 1:2]
        member = jnp.where(lane == e1, 1.0, jnp.where(lane == e2, 1.0, 0.0))
        before = jnp.dot(strict, member.astype(BF16), preferred_element_type=F32) + seen
        rank_scr[pl.ds(r0, tm), :] = two_lanes(pick(e1, before), pick(e2, before))
        return seen + jnp.sum(member, axis=0, keepdims=True)

    counts = lax.fori_loop(0, n // tm, count, jnp.zeros((1, LANES), F32))
    ntile = jnp.floor((counts + (MOE_TB - 1)) * (1.0 / MOE_TB))
    incl = jnp.where(lax.broadcasted_iota(I32, (LANES, LANES), 0) <= lax.broadcasted_iota(I32, (LANES, LANES), 1),
                     1.0, 0.0).astype(BF16)
    tile_end = jnp.dot(jnp.broadcast_to(ntile, (8, LANES)).astype(BF16), incl,
                       preferred_element_type=F32)[0:1]
    row_start = (tile_end - ntile) * MOE_TB

    def place(i, carry):
        r0 = pl.multiple_of(i * tm, tm)
        rt = route_ref[pl.ds(r0, tm), :]
        rk = rank_scr[pl.ds(r0, tm), :]
        d1 = pick(rt[:, 0:1], row_start) + rk[:, 0:1]
        d2 = pick(rt[:, 1:2], row_start) + rk[:, 1:2]
        dest_ref[pl.ds(r0, tm), :] = two_lanes(d1, d2).astype(I32)
        return carry

    lax.fori_loop(0, n // tm, place, 0)
    trow = lax.broadcasted_iota(I32, (meta_ref.shape[0], 1), 0).astype(F32)
    texp = jnp.sum(jnp.where((tile_end <= trow) & (lane_i < N_EXPERTS), 1.0, 0.0), axis=-1, keepdims=True)
    texp = jnp.minimum(texp, N_EXPERTS - 1.0)
    used = pick(N_EXPERTS - 1.0, tile_end)
    diag = lax.broadcasted_iota(I32, (meta_ref.shape[0], LANES), 0) == lane_i
    end_rows = jnp.sum(jnp.where(diag, tile_end, 0.0), axis=-1, keepdims=True)
    ntile_rows = jnp.sum(jnp.where(diag, ntile, 0.0), axis=-1, keepdims=True)
    meta_ref[...] = jnp.where(lane_i == 2, end_rows, jnp.where(lane_i == 3, ntile_rows, two_lanes(
        texp, jnp.broadcast_to(used, texp.shape)))).astype(I32)


def _rank(route):
    n = route.shape[0]
    return pl.pallas_call(
        _rank_kernel,
        out_shape=[jax.ShapeDtypeStruct((n, LANES), I32), jax.ShapeDtypeStruct((LANES, LANES), I32)],
        scratch_shapes=[pltpu.VMEM((n, LANES), F32)],
        compiler_params=pltpu.CompilerParams(vmem_limit_bytes=VMEM_LIMIT),
        name="moe_rank",
    )(route)


def _dispatch_kernel(dest_ref, ends_ref, h_ref, xs_ref, zero_scr, sem, zsem):
    i = pl.program_id(0)
    tm = h_ref.shape[0]
    tb = zero_scr.shape[0]

    @pl.when(i == 0)
    def _():
        zero_scr[...] = jnp.zeros(zero_scr.shape, F32)

        def zero_copy(e):
            r0 = pl.multiple_of((ends_ref[e] - 1) * tb, tb)
            return pltpu.make_async_copy(zero_scr, xs_ref.at[pl.ds(r0, tb)], zsem)

        def start(e, carry):
            @pl.when(ends_ref[N_EXPERTS + e] > 0)
            def _():
                zero_copy(e).start()
            return carry

        def wait(e, carry):
            @pl.when(ends_ref[N_EXPERTS + e] > 0)
            def _():
                zero_copy(e).wait()
            return carry

        def tail_copy(t):
            return pltpu.make_async_copy(zero_scr, xs_ref.at[pl.ds(pl.multiple_of(t * tb, tb), tb)], zsem)

        def tail_start(t, carry):
            tail_copy(t).start()
            return carry

        def tail_wait(t, carry):
            tail_copy(t).wait()
            return carry

        used = ends_ref[2 * N_EXPERTS]
        lax.fori_loop(0, N_EXPERTS, start, 0)
        lax.fori_loop(used, xs_ref.shape[0] // tb, tail_start, 0)
        lax.fori_loop(0, N_EXPERTS, wait, 0)
        lax.fori_loop(used, xs_ref.shape[0] // tb, tail_wait, 0)

    def issue(r, carry):
        p = (i * tm + r) * TOP_K
        for k in range(TOP_K):
            pltpu.make_async_copy(h_ref.at[pl.ds(r, 1)], xs_ref.at[pl.ds(dest_ref[p + k], 1)], sem).start()
        return carry

    lax.fori_loop(0, tm, issue, 0, unroll=DMA_UNROLL)
    for k in range(TOP_K):
        pltpu.make_async_copy(h_ref, xs_ref.at[pl.ds(0, tm)], sem).wait()


def _dispatch(dest_flat, ends_flat, h, cap):
    n, d = h.shape
    tm = DISPATCH_TM
    return pl.pallas_call(
        _dispatch_kernel,
        grid_spec=pltpu.PrefetchScalarGridSpec(
            num_scalar_prefetch=2, grid=(n // tm,),
            in_specs=[pl.BlockSpec((tm, d), lambda i, dst, ends: (i, 0))],
            out_specs=pl.BlockSpec(memory_space=pl.ANY),
            scratch_shapes=[pltpu.VMEM((MOE_TB, d), F32), pltpu.SemaphoreType.DMA(()), pltpu.SemaphoreType.DMA(())]),
        out_shape=jax.ShapeDtypeStruct((cap, d), F32),
        compiler_params=_cparams(("arbitrary",)),
        name="moe_dispatch",
    )(dest_flat, ends_flat, h)


def _ffn_kernel(meta_ref, x_ref, wg_hbm, wu_hbm, wd_hbm, y_ref, wg_st, wu_st, wd_st, wgb, wub, wdb, wsem):
    t = pl.program_id(0)
    ntile = pl.num_programs(0)
    n_used = meta_ref[ntile]
    e = meta_ref[t]
    e_prev = meta_ref[jnp.maximum(t - 1, 0)]
    active = t < n_used

    def fetch(expert):
        return [pltpu.make_async_copy(src.at[expert], dst, wsem)
                for src, dst in ((wg_hbm, wg_st), (wu_hbm, wu_st), (wd_hbm, wd_st))]

    @pl.when(t == 0)
    def _():
        for cp in fetch(e):
            cp.start()

    @pl.when(active & ((t == 0) | (e != e_prev)))
    def _():
        for cp in fetch(e):
            cp.wait()
        wgb[...] = wg_st[...].astype(BF16)
        wub[...] = wu_st[...].astype(BF16)
        wdb[...] = wd_st[...].astype(BF16)
        end = meta_ref[ntile + 1 + e]

        @pl.when(end < n_used)
        def _():
            for cp in fetch(meta_ref[end]):
                cp.start()

    @pl.when(active)
    def _():
        x = x_ref[...].astype(BF16)
        gate = jnp.dot(x, wgb[...], preferred_element_type=F32)
        up = jnp.dot(x, wub[...], preferred_element_type=F32)
        act = (gate * jax.nn.sigmoid(gate)) * up
        y_ref[...] = jnp.dot(act.astype(BF16), wdb[...], preferred_element_type=F32)

    @pl.when(jnp.logical_not(active))
    def _():
        y_ref[...] = jnp.zeros(y_ref.shape, F32)


def _ffn(meta_flat, xs, wg, wu, wd):
    cap, d = xs.shape
    ff = wg.shape[2]
    tb = MOE_TB
    ntile = cap // tb
    return pl.pallas_call(
        _ffn_kernel,
        grid_spec=pltpu.PrefetchScalarGridSpec(
            num_scalar_prefetch=1, grid=(ntile,),
            in_specs=[pl.BlockSpec((tb, d), lambda t, m: (jnp.minimum(t, m[ntile] - 1), 0)),
                      pl.BlockSpec(memory_space=pl.ANY), pl.BlockSpec(memory_space=pl.ANY),
                      pl.BlockSpec(memory_space=pl.ANY)],
            out_specs=pl.BlockSpec((tb, d), lambda t, m: (t, 0)),
            scratch_shapes=[pltpu.VMEM((d, ff), F32), pltpu.VMEM((d, ff), F32), pltpu.VMEM((ff, d), F32),
                            pltpu.VMEM((d, ff), BF16), pltpu.VMEM((d, ff), BF16), pltpu.VMEM((ff, d), BF16),
                            pltpu.SemaphoreType.DMA(())]),
        out_shape=jax.ShapeDtypeStruct((cap, d), F32),
        compiler_params=_cparams(("arbitrary",)),
        name="moe_ffn",
    )(meta_flat, xs, wg, wu, wd)


def _combine_kernel(dest_ref, y_ref, x1_ref, route_ref, g2_ref, nf_ref, o_ref, ybuf, sem):
    i = pl.program_id(0)
    nstep = pl.num_programs(0)
    tm = x1_ref.shape[0]

    def issue(tile, slot):
        def body(r, carry):
            p = (tile * tm + r) * TOP_K
            for k in range(TOP_K):
                pltpu.make_async_copy(y_ref.at[pl.ds(dest_ref[p + k], 1)], ybuf.at[slot, k, pl.ds(r, 1)],
                                      sem.at[slot]).start()
            return carry

        lax.fori_loop(0, tm, body, 0, unroll=DMA_UNROLL)

    @pl.when(i == 0)
    def _():
        issue(0, 0)

    @pl.when(i + 1 < nstep)
    def _():
        issue(i + 1, (i + 1) % 2)

    slot = i % 2
    for k in range(TOP_K):
        pltpu.make_async_copy(y_ref.at[pl.ds(0, tm)], ybuf.at[slot, k], sem.at[slot]).wait()
    rt = route_ref[...]
    moe = rt[:, 2:3] * ybuf[slot, 0] + rt[:, 3:4] * ybuf[slot, 1]
    xo = x1_ref[...] + g2_ref[0, 0] * moe
    o_ref[...] = xo * lax.rsqrt(jnp.mean(xo * xo, axis=-1, keepdims=True) + EPS) * nf_ref[...]


def _combine(dest_flat, y, x1, route, mod4, nf, seq):
    n, d = x1.shape
    tm = COMB_TM
    tiles_per_seq = seq // tm
    return pl.pallas_call(
        _combine_kernel,
        grid_spec=pltpu.PrefetchScalarGridSpec(
            num_scalar_prefetch=1, grid=(n // tm,),
            in_specs=[pl.BlockSpec(memory_space=pl.ANY),
                      pl.BlockSpec((tm, d), lambda i, dst: (i, 0)),
                      pl.BlockSpec((tm, LANES), lambda i, dst: (i, 0)),
                      pl.BlockSpec((1, 1, 1, d), lambda i, dst: (i // tiles_per_seq, 5, 0, 0)),
                      pl.BlockSpec((1, d), lambda i, dst: (0, 0))],
            out_specs=pl.BlockSpec((tm, d), lambda i, dst: (i, 0)),
            scratch_shapes=[pltpu.VMEM((2, TOP_K, tm, d), F32), pltpu.SemaphoreType.DMA((2,))]),
        out_shape=jax.ShapeDtypeStruct((n, d), F32),
        compiler_params=_cparams(("arbitrary",)),
        name="moe_combine",
    )(dest_flat, y, x1, route, mod4, nf)


def _alibi_slopes():
    n = NSA_HEADS
    full = jnp.asarray(2.0 ** (-8.0 * np.arange(1, n + 1) / n), dtype=F32)
    pieces, rest = [], full * LOG2E
    for _ in range(3):
        piece = rest.astype(BF16).astype(F32)
        pieces.append(piece)
        rest = rest - piece
    return jnp.concatenate([full] + pieces)


def _layer(x, c, w_ada, b_ada, norm1_w, w_in, cmp_pos_k, cmp_w1_k, cmp_w2_k, cmp_pos_v, cmp_w1_v, cmp_w2_v,
           gla_w_gate2, gla_b_gate, gla_norm_w, w_out, norm2_w, w_rg, b_rg, w_re, b_re, w_eg, w_eu, w_ed):
    bsz, s, d = x.shape
    dh = NSA_HEAD_DIM
    mod4 = _adaln(c, w_ada, b_ada).reshape(bsz, 6, 1, d)

    o_gate = NSA_COLS
    o_gla = o_gate + NSA_GATE_COLS
    o_lr = o_gla + 2 * GLA_HEADS * GLA_DK + 2 * GLA_HEADS * GLA_DV
    w_nsa, w_gla = _prep_w_in(w_in.T, o_gate, o_gla, o_lr)
    nw1 = norm1_w.reshape(1, d)
    proj_nsa, proj_gla = _inproj(x, mod4, nw1, w_nsa, w_gla, sc_idx=1, sh_idx=0)

    pos = jnp.stack([cmp_pos_k, cmp_pos_v]).reshape(2, 2, CMP_STRIDE * dh)
    kvc = _compress(proj_nsa, pos, jnp.stack([cmp_w1_k, cmp_w1_v]), jnp.stack([cmp_w2_k, cmp_w2_v]))
    o_nsa = _nsa(_alibi_slopes(), proj_nsa, kvc, proj_gla)

    wg_pad = jnp.zeros((LANES, GLA_HEADS * GLA_DK), F32).at[
        NSA_GATE_COLS:NSA_GATE_COLS + GLA_GATE_RANK].set(gla_w_gate2)
    o_gla_out = _gla(proj_gla, wg_pad, gla_b_gate.reshape(1, -1), gla_norm_w.reshape(1, -1))

    wr = jnp.concatenate([w_re, w_rg, jnp.zeros((d, LANES - N_EXPERTS - N_GROUPS), F32)], axis=1)
    br = jnp.concatenate([b_re, b_rg, jnp.zeros((LANES - N_EXPERTS - N_GROUPS,), F32)]).reshape(1, LANES)
    wr_hi = wr.astype(BF16)
    wr_cat = jnp.concatenate([wr_hi, (wr - wr_hi.astype(F32)).astype(BF16)], axis=1)
    x1, h2, route = _outproj(o_nsa, o_gla_out, x, w_out.astype(BF16), mod4, norm2_w.reshape(1, d), wr_cat, br)

    n = bsz * s
    npair = n * TOP_K
    cap = npair + N_EXPERTS * MOE_TB
    ntile = cap // MOE_TB
    route2 = route.reshape(n, LANES)
    dest, meta = _rank(route2)
    dest_flat = dest[:, :TOP_K].reshape(npair)
    meta_flat = jnp.concatenate([meta[:ntile, 0], meta[:1, 1], meta[:N_EXPERTS, 2]])
    ends_flat = jnp.concatenate([meta[:N_EXPERTS, 2], meta[:N_EXPERTS, 3], meta[:1, 1]])
    xs = _dispatch(dest_flat, ends_flat, h2.reshape(n, d), cap)
    y = _ffn(meta_flat, xs, w_eg, w_eu, w_ed)
    return x1.reshape(n, d), y, dest_flat, route2, mod4


def kernel(x, c, w_ada, b_ada, norm1_w, w_in, cmp_pos_k, cmp_w1_k, cmp_w2_k, cmp_pos_v, cmp_w1_v, cmp_w2_v,
           gla_w_gate2, gla_b_gate, gla_norm_w, w_out, norm2_w, w_router_group, b_router_group, w_router_expert,
           b_router_expert, w_expert_gate, w_expert_up, w_expert_down, norm_f_w):
    bsz, s, d = x.shape
    assert w_ada.shape[0] == 1, "single layer"
    x1, y, dest_flat, route2, mod4 = _layer(
        x, c, w_ada[0], b_ada[0], norm1_w[0], w_in[0], cmp_pos_k[0], cmp_w1_k[0], cmp_w2_k[0], cmp_pos_v[0],
        cmp_w1_v[0], cmp_w2_v[0], gla_w_gate2[0], gla_b_gate[0], gla_norm_w[0], w_out[0], norm2_w[0],
        w_router_group[0], b_router_group[0], w_router_expert[0], b_router_expert[0],
        w_expert_gate[0], w_expert_up[0], w_expert_down[0])
    out = _combine(dest_flat, y, x1, route2, mod4, norm_f_w.reshape(1, d), s)
    return out.reshape(bsz, s, d)
```

```python
import functools

import numpy as np
import jax
import jax.numpy as jnp
from jax import lax
from jax.experimental import pallas as pl
from jax.experimental.pallas import tpu as pltpu

F32 = jnp.float32
BF16 = jnp.bfloat16
I32 = jnp.int32

D_MODEL = 2048
NSA_HEAD_DIM = 64
NSA_HEADS = 16
NSA_KV_HEADS = 4
NSA_Q_PER_KV = 4
CMP_BLOCK = 32
CMP_STRIDE = 16
SEL_BLOCK = 64
N_SEL = 16
WINDOW = 512
N_BRANCH = 3
GLA_HEADS = 4
GLA_DV = 256
GLA_DK = 128
GLA_GATE_RANK = 16
GLA_GATE_NORM = 16.0
GLA_CHUNK = 64
GLA_SUB = 8
N_GROUPS = 4
EXPERTS_PER_GROUP = 8
N_EXPERTS = 32
TOP_K = 2
EXPERT_FF = 512
EPS = 1e-6
NEG = -1e30
FORCE = 1e30
LOG2E = 1.4426950408889634
MASKED = 2.0 ** 100

NSA_Q_COLS = NSA_HEADS * NSA_HEAD_DIM
NSA_KV_COLS = 2 * N_BRANCH * NSA_KV_HEADS * NSA_HEAD_DIM
NSA_GATE_COLS = N_BRANCH * NSA_HEADS
NSA_COLS = NSA_Q_COLS + NSA_KV_COLS
NSA_SLOTS = NSA_COLS // NSA_HEAD_DIM
GLA_Q_OFF = 0
GLA_K_OFF = GLA_HEADS * GLA_DK
GLA_V_OFF = 2 * GLA_HEADS * GLA_DK
GLA_OG_OFF = GLA_V_OFF + GLA_HEADS * GLA_DV
GLA_MISC_OFF = GLA_OG_OFF + GLA_HEADS * GLA_DV
LANES = 128
GLA_COLS = GLA_MISC_OFF + LANES

VMEM_LIMIT = 56 * 1024 * 1024

ADA_TN = 768
PREP_TR = 256
PREP_STEP = 512
INPROJ_TM = 256
INPROJ_TN = 512
NSA_TQ = 256
NSA_TK = 256
GLA_TS = 512
OUT_TM = 256
RANK_TM = 256
MOE_TB = 256
DISPATCH_TM = 1024
COMB_TM = 512
DMA_UNROLL = 8


def _cparams(sem):
    return pltpu.CompilerParams(dimension_semantics=sem, vmem_limit_bytes=VMEM_LIMIT)


def _adaln_kernel(ct_ref, w_ref, b_ref, o_ref, s_scr):
    nb = ct_ref.shape[1]
    kdim, tn = w_ref.shape

    @pl.when(pl.program_id(0) == 0)
    def _():
        ct = ct_ref[...]
        s = ct * jax.nn.sigmoid(ct)
        for b in range(nb):
            s_scr[b] = jnp.broadcast_to(s[:, b:b + 1], (kdim, LANES))

    def body(k, accs):
        r = pl.multiple_of(k * 8, 8)
        w8 = w_ref[pl.ds(r, 8), :]
        out = []
        for b, acc in enumerate(accs):
            s8 = s_scr[b, pl.ds(r, 8), :]
            out.append(acc + w8 * jnp.concatenate([s8] * (tn // LANES), axis=1))
        return tuple(out)

    accs = lax.fori_loop(0, kdim // 8, body, tuple(jnp.zeros((8, tn), F32) for _ in range(nb)), unroll=2)
    bias = b_ref[...]
    for b, acc in enumerate(accs):
        o_ref[b:b + 1, :] = jnp.sum(acc, axis=0, keepdims=True) + bias


def _adaln(c, w, b):
    nb, d = c.shape
    n = w.shape[1]
    return pl.pallas_call(
        _adaln_kernel,
        grid=(n // ADA_TN,),
        in_specs=[pl.BlockSpec((d, nb), lambda j: (0, 0)),
                  pl.BlockSpec((d, ADA_TN), lambda j: (0, j)),
                  pl.BlockSpec((1, ADA_TN), lambda j: (0, j))],
        out_specs=pl.BlockSpec((nb, ADA_TN), lambda j: (0, j)),
        out_shape=jax.ShapeDtypeStruct((nb, n), F32),
        scratch_shapes=[pltpu.VMEM((nb, d, LANES), F32)],
        compiler_params=_cparams(("arbitrary",)),
        name="adaln",
    )(c.T, w, b.reshape(1, n))


def _modulated_norm(x, nw, sc, sh):
    ms = jnp.mean(x * x, axis=-1, keepdims=True)
    h = x * lax.rsqrt(ms + EPS) * nw
    return h * (1.0 + sc) + sh


def _prep_w_in_kernel(wt_ref, wn_ref, wg_ref, *, o_gate, o_gla, o_lr):
    tr = wt_ref.shape[1]
    step = PREP_STEP

    def put(dst, c0, rows):
        dst[:, c0:c0 + rows.shape[0]] = rows.T.astype(BF16)

    for c in range(0, o_gate, step):
        put(wn_ref, c, wt_ref[c:c + step, :])
    for c in range(0, o_lr - o_gla, step):
        put(wg_ref, c, wt_ref[o_gla + c:o_gla + c + step, :])
    pad = wg_ref.shape[1] - (wt_ref.shape[0] - o_gate)
    misc = jnp.concatenate([wt_ref[o_gate:o_gla, :], wt_ref[o_lr:, :], jnp.zeros((pad, tr), F32)], axis=0)
    put(wg_ref, o_lr - o_gla, misc)


def _prep_w_in(w_in_t, o_gate, o_gla, o_lr):
    n, d = w_in_t.shape
    tr = PREP_TR
    assert o_gate % PREP_STEP == 0 and (o_lr - o_gla) % PREP_STEP == 0
    return pl.pallas_call(
        functools.partial(_prep_w_in_kernel, o_gate=o_gate, o_gla=o_gla, o_lr=o_lr),
        grid=(d // tr,),
        in_specs=[pl.BlockSpec((n, tr), lambda i: (0, i))],
        out_specs=[pl.BlockSpec((tr, NSA_COLS), lambda i: (i, 0)), pl.BlockSpec((tr, GLA_COLS), lambda i: (i, 0))],
        out_shape=[jax.ShapeDtypeStruct((d, NSA_COLS), BF16), jax.ShapeDtypeStruct((d, GLA_COLS), BF16)],
        compiler_params=_cparams(("parallel",)),
        name="prep_w_in",
    )(w_in_t)


def _inproj_kernel(x_ref, sc_ref, sh_ref, nw_ref, wn_ref, wg_ref, on_ref, og_ref):
    h = _modulated_norm(x_ref[0], nw_ref[...], sc_ref[0, 0], sh_ref[0, 0]).astype(BF16)
    dh = NSA_HEAD_DIM
    tn = INPROJ_TN
    for c in range(wn_ref.shape[1] // tn):
        acc = jnp.dot(h, wn_ref[:, c * tn:(c + 1) * tn], preferred_element_type=F32)
        for u in range(tn // dh):
            on_ref[0, c * (tn // dh) + u] = acc[:, u * dh:(u + 1) * dh]
    og_ref[0] = jnp.dot(h, wg_ref[...], preferred_element_type=F32)


def _inproj(x, mod4, nw, w_nsa, w_gla, *, sc_idx, sh_idx):
    bsz, s, d = x.shape
    n_nsa, n_gla = w_nsa.shape[1], w_gla.shape[1]
    tm = INPROJ_TM
    dh = NSA_HEAD_DIM
    return pl.pallas_call(
        _inproj_kernel,
        grid=(bsz, s // tm),
        in_specs=[pl.BlockSpec((1, tm, d), lambda b, i: (b, i, 0)),
                  pl.BlockSpec((1, 1, 1, d), lambda b, i: (b, sc_idx, 0, 0)),
                  pl.BlockSpec((1, 1, 1, d), lambda b, i: (b, sh_idx, 0, 0)),
                  pl.BlockSpec((1, d), lambda b, i: (0, 0)),
                  pl.BlockSpec((d, n_nsa), lambda b, i: (0, 0), pipeline_mode=pl.Buffered(1)),
                  pl.BlockSpec((d, n_gla), lambda b, i: (0, 0), pipeline_mode=pl.Buffered(1))],
        out_specs=[pl.BlockSpec((1, n_nsa // dh, tm, dh), lambda b, i: (b, 0, i, 0)),
                   pl.BlockSpec((1, tm, n_gla), lambda b, i: (b, i, 0))],
        out_shape=[jax.ShapeDtypeStruct((bsz, n_nsa // dh, s, dh), F32),
                   jax.ShapeDtypeStruct((bsz, s, n_gla), F32)],
        compiler_params=_cparams(("parallel", "parallel")),
        name="inproj",
    )(x, mod4, mod4, nw, w_nsa, w_gla)


def _hi_lo(x):
    hi = x.astype(BF16)
    return hi, (x - hi.astype(F32)).astype(BF16)


def _dot_hi_lo(x, w):
    x_hi, x_lo = _hi_lo(x)
    w_hi, w_lo = _hi_lo(w)
    return jnp.dot(jnp.concatenate([x_hi, x_lo, x_hi], axis=1), jnp.concatenate([w_hi, w_hi, w_lo], axis=0),
                   preferred_element_type=F32)


def _compress_kernel(a_ref, pos_ref, w1_ref, w2_ref, o_ref):
    ng = a_ref.shape[1]
    nch = a_ref.shape[2] // CMP_STRIDE
    a = jnp.concatenate([
        jnp.concatenate([a_ref[0, g, pl.ds(t, nch, stride=CMP_STRIDE), :] for t in range(CMP_STRIDE)], axis=1)
        for g in range(ng)], axis=0)
    pos = pos_ref[0]
    half = a.shape[1]
    y1 = _dot_hi_lo(a + pos[0:1], w1_ref[0, :half, :])
    y2 = _dot_hi_lo(a + pos[1:2], w1_ref[0, half:, :])
    h = y1 + pltpu.roll(y2, a.shape[0] - 1, axis=0)
    out = _dot_hi_lo(jax.nn.gelu(h), w2_ref[0])
    row = lax.broadcasted_iota(I32, (nch, out.shape[1]), 0)
    for g in range(ng):
        o_ref[0, g] = jnp.where(row < nch - 1, out[g * nch:(g + 1) * nch], 0.0)


def _compress(proj_nsa, pos, w1, w2):
    bsz, _, s, dh = proj_nsa.shape
    ng = NSA_KV_HEADS
    nch = s // CMP_STRIDE
    cw = CMP_STRIDE * dh
    return pl.pallas_call(
        _compress_kernel,
        grid=(bsz, 2),
        in_specs=[pl.BlockSpec((1, ng, s, dh), lambda b, t: (b, NSA_HEADS // ng + t, 0, 0)),
                  pl.BlockSpec((1, 2, cw), lambda b, t: (t, 0, 0)),
                  pl.BlockSpec((1, 2 * cw, dh), lambda b, t: (t, 0, 0)),
                  pl.BlockSpec((1, dh, dh), lambda b, t: (t, 0, 0))],
        out_specs=pl.BlockSpec((1, ng, nch, dh), lambda b, t: (b, t, 0, 0)),
        out_shape=jax.ShapeDtypeStruct((bsz, 2 * ng, nch, dh), F32),
        compiler_params=_cparams(("parallel", "parallel")),
        name="nsa_compress",
    )(proj_nsa, pos, w1, w2)


def _nt_dot(a, b, **kw):
    return lax.dot_general(a, b, (((1,), (1,)), ((), ())), preferred_element_type=F32, **kw)


def _nsa_kernel(slopes_ref, q_ref, kc_ref, vc_ref, ks_ref, vs_ref, kw_ref, vw_ref, gate_ref, o_ref,
                ksb, vst, kwb, vwt, q4_scr, notsel_scr, *scr):
    g = pl.program_id(1)
    qi = pl.program_id(2)
    tq_n = q_ref.shape[2]
    dh = NSA_HEAD_DIM
    nr = NSA_Q_PER_KV
    seq = ks_ref.shape[2]
    tk_n = NSA_TK
    nb = seq // SEL_BLOCK

    @pl.when(qi == 0)
    def _():
        row = lax.broadcasted_iota(I32, (seq, dh), 0)
        lane = lax.broadcasted_iota(I32, (seq, dh), 1)
        blk = row // SEL_BLOCK
        pos = jnp.where((lane >= nb) & (lane < nb + 3), (blk * SEL_BLOCK).astype(F32),
                        jnp.where((lane >= nb + 3) & (lane < nb + 6), (row % SEL_BLOCK).astype(F32), 0.0))
        ksb[...] = jnp.concatenate([jnp.where(lane == blk, -MASKED, pos), ks_ref[0, 0]], axis=1).astype(BF16)
        kwb[...] = jnp.concatenate([pos, kw_ref[0, 0]], axis=1).astype(BF16)
        for c in range(seq // tk_n):
            rows = slice(c * tk_n, (c + 1) * tk_n)
            for src, dst in ((vs_ref, vst), (vw_ref, vwt)):
                v = src[0, 0, rows, :]
                dst[c] = jnp.concatenate([v, v], axis=1).T[:dh].astype(BF16)

    t0 = qi * tq_n
    tq = t0 + lax.broadcasted_iota(I32, (1, tq_n), 1)
    slopes = [slopes_ref[g * nr + r] for r in range(nr)]
    scale = dh ** -0.5
    q_t = []
    for pair in range(nr // 2):
        both = jnp.concatenate([q_ref[0, 2 * pair], q_ref[0, 2 * pair + 1]], axis=1).T * scale
        q_t += [both[:dh], both[dh:]]

    ncp = kc_ref.shape[2]
    kc = kc_ref[0, 0]
    vc = vc_ref[0, 0]
    vc_t = jnp.concatenate([vc, vc], axis=1).T[:dh].astype(BF16)
    n_sub = lax.broadcasted_iota(I32, (ncp, 1), 0)
    blk_end = n_sub * CMP_STRIDE + (CMP_BLOCK - 1)
    center = n_sub.astype(F32) * CMP_STRIDE + (CMP_BLOCK - 1) / 2.0
    kc_hi, kc_lo = _hi_lo(kc)
    kc_cat = jnp.concatenate([kc_hi, kc_lo, kc_hi, jnp.zeros_like(kc_hi)], axis=1)
    q_hi, q_lo = _hi_lo(jnp.concatenate(q_t, axis=1))
    q_cat = jnp.concatenate([q_hi, q_hi, q_lo, jnp.zeros_like(q_hi)], axis=0)
    s = jnp.dot(kc_cat, q_cat, preferred_element_type=F32)
    tq_all = jnp.concatenate([tq] * nr, axis=1)
    slope_all = jnp.concatenate([jnp.full((1, tq_n), slopes[r], F32) for r in range(nr)], axis=1)
    valid_all = blk_end <= tq_all
    s = jnp.where(valid_all, s - slope_all * (tq_all.astype(F32) - center), NEG)
    e = jnp.exp(s - jnp.max(s, axis=0, keepdims=True))
    p = jnp.where(valid_all, e / jnp.sum(e, axis=0, keepdims=True), 0.0)
    psum = sum(p[:, r * tq_n:(r + 1) * tq_n] for r in range(nr))
    o_c_all = jnp.dot(vc_t, p.astype(BF16), preferred_element_type=F32)
    o_c = [o_c_all[:, r * tq_n:(r + 1) * tq_n] for r in range(nr)]

    n_sel = min(N_SEL, nb)
    notsel_scr[...] = jnp.zeros(notsel_scr.shape, F32)

    @pl.when((qi + 1) * tq_n > n_sel * SEL_BLOCK)
    def _():
        rowj = lax.broadcasted_iota(I32, (LANES, ncp), 0) * SEL_BLOCK
        coln = lax.broadcasted_iota(I32, (LANES, ncp), 1) * CMP_STRIDE
        overlap = jnp.where((coln < rowj + SEL_BLOCK) & (coln + CMP_BLOCK > rowj)
                            & (coln < (ncp - 1) * CMP_STRIDE) & (rowj < nb * SEL_BLOCK), 1.0, 0.0)
        ov = overlap.astype(BF16)
        imp = jnp.dot(jnp.concatenate([ov, ov], axis=1), jnp.concatenate(_hi_lo(psum), axis=0),
                      preferred_element_type=F32)[:nb]
        j_sub = lax.broadcasted_iota(I32, (nb, 1), 0)
        qblk = tq // SEL_BLOCK
        forced = (j_sub == 0) | (j_sub == qblk) | (j_sub == qblk - 1)
        imp = jnp.where(forced, FORCE, jnp.where(j_sub <= qblk, imp, NEG))
        cnt = jnp.zeros((nb, tq_n), F32)
        for i in range(nb):
            ci = imp[i:i + 1, :]
            tie = jnp.where(j_sub > i, 1.0, 0.0)
            cnt = cnt + jnp.where(ci > imp, 1.0, jnp.where(ci == imp, tie, 0.0))
        notsel_scr[...] = jnp.where(cnt < float(n_sel), 0.0, 1.0)

    notsel = notsel_scr[...]

    sub_h = lax.broadcasted_iota(I32, (dh - nb, 1), 0)
    for r in range(nr):
        scol = jnp.zeros((dh - nb, 1), F32)
        for i in range(3):
            piece = slopes_ref[(i + 1) * NSA_HEADS + g * nr + r]
            scol = jnp.where((sub_h == i) | (sub_h == 3 + i), piece, scol)
        q4_scr[:, r * tq_n:(r + 1) * tq_n] = jnp.concatenate(
            [notsel, jnp.broadcast_to(scol, (dh - nb, tq_n)), q_t[r] * LOG2E], axis=0).astype(BF16)

    nwt = WINDOW // tk_n
    nbuf = nwt + 2
    stats_s, stats_w = scr[0:3], scr[3:6]
    s_buf, p_buf, a_buf = scr[6:6 + nbuf], scr[6 + nbuf:6 + 2 * nbuf], scr[6 + 2 * nbuf:6 + 3 * nbuf]
    for m_ref, l_ref, acc_ref in (stats_s, stats_w):
        m_ref[...] = jnp.full(m_ref.shape, NEG, F32)
        l_ref[...] = jnp.zeros(l_ref.shape, F32)
        acc_ref[...] = jnp.zeros(acc_ref.shape, F32)
    key_i = lax.broadcasted_iota(I32, (tk_n, LANES), 0)
    qry_j = lax.broadcasted_iota(I32, (tk_n, LANES), 1)

    def scores(k_ref, kt, buf):
        k_tile = k_ref[kt * tk_n:(kt + 1) * tk_n, :]
        s_buf[buf][...] = jnp.dot(k_tile, q4_scr[...], preferred_element_type=F32)

    def softmax(buf, mode, stats):
        m_ref, l_ref, _ = stats
        for cb in range(nr * tq_n // LANES):
            cols = slice(cb * LANES, (cb + 1) * LANES)
            s = s_buf[buf][:, cols]
            if mode is not None:
                j = qry_j + (cb * LANES) % tq_n
                s = jnp.where(key_i <= j if mode == "causal" else key_i > j, s, -MASKED)
            m_prev = m_ref[:, cols]
            m_new = jnp.maximum(m_prev, jnp.max(s, axis=0, keepdims=True))
            alpha = jnp.exp2(m_prev - m_new)
            p = jnp.exp2(s - m_new)
            l_ref[:, cols] = alpha * l_ref[:, cols] + jnp.sum(p, axis=0, keepdims=True)
            m_ref[:, cols] = m_new
            a_buf[buf][:, cols] = alpha
            p_buf[buf][:, cols] = p.astype(BF16)

    def values(vt_ref, kt, buf, stats):
        acc_ref = stats[2]
        pv = jnp.dot(vt_ref[kt], p_buf[buf][...], preferred_element_type=F32)
        acc_ref[...] = acc_ref[...] * a_buf[buf][...] + pv

    def batch(jobs):
        for i in range(min(nbuf, len(jobs))):
            scores(jobs[i][0], jobs[i][2], i)
        for i, (_, vt_ref, kt, mode, stats) in enumerate(jobs):
            softmax(i % nbuf, mode, stats)
            values(vt_ref, kt, i % nbuf, stats)
            if i + nbuf < len(jobs):
                scores(jobs[i + nbuf][0], jobs[i + nbuf][2], i % nbuf)

    for k in range(seq // tq_n):
        @pl.when(qi == k)
        def _(k=k):
            jobs = [(ksb, vst, t, None, stats_s) for t in range(k)]
            jobs += [(kwb, vwt, k - back, "band" if back == nwt else None, stats_w)
                     for back in range(min(nwt, k), 0, -1)]
            jobs += [(kwb, vwt, k, "causal", stats_w), (ksb, vst, k, "causal", stats_s)]
            batch(jobs)

    gsel = jnp.where(lax.broadcasted_iota(I32, (LANES, LANES), 0)
                     == lax.broadcasted_iota(I32, (LANES, LANES), 1) + g * (nr * N_BRANCH), 1.0, 0.0)
    gs = gsel.astype(BF16)
    gates = jax.nn.sigmoid(jnp.dot(jnp.concatenate(_hi_lo(gate_ref[0]), axis=1), jnp.concatenate([gs, gs], axis=0),
                                   preferred_element_type=F32)).T
    (_, l_s, acc_s), (_, l_w, acc_w) = stats_s, stats_w
    for pair in range(nr // 2):
        o_t = []
        for r in (2 * pair, 2 * pair + 1):
            c0 = r * N_BRANCH
            cols = slice(r * tq_n, (r + 1) * tq_n)
            o_t.append(gates[c0:c0 + 1, :] * o_c[r] + (gates[c0 + 1:c0 + 2, :] / l_s[:, cols]) * acc_s[:, cols]
                       + (gates[c0 + 2:c0 + 3, :] / l_w[:, cols]) * acc_w[:, cols])
        o_ref[0, :, pair * LANES:(pair + 1) * LANES] = jnp.concatenate(o_t, axis=0).T


def _nsa(slopes, proj_nsa, kvc, proj_gla):
    bsz, _, s, dh = proj_nsa.shape
    g_n, nr = NSA_KV_HEADS, NSA_Q_PER_KV
    tq = NSA_TQ
    tk = NSA_TK
    assert tq == tk and WINDOW % tk == 0 and 2 * dh == LANES
    ncp = kvc.shape[2]
    kv0 = NSA_HEADS
    nq = nr * tq
    nbuf = WINDOW // tk + 2

    def kv_spec(i):
        return pl.BlockSpec((1, 1, s, dh), lambda b, g, q, i=i: (b, kv0 + i * g_n + g, 0, 0))

    return pl.pallas_call(
        _nsa_kernel,
        grid=(bsz, g_n, s // tq),
        in_specs=[pl.BlockSpec(memory_space=pltpu.SMEM),
                  pl.BlockSpec((1, nr, tq, dh), lambda b, g, q: (b, g, q, 0)),
                  pl.BlockSpec((1, 1, ncp, dh), lambda b, g, q: (b, g, 0, 0)),
                  pl.BlockSpec((1, 1, ncp, dh), lambda b, g, q: (b, g_n + g, 0, 0)),
                  kv_spec(2), kv_spec(3), kv_spec(4), kv_spec(5),
                  pl.BlockSpec((1, tq, LANES), lambda b, g, q: (b, q, GLA_MISC_OFF // LANES))],
        out_specs=pl.BlockSpec((1, tq, nr * dh), lambda b, g, q: (b, q, g)),
        out_shape=jax.ShapeDtypeStruct((bsz, s, NSA_HEADS * dh), F32),
        scratch_shapes=[pltpu.VMEM((s, LANES), BF16), pltpu.VMEM((s // tk, dh, tk), BF16)] * 2
        + [pltpu.VMEM((LANES, nq), BF16), pltpu.VMEM((s // SEL_BLOCK, tq), F32)]
        + [pltpu.VMEM((1, nq), F32), pltpu.VMEM((1, nq), F32), pltpu.VMEM((dh, nq), F32)] * 2
        + [pltpu.VMEM((tk, nq), F32)] * nbuf + [pltpu.VMEM((tk, nq), BF16)] * nbuf + [pltpu.VMEM((1, nq), F32)] * nbuf,
        compiler_params=_cparams(("parallel", "parallel", "arbitrary")),
        name="nsa_attention",
    )(slopes, proj_nsa, kvc, kvc, proj_nsa, proj_nsa, proj_nsa, proj_nsa, proj_gla)


def _gla_kernel(q_ref, k_ref, v_ref, og_ref, lr_ref, wg_ref, bg_ref, nw_ref, o_ref, st_scr, la_scr, b_scr):
    rows_n = q_ref.shape[1]
    c_n, sub = GLA_CHUNK, GLA_SUB
    nh, dk, dv = GLA_HEADS, GLA_DK, GLA_DV

    @pl.when(pl.program_id(1) == 0)
    def _():
        st_scr[...] = jnp.zeros(st_scr.shape, F32)

    z = _dot_hi_lo(lr_ref[0], wg_ref[...]) + bg_ref[...]
    la_scr[...] = (jnp.minimum(z, 0.0) - jnp.log1p(jnp.exp(-jnp.abs(z)))) * (1.0 / GLA_GATE_NORM)
    tril = jnp.where(lax.broadcasted_iota(I32, (c_n, c_n), 0) >= lax.broadcasted_iota(I32, (c_n, c_n), 1), 1.0, 0.0)
    tril3 = jnp.concatenate([tril.astype(BF16)] * 3, axis=1)
    row_c = lax.broadcasted_iota(I32, (c_n, 1), 0)
    row_s = lax.broadcasted_iota(I32, (sub, 1), 0)
    lane_c = lax.broadcasted_iota(I32, (1, c_n), 1)
    nw = nw_ref[...]
    hk = [slice(h * dk, (h + 1) * dk) for h in range(nh)]
    hv = [slice(h * dv, (h + 1) * dv) for h in range(nh)]

    def chunk(c, carry):
        r0 = pl.multiple_of(c * c_n, c_n)
        rows = pl.ds(r0, c_n)
        qc = q_ref[0, rows, :] * (dk ** -0.5)
        kc = k_ref[0, rows, :]
        vc = [v_ref[0, rows, hv[h]].astype(BF16) for h in range(nh)]
        pieces, rest = [], la_scr[rows, :]
        for _ in range(3):
            pieces.append(rest.astype(BF16))
            rest = rest - pieces[-1].astype(F32)
        b = jnp.dot(tril3, jnp.concatenate(pieces, axis=0), preferred_element_type=F32)
        b_scr[...] = b
        st = [st_scr[h] for h in range(nh)]
        q_e = (qc * jnp.exp(b)).astype(BF16)
        o = [_nt_dot(q_e[:, hk[h]], st[h].astype(BF16)) for h in range(nh)]
        strips = [[] for _ in range(nh)]
        for blk in range(c_n // sub):
            lo = blk * sub
            q_i = qc[lo:lo + sub]
            b_i = b[lo:lo + sub]
            a = [jnp.zeros((sub, c_n), F32) for _ in range(nh)]
            if blk > 0:
                b_r = b_scr[lo - 1:lo, :]
                q_d = (q_i * jnp.exp(b_i - b_r)).astype(BF16)
                k_d = (kc * jnp.exp(jnp.where(row_c < lo, b_r - b, -jnp.inf))).astype(BF16)
                a = [_nt_dot(q_d[:, hk[h]], k_d[:, hk[h]]) for h in range(nh)]
            for j in range(sub):
                b_j = b_scr[lo + j:lo + j + 1, :]
                k_j = k_ref[0, pl.ds(r0 + lo + j, 1), :]
                prod = q_i * k_j * jnp.exp(jnp.where(row_s >= j, b_i - b_j, -jnp.inf))
                for h in range(nh):
                    col = jnp.sum(prod[:, hk[h]], axis=-1, keepdims=True)
                    a[h] = jnp.where(lane_c == lo + j, col, a[h]) if blk == 0 else (
                        a[h] + jnp.where(lane_c == lo + j, col, 0.0))
            for h in range(nh):
                strips[h].append(a[h])
        for h in range(nh):
            attn = jnp.concatenate(strips[h], axis=0)
            o[h] = o[h] + jnp.dot(attn.astype(BF16), vc[h], preferred_element_type=F32)
        b_last = b_scr[c_n - 1:c_n, :]
        k_dec = (kc * jnp.exp(b_last - b)).astype(BF16)
        decay = jnp.exp(b_last)
        for h in range(nh):
            st_scr[h] = st[h] * decay[:, hk[h]] + lax.dot_general(
                vc[h], k_dec[:, hk[h]], (((0,), (0,)), ((), ())), preferred_element_type=F32)
        for h in range(nh):
            og = og_ref[0, rows, hv[h]]
            on = o[h] * lax.rsqrt(jnp.mean(o[h] * o[h], axis=-1, keepdims=True) + EPS) * nw
            o_ref[0, rows, hv[h]] = on * (og * jax.nn.sigmoid(og))
        return carry

    lax.fori_loop(0, rows_n // c_n, chunk, 0)


def _gla(proj_gla, wg_pad, bg, nw):
    bsz, s, _ = proj_gla.shape
    nh, dk, dv = GLA_HEADS, GLA_DK, GLA_DV
    ts = GLA_TS
    wk, wv = nh * dk, nh * dv
    return pl.pallas_call(
        _gla_kernel,
        grid=(bsz, s // ts),
        in_specs=[pl.BlockSpec((1, ts, wk), lambda b, i: (b, i, GLA_Q_OFF // wk)),
                  pl.BlockSpec((1, ts, wk), lambda b, i: (b, i, GLA_K_OFF // wk)),
                  pl.BlockSpec((1, ts, wv), lambda b, i: (b, i, GLA_V_OFF // wv)),
                  pl.BlockSpec((1, ts, wv), lambda b, i: (b, i, GLA_OG_OFF // wv)),
                  pl.BlockSpec((1, ts, LANES), lambda b, i: (b, i, GLA_MISC_OFF // LANES)),
                  pl.BlockSpec((LANES, wk), lambda b, i: (0, 0)),
                  pl.BlockSpec((1, wk), lambda b, i: (0, 0)),
                  pl.BlockSpec((1, dv), lambda b, i: (0, 0))],
        out_specs=pl.BlockSpec((1, ts, wv), lambda b, i: (b, i, 0)),
        out_shape=jax.ShapeDtypeStruct((bsz, s, wv), F32),
        scratch_shapes=[pltpu.VMEM((nh, dv, dk), F32), pltpu.VMEM((ts, wk), F32), pltpu.VMEM((GLA_CHUNK, wk), F32)],
        compiler_params=_cparams(("parallel", "arbitrary")),
        name="gla",
    )(proj_gla, proj_gla, proj_gla, proj_gla, proj_gla, wg_pad, bg, nw)


def _outproj_kernel(nsa_ref, gla_ref, x_ref, wo_ref, g1_ref, sc_ref, sh_ref, nw_ref, wr_ref, br_ref,
                    x1_ref, h_ref, route_ref):
    half = nsa_ref.shape[2]
    acc = jnp.dot(nsa_ref[0].astype(BF16), wo_ref[:half, :], preferred_element_type=F32)
    acc = acc + jnp.dot(gla_ref[0].astype(BF16), wo_ref[half:, :], preferred_element_type=F32)
    x1 = x_ref[0] + g1_ref[0, 0] * acc
    x1_ref[0] = x1
    h = _modulated_norm(x1, nw_ref[...], sc_ref[0, 0], sh_ref[0, 0])
    h_ref[0] = h
    h_hi = h.astype(BF16)
    h_lo = (h - h_hi.astype(F32)).astype(BF16)
    t = jnp.dot(h_hi, wr_ref[...], preferred_element_type=F32)
    logits = (t[:, :LANES] + t[:, LANES:] + jnp.dot(h_lo, wr_ref[:, :LANES], preferred_element_type=F32)
              + br_ref[...])
    lane = lax.broadcasted_iota(I32, (1, LANES), 1)
    ninf = -jnp.inf
    is_g = (lane >= N_EXPERTS) & (lane < N_EXPERTS + N_GROUPS)
    gl = jnp.where(is_g, logits, ninf)
    ge = jnp.exp(gl - jnp.max(gl, axis=-1, keepdims=True))
    gp = ge / jnp.sum(ge, axis=-1, keepdims=True)
    gp_max = jnp.max(gp, axis=-1, keepdims=True)
    grp = jnp.min(jnp.where((gp == gp_max) & is_g, lane - N_EXPERTS, LANES), axis=-1, keepdims=True)
    in_grp = (lane // EXPERTS_PER_GROUP == grp) & (lane < N_EXPERTS)
    el = jnp.where(in_grp, logits, ninf)
    v1 = jnp.max(el, axis=-1, keepdims=True)
    i1 = jnp.min(jnp.where(el == v1, lane, LANES), axis=-1, keepdims=True)
    el2 = jnp.where(lane == i1, ninf, el)
    v2 = jnp.max(el2, axis=-1, keepdims=True)
    i2 = jnp.min(jnp.where(el2 == v2, lane, LANES), axis=-1, keepdims=True)
    e2 = jnp.exp(v2 - v1)
    den = 1.0 + e2
    w1 = gp_max * (1.0 / den)
    w2 = gp_max * (e2 / den)
    route_ref[0] = jnp.where(lane == 0, i1.astype(F32), jnp.where(lane == 1, i2.astype(F32), jnp.where(
        lane == 2, w1, jnp.where(lane == 3, w2, 0.0))))


def _outproj(o_nsa, o_gla, x, wo, mod4, nw, wr, br):
    bsz, s, d = x.shape
    tm = OUT_TM
    half = o_nsa.shape[2]

    def mod_spec(idx):
        return pl.BlockSpec((1, 1, 1, d), lambda b, i: (b, idx, 0, 0))

    row = lambda w: pl.BlockSpec((1, tm, w), lambda b, i: (b, i, 0))
    return pl.pallas_call(
        _outproj_kernel,
        grid=(bsz, s // tm),
        in_specs=[row(half), row(half), row(d),
                  pl.BlockSpec((2 * half, d), lambda b, i: (0, 0)),
                  mod_spec(2), mod_spec(4), mod_spec(3),
                  pl.BlockSpec((1, d), lambda b, i: (0, 0)),
                  pl.BlockSpec((d, 2 * LANES), lambda b, i: (0, 0)),
                  pl.BlockSpec((1, LANES), lambda b, i: (0, 0))],
        out_specs=[row(d), row(d), row(LANES)],
        out_shape=[jax.ShapeDtypeStruct((bsz, s, d), F32), jax.ShapeDtypeStruct((bsz, s, d), F32),
                   jax.ShapeDtypeStruct((bsz, s, LANES), F32)],
        compiler_params=_cparams(("parallel", "parallel")),
        name="outproj_router",
    )(o_nsa, o_gla, x, wo, mod4, mod4, mod4, nw, wr, br)


def _rank_kernel(route_ref, dest_ref, meta_ref, rank_scr):
    n = route_ref.shape[0]
    tm = RANK_TM
    lane_i = lax.broadcasted_iota(I32, (1, LANES), 1)
    lane = lane_i.astype(F32)
    strict = jnp.where(lax.broadcasted_iota(I32, (tm, tm), 0) > lax.broadcasted_iota(I32, (tm, tm), 1),
                       1.0, 0.0).astype(BF16)

    def two_lanes(a, b):
        return jnp.where(lane_i == 0, a, jnp.where(lane_i == 1, b, 0.0))

    def pick(e, table):
        return jnp.sum(jnp.where(lane == e, table, 0.0), axis=-1, keepdims=True)

    def count(i, seen):
        r0 = pl.multiple_of(i * tm, tm)
        rt = route_ref[pl.ds(r0, tm), :]
        e1, e2 = rt[:, 0:1], rt[:, 1:2]
        member = jnp.where(lane == e1, 1.0, jnp.where(lane == e2, 1.0, 0.0))
        before = jnp.dot(strict, member.astype(BF16), preferred_element_type=F32) + seen
        rank_scr[pl.ds(r0, tm), :] = two_lanes(pick(e1, before), pick(e2, before))
        return seen + jnp.sum(member, axis=0, keepdims=True)

    counts = lax.fori_loop(0, n // tm, count, jnp.zeros((1, LANES), F32))
    ntile = jnp.floor((counts + (MOE_TB - 1)) * (1.0 / MOE_TB))
    incl = jnp.where(lax.broadcasted_iota(I32, (LANES, LANES), 0) <= lax.broadcasted_iota(I32, (LANES, LANES), 1),
                     1.0, 0.0).astype(BF16)
    tile_end = jnp.dot(jnp.broadcast_to(ntile, (8, LANES)).astype(BF16), incl,
                       preferred_element_type=F32)[0:1]
    row_start = (tile_end - ntile) * MOE_TB

    def place(i, carry):
        r0 = pl.multiple_of(i * tm, tm)
        rt = route_ref[pl.ds(r0, tm), :]
        rk = rank_scr[pl.ds(r0, tm), :]
        d1 = pick(rt[:, 0:1], row_start) + rk[:, 0:1]
        d2 = pick(rt[:, 1:2], row_start) + rk[:, 1:2]
        dest_ref[pl.ds(r0, tm), :] = two_lanes(d1, d2).astype(I32)
        return carry

    lax.fori_loop(0, n // tm, place, 0)
    trow = lax.broadcasted_iota(I32, (meta_ref.shape[0], 1), 0).astype(F32)
    texp = jnp.sum(jnp.where((tile_end <= trow) & (lane_i < N_EXPERTS), 1.0, 0.0), axis=-1, keepdims=True)
    texp = jnp.minimum(texp, N_EXPERTS - 1.0)
    used = pick(N_EXPERTS - 1.0, tile_end)
    diag = lax.broadcasted_iota(I32, (meta_ref.shape[0], LANES), 0) == lane_i
    end_rows = jnp.sum(jnp.where(diag, tile_end, 0.0), axis=-1, keepdims=True)
    ntile_rows = jnp.sum(jnp.where(diag, ntile, 0.0), axis=-1, keepdims=True)
    meta_ref[...] = jnp.where(lane_i == 2, end_rows, jnp.where(lane_i == 3, ntile_rows, two_lanes(
        texp, jnp.broadcast_to(used, texp.shape)))).astype(I32)


def _rank(route):
    n = route.shape[0]
    return pl.pallas_call(
        _rank_kernel,
        out_shape=[jax.ShapeDtypeStruct((n, LANES), I32), jax.ShapeDtypeStruct((LANES, LANES), I32)],
        scratch_shapes=[pltpu.VMEM((n, LANES), F32)],
        compiler_params=pltpu.CompilerParams(vmem_limit_bytes=VMEM_LIMIT),
        name="moe_rank",
    )(route)


def _dispatch_kernel(dest_ref, ends_ref, h_ref, xs_ref, zero_scr, sem, zsem):
    i = pl.program_id(0)
    tm = h_ref.shape[0]
    tb = zero_scr.shape[0]

    @pl.when(i == 0)
    def _():
        zero_scr[...] = jnp.zeros(zero_scr.shape, F32)

        def zero_copy(e):
            r0 = pl.multiple_of((ends_ref[e] - 1) * tb, tb)
            return pltpu.make_async_copy(zero_scr, xs_ref.at[pl.ds(r0, tb)], zsem)

        def start(e, carry):
            @pl.when(ends_ref[N_EXPERTS + e] > 0)
            def _():
                zero_copy(e).start()
            return carry

        def wait(e, carry):
            @pl.when(ends_ref[N_EXPERTS + e] > 0)
            def _():
                zero_copy(e).wait()
            return carry

        def tail_copy(t):
            return pltpu.make_async_copy(zero_scr, xs_ref.at[pl.ds(pl.multiple_of(t * tb, tb), tb)], zsem)

        def tail_start(t, carry):
            tail_copy(t).start()
            return carry

        def tail_wait(t, carry):
            tail_copy(t).wait()
            return carry

        used = ends_ref[2 * N_EXPERTS]
        lax.fori_loop(0, N_EXPERTS, start, 0)
        lax.fori_loop(used, xs_ref.shape[0] // tb, tail_start, 0)
        lax.fori_loop(0, N_EXPERTS, wait, 0)
        lax.fori_loop(used, xs_ref.shape[0] // tb, tail_wait, 0)

    def issue(r, carry):
        p = (i * tm + r) * TOP_K
        for k in range(TOP_K):
            pltpu.make_async_copy(h_ref.at[pl.ds(r, 1)], xs_ref.at[pl.ds(dest_ref[p + k], 1)], sem).start()
        return carry

    lax.fori_loop(0, tm, issue, 0, unroll=DMA_UNROLL)
    for k in range(TOP_K):
        pltpu.make_async_copy(h_ref, xs_ref.at[pl.ds(0, tm)], sem).wait()


def _dispatch(dest_flat, ends_flat, h, cap):
    n, d = h.shape
    tm = DISPATCH_TM
    return pl.pallas_call(
        _dispatch_kernel,
        grid_spec=pltpu.PrefetchScalarGridSpec(
            num_scalar_prefetch=2, grid=(n // tm,),
            in_specs=[pl.BlockSpec((tm, d), lambda i, dst, ends: (i, 0))],
            out_specs=pl.BlockSpec(memory_space=pl.ANY),
            scratch_shapes=[pltpu.VMEM((MOE_TB, d), F32), pltpu.SemaphoreType.DMA(()), pltpu.SemaphoreType.DMA(())]),
        out_shape=jax.ShapeDtypeStruct((cap, d), F32),
        compiler_params=_cparams(("arbitrary",)),
        name="moe_dispatch",
    )(dest_flat, ends_flat, h)


def _ffn_kernel(meta_ref, x_ref, wg_hbm, wu_hbm, wd_hbm, y_ref, wg_st, wu_st, wd_st, wgb, wub, wdb, wsem):
    t = pl.program_id(0)
    ntile = pl.num_programs(0)
    n_used = meta_ref[ntile]
    e = meta_ref[t]
    e_prev = meta_ref[jnp.maximum(t - 1, 0)]
    active = t < n_used

    def fetch(expert):
        return [pltpu.make_async_copy(src.at[expert], dst, wsem)
                for src, dst in ((wg_hbm, wg_st), (wu_hbm, wu_st), (wd_hbm, wd_st))]

    @pl.when(t == 0)
    def _():
        for cp in fetch(e):
            cp.start()

    @pl.when(active & ((t == 0) | (e != e_prev)))
    def _():
        for cp in fetch(e):
            cp.wait()
        wgb[...] = wg_st[...].astype(BF16)
        wub[...] = wu_st[...].astype(BF16)
        wdb[...] = wd_st[...].astype(BF16)
        end = meta_ref[ntile + 1 + e]

        @pl.when(end < n_used)
        def _():
            for cp in fetch(meta_ref[end]):
                cp.start()

    @pl.when(active)
    def _():
        x = x_ref[...].astype(BF16)
        gate = jnp.dot(x, wgb[...], preferred_element_type=F32)
        up = jnp.dot(x, wub[...], preferred_element_type=F32)
        act = (gate * jax.nn.sigmoid(gate)) * up
        y_ref[...] = jnp.dot(act.astype(BF16), wdb[...], preferred_element_type=F32)

    @pl.when(jnp.logical_not(active))
    def _():
        y_ref[...] = jnp.zeros(y_ref.shape, F32)


def _ffn(meta_flat, xs, wg, wu, wd):
    cap, d = xs.shape
    ff = wg.shape[2]
    tb = MOE_TB
    ntile = cap // tb
    return pl.pallas_call(
        _ffn_kernel,
        grid_spec=pltpu.PrefetchScalarGridSpec(
            num_scalar_prefetch=1, grid=(ntile,),
            in_specs=[pl.BlockSpec((tb, d), lambda t, m: (jnp.minimum(t, m[ntile] - 1), 0)),
                      pl.BlockSpec(memory_space=pl.ANY), pl.BlockSpec(memory_space=pl.ANY),
                      pl.BlockSpec(memory_space=pl.ANY)],
            out_specs=pl.BlockSpec((tb, d), lambda t, m: (t, 0)),
            scratch_shapes=[pltpu.VMEM((d, ff), F32), pltpu.VMEM((d, ff), F32), pltpu.VMEM((ff, d), F32),
                            pltpu.VMEM((d, ff), BF16), pltpu.VMEM((d, ff), BF16), pltpu.VMEM((ff, d), BF16),
                            pltpu.SemaphoreType.DMA(())]),
        out_shape=jax.ShapeDtypeStruct((cap, d), F32),
        compiler_params=_cparams(("arbitrary",)),
        name="moe_ffn",
    )(meta_flat, xs, wg, wu, wd)


def _combine_kernel(dest_ref, y_ref, x1_ref, route_ref, g2_ref, nf_ref, o_ref, ybuf, sem):
    i = pl.program_id(0)
    nstep = pl.num_programs(0)
    tm = x1_ref.shape[0]

    def issue(tile, slot):
        def body(r, carry):
            p = (tile * tm + r) * TOP_K
            for k in range(TOP_K):
                pltpu.make_async_copy(y_ref.at[pl.ds(dest_ref[p + k], 1)], ybuf.at[slot, k, pl.ds(r, 1)],
                                      sem.at[slot]).start()
            return carry

        lax.fori_loop(0, tm, body, 0, unroll=DMA_UNROLL)

    @pl.when(i == 0)
    def _():
        issue(0, 0)

    @pl.when(i + 1 < nstep)
    def _():
        issue(i + 1, (i + 1) % 2)

    slot = i % 2
    for k in range(TOP_K):
        pltpu.make_async_copy(y_ref.at[pl.ds(0, tm)], ybuf.at[slot, k], sem.at[slot]).wait()
    rt = route_ref[...]
    moe = rt[:, 2:3] * ybuf[slot, 0] + rt[:, 3:4] * ybuf[slot, 1]
    xo = x1_ref[...] + g2_ref[0, 0] * moe
    o_ref[...] = xo * lax.rsqrt(jnp.mean(xo * xo, axis=-1, keepdims=True) + EPS) * nf_ref[...]


def _combine(dest_flat, y, x1, route, mod4, nf, seq):
    n, d = x1.shape
    tm = COMB_TM
    tiles_per_seq = seq // tm
    return pl.pallas_call(
        _combine_kernel,
        grid_spec=pltpu.PrefetchScalarGridSpec(
            num_scalar_prefetch=1, grid=(n // tm,),
            in_specs=[pl.BlockSpec(memory_space=pl.ANY),
                      pl.BlockSpec((tm, d), lambda i, dst: (i, 0)),
                      pl.BlockSpec((tm, LANES), lambda i, dst: (i, 0)),
                      pl.BlockSpec((1, 1, 1, d), lambda i, dst: (i // tiles_per_seq, 5, 0, 0)),
                      pl.BlockSpec((1, d), lambda i, dst: (0, 0))],
            out_specs=pl.BlockSpec((tm, d), lambda i, dst: (i, 0)),
            scratch_shapes=[pltpu.VMEM((2, TOP_K, tm, d), F32), pltpu.SemaphoreType.DMA((2,))]),
        out_shape=jax.ShapeDtypeStruct((n, d), F32),
        compiler_params=_cparams(("arbitrary",)),
        name="moe_combine",
    )(dest_flat, y, x1, route, mod4, nf)


def _alibi_slopes():
    n = NSA_HEADS
    full = jnp.asarray(2.0 ** (-8.0 * np.arange(1, n + 1) / n), dtype=F32)
    pieces, rest = [], full * LOG2E
    for _ in range(3):
        piece = rest.astype(BF16).astype(F32)
        pieces.append(piece)
        rest = rest - piece
    return jnp.concatenate([full] + pieces)


def _layer(x, c, w_ada, b_ada, norm1_w, w_in, cmp_pos_k, cmp_w1_k, cmp_w2_k, cmp_pos_v, cmp_w1_v, cmp_w2_v,
           gla_w_gate2, gla_b_gate, gla_norm_w, w_out, norm2_w, w_rg, b_rg, w_re, b_re, w_eg, w_eu, w_ed):
    bsz, s, d = x.shape
    dh = NSA_HEAD_DIM
    mod4 = _adaln(c, w_ada, b_ada).reshape(bsz, 6, 1, d)

    o_gate = NSA_COLS
    o_gla = o_gate + NSA_GATE_COLS
    o_lr = o_gla + 2 * GLA_HEADS * GLA_DK + 2 * GLA_HEADS * GLA_DV
    w_nsa, w_gla = _prep_w_in(w_in.T, o_gate, o_gla, o_lr)
    nw1 = norm1_w.reshape(1, d)
    proj_nsa, proj_gla = _inproj(x, mod4, nw1, w_nsa, w_gla, sc_idx=1, sh_idx=0)

    pos = jnp.stack([cmp_pos_k, cmp_pos_v]).reshape(2, 2, CMP_STRIDE * dh)
    kvc = _compress(proj_nsa, pos, jnp.stack([cmp_w1_k, cmp_w1_v]), jnp.stack([cmp_w2_k, cmp_w2_v]))
    o_nsa = _nsa(_alibi_slopes(), proj_nsa, kvc, proj_gla)

    wg_pad = jnp.zeros((LANES, GLA_HEADS * GLA_DK), F32).at[
        NSA_GATE_COLS:NSA_GATE_COLS + GLA_GATE_RANK].set(gla_w_gate2)
    o_gla_out = _gla(proj_gla, wg_pad, gla_b_gate.reshape(1, -1), gla_norm_w.reshape(1, -1))

    wr = jnp.concatenate([w_re, w_rg, jnp.zeros((d, LANES - N_EXPERTS - N_GROUPS), F32)], axis=1)
    br = jnp.concatenate([b_re, b_rg, jnp.zeros((LANES - N_EXPERTS - N_GROUPS,), F32)]).reshape(1, LANES)
    wr_hi = wr.astype(BF16)
    wr_cat = jnp.concatenate([wr_hi, (wr - wr_hi.astype(F32)).astype(BF16)], axis=1)
    x1, h2, route = _outproj(o_nsa, o_gla_out, x, w_out.astype(BF16), mod4, norm2_w.reshape(1, d), wr_cat, br)

    n = bsz * s
    npair = n * TOP_K
    cap = npair + N_EXPERTS * MOE_TB
    ntile = cap // MOE_TB
    route2 = route.reshape(n, LANES)
    dest, meta = _rank(route2)
    dest_flat = dest[:, :TOP_K].reshape(npair)
    meta_flat = jnp.concatenate([meta[:ntile, 0], meta[:1, 1], meta[:N_EXPERTS, 2]])
    ends_flat = jnp.concatenate([meta[:N_EXPERTS, 2], meta[:N_EXPERTS, 3], meta[:1, 1]])
    xs = _dispatch(dest_flat, ends_flat, h2.reshape(n, d), cap)
    y = _ffn(meta_flat, xs, w_eg, w_eu, w_ed)
    return x1.reshape(n, d), y, dest_flat, route2, mod4


def kernel(x, c, w_ada, b_ada, norm1_w, w_in, cmp_pos_k, cmp_w1_k, cmp_w2_k, cmp_pos_v, cmp_w1_v, cmp_w2_v,
           gla_w_gate2, gla_b_gate, gla_norm_w, w_out, norm2_w, w_router_group, b_router_group, w_router_expert,
           b_router_expert, w_expert_gate, w_expert_up, w_expert_down, norm_f_w):
    bsz, s, d = x.shape
    assert w_ada.shape[0] == 1, "single layer"
    x1, y, dest_flat, route2, mod4 = _layer(
        x, c, w_ada[0], b_ada[0], norm1_w[0], w_in[0], cmp_pos_k[0], cmp_w1_k[0], cmp_w2_k[0], cmp_pos_v[0],
        cmp_w1_v[0], cmp_w2_v[0], gla_w_gate2[0], gla_b_gate[0], gla_norm_w[0], w_out[0], norm2_w[0],
        w_router_group[0], b_router_group[0], w_router_expert[0], b_router_expert[0],
        w_expert_gate[0], w_expert_up[0], w_expert_down[0])
    out = _combine(dest_flat, y, x1, route2, mod4, norm_f_w.reshape(1, d), s)
    return out.reshape(bsz, s, d)
```

```python
import functools

import numpy as np
import jax
import jax.numpy as jnp
from jax import lax
from jax.experimental import pallas as pl
from jax.experimental.pallas import tpu as pltpu

F32 = jnp.float32
BF16 = jnp.bfloat16
I32 = jnp.int32

D_MODEL = 2048
NSA_HEAD_DIM = 64
NSA_HEADS = 16
NSA_KV_HEADS = 4
NSA_Q_PER_KV = 4
CMP_BLOCK = 32
CMP_STRIDE = 16
SEL_BLOCK = 64
N_SEL = 16
WINDOW = 512
N_BRANCH = 3
GLA_HEADS = 4
GLA_DV = 256
GLA_DK = 128
GLA_GATE_RANK = 16
GLA_GATE_NORM = 16.0
GLA_CHUNK = 64
GLA_SUB = 8
N_GROUPS = 4
EXPERTS_PER_GROUP = 8
N_EXPERTS = 32
TOP_K = 2
EXPERT_FF = 512
EPS = 1e-6
NEG = -1e30
FORCE = 1e30
LOG2E = 1.4426950408889634
MASKED = 2.0 ** 100

NSA_Q_COLS = NSA_HEADS * NSA_HEAD_DIM
NSA_KV_COLS = 2 * N_BRANCH * NSA_KV_HEADS * NSA_HEAD_DIM
NSA_GATE_COLS = N_BRANCH * NSA_HEADS
NSA_COLS = NSA_Q_COLS + NSA_KV_COLS
NSA_SLOTS = NSA_COLS // NSA_HEAD_DIM
GLA_Q_OFF = 0
GLA_K_OFF = GLA_HEADS * GLA_DK
GLA_V_OFF = 2 * GLA_HEADS * GLA_DK
GLA_OG_OFF = GLA_V_OFF + GLA_HEADS * GLA_DV
GLA_MISC_OFF = GLA_OG_OFF + GLA_HEADS * GLA_DV
LANES = 128
GLA_COLS = GLA_MISC_OFF + LANES

VMEM_LIMIT = 56 * 1024 * 1024

ADA_TN = 768
PREP_TR = 256
PREP_STEP = 512
INPROJ_TM = 256
INPROJ_TN = 512
NSA_TQ = 256
NSA_TK = 256
GLA_TS = 512
OUT_TM = 512
RANK_TM = 256
MOE_TB = 256
DISPATCH_TM = 1024
COMB_TM = 512
DMA_UNROLL = 8


def _cparams(sem):
    return pltpu.CompilerParams(dimension_semantics=sem, vmem_limit_bytes=VMEM_LIMIT)


def _adaln_kernel(ct_ref, w_ref, b_ref, o_ref, s_scr):
    nb = ct_ref.shape[1]
    kdim, tn = w_ref.shape

    @pl.when(pl.program_id(0) == 0)
    def _():
        ct = ct_ref[...]
        s = ct * jax.nn.sigmoid(ct)
        for b in range(nb):
            s_scr[b] = jnp.broadcast_to(s[:, b:b + 1], (kdim, LANES))

    nlt = tn // LANES

    def body(k, accs):
        r = pl.multiple_of(k * 8, 8)
        w8 = [w_ref[pl.ds(r, 8), j * LANES:(j + 1) * LANES] for j in range(nlt)]
        out = []
        for b in range(nb):
            s8 = s_scr[b, pl.ds(r, 8), :]
            out.append(tuple(accs[b][j] + w8[j] * s8 for j in range(nlt)))
        return tuple(out)

    zero = jnp.zeros((8, LANES), F32)
    accs = lax.fori_loop(0, kdim // 8, body, tuple(tuple(zero for _ in range(nlt)) for _ in range(nb)), unroll=8)
    bias = b_ref[...]
    for b in range(nb):
        row = jnp.concatenate([jnp.sum(a, axis=0, keepdims=True) for a in accs[b]], axis=1)
        o_ref[b:b + 1, :] = row + bias


def _adaln(c, w, b):
    nb, d = c.shape
    n = w.shape[1]
    return pl.pallas_call(
        _adaln_kernel,
        grid=(n // ADA_TN,),
        in_specs=[pl.BlockSpec((d, nb), lambda j: (0, 0)),
                  pl.BlockSpec((d, ADA_TN), lambda j: (0, j)),
                  pl.BlockSpec((1, ADA_TN), lambda j: (0, j))],
        out_specs=pl.BlockSpec((nb, ADA_TN), lambda j: (0, j)),
        out_shape=jax.ShapeDtypeStruct((nb, n), F32),
        scratch_shapes=[pltpu.VMEM((nb, d, LANES), F32)],
        compiler_params=_cparams(("arbitrary",)),
        name="adaln",
    )(c.T, w, b.reshape(1, n))


def _modulated_norm(x, nw, sc, sh):
    ms = jnp.mean(x * x, axis=-1, keepdims=True)
    h = x * lax.rsqrt(ms + EPS) * nw
    return h * (1.0 + sc) + sh


def _prep_w_in_kernel(wt_ref, wn_ref, wg_ref, *, o_gate, o_gla, o_lr):
    tr = wt_ref.shape[1]
    step = PREP_STEP

    def put(dst, c0, rows):
        dst[:, c0:c0 + rows.shape[0]] = rows.T.astype(BF16)

    for c in range(0, o_gate, step):
        put(wn_ref, c, wt_ref[c:c + step, :])
    for c in range(0, o_lr - o_gla, step):
        put(wg_ref, c, wt_ref[o_gla + c:o_gla + c + step, :])
    pad = wg_ref.shape[1] - (wt_ref.shape[0] - o_gate)
    misc = jnp.concatenate([wt_ref[o_gate:o_gla, :], wt_ref[o_lr:, :], jnp.zeros((pad, tr), F32)], axis=0)
    put(wg_ref, o_lr - o_gla, misc)


def _prep_w_in(w_in_t, o_gate, o_gla, o_lr):
    n, d = w_in_t.shape
    tr = PREP_TR
    assert o_gate % PREP_STEP == 0 and (o_lr - o_gla) % PREP_STEP == 0
    return pl.pallas_call(
        functools.partial(_prep_w_in_kernel, o_gate=o_gate, o_gla=o_gla, o_lr=o_lr),
        grid=(d // tr,),
        in_specs=[pl.BlockSpec((n, tr), lambda i: (0, i))],
        out_specs=[pl.BlockSpec((tr, NSA_COLS), lambda i: (i, 0)), pl.BlockSpec((tr, GLA_COLS), lambda i: (i, 0))],
        out_shape=[jax.ShapeDtypeStruct((d, NSA_COLS), BF16), jax.ShapeDtypeStruct((d, GLA_COLS), BF16)],
        compiler_params=_cparams(("parallel",)),
        name="prep_w_in",
    )(w_in_t)


def _inproj_kernel(x_ref, sc_ref, sh_ref, nw_ref, wn_ref, wg_ref, on_ref, og_ref):
    h = _modulated_norm(x_ref[0], nw_ref[...], sc_ref[0, 0], sh_ref[0, 0]).astype(BF16)
    dh = NSA_HEAD_DIM
    tn = INPROJ_TN
    for c in range(wn_ref.shape[1] // tn):
        acc = jnp.dot(h, wn_ref[:, c * tn:(c + 1) * tn], preferred_element_type=F32)
        for u in range(tn // dh):
            on_ref[0, c * (tn // dh) + u] = acc[:, u * dh:(u + 1) * dh]
    og_ref[0] = jnp.dot(h, wg_ref[...], preferred_element_type=F32)


def _inproj(x, mod4, nw, w_nsa, w_gla, *, sc_idx, sh_idx):
    bsz, s, d = x.shape
    n_nsa, n_gla = w_nsa.shape[1], w_gla.shape[1]
    tm = INPROJ_TM
    dh = NSA_HEAD_DIM
    return pl.pallas_call(
        _inproj_kernel,
        grid=(bsz, s // tm),
        in_specs=[pl.BlockSpec((1, tm, d), lambda b, i: (b, i, 0)),
                  pl.BlockSpec((1, 1, 1, d), lambda b, i: (b, sc_idx, 0, 0)),
                  pl.BlockSpec((1, 1, 1, d), lambda b, i: (b, sh_idx, 0, 0)),
                  pl.BlockSpec((1, d), lambda b, i: (0, 0)),
                  pl.BlockSpec((d, n_nsa), lambda b, i: (0, 0), pipeline_mode=pl.Buffered(1)),
                  pl.BlockSpec((d, n_gla), lambda b, i: (0, 0), pipeline_mode=pl.Buffered(1))],
        out_specs=[pl.BlockSpec((1, n_nsa // dh, tm, dh), lambda b, i: (b, 0, i, 0)),
                   pl.BlockSpec((1, tm, n_gla), lambda b, i: (b, i, 0))],
        out_shape=[jax.ShapeDtypeStruct((bsz, n_nsa // dh, s, dh), F32),
                   jax.ShapeDtypeStruct((bsz, s, n_gla), F32)],
        compiler_params=_cparams(("parallel", "parallel")),
        name="inproj",
    )(x, mod4, mod4, nw, w_nsa, w_gla)


def _hi_lo(x):
    hi = x.astype(BF16)
    return hi, (x - hi.astype(F32)).astype(BF16)


def _dot_hi_lo(x, w):
    x_hi, x_lo = _hi_lo(x)
    w_hi, w_lo = _hi_lo(w)
    return jnp.dot(jnp.concatenate([x_hi, x_lo, x_hi], axis=1), jnp.concatenate([w_hi, w_hi, w_lo], axis=0),
                   preferred_element_type=F32)


def _compress_kernel(a_ref, pos_ref, w1_ref, w2_ref, o_ref):
    ng = a_ref.shape[1]
    nch = a_ref.shape[2] // CMP_STRIDE
    a = jnp.concatenate([
        jnp.concatenate([a_ref[0, g, pl.ds(t, nch, stride=CMP_STRIDE), :] for t in range(CMP_STRIDE)], axis=1)
        for g in range(ng)], axis=0)
    pos = pos_ref[0]
    half = a.shape[1]
    y1 = _dot_hi_lo(a + pos[0:1], w1_ref[0, :half, :])
    y2 = _dot_hi_lo(a + pos[1:2], w1_ref[0, half:, :])
    h = y1 + pltpu.roll(y2, a.shape[0] - 1, axis=0)
    out = _dot_hi_lo(jax.nn.gelu(h), w2_ref[0])
    row = lax.broadcasted_iota(I32, (nch, out.shape[1]), 0)
    for g in range(ng):
        o_ref[0, g] = jnp.where(row < nch - 1, out[g * nch:(g + 1) * nch], 0.0)


def _compress(proj_nsa, pos, w1, w2):
    bsz, _, s, dh = proj_nsa.shape
    ng = NSA_KV_HEADS
    nch = s // CMP_STRIDE
    cw = CMP_STRIDE * dh
    return pl.pallas_call(
        _compress_kernel,
        grid=(bsz, 2),
        in_specs=[pl.BlockSpec((1, ng, s, dh), lambda b, t: (b, NSA_HEADS // ng + t, 0, 0)),
                  pl.BlockSpec((1, 2, cw), lambda b, t: (t, 0, 0)),
                  pl.BlockSpec((1, 2 * cw, dh), lambda b, t: (t, 0, 0)),
                  pl.BlockSpec((1, dh, dh), lambda b, t: (t, 0, 0))],
        out_specs=pl.BlockSpec((1, ng, nch, dh), lambda b, t: (b, t, 0, 0)),
        out_shape=jax.ShapeDtypeStruct((bsz, 2 * ng, nch, dh), F32),
        compiler_params=_cparams(("parallel", "parallel")),
        name="nsa_compress",
    )(proj_nsa, pos, w1, w2)


def _nt_dot(a, b, **kw):
    return lax.dot_general(a, b, (((1,), (1,)), ((), ())), preferred_element_type=F32, **kw)


def _nsa_kernel(slopes_ref, q_ref, kc_ref, vc_ref, ks_ref, vs_ref, kw_ref, vw_ref, gate_ref, o_ref,
                ksb, vst, kwb, vwt, q4_scr, notsel_scr, *scr):
    g = pl.program_id(1)
    qi = pl.program_id(2)
    tq_n = q_ref.shape[2]
    dh = NSA_HEAD_DIM
    nr = NSA_Q_PER_KV
    seq = ks_ref.shape[2]
    tk_n = NSA_TK
    nb = seq // SEL_BLOCK

    @pl.when(qi == 0)
    def _():
        row = lax.broadcasted_iota(I32, (seq, dh), 0)
        lane = lax.broadcasted_iota(I32, (seq, dh), 1)
        blk = row // SEL_BLOCK
        pos = jnp.where((lane >= nb) & (lane < nb + 3), (blk * SEL_BLOCK).astype(F32),
                        jnp.where((lane >= nb + 3) & (lane < nb + 6), (row % SEL_BLOCK).astype(F32), 0.0))
        ksb[...] = jnp.concatenate([jnp.where(lane == blk, -MASKED, pos), ks_ref[0, 0]], axis=1).astype(BF16)
        kwb[...] = jnp.concatenate([pos, kw_ref[0, 0]], axis=1).astype(BF16)
        for c in range(seq // tk_n):
            rows = slice(c * tk_n, (c + 1) * tk_n)
            for src, dst in ((vs_ref, vst), (vw_ref, vwt)):
                v = src[0, 0, rows, :]
                dst[c] = jnp.concatenate([v, v], axis=1).T[:dh].astype(BF16)

    t0 = qi * tq_n
    tq = t0 + lax.broadcasted_iota(I32, (1, tq_n), 1)
    slopes = [slopes_ref[g * nr + r] for r in range(nr)]
    scale = dh ** -0.5
    q_t = []
    for pair in range(nr // 2):
        both = jnp.concatenate([q_ref[0, 2 * pair], q_ref[0, 2 * pair + 1]], axis=1).T * scale
        q_t += [both[:dh], both[dh:]]

    ncp = kc_ref.shape[2]
    kc = kc_ref[0, 0]
    vc = vc_ref[0, 0]
    vc_t = jnp.concatenate([vc, vc], axis=1).T[:dh].astype(BF16)
    n_sub = lax.broadcasted_iota(I32, (ncp, 1), 0)
    blk_end = n_sub * CMP_STRIDE + (CMP_BLOCK - 1)
    center = n_sub.astype(F32) * CMP_STRIDE + (CMP_BLOCK - 1) / 2.0
    kc_hi, kc_lo = _hi_lo(kc)
    kc_cat = jnp.concatenate([kc_hi, kc_lo, kc_hi, jnp.zeros_like(kc_hi)], axis=1)
    q_hi, q_lo = _hi_lo(jnp.concatenate(q_t, axis=1))
    q_cat = jnp.concatenate([q_hi, q_hi, q_lo, jnp.zeros_like(q_hi)], axis=0)
    s = jnp.dot(kc_cat, q_cat, preferred_element_type=F32)
    tq_all = jnp.concatenate([tq] * nr, axis=1)
    slope_all = jnp.concatenate([jnp.full((1, tq_n), slopes[r], F32) for r in range(nr)], axis=1)
    valid_all = blk_end <= tq_all
    s = jnp.where(valid_all, s - slope_all * (tq_all.astype(F32) - center), NEG)
    e = jnp.exp(s - jnp.max(s, axis=0, keepdims=True))
    p = jnp.where(valid_all, e / jnp.sum(e, axis=0, keepdims=True), 0.0)
    psum = sum(p[:, r * tq_n:(r + 1) * tq_n] for r in range(nr))
    o_c_all = jnp.dot(vc_t, p.astype(BF16), preferred_element_type=F32)
    o_c = [o_c_all[:, r * tq_n:(r + 1) * tq_n] for r in range(nr)]

    n_sel = min(N_SEL, nb)
    notsel_scr[...] = jnp.zeros(notsel_scr.shape, F32)

    @pl.when((qi + 1) * tq_n > n_sel * SEL_BLOCK)
    def _():
        rowj = lax.broadcasted_iota(I32, (LANES, ncp), 0) * SEL_BLOCK
        coln = lax.broadcasted_iota(I32, (LANES, ncp), 1) * CMP_STRIDE
        overlap = jnp.where((coln < rowj + SEL_BLOCK) & (coln + CMP_BLOCK > rowj)
                            & (coln < (ncp - 1) * CMP_STRIDE) & (rowj < nb * SEL_BLOCK), 1.0, 0.0)
        ov = overlap.astype(BF16)
        imp = jnp.dot(jnp.concatenate([ov, ov], axis=1), jnp.concatenate(_hi_lo(psum), axis=0),
                      preferred_element_type=F32)[:nb]
        j_sub = lax.broadcasted_iota(I32, (nb, 1), 0)
        qblk = tq // SEL_BLOCK
        forced = (j_sub == 0) | (j_sub == qblk) | (j_sub == qblk - 1)
        imp = jnp.where(forced, FORCE, jnp.where(j_sub <= qblk, imp, NEG))
        cnt = jnp.zeros((nb, tq_n), F32)
        for i in range(nb):
            ci = imp[i:i + 1, :]
            tie = jnp.where(j_sub > i, 1.0, 0.0)
            cnt = cnt + jnp.where(ci > imp, 1.0, jnp.where(ci == imp, tie, 0.0))
        notsel_scr[...] = jnp.where(cnt < float(n_sel), 0.0, 1.0)

    notsel = notsel_scr[...]

    sub_h = lax.broadcasted_iota(I32, (dh - nb, 1), 0)
    for r in range(nr):
        scol = jnp.zeros((dh - nb, 1), F32)
        for i in range(3):
            piece = slopes_ref[(i + 1) * NSA_HEADS + g * nr + r]
            scol = jnp.where((sub_h == i) | (sub_h == 3 + i), piece, scol)
        q4_scr[:, r * tq_n:(r + 1) * tq_n] = jnp.concatenate(
            [notsel, jnp.broadcast_to(scol, (dh - nb, tq_n)), q_t[r] * LOG2E], axis=0).astype(BF16)

    nwt = WINDOW // tk_n
    nbuf = nwt + 2
    stats_s, stats_w = scr[0:3], scr[3:6]
    s_buf, p_buf, a_buf = scr[6:6 + nbuf], scr[6 + nbuf:6 + 2 * nbuf], scr[6 + 2 * nbuf:6 + 3 * nbuf]
    for m_ref, l_ref, acc_ref in (stats_s, stats_w):
        m_ref[...] = jnp.full(m_ref.shape, NEG, F32)
        l_ref[...] = jnp.zeros(l_ref.shape, F32)
        acc_ref[...] = jnp.zeros(acc_ref.shape, F32)
    key_i = lax.broadcasted_iota(I32, (tk_n, LANES), 0)
    qry_j = lax.broadcasted_iota(I32, (tk_n, LANES), 1)

    def scores(k_ref, kt, buf):
        k_tile = k_ref[kt * tk_n:(kt + 1) * tk_n, :]
        s_buf[buf][...] = jnp.dot(k_tile, q4_scr[...], preferred_element_type=F32)

    def softmax(buf, mode, stats):
        m_ref, l_ref, _ = stats
        for cb in range(nr * tq_n // LANES):
            cols = slice(cb * LANES, (cb + 1) * LANES)
            s = s_buf[buf][:, cols]
            if mode is not None:
                j = qry_j + (cb * LANES) % tq_n
                s = jnp.where(key_i <= j if mode == "causal" else key_i > j, s, -MASKED)
            m_prev = m_ref[:, cols]
            m_new = jnp.maximum(m_prev, jnp.max(s, axis=0, keepdims=True))
            alpha = jnp.exp2(m_prev - m_new)
            p = jnp.exp2(s - m_new)
            l_ref[:, cols] = alpha * l_ref[:, cols] + jnp.sum(p, axis=0, keepdims=True)
            m_ref[:, cols] = m_new
            a_buf[buf][:, cols] = alpha
            p_buf[buf][:, cols] = p.astype(BF16)

    def values(vt_ref, kt, buf, stats):
        acc_ref = stats[2]
        pv = jnp.dot(vt_ref[kt], p_buf[buf][...], preferred_element_type=F32)
        acc_ref[...] = acc_ref[...] * a_buf[buf][...] + pv

    def batch(jobs):
        for i in range(min(nbuf, len(jobs))):
            scores(jobs[i][0], jobs[i][2], i)
        for i, (_, vt_ref, kt, mode, stats) in enumerate(jobs):
            softmax(i % nbuf, mode, stats)
            values(vt_ref, kt, i % nbuf, stats)
            if i + nbuf < len(jobs):
                scores(jobs[i + nbuf][0], jobs[i + nbuf][2], i % nbuf)

    for k in range(seq // tq_n):
        @pl.when(qi == k)
        def _(k=k):
            jobs = [(ksb, vst, t, None, stats_s) for t in range(k)]
            jobs += [(kwb, vwt, k - back, "band" if back == nwt else None, stats_w)
                     for back in range(min(nwt, k), 0, -1)]
            jobs += [(kwb, vwt, k, "causal", stats_w), (ksb, vst, k, "causal", stats_s)]
            batch(jobs)

    gsel = jnp.where(lax.broadcasted_iota(I32, (LANES, LANES), 0)
                     == lax.broadcasted_iota(I32, (LANES, LANES), 1) + g * (nr * N_BRANCH), 1.0, 0.0)
    gs = gsel.astype(BF16)
    gates = jax.nn.sigmoid(jnp.dot(jnp.concatenate(_hi_lo(gate_ref[0]), axis=1), jnp.concatenate([gs, gs], axis=0),
                                   preferred_element_type=F32)).T
    (_, l_s, acc_s), (_, l_w, acc_w) = stats_s, stats_w
    for pair in range(nr // 2):
        o_t = []
        for r in (2 * pair, 2 * pair + 1):
            c0 = r * N_BRANCH
            cols = slice(r * tq_n, (r + 1) * tq_n)
            o_t.append(gates[c0:c0 + 1, :] * o_c[r] + (gates[c0 + 1:c0 + 2, :] / l_s[:, cols]) * acc_s[:, cols]
                       + (gates[c0 + 2:c0 + 3, :] / l_w[:, cols]) * acc_w[:, cols])
        o_ref[0, :, pair * LANES:(pair + 1) * LANES] = jnp.concatenate(o_t, axis=0).T


def _nsa(slopes, proj_nsa, kvc, proj_gla):
    bsz, _, s, dh = proj_nsa.shape
    g_n, nr = NSA_KV_HEADS, NSA_Q_PER_KV
    tq = NSA_TQ
    tk = NSA_TK
    assert tq == tk and WINDOW % tk == 0 and 2 * dh == LANES
    ncp = kvc.shape[2]
    kv0 = NSA_HEADS
    nq = nr * tq
    nbuf = WINDOW // tk + 2

    def kv_spec(i):
        return pl.BlockSpec((1, 1, s, dh), lambda b, g, q, i=i: (b, kv0 + i * g_n + g, 0, 0))

    return pl.pallas_call(
        _nsa_kernel,
        grid=(bsz, g_n, s // tq),
        in_specs=[pl.BlockSpec(memory_space=pltpu.SMEM),
                  pl.BlockSpec((1, nr, tq, dh), lambda b, g, q: (b, g, q, 0)),
                  pl.BlockSpec((1, 1, ncp, dh), lambda b, g, q: (b, g, 0, 0)),
                  pl.BlockSpec((1, 1, ncp, dh), lambda b, g, q: (b, g_n + g, 0, 0)),
                  kv_spec(2), kv_spec(3), kv_spec(4), kv_spec(5),
                  pl.BlockSpec((1, tq, LANES), lambda b, g, q: (b, q, GLA_MISC_OFF // LANES))],
        out_specs=pl.BlockSpec((1, tq, nr * dh), lambda b, g, q: (b, q, g)),
        out_shape=jax.ShapeDtypeStruct((bsz, s, NSA_HEADS * dh), F32),
        scratch_shapes=[pltpu.VMEM((s, LANES), BF16), pltpu.VMEM((s // tk, dh, tk), BF16)] * 2
        + [pltpu.VMEM((LANES, nq), BF16), pltpu.VMEM((s // SEL_BLOCK, tq), F32)]
        + [pltpu.VMEM((1, nq), F32), pltpu.VMEM((1, nq), F32), pltpu.VMEM((dh, nq), F32)] * 2
        + [pltpu.VMEM((tk, nq), F32)] * nbuf + [pltpu.VMEM((tk, nq), BF16)] * nbuf + [pltpu.VMEM((1, nq), F32)] * nbuf,
        compiler_params=_cparams(("parallel", "parallel", "arbitrary")),
        name="nsa_attention",
    )(slopes, proj_nsa, kvc, kvc, proj_nsa, proj_nsa, proj_nsa, proj_nsa, proj_gla)


def _gla_kernel(q_ref, k_ref, v_ref, og_ref, lr_ref, wg_ref, bg_ref, nw_ref, o_ref, st_scr, la_scr, b_scr):
    rows_n = q_ref.shape[1]
    c_n, sub = GLA_CHUNK, GLA_SUB
    nh, dk, dv = GLA_HEADS, GLA_DK, GLA_DV

    @pl.when(pl.program_id(1) == 0)
    def _():
        st_scr[...] = jnp.zeros(st_scr.shape, F32)

    z = _dot_hi_lo(lr_ref[0], wg_ref[...]) + bg_ref[...]
    la_scr[...] = (jnp.minimum(z, 0.0) - jnp.log1p(jnp.exp(-jnp.abs(z)))) * (1.0 / GLA_GATE_NORM)
    tril = jnp.where(lax.broadcasted_iota(I32, (c_n, c_n), 0) >= lax.broadcasted_iota(I32, (c_n, c_n), 1), 1.0, 0.0)
    tril3 = jnp.concatenate([tril.astype(BF16)] * 3, axis=1)
    row_c = lax.broadcasted_iota(I32, (c_n, 1), 0)
    row_s = lax.broadcasted_iota(I32, (sub, 1), 0)
    lane_c = lax.broadcasted_iota(I32, (1, c_n), 1)
    nw = nw_ref[...]
    hk = [slice(h * dk, (h + 1) * dk) for h in range(nh)]
    hv = [slice(h * dv, (h + 1) * dv) for h in range(nh)]

    def chunk(c, carry):
        r0 = pl.multiple_of(c * c_n, c_n)
        rows = pl.ds(r0, c_n)
        qc = q_ref[0, rows, :] * (dk ** -0.5)
        kc = k_ref[0, rows, :]
        vc = [v_ref[0, rows, hv[h]].astype(BF16) for h in range(nh)]
        pieces, rest = [], la_scr[rows, :]
        for _ in range(3):
            pieces.append(rest.astype(BF16))
            rest = rest - pieces[-1].astype(F32)
        b = jnp.dot(tril3, jnp.concatenate(pieces, axis=0), preferred_element_type=F32)
        b_scr[...] = b
        st = [st_scr[h] for h in range(nh)]
        q_e = (qc * jnp.exp(b)).astype(BF16)
        o = [_nt_dot(q_e[:, hk[h]], st[h].astype(BF16)) for h in range(nh)]
        strips = [[] for _ in range(nh)]
        for blk in range(c_n // sub):
            lo = blk * sub
            q_i = qc[lo:lo + sub]
            b_i = b[lo:lo + sub]
            a = [jnp.zeros((sub, c_n), F32) for _ in range(nh)]
            if blk > 0:
                b_r = b_scr[lo - 1:lo, :]
                q_d = (q_i * jnp.exp(b_i - b_r)).astype(BF16)
                k_d = (kc * jnp.exp(jnp.where(row_c < lo, b_r - b, -jnp.inf))).astype(BF16)
                a = [_nt_dot(q_d[:, hk[h]], k_d[:, hk[h]]) for h in range(nh)]
            for j in range(sub):
                b_j = b_scr[lo + j:lo + j + 1, :]
                k_j = k_ref[0, pl.ds(r0 + lo + j, 1), :]
                prod = q_i * k_j * jnp.exp(jnp.where(row_s >= j, b_i - b_j, -jnp.inf))
                for h in range(nh):
                    col = jnp.sum(prod[:, hk[h]], axis=-1, keepdims=True)
                    a[h] = jnp.where(lane_c == lo + j, col, a[h]) if blk == 0 else (
                        a[h] + jnp.where(lane_c == lo + j, col, 0.0))
            for h in range(nh):
                strips[h].append(a[h])
        for h in range(nh):
            attn = jnp.concatenate(strips[h], axis=0)
            o[h] = o[h] + jnp.dot(attn.astype(BF16), vc[h], preferred_element_type=F32)
        b_last = b_scr[c_n - 1:c_n, :]
        k_dec = (kc * jnp.exp(b_last - b)).astype(BF16)
        decay = jnp.exp(b_last)
        for h in range(nh):
            st_scr[h] = st[h] * decay[:, hk[h]] + lax.dot_general(
                vc[h], k_dec[:, hk[h]], (((0,), (0,)), ((), ())), preferred_element_type=F32)
        for h in range(nh):
            og = og_ref[0, rows, hv[h]]
            on = o[h] * lax.rsqrt(jnp.mean(o[h] * o[h], axis=-1, keepdims=True) + EPS) * nw
            o_ref[0, rows, hv[h]] = on * (og * jax.nn.sigmoid(og))
        return carry

    lax.fori_loop(0, rows_n // c_n, chunk, 0)


def _gla(proj_gla, wg_pad, bg, nw):
    bsz, s, _ = proj_gla.shape
    nh, dk, dv = GLA_HEADS, GLA_DK, GLA_DV
    ts = GLA_TS
    wk, wv = nh * dk, nh * dv
    return pl.pallas_call(
        _gla_kernel,
        grid=(bsz, s // ts),
        in_specs=[pl.BlockSpec((1, ts, wk), lambda b, i: (b, i, GLA_Q_OFF // wk)),
                  pl.BlockSpec((1, ts, wk), lambda b, i: (b, i, GLA_K_OFF // wk)),
                  pl.BlockSpec((1, ts, wv), lambda b, i: (b, i, GLA_V_OFF // wv)),
                  pl.BlockSpec((1, ts, wv), lambda b, i: (b, i, GLA_OG_OFF // wv)),
                  pl.BlockSpec((1, ts, LANES), lambda b, i: (b, i, GLA_MISC_OFF // LANES)),
                  pl.BlockSpec((LANES, wk), lambda b, i: (0, 0)),
                  pl.BlockSpec((1, wk), lambda b, i: (0, 0)),
                  pl.BlockSpec((1, dv), lambda b, i: (0, 0))],
        out_specs=pl.BlockSpec((1, ts, wv), lambda b, i: (b, i, 0)),
        out_shape=jax.ShapeDtypeStruct((bsz, s, wv), F32),
        scratch_shapes=[pltpu.VMEM((nh, dv, dk), F32), pltpu.VMEM((ts, wk), F32), pltpu.VMEM((GLA_CHUNK, wk), F32)],
        compiler_params=_cparams(("parallel", "arbitrary")),
        name="gla",
    )(proj_gla, proj_gla, proj_gla, proj_gla, proj_gla, wg_pad, bg, nw)


def _outproj_kernel(nsa_ref, gla_ref, x_ref, wo_ref, g1_ref, sc_ref, sh_ref, nw_ref, wr_ref, br_ref,
                    x1_ref, h_ref, route_ref):
    half = nsa_ref.shape[2]
    acc = jnp.dot(nsa_ref[0].astype(BF16), wo_ref[:half, :], preferred_element_type=F32)
    acc = acc + jnp.dot(gla_ref[0].astype(BF16), wo_ref[half:, :], preferred_element_type=F32)
    x1 = x_ref[0] + g1_ref[0, 0] * acc
    x1_ref[0] = x1
    h = _modulated_norm(x1, nw_ref[...], sc_ref[0, 0], sh_ref[0, 0])
    h_ref[0] = h
    h_hi = h.astype(BF16)
    h_lo = (h - h_hi.astype(F32)).astype(BF16)
    t = jnp.dot(h_hi, wr_ref[...], preferred_element_type=F32)
    logits = (t[:, :LANES] + t[:, LANES:] + jnp.dot(h_lo, wr_ref[:, :LANES], preferred_element_type=F32)
              + br_ref[...])
    lane = lax.broadcasted_iota(I32, (1, LANES), 1)
    ninf = -jnp.inf
    is_g = (lane >= N_EXPERTS) & (lane < N_EXPERTS + N_GROUPS)
    gl = jnp.where(is_g, logits, ninf)
    ge = jnp.exp(gl - jnp.max(gl, axis=-1, keepdims=True))
    gp = ge / jnp.sum(ge, axis=-1, keepdims=True)
    gp_max = jnp.max(gp, axis=-1, keepdims=True)
    grp = jnp.min(jnp.where((gp == gp_max) & is_g, lane - N_EXPERTS, LANES), axis=-1, keepdims=True)
    in_grp = (lane // EXPERTS_PER_GROUP == grp) & (lane < N_EXPERTS)
    el = jnp.where(in_grp, logits, ninf)
    v1 = jnp.max(el, axis=-1, keepdims=True)
    i1 = jnp.min(jnp.where(el == v1, lane, LANES), axis=-1, keepdims=True)
    el2 = jnp.where(lane == i1, ninf, el)
    v2 = jnp.max(el2, axis=-1, keepdims=True)
    i2 = jnp.min(jnp.where(el2 == v2, lane, LANES), axis=-1, keepdims=True)
    e2 = jnp.exp(v2 - v1)
    den = 1.0 + e2
    w1 = gp_max * (1.0 / den)
    w2 = gp_max * (e2 / den)
    route_ref[0] = jnp.where(lane == 0, i1.astype(F32), jnp.where(lane == 1, i2.astype(F32), jnp.where(
        lane == 2, w1, jnp.where(lane == 3, w2, 0.0))))


def _outproj(o_nsa, o_gla, x, wo, mod4, nw, wr, br):
    bsz, s, d = x.shape
    tm = OUT_TM
    half = o_nsa.shape[2]

    def mod_spec(idx):
        return pl.BlockSpec((1, 1, 1, d), lambda b, i: (b, idx, 0, 0))

    row = lambda w: pl.BlockSpec((1, tm, w), lambda b, i: (b, i, 0))
    return pl.pallas_call(
        _outproj_kernel,
        grid=(bsz, s // tm),
        in_specs=[row(half), row(half), row(d),
                  pl.BlockSpec((2 * half, d), lambda b, i: (0, 0), pipeline_mode=pl.Buffered(1)),
                  mod_spec(2), mod_spec(4), mod_spec(3),
                  pl.BlockSpec((1, d), lambda b, i: (0, 0)),
                  pl.BlockSpec((d, 2 * LANES), lambda b, i: (0, 0)),
                  pl.BlockSpec((1, LANES), lambda b, i: (0, 0))],
        out_specs=[row(d), row(d), row(LANES)],
        out_shape=[jax.ShapeDtypeStruct((bsz, s, d), F32), jax.ShapeDtypeStruct((bsz, s, d), F32),
                   jax.ShapeDtypeStruct((bsz, s, LANES), F32)],
        compiler_params=_cparams(("parallel", "parallel")),
        name="outproj_router",
    )(o_nsa, o_gla, x, wo, mod4, mod4, mod4, nw, wr, br)


def _rank_kernel(route_ref, dest_ref, meta_ref, rank_scr):
    n = route_ref.shape[0]
    tm = RANK_TM
    lane_i = lax.broadcasted_iota(I32, (1, LANES), 1)
    lane = lane_i.astype(F32)
    strict = jnp.where(lax.broadcasted_iota(I32, (tm, tm), 0) > lax.broadcasted_iota(I32, (tm, tm), 1),
                       1.0, 0.0).astype(BF16)

    def two_lanes(a, b):
        return jnp.where(lane_i == 0, a, jnp.where(lane_i == 1, b, 0.0))

    def pick(e, table):
        return jnp.sum(jnp.where(lane == e, table, 0.0), axis=-1, keepdims=True)

    def count(i, seen):
        r0 = pl.multiple_of(i * tm, tm)
        rt = route_ref[pl.ds(r0, tm), :]
        e1, e2 = rt[:, 0:1], rt[:, 1:2]
        member = jnp.where(lane == e1, 1.0, jnp.where(lane == e2, 1.0, 0.0))
        before = jnp.dot(strict, member.astype(BF16), preferred_element_type=F32) + seen
        rank_scr[pl.ds(r0, tm), :] = two_lanes(pick(e1, before), pick(e2, before))
        return seen + jnp.sum(member, axis=0, keepdims=True)

    counts = lax.fori_loop(0, n // tm, count, jnp.zeros((1, LANES), F32))
    ntile = jnp.floor((counts + (MOE_TB - 1)) * (1.0 / MOE_TB))
    incl = jnp.where(lax.broadcasted_iota(I32, (LANES, LANES), 0) <= lax.broadcasted_iota(I32, (LANES, LANES), 1),
                     1.0, 0.0).astype(BF16)
    tile_end = jnp.dot(jnp.broadcast_to(ntile, (8, LANES)).astype(BF16), incl,
                       preferred_element_type=F32)[0:1]
    row_start = (tile_end - ntile) * MOE_TB

    def place(i, carry):
        r0 = pl.multiple_of(i * tm, tm)
        rt = route_ref[pl.ds(r0, tm), :]
        rk = rank_scr[pl.ds(r0, tm), :]
        d1 = pick(rt[:, 0:1], row_start) + rk[:, 0:1]
        d2 = pick(rt[:, 1:2], row_start) + rk[:, 1:2]
        dest_ref[pl.ds(r0, tm), :] = two_lanes(d1, d2).astype(I32)
        return carry

    lax.fori_loop(0, n // tm, place, 0)
    trow = lax.broadcasted_iota(I32, (meta_ref.shape[0], 1), 0).astype(F32)
    texp = jnp.sum(jnp.where((tile_end <= trow) & (lane_i < N_EXPERTS), 1.0, 0.0), axis=-1, keepdims=True)
    texp = jnp.minimum(texp, N_EXPERTS - 1.0)
    used = pick(N_EXPERTS - 1.0, tile_end)
    diag = lax.broadcasted_iota(I32, (meta_ref.shape[0], LANES), 0) == lane_i
    end_rows = jnp.sum(jnp.where(diag, tile_end, 0.0), axis=-1, keepdims=True)
    ntile_rows = jnp.sum(jnp.where(diag, ntile, 0.0), axis=-1, keepdims=True)
    meta_ref[...] = jnp.where(lane_i == 2, end_rows, jnp.where(lane_i == 3, ntile_rows, two_lanes(
        texp, jnp.broadcast_to(used, texp.shape)))).astype(I32)


def _rank(route):
    n = route.shape[0]
    return pl.pallas_call(
        _rank_kernel,
        out_shape=[jax.ShapeDtypeStruct((n, LANES), I32), jax.ShapeDtypeStruct((LANES, LANES), I32)],
        scratch_shapes=[pltpu.VMEM((n, LANES), F32)],
        compiler_params=pltpu.CompilerParams(vmem_limit_bytes=VMEM_LIMIT),
        name="moe_rank",
    )(route)


def _dispatch_kernel(dest_ref, ends_ref, h_ref, xs_ref, zero_scr, sem, zsem):
    i = pl.program_id(0)
    tm = h_ref.shape[0]
    tb = zero_scr.shape[0]

    @pl.when(i == 0)
    def _():
        zero_scr[...] = jnp.zeros(zero_scr.shape, F32)

        def zero_copy(e):
            r0 = pl.multiple_of((ends_ref[e] - 1) * tb, tb)
            return pltpu.make_async_copy(zero_scr, xs_ref.at[pl.ds(r0, tb)], zsem)

        def start(e, carry):
            @pl.when(ends_ref[N_EXPERTS + e] > 0)
            def _():
                zero_copy(e).start()
            return carry

        def wait(e, carry):
            @pl.when(ends_ref[N_EXPERTS + e] > 0)
            def _():
                zero_copy(e).wait()
            return carry

        def tail_copy(t):
            return pltpu.make_async_copy(zero_scr, xs_ref.at[pl.ds(pl.multiple_of(t * tb, tb), tb)], zsem)

        def tail_start(t, carry):
            tail_copy(t).start()
            return carry

        def tail_wait(t, carry):
            tail_copy(t).wait()
            return carry

        used = ends_ref[2 * N_EXPERTS]
        lax.fori_loop(0, N_EXPERTS, start, 0)
        lax.fori_loop(used, xs_ref.shape[0] // tb, tail_start, 0)
        lax.fori_loop(0, N_EXPERTS, wait, 0)
        lax.fori_loop(used, xs_ref.shape[0] // tb, tail_wait, 0)

    def issue(r, carry):
        p = (i * tm + r) * TOP_K
        for k in range(TOP_K):
            pltpu.make_async_copy(h_ref.at[pl.ds(r, 1)], xs_ref.at[pl.ds(dest_ref[p + k], 1)], sem).start()
        return carry

    lax.fori_loop(0, tm, issue, 0, unroll=DMA_UNROLL)
    for k in range(TOP_K):
        pltpu.make_async_copy(h_ref, xs_ref.at[pl.ds(0, tm)], sem).wait()


def _dispatch(dest_flat, ends_flat, h, cap):
    n, d = h.shape
    tm = DISPATCH_TM
    return pl.pallas_call(
        _dispatch_kernel,
        grid_spec=pltpu.PrefetchScalarGridSpec(
            num_scalar_prefetch=2, grid=(n // tm,),
            in_specs=[pl.BlockSpec((tm, d), lambda i, dst, ends: (i, 0))],
            out_specs=pl.BlockSpec(memory_space=pl.ANY),
            scratch_shapes=[pltpu.VMEM((MOE_TB, d), F32), pltpu.SemaphoreType.DMA(()), pltpu.SemaphoreType.DMA(())]),
        out_shape=jax.ShapeDtypeStruct((cap, d), F32),
        compiler_params=_cparams(("arbitrary",)),
        name="moe_dispatch",
    )(dest_flat, ends_flat, h)


def _ffn_kernel(meta_ref, x_ref, wg_hbm, wu_hbm, wd_hbm, y_ref, wg_st, wu_st, wd_st, wgb, wub, wdb, wsem):
    t = pl.program_id(0)
    ntile = pl.num_programs(0)
    n_used = meta_ref[ntile]
    e = meta_ref[t]
    e_prev = meta_ref[jnp.maximum(t - 1, 0)]
    active = t < n_used

    def fetch(expert):
        return [pltpu.make_async_copy(src.at[expert], dst, wsem)
                for src, dst in ((wg_hbm, wg_st), (wu_hbm, wu_st), (wd_hbm, wd_st))]

    @pl.when(t == 0)
    def _():
        for cp in fetch(e):
            cp.start()

    @pl.when(active & ((t == 0) | (e != e_prev)))
    def _():
        for cp in fetch(e):
            cp.wait()
        wgb[...] = wg_st[...].astype(BF16)
        wub[...] = wu_st[...].astype(BF16)
        wdb[...] = wd_st[...].astype(BF16)
        end = meta_ref[ntile + 1 + e]

        @pl.when(end < n_used)
        def _():
            for cp in fetch(meta_ref[end]):
                cp.start()

    @pl.when(active)
    def _():
        x = x_ref[...].astype(BF16)
        gate = jnp.dot(x, wgb[...], preferred_element_type=F32)
        up = jnp.dot(x, wub[...], preferred_element_type=F32)
        act = (gate * jax.nn.sigmoid(gate)) * up
        y_ref[...] = jnp.dot(act.astype(BF16), wdb[...], preferred_element_type=F32)

    @pl.when(jnp.logical_not(active))
    def _():
        y_ref[...] = jnp.zeros(y_ref.shape, F32)


def _ffn(meta_flat, xs, wg, wu, wd):
    cap, d = xs.shape
    ff = wg.shape[2]
    tb = MOE_TB
    ntile = cap // tb
    return pl.pallas_call(
        _ffn_kernel,
        grid_spec=pltpu.PrefetchScalarGridSpec(
            num_scalar_prefetch=1, grid=(ntile,),
            in_specs=[pl.BlockSpec((tb, d), lambda t, m: (jnp.minimum(t, m[ntile] - 1), 0)),
                      pl.BlockSpec(memory_space=pl.ANY), pl.BlockSpec(memory_space=pl.ANY),
                      pl.BlockSpec(memory_space=pl.ANY)],
            out_specs=pl.BlockSpec((tb, d), lambda t, m: (t, 0)),
            scratch_shapes=[pltpu.VMEM((d, ff), F32), pltpu.VMEM((d, ff), F32), pltpu.VMEM((ff, d), F32),
                            pltpu.VMEM((d, ff), BF16), pltpu.VMEM((d, ff), BF16), pltpu.VMEM((ff, d), BF16),
                            pltpu.SemaphoreType.DMA(())]),
        out_shape=jax.ShapeDtypeStruct((cap, d), F32),
        compiler_params=_cparams(("arbitrary",)),
        name="moe_ffn",
    )(meta_flat, xs, wg, wu, wd)


def _combine_kernel(dest_ref, y_ref, x1_ref, route_ref, g2_ref, nf_ref, o_ref, ybuf, sem):
    i = pl.program_id(0)
    nstep = pl.num_programs(0)
    tm = x1_ref.shape[0]

    def issue(tile, slot):
        def body(r, carry):
            p = (tile * tm + r) * TOP_K
            for k in range(TOP_K):
                pltpu.make_async_copy(y_ref.at[pl.ds(dest_ref[p + k], 1)], ybuf.at[slot, k, pl.ds(r, 1)],
                                      sem.at[slot]).start()
            return carry

        lax.fori_loop(0, tm, body, 0, unroll=DMA_UNROLL)

    @pl.when(i == 0)
    def _():
        issue(0, 0)

    @pl.when(i + 1 < nstep)
    def _():
        issue(i + 1, (i + 1) % 2)

    slot = i % 2
    for k in range(TOP_K):
        pltpu.make_async_copy(y_ref.at[pl.ds(0, tm)], ybuf.at[slot, k], sem.at[slot]).wait()
    rt = route_ref[...]
    moe = rt[:, 2:3] * ybuf[slot, 0] + rt[:, 3:4] * ybuf[slot, 1]
    xo = x1_ref[...] + g2_ref[0, 0] * moe
    o_ref[...] = xo * lax.rsqrt(jnp.mean(xo * xo, axis=-1, keepdims=True) + EPS) * nf_ref[...]


def _combine(dest_flat, y, x1, route, mod4, nf, seq):
    n, d = x1.shape
    tm = COMB_TM
    tiles_per_seq = seq // tm
    return pl.pallas_call(
        _combine_kernel,
        grid_spec=pltpu.PrefetchScalarGridSpec(
            num_scalar_prefetch=1, grid=(n // tm,),
            in_specs=[pl.BlockSpec(memory_space=pl.ANY),
                      pl.BlockSpec((tm, d), lambda i, dst: (i, 0)),
                      pl.BlockSpec((tm, LANES), lambda i, dst: (i, 0)),
                      pl.BlockSpec((1, 1, 1, d), lambda i, dst: (i // tiles_per_seq, 5, 0, 0)),
                      pl.BlockSpec((1, d), lambda i, dst: (0, 0))],
            out_specs=pl.BlockSpec((tm, d), lambda i, dst: (i, 0)),
            scratch_shapes=[pltpu.VMEM((2, TOP_K, tm, d), F32), pltpu.SemaphoreType.DMA((2,))]),
        out_shape=jax.ShapeDtypeStruct((n, d), F32),
        compiler_params=_cparams(("arbitrary",)),
        name="moe_combine",
    )(dest_flat, y, x1, route, mod4, nf)


def _alibi_slopes():
    n = NSA_HEADS
    full = jnp.asarray(2.0 ** (-8.0 * np.arange(1, n + 1) / n), dtype=F32)
    pieces, rest = [], full * LOG2E
    for _ in range(3):
        piece = rest.astype(BF16).astype(F32)
        pieces.append(piece)
        rest = rest - piece
    return jnp.concatenate([full] + pieces)


def _layer(x, c, w_ada, b_ada, norm1_w, w_in, cmp_pos_k, cmp_w1_k, cmp_w2_k, cmp_pos_v, cmp_w1_v, cmp_w2_v,
           gla_w_gate2, gla_b_gate, gla_norm_w, w_out, norm2_w, w_rg, b_rg, w_re, b_re, w_eg, w_eu, w_ed):
    bsz, s, d = x.shape
    dh = NSA_HEAD_DIM
    mod4 = _adaln(c, w_ada, b_ada).reshape(bsz, 6, 1, d)

    o_gate = NSA_COLS
    o_gla = o_gate + NSA_GATE_COLS
    o_lr = o_gla + 2 * GLA_HEADS * GLA_DK + 2 * GLA_HEADS * GLA_DV
    w_nsa, w_gla = _prep_w_in(w_in.T, o_gate, o_gla, o_lr)
    nw1 = norm1_w.reshape(1, d)
    proj_nsa, proj_gla = _inproj(x, mod4, nw1, w_nsa, w_gla, sc_idx=1, sh_idx=0)

    pos = jnp.stack([cmp_pos_k, cmp_pos_v]).reshape(2, 2, CMP_STRIDE * dh)
    kvc = _compress(proj_nsa, pos, jnp.stack([cmp_w1_k, cmp_w1_v]), jnp.stack([cmp_w2_k, cmp_w2_v]))
    o_nsa = _nsa(_alibi_slopes(), proj_nsa, kvc, proj_gla)

    wg_pad = jnp.zeros((LANES, GLA_HEADS * GLA_DK), F32).at[
        NSA_GATE_COLS:NSA_GATE_COLS + GLA_GATE_RANK].set(gla_w_gate2)
    o_gla_out = _gla(proj_gla, wg_pad, gla_b_gate.reshape(1, -1), gla_norm_w.reshape(1, -1))

    wr = jnp.concatenate([w_re, w_rg, jnp.zeros((d, LANES - N_EXPERTS - N_GROUPS), F32)], axis=1)
    br = jnp.concatenate([b_re, b_rg, jnp.zeros((LANES - N_EXPERTS - N_GROUPS,), F32)]).reshape(1, LANES)
    wr_hi = wr.astype(BF16)
    wr_cat = jnp.concatenate([wr_hi, (wr - wr_hi.astype(F32)).astype(BF16)], axis=1)
    x1, h2, route = _outproj(o_nsa, o_gla_out, x, w_out.astype(BF16), mod4, norm2_w.reshape(1, d), wr_cat, br)

    n = bsz * s
    npair = n * TOP_K
    cap = npair + N_EXPERTS * MOE_TB
    ntile = cap // MOE_TB
    route2 = route.reshape(n, LANES)
    dest, meta = _rank(route2)
    dest_flat = dest[:, :TOP_K].reshape(npair)
    meta_flat = jnp.concatenate([meta[:ntile, 0], meta[:1, 1], meta[:N_EXPERTS, 2]])
    ends_flat = jnp.concatenate([meta[:N_EXPERTS, 2], meta[:N_EXPERTS, 3], meta[:1, 1]])
    xs = _dispatch(dest_flat, ends_flat, h2.reshape(n, d), cap)
    y = _ffn(meta_flat, xs, w_eg, w_eu, w_ed)
    return x1.reshape(n, d), y, dest_flat, route2, mod4


def kernel(x, c, w_ada, b_ada, norm1_w, w_in, cmp_pos_k, cmp_w1_k, cmp_w2_k, cmp_pos_v, cmp_w1_v, cmp_w2_v,
           gla_w_gate2, gla_b_gate, gla_norm_w, w_out, norm2_w, w_router_group, b_router_group, w_router_expert,
           b_router_expert, w_expert_gate, w_expert_up, w_expert_down, norm_f_w):
    bsz, s, d = x.shape
    assert w_ada.shape[0] == 1, "single layer"
    x1, y, dest_flat, route2, mod4 = _layer(
        x, c, w_ada[0], b_ada[0], norm1_w[0], w_in[0], cmp_pos_k[0], cmp_w1_k[0], cmp_w2_k[0], cmp_pos_v[0],
        cmp_w1_v[0], cmp_w2_v[0], gla_w_gate2[0], gla_b_gate[0], gla_norm_w[0], w_out[0], norm2_w[0],
        w_router_group[0], b_router_group[0], w_router_expert[0], b_router_expert[0],
        w_expert_gate[0], w_expert_up[0], w_expert_down[0])
    out = _combine(dest_flat, y, x1, route2, mod4, norm_f_w.reshape(1, d), s)
    return out.reshape(bsz, s, d)
```

```python
import functools

import numpy as np
import jax
import jax.numpy as jnp
from jax import lax
from jax.experimental import pallas as pl
from jax.experimental.pallas import tpu as pltpu

F32 = jnp.float32
BF16 = jnp.bfloat16
I32 = jnp.int32

D_MODEL = 2048
NSA_HEAD_DIM = 64
NSA_HEADS = 16
NSA_KV_HEADS = 4
NSA_Q_PER_KV = 4
CMP_BLOCK = 32
CMP_STRIDE = 16
SEL_BLOCK = 64
N_SEL = 16
WINDOW = 512
N_BRANCH = 3
GLA_HEADS = 4
GLA_DV = 256
GLA_DK = 128
GLA_GATE_RANK = 16
GLA_GATE_NORM = 16.0
GLA_CHUNK = 64
GLA_SUB = 8
N_GROUPS = 4
EXPERTS_PER_GROUP = 8
N_EXPERTS = 32
TOP_K = 2
EXPERT_FF = 512
EPS = 1e-6
NEG = -1e30
FORCE = 1e30
LOG2E = 1.4426950408889634
MASKED = 2.0 ** 100

NSA_Q_COLS = NSA_HEADS * NSA_HEAD_DIM
NSA_KV_COLS = 2 * N_BRANCH * NSA_KV_HEADS * NSA_HEAD_DIM
NSA_GATE_COLS = N_BRANCH * NSA_HEADS
NSA_COLS = NSA_Q_COLS + NSA_KV_COLS
NSA_SLOTS = NSA_COLS // NSA_HEAD_DIM
GLA_Q_OFF = 0
GLA_K_OFF = GLA_HEADS * GLA_DK
GLA_V_OFF = 2 * GLA_HEADS * GLA_DK
GLA_OG_OFF = GLA_V_OFF + GLA_HEADS * GLA_DV
GLA_MISC_OFF = GLA_OG_OFF + GLA_HEADS * GLA_DV
LANES = 128
GLA_COLS = GLA_MISC_OFF + LANES

VMEM_LIMIT = 56 * 1024 * 1024

ADA_TN = 768
PREP_TR = 256
PREP_STEP = 512
INPROJ_TM = 256
INPROJ_TN = 512
NSA_TQ = 256
NSA_TK = 256
GLA_TS = 1024
OUT_TM = 512
RANK_TM = 256
MOE_TB = 256
DISPATCH_TM = 1024
COMB_TM = 512
DMA_UNROLL = 8


def _cparams(sem):
    return pltpu.CompilerParams(dimension_semantics=sem, vmem_limit_bytes=VMEM_LIMIT)


def _adaln_kernel(ct_ref, w_ref, b_ref, o_ref, s_scr):
    nb = ct_ref.shape[1]
    kdim, tn = w_ref.shape

    @pl.when(pl.program_id(0) == 0)
    def _():
        ct = ct_ref[...]
        s = ct * jax.nn.sigmoid(ct)
        for b in range(nb):
            s_scr[b] = jnp.broadcast_to(s[:, b:b + 1], (kdim, LANES))

    nlt = tn // LANES

    def body(k, accs):
        r = pl.multiple_of(k * 8, 8)
        w8 = [w_ref[pl.ds(r, 8), j * LANES:(j + 1) * LANES] for j in range(nlt)]
        out = []
        for b in range(nb):
            s8 = s_scr[b, pl.ds(r, 8), :]
            out.append(tuple(accs[b][j] + w8[j] * s8 for j in range(nlt)))
        return tuple(out)

    zero = jnp.zeros((8, LANES), F32)
    accs = lax.fori_loop(0, kdim // 8, body, tuple(tuple(zero for _ in range(nlt)) for _ in range(nb)), unroll=8)
    bias = b_ref[...]
    for b in range(nb):
        row = jnp.concatenate([jnp.sum(a, axis=0, keepdims=True) for a in accs[b]], axis=1)
        o_ref[b:b + 1, :] = row + bias


def _adaln(c, w, b):
    nb, d = c.shape
    n = w.shape[1]
    return pl.pallas_call(
        _adaln_kernel,
        grid=(n // ADA_TN,),
        in_specs=[pl.BlockSpec((d, nb), lambda j: (0, 0)),
                  pl.BlockSpec((d, ADA_TN), lambda j: (0, j)),
                  pl.BlockSpec((1, ADA_TN), lambda j: (0, j))],
        out_specs=pl.BlockSpec((nb, ADA_TN), lambda j: (0, j)),
        out_shape=jax.ShapeDtypeStruct((nb, n), F32),
        scratch_shapes=[pltpu.VMEM((nb, d, LANES), F32)],
        compiler_params=_cparams(("arbitrary",)),
        name="adaln",
    )(c.T, w, b.reshape(1, n))


def _modulated_norm(x, nw, sc, sh):
    ms = jnp.mean(x * x, axis=-1, keepdims=True)
    h = x * lax.rsqrt(ms + EPS) * nw
    return h * (1.0 + sc) + sh


def _prep_w_in_kernel(wt_ref, wn_ref, wg_ref, *, o_gate, o_gla, o_lr):
    tr = wt_ref.shape[1]
    step = PREP_STEP

    def put(dst, c0, rows):
        dst[:, c0:c0 + rows.shape[0]] = rows.T.astype(BF16)

    for c in range(0, o_gate, step):
        put(wn_ref, c, wt_ref[c:c + step, :])
    for c in range(0, o_lr - o_gla, step):
        put(wg_ref, c, wt_ref[o_gla + c:o_gla + c + step, :])
    pad = wg_ref.shape[1] - (wt_ref.shape[0] - o_gate)
    misc = jnp.concatenate([wt_ref[o_gate:o_gla, :], wt_ref[o_lr:, :], jnp.zeros((pad, tr), F32)], axis=0)
    put(wg_ref, o_lr - o_gla, misc)


def _prep_w_in(w_in_t, o_gate, o_gla, o_lr):
    n, d = w_in_t.shape
    tr = PREP_TR
    assert o_gate % PREP_STEP == 0 and (o_lr - o_gla) % PREP_STEP == 0
    return pl.pallas_call(
        functools.partial(_prep_w_in_kernel, o_gate=o_gate, o_gla=o_gla, o_lr=o_lr),
        grid=(d // tr,),
        in_specs=[pl.BlockSpec((n, tr), lambda i: (0, i))],
        out_specs=[pl.BlockSpec((tr, NSA_COLS), lambda i: (i, 0)), pl.BlockSpec((tr, GLA_COLS), lambda i: (i, 0))],
        out_shape=[jax.ShapeDtypeStruct((d, NSA_COLS), BF16), jax.ShapeDtypeStruct((d, GLA_COLS), BF16)],
        compiler_params=_cparams(("parallel",)),
        name="prep_w_in",
    )(w_in_t)


def _inproj_kernel(x_ref, sc_ref, sh_ref, nw_ref, wn_ref, wg_ref, on_ref, og_ref):
    h = _modulated_norm(x_ref[0], nw_ref[...], sc_ref[0, 0], sh_ref[0, 0]).astype(BF16)
    dh = NSA_HEAD_DIM
    tn = INPROJ_TN
    for c in range(wn_ref.shape[1] // tn):
        acc = jnp.dot(h, wn_ref[:, c * tn:(c + 1) * tn], preferred_element_type=F32)
        for u in range(tn // dh):
            on_ref[0, c * (tn // dh) + u] = acc[:, u * dh:(u + 1) * dh]
    og_ref[0] = jnp.dot(h, wg_ref[...], preferred_element_type=F32)


def _inproj(x, mod4, nw, w_nsa, w_gla, *, sc_idx, sh_idx):
    bsz, s, d = x.shape
    n_nsa, n_gla = w_nsa.shape[1], w_gla.shape[1]
    tm = INPROJ_TM
    dh = NSA_HEAD_DIM
    return pl.pallas_call(
        _inproj_kernel,
        grid=(bsz, s // tm),
        in_specs=[pl.BlockSpec((1, tm, d), lambda b, i: (b, i, 0)),
                  pl.BlockSpec((1, 1, 1, d), lambda b, i: (b, sc_idx, 0, 0)),
                  pl.BlockSpec((1, 1, 1, d), lambda b, i: (b, sh_idx, 0, 0)),
                  pl.BlockSpec((1, d), lambda b, i: (0, 0)),
                  pl.BlockSpec((d, n_nsa), lambda b, i: (0, 0), pipeline_mode=pl.Buffered(1)),
                  pl.BlockSpec((d, n_gla), lambda b, i: (0, 0), pipeline_mode=pl.Buffered(1))],
        out_specs=[pl.BlockSpec((1, n_nsa // dh, tm, dh), lambda b, i: (b, 0, i, 0)),
                   pl.BlockSpec((1, tm, n_gla), lambda b, i: (b, i, 0))],
        out_shape=[jax.ShapeDtypeStruct((bsz, n_nsa // dh, s, dh), F32),
                   jax.ShapeDtypeStruct((bsz, s, n_gla), F32)],
        compiler_params=_cparams(("parallel", "parallel")),
        name="inproj",
    )(x, mod4, mod4, nw, w_nsa, w_gla)


def _hi_lo(x):
    hi = x.astype(BF16)
    return hi, (x - hi.astype(F32)).astype(BF16)


def _dot_hi_lo(x, w):
    x_hi, x_lo = _hi_lo(x)
    w_hi, w_lo = _hi_lo(w)
    return jnp.dot(jnp.concatenate([x_hi, x_lo, x_hi], axis=1), jnp.concatenate([w_hi, w_hi, w_lo], axis=0),
                   preferred_element_type=F32)


def _compress_kernel(a_ref, pos_ref, w1_ref, w2_ref, o_ref):
    ng = a_ref.shape[1]
    nch = a_ref.shape[2] // CMP_STRIDE
    a = jnp.concatenate([
        jnp.concatenate([a_ref[0, g, pl.ds(t, nch, stride=CMP_STRIDE), :] for t in range(CMP_STRIDE)], axis=1)
        for g in range(ng)], axis=0)
    pos = pos_ref[0]
    half = a.shape[1]
    y1 = _dot_hi_lo(a + pos[0:1], w1_ref[0, :half, :])
    y2 = _dot_hi_lo(a + pos[1:2], w1_ref[0, half:, :])
    h = y1 + pltpu.roll(y2, a.shape[0] - 1, axis=0)
    out = _dot_hi_lo(jax.nn.gelu(h), w2_ref[0])
    row = lax.broadcasted_iota(I32, (nch, out.shape[1]), 0)
    for g in range(ng):
        o_ref[0, g] = jnp.where(row < nch - 1, out[g * nch:(g + 1) * nch], 0.0)


def _compress(proj_nsa, pos, w1, w2):
    bsz, _, s, dh = proj_nsa.shape
    ng = NSA_KV_HEADS
    nch = s // CMP_STRIDE
    cw = CMP_STRIDE * dh
    return pl.pallas_call(
        _compress_kernel,
        grid=(bsz, 2),
        in_specs=[pl.BlockSpec((1, ng, s, dh), lambda b, t: (b, NSA_HEADS // ng + t, 0, 0)),
                  pl.BlockSpec((1, 2, cw), lambda b, t: (t, 0, 0)),
                  pl.BlockSpec((1, 2 * cw, dh), lambda b, t: (t, 0, 0)),
                  pl.BlockSpec((1, dh, dh), lambda b, t: (t, 0, 0))],
        out_specs=pl.BlockSpec((1, ng, nch, dh), lambda b, t: (b, t, 0, 0)),
        out_shape=jax.ShapeDtypeStruct((bsz, 2 * ng, nch, dh), F32),
        compiler_params=_cparams(("parallel", "parallel")),
        name="nsa_compress",
    )(proj_nsa, pos, w1, w2)


def _nt_dot(a, b, **kw):
    return lax.dot_general(a, b, (((1,), (1,)), ((), ())), preferred_element_type=F32, **kw)


def _nsa_kernel(slopes_ref, q_ref, kc_ref, vc_ref, ks_ref, vs_ref, kw_ref, vw_ref, gate_ref, o_ref,
                ksb, vst, kwb, vwt, q4_scr, notsel_scr, *scr):
    g = pl.program_id(1)
    qi = pl.program_id(2)
    tq_n = q_ref.shape[2]
    dh = NSA_HEAD_DIM
    nr = NSA_Q_PER_KV
    seq = ks_ref.shape[2]
    tk_n = NSA_TK
    nb = seq // SEL_BLOCK

    @pl.when(qi == 0)
    def _():
        row = lax.broadcasted_iota(I32, (seq, dh), 0)
        lane = lax.broadcasted_iota(I32, (seq, dh), 1)
        blk = row // SEL_BLOCK
        pos = jnp.where((lane >= nb) & (lane < nb + 3), (blk * SEL_BLOCK).astype(F32),
                        jnp.where((lane >= nb + 3) & (lane < nb + 6), (row % SEL_BLOCK).astype(F32), 0.0))
        ksb[...] = jnp.concatenate([jnp.where(lane == blk, -MASKED, pos), ks_ref[0, 0]], axis=1).astype(BF16)
        kwb[...] = jnp.concatenate([pos, kw_ref[0, 0]], axis=1).astype(BF16)
        for c in range(seq // tk_n):
            rows = slice(c * tk_n, (c + 1) * tk_n)
            for src, dst in ((vs_ref, vst), (vw_ref, vwt)):
                v = src[0, 0, rows, :]
                dst[c] = jnp.concatenate([v, v], axis=1).T[:dh].astype(BF16)

    t0 = qi * tq_n
    tq = t0 + lax.broadcasted_iota(I32, (1, tq_n), 1)
    slopes = [slopes_ref[g * nr + r] for r in range(nr)]
    scale = dh ** -0.5
    q_t = []
    for pair in range(nr // 2):
        both = jnp.concatenate([q_ref[0, 2 * pair], q_ref[0, 2 * pair + 1]], axis=1).T * scale
        q_t += [both[:dh], both[dh:]]

    ncp = kc_ref.shape[2]
    kc = kc_ref[0, 0]
    vc = vc_ref[0, 0]
    vc_t = jnp.concatenate([vc, vc], axis=1).T[:dh].astype(BF16)
    n_sub = lax.broadcasted_iota(I32, (ncp, 1), 0)
    blk_end = n_sub * CMP_STRIDE + (CMP_BLOCK - 1)
    center = n_sub.astype(F32) * CMP_STRIDE + (CMP_BLOCK - 1) / 2.0
    kc_hi, kc_lo = _hi_lo(kc)
    kc_cat = jnp.concatenate([kc_hi, kc_lo, kc_hi, jnp.zeros_like(kc_hi)], axis=1)
    q_hi, q_lo = _hi_lo(jnp.concatenate(q_t, axis=1))
    q_cat = jnp.concatenate([q_hi, q_hi, q_lo, jnp.zeros_like(q_hi)], axis=0)
    s = jnp.dot(kc_cat, q_cat, preferred_element_type=F32)
    tq_all = jnp.concatenate([tq] * nr, axis=1)
    slope_all = jnp.concatenate([jnp.full((1, tq_n), slopes[r], F32) for r in range(nr)], axis=1)
    valid_all = blk_end <= tq_all
    s = jnp.where(valid_all, s - slope_all * (tq_all.astype(F32) - center), NEG)
    e = jnp.exp(s - jnp.max(s, axis=0, keepdims=True))
    p = jnp.where(valid_all, e / jnp.sum(e, axis=0, keepdims=True), 0.0)
    psum = sum(p[:, r * tq_n:(r + 1) * tq_n] for r in range(nr))
    o_c_all = jnp.dot(vc_t, p.astype(BF16), preferred_element_type=F32)
    o_c = [o_c_all[:, r * tq_n:(r + 1) * tq_n] for r in range(nr)]

    n_sel = min(N_SEL, nb)
    notsel_scr[...] = jnp.zeros(notsel_scr.shape, F32)

    @pl.when((qi + 1) * tq_n > n_sel * SEL_BLOCK)
    def _():
        rowj = lax.broadcasted_iota(I32, (LANES, ncp), 0) * SEL_BLOCK
        coln = lax.broadcasted_iota(I32, (LANES, ncp), 1) * CMP_STRIDE
        overlap = jnp.where((coln < rowj + SEL_BLOCK) & (coln + CMP_BLOCK > rowj)
                            & (coln < (ncp - 1) * CMP_STRIDE) & (rowj < nb * SEL_BLOCK), 1.0, 0.0)
        ov = overlap.astype(BF16)
        imp = jnp.dot(jnp.concatenate([ov, ov], axis=1), jnp.concatenate(_hi_lo(psum), axis=0),
                      preferred_element_type=F32)[:nb]
        j_sub = lax.broadcasted_iota(I32, (nb, 1), 0)
        qblk = tq // SEL_BLOCK
        forced = (j_sub == 0) | (j_sub == qblk) | (j_sub == qblk - 1)
        imp = jnp.where(forced, FORCE, jnp.where(j_sub <= qblk, imp, NEG))
        cnt = jnp.zeros((nb, tq_n), F32)
        for i in range(nb):
            ci = imp[i:i + 1, :]
            tie = jnp.where(j_sub > i, 1.0, 0.0)
            cnt = cnt + jnp.where(ci > imp, 1.0, jnp.where(ci == imp, tie, 0.0))
        notsel_scr[...] = jnp.where(cnt < float(n_sel), 0.0, 1.0)

    notsel = notsel_scr[...]

    sub_h = lax.broadcasted_iota(I32, (dh - nb, 1), 0)
    for r in range(nr):
        scol = jnp.zeros((dh - nb, 1), F32)
        for i in range(3):
            piece = slopes_ref[(i + 1) * NSA_HEADS + g * nr + r]
            scol = jnp.where((sub_h == i) | (sub_h == 3 + i), piece, scol)
        q4_scr[:, r * tq_n:(r + 1) * tq_n] = jnp.concatenate(
            [notsel, jnp.broadcast_to(scol, (dh - nb, tq_n)), q_t[r] * LOG2E], axis=0).astype(BF16)

    nwt = WINDOW // tk_n
    nbuf = nwt + 2
    stats_s, stats_w = scr[0:3], scr[3:6]
    s_buf, p_buf, a_buf = scr[6:6 + nbuf], scr[6 + nbuf:6 + 2 * nbuf], scr[6 + 2 * nbuf:6 + 3 * nbuf]
    for m_ref, l_ref, acc_ref in (stats_s, stats_w):
        m_ref[...] = jnp.full(m_ref.shape, NEG, F32)
        l_ref[...] = jnp.zeros(l_ref.shape, F32)
        acc_ref[...] = jnp.zeros(acc_ref.shape, F32)
    key_i = lax.broadcasted_iota(I32, (tk_n, LANES), 0)
    qry_j = lax.broadcasted_iota(I32, (tk_n, LANES), 1)

    def scores(k_ref, kt, buf):
        k_tile = k_ref[kt * tk_n:(kt + 1) * tk_n, :]
        s_buf[buf][...] = jnp.dot(k_tile, q4_scr[...], preferred_element_type=F32)

    def softmax(buf, mode, stats):
        m_ref, l_ref, _ = stats
        for cb in range(nr * tq_n // LANES):
            cols = slice(cb * LANES, (cb + 1) * LANES)
            s = s_buf[buf][:, cols]
            if mode is not None:
                j = qry_j + (cb * LANES) % tq_n
                s = jnp.where(key_i <= j if mode == "causal" else key_i > j, s, -MASKED)
            m_prev = m_ref[:, cols]
            m_new = jnp.maximum(m_prev, jnp.max(s, axis=0, keepdims=True))
            alpha = jnp.exp2(m_prev - m_new)
            p = jnp.exp2(s - m_new)
            l_ref[:, cols] = alpha * l_ref[:, cols] + jnp.sum(p, axis=0, keepdims=True)
            m_ref[:, cols] = m_new
            a_buf[buf][:, cols] = alpha
            p_buf[buf][:, cols] = p.astype(BF16)

    def values(vt_ref, kt, buf, stats):
        acc_ref = stats[2]
        pv = jnp.dot(vt_ref[kt], p_buf[buf][...], preferred_element_type=F32)
        acc_ref[...] = acc_ref[...] * a_buf[buf][...] + pv

    def batch(jobs):
        for i in range(min(nbuf, len(jobs))):
            scores(jobs[i][0], jobs[i][2], i)
        for i, (_, vt_ref, kt, mode, stats) in enumerate(jobs):
            softmax(i % nbuf, mode, stats)
            values(vt_ref, kt, i % nbuf, stats)
            if i + nbuf < len(jobs):
                scores(jobs[i + nbuf][0], jobs[i + nbuf][2], i % nbuf)

    for k in range(seq // tq_n):
        @pl.when(qi == k)
        def _(k=k):
            jobs = [(ksb, vst, t, None, stats_s) for t in range(k)]
            jobs += [(kwb, vwt, k - back, "band" if back == nwt else None, stats_w)
                     for back in range(min(nwt, k), 0, -1)]
            jobs += [(kwb, vwt, k, "causal", stats_w), (ksb, vst, k, "causal", stats_s)]
            batch(jobs)

    gsel = jnp.where(lax.broadcasted_iota(I32, (LANES, LANES), 0)
                     == lax.broadcasted_iota(I32, (LANES, LANES), 1) + g * (nr * N_BRANCH), 1.0, 0.0)
    gs = gsel.astype(BF16)
    gates = jax.nn.sigmoid(jnp.dot(jnp.concatenate(_hi_lo(gate_ref[0]), axis=1), jnp.concatenate([gs, gs], axis=0),
                                   preferred_element_type=F32)).T
    (_, l_s, acc_s), (_, l_w, acc_w) = stats_s, stats_w
    for pair in range(nr // 2):
        o_t = []
        for r in (2 * pair, 2 * pair + 1):
            c0 = r * N_BRANCH
            cols = slice(r * tq_n, (r + 1) * tq_n)
            o_t.append(gates[c0:c0 + 1, :] * o_c[r] + (gates[c0 + 1:c0 + 2, :] / l_s[:, cols]) * acc_s[:, cols]
                       + (gates[c0 + 2:c0 + 3, :] / l_w[:, cols]) * acc_w[:, cols])
        o_ref[0, :, pair * LANES:(pair + 1) * LANES] = jnp.concatenate(o_t, axis=0).T


def _nsa(slopes, proj_nsa, kvc, proj_gla):
    bsz, _, s, dh = proj_nsa.shape
    g_n, nr = NSA_KV_HEADS, NSA_Q_PER_KV
    tq = NSA_TQ
    tk = NSA_TK
    assert tq == tk and WINDOW % tk == 0 and 2 * dh == LANES
    ncp = kvc.shape[2]
    kv0 = NSA_HEADS
    nq = nr * tq
    nbuf = WINDOW // tk + 2

    def kv_spec(i):
        return pl.BlockSpec((1, 1, s, dh), lambda b, g, q, i=i: (b, kv0 + i * g_n + g, 0, 0))

    return pl.pallas_call(
        _nsa_kernel,
        grid=(bsz, g_n, s // tq),
        in_specs=[pl.BlockSpec(memory_space=pltpu.SMEM),
                  pl.BlockSpec((1, nr, tq, dh), lambda b, g, q: (b, g, q, 0)),
                  pl.BlockSpec((1, 1, ncp, dh), lambda b, g, q: (b, g, 0, 0)),
                  pl.BlockSpec((1, 1, ncp, dh), lambda b, g, q: (b, g_n + g, 0, 0)),
                  kv_spec(2), kv_spec(3), kv_spec(4), kv_spec(5),
                  pl.BlockSpec((1, tq, LANES), lambda b, g, q: (b, q, GLA_MISC_OFF // LANES))],
        out_specs=pl.BlockSpec((1, tq, nr * dh), lambda b, g, q: (b, q, g)),
        out_shape=jax.ShapeDtypeStruct((bsz, s, NSA_HEADS * dh), F32),
        scratch_shapes=[pltpu.VMEM((s, LANES), BF16), pltpu.VMEM((s // tk, dh, tk), BF16)] * 2
        + [pltpu.VMEM((LANES, nq), BF16), pltpu.VMEM((s // SEL_BLOCK, tq), F32)]
        + [pltpu.VMEM((1, nq), F32), pltpu.VMEM((1, nq), F32), pltpu.VMEM((dh, nq), F32)] * 2
        + [pltpu.VMEM((tk, nq), F32)] * nbuf + [pltpu.VMEM((tk, nq), BF16)] * nbuf + [pltpu.VMEM((1, nq), F32)] * nbuf,
        compiler_params=_cparams(("parallel", "parallel", "arbitrary")),
        name="nsa_attention",
    )(slopes, proj_nsa, kvc, kvc, proj_nsa, proj_nsa, proj_nsa, proj_nsa, proj_gla)


def _gla_kernel(q_ref, k_ref, v_ref, og_ref, lr_ref, wg_ref, bg_ref, nw_ref, o_ref, st_scr, la_scr, b_scr):
    rows_n = q_ref.shape[1]
    c_n, sub = GLA_CHUNK, GLA_SUB
    nh, dk, dv = GLA_HEADS, GLA_DK, GLA_DV

    @pl.when(pl.program_id(1) == 0)
    def _():
        st_scr[...] = jnp.zeros(st_scr.shape, F32)

    z = _dot_hi_lo(lr_ref[0], wg_ref[...]) + bg_ref[...]
    la_scr[...] = (jnp.minimum(z, 0.0) - jnp.log1p(jnp.exp(-jnp.abs(z)))) * (1.0 / GLA_GATE_NORM)
    tril = jnp.where(lax.broadcasted_iota(I32, (c_n, c_n), 0) >= lax.broadcasted_iota(I32, (c_n, c_n), 1), 1.0, 0.0)
    tril3 = jnp.concatenate([tril.astype(BF16)] * 3, axis=1)
    row_c = lax.broadcasted_iota(I32, (c_n, 1), 0)
    row_s = lax.broadcasted_iota(I32, (sub, 1), 0)
    lane_c = lax.broadcasted_iota(I32, (1, c_n), 1)
    nw = nw_ref[...]
    hk = [slice(h * dk, (h + 1) * dk) for h in range(nh)]
    hv = [slice(h * dv, (h + 1) * dv) for h in range(nh)]

    def chunk(c, carry):
        r0 = pl.multiple_of(c * c_n, c_n)
        rows = pl.ds(r0, c_n)
        qc = q_ref[0, rows, :] * (dk ** -0.5)
        kc = k_ref[0, rows, :]
        vc = [v_ref[0, rows, hv[h]].astype(BF16) for h in range(nh)]
        pieces, rest = [], la_scr[rows, :]
        for _ in range(3):
            pieces.append(rest.astype(BF16))
            rest = rest - pieces[-1].astype(F32)
        b = jnp.dot(tril3, jnp.concatenate(pieces, axis=0), preferred_element_type=F32)
        b_scr[...] = b
        st = [st_scr[h] for h in range(nh)]
        q_e = (qc * jnp.exp(b)).astype(BF16)
        o = [_nt_dot(q_e[:, hk[h]], st[h].astype(BF16)) for h in range(nh)]
        strips = [[] for _ in range(nh)]
        for blk in range(c_n // sub):
            lo = blk * sub
            q_i = qc[lo:lo + sub]
            b_i = b[lo:lo + sub]
            a = [jnp.zeros((sub, c_n), F32) for _ in range(nh)]
            if blk > 0:
                b_r = b_scr[lo - 1:lo, :]
                q_d = (q_i * jnp.exp(b_i - b_r)).astype(BF16)
                k_d = (kc * jnp.exp(jnp.where(row_c < lo, b_r - b, -jnp.inf))).astype(BF16)
                a = [_nt_dot(q_d[:, hk[h]], k_d[:, hk[h]]) for h in range(nh)]
            for j in range(sub):
                b_j = b_scr[lo + j:lo + j + 1, :]
                k_j = k_ref[0, pl.ds(r0 + lo + j, 1), :]
                prod = q_i * k_j * jnp.exp(jnp.where(row_s >= j, b_i - b_j, -jnp.inf))
                for h in range(nh):
                    col = jnp.sum(prod[:, hk[h]], axis=-1, keepdims=True)
                    a[h] = jnp.where(lane_c == lo + j, col, a[h]) if blk == 0 else (
                        a[h] + jnp.where(lane_c == lo + j, col, 0.0))
            for h in range(nh):
                strips[h].append(a[h])
        for h in range(nh):
            attn = jnp.concatenate(strips[h], axis=0)
            o[h] = o[h] + jnp.dot(attn.astype(BF16), vc[h], preferred_element_type=F32)
        b_last = b_scr[c_n - 1:c_n, :]
        k_dec = (kc * jnp.exp(b_last - b)).astype(BF16)
        decay = jnp.exp(b_last)
        for h in range(nh):
            st_scr[h] = st[h] * decay[:, hk[h]] + lax.dot_general(
                vc[h], k_dec[:, hk[h]], (((0,), (0,)), ((), ())), preferred_element_type=F32)
        for h in range(nh):
            og = og_ref[0, rows, hv[h]]
            on = o[h] * lax.rsqrt(jnp.mean(o[h] * o[h], axis=-1, keepdims=True) + EPS) * nw
            o_ref[0, rows, hv[h]] = on * (og * jax.nn.sigmoid(og))
        return carry

    lax.fori_loop(0, rows_n // c_n, chunk, 0)


def _gla(proj_gla, wg_pad, bg, nw):
    bsz, s, _ = proj_gla.shape
    nh, dk, dv = GLA_HEADS, GLA_DK, GLA_DV
    ts = GLA_TS
    wk, wv = nh * dk, nh * dv
    return pl.pallas_call(
        _gla_kernel,
        grid=(bsz, s // ts),
        in_specs=[pl.BlockSpec((1, ts, wk), lambda b, i: (b, i, GLA_Q_OFF // wk)),
                  pl.BlockSpec((1, ts, wk), lambda b, i: (b, i, GLA_K_OFF // wk)),
                  pl.BlockSpec((1, ts, wv), lambda b, i: (b, i, GLA_V_OFF // wv)),
                  pl.BlockSpec((1, ts, wv), lambda b, i: (b, i, GLA_OG_OFF // wv)),
                  pl.BlockSpec((1, ts, LANES), lambda b, i: (b, i, GLA_MISC_OFF // LANES)),
                  pl.BlockSpec((LANES, wk), lambda b, i: (0, 0)),
                  pl.BlockSpec((1, wk), lambda b, i: (0, 0)),
                  pl.BlockSpec((1, dv), lambda b, i: (0, 0))],
        out_specs=pl.BlockSpec((1, ts, wv), lambda b, i: (b, i, 0)),
        out_shape=jax.ShapeDtypeStruct((bsz, s, wv), F32),
        scratch_shapes=[pltpu.VMEM((nh, dv, dk), F32), pltpu.VMEM((ts, wk), F32), pltpu.VMEM((GLA_CHUNK, wk), F32)],
        compiler_params=_cparams(("parallel", "arbitrary")),
        name="gla",
    )(proj_gla, proj_gla, proj_gla, proj_gla, proj_gla, wg_pad, bg, nw)


def _outproj_kernel(nsa_ref, gla_ref, x_ref, wo_ref, g1_ref, sc_ref, sh_ref, nw_ref, wr_ref, br_ref,
                    x1_ref, h_ref, route_ref):
    half = nsa_ref.shape[2]
    acc = jnp.dot(nsa_ref[0].astype(BF16), wo_ref[:half, :], preferred_element_type=F32)
    acc = acc + jnp.dot(gla_ref[0].astype(BF16), wo_ref[half:, :], preferred_element_type=F32)
    x1 = x_ref[0] + g1_ref[0, 0] * acc
    x1_ref[0] = x1
    h = _modulated_norm(x1, nw_ref[...], sc_ref[0, 0], sh_ref[0, 0])
    h_ref[0] = h
    h_hi = h.astype(BF16)
    h_lo = (h - h_hi.astype(F32)).astype(BF16)
    t = jnp.dot(h_hi, wr_ref[...], preferred_element_type=F32)
    logits = (t[:, :LANES] + t[:, LANES:] + jnp.dot(h_lo, wr_ref[:, :LANES], preferred_element_type=F32)
              + br_ref[...])
    lane = lax.broadcasted_iota(I32, (1, LANES), 1)
    ninf = -jnp.inf
    is_g = (lane >= N_EXPERTS) & (lane < N_EXPERTS + N_GROUPS)
    gl = jnp.where(is_g, logits, ninf)
    ge = jnp.exp(gl - jnp.max(gl, axis=-1, keepdims=True))
    gp = ge / jnp.sum(ge, axis=-1, keepdims=True)
    gp_max = jnp.max(gp, axis=-1, keepdims=True)
    grp = jnp.min(jnp.where((gp == gp_max) & is_g, lane - N_EXPERTS, LANES), axis=-1, keepdims=True)
    in_grp = (lane // EXPERTS_PER_GROUP == grp) & (lane < N_EXPERTS)
    el = jnp.where(in_grp, logits, ninf)
    v1 = jnp.max(el, axis=-1, keepdims=True)
    i1 = jnp.min(jnp.where(el == v1, lane, LANES), axis=-1, keepdims=True)
    el2 = jnp.where(lane == i1, ninf, el)
    v2 = jnp.max(el2, axis=-1, keepdims=True)
    i2 = jnp.min(jnp.where(el2 == v2, lane, LANES), axis=-1, keepdims=True)
    e2 = jnp.exp(v2 - v1)
    den = 1.0 + e2
    w1 = gp_max * (1.0 / den)
    w2 = gp_max * (e2 / den)
    route_ref[0] = jnp.where(lane == 0, i1.astype(F32), jnp.where(lane == 1, i2.astype(F32), jnp.where(
        lane == 2, w1, jnp.where(lane == 3, w2, 0.0))))


def _outproj(o_nsa, o_gla, x, wo, mod4, nw, wr, br):
    bsz, s, d = x.shape
    tm = OUT_TM
    half = o_nsa.shape[2]

    def mod_spec(idx):
        return pl.BlockSpec((1, 1, 1, d), lambda b, i: (b, idx, 0, 0))

    row = lambda w: pl.BlockSpec((1, tm, w), lambda b, i: (b, i, 0))
    return pl.pallas_call(
        _outproj_kernel,
        grid=(bsz, s // tm),
        in_specs=[row(half), row(half), row(d),
                  pl.BlockSpec((2 * half, d), lambda b, i: (0, 0), pipeline_mode=pl.Buffered(1)),
                  mod_spec(2), mod_spec(4), mod_spec(3),
                  pl.BlockSpec((1, d), lambda b, i: (0, 0)),
                  pl.BlockSpec((d, 2 * LANES), lambda b, i: (0, 0)),
                  pl.BlockSpec((1, LANES), lambda b, i: (0, 0))],
        out_specs=[row(d), row(d), row(LANES)],
        out_shape=[jax.ShapeDtypeStruct((bsz, s, d), F32), jax.ShapeDtypeStruct((bsz, s, d), F32),
                   jax.ShapeDtypeStruct((bsz, s, LANES), F32)],
        compiler_params=_cparams(("parallel", "parallel")),
        name="outproj_router",
    )(o_nsa, o_gla, x, wo, mod4, mod4, mod4, nw, wr, br)


def _rank_kernel(route_ref, dest_ref, meta_ref, rank_scr):
    n = route_ref.shape[0]
    tm = RANK_TM
    lane_i = lax.broadcasted_iota(I32, (1, LANES), 1)
    lane = lane_i.astype(F32)
    strict = jnp.where(lax.broadcasted_iota(I32, (tm, tm), 0) > lax.broadcasted_iota(I32, (tm, tm), 1),
                       1.0, 0.0).astype(BF16)

    def two_lanes(a, b):
        return jnp.where(lane_i == 0, a, jnp.where(lane_i == 1, b, 0.0))

    def pick(e, table):
        return jnp.sum(jnp.where(lane == e, table, 0.0), axis=-1, keepdims=True)

    def count(i, seen):
        r0 = pl.multiple_of(i * tm, tm)
        rt = route_ref[pl.ds(r0, tm), :]
        e1, e2 = rt[:, 0:1], rt[:, 1:2]
        member = jnp.where(lane == e1, 1.0, jnp.where(lane == e2, 1.0, 0.0))
        before = jnp.dot(strict, member.astype(BF16), preferred_element_type=F32) + seen
        rank_scr[pl.ds(r0, tm), :] = two_lanes(pick(e1, before), pick(e2, before))
        return seen + jnp.sum(member, axis=0, keepdims=True)

    counts = lax.fori_loop(0, n // tm, count, jnp.zeros((1, LANES), F32))
    ntile = jnp.floor((counts + (MOE_TB - 1)) * (1.0 / MOE_TB))
    incl = jnp.where(lax.broadcasted_iota(I32, (LANES, LANES), 0) <= lax.broadcasted_iota(I32, (LANES, LANES), 1),
                     1.0, 0.0).astype(BF16)
    tile_end = jnp.dot(jnp.broadcast_to(ntile, (8, LANES)).astype(BF16), incl,
                       preferred_element_type=F32)[0:1]
    row_start = (tile_end - ntile) * MOE_TB

    def place(i, carry):
        r0 = pl.multiple_of(i * tm, tm)
        rt = route_ref[pl.ds(r0, tm), :]
        rk = rank_scr[pl.ds(r0, tm), :]
        d1 = pick(rt[:, 0:1], row_start) + rk[:, 0:1]
        d2 = pick(rt[:, 1:2], row_start) + rk[:, 1:2]
        dest_ref[pl.ds(r0, tm), :] = two_lanes(d1, d2).astype(I32)
        return carry

    lax.fori_loop(0, n // tm, place, 0)
    trow = lax.broadcasted_iota(I32, (meta_ref.shape[0], 1), 0).astype(F32)
    texp = jnp.sum(jnp.where((tile_end <= trow) & (lane_i < N_EXPERTS), 1.0, 0.0), axis=-1, keepdims=True)
    texp = jnp.minimum(texp, N_EXPERTS - 1.0)
    used = pick(N_EXPERTS - 1.0, tile_end)
    diag = lax.broadcasted_iota(I32, (meta_ref.shape[0], LANES), 0) == lane_i
    end_rows = jnp.sum(jnp.where(diag, tile_end, 0.0), axis=-1, keepdims=True)
    ntile_rows = jnp.sum(jnp.where(diag, ntile, 0.0), axis=-1, keepdims=True)
    meta_ref[...] = jnp.where(lane_i == 2, end_rows, jnp.where(lane_i == 3, ntile_rows, two_lanes(
        texp, jnp.broadcast_to(used, texp.shape)))).astype(I32)


def _rank(route):
    n = route.shape[0]
    return pl.pallas_call(
        _rank_kernel,
        out_shape=[jax.ShapeDtypeStruct((n, LANES), I32), jax.ShapeDtypeStruct((LANES, LANES), I32)],
        scratch_shapes=[pltpu.VMEM((n, LANES), F32)],
        compiler_params=pltpu.CompilerParams(vmem_limit_bytes=VMEM_LIMIT),
        name="moe_rank",
    )(route)


def _dispatch_kernel(dest_ref, ends_ref, h_ref, xs_ref, zero_scr, sem, zsem):
    i = pl.program_id(0)
    tm = h_ref.shape[0]
    tb = zero_scr.shape[0]

    @pl.when(i == 0)
    def _():
        zero_scr[...] = jnp.zeros(zero_scr.shape, F32)

        def zero_copy(e):
            r0 = pl.multiple_of((ends_ref[e] - 1) * tb, tb)
            return pltpu.make_async_copy(zero_scr, xs_ref.at[pl.ds(r0, tb)], zsem)

        def start(e, carry):
            @pl.when(ends_ref[N_EXPERTS + e] > 0)
            def _():
                zero_copy(e).start()
            return carry

        def wait(e, carry):
            @pl.when(ends_ref[N_EXPERTS + e] > 0)
            def _():
                zero_copy(e).wait()
            return carry

        def tail_copy(t):
            return pltpu.make_async_copy(zero_scr, xs_ref.at[pl.ds(pl.multiple_of(t * tb, tb), tb)], zsem)

        def tail_start(t, carry):
            tail_copy(t).start()
            return carry

        def tail_wait(t, carry):
            tail_copy(t).wait()
            return carry

        used = ends_ref[2 * N_EXPERTS]
        lax.fori_loop(0, N_EXPERTS, start, 0)
        lax.fori_loop(used, xs_ref.shape[0] // tb, tail_start, 0)
        lax.fori_loop(0, N_EXPERTS, wait, 0)
        lax.fori_loop(used, xs_ref.shape[0] // tb, tail_wait, 0)

    def issue(r, carry):
        p = (i * tm + r) * TOP_K
        for k in range(TOP_K):
            pltpu.make_async_copy(h_ref.at[pl.ds(r, 1)], xs_ref.at[pl.ds(dest_ref[p + k], 1)], sem).start()
        return carry

    lax.fori_loop(0, tm, issue, 0, unroll=DMA_UNROLL)
    for k in range(TOP_K):
        pltpu.make_async_copy(h_ref, xs_ref.at[pl.ds(0, tm)], sem).wait()


def _dispatch(dest_flat, ends_flat, h, cap):
    n, d = h.shape
    tm = DISPATCH_TM
    return pl.pallas_call(
        _dispatch_kernel,
        grid_spec=pltpu.PrefetchScalarGridSpec(
            num_scalar_prefetch=2, grid=(n // tm,),
            in_specs=[pl.BlockSpec((tm, d), lambda i, dst, ends: (i, 0))],
            out_specs=pl.BlockSpec(memory_space=pl.ANY),
            scratch_shapes=[pltpu.VMEM((MOE_TB, d), F32), pltpu.SemaphoreType.DMA(()), pltpu.SemaphoreType.DMA(())]),
        out_shape=jax.ShapeDtypeStruct((cap, d), F32),
        compiler_params=_cparams(("arbitrary",)),
        name="moe_dispatch",
    )(dest_flat, ends_flat, h)


def _ffn_kernel(meta_ref, x_ref, wg_hbm, wu_hbm, wd_hbm, y_ref, wg_st, wu_st, wd_st, wgb, wub, wdb, wsem):
    t = pl.program_id(0)
    ntile = pl.num_programs(0)
    n_used = meta_ref[ntile]
    e = meta_ref[t]
    e_prev = meta_ref[jnp.maximum(t - 1, 0)]
    active = t < n_used

    def fetch(expert):
        return [pltpu.make_async_copy(src.at[expert], dst, wsem)
                for src, dst in ((wg_hbm, wg_st), (wu_hbm, wu_st), (wd_hbm, wd_st))]

    @pl.when(t == 0)
    def _():
        for cp in fetch(e):
            cp.start()

    @pl.when(active & ((t == 0) | (e != e_prev)))
    def _():
        for cp in fetch(e):
            cp.wait()
        wgb[...] = wg_st[...].astype(BF16)
        wub[...] = wu_st[...].astype(BF16)
        wdb[...] = wd_st[...].astype(BF16)
        end = meta_ref[ntile + 1 + e]

        @pl.when(end < n_used)
        def _():
            for cp in fetch(meta_ref[end]):
                cp.start()

    @pl.when(active)
    def _():
        x = x_ref[...].astype(BF16)
        gate = jnp.dot(x, wgb[...], preferred_element_type=F32)
        up = jnp.dot(x, wub[...], preferred_element_type=F32)
        act = (gate * jax.nn.sigmoid(gate)) * up
        y_ref[...] = jnp.dot(act.astype(BF16), wdb[...], preferred_element_type=F32)

    @pl.when(jnp.logical_not(active))
    def _():
        y_ref[...] = jnp.zeros(y_ref.shape, F32)


def _ffn(meta_flat, xs, wg, wu, wd):
    cap, d = xs.shape
    ff = wg.shape[2]
    tb = MOE_TB
    ntile = cap // tb
    return pl.pallas_call(
        _ffn_kernel,
        grid_spec=pltpu.PrefetchScalarGridSpec(
            num_scalar_prefetch=1, grid=(ntile,),
            in_specs=[pl.BlockSpec((tb, d), lambda t, m: (jnp.minimum(t, m[ntile] - 1), 0)),
                      pl.BlockSpec(memory_space=pl.ANY), pl.BlockSpec(memory_space=pl.ANY),
                      pl.BlockSpec(memory_space=pl.ANY)],
            out_specs=pl.BlockSpec((tb, d), lambda t, m: (t, 0)),
            scratch_shapes=[pltpu.VMEM((d, ff), F32), pltpu.VMEM((d, ff), F32), pltpu.VMEM((ff, d), F32),
                            pltpu.VMEM((d, ff), BF16), pltpu.VMEM((d, ff), BF16), pltpu.VMEM((ff, d), BF16),
                            pltpu.SemaphoreType.DMA(())]),
        out_shape=jax.ShapeDtypeStruct((cap, d), F32),
        compiler_params=_cparams(("arbitrary",)),
        name="moe_ffn",
    )(meta_flat, xs, wg, wu, wd)


def _combine_kernel(dest_ref, y_ref, x1_ref, route_ref, g2_ref, nf_ref, o_ref, ybuf, sem):
    i = pl.program_id(0)
    nstep = pl.num_programs(0)
    tm = x1_ref.shape[0]

    def issue(tile, slot):
        def body(r, carry):
            p = (tile * tm + r) * TOP_K
            for k in range(TOP_K):
                pltpu.make_async_copy(y_ref.at[pl.ds(dest_ref[p + k], 1)], ybuf.at[slot, k, pl.ds(r, 1)],
                                      sem.at[slot]).start()
            return carry

        lax.fori_loop(0, tm, body, 0, unroll=DMA_UNROLL)

    @pl.when(i == 0)
    def _():
        issue(0, 0)

    @pl.when(i + 1 < nstep)
    def _():
        issue(i + 1, (i + 1) % 2)

    slot = i % 2
    for k in range(TOP_K):
        pltpu.make_async_copy(y_ref.at[pl.ds(0, tm)], ybuf.at[slot, k], sem.at[slot]).wait()
    rt = route_ref[...]
    moe = rt[:, 2:3] * ybuf[slot, 0] + rt[:, 3:4] * ybuf[slot, 1]
    xo = x1_ref[...] + g2_ref[0, 0] * moe
    o_ref[...] = xo * lax.rsqrt(jnp.mean(xo * xo, axis=-1, keepdims=True) + EPS) * nf_ref[...]


def _combine(dest_flat, y, x1, route, mod4, nf, seq):
    n, d = x1.shape
    tm = COMB_TM
    tiles_per_seq = seq // tm
    return pl.pallas_call(
        _combine_kernel,
        grid_spec=pltpu.PrefetchScalarGridSpec(
            num_scalar_prefetch=1, grid=(n // tm,),
            in_specs=[pl.BlockSpec(memory_space=pl.ANY),
                      pl.BlockSpec((tm, d), lambda i, dst: (i, 0)),
                      pl.BlockSpec((tm, LANES), lambda i, dst: (i, 0)),
                      pl.BlockSpec((1, 1, 1, d), lambda i, dst: (i // tiles_per_seq, 5, 0, 0)),
                      pl.BlockSpec((1, d), lambda i, dst: (0, 0))],
            out_specs=pl.BlockSpec((tm, d), lambda i, dst: (i, 0)),
            scratch_shapes=[pltpu.VMEM((2, TOP_K, tm, d), F32), pltpu.SemaphoreType.DMA((2,))]),
        out_shape=jax.ShapeDtypeStruct((n, d), F32),
        compiler_params=_cparams(("arbitrary",)),
        name="moe_combine",
    )(dest_flat, y, x1, route, mod4, nf)


def _alibi_slopes():
    n = NSA_HEADS
    full = jnp.asarray(2.0 ** (-8.0 * np.arange(1, n + 1) / n), dtype=F32)
    pieces, rest = [], full * LOG2E
    for _ in range(3):
        piece = rest.astype(BF16).astype(F32)
        pieces.append(piece)
        rest = rest - piece
    return jnp.concatenate([full] + pieces)


def _layer(x, c, w_ada, b_ada, norm1_w, w_in, cmp_pos_k, cmp_w1_k, cmp_w2_k, cmp_pos_v, cmp_w1_v, cmp_w2_v,
           gla_w_gate2, gla_b_gate, gla_norm_w, w_out, norm2_w, w_rg, b_rg, w_re, b_re, w_eg, w_eu, w_ed):
    bsz, s, d = x.shape
    dh = NSA_HEAD_DIM
    mod4 = _adaln(c, w_ada, b_ada).reshape(bsz, 6, 1, d)

    o_gate = NSA_COLS
    o_gla = o_gate + NSA_GATE_COLS
    o_lr = o_gla + 2 * GLA_HEADS * GLA_DK + 2 * GLA_HEADS * GLA_DV
    w_nsa, w_gla = _prep_w_in(w_in.T, o_gate, o_gla, o_lr)
    nw1 = norm1_w.reshape(1, d)
    proj_nsa, proj_gla = _inproj(x, mod4, nw1, w_nsa, w_gla, sc_idx=1, sh_idx=0)

    pos = jnp.stack([cmp_pos_k, cmp_pos_v]).reshape(2, 2, CMP_STRIDE * dh)
    kvc = _compress(proj_nsa, pos, jnp.stack([cmp_w1_k, cmp_w1_v]), jnp.stack([cmp_w2_k, cmp_w2_v]))
    o_nsa = _nsa(_alibi_slopes(), proj_nsa, kvc, proj_gla)

    wg_pad = jnp.zeros((LANES, GLA_HEADS * GLA_DK), F32).at[
        NSA_GATE_COLS:NSA_GATE_COLS + GLA_GATE_RANK].set(gla_w_gate2)
    o_gla_out = _gla(proj_gla, wg_pad, gla_b_gate.reshape(1, -1), gla_norm_w.reshape(1, -1))

    wr = jnp.concatenate([w_re, w_rg, jnp.zeros((d, LANES - N_EXPERTS - N_GROUPS), F32)], axis=1)
    br = jnp.concatenate([b_re, b_rg, jnp.zeros((LANES - N_EXPERTS - N_GROUPS,), F32)]).reshape(1, LANES)
    wr_hi = wr.astype(BF16)
    wr_cat = jnp.concatenate([wr_hi, (wr - wr_hi.astype(F32)).astype(BF16)], axis=1)
    x1, h2, route = _outproj(o_nsa, o_gla_out, x, w_out.astype(BF16), mod4, norm2_w.reshape(1, d), wr_cat, br)

    n = bsz * s
    npair = n * TOP_K
    cap = npair + N_EXPERTS * MOE_TB
    ntile = cap // MOE_TB
    route2 = route.reshape(n, LANES)
    dest, meta = _rank(route2)
    dest_flat = dest[:, :TOP_K].reshape(npair)
    meta_flat = jnp.concatenate([meta[:ntile, 0], meta[:1, 1], meta[:N_EXPERTS, 2]])
    ends_flat = jnp.concatenate([meta[:N_EXPERTS, 2], meta[:N_EXPERTS, 3], meta[:1, 1]])
    xs = _dispatch(dest_flat, ends_flat, h2.reshape(n, d), cap)
    y = _ffn(meta_flat, xs, w_eg, w_eu, w_ed)
    return x1.reshape(n, d), y, dest_flat, route2, mod4


def kernel(x, c, w_ada, b_ada, norm1_w, w_in, cmp_pos_k, cmp_w1_k, cmp_w2_k, cmp_pos_v, cmp_w1_v, cmp_w2_v,
           gla_w_gate2, gla_b_gate, gla_norm_w, w_out, norm2_w, w_router_group, b_router_group, w_router_expert,
           b_router_expert, w_expert_gate, w_expert_up, w_expert_down, norm_f_w):
    bsz, s, d = x.shape
    assert w_ada.shape[0] == 1, "single layer"
    x1, y, dest_flat, route2, mod4 = _layer(
        x, c, w_ada[0], b_ada[0], norm1_w[0], w_in[0], cmp_pos_k[0], cmp_w1_k[0], cmp_w2_k[0], cmp_pos_v[0],
        cmp_w1_v[0], cmp_w2_v[0], gla_w_gate2[0], gla_b_gate[0], gla_norm_w[0], w_out[0], norm2_w[0],
        w_router_group[0], b_router_group[0], w_router_expert[0], b_router_expert[0],
        w_expert_gate[0], w_expert_up[0], w_expert_down[0])
    out = _combine(dest_flat, y, x1, route2, mod4, norm_f_w.reshape(1, d), s)
    return out.reshape(bsz, s, d)
```

```python
import functools

import numpy as np
import jax
import jax.numpy as jnp
from jax import lax
from jax.experimental import pallas as pl
from jax.experimental.pallas import tpu as pltpu

F32 = jnp.float32
BF16 = jnp.bfloat16
I32 = jnp.int32

NSA_HEAD_DIM = 64
NSA_HEADS = 16
NSA_KV_HEADS = 4
NSA_Q_PER_KV = 4
CMP_BLOCK = 32
CMP_STRIDE = 16
SEL_BLOCK = 64
N_SEL = 16
WINDOW = 512
N_BRANCH = 3
GLA_HEADS = 4
GLA_DV = 256
GLA_DK = 128
GLA_GATE_RANK = 16
GLA_GATE_NORM = 16.0
GLA_CHUNK = 64
GLA_SUB = 8
N_GROUPS = 4
EXPERTS_PER_GROUP = 8
N_EXPERTS = 32
TOP_K = 2
EPS = 1e-6
NEG = -1e30
FORCE = 1e30
LOG2E = 1.4426950408889634
MASKED = 2.0 ** 100

NSA_Q_COLS = NSA_HEADS * NSA_HEAD_DIM
NSA_KV_COLS = 2 * N_BRANCH * NSA_KV_HEADS * NSA_HEAD_DIM
NSA_GATE_COLS = N_BRANCH * NSA_HEADS
NSA_COLS = NSA_Q_COLS + NSA_KV_COLS
GLA_Q_OFF = 0
GLA_K_OFF = GLA_HEADS * GLA_DK
GLA_V_OFF = 2 * GLA_HEADS * GLA_DK
GLA_OG_OFF = GLA_V_OFF + GLA_HEADS * GLA_DV
GLA_MISC_OFF = GLA_OG_OFF + GLA_HEADS * GLA_DV
LANES = 128
GLA_COLS = GLA_MISC_OFF + LANES

VMEM_LIMIT = 56 * 1024 * 1024

ADA_TN = 768
PREP_TR = 256
PREP_STEP = 512
INPROJ_TM = 256
INPROJ_TN = 512
NSA_TQ = 256
NSA_TK = 256
GLA_TS = 512
OUT_TM = 512
RANK_TM = 256
CAST_ROWS = 64
MOE_TB = 256
DISPATCH_TM = 1024
COMB_TM = 512
DMA_UNROLL = 8


def _cparams(sem):
    return pltpu.CompilerParams(dimension_semantics=sem, vmem_limit_bytes=VMEM_LIMIT)


def _adaln_kernel(ct_ref, w_ref, b_ref, o_ref, s_scr):
    nb = ct_ref.shape[1]
    kdim, tn = w_ref.shape

    @pl.when(pl.program_id(0) == 0)
    def _():
        ct = ct_ref[...]
        s = ct * jax.nn.sigmoid(ct)
        for b in range(nb):
            s_scr[b] = jnp.broadcast_to(s[:, b:b + 1], (kdim, LANES))

    nlt = tn // LANES

    def body(k, accs):
        r = pl.multiple_of(k * 8, 8)
        w8 = [w_ref[pl.ds(r, 8), j * LANES:(j + 1) * LANES] for j in range(nlt)]
        out = []
        for b in range(nb):
            s8 = s_scr[b, pl.ds(r, 8), :]
            out.append(tuple(accs[b][j] + w8[j] * s8 for j in range(nlt)))
        return tuple(out)

    zero = jnp.zeros((8, LANES), F32)
    accs = lax.fori_loop(0, kdim // 8, body, tuple(tuple(zero for _ in range(nlt)) for _ in range(nb)), unroll=8)
    bias = b_ref[...]
    for b in range(nb):
        row = jnp.concatenate([jnp.sum(a, axis=0, keepdims=True) for a in accs[b]], axis=1)
        o_ref[b:b + 1, :] = row + bias


def _adaln(c, w, b):
    nb, d = c.shape
    n = w.shape[1]
    return pl.pallas_call(
        _adaln_kernel,
        grid=(n // ADA_TN,),
        in_specs=[pl.BlockSpec((d, nb), lambda j: (0, 0)),
                  pl.BlockSpec((d, ADA_TN), lambda j: (0, j)),
                  pl.BlockSpec((1, ADA_TN), lambda j: (0, j))],
        out_specs=pl.BlockSpec((nb, ADA_TN), lambda j: (0, j)),
        out_shape=jax.ShapeDtypeStruct((nb, n), F32),
        scratch_shapes=[pltpu.VMEM((nb, d, LANES), F32)],
        compiler_params=_cparams(("arbitrary",)),
        name="adaln",
    )(c.T, w, b.reshape(1, n))


def _modulated_norm(x, nw, sc, sh):
    ms = jnp.mean(x * x, axis=-1, keepdims=True)
    h = x * lax.rsqrt(ms + EPS) * nw
    return h * (1.0 + sc) + sh


def _prep_w_in_kernel(wt_ref, wn_ref, wg_ref, *, o_gate, o_gla, o_lr):
    tr = wt_ref.shape[1]
    step = PREP_STEP

    def put(dst, c0, rows):
        dst[:, c0:c0 + rows.shape[0]] = rows.T.astype(BF16)

    for c in range(0, o_gate, step):
        put(wn_ref, c, wt_ref[c:c + step, :])
    for c in range(0, o_lr - o_gla, step):
        put(wg_ref, c, wt_ref[o_gla + c:o_gla + c + step, :])
    pad = wg_ref.shape[1] - (wt_ref.shape[0] - o_gate)
    misc = jnp.concatenate([wt_ref[o_gate:o_gla, :], wt_ref[o_lr:, :], jnp.zeros((pad, tr), F32)], axis=0)
    put(wg_ref, o_lr - o_gla, misc)


def _prep_w_in(w_in_t, o_gate, o_gla, o_lr):
    n, d = w_in_t.shape
    tr = PREP_TR
    assert o_gate % PREP_STEP == 0 and (o_lr - o_gla) % PREP_STEP == 0
    return pl.pallas_call(
        functools.partial(_prep_w_in_kernel, o_gate=o_gate, o_gla=o_gla, o_lr=o_lr),
        grid=(d // tr,),
        in_specs=[pl.BlockSpec((n, tr), lambda i: (0, i))],
        out_specs=[pl.BlockSpec((tr, NSA_COLS), lambda i: (i, 0)), pl.BlockSpec((tr, GLA_COLS), lambda i: (i, 0))],
        out_shape=[jax.ShapeDtypeStruct((d, NSA_COLS), BF16), jax.ShapeDtypeStruct((d, GLA_COLS), BF16)],
        compiler_params=_cparams(("parallel",)),
        name="prep_w_in",
    )(w_in_t)


def _inproj_kernel(x_ref, sc_ref, sh_ref, nw_ref, wn_ref, wg_ref, on_ref, og_ref):
    h = _modulated_norm(x_ref[0], nw_ref[...], sc_ref[0, 0], sh_ref[0, 0]).astype(BF16)
    dh = NSA_HEAD_DIM
    tn = INPROJ_TN
    for c in range(wn_ref.shape[1] // tn):
        acc = jnp.dot(h, wn_ref[:, c * tn:(c + 1) * tn], preferred_element_type=F32)
        for u in range(tn // dh):
            on_ref[0, c * (tn // dh) + u] = acc[:, u * dh:(u + 1) * dh]
    og_ref[0] = jnp.dot(h, wg_ref[...], preferred_element_type=F32)


def _inproj(x, mod4, nw, w_nsa, w_gla, *, sc_idx, sh_idx):
    bsz, s, d = x.shape
    n_nsa, n_gla = w_nsa.shape[1], w_gla.shape[1]
    tm = INPROJ_TM
    dh = NSA_HEAD_DIM
    return pl.pallas_call(
        _inproj_kernel,
        grid=(bsz, s // tm),
        in_specs=[pl.BlockSpec((1, tm, d), lambda b, i: (b, i, 0)),
                  pl.BlockSpec((1, 1, 1, d), lambda b, i: (b, sc_idx, 0, 0)),
                  pl.BlockSpec((1, 1, 1, d), lambda b, i: (b, sh_idx, 0, 0)),
                  pl.BlockSpec((1, d), lambda b, i: (0, 0)),
                  pl.BlockSpec((d, n_nsa), lambda b, i: (0, 0), pipeline_mode=pl.Buffered(1)),
                  pl.BlockSpec((d, n_gla), lambda b, i: (0, 0), pipeline_mode=pl.Buffered(1))],
        out_specs=[pl.BlockSpec((1, n_nsa // dh, tm, dh), lambda b, i: (b, 0, i, 0)),
                   pl.BlockSpec((1, tm, n_gla), lambda b, i: (b, i, 0))],
        out_shape=[jax.ShapeDtypeStruct((bsz, n_nsa // dh, s, dh), F32),
                   jax.ShapeDtypeStruct((bsz, s, n_gla), F32)],
        compiler_params=_cparams(("parallel", "parallel")),
        name="inproj",
    )(x, mod4, mod4, nw, w_nsa, w_gla)


def _hi_lo(x):
    hi = x.astype(BF16)
    return hi, (x - hi.astype(F32)).astype(BF16)


def _dot_hi_lo(x, w):
    x_hi, x_lo = _hi_lo(x)
    w_hi, w_lo = _hi_lo(w)
    return jnp.dot(jnp.concatenate([x_hi, x_lo, x_hi], axis=1), jnp.concatenate([w_hi, w_hi, w_lo], axis=0),
                   preferred_element_type=F32)


def _compress_kernel(a_ref, pos_ref, w1_ref, w2_ref, o_ref):
    ng = a_ref.shape[1]
    nch = a_ref.shape[2] // CMP_STRIDE
    a = jnp.concatenate([
        jnp.concatenate([a_ref[0, g, pl.ds(t, nch, stride=CMP_STRIDE), :] for t in range(CMP_STRIDE)], axis=1)
        for g in range(ng)], axis=0)
    pos = pos_ref[0]
    half = a.shape[1]
    y1 = _dot_hi_lo(a + pos[0:1], w1_ref[0, :half, :])
    y2 = _dot_hi_lo(a + pos[1:2], w1_ref[0, half:, :])
    h = y1 + pltpu.roll(y2, a.shape[0] - 1, axis=0)
    out = _dot_hi_lo(jax.nn.gelu(h), w2_ref[0])
    row = lax.broadcasted_iota(I32, (nch, out.shape[1]), 0)
    for g in range(ng):
        o_ref[0, g] = jnp.where(row < nch - 1, out[g * nch:(g + 1) * nch], 0.0)


def _compress(proj_nsa, pos, w1, w2):
    bsz, _, s, dh = proj_nsa.shape
    ng = NSA_KV_HEADS
    nch = s // CMP_STRIDE
    cw = CMP_STRIDE * dh
    return pl.pallas_call(
        _compress_kernel,
        grid=(bsz, 2),
        in_specs=[pl.BlockSpec((1, ng, s, dh), lambda b, t: (b, NSA_HEADS // ng + t, 0, 0)),
                  pl.BlockSpec((1, 2, cw), lambda b, t: (t, 0, 0)),
                  pl.BlockSpec((1, 2 * cw, dh), lambda b, t: (t, 0, 0)),
                  pl.BlockSpec((1, dh, dh), lambda b, t: (t, 0, 0))],
        out_specs=pl.BlockSpec((1, ng, nch, dh), lambda b, t: (b, t, 0, 0)),
        out_shape=jax.ShapeDtypeStruct((bsz, 2 * ng, nch, dh), F32),
        compiler_params=_cparams(("parallel", "parallel")),
        name="nsa_compress",
    )(proj_nsa, pos, w1, w2)


def _nt_dot(a, b, **kw):
    return lax.dot_general(a, b, (((1,), (1,)), ((), ())), preferred_element_type=F32, **kw)


def _nsa_kernel(slopes_ref, q_ref, kc_ref, vc_ref, ks_ref, vs_ref, kw_ref, vw_ref, gate_ref, o_ref,
                ksb, vst, kwb, vwt, q4_scr, notsel_scr, *scr):
    g = pl.program_id(1)
    qi = pl.program_id(2)
    tq_n = q_ref.shape[2]
    dh = NSA_HEAD_DIM
    nr = NSA_Q_PER_KV
    seq = ks_ref.shape[2]
    tk_n = NSA_TK
    nb = seq // SEL_BLOCK

    @pl.when(qi == 0)
    def _():
        row = lax.broadcasted_iota(I32, (seq, dh), 0)
        lane = lax.broadcasted_iota(I32, (seq, dh), 1)
        blk = row // SEL_BLOCK
        pos = jnp.where((lane >= nb) & (lane < nb + 3), (blk * SEL_BLOCK).astype(F32),
                        jnp.where((lane >= nb + 3) & (lane < nb + 6), (row % SEL_BLOCK).astype(F32), 0.0))
        ksb[...] = jnp.concatenate([jnp.where(lane == blk, -MASKED, pos), ks_ref[0, 0]], axis=1).astype(BF16)
        kwb[...] = jnp.concatenate([pos, kw_ref[0, 0]], axis=1).astype(BF16)
        for c in range(seq // tk_n):
            rows = slice(c * tk_n, (c + 1) * tk_n)
            for src, dst in ((vs_ref, vst), (vw_ref, vwt)):
                v = src[0, 0, rows, :]
                dst[c] = jnp.concatenate([v, v], axis=1).T[:dh].astype(BF16)

    t0 = qi * tq_n
    tq = t0 + lax.broadcasted_iota(I32, (1, tq_n), 1)
    slopes = [slopes_ref[g * nr + r] for r in range(nr)]
    scale = dh ** -0.5
    q_t = []
    for pair in range(nr // 2):
        both = jnp.concatenate([q_ref[0, 2 * pair], q_ref[0, 2 * pair + 1]], axis=1).T * scale
        q_t += [both[:dh], both[dh:]]

    ncp = kc_ref.shape[2]
    kc = kc_ref[0, 0]
    vc = vc_ref[0, 0]
    vc_t = jnp.concatenate([vc, vc], axis=1).T[:dh].astype(BF16)
    n_sub = lax.broadcasted_iota(I32, (ncp, 1), 0)
    blk_end = n_sub * CMP_STRIDE + (CMP_BLOCK - 1)
    center = n_sub.astype(F32) * CMP_STRIDE + (CMP_BLOCK - 1) / 2.0
    kc_hi, kc_lo = _hi_lo(kc)
    kc_cat = jnp.concatenate([kc_hi, kc_lo, kc_hi, jnp.zeros_like(kc_hi)], axis=1)
    q_hi, q_lo = _hi_lo(jnp.concatenate(q_t, axis=1))
    q_cat = jnp.concatenate([q_hi, q_hi, q_lo, jnp.zeros_like(q_hi)], axis=0)
    s = jnp.dot(kc_cat, q_cat, preferred_element_type=F32)
    tq_all = jnp.concatenate([tq] * nr, axis=1)
    slope_all = jnp.concatenate([jnp.full((1, tq_n), slopes[r], F32) for r in range(nr)], axis=1)
    valid_all = blk_end <= tq_all
    s = jnp.where(valid_all, s - slope_all * (tq_all.astype(F32) - center), NEG)
    e = jnp.exp(s - jnp.max(s, axis=0, keepdims=True))
    p = jnp.where(valid_all, e / jnp.sum(e, axis=0, keepdims=True), 0.0)
    psum = sum(p[:, r * tq_n:(r + 1) * tq_n] for r in range(nr))
    o_c_all = jnp.dot(vc_t, p.astype(BF16), preferred_element_type=F32)
    o_c = [o_c_all[:, r * tq_n:(r + 1) * tq_n] for r in range(nr)]

    n_sel = min(N_SEL, nb)
    notsel_scr[...] = jnp.zeros(notsel_scr.shape, F32)

    @pl.when((qi + 1) * tq_n > n_sel * SEL_BLOCK)
    def _():
        rowj = lax.broadcasted_iota(I32, (LANES, ncp), 0) * SEL_BLOCK
        coln = lax.broadcasted_iota(I32, (LANES, ncp), 1) * CMP_STRIDE
        overlap = jnp.where((coln < rowj + SEL_BLOCK) & (coln + CMP_BLOCK > rowj)
                            & (coln < (ncp - 1) * CMP_STRIDE) & (rowj < nb * SEL_BLOCK), 1.0, 0.0)
        ov = overlap.astype(BF16)
        imp = jnp.dot(jnp.concatenate([ov, ov], axis=1), jnp.concatenate(_hi_lo(psum), axis=0),
                      preferred_element_type=F32)[:nb]
        j_sub = lax.broadcasted_iota(I32, (nb, 1), 0)
        qblk = tq // SEL_BLOCK
        forced = (j_sub == 0) | (j_sub == qblk) | (j_sub == qblk - 1)
        imp = jnp.where(forced, FORCE, jnp.where(j_sub <= qblk, imp, NEG))
        cnt = jnp.zeros((nb, tq_n), F32)
        for i in range(nb):
            ci = imp[i:i + 1, :]
            tie = jnp.where(j_sub > i, 1.0, 0.0)
            cnt = cnt + jnp.where(ci > imp, 1.0, jnp.where(ci == imp, tie, 0.0))
        notsel_scr[...] = jnp.where(cnt < float(n_sel), 0.0, 1.0)

    notsel = notsel_scr[...]

    sub_h = lax.broadcasted_iota(I32, (dh - nb, 1), 0)
    for r in range(nr):
        scol = jnp.zeros((dh - nb, 1), F32)
        for i in range(3):
            piece = slopes_ref[(i + 1) * NSA_HEADS + g * nr + r]
            scol = jnp.where((sub_h == i) | (sub_h == 3 + i), piece, scol)
        q4_scr[:, r * tq_n:(r + 1) * tq_n] = jnp.concatenate(
            [notsel, jnp.broadcast_to(scol, (dh - nb, tq_n)), q_t[r] * LOG2E], axis=0).astype(BF16)

    nwt = WINDOW // tk_n
    nbuf = nwt + 2
    stats_s, stats_w = scr[0:3], scr[3:6]
    s_buf, p_buf, a_buf = scr[6:6 + nbuf], scr[6 + nbuf:6 + 2 * nbuf], scr[6 + 2 * nbuf:6 + 3 * nbuf]
    for m_ref, l_ref, acc_ref in (stats_s, stats_w):
        m_ref[...] = jnp.full(m_ref.shape, NEG, F32)
        l_ref[...] = jnp.zeros(l_ref.shape, F32)
        acc_ref[...] = jnp.zeros(acc_ref.shape, F32)
    key_i = lax.broadcasted_iota(I32, (tk_n, LANES), 0)
    qry_j = lax.broadcasted_iota(I32, (tk_n, LANES), 1)

    def scores(k_ref, kt, buf):
        k_tile = k_ref[kt * tk_n:(kt + 1) * tk_n, :]
        s_buf[buf][...] = jnp.dot(k_tile, q4_scr[...], preferred_element_type=F32)

    def softmax(buf, mode, stats):
        m_ref, l_ref, _ = stats
        for cb in range(nr * tq_n // LANES):
            cols = slice(cb * LANES, (cb + 1) * LANES)
            s = s_buf[buf][:, cols]
            if mode is not None:
                j = qry_j + (cb * LANES) % tq_n
                s = jnp.where(key_i <= j if mode == "causal" else key_i > j, s, -MASKED)
            m_prev = m_ref[:, cols]
            m_new = jnp.maximum(m_prev, jnp.max(s, axis=0, keepdims=True))
            alpha = jnp.exp2(m_prev - m_new)
            p = jnp.exp2(s - m_new)
            l_ref[:, cols] = alpha * l_ref[:, cols] + jnp.sum(p, axis=0, keepdims=True)
            m_ref[:, cols] = m_new
            a_buf[buf][:, cols] = alpha
            p_buf[buf][:, cols] = p.astype(BF16)

    def values(vt_ref, kt, buf, stats):
        acc_ref = stats[2]
        pv = jnp.dot(vt_ref[kt], p_buf[buf][...], preferred_element_type=F32)
        acc_ref[...] = acc_ref[...] * a_buf[buf][...] + pv

    def batch(jobs):
        for i in range(min(nbuf, len(jobs))):
            scores(jobs[i][0], jobs[i][2], i)
        for i, (_, vt_ref, kt, mode, stats) in enumerate(jobs):
            softmax(i % nbuf, mode, stats)
            values(vt_ref, kt, i % nbuf, stats)
            if i + nbuf < len(jobs):
                scores(jobs[i + nbuf][0], jobs[i + nbuf][2], i % nbuf)

    for k in range(seq // tq_n):
        @pl.when(qi == k)
        def _(k=k):
            jobs = [(ksb, vst, t, None, stats_s) for t in range(k)]
            jobs += [(kwb, vwt, k - back, "band" if back == nwt else None, stats_w)
                     for back in range(min(nwt, k), 0, -1)]
            jobs += [(kwb, vwt, k, "causal", stats_w), (ksb, vst, k, "causal", stats_s)]
            batch(jobs)

    gsel = jnp.where(lax.broadcasted_iota(I32, (LANES, LANES), 0)
                     == lax.broadcasted_iota(I32, (LANES, LANES), 1) + g * (nr * N_BRANCH), 1.0, 0.0)
    gs = gsel.astype(BF16)
    gates = jax.nn.sigmoid(jnp.dot(jnp.concatenate(_hi_lo(gate_ref[0]), axis=1), jnp.concatenate([gs, gs], axis=0),
                                   preferred_element_type=F32)).T
    (_, l_s, acc_s), (_, l_w, acc_w) = stats_s, stats_w
    for pair in range(nr // 2):
        o_t = []
        for r in (2 * pair, 2 * pair + 1):
            c0 = r * N_BRANCH
            cols = slice(r * tq_n, (r + 1) * tq_n)
            o_t.append(gates[c0:c0 + 1, :] * o_c[r] + (gates[c0 + 1:c0 + 2, :] / l_s[:, cols]) * acc_s[:, cols]
                       + (gates[c0 + 2:c0 + 3, :] / l_w[:, cols]) * acc_w[:, cols])
        o_ref[0, :, pair * LANES:(pair + 1) * LANES] = jnp.concatenate(o_t, axis=0).T


def _nsa(slopes, proj_nsa, kvc, proj_gla):
    bsz, _, s, dh = proj_nsa.shape
    g_n, nr = NSA_KV_HEADS, NSA_Q_PER_KV
    tq = NSA_TQ
    tk = NSA_TK
    assert tq == tk and WINDOW % tk == 0 and 2 * dh == LANES
    ncp = kvc.shape[2]
    kv0 = NSA_HEADS
    nq = nr * tq
    nbuf = WINDOW // tk + 2

    def kv_spec(i):
        return pl.BlockSpec((1, 1, s, dh), lambda b, g, q, i=i: (b, kv0 + i * g_n + g, 0, 0))

    return pl.pallas_call(
        _nsa_kernel,
        grid=(bsz, g_n, s // tq),
        in_specs=[pl.BlockSpec(memory_space=pltpu.SMEM),
                  pl.BlockSpec((1, nr, tq, dh), lambda b, g, q: (b, g, q, 0)),
                  pl.BlockSpec((1, 1, ncp, dh), lambda b, g, q: (b, g, 0, 0)),
                  pl.BlockSpec((1, 1, ncp, dh), lambda b, g, q: (b, g_n + g, 0, 0)),
                  kv_spec(2), kv_spec(3), kv_spec(4), kv_spec(5),
                  pl.BlockSpec((1, tq, LANES), lambda b, g, q: (b, q, GLA_MISC_OFF // LANES))],
        out_specs=pl.BlockSpec((1, tq, nr * dh), lambda b, g, q: (b, q, g)),
        out_shape=jax.ShapeDtypeStruct((bsz, s, NSA_HEADS * dh), F32),
        scratch_shapes=[pltpu.VMEM((s, LANES), BF16), pltpu.VMEM((s // tk, dh, tk), BF16)] * 2
        + [pltpu.VMEM((LANES, nq), BF16), pltpu.VMEM((s // SEL_BLOCK, tq), F32)]
        + [pltpu.VMEM((1, nq), F32), pltpu.VMEM((1, nq), F32), pltpu.VMEM((dh, nq), F32)] * 2
        + [pltpu.VMEM((tk, nq), F32)] * nbuf + [pltpu.VMEM((tk, nq), BF16)] * nbuf + [pltpu.VMEM((1, nq), F32)] * nbuf,
        compiler_params=_cparams(("parallel", "parallel", "arbitrary")),
        name="nsa_attention",
    )(slopes, proj_nsa, kvc, kvc, proj_nsa, proj_nsa, proj_nsa, proj_nsa, proj_gla)


def _gla_kernel(q_ref, k_ref, v_ref, og_ref, lr_ref, wg_ref, bg_ref, nw_ref, o_ref, st_scr, la_scr, b_scr):
    rows_n = q_ref.shape[1]
    c_n, sub = GLA_CHUNK, GLA_SUB
    nh, dk, dv = GLA_HEADS, GLA_DK, GLA_DV

    @pl.when(pl.program_id(1) == 0)
    def _():
        st_scr[...] = jnp.zeros(st_scr.shape, F32)

    z = _dot_hi_lo(lr_ref[0], wg_ref[...]) + bg_ref[...]
    la_scr[...] = (jnp.minimum(z, 0.0) - jnp.log1p(jnp.exp(-jnp.abs(z)))) * (1.0 / GLA_GATE_NORM)
    tril = jnp.where(lax.broadcasted_iota(I32, (c_n, c_n), 0) >= lax.broadcasted_iota(I32, (c_n, c_n), 1), 1.0, 0.0)
    tril3 = jnp.concatenate([tril.astype(BF16)] * 3, axis=1)
    row_c = lax.broadcasted_iota(I32, (c_n, 1), 0)
    row_s = lax.broadcasted_iota(I32, (sub, 1), 0)
    lane_c = lax.broadcasted_iota(I32, (1, c_n), 1)
    nw = nw_ref[...]
    hk = [slice(h * dk, (h + 1) * dk) for h in range(nh)]
    hv = [slice(h * dv, (h + 1) * dv) for h in range(nh)]

    def chunk(c, carry):
        r0 = pl.multiple_of(c * c_n, c_n)
        rows = pl.ds(r0, c_n)
        qc = q_ref[0, rows, :] * (dk ** -0.5)
        kc = k_ref[0, rows, :]
        vc = [v_ref[0, rows, hv[h]].astype(BF16) for h in range(nh)]
        pieces, rest = [], la_scr[rows, :]
        for _ in range(3):
            pieces.append(rest.astype(BF16))
            rest = rest - pieces[-1].astype(F32)
        b = jnp.dot(tril3, jnp.concatenate(pieces, axis=0), preferred_element_type=F32)
        b_scr[...] = b
        st = [st_scr[h] for h in range(nh)]
        q_e = (qc * jnp.exp(b)).astype(BF16)
        o = [_nt_dot(q_e[:, hk[h]], st[h].astype(BF16)) for h in range(nh)]
        strips = [[] for _ in range(nh)]
        for blk in range(c_n // sub):
            lo = blk * sub
            q_i = qc[lo:lo + sub]
            b_i = b[lo:lo + sub]
            a = [jnp.zeros((sub, c_n), F32) for _ in range(nh)]
            if blk > 0:
                b_r = b_scr[lo - 1:lo, :]
                q_d = (q_i * jnp.exp(b_i - b_r)).astype(BF16)
                k_d = (kc * jnp.exp(jnp.where(row_c < lo, b_r - b, -jnp.inf))).astype(BF16)
                a = [_nt_dot(q_d[:, hk[h]], k_d[:, hk[h]]) for h in range(nh)]
            for j in range(sub):
                b_j = b_scr[lo + j:lo + j + 1, :]
                k_j = k_ref[0, pl.ds(r0 + lo + j, 1), :]
                prod = q_i * k_j * jnp.exp(jnp.where(row_s >= j, b_i - b_j, -jnp.inf))
                for h in range(nh):
                    col = jnp.sum(prod[:, hk[h]], axis=-1, keepdims=True)
                    a[h] = jnp.where(lane_c == lo + j, col, a[h]) if blk == 0 else (
                        a[h] + jnp.where(lane_c == lo + j, col, 0.0))
            for h in range(nh):
                strips[h].append(a[h])
        for h in range(nh):
            attn = jnp.concatenate(strips[h], axis=0)
            o[h] = o[h] + jnp.dot(attn.astype(BF16), vc[h], preferred_element_type=F32)
        b_last = b_scr[c_n - 1:c_n, :]
        k_dec = (kc * jnp.exp(b_last - b)).astype(BF16)
        decay = jnp.exp(b_last)
        for h in range(nh):
            st_scr[h] = st[h] * decay[:, hk[h]] + lax.dot_general(
                vc[h], k_dec[:, hk[h]], (((0,), (0,)), ((), ())), preferred_element_type=F32)
        for h in range(nh):
            og = og_ref[0, rows, hv[h]]
            on = o[h] * lax.rsqrt(jnp.mean(o[h] * o[h], axis=-1, keepdims=True) + EPS) * nw
            o_ref[0, rows, hv[h]] = on * (og * jax.nn.sigmoid(og))
        return carry

    lax.fori_loop(0, rows_n // c_n, chunk, 0)


def _gla(proj_gla, wg_pad, bg, nw):
    bsz, s, _ = proj_gla.shape
    nh, dk, dv = GLA_HEADS, GLA_DK, GLA_DV
    ts = GLA_TS
    wk, wv = nh * dk, nh * dv
    return pl.pallas_call(
        _gla_kernel,
        grid=(bsz, s // ts),
        in_specs=[pl.BlockSpec((1, ts, wk), lambda b, i: (b, i, GLA_Q_OFF // wk)),
                  pl.BlockSpec((1, ts, wk), lambda b, i: (b, i, GLA_K_OFF // wk)),
                  pl.BlockSpec((1, ts, wv), lambda b, i: (b, i, GLA_V_OFF // wv)),
                  pl.BlockSpec((1, ts, wv), lambda b, i: (b, i, GLA_OG_OFF // wv)),
                  pl.BlockSpec((1, ts, LANES), lambda b, i: (b, i, GLA_MISC_OFF // LANES)),
                  pl.BlockSpec((LANES, wk), lambda b, i: (0, 0)),
                  pl.BlockSpec((1, wk), lambda b, i: (0, 0)),
                  pl.BlockSpec((1, dv), lambda b, i: (0, 0))],
        out_specs=pl.BlockSpec((1, ts, wv), lambda b, i: (b, i, 0)),
        out_shape=jax.ShapeDtypeStruct((bsz, s, wv), F32),
        scratch_shapes=[pltpu.VMEM((nh, dv, dk), F32), pltpu.VMEM((ts, wk), F32), pltpu.VMEM((GLA_CHUNK, wk), F32)],
        compiler_params=_cparams(("parallel", "arbitrary")),
        name="gla",
    )(proj_gla, proj_gla, proj_gla, proj_gla, proj_gla, wg_pad, bg, nw)


def _outproj_kernel(nsa_ref, gla_ref, x_ref, wo_ref, g1_ref, sc_ref, sh_ref, nw_ref, wr_ref, br_ref,
                    x1_ref, h_ref, route_ref):
    half = nsa_ref.shape[2]
    acc = jnp.dot(nsa_ref[0].astype(BF16), wo_ref[:half, :], preferred_element_type=F32)
    acc = acc + jnp.dot(gla_ref[0].astype(BF16), wo_ref[half:, :], preferred_element_type=F32)
    x1 = x_ref[0] + g1_ref[0, 0] * acc
    x1_ref[0] = x1
    h = _modulated_norm(x1, nw_ref[...], sc_ref[0, 0], sh_ref[0, 0])
    h_ref[0] = h
    h_hi = h.astype(BF16)
    h_lo = (h - h_hi.astype(F32)).astype(BF16)
    t = jnp.dot(h_hi, wr_ref[...], preferred_element_type=F32)
    logits = (t[:, :LANES] + t[:, LANES:] + jnp.dot(h_lo, wr_ref[:, :LANES], preferred_element_type=F32)
              + br_ref[...])
    lane = lax.broadcasted_iota(I32, (1, LANES), 1)
    ninf = -jnp.inf
    is_g = (lane >= N_EXPERTS) & (lane < N_EXPERTS + N_GROUPS)
    gl = jnp.where(is_g, logits, ninf)
    ge = jnp.exp(gl - jnp.max(gl, axis=-1, keepdims=True))
    gp = ge / jnp.sum(ge, axis=-1, keepdims=True)
    gp_max = jnp.max(gp, axis=-1, keepdims=True)
    grp = jnp.min(jnp.where((gp == gp_max) & is_g, lane - N_EXPERTS, LANES), axis=-1, keepdims=True)
    in_grp = (lane // EXPERTS_PER_GROUP == grp) & (lane < N_EXPERTS)
    el = jnp.where(in_grp, logits, ninf)
    v1 = jnp.max(el, axis=-1, keepdims=True)
    i1 = jnp.min(jnp.where(el == v1, lane, LANES), axis=-1, keepdims=True)
    el2 = jnp.where(lane == i1, ninf, el)
    v2 = jnp.max(el2, axis=-1, keepdims=True)
    i2 = jnp.min(jnp.where(el2 == v2, lane, LANES), axis=-1, keepdims=True)
    e2 = jnp.exp(v2 - v1)
    den = 1.0 + e2
    w1 = gp_max * (1.0 / den)
    w2 = gp_max * (e2 / den)
    route_ref[0] = jnp.where(lane == 0, i1.astype(F32), jnp.where(lane == 1, i2.astype(F32), jnp.where(
        lane == 2, w1, jnp.where(lane == 3, w2, 0.0))))


def _outproj(o_nsa, o_gla, x, wo, mod4, nw, wr, br):
    bsz, s, d = x.shape
    tm = OUT_TM
    half = o_nsa.shape[2]

    def mod_spec(idx):
        return pl.BlockSpec((1, 1, 1, d), lambda b, i: (b, idx, 0, 0))

    row = lambda w: pl.BlockSpec((1, tm, w), lambda b, i: (b, i, 0))
    return pl.pallas_call(
        _outproj_kernel,
        grid=(bsz, s // tm),
        in_specs=[row(half), row(half), row(d),
                  pl.BlockSpec((2 * half, d), lambda b, i: (0, 0), pipeline_mode=pl.Buffered(1)),
                  mod_spec(2), mod_spec(4), mod_spec(3),
                  pl.BlockSpec((1, d), lambda b, i: (0, 0)),
                  pl.BlockSpec((d, 2 * LANES), lambda b, i: (0, 0)),
                  pl.BlockSpec((1, LANES), lambda b, i: (0, 0))],
        out_specs=[row(d), row(d), row(LANES)],
        out_shape=[jax.ShapeDtypeStruct((bsz, s, d), F32), jax.ShapeDtypeStruct((bsz, s, d), F32),
                   jax.ShapeDtypeStruct((bsz, s, LANES), F32)],
        compiler_params=_cparams(("parallel", "parallel")),
        name="outproj_router",
    )(o_nsa, o_gla, x, wo, mod4, mod4, mod4, nw, wr, br)


def _rank_kernel(route_ref, dest_ref, meta_ref, rank_scr):
    n = route_ref.shape[0]
    tm = RANK_TM
    lane_i = lax.broadcasted_iota(I32, (1, LANES), 1)
    lane = lane_i.astype(F32)
    strict = jnp.where(lax.broadcasted_iota(I32, (tm, tm), 0) > lax.broadcasted_iota(I32, (tm, tm), 1),
                       1.0, 0.0).astype(BF16)

    def two_lanes(a, b):
        return jnp.where(lane_i == 0, a, jnp.where(lane_i == 1, b, 0.0))

    def pick(e, table):
        return jnp.sum(jnp.where(lane == e, table, 0.0), axis=-1, keepdims=True)

    def count(i, seen):
        r0 = pl.multiple_of(i * tm, tm)
        rt = route_ref[pl.ds(r0, tm), :]
        e1, e2 = rt[:, 0:1], rt[:, 1:2]
        member = jnp.where(lane == e1, 1.0, jnp.where(lane == e2, 1.0, 0.0))
        before = jnp.dot(strict, member.astype(BF16), preferred_element_type=F32) + seen
        rank_scr[pl.ds(r0, tm), :] = two_lanes(pick(e1, before), pick(e2, before))
        return seen + jnp.sum(member, axis=0, keepdims=True)

    counts = lax.fori_loop(0, n // tm, count, jnp.zeros((1, LANES), F32))
    ntile = jnp.floor((counts + (MOE_TB - 1)) * (1.0 / MOE_TB))
    incl = jnp.where(lax.broadcasted_iota(I32, (LANES, LANES), 0) <= lax.broadcasted_iota(I32, (LANES, LANES), 1),
                     1.0, 0.0).astype(BF16)
    tile_end = jnp.dot(jnp.broadcast_to(ntile, (8, LANES)).astype(BF16), incl,
                       preferred_element_type=F32)[0:1]
    row_start = (tile_end - ntile) * MOE_TB

    def place(i, carry):
        r0 = pl.multiple_of(i * tm, tm)
        rt = route_ref[pl.ds(r0, tm), :]
        rk = rank_scr[pl.ds(r0, tm), :]
        d1 = pick(rt[:, 0:1], row_start) + rk[:, 0:1]
        d2 = pick(rt[:, 1:2], row_start) + rk[:, 1:2]
        dest_ref[pl.ds(r0, tm), :] = two_lanes(d1, d2).astype(I32)
        return carry

    lax.fori_loop(0, n // tm, place, 0)
    trow = lax.broadcasted_iota(I32, (meta_ref.shape[0], 1), 0).astype(F32)
    texp = jnp.sum(jnp.where((tile_end <= trow) & (lane_i < N_EXPERTS), 1.0, 0.0), axis=-1, keepdims=True)
    texp = jnp.minimum(texp, N_EXPERTS - 1.0)
    used = pick(N_EXPERTS - 1.0, tile_end)
    diag = lax.broadcasted_iota(I32, (meta_ref.shape[0], LANES), 0) == lane_i
    end_rows = jnp.sum(jnp.where(diag, tile_end, 0.0), axis=-1, keepdims=True)
    ntile_rows = jnp.sum(jnp.where(diag, ntile, 0.0), axis=-1, keepdims=True)
    meta_ref[...] = jnp.where(lane_i == 2, end_rows, jnp.where(lane_i == 3, ntile_rows, two_lanes(
        texp, jnp.broadcast_to(used, texp.shape)))).astype(I32)


def _rank(route):
    n = route.shape[0]
    return pl.pallas_call(
        _rank_kernel,
        out_shape=[jax.ShapeDtypeStruct((n, LANES), I32), jax.ShapeDtypeStruct((LANES, LANES), I32)],
        scratch_shapes=[pltpu.VMEM((n, LANES), F32)],
        compiler_params=pltpu.CompilerParams(vmem_limit_bytes=VMEM_LIMIT),
        name="moe_rank",
    )(route)


def _dispatch_kernel(dest_ref, ends_ref, h_ref, xs_ref, zero_scr, sem, zsem):
    i = pl.program_id(0)
    tm = h_ref.shape[0]
    tb = zero_scr.shape[0]

    @pl.when(i == 0)
    def _():
        zero_scr[...] = jnp.zeros(zero_scr.shape, F32)

        def zero_copy(e):
            r0 = pl.multiple_of((ends_ref[e] - 1) * tb, tb)
            return pltpu.make_async_copy(zero_scr, xs_ref.at[pl.ds(r0, tb)], zsem)

        def start(e, carry):
            @pl.when(ends_ref[N_EXPERTS + e] > 0)
            def _():
                zero_copy(e).start()
            return carry

        def wait(e, carry):
            @pl.when(ends_ref[N_EXPERTS + e] > 0)
            def _():
                zero_copy(e).wait()
            return carry

        def tail_copy(t):
            return pltpu.make_async_copy(zero_scr, xs_ref.at[pl.ds(pl.multiple_of(t * tb, tb), tb)], zsem)

        def tail_start(t, carry):
            tail_copy(t).start()
            return carry

        def tail_wait(t, carry):
            tail_copy(t).wait()
            return carry

        used = ends_ref[2 * N_EXPERTS]
        lax.fori_loop(0, N_EXPERTS, start, 0)
        lax.fori_loop(used, xs_ref.shape[0] // tb, tail_start, 0)
        lax.fori_loop(0, N_EXPERTS, wait, 0)
        lax.fori_loop(used, xs_ref.shape[0] // tb, tail_wait, 0)

    def issue(r, carry):
        p = (i * tm + r) * TOP_K
        for k in range(TOP_K):
            pltpu.make_async_copy(h_ref.at[pl.ds(r, 1)], xs_ref.at[pl.ds(dest_ref[p + k], 1)], sem).start()
        return carry

    lax.fori_loop(0, tm, issue, 0, unroll=DMA_UNROLL)
    for k in range(TOP_K):
        pltpu.make_async_copy(h_ref, xs_ref.at[pl.ds(0, tm)], sem).wait()


def _dispatch(dest_flat, ends_flat, h, cap):
    n, d = h.shape
    tm = DISPATCH_TM
    return pl.pallas_call(
        _dispatch_kernel,
        grid_spec=pltpu.PrefetchScalarGridSpec(
            num_scalar_prefetch=2, grid=(n // tm,),
            in_specs=[pl.BlockSpec((tm, d), lambda i, dst, ends: (i, 0))],
            out_specs=pl.BlockSpec(memory_space=pl.ANY),
            scratch_shapes=[pltpu.VMEM((MOE_TB, d), F32), pltpu.SemaphoreType.DMA(()), pltpu.SemaphoreType.DMA(())]),
        out_shape=jax.ShapeDtypeStruct((cap, d), F32),
        compiler_params=_cparams(("arbitrary",)),
        name="moe_dispatch",
    )(dest_flat, ends_flat, h)


def _ffn_kernel(meta_ref, x_ref, wg_hbm, wu_hbm, wd_hbm, y_ref, wg_st, wu_st, wd_st, wgb, wub, wdb, wsem):
    t = pl.program_id(0)
    ntile = pl.num_programs(0)
    n_used = meta_ref[ntile]
    e = meta_ref[t]
    e_prev = meta_ref[jnp.maximum(t - 1, 0)]
    active = t < n_used

    def fetch(expert):
        return [pltpu.make_async_copy(src.at[expert], dst, wsem)
                for src, dst in ((wg_hbm, wg_st), (wu_hbm, wu_st), (wd_hbm, wd_st))]

    @pl.when(t == 0)
    def _():
        for cp in fetch(e):
            cp.start()

    @pl.when(active & ((t == 0) | (e != e_prev)))
    def _():
        for cp in fetch(e):
            cp.wait()
        for src, dst in ((wg_st, wgb), (wu_st, wub), (wd_st, wdb)):
            def cast_rows(c, carry, src=src, dst=dst):
                rows = pl.ds(pl.multiple_of(c * CAST_ROWS, CAST_ROWS), CAST_ROWS)
                dst[rows, :] = src[rows, :].astype(BF16)
                return carry

            lax.fori_loop(0, src.shape[0] // CAST_ROWS, cast_rows, 0, unroll=2)
        end = meta_ref[ntile + 1 + e]

        @pl.when(end < n_used)
        def _():
            for cp in fetch(meta_ref[end]):
                cp.start()

    @pl.when(active)
    def _():
        x = x_ref[...].astype(BF16)
        gate = jnp.dot(x, wgb[...], preferred_element_type=F32)
        up = jnp.dot(x, wub[...], preferred_element_type=F32)
        act = (gate * jax.nn.sigmoid(gate)) * up
        y_ref[...] = jnp.dot(act.astype(BF16), wdb[...], preferred_element_type=F32)

    @pl.when(jnp.logical_not(active))
    def _():
        y_ref[...] = jnp.zeros(y_ref.shape, F32)


def _ffn(meta_flat, xs, wg, wu, wd):
    cap, d = xs.shape
    ff = wg.shape[2]
    tb = MOE_TB
    ntile = cap // tb
    return pl.pallas_call(
        _ffn_kernel,
        grid_spec=pltpu.PrefetchScalarGridSpec(
            num_scalar_prefetch=1, grid=(ntile,),
            in_specs=[pl.BlockSpec((tb, d), lambda t, m: (jnp.minimum(t, m[ntile] - 1), 0)),
                      pl.BlockSpec(memory_space=pl.ANY), pl.BlockSpec(memory_space=pl.ANY),
                      pl.BlockSpec(memory_space=pl.ANY)],
            out_specs=pl.BlockSpec((tb, d), lambda t, m: (t, 0)),
            scratch_shapes=[pltpu.VMEM((d, ff), F32), pltpu.VMEM((d, ff), F32), pltpu.VMEM((ff, d), F32),
                            pltpu.VMEM((d, ff), BF16), pltpu.VMEM((d, ff), BF16), pltpu.VMEM((ff, d), BF16),
                            pltpu.SemaphoreType.DMA(())]),
        out_shape=jax.ShapeDtypeStruct((cap, d), F32),
        compiler_params=_cparams(("arbitrary",)),
        name="moe_ffn",
    )(meta_flat, xs, wg, wu, wd)


def _combine_kernel(dest_ref, y_ref, x1_ref, route_ref, g2_ref, nf_ref, o_ref, ybuf, sem):
    i = pl.program_id(0)
    nstep = pl.num_programs(0)
    tm = x1_ref.shape[0]

    def issue(tile, slot):
        def body(r, carry):
            p = (tile * tm + r) * TOP_K
            for k in range(TOP_K):
                pltpu.make_async_copy(y_ref.at[pl.ds(dest_ref[p + k], 1)], ybuf.at[slot, k, pl.ds(r, 1)],
                                      sem.at[slot]).start()
            return carry

        lax.fori_loop(0, tm, body, 0, unroll=DMA_UNROLL)

    @pl.when(i == 0)
    def _():
        issue(0, 0)

    @pl.when(i + 1 < nstep)
    def _():
        issue(i + 1, (i + 1) % 2)

    slot = i % 2
    for k in range(TOP_K):
        pltpu.make_async_copy(y_ref.at[pl.ds(0, tm)], ybuf.at[slot, k], sem.at[slot]).wait()
    rt = route_ref[...]
    moe = rt[:, 2:3] * ybuf[slot, 0] + rt[:, 3:4] * ybuf[slot, 1]
    xo = x1_ref[...] + g2_ref[0, 0] * moe
    o_ref[...] = xo * lax.rsqrt(jnp.mean(xo * xo, axis=-1, keepdims=True) + EPS) * nf_ref[...]


def _combine(dest_flat, y, x1, route, mod4, nf, seq):
    n, d = x1.shape
    tm = COMB_TM
    tiles_per_seq = seq // tm
    return pl.pallas_call(
        _combine_kernel,
        grid_spec=pltpu.PrefetchScalarGridSpec(
            num_scalar_prefetch=1, grid=(n // tm,),
            in_specs=[pl.BlockSpec(memory_space=pl.ANY),
                      pl.BlockSpec((tm, d), lambda i, dst: (i, 0)),
                      pl.BlockSpec((tm, LANES), lambda i, dst: (i, 0)),
                      pl.BlockSpec((1, 1, 1, d), lambda i, dst: (i // tiles_per_seq, 5, 0, 0)),
                      pl.BlockSpec((1, d), lambda i, dst: (0, 0))],
            out_specs=pl.BlockSpec((tm, d), lambda i, dst: (i, 0)),
            scratch_shapes=[pltpu.VMEM((2, TOP_K, tm, d), F32), pltpu.SemaphoreType.DMA((2,))]),
        out_shape=jax.ShapeDtypeStruct((n, d), F32),
        compiler_params=_cparams(("arbitrary",)),
        name="moe_combine",
    )(dest_flat, y, x1, route, mod4, nf)


def _alibi_slopes():
    n = NSA_HEADS
    full = jnp.asarray(2.0 ** (-8.0 * np.arange(1, n + 1) / n), dtype=F32)
    pieces, rest = [], full * LOG2E
    for _ in range(3):
        piece = rest.astype(BF16).astype(F32)
        pieces.append(piece)
        rest = rest - piece
    return jnp.concatenate([full] + pieces)


def _layer(x, c, w_ada, b_ada, norm1_w, w_in, cmp_pos_k, cmp_w1_k, cmp_w2_k, cmp_pos_v, cmp_w1_v, cmp_w2_v,
           gla_w_gate2, gla_b_gate, gla_norm_w, w_out, norm2_w, w_rg, b_rg, w_re, b_re, w_eg, w_eu, w_ed):
    bsz, s, d = x.shape
    dh = NSA_HEAD_DIM
    mod4 = _adaln(c, w_ada, b_ada).reshape(bsz, 6, 1, d)

    o_gate = NSA_COLS
    o_gla = o_gate + NSA_GATE_COLS
    o_lr = o_gla + 2 * GLA_HEADS * GLA_DK + 2 * GLA_HEADS * GLA_DV
    w_nsa, w_gla = _prep_w_in(w_in.T, o_gate, o_gla, o_lr)
    nw1 = norm1_w.reshape(1, d)
    proj_nsa, proj_gla = _inproj(x, mod4, nw1, w_nsa, w_gla, sc_idx=1, sh_idx=0)

    pos = jnp.stack([cmp_pos_k, cmp_pos_v]).reshape(2, 2, CMP_STRIDE * dh)
    kvc = _compress(proj_nsa, pos, jnp.stack([cmp_w1_k, cmp_w1_v]), jnp.stack([cmp_w2_k, cmp_w2_v]))
    o_nsa = _nsa(_alibi_slopes(), proj_nsa, kvc, proj_gla)

    wg_pad = jnp.zeros((LANES, GLA_HEADS * GLA_DK), F32).at[
        NSA_GATE_COLS:NSA_GATE_COLS + GLA_GATE_RANK].set(gla_w_gate2)
    o_gla_out = _gla(proj_gla, wg_pad, gla_b_gate.reshape(1, -1), gla_norm_w.reshape(1, -1))

    wr = jnp.concatenate([w_re, w_rg, jnp.zeros((d, LANES - N_EXPERTS - N_GROUPS), F32)], axis=1)
    br = jnp.concatenate([b_re, b_rg, jnp.zeros((LANES - N_EXPERTS - N_GROUPS,), F32)]).reshape(1, LANES)
    wr_hi = wr.astype(BF16)
    wr_cat = jnp.concatenate([wr_hi, (wr - wr_hi.astype(F32)).astype(BF16)], axis=1)
    x1, h2, route = _outproj(o_nsa, o_gla_out, x, w_out.astype(BF16), mod4, norm2_w.reshape(1, d), wr_cat, br)

    n = bsz * s
    npair = n * TOP_K
    cap = npair + N_EXPERTS * MOE_TB
    ntile = cap // MOE_TB
    route2 = route.reshape(n, LANES)
    dest, meta = _rank(route2)
    dest_flat = dest[:, :TOP_K].reshape(npair)
    meta_flat = jnp.concatenate([meta[:ntile, 0], meta[:1, 1], meta[:N_EXPERTS, 2]])
    ends_flat = jnp.concatenate([meta[:N_EXPERTS, 2], meta[:N_EXPERTS, 3], meta[:1, 1]])
    xs = _dispatch(dest_flat, ends_flat, h2.reshape(n, d), cap)
    y = _ffn(meta_flat, xs, w_eg, w_eu, w_ed)
    return x1.reshape(n, d), y, dest_flat, route2, mod4


def kernel(x, c, w_ada, b_ada, norm1_w, w_in, cmp_pos_k, cmp_w1_k, cmp_w2_k, cmp_pos_v, cmp_w1_v, cmp_w2_v,
           gla_w_gate2, gla_b_gate, gla_norm_w, w_out, norm2_w, w_router_group, b_router_group, w_router_expert,
           b_router_expert, w_expert_gate, w_expert_up, w_expert_down, norm_f_w):
    bsz, s, d = x.shape
    assert w_ada.shape[0] == 1, "single layer"
    x1, y, dest_flat, route2, mod4 = _layer(
        x, c, w_ada[0], b_ada[0], norm1_w[0], w_in[0], cmp_pos_k[0], cmp_w1_k[0], cmp_w2_k[0], cmp_pos_v[0],
        cmp_w1_v[0], cmp_w2_v[0], gla_w_gate2[0], gla_b_gate[0], gla_norm_w[0], w_out[0], norm2_w[0],
        w_router_group[0], b_router_group[0], w_router_expert[0], b_router_expert[0],
        w_expert_gate[0], w_expert_up[0], w_expert_down[0])
    out = _combine(dest_flat, y, x1, route2, mod4, norm_f_w.reshape(1, d), s)
    return out.reshape(bsz, s, d)
```

```python
import functools

import numpy as np
import jax
import jax.numpy as jnp
from jax import lax
from jax.experimental import pallas as pl
from jax.experimental.pallas import tpu as pltpu

F32 = jnp.float32
BF16 = jnp.bfloat16
I32 = jnp.int32

D_MODEL = 2048
NSA_HEAD_DIM = 64
NSA_HEADS = 16
NSA_KV_HEADS = 4
NSA_Q_PER_KV = 4
CMP_BLOCK = 32
CMP_STRIDE = 16
SEL_BLOCK = 64
N_SEL = 16
WINDOW = 512
N_BRANCH = 3
GLA_HEADS = 4
GLA_DV = 256
GLA_DK = 128
GLA_GATE_RANK = 16
GLA_GATE_NORM = 16.0
GLA_CHUNK = 64
GLA_SUB = 8
N_GROUPS = 4
EXPERTS_PER_GROUP = 8
N_EXPERTS = 32
TOP_K = 2
EXPERT_FF = 512
EPS = 1e-6
NEG = -1e30
FORCE = 1e30
LOG2E = 1.4426950408889634
MASKED = 2.0 ** 100

NSA_Q_COLS = NSA_HEADS * NSA_HEAD_DIM
NSA_KV_COLS = 2 * N_BRANCH * NSA_KV_HEADS * NSA_HEAD_DIM
NSA_GATE_COLS = N_BRANCH * NSA_HEADS
NSA_COLS = NSA_Q_COLS + NSA_KV_COLS
NSA_SLOTS = NSA_COLS // NSA_HEAD_DIM
GLA_Q_OFF = 0
GLA_K_OFF = GLA_HEADS * GLA_DK
GLA_V_OFF = 2 * GLA_HEADS * GLA_DK
GLA_OG_OFF = GLA_V_OFF + GLA_HEADS * GLA_DV
GLA_MISC_OFF = GLA_OG_OFF + GLA_HEADS * GLA_DV
LANES = 128
GLA_COLS = GLA_MISC_OFF + LANES

VMEM_LIMIT = 56 * 1024 * 1024

ADA_TN = 768
PREP_TR = 256
PREP_STEP = 512
INPROJ_TM = 256
INPROJ_TN = 512
NSA_TQ = 256
NSA_TK = 256
GLA_TS = 512
OUT_TM = 512
RANK_TM = 256
MOE_TB = 256
DISPATCH_TM = 1024
COMB_TM = 512
DMA_UNROLL = 8


def _cparams(sem):
    return pltpu.CompilerParams(dimension_semantics=sem, vmem_limit_bytes=VMEM_LIMIT)


def _adaln_kernel(c_ref, w_ref, b_ref, o_ref):
    c = c_ref[...]
    s = (c * jax.nn.sigmoid(c)).astype(BF16)
    o_ref[...] = jnp.dot(s, w_ref[...].astype(BF16), preferred_element_type=F32) + b_ref[...]


def _adaln(c, w, b):
    nb, d = c.shape
    n = w.shape[1]
    return pl.pallas_call(
        _adaln_kernel,
        grid=(n // ADA_TN,),
        in_specs=[pl.BlockSpec((nb, d), lambda j: (0, 0)),
                  pl.BlockSpec((d, ADA_TN), lambda j: (0, j)),
                  pl.BlockSpec((1, ADA_TN), lambda j: (0, j))],
        out_specs=pl.BlockSpec((nb, ADA_TN), lambda j: (0, j)),
        out_shape=jax.ShapeDtypeStruct((nb, n), F32),
        compiler_params=_cparams(("parallel",)),
        name="adaln",
    )(c, w, b.reshape(1, n))


def _modulated_norm(x, nw, sc, sh):
    ms = jnp.mean(x * x, axis=-1, keepdims=True)
    h = x * lax.rsqrt(ms + EPS) * nw
    return h * (1.0 + sc) + sh


def _prep_w_in_kernel(wt_ref, wn_ref, wg_ref, *, o_gate, o_gla, o_lr):
    tr = wt_ref.shape[1]
    step = PREP_STEP

    def put(dst, c0, rows):
        dst[:, c0:c0 + rows.shape[0]] = rows.T.astype(BF16)

    for c in range(0, o_gate, step):
        put(wn_ref, c, wt_ref[c:c + step, :])
    for c in range(0, o_lr - o_gla, step):
        put(wg_ref, c, wt_ref[o_gla + c:o_gla + c + step, :])
    pad = wg_ref.shape[1] - (wt_ref.shape[0] - o_gate)
    misc = jnp.concatenate([wt_ref[o_gate:o_gla, :], wt_ref[o_lr:, :], jnp.zeros((pad, tr), F32)], axis=0)
    put(wg_ref, o_lr - o_gla, misc)


def _prep_w_in(w_in_t, o_gate, o_gla, o_lr):
    n, d = w_in_t.shape
    tr = PREP_TR
    assert o_gate % PREP_STEP == 0 and (o_lr - o_gla) % PREP_STEP == 0
    return pl.pallas_call(
        functools.partial(_prep_w_in_kernel, o_gate=o_gate, o_gla=o_gla, o_lr=o_lr),
        grid=(d // tr,),
        in_specs=[pl.BlockSpec((n, tr), lambda i: (0, i))],
        out_specs=[pl.BlockSpec((tr, NSA_COLS), lambda i: (i, 0)), pl.BlockSpec((tr, GLA_COLS), lambda i: (i, 0))],
        out_shape=[jax.ShapeDtypeStruct((d, NSA_COLS), BF16), jax.ShapeDtypeStruct((d, GLA_COLS), BF16)],
        compiler_params=_cparams(("parallel",)),
        name="prep_w_in",
    )(w_in_t)


def _inproj_kernel(x_ref, sc_ref, sh_ref, nw_ref, wn_ref, wg_ref, on_ref, og_ref):
    h = _modulated_norm(x_ref[0], nw_ref[...], sc_ref[0, 0], sh_ref[0, 0]).astype(BF16)
    dh = NSA_HEAD_DIM
    tn = INPROJ_TN
    for c in range(wn_ref.shape[1] // tn):
        acc = jnp.dot(h, wn_ref[:, c * tn:(c + 1) * tn], preferred_element_type=F32)
        for u in range(tn // dh):
            on_ref[0, c * (tn // dh) + u] = acc[:, u * dh:(u + 1) * dh]
    og_ref[0] = jnp.dot(h, wg_ref[...], preferred_element_type=F32)


def _inproj(x, mod4, nw, w_nsa, w_gla, *, sc_idx, sh_idx):
    bsz, s, d = x.shape
    n_nsa, n_gla = w_nsa.shape[1], w_gla.shape[1]
    tm = INPROJ_TM
    dh = NSA_HEAD_DIM
    return pl.pallas_call(
        _inproj_kernel,
        grid=(bsz, s // tm),
        in_specs=[pl.BlockSpec((1, tm, d), lambda b, i: (b, i, 0)),
                  pl.BlockSpec((1, 1, 1, d), lambda b, i: (b, sc_idx, 0, 0)),
                  pl.BlockSpec((1, 1, 1, d), lambda b, i: (b, sh_idx, 0, 0)),
                  pl.BlockSpec((1, d), lambda b, i: (0, 0)),
                  pl.BlockSpec((d, n_nsa), lambda b, i: (0, 0), pipeline_mode=pl.Buffered(1)),
                  pl.BlockSpec((d, n_gla), lambda b, i: (0, 0), pipeline_mode=pl.Buffered(1))],
        out_specs=[pl.BlockSpec((1, n_nsa // dh, tm, dh), lambda b, i: (b, 0, i, 0)),
                   pl.BlockSpec((1, tm, n_gla), lambda b, i: (b, i, 0))],
        out_shape=[jax.ShapeDtypeStruct((bsz, n_nsa // dh, s, dh), F32),
                   jax.ShapeDtypeStruct((bsz, s, n_gla), F32)],
        compiler_params=_cparams(("parallel", "parallel")),
        name="inproj",
    )(x, mod4, mod4, nw, w_nsa, w_gla)


def _hi_lo(x):
    hi = x.astype(BF16)
    return hi, (x - hi.astype(F32)).astype(BF16)


def _dot_hi_lo(x, w):
    x_hi, x_lo = _hi_lo(x)
    w_hi, w_lo = _hi_lo(w)
    return jnp.dot(jnp.concatenate([x_hi, x_lo, x_hi], axis=1), jnp.concatenate([w_hi, w_hi, w_lo], axis=0),
                   preferred_element_type=F32)


def _compress_kernel(a_ref, pos_ref, w1_ref, w2_ref, o_ref):
    ng = a_ref.shape[1]
    nch = a_ref.shape[2] // CMP_STRIDE
    a = jnp.concatenate([
        jnp.concatenate([a_ref[0, g, pl.ds(t, nch, stride=CMP_STRIDE), :] for t in range(CMP_STRIDE)], axis=1)
        for g in range(ng)], axis=0)
    pos = pos_ref[0]
    half = a.shape[1]
    y1 = _dot_hi_lo(a + pos[0:1], w1_ref[0, :half, :])
    y2 = _dot_hi_lo(a + pos[1:2], w1_ref[0, half:, :])
    h = y1 + pltpu.roll(y2, a.shape[0] - 1, axis=0)
    out = _dot_hi_lo(jax.nn.gelu(h), w2_ref[0])
    row = lax.broadcasted_iota(I32, (nch, out.shape[1]), 0)
    for g in range(ng):
        o_ref[0, g] = jnp.where(row < nch - 1, out[g * nch:(g + 1) * nch], 0.0)


def _compress(proj_nsa, pos, w1, w2):
    bsz, _, s, dh = proj_nsa.shape
    ng = NSA_KV_HEADS
    nch = s // CMP_STRIDE
    cw = CMP_STRIDE * dh
    return pl.pallas_call(
        _compress_kernel,
        grid=(bsz, 2),
        in_specs=[pl.BlockSpec((1, ng, s, dh), lambda b, t: (b, NSA_HEADS // ng + t, 0, 0)),
                  pl.BlockSpec((1, 2, cw), lambda b, t: (t, 0, 0)),
                  pl.BlockSpec((1, 2 * cw, dh), lambda b, t: (t, 0, 0)),
                  pl.BlockSpec((1, dh, dh), lambda b, t: (t, 0, 0))],
        out_specs=pl.BlockSpec((1, ng, nch, dh), lambda b, t: (b, t, 0, 0)),
        out_shape=jax.ShapeDtypeStruct((bsz, 2 * ng, nch, dh), F32),
        compiler_params=_cparams(("parallel", "parallel")),
        name="nsa_compress",
    )(proj_nsa, pos, w1, w2)


def _nt_dot(a, b, **kw):
    return lax.dot_general(a, b, (((1,), (1,)), ((), ())), preferred_element_type=F32, **kw)


def _nsa_kernel(slopes_ref, q_ref, kc_ref, vc_ref, ks_ref, vs_ref, kw_ref, vw_ref, gate_ref, o_ref,
                ksb, vst, kwb, vwt, q4_scr, notsel_scr, *scr):
    g = pl.program_id(1)
    qi = pl.program_id(2)
    tq_n = q_ref.shape[2]
    dh = NSA_HEAD_DIM
    nr = NSA_Q_PER_KV
    seq = ks_ref.shape[2]
    tk_n = NSA_TK
    nb = seq // SEL_BLOCK

    @pl.when(qi == 0)
    def _():
        row = lax.broadcasted_iota(I32, (seq, dh), 0)
        lane = lax.broadcasted_iota(I32, (seq, dh), 1)
        blk = row // SEL_BLOCK
        pos = jnp.where((lane >= nb) & (lane < nb + 3), (blk * SEL_BLOCK).astype(F32),
                        jnp.where((lane >= nb + 3) & (lane < nb + 6), (row % SEL_BLOCK).astype(F32), 0.0))
        ksb[...] = jnp.concatenate([jnp.where(lane == blk, -MASKED, pos), ks_ref[0, 0]], axis=1).astype(BF16)
        kwb[...] = jnp.concatenate([pos, kw_ref[0, 0]], axis=1).astype(BF16)
        for c in range(seq // tk_n):
            rows = slice(c * tk_n, (c + 1) * tk_n)
            for src, dst in ((vs_ref, vst), (vw_ref, vwt)):
                v = src[0, 0, rows, :]
                dst[c] = jnp.concatenate([v, v], axis=1).T[:dh].astype(BF16)

    t0 = qi * tq_n
    tq = t0 + lax.broadcasted_iota(I32, (1, tq_n), 1)
    slopes = [slopes_ref[g * nr + r] for r in range(nr)]
    scale = dh ** -0.5
    q_t = []
    for pair in range(nr // 2):
        both = jnp.concatenate([q_ref[0, 2 * pair], q_ref[0, 2 * pair + 1]], axis=1).T * scale
        q_t += [both[:dh], both[dh:]]

    ncp = kc_ref.shape[2]
    kc = kc_ref[0, 0]
    vc = vc_ref[0, 0]
    vc_t = jnp.concatenate([vc, vc], axis=1).T[:dh].astype(BF16)
    n_sub = lax.broadcasted_iota(I32, (ncp, 1), 0)
    blk_end = n_sub * CMP_STRIDE + (CMP_BLOCK - 1)
    center = n_sub.astype(F32) * CMP_STRIDE + (CMP_BLOCK - 1) / 2.0
    kc_hi, kc_lo = _hi_lo(kc)
    kc_cat = jnp.concatenate([kc_hi, kc_lo, kc_hi, jnp.zeros_like(kc_hi)], axis=1)
    q_hi, q_lo = _hi_lo(jnp.concatenate(q_t, axis=1))
    q_cat = jnp.concatenate([q_hi, q_hi, q_lo, jnp.zeros_like(q_hi)], axis=0)
    s = jnp.dot(kc_cat, q_cat, preferred_element_type=F32)
    tq_all = jnp.concatenate([tq] * nr, axis=1)
    slope_all = jnp.concatenate([jnp.full((1, tq_n), slopes[r], F32) for r in range(nr)], axis=1)
    valid_all = blk_end <= tq_all
    s = jnp.where(valid_all, s - slope_all * (tq_all.astype(F32) - center), NEG)
    e = jnp.exp(s - jnp.max(s, axis=0, keepdims=True))
    p = jnp.where(valid_all, e / jnp.sum(e, axis=0, keepdims=True), 0.0)
    psum = sum(p[:, r * tq_n:(r + 1) * tq_n] for r in range(nr))
    o_c_all = jnp.dot(vc_t, p.astype(BF16), preferred_element_type=F32)
    o_c = [o_c_all[:, r * tq_n:(r + 1) * tq_n] for r in range(nr)]

    n_sel = min(N_SEL, nb)
    notsel_scr[...] = jnp.zeros(notsel_scr.shape, F32)

    @pl.when((qi + 1) * tq_n > n_sel * SEL_BLOCK)
    def _():
        rowj = lax.broadcasted_iota(I32, (LANES, ncp), 0) * SEL_BLOCK
        coln = lax.broadcasted_iota(I32, (LANES, ncp), 1) * CMP_STRIDE
        overlap = jnp.where((coln < rowj + SEL_BLOCK) & (coln + CMP_BLOCK > rowj)
                            & (coln < (ncp - 1) * CMP_STRIDE) & (rowj < nb * SEL_BLOCK), 1.0, 0.0)
        ov = overlap.astype(BF16)
        imp = jnp.dot(jnp.concatenate([ov, ov], axis=1), jnp.concatenate(_hi_lo(psum), axis=0),
                      preferred_element_type=F32)[:nb]
        j_sub = lax.broadcasted_iota(I32, (nb, 1), 0)
        qblk = tq // SEL_BLOCK
        forced = (j_sub == 0) | (j_sub == qblk) | (j_sub == qblk - 1)
        imp = jnp.where(forced, FORCE, jnp.where(j_sub <= qblk, imp, NEG))
        cnt = jnp.zeros((nb, tq_n), F32)
        for i in range(nb):
            ci = imp[i:i + 1, :]
            tie = jnp.where(j_sub > i, 1.0, 0.0)
            cnt = cnt + jnp.where(ci > imp, 1.0, jnp.where(ci == imp, tie, 0.0))
        notsel_scr[...] = jnp.where(cnt < float(n_sel), 0.0, 1.0)

    notsel = notsel_scr[...]

    sub_h = lax.broadcasted_iota(I32, (dh - nb, 1), 0)
    for r in range(nr):
        scol = jnp.zeros((dh - nb, 1), F32)
        for i in range(3):
            piece = slopes_ref[(i + 1) * NSA_HEADS + g * nr + r]
            scol = jnp.where((sub_h == i) | (sub_h == 3 + i), piece, scol)
        q4_scr[:, r * tq_n:(r + 1) * tq_n] = jnp.concatenate(
            [notsel, jnp.broadcast_to(scol, (dh - nb, tq_n)), q_t[r] * LOG2E], axis=0).astype(BF16)

    nwt = WINDOW // tk_n
    nbuf = nwt + 2
    stats_s, stats_w = scr[0:3], scr[3:6]
    s_buf, p_buf, a_buf = scr[6:6 + nbuf], scr[6 + nbuf:6 + 2 * nbuf], scr[6 + 2 * nbuf:6 + 3 * nbuf]
    for m_ref, l_ref, acc_ref in (stats_s, stats_w):
        m_ref[...] = jnp.full(m_ref.shape, NEG, F32)
        l_ref[...] = jnp.zeros(l_ref.shape, F32)
        acc_ref[...] = jnp.zeros(acc_ref.shape, F32)
    key_i = lax.broadcasted_iota(I32, (tk_n, LANES), 0)
    qry_j = lax.broadcasted_iota(I32, (tk_n, LANES), 1)

    def scores(k_ref, kt, buf):
        k_tile = k_ref[kt * tk_n:(kt + 1) * tk_n, :]
        s_buf[buf][...] = jnp.dot(k_tile, q4_scr[...], preferred_element_type=F32)

    def softmax(buf, mode, stats):
        m_ref, l_ref, _ = stats
        for cb in range(nr * tq_n // LANES):
            cols = slice(cb * LANES, (cb + 1) * LANES)
            s = s_buf[buf][:, cols]
            if mode is not None:
                j = qry_j + (cb * LANES) % tq_n
                s = jnp.where(key_i <= j if mode == "causal" else key_i > j, s, -MASKED)
            m_prev = m_ref[:, cols]
            m_new = jnp.maximum(m_prev, jnp.max(s, axis=0, keepdims=True))
            alpha = jnp.exp2(m_prev - m_new)
            p = jnp.exp2(s - m_new)
            l_ref[:, cols] = alpha * l_ref[:, cols] + jnp.sum(p, axis=0, keepdims=True)
            m_ref[:, cols] = m_new
            a_buf[buf][:, cols] = alpha
            p_buf[buf][:, cols] = p.astype(BF16)

    def values(vt_ref, kt, buf, stats):
        acc_ref = stats[2]
        pv = jnp.dot(vt_ref[kt], p_buf[buf][...], preferred_element_type=F32)
        acc_ref[...] = acc_ref[...] * a_buf[buf][...] + pv

    def batch(jobs):
        for i in range(min(nbuf, len(jobs))):
            scores(jobs[i][0], jobs[i][2], i)
        for i, (_, vt_ref, kt, mode, stats) in enumerate(jobs):
            softmax(i % nbuf, mode, stats)
            values(vt_ref, kt, i % nbuf, stats)
            if i + nbuf < len(jobs):
                scores(jobs[i + nbuf][0], jobs[i + nbuf][2], i % nbuf)

    for k in range(seq // tq_n):
        @pl.when(qi == k)
        def _(k=k):
            jobs = [(ksb, vst, t, None, stats_s) for t in range(k)]
            jobs += [(kwb, vwt, k - back, "band" if back == nwt else None, stats_w)
                     for back in range(min(nwt, k), 0, -1)]
            jobs += [(kwb, vwt, k, "causal", stats_w), (ksb, vst, k, "causal", stats_s)]
            batch(jobs)

    gsel = jnp.where(lax.broadcasted_iota(I32, (LANES, LANES), 0)
                     == lax.broadcasted_iota(I32, (LANES, LANES), 1) + g * (nr * N_BRANCH), 1.0, 0.0)
    gs = gsel.astype(BF16)
    gates = jax.nn.sigmoid(jnp.dot(jnp.concatenate(_hi_lo(gate_ref[0]), axis=1), jnp.concatenate([gs, gs], axis=0),
                                   preferred_element_type=F32)).T
    (_, l_s, acc_s), (_, l_w, acc_w) = stats_s, stats_w
    for pair in range(nr // 2):
        o_t = []
        for r in (2 * pair, 2 * pair + 1):
            c0 = r * N_BRANCH
            cols = slice(r * tq_n, (r + 1) * tq_n)
            o_t.append(gates[c0:c0 + 1, :] * o_c[r] + (gates[c0 + 1:c0 + 2, :] / l_s[:, cols]) * acc_s[:, cols]
                       + (gates[c0 + 2:c0 + 3, :] / l_w[:, cols]) * acc_w[:, cols])
        o_ref[0, :, pair * LANES:(pair + 1) * LANES] = jnp.concatenate(o_t, axis=0).T


def _nsa(slopes, proj_nsa, kvc, proj_gla):
    bsz, _, s, dh = proj_nsa.shape
    g_n, nr = NSA_KV_HEADS, NSA_Q_PER_KV
    tq = NSA_TQ
    tk = NSA_TK
    assert tq == tk and WINDOW % tk == 0 and 2 * dh == LANES
    ncp = kvc.shape[2]
    kv0 = NSA_HEADS
    nq = nr * tq
    nbuf = WINDOW // tk + 2

    def kv_spec(i):
        return pl.BlockSpec((1, 1, s, dh), lambda b, g, q, i=i: (b, kv0 + i * g_n + g, 0, 0))

    return pl.pallas_call(
        _nsa_kernel,
        grid=(bsz, g_n, s // tq),
        in_specs=[pl.BlockSpec(memory_space=pltpu.SMEM),
                  pl.BlockSpec((1, nr, tq, dh), lambda b, g, q: (b, g, q, 0)),
                  pl.BlockSpec((1, 1, ncp, dh), lambda b, g, q: (b, g, 0, 0)),
                  pl.BlockSpec((1, 1, ncp, dh), lambda b, g, q: (b, g_n + g, 0, 0)),
                  kv_spec(2), kv_spec(3), kv_spec(4), kv_spec(5),
                  pl.BlockSpec((1, tq, LANES), lambda b, g, q: (b, q, GLA_MISC_OFF // LANES))],
        out_specs=pl.BlockSpec((1, tq, nr * dh), lambda b, g, q: (b, q, g)),
        out_shape=jax.ShapeDtypeStruct((bsz, s, NSA_HEADS * dh), F32),
        scratch_shapes=[pltpu.VMEM((s, LANES), BF16), pltpu.VMEM((s // tk, dh, tk), BF16)] * 2
        + [pltpu.VMEM((LANES, nq), BF16), pltpu.VMEM((s // SEL_BLOCK, tq), F32)]
        + [pltpu.VMEM((1, nq), F32), pltpu.VMEM((1, nq), F32), pltpu.VMEM((dh, nq), F32)] * 2
        + [pltpu.VMEM((tk, nq), F32)] * nbuf + [pltpu.VMEM((tk, nq), BF16)] * nbuf + [pltpu.VMEM((1, nq), F32)] * nbuf,
        compiler_params=_cparams(("parallel", "parallel", "arbitrary")),
        name="nsa_attention",
    )(slopes, proj_nsa, kvc, kvc, proj_nsa, proj_nsa, proj_nsa, proj_nsa, proj_gla)


def _gla_kernel(q_ref, k_ref, v_ref, og_ref, lr_ref, wg_ref, bg_ref, nw_ref, o_ref, st_scr, la_scr, b_scr):
    rows_n = q_ref.shape[1]
    c_n, sub = GLA_CHUNK, GLA_SUB
    nh, dk, dv = GLA_HEADS, GLA_DK, GLA_DV

    @pl.when(pl.program_id(1) == 0)
    def _():
        st_scr[...] = jnp.zeros(st_scr.shape, F32)

    z = _dot_hi_lo(lr_ref[0], wg_ref[...]) + bg_ref[...]
    la_scr[...] = (jnp.minimum(z, 0.0) - jnp.log1p(jnp.exp(-jnp.abs(z)))) * (1.0 / GLA_GATE_NORM)
    tril = jnp.where(lax.broadcasted_iota(I32, (c_n, c_n), 0) >= lax.broadcasted_iota(I32, (c_n, c_n), 1), 1.0, 0.0)
    tril3 = jnp.concatenate([tril.astype(BF16)] * 3, axis=1)
    row_c = lax.broadcasted_iota(I32, (c_n, 1), 0)
    row_s = lax.broadcasted_iota(I32, (sub, 1), 0)
    lane_c = lax.broadcasted_iota(I32, (1, c_n), 1)
    nw = nw_ref[...]
    hk = [slice(h * dk, (h + 1) * dk) for h in range(nh)]
    hv = [slice(h * dv, (h + 1) * dv) for h in range(nh)]

    def chunk(c, carry):
        r0 = pl.multiple_of(c * c_n, c_n)
        rows = pl.ds(r0, c_n)
        qc = q_ref[0, rows, :] * (dk ** -0.5)
        kc = k_ref[0, rows, :]
        vc = [v_ref[0, rows, hv[h]].astype(BF16) for h in range(nh)]
        pieces, rest = [], la_scr[rows, :]
        for _ in range(3):
            pieces.append(rest.astype(BF16))
            rest = rest - pieces[-1].astype(F32)
        b = jnp.dot(tril3, jnp.concatenate(pieces, axis=0), preferred_element_type=F32)
        b_scr[...] = b
        st = [st_scr[h] for h in range(nh)]
        q_e = (qc * jnp.exp(b)).astype(BF16)
        o = [_nt_dot(q_e[:, hk[h]], st[h].astype(BF16)) for h in range(nh)]
        strips = [[] for _ in range(nh)]
        for blk in range(c_n // sub):
            lo = blk * sub
            q_i = qc[lo:lo + sub]
            b_i = b[lo:lo + sub]
            a = [jnp.zeros((sub, c_n), F32) for _ in range(nh)]
            if blk > 0:
                b_r = b_scr[lo - 1:lo, :]
                q_d = (q_i * jnp.exp(b_i - b_r)).astype(BF16)
                k_d = (kc * jnp.exp(jnp.where(row_c < lo, b_r - b, -jnp.inf))).astype(BF16)
                a = [_nt_dot(q_d[:, hk[h]], k_d[:, hk[h]]) for h in range(nh)]
            for j in range(sub):
                b_j = b_scr[lo + j:lo + j + 1, :]
                k_j = k_ref[0, pl.ds(r0 + lo + j, 1), :]
                prod = q_i * k_j * jnp.exp(jnp.where(row_s >= j, b_i - b_j, -jnp.inf))
                for h in range(nh):
                    col = jnp.sum(prod[:, hk[h]], axis=-1, keepdims=True)
                    a[h] = jnp.where(lane_c == lo + j, col, a[h]) if blk == 0 else (
                        a[h] + jnp.where(lane_c == lo + j, col, 0.0))
            for h in range(nh):
                strips[h].append(a[h])
        for h in range(nh):
            attn = jnp.concatenate(strips[h], axis=0)
            o[h] = o[h] + jnp.dot(attn.astype(BF16), vc[h], preferred_element_type=F32)
        b_last = b_scr[c_n - 1:c_n, :]
        k_dec = (kc * jnp.exp(b_last - b)).astype(BF16)
        decay = jnp.exp(b_last)
        for h in range(nh):
            st_scr[h] = st[h] * decay[:, hk[h]] + lax.dot_general(
                vc[h], k_dec[:, hk[h]], (((0,), (0,)), ((), ())), preferred_element_type=F32)
        for h in range(nh):
            og = og_ref[0, rows, hv[h]]
            on = o[h] * lax.rsqrt(jnp.mean(o[h] * o[h], axis=-1, keepdims=True) + EPS) * nw
            o_ref[0, rows, hv[h]] = on * (og * jax.nn.sigmoid(og))
        return carry

    lax.fori_loop(0, rows_n // c_n, chunk, 0)


def _gla(proj_gla, wg_pad, bg, nw):
    bsz, s, _ = proj_gla.shape
    nh, dk, dv = GLA_HEADS, GLA_DK, GLA_DV
    ts = GLA_TS
    wk, wv = nh * dk, nh * dv
    return pl.pallas_call(
        _gla_kernel,
        grid=(bsz, s // ts),
        in_specs=[pl.BlockSpec((1, ts, wk), lambda b, i: (b, i, GLA_Q_OFF // wk)),
                  pl.BlockSpec((1, ts, wk), lambda b, i: (b, i, GLA_K_OFF // wk)),
                  pl.BlockSpec((1, ts, wv), lambda b, i: (b, i, GLA_V_OFF // wv)),
                  pl.BlockSpec((1, ts, wv), lambda b, i: (b, i, GLA_OG_OFF // wv)),
                  pl.BlockSpec((1, ts, LANES), lambda b, i: (b, i, GLA_MISC_OFF // LANES)),
                  pl.BlockSpec((LANES, wk), lambda b, i: (0, 0)),
                  pl.BlockSpec((1, wk), lambda b, i: (0, 0)),
                  pl.BlockSpec((1, dv), lambda b, i: (0, 0))],
        out_specs=pl.BlockSpec((1, ts, wv), lambda b, i: (b, i, 0)),
        out_shape=jax.ShapeDtypeStruct((bsz, s, wv), F32),
        scratch_shapes=[pltpu.VMEM((nh, dv, dk), F32), pltpu.VMEM((ts, wk), F32), pltpu.VMEM((GLA_CHUNK, wk), F32)],
        compiler_params=_cparams(("parallel", "arbitrary")),
        name="gla",
    )(proj_gla, proj_gla, proj_gla, proj_gla, proj_gla, wg_pad, bg, nw)


def _outproj_kernel(nsa_ref, gla_ref, x_ref, wo_ref, g1_ref, sc_ref, sh_ref, nw_ref, wr_ref, br_ref,
                    x1_ref, h_ref, route_ref):
    half = nsa_ref.shape[2]
    acc = jnp.dot(nsa_ref[0].astype(BF16), wo_ref[:half, :], preferred_element_type=F32)
    acc = acc + jnp.dot(gla_ref[0].astype(BF16), wo_ref[half:, :], preferred_element_type=F32)
    x1 = x_ref[0] + g1_ref[0, 0] * acc
    x1_ref[0] = x1
    h = _modulated_norm(x1, nw_ref[...], sc_ref[0, 0], sh_ref[0, 0])
    h_ref[0] = h
    h_hi = h.astype(BF16)
    h_lo = (h - h_hi.astype(F32)).astype(BF16)
    t = jnp.dot(h_hi, wr_ref[...], preferred_element_type=F32)
    logits = (t[:, :LANES] + t[:, LANES:] + jnp.dot(h_lo, wr_ref[:, :LANES], preferred_element_type=F32)
              + br_ref[...])
    lane = lax.broadcasted_iota(I32, (1, LANES), 1)
    ninf = -jnp.inf
    is_g = (lane >= N_EXPERTS) & (lane < N_EXPERTS + N_GROUPS)
    gl = jnp.where(is_g, logits, ninf)
    ge = jnp.exp(gl - jnp.max(gl, axis=-1, keepdims=True))
    gp = ge / jnp.sum(ge, axis=-1, keepdims=True)
    gp_max = jnp.max(gp, axis=-1, keepdims=True)
    grp = jnp.min(jnp.where((gp == gp_max) & is_g, lane - N_EXPERTS, LANES), axis=-1, keepdims=True)
    in_grp = (lane // EXPERTS_PER_GROUP == grp) & (lane < N_EXPERTS)
    el = jnp.where(in_grp, logits, ninf)
    v1 = jnp.max(el, axis=-1, keepdims=True)
    i1 = jnp.min(jnp.where(el == v1, lane, LANES), axis=-1, keepdims=True)
    el2 = jnp.where(lane == i1, ninf, el)
    v2 = jnp.max(el2, axis=-1, keepdims=True)
    i2 = jnp.min(jnp.where(el2 == v2, lane, LANES), axis=-1, keepdims=True)
    e2 = jnp.exp(v2 - v1)
    den = 1.0 + e2
    w1 = gp_max * (1.0 / den)
    w2 = gp_max * (e2 / den)
    route_ref[0] = jnp.where(lane == 0, i1.astype(F32), jnp.where(lane == 1, i2.astype(F32), jnp.where(
        lane == 2, w1, jnp.where(lane == 3, w2, 0.0))))


def _outproj(o_nsa, o_gla, x, wo, mod4, nw, wr, br):
    bsz, s, d = x.shape
    tm = OUT_TM
    half = o_nsa.shape[2]

    def mod_spec(idx):
        return pl.BlockSpec((1, 1, 1, d), lambda b, i: (b, idx, 0, 0))

    row = lambda w: pl.BlockSpec((1, tm, w), lambda b, i: (b, i, 0))
    return pl.pallas_call(
        _outproj_kernel,
        grid=(bsz, s // tm),
        in_specs=[row(half), row(half), row(d),
                  pl.BlockSpec((2 * half, d), lambda b, i: (0, 0), pipeline_mode=pl.Buffered(1)),
                  mod_spec(2), mod_spec(4), mod_spec(3),
                  pl.BlockSpec((1, d), lambda b, i: (0, 0)),
                  pl.BlockSpec((d, 2 * LANES), lambda b, i: (0, 0)),
                  pl.BlockSpec((1, LANES), lambda b, i: (0, 0))],
        out_specs=[row(d), row(d), row(LANES)],
        out_shape=[jax.ShapeDtypeStruct((bsz, s, d), F32), jax.ShapeDtypeStruct((bsz, s, d), F32),
                   jax.ShapeDtypeStruct((bsz, s, LANES), F32)],
        compiler_params=_cparams(("parallel", "parallel")),
        name="outproj_router",
    )(o_nsa, o_gla, x, wo, mod4, mod4, mod4, nw, wr, br)


def _rank_kernel(route_ref, dest_ref, meta_ref, rank_scr):
    n = route_ref.shape[0]
    tm = RANK_TM
    lane_i = lax.broadcasted_iota(I32, (1, LANES), 1)
    lane = lane_i.astype(F32)
    strict = jnp.where(lax.broadcasted_iota(I32, (tm, tm), 0) > lax.broadcasted_iota(I32, (tm, tm), 1),
                       1.0, 0.0).astype(BF16)

    def two_lanes(a, b):
        return jnp.where(lane_i == 0, a, jnp.where(lane_i == 1, b, 0.0))

    def pick(e, table):
        return jnp.sum(jnp.where(lane == e, table, 0.0), axis=-1, keepdims=True)

    def count(i, seen):
        r0 = pl.multiple_of(i * tm, tm)
        rt = route_ref[pl.ds(r0, tm), :]
        e1, e2 = rt[:, 0:1], rt[:, 1:2]
        member = jnp.where(lane == e1, 1.0, jnp.where(lane == e2, 1.0, 0.0))
        before = jnp.dot(strict, member.astype(BF16), preferred_element_type=F32) + seen
        rank_scr[pl.ds(r0, tm), :] = two_lanes(pick(e1, before), pick(e2, before))
        return seen + jnp.sum(member, axis=0, keepdims=True)

    counts = lax.fori_loop(0, n // tm, count, jnp.zeros((1, LANES), F32))
    ntile = jnp.floor((counts + (MOE_TB - 1)) * (1.0 / MOE_TB))
    incl = jnp.where(lax.broadcasted_iota(I32, (LANES, LANES), 0) <= lax.broadcasted_iota(I32, (LANES, LANES), 1),
                     1.0, 0.0).astype(BF16)
    tile_end = jnp.dot(jnp.broadcast_to(ntile, (8, LANES)).astype(BF16), incl,
                       preferred_element_type=F32)[0:1]
    row_start = (tile_end - ntile) * MOE_TB

    def place(i, carry):
        r0 = pl.multiple_of(i * tm, tm)
        rt = route_ref[pl.ds(r0, tm), :]
        rk = rank_scr[pl.ds(r0, tm), :]
        d1 = pick(rt[:, 0:1], row_start) + rk[:, 0:1]
        d2 = pick(rt[:, 1:2], row_start) + rk[:, 1:2]
        dest_ref[pl.ds(r0, tm), :] = two_lanes(d1, d2).astype(I32)
        return carry

    lax.fori_loop(0, n // tm, place, 0)
    trow = lax.broadcasted_iota(I32, (meta_ref.shape[0], 1), 0).astype(F32)
    texp = jnp.sum(jnp.where((tile_end <= trow) & (lane_i < N_EXPERTS), 1.0, 0.0), axis=-1, keepdims=True)
    texp = jnp.minimum(texp, N_EXPERTS - 1.0)
    used = pick(N_EXPERTS - 1.0, tile_end)
    diag = lax.broadcasted_iota(I32, (meta_ref.shape[0], LANES), 0) == lane_i
    end_rows = jnp.sum(jnp.where(diag, tile_end, 0.0), axis=-1, keepdims=True)
    ntile_rows = jnp.sum(jnp.where(diag, ntile, 0.0), axis=-1, keepdims=True)
    meta_ref[...] = jnp.where(lane_i == 2, end_rows, jnp.where(lane_i == 3, ntile_rows, two_lanes(
        texp, jnp.broadcast_to(used, texp.shape)))).astype(I32)


def _rank(route):
    n = route.shape[0]
    return pl.pallas_call(
        _rank_kernel,
        out_shape=[jax.ShapeDtypeStruct((n, LANES), I32), jax.ShapeDtypeStruct((LANES, LANES), I32)],
        scratch_shapes=[pltpu.VMEM((n, LANES), F32)],
        compiler_params=pltpu.CompilerParams(vmem_limit_bytes=VMEM_LIMIT),
        name="moe_rank",
    )(route)


def _dispatch_kernel(dest_ref, ends_ref, h_ref, xs_ref, zero_scr, sem, zsem):
    i = pl.program_id(0)
    tm = h_ref.shape[0]
    tb = zero_scr.shape[0]

    @pl.when(i == 0)
    def _():
        zero_scr[...] = jnp.zeros(zero_scr.shape, F32)

        def zero_copy(e):
            r0 = pl.multiple_of((ends_ref[e] - 1) * tb, tb)
            return pltpu.make_async_copy(zero_scr, xs_ref.at[pl.ds(r0, tb)], zsem)

        def start(e, carry):
            @pl.when(ends_ref[N_EXPERTS + e] > 0)
            def _():
                zero_copy(e).start()
            return carry

        def wait(e, carry):
            @pl.when(ends_ref[N_EXPERTS + e] > 0)
            def _():
                zero_copy(e).wait()
            return carry

        def tail_copy(t):
            return pltpu.make_async_copy(zero_scr, xs_ref.at[pl.ds(pl.multiple_of(t * tb, tb), tb)], zsem)

        def tail_start(t, carry):
            tail_copy(t).start()
            return carry

        def tail_wait(t, carry):
            tail_copy(t).wait()
            return carry

        used = ends_ref[2 * N_EXPERTS]
        lax.fori_loop(0, N_EXPERTS, start, 0)
        lax.fori_loop(used, xs_ref.shape[0] // tb, tail_start, 0)
        lax.fori_loop(0, N_EXPERTS, wait, 0)
        lax.fori_loop(used, xs_ref.shape[0] // tb, tail_wait, 0)

    def issue(r, carry):
        p = (i * tm + r) * TOP_K
        for k in range(TOP_K):
            pltpu.make_async_copy(h_ref.at[pl.ds(r, 1)], xs_ref.at[pl.ds(dest_ref[p + k], 1)], sem).start()
        return carry

    lax.fori_loop(0, tm, issue, 0, unroll=DMA_UNROLL)
    for k in range(TOP_K):
        pltpu.make_async_copy(h_ref, xs_ref.at[pl.ds(0, tm)], sem).wait()


def _dispatch(dest_flat, ends_flat, h, cap):
    n, d = h.shape
    tm = DISPATCH_TM
    return pl.pallas_call(
        _dispatch_kernel,
        grid_spec=pltpu.PrefetchScalarGridSpec(
            num_scalar_prefetch=2, grid=(n // tm,),
            in_specs=[pl.BlockSpec((tm, d), lambda i, dst, ends: (i, 0))],
            out_specs=pl.BlockSpec(memory_space=pl.ANY),
            scratch_shapes=[pltpu.VMEM((MOE_TB, d), F32), pltpu.SemaphoreType.DMA(()), pltpu.SemaphoreType.DMA(())]),
        out_shape=jax.ShapeDtypeStruct((cap, d), F32),
        compiler_params=_cparams(("arbitrary",)),
        name="moe_dispatch",
    )(dest_flat, ends_flat, h)


def _ffn_kernel(meta_ref, x_ref, wg_hbm, wu_hbm, wd_hbm, y_ref, wg_st, wu_st, wd_st, wgb, wub, wdb, wsem):
    t = pl.program_id(0)
    ntile = pl.num_programs(0)
    n_used = meta_ref[ntile]
    e = meta_ref[t]
    e_prev = meta_ref[jnp.maximum(t - 1, 0)]
    active = t < n_used

    def fetch(expert):
        return [pltpu.make_async_copy(src.at[expert], dst, wsem)
                for src, dst in ((wg_hbm, wg_st), (wu_hbm, wu_st), (wd_hbm, wd_st))]

    @pl.when(t == 0)
    def _():
        for cp in fetch(e):
            cp.start()

    @pl.when(active & ((t == 0) | (e != e_prev)))
    def _():
        for cp in fetch(e):
            cp.wait()
        wgb[...] = wg_st[...].astype(BF16)
        wub[...] = wu_st[...].astype(BF16)
        wdb[...] = wd_st[...].astype(BF16)
        end = meta_ref[ntile + 1 + e]

        @pl.when(end < n_used)
        def _():
            for cp in fetch(meta_ref[end]):
                cp.start()

    @pl.when(active)
    def _():
        x = x_ref[...].astype(BF16)
        gate = jnp.dot(x, wgb[...], preferred_element_type=F32)
        up = jnp.dot(x, wub[...], preferred_element_type=F32)
        act = (gate * jax.nn.sigmoid(gate)) * up
        y_ref[...] = jnp.dot(act.astype(BF16), wdb[...], preferred_element_type=F32)

    @pl.when(jnp.logical_not(active))
    def _():
        y_ref[...] = jnp.zeros(y_ref.shape, F32)


def _ffn(meta_flat, xs, wg, wu, wd):
    cap, d = xs.shape
    ff = wg.shape[2]
    tb = MOE_TB
    ntile = cap // tb
    return pl.pallas_call(
        _ffn_kernel,
        grid_spec=pltpu.PrefetchScalarGridSpec(
            num_scalar_prefetch=1, grid=(ntile,),
            in_specs=[pl.BlockSpec((tb, d), lambda t, m: (jnp.minimum(t, m[ntile] - 1), 0)),
                      pl.BlockSpec(memory_space=pl.ANY), pl.BlockSpec(memory_space=pl.ANY),
                      pl.BlockSpec(memory_space=pl.ANY)],
            out_specs=pl.BlockSpec((tb, d), lambda t, m: (t, 0)),
            scratch_shapes=[pltpu.VMEM((d, ff), F32), pltpu.VMEM((d, ff), F32), pltpu.VMEM((ff, d), F32),
                            pltpu.VMEM((d, ff), BF16), pltpu.VMEM((d, ff), BF16), pltpu.VMEM((ff, d), BF16),
                            pltpu.SemaphoreType.DMA(())]),
        out_shape=jax.ShapeDtypeStruct((cap, d), F32),
        compiler_params=_cparams(("arbitrary",)),
        name="moe_ffn",
    )(meta_flat, xs, wg, wu, wd)


def _combine_kernel(dest_ref, y_ref, x1_ref, route_ref, g2_ref, nf_ref, o_ref, ybuf, sem):
    i = pl.program_id(0)
    nstep = pl.num_programs(0)
    tm = x1_ref.shape[0]

    def issue(tile, slot):
        def body(r, carry):
            p = (tile * tm + r) * TOP_K
            for k in range(TOP_K):
                pltpu.make_async_copy(y_ref.at[pl.ds(dest_ref[p + k], 1)], ybuf.at[slot, k, pl.ds(r, 1)],
                                      sem.at[slot]).start()
            return carry

        lax.fori_loop(0, tm, body, 0, unroll=DMA_UNROLL)

    @pl.when(i == 0)
    def _():
        issue(0, 0)

    @pl.when(i + 1 < nstep)
    def _():
        issue(i + 1, (i + 1) % 2)

    slot = i % 2
    for k in range(TOP_K):
        pltpu.make_async_copy(y_ref.at[pl.ds(0, tm)], ybuf.at[slot, k], sem.at[slot]).wait()
    rt = route_ref[...]
    moe = rt[:, 2:3] * ybuf[slot, 0] + rt[:, 3:4] * ybuf[slot, 1]
    xo = x1_ref[...] + g2_ref[0, 0] * moe
    o_ref[...] = xo * lax.rsqrt(jnp.mean(xo * xo, axis=-1, keepdims=True) + EPS) * nf_ref[...]


def _combine(dest_flat, y, x1, route, mod4, nf, seq):
    n, d = x1.shape
    tm = COMB_TM
    tiles_per_seq = seq // tm
    return pl.pallas_call(
        _combine_kernel,
        grid_spec=pltpu.PrefetchScalarGridSpec(
            num_scalar_prefetch=1, grid=(n // tm,),
            in_specs=[pl.BlockSpec(memory_space=pl.ANY),
                      pl.BlockSpec((tm, d), lambda i, dst: (i, 0)),
                      pl.BlockSpec((tm, LANES), lambda i, dst: (i, 0)),
                      pl.BlockSpec((1, 1, 1, d), lambda i, dst: (i // tiles_per_seq, 5, 0, 0)),
                      pl.BlockSpec((1, d), lambda i, dst: (0, 0))],
            out_specs=pl.BlockSpec((tm, d), lambda i, dst: (i, 0)),
            scratch_shapes=[pltpu.VMEM((2, TOP_K, tm, d), F32), pltpu.SemaphoreType.DMA((2,))]),
        out_shape=jax.ShapeDtypeStruct((n, d), F32),
        compiler_params=_cparams(("arbitrary",)),
        name="moe_combine",
    )(dest_flat, y, x1, route, mod4, nf)


def _alibi_slopes():
    n = NSA_HEADS
    full = jnp.asarray(2.0 ** (-8.0 * np.arange(1, n + 1) / n), dtype=F32)
    pieces, rest = [], full * LOG2E
    for _ in range(3):
        piece = rest.astype(BF16).astype(F32)
        pieces.append(piece)
        rest = rest - piece
    return jnp.concatenate([full] + pieces)


def _layer(x, c, w_ada, b_ada, norm1_w, w_in, cmp_pos_k, cmp_w1_k, cmp_w2_k, cmp_pos_v, cmp_w1_v, cmp_w2_v,
           gla_w_gate2, gla_b_gate, gla_norm_w, w_out, norm2_w, w_rg, b_rg, w_re, b_re, w_eg, w_eu, w_ed):
    bsz, s, d = x.shape
    dh = NSA_HEAD_DIM
    mod4 = _adaln(c, w_ada, b_ada).reshape(bsz, 6, 1, d)

    o_gate = NSA_COLS
    o_gla = o_gate + NSA_GATE_COLS
    o_lr = o_gla + 2 * GLA_HEADS * GLA_DK + 2 * GLA_HEADS * GLA_DV
    w_nsa, w_gla = _prep_w_in(w_in.T, o_gate, o_gla, o_lr)
    nw1 = norm1_w.reshape(1, d)
    proj_nsa, proj_gla = _inproj(x, mod4, nw1, w_nsa, w_gla, sc_idx=1, sh_idx=0)

    pos = jnp.stack([cmp_pos_k, cmp_pos_v]).reshape(2, 2, CMP_STRIDE * dh)
    kvc = _compress(proj_nsa, pos, jnp.stack([cmp_w1_k, cmp_w1_v]), jnp.stack([cmp_w2_k, cmp_w2_v]))
    o_nsa = _nsa(_alibi_slopes(), proj_nsa, kvc, proj_gla)

    wg_pad = jnp.zeros((LANES, GLA_HEADS * GLA_DK), F32).at[
        NSA_GATE_COLS:NSA_GATE_COLS + GLA_GATE_RANK].set(gla_w_gate2)
    o_gla_out = _gla(proj_gla, wg_pad, gla_b_gate.reshape(1, -1), gla_norm_w.reshape(1, -1))

    wr = jnp.concatenate([w_re, w_rg, jnp.zeros((d, LANES - N_EXPERTS - N_GROUPS), F32)], axis=1)
    br = jnp.concatenate([b_re, b_rg, jnp.zeros((LANES - N_EXPERTS - N_GROUPS,), F32)]).reshape(1, LANES)
    wr_hi = wr.astype(BF16)
    wr_cat = jnp.concatenate([wr_hi, (wr - wr_hi.astype(F32)).astype(BF16)], axis=1)
    x1, h2, route = _outproj(o_nsa, o_gla_out, x, w_out.astype(BF16), mod4, norm2_w.reshape(1, d), wr_cat, br)

    n = bsz * s
    npair = n * TOP_K
    cap = npair + N_EXPERTS * MOE_TB
    ntile = cap // MOE_TB
    route2 = route.reshape(n, LANES)
    dest, meta = _rank(route2)
    dest_flat = dest[:, :TOP_K].reshape(npair)
    meta_flat = jnp.concatenate([meta[:ntile, 0], meta[:1, 1], meta[:N_EXPERTS, 2]])
    ends_flat = jnp.concatenate([meta[:N_EXPERTS, 2], meta[:N_EXPERTS, 3], meta[:1, 1]])
    xs = _dispatch(dest_flat, ends_flat, h2.reshape(n, d), cap)
    y = _ffn(meta_flat, xs, w_eg, w_eu, w_ed)
    return x1.reshape(n, d), y, dest_flat, route2, mod4


def kernel(x, c, w_ada, b_ada, norm1_w, w_in, cmp_pos_k, cmp_w1_k, cmp_w2_k, cmp_pos_v, cmp_w1_v, cmp_w2_v,
           gla_w_gate2, gla_b_gate, gla_norm_w, w_out, norm2_w, w_router_group, b_router_group, w_router_expert,
           b_router_expert, w_expert_gate, w_expert_up, w_expert_down, norm_f_w):
    bsz, s, d = x.shape
    assert w_ada.shape[0] == 1, "single layer"
    x1, y, dest_flat, route2, mod4 = _layer(
        x, c, w_ada[0], b_ada[0], norm1_w[0], w_in[0], cmp_pos_k[0], cmp_w1_k[0], cmp_w2_k[0], cmp_pos_v[0],
        cmp_w1_v[0], cmp_w2_v[0], gla_w_gate2[0], gla_b_gate[0], gla_norm_w[0], w_out[0], norm2_w[0],
        w_router_group[0], b_router_group[0], w_router_expert[0], b_router_expert[0],
        w_expert_gate[0], w_expert_up[0], w_expert_down[0])
    out = _combine(dest_flat, y, x1, route2, mod4, norm_f_w.reshape(1, d), s)
    return out.reshape(bsz, s, d)
```

```python
import functools

import numpy as np
import jax
import jax.numpy as jnp
from jax import lax
from jax.experimental import pallas as pl
from jax.experimental.pallas import tpu as pltpu

F32 = jnp.float32
BF16 = jnp.bfloat16
I32 = jnp.int32

D_MODEL = 2048
NSA_HEAD_DIM = 64
NSA_HEADS = 16
NSA_KV_HEADS = 4
NSA_Q_PER_KV = 4
CMP_BLOCK = 32
CMP_STRIDE = 16
SEL_BLOCK = 64
N_SEL = 16
WINDOW = 512
N_BRANCH = 3
GLA_HEADS = 4
GLA_DV = 256
GLA_DK = 128
GLA_GATE_RANK = 16
GLA_GATE_NORM = 16.0
GLA_CHUNK = 64
GLA_SUB = 8
N_GROUPS = 4
EXPERTS_PER_GROUP = 8
N_EXPERTS = 32
TOP_K = 2
EXPERT_FF = 512
EPS = 1e-6
NEG = -1e30
FORCE = 1e30
LOG2E = 1.4426950408889634
MASKED = 2.0 ** 100

NSA_Q_COLS = NSA_HEADS * NSA_HEAD_DIM
NSA_KV_COLS = 2 * N_BRANCH * NSA_KV_HEADS * NSA_HEAD_DIM
NSA_GATE_COLS = N_BRANCH * NSA_HEADS
NSA_COLS = NSA_Q_COLS + NSA_KV_COLS
NSA_SLOTS = NSA_COLS // NSA_HEAD_DIM
GLA_Q_OFF = 0
GLA_K_OFF = GLA_HEADS * GLA_DK
GLA_V_OFF = 2 * GLA_HEADS * GLA_DK
GLA_OG_OFF = GLA_V_OFF + GLA_HEADS * GLA_DV
GLA_MISC_OFF = GLA_OG_OFF + GLA_HEADS * GLA_DV
LANES = 128
GLA_COLS = GLA_MISC_OFF + LANES

VMEM_LIMIT = 56 * 1024 * 1024

ADA_TN = 768
PREP_TR = 256
PREP_STEP = 512
INPROJ_TM = 256
INPROJ_TN = 512
NSA_TQ = 256
NSA_TK = 256
GLA_TS = 512
OUT_TM = 512
RANK_TM = 256
MOE_TB = 256
DISPATCH_TM = 1024
COMB_TM = 512
DMA_UNROLL = 8


def _cparams(sem):
    return pltpu.CompilerParams(dimension_semantics=sem, vmem_limit_bytes=VMEM_LIMIT)


def _adaln_kernel(c_ref, w_ref, b_ref, o_ref):
    c = c_ref[...]
    s = (c * jax.nn.sigmoid(c)).astype(BF16)
    o_ref[...] = jnp.dot(s, w_ref[...].astype(BF16), preferred_element_type=F32) + b_ref[...]


def _adaln(c, w, b):
    nb, d = c.shape
    n = w.shape[1]
    return pl.pallas_call(
        _adaln_kernel,
        grid=(n // ADA_TN,),
        in_specs=[pl.BlockSpec((nb, d), lambda j: (0, 0)),
                  pl.BlockSpec((d, ADA_TN), lambda j: (0, j)),
                  pl.BlockSpec((1, ADA_TN), lambda j: (0, j))],
        out_specs=pl.BlockSpec((nb, ADA_TN), lambda j: (0, j)),
        out_shape=jax.ShapeDtypeStruct((nb, n), F32),
        compiler_params=_cparams(("parallel",)),
        name="adaln",
    )(c, w, b.reshape(1, n))


def _modulated_norm(x, nw, sc, sh):
    ms = jnp.mean(x * x, axis=-1, keepdims=True)
    h = x * lax.rsqrt(ms + EPS) * nw
    return h * (1.0 + sc) + sh


def _prep_w_in_kernel(wt_ref, wn_ref, wg_ref, *, o_gate, o_gla, o_lr):
    tr = wt_ref.shape[1]
    step = PREP_STEP

    def put(dst, c0, rows):
        dst[:, c0:c0 + rows.shape[0]] = rows.T.astype(BF16)

    for c in range(0, o_gate, step):
        put(wn_ref, c, wt_ref[c:c + step, :])
    for c in range(0, o_lr - o_gla, step):
        put(wg_ref, c, wt_ref[o_gla + c:o_gla + c + step, :])
    pad = wg_ref.shape[1] - (wt_ref.shape[0] - o_gate)
    misc = jnp.concatenate([wt_ref[o_gate:o_gla, :], wt_ref[o_lr:, :], jnp.zeros((pad, tr), F32)], axis=0)
    put(wg_ref, o_lr - o_gla, misc)


def _prep_w_in(w_in_t, o_gate, o_gla, o_lr):
    n, d = w_in_t.shape
    tr = PREP_TR
    assert o_gate % PREP_STEP == 0 and (o_lr - o_gla) % PREP_STEP == 0
    return pl.pallas_call(
        functools.partial(_prep_w_in_kernel, o_gate=o_gate, o_gla=o_gla, o_lr=o_lr),
        grid=(d // tr,),
        in_specs=[pl.BlockSpec((n, tr), lambda i: (0, i))],
        out_specs=[pl.BlockSpec((tr, NSA_COLS), lambda i: (i, 0)), pl.BlockSpec((tr, GLA_COLS), lambda i: (i, 0))],
        out_shape=[jax.ShapeDtypeStruct((d, NSA_COLS), BF16), jax.ShapeDtypeStruct((d, GLA_COLS), BF16)],
        compiler_params=_cparams(("parallel",)),
        name="prep_w_in",
    )(w_in_t)


def _inproj_kernel(x_ref, sc_ref, sh_ref, nw_ref, wn_ref, wg_ref, on_ref, og_ref):
    h = _modulated_norm(x_ref[0], nw_ref[...], sc_ref[0, 0], sh_ref[0, 0]).astype(BF16)
    dh = NSA_HEAD_DIM
    tn = INPROJ_TN
    for c in range(wn_ref.shape[1] // tn):
        acc = jnp.dot(h, wn_ref[:, c * tn:(c + 1) * tn], preferred_element_type=F32)
        for u in range(tn // dh):
            on_ref[0, c * (tn // dh) + u] = acc[:, u * dh:(u + 1) * dh]
    og_ref[0] = jnp.dot(h, wg_ref[...], preferred_element_type=F32)


def _inproj(x, mod4, nw, w_nsa, w_gla, *, sc_idx, sh_idx):
    bsz, s, d = x.shape
    n_nsa, n_gla = w_nsa.shape[1], w_gla.shape[1]
    tm = INPROJ_TM
    dh = NSA_HEAD_DIM
    return pl.pallas_call(
        _inproj_kernel,
        grid=(bsz, s // tm),
        in_specs=[pl.BlockSpec((1, tm, d), lambda b, i: (b, i, 0)),
                  pl.BlockSpec((1, 1, 1, d), lambda b, i: (b, sc_idx, 0, 0)),
                  pl.BlockSpec((1, 1, 1, d), lambda b, i: (b, sh_idx, 0, 0)),
                  pl.BlockSpec((1, d), lambda b, i: (0, 0)),
                  pl.BlockSpec((d, n_nsa), lambda b, i: (0, 0), pipeline_mode=pl.Buffered(1)),
                  pl.BlockSpec((d, n_gla), lambda b, i: (0, 0), pipeline_mode=pl.Buffered(1))],
        out_specs=[pl.BlockSpec((1, n_nsa // dh, tm, dh), lambda b, i: (b, 0, i, 0)),
                   pl.BlockSpec((1, tm, n_gla), lambda b, i: (b, i, 0))],
        out_shape=[jax.ShapeDtypeStruct((bsz, n_nsa // dh, s, dh), F32),
                   jax.ShapeDtypeStruct((bsz, s, n_gla), F32)],
        compiler_params=_cparams(("parallel", "parallel")),
        name="inproj",
    )(x, mod4, mod4, nw, w_nsa, w_gla)


def _hi_lo(x):
    hi = x.astype(BF16)
    return hi, (x - hi.astype(F32)).astype(BF16)


def _dot_hi_lo(x, w):
    x_hi, x_lo = _hi_lo(x)
    w_hi, w_lo = _hi_lo(w)
    return jnp.dot(jnp.concatenate([x_hi, x_lo, x_hi], axis=1), jnp.concatenate([w_hi, w_hi, w_lo], axis=0),
                   preferred_element_type=F32)


def _compress_kernel(a_ref, pos_ref, w1_ref, w2_ref, o_ref):
    ng = a_ref.shape[1]
    nch = a_ref.shape[2] // CMP_STRIDE
    a = jnp.concatenate([
        jnp.concatenate([a_ref[0, g, pl.ds(t, nch, stride=CMP_STRIDE), :] for t in range(CMP_STRIDE)], axis=1)
        for g in range(ng)], axis=0)
    pos = pos_ref[0]
    half = a.shape[1]
    y1 = _dot_hi_lo(a + pos[0:1], w1_ref[0, :half, :])
    y2 = _dot_hi_lo(a + pos[1:2], w1_ref[0, half:, :])
    h = y1 + pltpu.roll(y2, a.shape[0] - 1, axis=0)
    out = _dot_hi_lo(jax.nn.gelu(h), w2_ref[0])
    row = lax.broadcasted_iota(I32, (nch, out.shape[1]), 0)
    for g in range(ng):
        o_ref[0, g] = jnp.where(row < nch - 1, out[g * nch:(g + 1) * nch], 0.0)


def _compress(proj_nsa, pos, w1, w2):
    bsz, _, s, dh = proj_nsa.shape
    ng = NSA_KV_HEADS
    nch = s // CMP_STRIDE
    cw = CMP_STRIDE * dh
    return pl.pallas_call(
        _compress_kernel,
        grid=(bsz, 2),
        in_specs=[pl.BlockSpec((1, ng, s, dh), lambda b, t: (b, NSA_HEADS // ng + t, 0, 0)),
                  pl.BlockSpec((1, 2, cw), lambda b, t: (t, 0, 0)),
                  pl.BlockSpec((1, 2 * cw, dh), lambda b, t: (t, 0, 0)),
                  pl.BlockSpec((1, dh, dh), lambda b, t: (t, 0, 0))],
        out_specs=pl.BlockSpec((1, ng, nch, dh), lambda b, t: (b, t, 0, 0)),
        out_shape=jax.ShapeDtypeStruct((bsz, 2 * ng, nch, dh), F32),
        compiler_params=_cparams(("parallel", "parallel")),
        name="nsa_compress",
    )(proj_nsa, pos, w1, w2)


def _nt_dot(a, b, **kw):
    return lax.dot_general(a, b, (((1,), (1,)), ((), ())), preferred_element_type=F32, **kw)


def _nsa_kernel(slopes_ref, q_ref, kc_ref, vc_ref, ks_ref, vs_ref, kw_ref, vw_ref, gate_ref, o_ref,
                ksb, vst, kwb, vwt, q4_scr, notsel_scr, *scr):
    g = pl.program_id(1)
    qi = pl.program_id(2)
    tq_n = q_ref.shape[2]
    dh = NSA_HEAD_DIM
    nr = NSA_Q_PER_KV
    seq = ks_ref.shape[2]
    tk_n = NSA_TK
    nb = seq // SEL_BLOCK

    @pl.when(qi == 0)
    def _():
        row = lax.broadcasted_iota(I32, (seq, dh), 0)
        lane = lax.broadcasted_iota(I32, (seq, dh), 1)
        blk = row // SEL_BLOCK
        pos = jnp.where((lane >= nb) & (lane < nb + 3), (blk * SEL_BLOCK).astype(F32),
                        jnp.where((lane >= nb + 3) & (lane < nb + 6), (row % SEL_BLOCK).astype(F32), 0.0))
        ksb[...] = jnp.concatenate([jnp.where(lane == blk, -MASKED, pos), ks_ref[0, 0]], axis=1).astype(BF16)
        kwb[...] = jnp.concatenate([pos, kw_ref[0, 0]], axis=1).astype(BF16)
        for c in range(seq // tk_n):
            rows = slice(c * tk_n, (c + 1) * tk_n)
            for src, dst in ((vs_ref, vst), (vw_ref, vwt)):
                v = src[0, 0, rows, :]
                dst[c] = jnp.concatenate([v, v], axis=1).T[:dh].astype(BF16)

    t0 = qi * tq_n
    tq = t0 + lax.broadcasted_iota(I32, (1, tq_n), 1)
    slopes = [slopes_ref[g * nr + r] for r in range(nr)]
    scale = dh ** -0.5
    q_t = []
    for pair in range(nr // 2):
        both = jnp.concatenate([q_ref[0, 2 * pair], q_ref[0, 2 * pair + 1]], axis=1).T * scale
        q_t += [both[:dh], both[dh:]]

    ncp = kc_ref.shape[2]
    kc = kc_ref[0, 0]
    vc = vc_ref[0, 0]
    vc_t = jnp.concatenate([vc, vc], axis=1).T[:dh].astype(BF16)
    n_sub = lax.broadcasted_iota(I32, (ncp, 1), 0)
    blk_end = n_sub * CMP_STRIDE + (CMP_BLOCK - 1)
    center = n_sub.astype(F32) * CMP_STRIDE + (CMP_BLOCK - 1) / 2.0
    kc_hi, kc_lo = _hi_lo(kc)
    kc_cat = jnp.concatenate([kc_hi, kc_lo, kc_hi, jnp.zeros_like(kc_hi)], axis=1)
    q_hi, q_lo = _hi_lo(jnp.concatenate(q_t, axis=1))
    q_cat = jnp.concatenate([q_hi, q_hi, q_lo, jnp.zeros_like(q_hi)], axis=0)
    s = jnp.dot(kc_cat, q_cat, preferred_element_type=F32)
    tq_all = jnp.concatenate([tq] * nr, axis=1)
    slope_all = jnp.concatenate([jnp.full((1, tq_n), slopes[r], F32) for r in range(nr)], axis=1)
    valid_all = blk_end <= tq_all
    s = jnp.where(valid_all, s - slope_all * (tq_all.astype(F32) - center), NEG)
    e = jnp.exp(s - jnp.max(s, axis=0, keepdims=True))
    p = jnp.where(valid_all, e / jnp.sum(e, axis=0, keepdims=True), 0.0)
    psum = sum(p[:, r * tq_n:(r + 1) * tq_n] for r in range(nr))
    o_c_all = jnp.dot(vc_t, p.astype(BF16), preferred_element_type=F32)
    o_c = [o_c_all[:, r * tq_n:(r + 1) * tq_n] for r in range(nr)]

    n_sel = min(N_SEL, nb)
    notsel_scr[...] = jnp.zeros(notsel_scr.shape, F32)

    @pl.when((qi + 1) * tq_n > n_sel * SEL_BLOCK)
    def _():
        rowj = lax.broadcasted_iota(I32, (LANES, ncp), 0) * SEL_BLOCK
        coln = lax.broadcasted_iota(I32, (LANES, ncp), 1) * CMP_STRIDE
        overlap = jnp.where((coln < rowj + SEL_BLOCK) & (coln + CMP_BLOCK > rowj)
                            & (coln < (ncp - 1) * CMP_STRIDE) & (rowj < nb * SEL_BLOCK), 1.0, 0.0)
        ov = overlap.astype(BF16)
        imp = jnp.dot(jnp.concatenate([ov, ov], axis=1), jnp.concatenate(_hi_lo(psum), axis=0),
                      preferred_element_type=F32)[:nb]
        j_sub = lax.broadcasted_iota(I32, (nb, 1), 0)
        qblk = tq // SEL_BLOCK
        forced = (j_sub == 0) | (j_sub == qblk) | (j_sub == qblk - 1)
        imp = jnp.where(forced, FORCE, jnp.where(j_sub <= qblk, imp, NEG))
        cnt = jnp.zeros((nb, tq_n), F32)
        for i in range(nb):
            ci = imp[i:i + 1, :]
            tie = jnp.where(j_sub > i, 1.0, 0.0)
            cnt = cnt + jnp.where(ci > imp, 1.0, jnp.where(ci == imp, tie, 0.0))
        notsel_scr[...] = jnp.where(cnt < float(n_sel), 0.0, 1.0)

    notsel = notsel_scr[...]

    sub_h = lax.broadcasted_iota(I32, (dh - nb, 1), 0)
    for r in range(nr):
        scol = jnp.zeros((dh - nb, 1), F32)
        for i in range(3):
            piece = slopes_ref[(i + 1) * NSA_HEADS + g * nr + r]
            scol = jnp.where((sub_h == i) | (sub_h == 3 + i), piece, scol)
        q4_scr[:, r * tq_n:(r + 1) * tq_n] = jnp.concatenate(
            [notsel, jnp.broadcast_to(scol, (dh - nb, tq_n)), q_t[r] * LOG2E], axis=0).astype(BF16)

    nwt = WINDOW // tk_n
    nbuf = nwt + 2
    stats_s, stats_w = scr[0:3], scr[3:6]
    s_buf, p_buf, a_buf = scr[6:6 + nbuf], scr[6 + nbuf:6 + 2 * nbuf], scr[6 + 2 * nbuf:6 + 3 * nbuf]
    for m_ref, l_ref, acc_ref in (stats_s, stats_w):
        m_ref[...] = jnp.full(m_ref.shape, NEG, F32)
        l_ref[...] = jnp.zeros(l_ref.shape, F32)
        acc_ref[...] = jnp.zeros(acc_ref.shape, F32)
    key_i = lax.broadcasted_iota(I32, (tk_n, LANES), 0)
    qry_j = lax.broadcasted_iota(I32, (tk_n, LANES), 1)

    def scores(k_ref, kt, buf):
        k_tile = k_ref[kt * tk_n:(kt + 1) * tk_n, :]
        s_buf[buf][...] = jnp.dot(k_tile, q4_scr[...], preferred_element_type=F32)

    def softmax(buf, mode, stats):
        m_ref, l_ref, _ = stats
        for cb in range(nr * tq_n // LANES):
            cols = slice(cb * LANES, (cb + 1) * LANES)
            s = s_buf[buf][:, cols]
            if mode is not None:
                j = qry_j + (cb * LANES) % tq_n
                s = jnp.where(key_i <= j if mode == "causal" else key_i > j, s, -MASKED)
            m_prev = m_ref[:, cols]
            m_new = jnp.maximum(m_prev, jnp.max(s, axis=0, keepdims=True))
            alpha = jnp.exp2(m_prev - m_new)
            p = jnp.exp2(s - m_new)
            l_ref[:, cols] = alpha * l_ref[:, cols] + jnp.sum(p, axis=0, keepdims=True)
            m_ref[:, cols] = m_new
            a_buf[buf][:, cols] = alpha
            p_buf[buf][:, cols] = p.astype(BF16)

    def values(vt_ref, kt, buf, stats):
        acc_ref = stats[2]
        pv = jnp.dot(vt_ref[kt], p_buf[buf][...], preferred_element_type=F32)
        acc_ref[...] = acc_ref[...] * a_buf[buf][...] + pv

    def batch(jobs):
        for i in range(min(nbuf, len(jobs))):
            scores(jobs[i][0], jobs[i][2], i)
        for i, (_, vt_ref, kt, mode, stats) in enumerate(jobs):
            softmax(i % nbuf, mode, stats)
            values(vt_ref, kt, i % nbuf, stats)
            if i + nbuf < len(jobs):
                scores(jobs[i + nbuf][0], jobs[i + nbuf][2], i % nbuf)

    for k in range(seq // tq_n):
        @pl.when(qi == k)
        def _(k=k):
            jobs = [(ksb, vst, t, None, stats_s) for t in range(k)]
            jobs += [(kwb, vwt, k - back, "band" if back == nwt else None, stats_w)
                     for back in range(min(nwt, k), 0, -1)]
            jobs += [(kwb, vwt, k, "causal", stats_w), (ksb, vst, k, "causal", stats_s)]
            batch(jobs)

    gsel = jnp.where(lax.broadcasted_iota(I32, (LANES, LANES), 0)
                     == lax.broadcasted_iota(I32, (LANES, LANES), 1) + g * (nr * N_BRANCH), 1.0, 0.0)
    gs = gsel.astype(BF16)
    gates = jax.nn.sigmoid(jnp.dot(jnp.concatenate(_hi_lo(gate_ref[0]), axis=1), jnp.concatenate([gs, gs], axis=0),
                                   preferred_element_type=F32)).T
    (_, l_s, acc_s), (_, l_w, acc_w) = stats_s, stats_w
    for pair in range(nr // 2):
        o_t = []
        for r in (2 * pair, 2 * pair + 1):
            c0 = r * N_BRANCH
            cols = slice(r * tq_n, (r + 1) * tq_n)
            o_t.append(gates[c0:c0 + 1, :] * o_c[r] + (gates[c0 + 1:c0 + 2, :] / l_s[:, cols]) * acc_s[:, cols]
                       + (gates[c0 + 2:c0 + 3, :] / l_w[:, cols]) * acc_w[:, cols])
        o_ref[0, :, pair * LANES:(pair + 1) * LANES] = jnp.concatenate(o_t, axis=0).T


def _nsa(slopes, proj_nsa, kvc, proj_gla):
    bsz, _, s, dh = proj_nsa.shape
    g_n, nr = NSA_KV_HEADS, NSA_Q_PER_KV
    tq = NSA_TQ
    tk = NSA_TK
    assert tq == tk and WINDOW % tk == 0 and 2 * dh == LANES
    ncp = kvc.shape[2]
    kv0 = NSA_HEADS
    nq = nr * tq
    nbuf = WINDOW // tk + 2

    def kv_spec(i):
        return pl.BlockSpec((1, 1, s, dh), lambda b, g, q, i=i: (b, kv0 + i * g_n + g, 0, 0))

    return pl.pallas_call(
        _nsa_kernel,
        grid=(bsz, g_n, s // tq),
        in_specs=[pl.BlockSpec(memory_space=pltpu.SMEM),
                  pl.BlockSpec((1, nr, tq, dh), lambda b, g, q: (b, g, q, 0)),
                  pl.BlockSpec((1, 1, ncp, dh), lambda b, g, q: (b, g, 0, 0)),
                  pl.BlockSpec((1, 1, ncp, dh), lambda b, g, q: (b, g_n + g, 0, 0)),
                  kv_spec(2), kv_spec(3), kv_spec(4), kv_spec(5),
                  pl.BlockSpec((1, tq, LANES), lambda b, g, q: (b, q, GLA_MISC_OFF // LANES))],
        out_specs=pl.BlockSpec((1, tq, nr * dh), lambda b, g, q: (b, q, g)),
        out_shape=jax.ShapeDtypeStruct((bsz, s, NSA_HEADS * dh), F32),
        scratch_shapes=[pltpu.VMEM((s, LANES), BF16), pltpu.VMEM((s // tk, dh, tk), BF16)] * 2
        + [pltpu.VMEM((LANES, nq), BF16), pltpu.VMEM((s // SEL_BLOCK, tq), F32)]
        + [pltpu.VMEM((1, nq), F32), pltpu.VMEM((1, nq), F32), pltpu.VMEM((dh, nq), F32)] * 2
        + [pltpu.VMEM((tk, nq), F32)] * nbuf + [pltpu.VMEM((tk, nq), BF16)] * nbuf + [pltpu.VMEM((1, nq), F32)] * nbuf,
        compiler_params=_cparams(("parallel", "parallel", "arbitrary")),
        name="nsa_attention",
    )(slopes, proj_nsa, kvc, kvc, proj_nsa, proj_nsa, proj_nsa, proj_nsa, proj_gla)


def _gla_kernel(q_ref, k_ref, v_ref, og_ref, lr_ref, wg_ref, bg_ref, nw_ref, o_ref, st_scr, la_scr, b_scr):
    rows_n = q_ref.shape[1]
    c_n, sub = GLA_CHUNK, GLA_SUB
    nh, dk, dv = GLA_HEADS, GLA_DK, GLA_DV

    @pl.when(pl.program_id(1) == 0)
    def _():
        st_scr[...] = jnp.zeros(st_scr.shape, F32)

    z = _dot_hi_lo(lr_ref[0], wg_ref[...]) + bg_ref[...]
    la_scr[...] = (jnp.minimum(z, 0.0) - jnp.log1p(jnp.exp(-jnp.abs(z)))) * (1.0 / GLA_GATE_NORM)
    tril = jnp.where(lax.broadcasted_iota(I32, (c_n, c_n), 0) >= lax.broadcasted_iota(I32, (c_n, c_n), 1), 1.0, 0.0)
    tril3 = jnp.concatenate([tril.astype(BF16)] * 3, axis=1)
    row_c = lax.broadcasted_iota(I32, (c_n, 1), 0)
    row_s = lax.broadcasted_iota(I32, (sub, 1), 0)
    lane_c = lax.broadcasted_iota(I32, (1, c_n), 1)
    nw = nw_ref[...]
    hk = [slice(h * dk, (h + 1) * dk) for h in range(nh)]
    hv = [slice(h * dv, (h + 1) * dv) for h in range(nh)]

    def chunk(c, carry):
        r0 = pl.multiple_of(c * c_n, c_n)
        rows = pl.ds(r0, c_n)
        qc = q_ref[0, rows, :] * (dk ** -0.5)
        kc = k_ref[0, rows, :]
        vc = [v_ref[0, rows, hv[h]].astype(BF16) for h in range(nh)]
        pieces, rest = [], la_scr[rows, :]
        for _ in range(3):
            pieces.append(rest.astype(BF16))
            rest = rest - pieces[-1].astype(F32)
        b = jnp.dot(tril3, jnp.concatenate(pieces, axis=0), preferred_element_type=F32)
        b_scr[...] = b
        st = [st_scr[h] for h in range(nh)]
        q_e = (qc * jnp.exp(b)).astype(BF16)
        o = [_nt_dot(q_e[:, hk[h]], st[h].astype(BF16)) for h in range(nh)]
        strips = [[] for _ in range(nh)]
        for blk in range(c_n // sub):
            lo = blk * sub
            q_i = qc[lo:lo + sub]
            b_i = b[lo:lo + sub]
            a = [jnp.zeros((sub, c_n), F32) for _ in range(nh)]
            if blk > 0:
                b_r = b_scr[lo - 1:lo, :]
                q_d = (q_i * jnp.exp(b_i - b_r)).astype(BF16)
                k_d = (kc * jnp.exp(jnp.where(row_c < lo, b_r - b, -jnp.inf))).astype(BF16)
                a = [_nt_dot(q_d[:, hk[h]], k_d[:, hk[h]]) for h in range(nh)]
            for j in range(sub):
                b_j = b_scr[lo + j:lo + j + 1, :]
                k_j = k_ref[0, pl.ds(r0 + lo + j, 1), :]
                prod = q_i * k_j * jnp.exp(jnp.where(row_s >= j, b_i - b_j, -jnp.inf))
                for h in range(nh):
                    col = jnp.sum(prod[:, hk[h]], axis=-1, keepdims=True)
                    a[h] = jnp.where(lane_c == lo + j, col, a[h]) if blk == 0 else (
                        a[h] + jnp.where(lane_c == lo + j, col, 0.0))
            for h in range(nh):
                strips[h].append(a[h])
        for h in range(nh):
            attn = jnp.concatenate(strips[h], axis=0)
            o[h] = o[h] + jnp.dot(attn.astype(BF16), vc[h], preferred_element_type=F32)
        b_last = b_scr[c_n - 1:c_n, :]
        k_dec = (kc * jnp.exp(b_last - b)).astype(BF16)
        decay = jnp.exp(b_last)
        for h in range(nh):
            st_scr[h] = st[h] * decay[:, hk[h]] + lax.dot_general(
                vc[h], k_dec[:, hk[h]], (((0,), (0,)), ((), ())), preferred_element_type=F32)
        for h in range(nh):
            og = og_ref[0, rows, hv[h]]
            on = o[h] * lax.rsqrt(jnp.mean(o[h] * o[h], axis=-1, keepdims=True) + EPS) * nw
            o_ref[0, rows, hv[h]] = on * (og * jax.nn.sigmoid(og))
        return carry

    lax.fori_loop(0, rows_n // c_n, chunk, 0)


def _gla(proj_gla, wg_pad, bg, nw):
    bsz, s, _ = proj_gla.shape
    nh, dk, dv = GLA_HEADS, GLA_DK, GLA_DV
    ts = GLA_TS
    wk, wv = nh * dk, nh * dv
    return pl.pallas_call(
        _gla_kernel,
        grid=(bsz, s // ts),
        in_specs=[pl.BlockSpec((1, ts, wk), lambda b, i: (b, i, GLA_Q_OFF // wk)),
                  pl.BlockSpec((1, ts, wk), lambda b, i: (b, i, GLA_K_OFF // wk)),
                  pl.BlockSpec((1, ts, wv), lambda b, i: (b, i, GLA_V_OFF // wv)),
                  pl.BlockSpec((1, ts, wv), lambda b, i: (b, i, GLA_OG_OFF // wv)),
                  pl.BlockSpec((1, ts, LANES), lambda b, i: (b, i, GLA_MISC_OFF // LANES)),
                  pl.BlockSpec((LANES, wk), lambda b, i: (0, 0)),
                  pl.BlockSpec((1, wk), lambda b, i: (0, 0)),
                  pl.BlockSpec((1, dv), lambda b, i: (0, 0))],
        out_specs=pl.BlockSpec((1, ts, wv), lambda b, i: (b, i, 0)),
        out_shape=jax.ShapeDtypeStruct((bsz, s, wv), F32),
        scratch_shapes=[pltpu.VMEM((nh, dv, dk), F32), pltpu.VMEM((ts, wk), F32), pltpu.VMEM((GLA_CHUNK, wk), F32)],
        compiler_params=_cparams(("parallel", "arbitrary")),
        name="gla",
    )(proj_gla, proj_gla, proj_gla, proj_gla, proj_gla, wg_pad, bg, nw)


def _outproj_kernel(nsa_ref, gla_ref, x_ref, wo_ref, g1_ref, sc_ref, sh_ref, nw_ref, wr_ref, br_ref,
                    x1_ref, h_ref, route_ref):
    half = nsa_ref.shape[2]
    acc = jnp.dot(nsa_ref[0].astype(BF16), wo_ref[:half, :], preferred_element_type=F32)
    acc = acc + jnp.dot(gla_ref[0].astype(BF16), wo_ref[half:, :], preferred_element_type=F32)
    x1 = x_ref[0] + g1_ref[0, 0] * acc
    x1_ref[0] = x1
    h = _modulated_norm(x1, nw_ref[...], sc_ref[0, 0], sh_ref[0, 0])
    h_ref[0] = h
    h_hi = h.astype(BF16)
    h_lo = (h - h_hi.astype(F32)).astype(BF16)
    t = jnp.dot(h_hi, wr_ref[...], preferred_element_type=F32)
    logits = (t[:, :LANES] + t[:, LANES:] + jnp.dot(h_lo, wr_ref[:, :LANES], preferred_element_type=F32)
              + br_ref[...])
    lane = lax.broadcasted_iota(I32, (1, LANES), 1)
    ninf = -jnp.inf
    is_g = (lane >= N_EXPERTS) & (lane < N_EXPERTS + N_GROUPS)
    gl = jnp.where(is_g, logits, ninf)
    ge = jnp.exp(gl - jnp.max(gl, axis=-1, keepdims=True))
    gp = ge / jnp.sum(ge, axis=-1, keepdims=True)
    gp_max = jnp.max(gp, axis=-1, keepdims=True)
    grp = jnp.min(jnp.where((gp == gp_max) & is_g, lane - N_EXPERTS, LANES), axis=-1, keepdims=True)
    in_grp = (lane // EXPERTS_PER_GROUP == grp) & (lane < N_EXPERTS)
    el = jnp.where(in_grp, logits, ninf)
    v1 = jnp.max(el, axis=-1, keepdims=True)
    i1 = jnp.min(jnp.where(el == v1, lane, LANES), axis=-1, keepdims=True)
    el2 = jnp.where(lane == i1, ninf, el)
    v2 = jnp.max(el2, axis=-1, keepdims=True)
    i2 = jnp.min(jnp.where(el2 == v2, lane, LANES), axis=-1, keepdims=True)
    e2 = jnp.exp(v2 - v1)
    den = 1.0 + e2
    w1 = gp_max * (1.0 / den)
    w2 = gp_max * (e2 / den)
    route_ref[0] = jnp.where(lane == 0, i1.astype(F32), jnp.where(lane == 1, i2.astype(F32), jnp.where(
        lane == 2, w1, jnp.where(lane == 3, w2, 0.0))))


def _outproj(o_nsa, o_gla, x, wo, mod4, nw, wr, br):
    bsz, s, d = x.shape
    tm = OUT_TM
    half = o_nsa.shape[2]

    def mod_spec(idx):
        return pl.BlockSpec((1, 1, 1, d), lambda b, i: (b, idx, 0, 0))

    row = lambda w: pl.BlockSpec((1, tm, w), lambda b, i: (b, i, 0))
    return pl.pallas_call(
        _outproj_kernel,
        grid=(bsz, s // tm),
        in_specs=[row(half), row(half), row(d),
                  pl.BlockSpec((2 * half, d), lambda b, i: (0, 0), pipeline_mode=pl.Buffered(1)),
                  mod_spec(2), mod_spec(4), mod_spec(3),
                  pl.BlockSpec((1, d), lambda b, i: (0, 0)),
                  pl.BlockSpec((d, 2 * LANES), lambda b, i: (0, 0)),
                  pl.BlockSpec((1, LANES), lambda b, i: (0, 0))],
        out_specs=[row(d), row(d), row(LANES)],
        out_shape=[jax.ShapeDtypeStruct((bsz, s, d), F32), jax.ShapeDtypeStruct((bsz, s, d), F32),
                   jax.ShapeDtypeStruct((bsz, s, LANES), F32)],
        compiler_params=_cparams(("parallel", "parallel")),
        name="outproj_router",
    )(o_nsa, o_gla, x, wo, mod4, mod4, mod4, nw, wr, br)


def _rank_kernel(route_ref, dest_ref, meta_ref, rank_scr):
    n = route_ref.shape[0]
    tm = RANK_TM
    lane_i = lax.broadcasted_iota(I32, (1, LANES), 1)
    lane = lane_i.astype(F32)
    strict = jnp.where(lax.broadcasted_iota(I32, (tm, tm), 0) > lax.broadcasted_iota(I32, (tm, tm), 1),
                       1.0, 0.0).astype(BF16)

    def two_lanes(a, b):
        return jnp.where(lane_i == 0, a, jnp.where(lane_i == 1, b, 0.0))

    def pick(e, table):
        return jnp.sum(jnp.where(lane == e, table, 0.0), axis=-1, keepdims=True)

    def count(i, seen):
        r0 = pl.multiple_of(i * tm, tm)
        rt = route_ref[pl.ds(r0, tm), :]
        e1, e2 = rt[:, 0:1], rt[:, 1:2]
        member = jnp.where(lane == e1, 1.0, jnp.where(lane == e2, 1.0, 0.0))
        before = jnp.dot(strict, member.astype(BF16), preferred_element_type=F32) + seen
        rank_scr[pl.ds(r0, tm), :] = two_lanes(pick(e1, before), pick(e2, before))
        return seen + jnp.sum(member, axis=0, keepdims=True)

    counts = lax.fori_loop(0, n // tm, count, jnp.zeros((1, LANES), F32), unroll=4)
    ntile = jnp.floor((counts + (MOE_TB - 1)) * (1.0 / MOE_TB))
    incl = jnp.where(lax.broadcasted_iota(I32, (LANES, LANES), 0) <= lax.broadcasted_iota(I32, (LANES, LANES), 1),
                     1.0, 0.0).astype(BF16)
    tile_end = jnp.dot(jnp.broadcast_to(ntile, (8, LANES)).astype(BF16), incl,
                       preferred_element_type=F32)[0:1]
    row_start = (tile_end - ntile) * MOE_TB

    def place(i, carry):
        r0 = pl.multiple_of(i * tm, tm)
        rt = route_ref[pl.ds(r0, tm), :]
        rk = rank_scr[pl.ds(r0, tm), :]
        d1 = pick(rt[:, 0:1], row_start) + rk[:, 0:1]
        d2 = pick(rt[:, 1:2], row_start) + rk[:, 1:2]
        dest_ref[pl.ds(r0, tm), :] = two_lanes(d1, d2).astype(I32)
        return carry

    lax.fori_loop(0, n // tm, place, 0, unroll=4)
    trow = lax.broadcasted_iota(I32, (meta_ref.shape[0], 1), 0).astype(F32)
    texp = jnp.sum(jnp.where((tile_end <= trow) & (lane_i < N_EXPERTS), 1.0, 0.0), axis=-1, keepdims=True)
    texp = jnp.minimum(texp, N_EXPERTS - 1.0)
    used = pick(N_EXPERTS - 1.0, tile_end)
    diag = lax.broadcasted_iota(I32, (meta_ref.shape[0], LANES), 0) == lane_i
    end_rows = jnp.sum(jnp.where(diag, tile_end, 0.0), axis=-1, keepdims=True)
    ntile_rows = jnp.sum(jnp.where(diag, ntile, 0.0), axis=-1, keepdims=True)
    meta_ref[...] = jnp.where(lane_i == 2, end_rows, jnp.where(lane_i == 3, ntile_rows, two_lanes(
        texp, jnp.broadcast_to(used, texp.shape)))).astype(I32)


def _rank(route):
    n = route.shape[0]
    return pl.pallas_call(
        _rank_kernel,
        out_shape=[jax.ShapeDtypeStruct((n, LANES), I32), jax.ShapeDtypeStruct((LANES, LANES), I32)],
        scratch_shapes=[pltpu.VMEM((n, LANES), F32)],
        compiler_params=pltpu.CompilerParams(vmem_limit_bytes=VMEM_LIMIT),
        name="moe_rank",
    )(route)


def _dispatch_kernel(dest_ref, ends_ref, h_ref, xs_ref, zero_scr, sem, zsem):
    i = pl.program_id(0)
    tm = h_ref.shape[0]
    tb = zero_scr.shape[0]

    @pl.when(i == 0)
    def _():
        zero_scr[...] = jnp.zeros(zero_scr.shape, F32)

        def zero_copy(e):
            r0 = pl.multiple_of((ends_ref[e] - 1) * tb, tb)
            return pltpu.make_async_copy(zero_scr, xs_ref.at[pl.ds(r0, tb)], zsem)

        def start(e, carry):
            @pl.when(ends_ref[N_EXPERTS + e] > 0)
            def _():
                zero_copy(e).start()
            return carry

        def wait(e, carry):
            @pl.when(ends_ref[N_EXPERTS + e] > 0)
            def _():
                zero_copy(e).wait()
            return carry

        def tail_copy(t):
            return pltpu.make_async_copy(zero_scr, xs_ref.at[pl.ds(pl.multiple_of(t * tb, tb), tb)], zsem)

        def tail_start(t, carry):
            tail_copy(t).start()
            return carry

        def tail_wait(t, carry):
            tail_copy(t).wait()
            return carry

        used = ends_ref[2 * N_EXPERTS]
        lax.fori_loop(0, N_EXPERTS, start, 0)
        lax.fori_loop(used, xs_ref.shape[0] // tb, tail_start, 0)
        lax.fori_loop(0, N_EXPERTS, wait, 0)
        lax.fori_loop(used, xs_ref.shape[0] // tb, tail_wait, 0)

    def issue(r, carry):
        p = (i * tm + r) * TOP_K
        for k in range(TOP_K):
            pltpu.make_async_copy(h_ref.at[pl.ds(r, 1)], xs_ref.at[pl.ds(dest_ref[p + k], 1)], sem).start()
        return carry

    lax.fori_loop(0, tm, issue, 0, unroll=DMA_UNROLL)
    for k in range(TOP_K):
        pltpu.make_async_copy(h_ref, xs_ref.at[pl.ds(0, tm)], sem).wait()


def _dispatch(dest_flat, ends_flat, h, cap):
    n, d = h.shape
    tm = DISPATCH_TM
    return pl.pallas_call(
        _dispatch_kernel,
        grid_spec=pltpu.PrefetchScalarGridSpec(
            num_scalar_prefetch=2, grid=(n // tm,),
            in_specs=[pl.BlockSpec((tm, d), lambda i, dst, ends: (i, 0))],
            out_specs=pl.BlockSpec(memory_space=pl.ANY),
            scratch_shapes=[pltpu.VMEM((MOE_TB, d), F32), pltpu.SemaphoreType.DMA(()), pltpu.SemaphoreType.DMA(())]),
        out_shape=jax.ShapeDtypeStruct((cap, d), F32),
        compiler_params=_cparams(("arbitrary",)),
        name="moe_dispatch",
    )(dest_flat, ends_flat, h)


def _ffn_kernel(meta_ref, x_ref, wg_hbm, wu_hbm, wd_hbm, y_ref, wg_st, wu_st, wd_st, wgb, wub, wdb, wsem):
    t = pl.program_id(0)
    ntile = pl.num_programs(0)
    n_used = meta_ref[ntile]
    e = meta_ref[t]
    e_prev = meta_ref[jnp.maximum(t - 1, 0)]
    active = t < n_used

    def fetch(expert):
        return [pltpu.make_async_copy(src.at[expert], dst, wsem)
                for src, dst in ((wg_hbm, wg_st), (wu_hbm, wu_st), (wd_hbm, wd_st))]

    @pl.when(t == 0)
    def _():
        for cp in fetch(e):
            cp.start()

    @pl.when(active & ((t == 0) | (e != e_prev)))
    def _():
        for cp in fetch(e):
            cp.wait()
        wgb[...] = wg_st[...].astype(BF16)
        wub[...] = wu_st[...].astype(BF16)
        wdb[...] = wd_st[...].astype(BF16)
        end = meta_ref[ntile + 1 + e]

        @pl.when(end < n_used)
        def _():
            for cp in fetch(meta_ref[end]):
                cp.start()

    @pl.when(active)
    def _():
        x = x_ref[...].astype(BF16)
        gate = jnp.dot(x, wgb[...], preferred_element_type=F32)
        up = jnp.dot(x, wub[...], preferred_element_type=F32)
        act = (gate * jax.nn.sigmoid(gate)) * up
        y_ref[...] = jnp.dot(act.astype(BF16), wdb[...], preferred_element_type=F32)

    @pl.when(jnp.logical_not(active))
    def _():
        y_ref[...] = jnp.zeros(y_ref.shape, F32)


def _ffn(meta_flat, xs, wg, wu, wd):
    cap, d = xs.shape
    ff = wg.shape[2]
    tb = MOE_TB
    ntile = cap // tb
    return pl.pallas_call(
        _ffn_kernel,
        grid_spec=pltpu.PrefetchScalarGridSpec(
            num_scalar_prefetch=1, grid=(ntile,),
            in_specs=[pl.BlockSpec((tb, d), lambda t, m: (jnp.minimum(t, m[ntile] - 1), 0)),
                      pl.BlockSpec(memory_space=pl.ANY), pl.BlockSpec(memory_space=pl.ANY),
                      pl.BlockSpec(memory_space=pl.ANY)],
            out_specs=pl.BlockSpec((tb, d), lambda t, m: (t, 0)),
            scratch_shapes=[pltpu.VMEM((d, ff), F32), pltpu.VMEM((d, ff), F32), pltpu.VMEM((ff, d), F32),
                            pltpu.VMEM((d, ff), BF16), pltpu.VMEM((d, ff), BF16), pltpu.VMEM((ff, d), BF16),
                            pltpu.SemaphoreType.DMA(())]),
        out_shape=jax.ShapeDtypeStruct((cap, d), F32),
        compiler_params=_cparams(("arbitrary",)),
        name="moe_ffn",
    )(meta_flat, xs, wg, wu, wd)


def _combine_kernel(dest_ref, y_ref, x1_ref, route_ref, g2_ref, nf_ref, o_ref, ybuf, sem):
    i = pl.program_id(0)
    nstep = pl.num_programs(0)
    tm = x1_ref.shape[0]

    def issue(tile, slot):
        def body(r, carry):
            p = (tile * tm + r) * TOP_K
            for k in range(TOP_K):
                pltpu.make_async_copy(y_ref.at[pl.ds(dest_ref[p + k], 1)], ybuf.at[slot, k, pl.ds(r, 1)],
                                      sem.at[slot]).start()
            return carry

        lax.fori_loop(0, tm, body, 0, unroll=DMA_UNROLL)

    @pl.when(i == 0)
    def _():
        issue(0, 0)

    @pl.when(i + 1 < nstep)
    def _():
        issue(i + 1, (i + 1) % 2)

    slot = i % 2
    for k in range(TOP_K):
        pltpu.make_async_copy(y_ref.at[pl.ds(0, tm)], ybuf.at[slot, k], sem.at[slot]).wait()
    rt = route_ref[...]
    moe = rt[:, 2:3] * ybuf[slot, 0] + rt[:, 3:4] * ybuf[slot, 1]
    xo = x1_ref[...] + g2_ref[0, 0] * moe
    o_ref[...] = xo * lax.rsqrt(jnp.mean(xo * xo, axis=-1, keepdims=True) + EPS) * nf_ref[...]


def _combine(dest_flat, y, x1, route, mod4, nf, seq):
    n, d = x1.shape
    tm = COMB_TM
    tiles_per_seq = seq // tm
    return pl.pallas_call(
        _combine_kernel,
        grid_spec=pltpu.PrefetchScalarGridSpec(
            num_scalar_prefetch=1, grid=(n // tm,),
            in_specs=[pl.BlockSpec(memory_space=pl.ANY),
                      pl.BlockSpec((tm, d), lambda i, dst: (i, 0)),
                      pl.BlockSpec((tm, LANES), lambda i, dst: (i, 0)),
                      pl.BlockSpec((1, 1, 1, d), lambda i, dst: (i // tiles_per_seq, 5, 0, 0)),
                      pl.BlockSpec((1, d), lambda i, dst: (0, 0))],
            out_specs=pl.BlockSpec((tm, d), lambda i, dst: (i, 0)),
            scratch_shapes=[pltpu.VMEM((2, TOP_K, tm, d), F32), pltpu.SemaphoreType.DMA((2,))]),
        out_shape=jax.ShapeDtypeStruct((n, d), F32),
        compiler_params=_cparams(("arbitrary",)),
        name="moe_combine",
    )(dest_flat, y, x1, route, mod4, nf)


def _alibi_slopes():
    n = NSA_HEADS
    full = jnp.asarray(2.0 ** (-8.0 * np.arange(1, n + 1) / n), dtype=F32)
    pieces, rest = [], full * LOG2E
    for _ in range(3):
        piece = rest.astype(BF16).astype(F32)
        pieces.append(piece)
        rest = rest - piece
    return jnp.concatenate([full] + pieces)


def _layer(x, c, w_ada, b_ada, norm1_w, w_in, cmp_pos_k, cmp_w1_k, cmp_w2_k, cmp_pos_v, cmp_w1_v, cmp_w2_v,
           gla_w_gate2, gla_b_gate, gla_norm_w, w_out, norm2_w, w_rg, b_rg, w_re, b_re, w_eg, w_eu, w_ed):
    bsz, s, d = x.shape
    dh = NSA_HEAD_DIM
    mod4 = _adaln(c, w_ada, b_ada).reshape(bsz, 6, 1, d)

    o_gate = NSA_COLS
    o_gla = o_gate + NSA_GATE_COLS
    o_lr = o_gla + 2 * GLA_HEADS * GLA_DK + 2 * GLA_HEADS * GLA_DV
    w_nsa, w_gla = _prep_w_in(w_in.T, o_gate, o_gla, o_lr)
    nw1 = norm1_w.reshape(1, d)
    proj_nsa, proj_gla = _inproj(x, mod4, nw1, w_nsa, w_gla, sc_idx=1, sh_idx=0)

    pos = jnp.stack([cmp_pos_k, cmp_pos_v]).reshape(2, 2, CMP_STRIDE * dh)
    kvc = _compress(proj_nsa, pos, jnp.stack([cmp_w1_k, cmp_w1_v]), jnp.stack([cmp_w2_k, cmp_w2_v]))
    o_nsa = _nsa(_alibi_slopes(), proj_nsa, kvc, proj_gla)

    wg_pad = jnp.zeros((LANES, GLA_HEADS * GLA_DK), F32).at[
        NSA_GATE_COLS:NSA_GATE_COLS + GLA_GATE_RANK].set(gla_w_gate2)
    o_gla_out = _gla(proj_gla, wg_pad, gla_b_gate.reshape(1, -1), gla_norm_w.reshape(1, -1))

    wr = jnp.concatenate([w_re, w_rg, jnp.zeros((d, LANES - N_EXPERTS - N_GROUPS), F32)], axis=1)
    br = jnp.concatenate([b_re, b_rg, jnp.zeros((LANES - N_EXPERTS - N_GROUPS,), F32)]).reshape(1, LANES)
    wr_hi = wr.astype(BF16)
    wr_cat = jnp.concatenate([wr_hi, (wr - wr_hi.astype(F32)).astype(BF16)], axis=1)
    x1, h2, route = _outproj(o_nsa, o_gla_out, x, w_out.astype(BF16), mod4, norm2_w.reshape(1, d), wr_cat, br)

    n = bsz * s
    npair = n * TOP_K
    cap = npair + N_EXPERTS * MOE_TB
    ntile = cap // MOE_TB
    route2 = route.reshape(n, LANES)
    dest, meta = _rank(route2)
    dest_flat = dest[:, :TOP_K].reshape(npair)
    meta_flat = jnp.concatenate([meta[:ntile, 0], meta[:1, 1], meta[:N_EXPERTS, 2]])
    ends_flat = jnp.concatenate([meta[:N_EXPERTS, 2], meta[:N_EXPERTS, 3], meta[:1, 1]])
    xs = _dispatch(dest_flat, ends_flat, h2.reshape(n, d), cap)
    y = _ffn(meta_flat, xs, w_eg, w_eu, w_ed)
    return x1.reshape(n, d), y, dest_flat, route2, mod4


def kernel(x, c, w_ada, b_ada, norm1_w, w_in, cmp_pos_k, cmp_w1_k, cmp_w2_k, cmp_pos_v, cmp_w1_v, cmp_w2_v,
           gla_w_gate2, gla_b_gate, gla_norm_w, w_out, norm2_w, w_router_group, b_router_group, w_router_expert,
           b_router_expert, w_expert_gate, w_expert_up, w_expert_down, norm_f_w):
    bsz, s, d = x.shape
    assert w_ada.shape[0] == 1, "single layer"
    x1, y, dest_flat, route2, mod4 = _layer(
        x, c, w_ada[0], b_ada[0], norm1_w[0], w_in[0], cmp_pos_k[0], cmp_w1_k[0], cmp_w2_k[0], cmp_pos_v[0],
        cmp_w1_v[0], cmp_w2_v[0], gla_w_gate2[0], gla_b_gate[0], gla_norm_w[0], w_out[0], norm2_w[0],
        w_router_group[0], b_router_group[0], w_router_expert[0], b_router_expert[0],
        w_expert_gate[0], w_expert_up[0], w_expert_down[0])
    out = _combine(dest_flat, y, x1, route2, mod4, norm_f_w.reshape(1, d), s)
    return out.reshape(bsz, s, d)
```

```python
import functools

import numpy as np
import jax
import jax.numpy as jnp
from jax import lax
from jax.experimental import pallas as pl
from jax.experimental.pallas import tpu as pltpu

F32 = jnp.float32
BF16 = jnp.bfloat16
I32 = jnp.int32

D_MODEL = 2048
NSA_HEAD_DIM = 64
NSA_HEADS = 16
NSA_KV_HEADS = 4
NSA_Q_PER_KV = 4
CMP_BLOCK = 32
CMP_STRIDE = 16
SEL_BLOCK = 64
N_SEL = 16
WINDOW = 512
N_BRANCH = 3
GLA_HEADS = 4
GLA_DV = 256
GLA_DK = 128
GLA_GATE_RANK = 16
GLA_GATE_NORM = 16.0
GLA_CHUNK = 64
GLA_SUB = 8
N_GROUPS = 4
EXPERTS_PER_GROUP = 8
N_EXPERTS = 32
TOP_K = 2
EXPERT_FF = 512
EPS = 1e-6
NEG = -1e30
FORCE = 1e30
LOG2E = 1.4426950408889634
MASKED = 2.0 ** 100

NSA_Q_COLS = NSA_HEADS * NSA_HEAD_DIM
NSA_KV_COLS = 2 * N_BRANCH * NSA_KV_HEADS * NSA_HEAD_DIM
NSA_GATE_COLS = N_BRANCH * NSA_HEADS
NSA_COLS = NSA_Q_COLS + NSA_KV_COLS
NSA_SLOTS = NSA_COLS // NSA_HEAD_DIM
GLA_Q_OFF = 0
GLA_K_OFF = GLA_HEADS * GLA_DK
GLA_V_OFF = 2 * GLA_HEADS * GLA_DK
GLA_OG_OFF = GLA_V_OFF + GLA_HEADS * GLA_DV
GLA_MISC_OFF = GLA_OG_OFF + GLA_HEADS * GLA_DV
LANES = 128
GLA_COLS = GLA_MISC_OFF + LANES

VMEM_LIMIT = 56 * 1024 * 1024

ADA_TN = 768
PREP_TR = 256
PREP_STEP = 512
INPROJ_TM = 256
INPROJ_TN = 512
NSA_TQ = 256
NSA_TK = 256
GLA_TS = 512
OUT_TM = 512
RANK_TM = 256
MOE_TB = 256
DISPATCH_TM = 1024
COMB_TM = 512
DMA_UNROLL = 8


def _cparams(sem):
    return pltpu.CompilerParams(dimension_semantics=sem, vmem_limit_bytes=VMEM_LIMIT)


def _adaln_kernel(c_ref, w_ref, b_ref, o_ref):
    c = c_ref[...]
    s = (c * jax.nn.sigmoid(c)).astype(BF16)
    o_ref[...] = jnp.dot(s, w_ref[...].astype(BF16), preferred_element_type=F32) + b_ref[...]


def _adaln(c, w, b):
    nb, d = c.shape
    n = w.shape[1]
    return pl.pallas_call(
        _adaln_kernel,
        grid=(n // ADA_TN,),
        in_specs=[pl.BlockSpec((nb, d), lambda j: (0, 0)),
                  pl.BlockSpec((d, ADA_TN), lambda j: (0, j)),
                  pl.BlockSpec((1, ADA_TN), lambda j: (0, j))],
        out_specs=pl.BlockSpec((nb, ADA_TN), lambda j: (0, j)),
        out_shape=jax.ShapeDtypeStruct((nb, n), F32),
        compiler_params=_cparams(("parallel",)),
        name="adaln",
    )(c, w, b.reshape(1, n))


def _modulated_norm(x, nw, sc, sh):
    ms = jnp.mean(x * x, axis=-1, keepdims=True)
    h = x * lax.rsqrt(ms + EPS) * nw
    return h * (1.0 + sc) + sh


def _prep_w_in_kernel(wt_ref, wn_ref, wg_ref, *, o_gate, o_gla, o_lr):
    tr = wt_ref.shape[1]
    step = PREP_STEP

    def put(dst, c0, rows):
        dst[:, c0:c0 + rows.shape[0]] = rows.T.astype(BF16)

    for c in range(0, o_gate, step):
        put(wn_ref, c, wt_ref[c:c + step, :])
    for c in range(0, o_lr - o_gla, step):
        put(wg_ref, c, wt_ref[o_gla + c:o_gla + c + step, :])
    pad = wg_ref.shape[1] - (wt_ref.shape[0] - o_gate)
    misc = jnp.concatenate([wt_ref[o_gate:o_gla, :], wt_ref[o_lr:, :], jnp.zeros((pad, tr), F32)], axis=0)
    put(wg_ref, o_lr - o_gla, misc)


def _prep_w_in(w_in_t, o_gate, o_gla, o_lr):
    n, d = w_in_t.shape
    tr = PREP_TR
    assert o_gate % PREP_STEP == 0 and (o_lr - o_gla) % PREP_STEP == 0
    return pl.pallas_call(
        functools.partial(_prep_w_in_kernel, o_gate=o_gate, o_gla=o_gla, o_lr=o_lr),
        grid=(d // tr,),
        in_specs=[pl.BlockSpec((n, tr), lambda i: (0, i))],
        out_specs=[pl.BlockSpec((tr, NSA_COLS), lambda i: (i, 0)), pl.BlockSpec((tr, GLA_COLS), lambda i: (i, 0))],
        out_shape=[jax.ShapeDtypeStruct((d, NSA_COLS), BF16), jax.ShapeDtypeStruct((d, GLA_COLS), BF16)],
        compiler_params=_cparams(("parallel",)),
        name="prep_w_in",
    )(w_in_t)


def _inproj_kernel(x_ref, sc_ref, sh_ref, nw_ref, wn_ref, wg_ref, on_ref, og_ref):
    h = _modulated_norm(x_ref[0], nw_ref[...], sc_ref[0, 0], sh_ref[0, 0]).astype(BF16)
    dh = NSA_HEAD_DIM
    tn = INPROJ_TN
    for c in range(wn_ref.shape[1] // tn):
        acc = jnp.dot(h, wn_ref[:, c * tn:(c + 1) * tn], preferred_element_type=F32)
        for u in range(tn // dh):
            on_ref[0, c * (tn // dh) + u] = acc[:, u * dh:(u + 1) * dh]
    og_ref[0] = jnp.dot(h, wg_ref[...], preferred_element_type=F32)


def _inproj(x, mod4, nw, w_nsa, w_gla, *, sc_idx, sh_idx):
    bsz, s, d = x.shape
    n_nsa, n_gla = w_nsa.shape[1], w_gla.shape[1]
    tm = INPROJ_TM
    dh = NSA_HEAD_DIM
    return pl.pallas_call(
        _inproj_kernel,
        grid=(bsz, s // tm),
        in_specs=[pl.BlockSpec((1, tm, d), lambda b, i: (b, i, 0)),
                  pl.BlockSpec((1, 1, 1, d), lambda b, i: (b, sc_idx, 0, 0)),
                  pl.BlockSpec((1, 1, 1, d), lambda b, i: (b, sh_idx, 0, 0)),
                  pl.BlockSpec((1, d), lambda b, i: (0, 0)),
                  pl.BlockSpec((d, n_nsa), lambda b, i: (0, 0), pipeline_mode=pl.Buffered(1)),
                  pl.BlockSpec((d, n_gla), lambda b, i: (0, 0), pipeline_mode=pl.Buffered(1))],
        out_specs=[pl.BlockSpec((1, n_nsa // dh, tm, dh), lambda b, i: (b, 0, i, 0)),
                   pl.BlockSpec((1, tm, n_gla), lambda b, i: (b, i, 0))],
        out_shape=[jax.ShapeDtypeStruct((bsz, n_nsa // dh, s, dh), F32),
                   jax.ShapeDtypeStruct((bsz, s, n_gla), F32)],
        compiler_params=_cparams(("parallel", "parallel")),
        name="inproj",
    )(x, mod4, mod4, nw, w_nsa, w_gla)


def _hi_lo(x):
    hi = x.astype(BF16)
    return hi, (x - hi.astype(F32)).astype(BF16)


def _dot_hi_lo(x, w):
    x_hi, x_lo = _hi_lo(x)
    w_hi, w_lo = _hi_lo(w)
    return jnp.dot(jnp.concatenate([x_hi, x_lo, x_hi], axis=1), jnp.concatenate([w_hi, w_hi, w_lo], axis=0),
                   preferred_element_type=F32)


def _compress_kernel(a_ref, pos_ref, w1_ref, w2_ref, o_ref):
    ng = a_ref.shape[1]
    nch = a_ref.shape[2] // CMP_STRIDE
    a = jnp.concatenate([
        jnp.concatenate([a_ref[0, g, pl.ds(t, nch, stride=CMP_STRIDE), :] for t in range(CMP_STRIDE)], axis=1)
        for g in range(ng)], axis=0)
    pos = pos_ref[0]
    half = a.shape[1]
    y1 = _dot_hi_lo(a + pos[0:1], w1_ref[0, :half, :])
    y2 = _dot_hi_lo(a + pos[1:2], w1_ref[0, half:, :])
    h = y1 + pltpu.roll(y2, a.shape[0] - 1, axis=0)
    out = _dot_hi_lo(jax.nn.gelu(h), w2_ref[0])
    row = lax.broadcasted_iota(I32, (nch, out.shape[1]), 0)
    for g in range(ng):
        o_ref[0, g] = jnp.where(row < nch - 1, out[g * nch:(g + 1) * nch], 0.0)


def _compress(proj_nsa, pos, w1, w2):
    bsz, _, s, dh = proj_nsa.shape
    ng = NSA_KV_HEADS
    nch = s // CMP_STRIDE
    cw = CMP_STRIDE * dh
    return pl.pallas_call(
        _compress_kernel,
        grid=(bsz, 2),
        in_specs=[pl.BlockSpec((1, ng, s, dh), lambda b, t: (b, NSA_HEADS // ng + t, 0, 0)),
                  pl.BlockSpec((1, 2, cw), lambda b, t: (t, 0, 0)),
                  pl.BlockSpec((1, 2 * cw, dh), lambda b, t: (t, 0, 0)),
                  pl.BlockSpec((1, dh, dh), lambda b, t: (t, 0, 0))],
        out_specs=pl.BlockSpec((1, ng, nch, dh), lambda b, t: (b, t, 0, 0)),
        out_shape=jax.ShapeDtypeStruct((bsz, 2 * ng, nch, dh), F32),
        compiler_params=_cparams(("parallel", "parallel")),
        name="nsa_compress",
    )(proj_nsa, pos, w1, w2)


def _nt_dot(a, b, **kw):
    return lax.dot_general(a, b, (((1,), (1,)), ((), ())), preferred_element_type=F32, **kw)


def _nsa_kernel(slopes_ref, q_ref, kc_ref, vc_ref, ks_ref, vs_ref, kw_ref, vw_ref, gate_ref, o_ref,
                ksb, vst, kwb, vwt, q4_scr, notsel_scr, *scr):
    g = pl.program_id(1)
    qi = pl.program_id(2)
    tq_n = q_ref.shape[2]
    dh = NSA_HEAD_DIM
    nr = NSA_Q_PER_KV
    seq = ks_ref.shape[2]
    tk_n = NSA_TK
    nb = seq // SEL_BLOCK

    @pl.when(qi == 0)
    def _():
        row = lax.broadcasted_iota(I32, (seq, dh), 0)
        lane = lax.broadcasted_iota(I32, (seq, dh), 1)
        blk = row // SEL_BLOCK
        pos = jnp.where((lane >= nb) & (lane < nb + 3), (blk * SEL_BLOCK).astype(F32),
                        jnp.where((lane >= nb + 3) & (lane < nb + 6), (row % SEL_BLOCK).astype(F32), 0.0))
        ksb[...] = jnp.concatenate([jnp.where(lane == blk, -MASKED, pos), ks_ref[0, 0]], axis=1).astype(BF16)
        kwb[...] = jnp.concatenate([pos, kw_ref[0, 0]], axis=1).astype(BF16)
        for c in range(seq // tk_n):
            rows = slice(c * tk_n, (c + 1) * tk_n)
            for src, dst in ((vs_ref, vst), (vw_ref, vwt)):
                v = src[0, 0, rows, :]
                dst[c] = jnp.concatenate([v, v], axis=1).T[:dh].astype(BF16)

    t0 = qi * tq_n
    tq = t0 + lax.broadcasted_iota(I32, (1, tq_n), 1)
    slopes = [slopes_ref[g * nr + r] for r in range(nr)]
    scale = dh ** -0.5
    q_t = []
    for pair in range(nr // 2):
        both = jnp.concatenate([q_ref[0, 2 * pair], q_ref[0, 2 * pair + 1]], axis=1).T * scale
        q_t += [both[:dh], both[dh:]]

    ncp = kc_ref.shape[2]
    kc = kc_ref[0, 0]
    vc = vc_ref[0, 0]
    vc_t = jnp.concatenate([vc, vc], axis=1).T[:dh].astype(BF16)
    n_sub = lax.broadcasted_iota(I32, (ncp, 1), 0)
    blk_end = n_sub * CMP_STRIDE + (CMP_BLOCK - 1)
    center = n_sub.astype(F32) * CMP_STRIDE + (CMP_BLOCK - 1) / 2.0
    kc_hi, kc_lo = _hi_lo(kc)
    kc_cat = jnp.concatenate([kc_hi, kc_lo, kc_hi, jnp.zeros_like(kc_hi)], axis=1)
    q_hi, q_lo = _hi_lo(jnp.concatenate(q_t, axis=1))
    q_cat = jnp.concatenate([q_hi, q_hi, q_lo, jnp.zeros_like(q_hi)], axis=0)
    s = jnp.dot(kc_cat, q_cat, preferred_element_type=F32)
    tq_all = jnp.concatenate([tq] * nr, axis=1)
    slope_all = jnp.concatenate([jnp.full((1, tq_n), slopes[r], F32) for r in range(nr)], axis=1)
    valid_all = blk_end <= tq_all
    s = jnp.where(valid_all, s - slope_all * (tq_all.astype(F32) - center), NEG)
    e = jnp.exp(s - jnp.max(s, axis=0, keepdims=True))
    p = jnp.where(valid_all, e / jnp.sum(e, axis=0, keepdims=True), 0.0)
    psum = sum(p[:, r * tq_n:(r + 1) * tq_n] for r in range(nr))
    o_c_all = jnp.dot(vc_t, p.astype(BF16), preferred_element_type=F32)
    o_c = [o_c_all[:, r * tq_n:(r + 1) * tq_n] for r in range(nr)]

    n_sel = min(N_SEL, nb)
    notsel_scr[...] = jnp.zeros(notsel_scr.shape, F32)

    @pl.when((qi + 1) * tq_n > n_sel * SEL_BLOCK)
    def _():
        rowj = lax.broadcasted_iota(I32, (LANES, ncp), 0) * SEL_BLOCK
        coln = lax.broadcasted_iota(I32, (LANES, ncp), 1) * CMP_STRIDE
        overlap = jnp.where((coln < rowj + SEL_BLOCK) & (coln + CMP_BLOCK > rowj)
                            & (coln < (ncp - 1) * CMP_STRIDE) & (rowj < nb * SEL_BLOCK), 1.0, 0.0)
        ov = overlap.astype(BF16)
        imp = jnp.dot(jnp.concatenate([ov, ov], axis=1), jnp.concatenate(_hi_lo(psum), axis=0),
                      preferred_element_type=F32)[:nb]
        j_sub = lax.broadcasted_iota(I32, (nb, 1), 0)
        qblk = tq // SEL_BLOCK
        forced = (j_sub == 0) | (j_sub == qblk) | (j_sub == qblk - 1)
        imp = jnp.where(forced, FORCE, jnp.where(j_sub <= qblk, imp, NEG))
        cnt = jnp.zeros((nb, tq_n), F32)
        for i in range(nb):
            ci = imp[i:i + 1, :]
            tie = jnp.where(j_sub > i, 1.0, 0.0)
            cnt = cnt + jnp.where(ci > imp, 1.0, jnp.where(ci == imp, tie, 0.0))
        notsel_scr[...] = jnp.where(cnt < float(n_sel), 0.0, 1.0)

    notsel = notsel_scr[...]

    sub_h = lax.broadcasted_iota(I32, (dh - nb, 1), 0)
    for r in range(nr):
        scol = jnp.zeros((dh - nb, 1), F32)
        for i in range(3):
            piece = slopes_ref[(i + 1) * NSA_HEADS + g * nr + r]
            scol = jnp.where((sub_h == i) | (sub_h == 3 + i), piece, scol)
        q4_scr[:, r * tq_n:(r + 1) * tq_n] = jnp.concatenate(
            [notsel, jnp.broadcast_to(scol, (dh - nb, tq_n)), q_t[r] * LOG2E], axis=0).astype(BF16)

    nwt = WINDOW // tk_n
    nbuf = nwt + 2
    stats_s, stats_w = scr[0:3], scr[3:6]
    s_buf, p_buf, a_buf = scr[6:6 + nbuf], scr[6 + nbuf:6 + 2 * nbuf], scr[6 + 2 * nbuf:6 + 3 * nbuf]
    for m_ref, l_ref, acc_ref in (stats_s, stats_w):
        m_ref[...] = jnp.full(m_ref.shape, NEG, F32)
        l_ref[...] = jnp.zeros(l_ref.shape, F32)
        acc_ref[...] = jnp.zeros(acc_ref.shape, F32)
    key_i = lax.broadcasted_iota(I32, (tk_n, LANES), 0)
    qry_j = lax.broadcasted_iota(I32, (tk_n, LANES), 1)

    def scores(k_ref, kt, buf):
        k_tile = k_ref[kt * tk_n:(kt + 1) * tk_n, :]
        s_buf[buf][...] = jnp.dot(k_tile, q4_scr[...], preferred_element_type=F32)

    def softmax(buf, mode, stats):
        m_ref, l_ref, _ = stats
        for cb in range(nr * tq_n // LANES):
            cols = slice(cb * LANES, (cb + 1) * LANES)
            s = s_buf[buf][:, cols]
            if mode is not None:
                j = qry_j + (cb * LANES) % tq_n
                s = jnp.where(key_i <= j if mode == "causal" else key_i > j, s, -MASKED)
            m_prev = m_ref[:, cols]
            m_new = jnp.maximum(m_prev, jnp.max(s, axis=0, keepdims=True))
            alpha = jnp.exp2(m_prev - m_new)
            p = jnp.exp2(s - m_new)
            l_ref[:, cols] = alpha * l_ref[:, cols] + jnp.sum(p, axis=0, keepdims=True)
            m_ref[:, cols] = m_new
            a_buf[buf][:, cols] = alpha
            p_buf[buf][:, cols] = p.astype(BF16)

    def values(vt_ref, kt, buf, stats):
        acc_ref = stats[2]
        pv = jnp.dot(vt_ref[kt], p_buf[buf][...], preferred_element_type=F32)
        acc_ref[...] = acc_ref[...] * a_buf[buf][...] + pv

    def batch(jobs):
        for i in range(min(nbuf, len(jobs))):
            scores(jobs[i][0], jobs[i][2], i)
        for i, (_, vt_ref, kt, mode, stats) in enumerate(jobs):
            softmax(i % nbuf, mode, stats)
            values(vt_ref, kt, i % nbuf, stats)
            if i + nbuf < len(jobs):
                scores(jobs[i + nbuf][0], jobs[i + nbuf][2], i % nbuf)

    for k in range(seq // tq_n):
        @pl.when(qi == k)
        def _(k=k):
            jobs = [(ksb, vst, t, None, stats_s) for t in range(k)]
            jobs += [(kwb, vwt, k - back, "band" if back == nwt else None, stats_w)
                     for back in range(min(nwt, k), 0, -1)]
            jobs += [(kwb, vwt, k, "causal", stats_w), (ksb, vst, k, "causal", stats_s)]
            batch(jobs)

    gsel = jnp.where(lax.broadcasted_iota(I32, (LANES, LANES), 0)
                     == lax.broadcasted_iota(I32, (LANES, LANES), 1) + g * (nr * N_BRANCH), 1.0, 0.0)
    gs = gsel.astype(BF16)
    gates = jax.nn.sigmoid(jnp.dot(jnp.concatenate(_hi_lo(gate_ref[0]), axis=1), jnp.concatenate([gs, gs], axis=0),
                                   preferred_element_type=F32)).T
    (_, l_s, acc_s), (_, l_w, acc_w) = stats_s, stats_w
    for pair in range(nr // 2):
        o_t = []
        for r in (2 * pair, 2 * pair + 1):
            c0 = r * N_BRANCH
            cols = slice(r * tq_n, (r + 1) * tq_n)
            o_t.append(gates[c0:c0 + 1, :] * o_c[r] + (gates[c0 + 1:c0 + 2, :] / l_s[:, cols]) * acc_s[:, cols]
                       + (gates[c0 + 2:c0 + 3, :] / l_w[:, cols]) * acc_w[:, cols])
        o_ref[0, :, pair * LANES:(pair + 1) * LANES] = jnp.concatenate(o_t, axis=0).T


def _nsa(slopes, proj_nsa, kvc, proj_gla):
    bsz, _, s, dh = proj_nsa.shape
    g_n, nr = NSA_KV_HEADS, NSA_Q_PER_KV
    tq = NSA_TQ
    tk = NSA_TK
    assert tq == tk and WINDOW % tk == 0 and 2 * dh == LANES
    ncp = kvc.shape[2]
    kv0 = NSA_HEADS
    nq = nr * tq
    nbuf = WINDOW // tk + 2

    def kv_spec(i):
        return pl.BlockSpec((1, 1, s, dh), lambda b, g, q, i=i: (b, kv0 + i * g_n + g, 0, 0))

    return pl.pallas_call(
        _nsa_kernel,
        grid=(bsz, g_n, s // tq),
        in_specs=[pl.BlockSpec(memory_space=pltpu.SMEM),
                  pl.BlockSpec((1, nr, tq, dh), lambda b, g, q: (b, g, q, 0)),
                  pl.BlockSpec((1, 1, ncp, dh), lambda b, g, q: (b, g, 0, 0)),
                  pl.BlockSpec((1, 1, ncp, dh), lambda b, g, q: (b, g_n + g, 0, 0)),
                  kv_spec(2), kv_spec(3), kv_spec(4), kv_spec(5),
                  pl.BlockSpec((1, tq, LANES), lambda b, g, q: (b, q, GLA_MISC_OFF // LANES))],
        out_specs=pl.BlockSpec((1, tq, nr * dh), lambda b, g, q: (b, q, g)),
        out_shape=jax.ShapeDtypeStruct((bsz, s, NSA_HEADS * dh), F32),
        scratch_shapes=[pltpu.VMEM((s, LANES), BF16), pltpu.VMEM((s // tk, dh, tk), BF16)] * 2
        + [pltpu.VMEM((LANES, nq), BF16), pltpu.VMEM((s // SEL_BLOCK, tq), F32)]
        + [pltpu.VMEM((1, nq), F32), pltpu.VMEM((1, nq), F32), pltpu.VMEM((dh, nq), F32)] * 2
        + [pltpu.VMEM((tk, nq), F32)] * nbuf + [pltpu.VMEM((tk, nq), BF16)] * nbuf + [pltpu.VMEM((1, nq), F32)] * nbuf,
        compiler_params=_cparams(("parallel", "parallel", "arbitrary")),
        name="nsa_attention",
    )(slopes, proj_nsa, kvc, kvc, proj_nsa, proj_nsa, proj_nsa, proj_nsa, proj_gla)


def _gla_kernel(q_ref, k_ref, v_ref, og_ref, lr_ref, wg_ref, bg_ref, nw_ref, o_ref, st_scr, la_scr, b_scr):
    rows_n = q_ref.shape[1]
    c_n, sub = GLA_CHUNK, GLA_SUB
    nh, dk, dv = GLA_HEADS, GLA_DK, GLA_DV

    @pl.when(pl.program_id(1) == 0)
    def _():
        st_scr[...] = jnp.zeros(st_scr.shape, F32)

    z = _dot_hi_lo(lr_ref[0], wg_ref[...]) + bg_ref[...]
    la_scr[...] = (jnp.minimum(z, 0.0) - jnp.log1p(jnp.exp(-jnp.abs(z)))) * (LOG2E / GLA_GATE_NORM)
    tril = jnp.where(lax.broadcasted_iota(I32, (c_n, c_n), 0) >= lax.broadcasted_iota(I32, (c_n, c_n), 1), 1.0, 0.0)
    tril3 = jnp.concatenate([tril.astype(BF16)] * 3, axis=1)
    row_c = lax.broadcasted_iota(I32, (c_n, 1), 0)
    row_s = lax.broadcasted_iota(I32, (sub, 1), 0)
    lane_c = lax.broadcasted_iota(I32, (1, c_n), 1)
    nw = nw_ref[...]
    hk = [slice(h * dk, (h + 1) * dk) for h in range(nh)]
    hv = [slice(h * dv, (h + 1) * dv) for h in range(nh)]

    def chunk(c, carry):
        r0 = pl.multiple_of(c * c_n, c_n)
        rows = pl.ds(r0, c_n)
        qc = q_ref[0, rows, :] * (dk ** -0.5)
        kc = k_ref[0, rows, :]
        vc = [v_ref[0, rows, hv[h]].astype(BF16) for h in range(nh)]
        pieces, rest = [], la_scr[rows, :]
        for _ in range(3):
            pieces.append(rest.astype(BF16))
            rest = rest - pieces[-1].astype(F32)
        b = jnp.dot(tril3, jnp.concatenate(pieces, axis=0), preferred_element_type=F32)
        b_scr[...] = b
        st = [st_scr[h] for h in range(nh)]
        q_e = (qc * jnp.exp2(b)).astype(BF16)
        o = [_nt_dot(q_e[:, hk[h]], st[h].astype(BF16)) for h in range(nh)]
        strips = [[] for _ in range(nh)]
        for blk in range(c_n // sub):
            lo = blk * sub
            q_i = qc[lo:lo + sub]
            b_i = b[lo:lo + sub]
            a = [jnp.zeros((sub, c_n), F32) for _ in range(nh)]
            if blk > 0:
                b_r = b_scr[lo - 1:lo, :]
                q_d = (q_i * jnp.exp2(b_i - b_r)).astype(BF16)
                k_d = (kc * jnp.exp2(jnp.where(row_c < lo, b_r - b, -jnp.inf))).astype(BF16)
                a = [_nt_dot(q_d[:, hk[h]], k_d[:, hk[h]]) for h in range(nh)]
            for j in range(sub):
                b_j = b_scr[lo + j:lo + j + 1, :]
                k_j = k_ref[0, pl.ds(r0 + lo + j, 1), :]
                prod = q_i * k_j * jnp.exp2(jnp.where(row_s >= j, b_i - b_j, -jnp.inf))
                for h in range(nh):
                    col = jnp.sum(prod[:, hk[h]], axis=-1, keepdims=True)
                    a[h] = jnp.where(lane_c == lo + j, col, a[h]) if blk == 0 else (
                        a[h] + jnp.where(lane_c == lo + j, col, 0.0))
            for h in range(nh):
                strips[h].append(a[h])
        for h in range(nh):
            attn = jnp.concatenate(strips[h], axis=0)
            o[h] = o[h] + jnp.dot(attn.astype(BF16), vc[h], preferred_element_type=F32)
        b_last = b_scr[c_n - 1:c_n, :]
        k_dec = (kc * jnp.exp2(b_last - b)).astype(BF16)
        decay = jnp.exp2(b_last)
        for h in range(nh):
            st_scr[h] = st[h] * decay[:, hk[h]] + lax.dot_general(
                vc[h], k_dec[:, hk[h]], (((0,), (0,)), ((), ())), preferred_element_type=F32)
        for h in range(nh):
            og = og_ref[0, rows, hv[h]]
            on = o[h] * lax.rsqrt(jnp.mean(o[h] * o[h], axis=-1, keepdims=True) + EPS) * nw
            o_ref[0, rows, hv[h]] = on * (og * jax.nn.sigmoid(og))
        return carry

    lax.fori_loop(0, rows_n // c_n, chunk, 0)


def _gla(proj_gla, wg_pad, bg, nw):
    bsz, s, _ = proj_gla.shape
    nh, dk, dv = GLA_HEADS, GLA_DK, GLA_DV
    ts = GLA_TS
    wk, wv = nh * dk, nh * dv
    return pl.pallas_call(
        _gla_kernel,
        grid=(bsz, s // ts),
        in_specs=[pl.BlockSpec((1, ts, wk), lambda b, i: (b, i, GLA_Q_OFF // wk)),
                  pl.BlockSpec((1, ts, wk), lambda b, i: (b, i, GLA_K_OFF // wk)),
                  pl.BlockSpec((1, ts, wv), lambda b, i: (b, i, GLA_V_OFF // wv)),
                  pl.BlockSpec((1, ts, wv), lambda b, i: (b, i, GLA_OG_OFF // wv)),
                  pl.BlockSpec((1, ts, LANES), lambda b, i: (b, i, GLA_MISC_OFF // LANES)),
                  pl.BlockSpec((LANES, wk), lambda b, i: (0, 0)),
                  pl.BlockSpec((1, wk), lambda b, i: (0, 0)),
                  pl.BlockSpec((1, dv), lambda b, i: (0, 0))],
        out_specs=pl.BlockSpec((1, ts, wv), lambda b, i: (b, i, 0)),
        out_shape=jax.ShapeDtypeStruct((bsz, s, wv), F32),
        scratch_shapes=[pltpu.VMEM((nh, dv, dk), F32), pltpu.VMEM((ts, wk), F32), pltpu.VMEM((GLA_CHUNK, wk), F32)],
        compiler_params=_cparams(("parallel", "arbitrary")),
        name="gla",
    )(proj_gla, proj_gla, proj_gla, proj_gla, proj_gla, wg_pad, bg, nw)


def _outproj_kernel(nsa_ref, gla_ref, x_ref, wo_ref, g1_ref, sc_ref, sh_ref, nw_ref, wr_ref, br_ref,
                    x1_ref, h_ref, route_ref):
    half = nsa_ref.shape[2]
    acc = jnp.dot(nsa_ref[0].astype(BF16), wo_ref[:half, :], preferred_element_type=F32)
    acc = acc + jnp.dot(gla_ref[0].astype(BF16), wo_ref[half:, :], preferred_element_type=F32)
    x1 = x_ref[0] + g1_ref[0, 0] * acc
    x1_ref[0] = x1
    h = _modulated_norm(x1, nw_ref[...], sc_ref[0, 0], sh_ref[0, 0])
    h_ref[0] = h
    h_hi = h.astype(BF16)
    h_lo = (h - h_hi.astype(F32)).astype(BF16)
    t = jnp.dot(h_hi, wr_ref[...], preferred_element_type=F32)
    logits = (t[:, :LANES] + t[:, LANES:] + jnp.dot(h_lo, wr_ref[:, :LANES], preferred_element_type=F32)
              + br_ref[...])
    lane = lax.broadcasted_iota(I32, (1, LANES), 1)
    ninf = -jnp.inf
    is_g = (lane >= N_EXPERTS) & (lane < N_EXPERTS + N_GROUPS)
    gl = jnp.where(is_g, logits, ninf)
    ge = jnp.exp(gl - jnp.max(gl, axis=-1, keepdims=True))
    gp = ge / jnp.sum(ge, axis=-1, keepdims=True)
    gp_max = jnp.max(gp, axis=-1, keepdims=True)
    grp = jnp.min(jnp.where((gp == gp_max) & is_g, lane - N_EXPERTS, LANES), axis=-1, keepdims=True)
    in_grp = (lane // EXPERTS_PER_GROUP == grp) & (lane < N_EXPERTS)
    el = jnp.where(in_grp, logits, ninf)
    v1 = jnp.max(el, axis=-1, keepdims=True)
    i1 = jnp.min(jnp.where(el == v1, lane, LANES), axis=-1, keepdims=True)
    el2 = jnp.where(lane == i1, ninf, el)
    v2 = jnp.max(el2, axis=-1, keepdims=True)
    i2 = jnp.min(jnp.where(el2 == v2, lane, LANES), axis=-1, keepdims=True)
    e2 = jnp.exp(v2 - v1)
    den = 1.0 + e2
    w1 = gp_max * (1.0 / den)
    w2 = gp_max * (e2 / den)
    route_ref[0] = jnp.where(lane == 0, i1.astype(F32), jnp.where(lane == 1, i2.astype(F32), jnp.where(
        lane == 2, w1, jnp.where(lane == 3, w2, 0.0))))


def _outproj(o_nsa, o_gla, x, wo, mod4, nw, wr, br):
    bsz, s, d = x.shape
    tm = OUT_TM
    half = o_nsa.shape[2]

    def mod_spec(idx):
        return pl.BlockSpec((1, 1, 1, d), lambda b, i: (b, idx, 0, 0))

    row = lambda w: pl.BlockSpec((1, tm, w), lambda b, i: (b, i, 0))
    return pl.pallas_call(
        _outproj_kernel,
        grid=(bsz, s // tm),
        in_specs=[row(half), row(half), row(d),
                  pl.BlockSpec((2 * half, d), lambda b, i: (0, 0), pipeline_mode=pl.Buffered(1)),
                  mod_spec(2), mod_spec(4), mod_spec(3),
                  pl.BlockSpec((1, d), lambda b, i: (0, 0)),
                  pl.BlockSpec((d, 2 * LANES), lambda b, i: (0, 0)),
                  pl.BlockSpec((1, LANES), lambda b, i: (0, 0))],
        out_specs=[row(d), row(d), row(LANES)],
        out_shape=[jax.ShapeDtypeStruct((bsz, s, d), F32), jax.ShapeDtypeStruct((bsz, s, d), F32),
                   jax.ShapeDtypeStruct((bsz, s, LANES), F32)],
        compiler_params=_cparams(("parallel", "parallel")),
        name="outproj_router",
    )(o_nsa, o_gla, x, wo, mod4, mod4, mod4, nw, wr, br)


def _rank_kernel(route_ref, dest_ref, meta_ref, rank_scr):
    n = route_ref.shape[0]
    tm = RANK_TM
    lane_i = lax.broadcasted_iota(I32, (1, LANES), 1)
    lane = lane_i.astype(F32)
    strict = jnp.where(lax.broadcasted_iota(I32, (tm, tm), 0) > lax.broadcasted_iota(I32, (tm, tm), 1),
                       1.0, 0.0).astype(BF16)

    def two_lanes(a, b):
        return jnp.where(lane_i == 0, a, jnp.where(lane_i == 1, b, 0.0))

    def pick(e, table):
        return jnp.sum(jnp.where(lane == e, table, 0.0), axis=-1, keepdims=True)

    def count(i, seen):
        r0 = pl.multiple_of(i * tm, tm)
        rt = route_ref[pl.ds(r0, tm), :]
        e1, e2 = rt[:, 0:1], rt[:, 1:2]
        member = jnp.where(lane == e1, 1.0, jnp.where(lane == e2, 1.0, 0.0))
        before = jnp.dot(strict, member.astype(BF16), preferred_element_type=F32) + seen
        rank_scr[pl.ds(r0, tm), :] = two_lanes(pick(e1, before), pick(e2, before))
        return seen + jnp.sum(member, axis=0, keepdims=True)

    counts = lax.fori_loop(0, n // tm, count, jnp.zeros((1, LANES), F32), unroll=4)
    ntile = jnp.floor((counts + (MOE_TB - 1)) * (1.0 / MOE_TB))
    incl = jnp.where(lax.broadcasted_iota(I32, (LANES, LANES), 0) <= lax.broadcasted_iota(I32, (LANES, LANES), 1),
                     1.0, 0.0).astype(BF16)
    tile_end = jnp.dot(jnp.broadcast_to(ntile, (8, LANES)).astype(BF16), incl,
                       preferred_element_type=F32)[0:1]
    row_start = (tile_end - ntile) * MOE_TB

    def place(i, carry):
        r0 = pl.multiple_of(i * tm, tm)
        rt = route_ref[pl.ds(r0, tm), :]
        rk = rank_scr[pl.ds(r0, tm), :]
        d1 = pick(rt[:, 0:1], row_start) + rk[:, 0:1]
        d2 = pick(rt[:, 1:2], row_start) + rk[:, 1:2]
        dest_ref[pl.ds(r0, tm), :] = two_lanes(d1, d2).astype(I32)
        return carry

    lax.fori_loop(0, n // tm, place, 0, unroll=4)
    trow = lax.broadcasted_iota(I32, (meta_ref.shape[0], 1), 0).astype(F32)
    texp = jnp.sum(jnp.where((tile_end <= trow) & (lane_i < N_EXPERTS), 1.0, 0.0), axis=-1, keepdims=True)
    texp = jnp.minimum(texp, N_EXPERTS - 1.0)
    used = pick(N_EXPERTS - 1.0, tile_end)
    diag = lax.broadcasted_iota(I32, (meta_ref.shape[0], LANES), 0) == lane_i
    end_rows = jnp.sum(jnp.where(diag, tile_end, 0.0), axis=-1, keepdims=True)
    ntile_rows = jnp.sum(jnp.where(diag, ntile, 0.0), axis=-1, keepdims=True)
    meta_ref[...] = jnp.where(lane_i == 2, end_rows, jnp.where(lane_i == 3, ntile_rows, two_lanes(
        texp, jnp.broadcast_to(used, texp.shape)))).astype(I32)


def _rank(route):
    n = route.shape[0]
    return pl.pallas_call(
        _rank_kernel,
        out_shape=[jax.ShapeDtypeStruct((n, LANES), I32), jax.ShapeDtypeStruct((LANES, LANES), I32)],
        scratch_shapes=[pltpu.VMEM((n, LANES), F32)],
        compiler_params=pltpu.CompilerParams(vmem_limit_bytes=VMEM_LIMIT),
        name="moe_rank",
    )(route)


def _dispatch_kernel(dest_ref, ends_ref, h_ref, xs_ref, zero_scr, sem, zsem):
    i = pl.program_id(0)
    tm = h_ref.shape[0]
    tb = zero_scr.shape[0]

    @pl.when(i == 0)
    def _():
        zero_scr[...] = jnp.zeros(zero_scr.shape, F32)

        def zero_copy(e):
            r0 = pl.multiple_of((ends_ref[e] - 1) * tb, tb)
            return pltpu.make_async_copy(zero_scr, xs_ref.at[pl.ds(r0, tb)], zsem)

        def start(e, carry):
            @pl.when(ends_ref[N_EXPERTS + e] > 0)
            def _():
                zero_copy(e).start()
            return carry

        def wait(e, carry):
            @pl.when(ends_ref[N_EXPERTS + e] > 0)
            def _():
                zero_copy(e).wait()
            return carry

        def tail_copy(t):
            return pltpu.make_async_copy(zero_scr, xs_ref.at[pl.ds(pl.multiple_of(t * tb, tb), tb)], zsem)

        def tail_start(t, carry):
            tail_copy(t).start()
            return carry

        def tail_wait(t, carry):
            tail_copy(t).wait()
            return carry

        used = ends_ref[2 * N_EXPERTS]
        lax.fori_loop(0, N_EXPERTS, start, 0)
        lax.fori_loop(used, xs_ref.shape[0] // tb, tail_start, 0)
        lax.fori_loop(0, N_EXPERTS, wait, 0)
        lax.fori_loop(used, xs_ref.shape[0] // tb, tail_wait, 0)

    def issue(r, carry):
        p = (i * tm + r) * TOP_K
        for k in range(TOP_K):
            pltpu.make_async_copy(h_ref.at[pl.ds(r, 1)], xs_ref.at[pl.ds(dest_ref[p + k], 1)], sem).start()
        return carry

    lax.fori_loop(0, tm, issue, 0, unroll=DMA_UNROLL)
    for k in range(TOP_K):
        pltpu.make_async_copy(h_ref, xs_ref.at[pl.ds(0, tm)], sem).wait()


def _dispatch(dest_flat, ends_flat, h, cap):
    n, d = h.shape
    tm = DISPATCH_TM
    return pl.pallas_call(
        _dispatch_kernel,
        grid_spec=pltpu.PrefetchScalarGridSpec(
            num_scalar_prefetch=2, grid=(n // tm,),
            in_specs=[pl.BlockSpec((tm, d), lambda i, dst, ends: (i, 0))],
            out_specs=pl.BlockSpec(memory_space=pl.ANY),
            scratch_shapes=[pltpu.VMEM((MOE_TB, d), F32), pltpu.SemaphoreType.DMA(()), pltpu.SemaphoreType.DMA(())]),
        out_shape=jax.ShapeDtypeStruct((cap, d), F32),
        compiler_params=_cparams(("arbitrary",)),
        name="moe_dispatch",
    )(dest_flat, ends_flat, h)


def _ffn_kernel(meta_ref, x_ref, wg_hbm, wu_hbm, wd_hbm, y_ref, wg_st, wu_st, wd_st, wgb, wub, wdb, wsem):
    t = pl.program_id(0)
    ntile = pl.num_programs(0)
    n_used = meta_ref[ntile]
    e = meta_ref[t]
    e_prev = meta_ref[jnp.maximum(t - 1, 0)]
    active = t < n_used

    def fetch(expert):
        return [pltpu.make_async_copy(src.at[expert], dst, wsem)
                for src, dst in ((wg_hbm, wg_st), (wu_hbm, wu_st), (wd_hbm, wd_st))]

    @pl.when(t == 0)
    def _():
        for cp in fetch(e):
            cp.start()

    @pl.when(active & ((t == 0) | (e != e_prev)))
    def _():
        for cp in fetch(e):
            cp.wait()
        wgb[...] = wg_st[...].astype(BF16)
        wub[...] = wu_st[...].astype(BF16)
        wdb[...] = wd_st[...].astype(BF16)
        end = meta_ref[ntile + 1 + e]

        @pl.when(end < n_used)
        def _():
            for cp in fetch(meta_ref[end]):
                cp.start()

    @pl.when(active)
    def _():
        x = x_ref[...].astype(BF16)
        gate = jnp.dot(x, wgb[...], preferred_element_type=F32)
        up = jnp.dot(x, wub[...], preferred_element_type=F32)
        act = (gate * jax.nn.sigmoid(gate)) * up
        y_ref[...] = jnp.dot(act.astype(BF16), wdb[...], preferred_element_type=F32)

    @pl.when(jnp.logical_not(active))
    def _():
        y_ref[...] = jnp.zeros(y_ref.shape, F32)


def _ffn(meta_flat, xs, wg, wu, wd):
    cap, d = xs.shape
    ff = wg.shape[2]
    tb = MOE_TB
    ntile = cap // tb
    return pl.pallas_call(
        _ffn_kernel,
        grid_spec=pltpu.PrefetchScalarGridSpec(
            num_scalar_prefetch=1, grid=(ntile,),
            in_specs=[pl.BlockSpec((tb, d), lambda t, m: (jnp.minimum(t, m[ntile] - 1), 0)),
                      pl.BlockSpec(memory_space=pl.ANY), pl.BlockSpec(memory_space=pl.ANY),
                      pl.BlockSpec(memory_space=pl.ANY)],
            out_specs=pl.BlockSpec((tb, d), lambda t, m: (t, 0)),
            scratch_shapes=[pltpu.VMEM((d, ff), F32), pltpu.VMEM((d, ff), F32), pltpu.VMEM((ff, d), F32),
                            pltpu.VMEM((d, ff), BF16), pltpu.VMEM((d, ff), BF16), pltpu.VMEM((ff, d), BF16),
                            pltpu.SemaphoreType.DMA(())]),
        out_shape=jax.ShapeDtypeStruct((cap, d), F32),
        compiler_params=_cparams(("arbitrary",)),
        name="moe_ffn",
    )(meta_flat, xs, wg, wu, wd)


def _combine_kernel(dest_ref, y_ref, x1_ref, route_ref, g2_ref, nf_ref, o_ref, ybuf, sem):
    i = pl.program_id(0)
    nstep = pl.num_programs(0)
    tm = x1_ref.shape[0]

    def issue(tile, slot):
        def body(r, carry):
            p = (tile * tm + r) * TOP_K
            for k in range(TOP_K):
                pltpu.make_async_copy(y_ref.at[pl.ds(dest_ref[p + k], 1)], ybuf.at[slot, k, pl.ds(r, 1)],
                                      sem.at[slot]).start()
            return carry

        lax.fori_loop(0, tm, body, 0, unroll=DMA_UNROLL)

    @pl.when(i == 0)
    def _():
        issue(0, 0)

    @pl.when(i + 1 < nstep)
    def _():
        issue(i + 1, (i + 1) % 2)

    slot = i % 2
    for k in range(TOP_K):
        pltpu.make_async_copy(y_ref.at[pl.ds(0, tm)], ybuf.at[slot, k], sem.at[slot]).wait()
    rt = route_ref[...]
    moe = rt[:, 2:3] * ybuf[slot, 0] + rt[:, 3:4] * ybuf[slot, 1]
    xo = x1_ref[...] + g2_ref[0, 0] * moe
    o_ref[...] = xo * lax.rsqrt(jnp.mean(xo * xo, axis=-1, keepdims=True) + EPS) * nf_ref[...]


def _combine(dest_flat, y, x1, route, mod4, nf, seq):
    n, d = x1.shape
    tm = COMB_TM
    tiles_per_seq = seq // tm
    return pl.pallas_call(
        _combine_kernel,
        grid_spec=pltpu.PrefetchScalarGridSpec(
            num_scalar_prefetch=1, grid=(n // tm,),
            in_specs=[pl.BlockSpec(memory_space=pl.ANY),
                      pl.BlockSpec((tm, d), lambda i, dst: (i, 0)),
                      pl.BlockSpec((tm, LANES), lambda i, dst: (i, 0)),
                      pl.BlockSpec((1, 1, 1, d), lambda i, dst: (i // tiles_per_seq, 5, 0, 0)),
                      pl.BlockSpec((1, d), lambda i, dst: (0, 0))],
            out_specs=pl.BlockSpec((tm, d), lambda i, dst: (i, 0)),
            scratch_shapes=[pltpu.VMEM((2, TOP_K, tm, d), F32), pltpu.SemaphoreType.DMA((2,))]),
        out_shape=jax.ShapeDtypeStruct((n, d), F32),
        compiler_params=_cparams(("arbitrary",)),
        name="moe_combine",
    )(dest_flat, y, x1, route, mod4, nf)


def _alibi_slopes():
    n = NSA_HEADS
    full = jnp.asarray(2.0 ** (-8.0 * np.arange(1, n + 1) / n), dtype=F32)
    pieces, rest = [], full * LOG2E
    for _ in range(3):
        piece = rest.astype(BF16).astype(F32)
        pieces.append(piece)
        rest = rest - piece
    return jnp.concatenate([full] + pieces)


def _layer(x, c, w_ada, b_ada, norm1_w, w_in, cmp_pos_k, cmp_w1_k, cmp_w2_k, cmp_pos_v, cmp_w1_v, cmp_w2_v,
           gla_w_gate2, gla_b_gate, gla_norm_w, w_out, norm2_w, w_rg, b_rg, w_re, b_re, w_eg, w_eu, w_ed):
    bsz, s, d = x.shape
    dh = NSA_HEAD_DIM
    mod4 = _adaln(c, w_ada, b_ada).reshape(bsz, 6, 1, d)

    o_gate = NSA_COLS
    o_gla = o_gate + NSA_GATE_COLS
    o_lr = o_gla + 2 * GLA_HEADS * GLA_DK + 2 * GLA_HEADS * GLA_DV
    w_nsa, w_gla = _prep_w_in(w_in.T, o_gate, o_gla, o_lr)
    nw1 = norm1_w.reshape(1, d)
    proj_nsa, proj_gla = _inproj(x, mod4, nw1, w_nsa, w_gla, sc_idx=1, sh_idx=0)

    pos = jnp.stack([cmp_pos_k, cmp_pos_v]).reshape(2, 2, CMP_STRIDE * dh)
    kvc = _compress(proj_nsa, pos, jnp.stack([cmp_w1_k, cmp_w1_v]), jnp.stack([cmp_w2_k, cmp_w2_v]))
    o_nsa = _nsa(_alibi_slopes(), proj_nsa, kvc, proj_gla)

    wg_pad = jnp.zeros((LANES, GLA_HEADS * GLA_DK), F32).at[
        NSA_GATE_COLS:NSA_GATE_COLS + GLA_GATE_RANK].set(gla_w_gate2)
    o_gla_out = _gla(proj_gla, wg_pad, gla_b_gate.reshape(1, -1), gla_norm_w.reshape(1, -1))

    wr = jnp.concatenate([w_re, w_rg, jnp.zeros((d, LANES - N_EXPERTS - N_GROUPS), F32)], axis=1)
    br = jnp.concatenate([b_re, b_rg, jnp.zeros((LANES - N_EXPERTS - N_GROUPS,), F32)]).reshape(1, LANES)
    wr_hi = wr.astype(BF16)
    wr_cat = jnp.concatenate([wr_hi, (wr - wr_hi.astype(F32)).astype(BF16)], axis=1)
    x1, h2, route = _outproj(o_nsa, o_gla_out, x, w_out.astype(BF16), mod4, norm2_w.reshape(1, d), wr_cat, br)

    n = bsz * s
    npair = n * TOP_K
    cap = npair + N_EXPERTS * MOE_TB
    ntile = cap // MOE_TB
    route2 = route.reshape(n, LANES)
    dest, meta = _rank(route2)
    dest_flat = dest[:, :TOP_K].reshape(npair)
    meta_flat = jnp.concatenate([meta[:ntile, 0], meta[:1, 1], meta[:N_EXPERTS, 2]])
    ends_flat = jnp.concatenate([meta[:N_EXPERTS, 2], meta[:N_EXPERTS, 3], meta[:1, 1]])
    xs = _dispatch(dest_flat, ends_flat, h2.reshape(n, d), cap)
    y = _ffn(meta_flat, xs, w_eg, w_eu, w_ed)
    return x1.reshape(n, d), y, dest_flat, route2, mod4


def kernel(x, c, w_ada, b_ada, norm1_w, w_in, cmp_pos_k, cmp_w1_k, cmp_w2_k, cmp_pos_v, cmp_w1_v, cmp_w2_v,
           gla_w_gate2, gla_b_gate, gla_norm_w, w_out, norm2_w, w_router_group, b_router_group, w_router_expert,
           b_router_expert, w_expert_gate, w_expert_up, w_expert_down, norm_f_w):
    bsz, s, d = x.shape
    assert w_ada.shape[0] == 1, "single layer"
    x1, y, dest_flat, route2, mod4 = _layer(
        x, c, w_ada[0], b_ada[0], norm1_w[0], w_in[0], cmp_pos_k[0], cmp_w1_k[0], cmp_w2_k[0], cmp_pos_v[0],
        cmp_w1_v[0], cmp_w2_v[0], gla_w_gate2[0], gla_b_gate[0], gla_norm_w[0], w_out[0], norm2_w[0],
        w_router_group[0], b_router_group[0], w_router_expert[0], b_router_expert[0],
        w_expert_gate[0], w_expert_up[0], w_expert_down[0])
    out = _combine(dest_flat, y, x1, route2, mod4, norm_f_w.reshape(1, d), s)
    return out.reshape(bsz, s, d)
```

```python
import functools

import numpy as np
import jax
import jax.numpy as jnp
from jax import lax
from jax.experimental import pallas as pl
from jax.experimental.pallas import tpu as pltpu

F32 = jnp.float32
BF16 = jnp.bfloat16
I32 = jnp.int32

D_MODEL = 2048
NSA_HEAD_DIM = 64
NSA_HEADS = 16
NSA_KV_HEADS = 4
NSA_Q_PER_KV = 4
CMP_BLOCK = 32
CMP_STRIDE = 16
SEL_BLOCK = 64
N_SEL = 16
WINDOW = 512
N_BRANCH = 3
GLA_HEADS = 4
GLA_DV = 256
GLA_DK = 128
GLA_GATE_RANK = 16
GLA_GATE_NORM = 16.0
GLA_CHUNK = 64
GLA_SUB = 8
N_GROUPS = 4
EXPERTS_PER_GROUP = 8
N_EXPERTS = 32
TOP_K = 2
EXPERT_FF = 512
EPS = 1e-6
NEG = -1e30
FORCE = 1e30
LOG2E = 1.4426950408889634
MASKED = 2.0 ** 100

NSA_Q_COLS = NSA_HEADS * NSA_HEAD_DIM
NSA_KV_COLS = 2 * N_BRANCH * NSA_KV_HEADS * NSA_HEAD_DIM
NSA_GATE_COLS = N_BRANCH * NSA_HEADS
NSA_COLS = NSA_Q_COLS + NSA_KV_COLS
NSA_SLOTS = NSA_COLS // NSA_HEAD_DIM
GLA_Q_OFF = 0
GLA_K_OFF = GLA_HEADS * GLA_DK
GLA_V_OFF = 2 * GLA_HEADS * GLA_DK
GLA_OG_OFF = GLA_V_OFF + GLA_HEADS * GLA_DV
GLA_MISC_OFF = GLA_OG_OFF + GLA_HEADS * GLA_DV
LANES = 128
GLA_COLS = GLA_MISC_OFF + LANES

VMEM_LIMIT = 56 * 1024 * 1024

ADA_TN = 768
PREP_TR = 256
PREP_STEP = 512
INPROJ_TM = 256
INPROJ_TN = 512
NSA_TQ = 256
NSA_TK = 256
NSA_AHEAD = 2
GLA_TS = 512
OUT_TM = 512
RANK_TM = 256
MOE_TB = 256
DISPATCH_TM = 1024
COMB_TM = 512
DMA_UNROLL = 8


def _cparams(sem):
    return pltpu.CompilerParams(dimension_semantics=sem, vmem_limit_bytes=VMEM_LIMIT)


def _adaln_kernel(c_ref, w_ref, b_ref, o_ref):
    c = c_ref[...]
    s = (c * jax.nn.sigmoid(c)).astype(BF16)
    o_ref[...] = jnp.dot(s, w_ref[...].astype(BF16), preferred_element_type=F32) + b_ref[...]


def _adaln(c, w, b):
    nb, d = c.shape
    n = w.shape[1]
    return pl.pallas_call(
        _adaln_kernel,
        grid=(n // ADA_TN,),
        in_specs=[pl.BlockSpec((nb, d), lambda j: (0, 0)),
                  pl.BlockSpec((d, ADA_TN), lambda j: (0, j)),
                  pl.BlockSpec((1, ADA_TN), lambda j: (0, j))],
        out_specs=pl.BlockSpec((nb, ADA_TN), lambda j: (0, j)),
        out_shape=jax.ShapeDtypeStruct((nb, n), F32),
        compiler_params=_cparams(("parallel",)),
        name="adaln",
    )(c, w, b.reshape(1, n))


def _modulated_norm(x, nw, sc, sh):
    ms = jnp.mean(x * x, axis=-1, keepdims=True)
    h = x * lax.rsqrt(ms + EPS) * nw
    return h * (1.0 + sc) + sh


def _prep_w_in_kernel(wt_ref, wn_ref, wg_ref, *, o_gate, o_gla, o_lr):
    tr = wt_ref.shape[1]
    step = PREP_STEP

    def put(dst, c0, rows):
        dst[:, c0:c0 + rows.shape[0]] = rows.T.astype(BF16)

    for c in range(0, o_gate, step):
        put(wn_ref, c, wt_ref[c:c + step, :])
    for c in range(0, o_lr - o_gla, step):
        put(wg_ref, c, wt_ref[o_gla + c:o_gla + c + step, :])
    pad = wg_ref.shape[1] - (wt_ref.shape[0] - o_gate)
    misc = jnp.concatenate([wt_ref[o_gate:o_gla, :], wt_ref[o_lr:, :], jnp.zeros((pad, tr), F32)], axis=0)
    put(wg_ref, o_lr - o_gla, misc)


def _prep_w_in(w_in_t, o_gate, o_gla, o_lr):
    n, d = w_in_t.shape
    tr = PREP_TR
    assert o_gate % PREP_STEP == 0 and (o_lr - o_gla) % PREP_STEP == 0
    return pl.pallas_call(
        functools.partial(_prep_w_in_kernel, o_gate=o_gate, o_gla=o_gla, o_lr=o_lr),
        grid=(d // tr,),
        in_specs=[pl.BlockSpec((n, tr), lambda i: (0, i))],
        out_specs=[pl.BlockSpec((tr, NSA_COLS), lambda i: (i, 0)), pl.BlockSpec((tr, GLA_COLS), lambda i: (i, 0))],
        out_shape=[jax.ShapeDtypeStruct((d, NSA_COLS), BF16), jax.ShapeDtypeStruct((d, GLA_COLS), BF16)],
        compiler_params=_cparams(("parallel",)),
        name="prep_w_in",
    )(w_in_t)


def _inproj_kernel(x_ref, sc_ref, sh_ref, nw_ref, wn_ref, wg_ref, on_ref, og_ref):
    h = _modulated_norm(x_ref[0], nw_ref[...], sc_ref[0, 0], sh_ref[0, 0]).astype(BF16)
    dh = NSA_HEAD_DIM
    tn = INPROJ_TN
    for c in range(wn_ref.shape[1] // tn):
        acc = jnp.dot(h, wn_ref[:, c * tn:(c + 1) * tn], preferred_element_type=F32)
        for u in range(tn // dh):
            on_ref[0, c * (tn // dh) + u] = acc[:, u * dh:(u + 1) * dh]
    og_ref[0] = jnp.dot(h, wg_ref[...], preferred_element_type=F32)


def _inproj(x, mod4, nw, w_nsa, w_gla, *, sc_idx, sh_idx):
    bsz, s, d = x.shape
    n_nsa, n_gla = w_nsa.shape[1], w_gla.shape[1]
    tm = INPROJ_TM
    dh = NSA_HEAD_DIM
    return pl.pallas_call(
        _inproj_kernel,
        grid=(bsz, s // tm),
        in_specs=[pl.BlockSpec((1, tm, d), lambda b, i: (b, i, 0)),
                  pl.BlockSpec((1, 1, 1, d), lambda b, i: (b, sc_idx, 0, 0)),
                  pl.BlockSpec((1, 1, 1, d), lambda b, i: (b, sh_idx, 0, 0)),
                  pl.BlockSpec((1, d), lambda b, i: (0, 0)),
                  pl.BlockSpec((d, n_nsa), lambda b, i: (0, 0), pipeline_mode=pl.Buffered(1)),
                  pl.BlockSpec((d, n_gla), lambda b, i: (0, 0), pipeline_mode=pl.Buffered(1))],
        out_specs=[pl.BlockSpec((1, n_nsa // dh, tm, dh), lambda b, i: (b, 0, i, 0)),
                   pl.BlockSpec((1, tm, n_gla), lambda b, i: (b, i, 0))],
        out_shape=[jax.ShapeDtypeStruct((bsz, n_nsa // dh, s, dh), F32),
                   jax.ShapeDtypeStruct((bsz, s, n_gla), F32)],
        compiler_params=_cparams(("parallel", "parallel")),
        name="inproj",
    )(x, mod4, mod4, nw, w_nsa, w_gla)


def _hi_lo(x):
    hi = x.astype(BF16)
    return hi, (x - hi.astype(F32)).astype(BF16)


def _dot_hi_lo(x, w):
    x_hi, x_lo = _hi_lo(x)
    w_hi, w_lo = _hi_lo(w)
    return jnp.dot(jnp.concatenate([x_hi, x_lo, x_hi], axis=1), jnp.concatenate([w_hi, w_hi, w_lo], axis=0),
                   preferred_element_type=F32)


def _compress_kernel(a_ref, pos_ref, w1_ref, w2_ref, o_ref):
    ng = a_ref.shape[1]
    nch = a_ref.shape[2] // CMP_STRIDE
    a = jnp.concatenate([
        jnp.concatenate([a_ref[0, g, pl.ds(t, nch, stride=CMP_STRIDE), :] for t in range(CMP_STRIDE)], axis=1)
        for g in range(ng)], axis=0)
    pos = pos_ref[0]
    half = a.shape[1]
    y1 = _dot_hi_lo(a + pos[0:1], w1_ref[0, :half, :])
    y2 = _dot_hi_lo(a + pos[1:2], w1_ref[0, half:, :])
    h = y1 + pltpu.roll(y2, a.shape[0] - 1, axis=0)
    out = _dot_hi_lo(jax.nn.gelu(h), w2_ref[0])
    row = lax.broadcasted_iota(I32, (nch, out.shape[1]), 0)
    for g in range(ng):
        o_ref[0, g] = jnp.where(row < nch - 1, out[g * nch:(g + 1) * nch], 0.0)


def _compress(proj_nsa, pos, w1, w2):
    bsz, _, s, dh = proj_nsa.shape
    ng = NSA_KV_HEADS
    nch = s // CMP_STRIDE
    cw = CMP_STRIDE * dh
    return pl.pallas_call(
        _compress_kernel,
        grid=(bsz, 2),
        in_specs=[pl.BlockSpec((1, ng, s, dh), lambda b, t: (b, NSA_HEADS // ng + t, 0, 0)),
                  pl.BlockSpec((1, 2, cw), lambda b, t: (t, 0, 0)),
                  pl.BlockSpec((1, 2 * cw, dh), lambda b, t: (t, 0, 0)),
                  pl.BlockSpec((1, dh, dh), lambda b, t: (t, 0, 0))],
        out_specs=pl.BlockSpec((1, ng, nch, dh), lambda b, t: (b, t, 0, 0)),
        out_shape=jax.ShapeDtypeStruct((bsz, 2 * ng, nch, dh), F32),
        compiler_params=_cparams(("parallel", "parallel")),
        name="nsa_compress",
    )(proj_nsa, pos, w1, w2)


def _nt_dot(a, b, **kw):
    return lax.dot_general(a, b, (((1,), (1,)), ((), ())), preferred_element_type=F32, **kw)


def _nsa_kernel(slopes_ref, q_ref, kc_ref, vc_ref, ks_ref, vs_ref, kw_ref, vw_ref, gate_ref, o_ref,
                ksb, vst, kwb, vwt, q4_scr, notsel_scr, *scr):
    g = pl.program_id(1)
    qi = pl.program_id(2)
    tq_n = q_ref.shape[2]
    dh = NSA_HEAD_DIM
    nr = NSA_Q_PER_KV
    seq = ks_ref.shape[2]
    tk_n = NSA_TK
    nb = seq // SEL_BLOCK

    @pl.when(qi == 0)
    def _():
        row = lax.broadcasted_iota(I32, (seq, dh), 0)
        lane = lax.broadcasted_iota(I32, (seq, dh), 1)
        blk = row // SEL_BLOCK
        pos = jnp.where((lane >= nb) & (lane < nb + 3), (blk * SEL_BLOCK).astype(F32),
                        jnp.where((lane >= nb + 3) & (lane < nb + 6), (row % SEL_BLOCK).astype(F32), 0.0))
        ksb[...] = jnp.concatenate([jnp.where(lane == blk, -MASKED, pos), ks_ref[0, 0]], axis=1).astype(BF16)
        kwb[...] = jnp.concatenate([pos, kw_ref[0, 0]], axis=1).astype(BF16)
        for c in range(seq // tk_n):
            rows = slice(c * tk_n, (c + 1) * tk_n)
            for src, dst in ((vs_ref, vst), (vw_ref, vwt)):
                v = src[0, 0, rows, :]
                dst[c] = jnp.concatenate([v, v], axis=1).T[:dh].astype(BF16)

    t0 = qi * tq_n
    tq = t0 + lax.broadcasted_iota(I32, (1, tq_n), 1)
    slopes = [slopes_ref[g * nr + r] for r in range(nr)]
    scale = dh ** -0.5
    q_t = []
    for pair in range(nr // 2):
        both = jnp.concatenate([q_ref[0, 2 * pair], q_ref[0, 2 * pair + 1]], axis=1).T * scale
        q_t += [both[:dh], both[dh:]]

    ncp = kc_ref.shape[2]
    kc = kc_ref[0, 0]
    vc = vc_ref[0, 0]
    vc_t = jnp.concatenate([vc, vc], axis=1).T[:dh].astype(BF16)
    n_sub = lax.broadcasted_iota(I32, (ncp, 1), 0)
    blk_end = n_sub * CMP_STRIDE + (CMP_BLOCK - 1)
    center = n_sub.astype(F32) * CMP_STRIDE + (CMP_BLOCK - 1) / 2.0
    kc_hi, kc_lo = _hi_lo(kc)
    kc_cat = jnp.concatenate([kc_hi, kc_lo, kc_hi, jnp.zeros_like(kc_hi)], axis=1)
    q_hi, q_lo = _hi_lo(jnp.concatenate(q_t, axis=1))
    q_cat = jnp.concatenate([q_hi, q_hi, q_lo, jnp.zeros_like(q_hi)], axis=0)
    s = jnp.dot(kc_cat, q_cat, preferred_element_type=F32)
    tq_all = jnp.concatenate([tq] * nr, axis=1)
    slope_all = jnp.concatenate([jnp.full((1, tq_n), slopes[r], F32) for r in range(nr)], axis=1)
    valid_all = blk_end <= tq_all
    s = jnp.where(valid_all, s - slope_all * (tq_all.astype(F32) - center), NEG)
    e = jnp.exp(s - jnp.max(s, axis=0, keepdims=True))
    p = jnp.where(valid_all, e / jnp.sum(e, axis=0, keepdims=True), 0.0)
    psum = sum(p[:, r * tq_n:(r + 1) * tq_n] for r in range(nr))
    o_c_all = jnp.dot(vc_t, p.astype(BF16), preferred_element_type=F32)
    o_c = [o_c_all[:, r * tq_n:(r + 1) * tq_n] for r in range(nr)]

    n_sel = min(N_SEL, nb)
    notsel_scr[...] = jnp.zeros(notsel_scr.shape, F32)

    @pl.when((qi + 1) * tq_n > n_sel * SEL_BLOCK)
    def _():
        rowj = lax.broadcasted_iota(I32, (LANES, ncp), 0) * SEL_BLOCK
        coln = lax.broadcasted_iota(I32, (LANES, ncp), 1) * CMP_STRIDE
        overlap = jnp.where((coln < rowj + SEL_BLOCK) & (coln + CMP_BLOCK > rowj)
                            & (coln < (ncp - 1) * CMP_STRIDE) & (rowj < nb * SEL_BLOCK), 1.0, 0.0)
        ov = overlap.astype(BF16)
        imp = jnp.dot(jnp.concatenate([ov, ov], axis=1), jnp.concatenate(_hi_lo(psum), axis=0),
                      preferred_element_type=F32)[:nb]
        j_sub = lax.broadcasted_iota(I32, (nb, 1), 0)
        qblk = tq // SEL_BLOCK
        forced = (j_sub == 0) | (j_sub == qblk) | (j_sub == qblk - 1)
        imp = jnp.where(forced, FORCE, jnp.where(j_sub <= qblk, imp, NEG))
        cnt = jnp.zeros((nb, tq_n), F32)
        for i in range(nb):
            ci = imp[i:i + 1, :]
            tie = jnp.where(j_sub > i, 1.0, 0.0)
            cnt = cnt + jnp.where(ci > imp, 1.0, jnp.where(ci == imp, tie, 0.0))
        notsel_scr[...] = jnp.where(cnt < float(n_sel), 0.0, 1.0)

    notsel = notsel_scr[...]

    sub_h = lax.broadcasted_iota(I32, (dh - nb, 1), 0)
    for r in range(nr):
        scol = jnp.zeros((dh - nb, 1), F32)
        for i in range(3):
            piece = slopes_ref[(i + 1) * NSA_HEADS + g * nr + r]
            scol = jnp.where((sub_h == i) | (sub_h == 3 + i), piece, scol)
        q4_scr[:, r * tq_n:(r + 1) * tq_n] = jnp.concatenate(
            [notsel, jnp.broadcast_to(scol, (dh - nb, tq_n)), q_t[r] * LOG2E], axis=0).astype(BF16)

    nwt = WINDOW // tk_n
    nbuf = nwt + 2
    stats_s, stats_w = scr[0:3], scr[3:6]
    s_buf, p_buf, a_buf = scr[6:6 + nbuf], scr[6 + nbuf:6 + 2 * nbuf], scr[6 + 2 * nbuf:6 + 3 * nbuf]
    for m_ref, l_ref, acc_ref in (stats_s, stats_w):
        m_ref[...] = jnp.full(m_ref.shape, NEG, F32)
        l_ref[...] = jnp.zeros(l_ref.shape, F32)
        acc_ref[...] = jnp.zeros(acc_ref.shape, F32)
    key_i = lax.broadcasted_iota(I32, (tk_n, LANES), 0)
    qry_j = lax.broadcasted_iota(I32, (tk_n, LANES), 1)

    def scores(k_ref, kt, buf):
        k_tile = k_ref[kt * tk_n:(kt + 1) * tk_n, :]
        s_buf[buf][...] = jnp.dot(k_tile, q4_scr[...], preferred_element_type=F32)

    def softmax(buf, mode, stats):
        m_ref, l_ref, _ = stats
        for cb in range(nr * tq_n // LANES):
            cols = slice(cb * LANES, (cb + 1) * LANES)
            s = s_buf[buf][:, cols]
            if mode is not None:
                j = qry_j + (cb * LANES) % tq_n
                s = jnp.where(key_i <= j if mode == "causal" else key_i > j, s, -MASKED)
            m_prev = m_ref[:, cols]
            m_new = jnp.maximum(m_prev, jnp.max(s, axis=0, keepdims=True))
            alpha = jnp.exp2(m_prev - m_new)
            p = jnp.exp2(s - m_new)
            l_ref[:, cols] = alpha * l_ref[:, cols] + jnp.sum(p, axis=0, keepdims=True)
            m_ref[:, cols] = m_new
            a_buf[buf][:, cols] = alpha
            p_buf[buf][:, cols] = p.astype(BF16)

    def values(vt_ref, kt, buf, stats):
        acc_ref = stats[2]
        pv = jnp.dot(vt_ref[kt], p_buf[buf][...], preferred_element_type=F32)
        acc_ref[...] = acc_ref[...] * a_buf[buf][...] + pv

    def batch(jobs):
        ahead = NSA_AHEAD
        for i in range(min(ahead, len(jobs))):
            scores(jobs[i][0], jobs[i][2], i % nbuf)
        for i, (_, vt_ref, kt, mode, stats) in enumerate(jobs):
            softmax(i % nbuf, mode, stats)
            values(vt_ref, kt, i % nbuf, stats)
            if i + ahead < len(jobs):
                scores(jobs[i + ahead][0], jobs[i + ahead][2], (i + ahead) % nbuf)

    for k in range(seq // tq_n):
        @pl.when(qi == k)
        def _(k=k):
            jobs = [(ksb, vst, t, None, stats_s) for t in range(k)]
            jobs += [(kwb, vwt, k - back, "band" if back == nwt else None, stats_w)
                     for back in range(min(nwt, k), 0, -1)]
            jobs += [(kwb, vwt, k, "causal", stats_w), (ksb, vst, k, "causal", stats_s)]
            batch(jobs)

    gsel = jnp.where(lax.broadcasted_iota(I32, (LANES, LANES), 0)
                     == lax.broadcasted_iota(I32, (LANES, LANES), 1) + g * (nr * N_BRANCH), 1.0, 0.0)
    gs = gsel.astype(BF16)
    gates = jax.nn.sigmoid(jnp.dot(jnp.concatenate(_hi_lo(gate_ref[0]), axis=1), jnp.concatenate([gs, gs], axis=0),
                                   preferred_element_type=F32)).T
    (_, l_s, acc_s), (_, l_w, acc_w) = stats_s, stats_w
    for pair in range(nr // 2):
        o_t = []
        for r in (2 * pair, 2 * pair + 1):
            c0 = r * N_BRANCH
            cols = slice(r * tq_n, (r + 1) * tq_n)
            o_t.append(gates[c0:c0 + 1, :] * o_c[r] + (gates[c0 + 1:c0 + 2, :] / l_s[:, cols]) * acc_s[:, cols]
                       + (gates[c0 + 2:c0 + 3, :] / l_w[:, cols]) * acc_w[:, cols])
        o_ref[0, :, pair * LANES:(pair + 1) * LANES] = jnp.concatenate(o_t, axis=0).T


def _nsa(slopes, proj_nsa, kvc, proj_gla):
    bsz, _, s, dh = proj_nsa.shape
    g_n, nr = NSA_KV_HEADS, NSA_Q_PER_KV
    tq = NSA_TQ
    tk = NSA_TK
    assert tq == tk and WINDOW % tk == 0 and 2 * dh == LANES
    ncp = kvc.shape[2]
    kv0 = NSA_HEADS
    nq = nr * tq
    nbuf = WINDOW // tk + 2

    def kv_spec(i):
        return pl.BlockSpec((1, 1, s, dh), lambda b, g, q, i=i: (b, kv0 + i * g_n + g, 0, 0))

    return pl.pallas_call(
        _nsa_kernel,
        grid=(bsz, g_n, s // tq),
        in_specs=[pl.BlockSpec(memory_space=pltpu.SMEM),
                  pl.BlockSpec((1, nr, tq, dh), lambda b, g, q: (b, g, q, 0)),
                  pl.BlockSpec((1, 1, ncp, dh), lambda b, g, q: (b, g, 0, 0)),
                  pl.BlockSpec((1, 1, ncp, dh), lambda b, g, q: (b, g_n + g, 0, 0)),
                  kv_spec(2), kv_spec(3), kv_spec(4), kv_spec(5),
                  pl.BlockSpec((1, tq, LANES), lambda b, g, q: (b, q, GLA_MISC_OFF // LANES))],
        out_specs=pl.BlockSpec((1, tq, nr * dh), lambda b, g, q: (b, q, g)),
        out_shape=jax.ShapeDtypeStruct((bsz, s, NSA_HEADS * dh), F32),
        scratch_shapes=[pltpu.VMEM((s, LANES), BF16), pltpu.VMEM((s // tk, dh, tk), BF16)] * 2
        + [pltpu.VMEM((LANES, nq), BF16), pltpu.VMEM((s // SEL_BLOCK, tq), F32)]
        + [pltpu.VMEM((1, nq), F32), pltpu.VMEM((1, nq), F32), pltpu.VMEM((dh, nq), F32)] * 2
        + [pltpu.VMEM((tk, nq), F32)] * nbuf + [pltpu.VMEM((tk, nq), BF16)] * nbuf + [pltpu.VMEM((1, nq), F32)] * nbuf,
        compiler_params=_cparams(("parallel", "parallel", "arbitrary")),
        name="nsa_attention",
    )(slopes, proj_nsa, kvc, kvc, proj_nsa, proj_nsa, proj_nsa, proj_nsa, proj_gla)


def _gla_kernel(q_ref, k_ref, v_ref, og_ref, lr_ref, wg_ref, bg_ref, nw_ref, o_ref, st_scr, la_scr, b_scr):
    rows_n = q_ref.shape[1]
    c_n, sub = GLA_CHUNK, GLA_SUB
    nh, dk, dv = GLA_HEADS, GLA_DK, GLA_DV

    @pl.when(pl.program_id(1) == 0)
    def _():
        st_scr[...] = jnp.zeros(st_scr.shape, F32)

    z = _dot_hi_lo(lr_ref[0], wg_ref[...]) + bg_ref[...]
    la_scr[...] = (jnp.minimum(z, 0.0) - jnp.log1p(jnp.exp(-jnp.abs(z)))) * (LOG2E / GLA_GATE_NORM)
    tril = jnp.where(lax.broadcasted_iota(I32, (c_n, c_n), 0) >= lax.broadcasted_iota(I32, (c_n, c_n), 1), 1.0, 0.0)
    tril3 = jnp.concatenate([tril.astype(BF16)] * 3, axis=1)
    row_c = lax.broadcasted_iota(I32, (c_n, 1), 0)
    row_s = lax.broadcasted_iota(I32, (sub, 1), 0)
    lane_c = lax.broadcasted_iota(I32, (1, c_n), 1)
    nw = nw_ref[...]
    hk = [slice(h * dk, (h + 1) * dk) for h in range(nh)]
    hv = [slice(h * dv, (h + 1) * dv) for h in range(nh)]

    def chunk(c, carry):
        r0 = pl.multiple_of(c * c_n, c_n)
        rows = pl.ds(r0, c_n)
        qc = q_ref[0, rows, :] * (dk ** -0.5)
        kc = k_ref[0, rows, :]
        vc = [v_ref[0, rows, hv[h]].astype(BF16) for h in range(nh)]
        pieces, rest = [], la_scr[rows, :]
        for _ in range(3):
            pieces.append(rest.astype(BF16))
            rest = rest - pieces[-1].astype(F32)
        b = jnp.dot(tril3, jnp.concatenate(pieces, axis=0), preferred_element_type=F32)
        b_scr[...] = b
        st = [st_scr[h] for h in range(nh)]
        q_e = (qc * jnp.exp2(b)).astype(BF16)
        o = [_nt_dot(q_e[:, hk[h]], st[h].astype(BF16)) for h in range(nh)]
        strips = [[] for _ in range(nh)]
        for blk in range(c_n // sub):
            lo = blk * sub
            q_i = qc[lo:lo + sub]
            b_i = b[lo:lo + sub]
            a = [jnp.zeros((sub, c_n), F32) for _ in range(nh)]
            if blk > 0:
                b_r = b_scr[lo - 1:lo, :]
                q_d = (q_i * jnp.exp2(b_i - b_r)).astype(BF16)
                k_d = (kc * jnp.exp2(jnp.where(row_c < lo, b_r - b, -jnp.inf))).astype(BF16)
                a = [_nt_dot(q_d[:, hk[h]], k_d[:, hk[h]]) for h in range(nh)]
            for j in range(sub):
                b_j = b_scr[lo + j:lo + j + 1, :]
                k_j = k_ref[0, pl.ds(r0 + lo + j, 1), :]
                prod = q_i * k_j * jnp.exp2(jnp.where(row_s >= j, b_i - b_j, -jnp.inf))
                for h in range(nh):
                    col = jnp.sum(prod[:, hk[h]], axis=-1, keepdims=True)
                    a[h] = jnp.where(lane_c == lo + j, col, a[h]) if blk == 0 else (
                        a[h] + jnp.where(lane_c == lo + j, col, 0.0))
            for h in range(nh):
                strips[h].append(a[h])
        for h in range(nh):
            attn = jnp.concatenate(strips[h], axis=0)
            o[h] = o[h] + jnp.dot(attn.astype(BF16), vc[h], preferred_element_type=F32)
        b_last = b_scr[c_n - 1:c_n, :]
        k_dec = (kc * jnp.exp2(b_last - b)).astype(BF16)
        decay = jnp.exp2(b_last)
        for h in range(nh):
            st_scr[h] = st[h] * decay[:, hk[h]] + lax.dot_general(
                vc[h], k_dec[:, hk[h]], (((0,), (0,)), ((), ())), preferred_element_type=F32)
        for h in range(nh):
            og = og_ref[0, rows, hv[h]]
            on = o[h] * lax.rsqrt(jnp.mean(o[h] * o[h], axis=-1, keepdims=True) + EPS) * nw
            o_ref[0, rows, hv[h]] = on * (og * jax.nn.sigmoid(og))
        return carry

    lax.fori_loop(0, rows_n // c_n, chunk, 0)


def _gla(proj_gla, wg_pad, bg, nw):
    bsz, s, _ = proj_gla.shape
    nh, dk, dv = GLA_HEADS, GLA_DK, GLA_DV
    ts = GLA_TS
    wk, wv = nh * dk, nh * dv
    return pl.pallas_call(
        _gla_kernel,
        grid=(bsz, s // ts),
        in_specs=[pl.BlockSpec((1, ts, wk), lambda b, i: (b, i, GLA_Q_OFF // wk)),
                  pl.BlockSpec((1, ts, wk), lambda b, i: (b, i, GLA_K_OFF // wk)),
                  pl.BlockSpec((1, ts, wv), lambda b, i: (b, i, GLA_V_OFF // wv)),
                  pl.BlockSpec((1, ts, wv), lambda b, i: (b, i, GLA_OG_OFF // wv)),
                  pl.BlockSpec((1, ts, LANES), lambda b, i: (b, i, GLA_MISC_OFF // LANES)),
                  pl.BlockSpec((LANES, wk), lambda b, i: (0, 0)),
                  pl.BlockSpec((1, wk), lambda b, i: (0, 0)),
                  pl.BlockSpec((1, dv), lambda b, i: (0, 0))],
        out_specs=pl.BlockSpec((1, ts, wv), lambda b, i: (b, i, 0)),
        out_shape=jax.ShapeDtypeStruct((bsz, s, wv), F32),
        scratch_shapes=[pltpu.VMEM((nh, dv, dk), F32), pltpu.VMEM((ts, wk), F32), pltpu.VMEM((GLA_CHUNK, wk), F32)],
        compiler_params=_cparams(("parallel", "arbitrary")),
        name="gla",
    )(proj_gla, proj_gla, proj_gla, proj_gla, proj_gla, wg_pad, bg, nw)


def _outproj_kernel(nsa_ref, gla_ref, x_ref, wo_ref, g1_ref, sc_ref, sh_ref, nw_ref, wr_ref, br_ref,
                    x1_ref, h_ref, route_ref):
    half = nsa_ref.shape[2]
    acc = jnp.dot(nsa_ref[0].astype(BF16), wo_ref[:half, :], preferred_element_type=F32)
    acc = acc + jnp.dot(gla_ref[0].astype(BF16), wo_ref[half:, :], preferred_element_type=F32)
    x1 = x_ref[0] + g1_ref[0, 0] * acc
    x1_ref[0] = x1
    h = _modulated_norm(x1, nw_ref[...], sc_ref[0, 0], sh_ref[0, 0])
    h_ref[0] = h
    h_hi = h.astype(BF16)
    h_lo = (h - h_hi.astype(F32)).astype(BF16)
    t = jnp.dot(h_hi, wr_ref[...], preferred_element_type=F32)
    logits = (t[:, :LANES] + t[:, LANES:] + jnp.dot(h_lo, wr_ref[:, :LANES], preferred_element_type=F32)
              + br_ref[...])
    lane = lax.broadcasted_iota(I32, (1, LANES), 1)
    ninf = -jnp.inf
    is_g = (lane >= N_EXPERTS) & (lane < N_EXPERTS + N_GROUPS)
    gl = jnp.where(is_g, logits, ninf)
    ge = jnp.exp(gl - jnp.max(gl, axis=-1, keepdims=True))
    gp = ge / jnp.sum(ge, axis=-1, keepdims=True)
    gp_max = jnp.max(gp, axis=-1, keepdims=True)
    grp = jnp.min(jnp.where((gp == gp_max) & is_g, lane - N_EXPERTS, LANES), axis=-1, keepdims=True)
    in_grp = (lane // EXPERTS_PER_GROUP == grp) & (lane < N_EXPERTS)
    el = jnp.where(in_grp, logits, ninf)
    v1 = jnp.max(el, axis=-1, keepdims=True)
    i1 = jnp.min(jnp.where(el == v1, lane, LANES), axis=-1, keepdims=True)
    el2 = jnp.where(lane == i1, ninf, el)
    v2 = jnp.max(el2, axis=-1, keepdims=True)
    i2 = jnp.min(jnp.where(el2 == v2, lane, LANES), axis=-1, keepdims=True)
    e2 = jnp.exp(v2 - v1)
    den = 1.0 + e2
    w1 = gp_max * (1.0 / den)
    w2 = gp_max * (e2 / den)
    route_ref[0] = jnp.where(lane == 0, i1.astype(F32), jnp.where(lane == 1, i2.astype(F32), jnp.where(
        lane == 2, w1, jnp.where(lane == 3, w2, 0.0))))


def _outproj(o_nsa, o_gla, x, wo, mod4, nw, wr, br):
    bsz, s, d = x.shape
    tm = OUT_TM
    half = o_nsa.shape[2]

    def mod_spec(idx):
        return pl.BlockSpec((1, 1, 1, d), lambda b, i: (b, idx, 0, 0))

    row = lambda w: pl.BlockSpec((1, tm, w), lambda b, i: (b, i, 0))
    return pl.pallas_call(
        _outproj_kernel,
        grid=(bsz, s // tm),
        in_specs=[row(half), row(half), row(d),
                  pl.BlockSpec((2 * half, d), lambda b, i: (0, 0), pipeline_mode=pl.Buffered(1)),
                  mod_spec(2), mod_spec(4), mod_spec(3),
                  pl.BlockSpec((1, d), lambda b, i: (0, 0)),
                  pl.BlockSpec((d, 2 * LANES), lambda b, i: (0, 0)),
                  pl.BlockSpec((1, LANES), lambda b, i: (0, 0))],
        out_specs=[row(d), row(d), row(LANES)],
        out_shape=[jax.ShapeDtypeStruct((bsz, s, d), F32), jax.ShapeDtypeStruct((bsz, s, d), F32),
                   jax.ShapeDtypeStruct((bsz, s, LANES), F32)],
        compiler_params=_cparams(("parallel", "parallel")),
        name="outproj_router",
    )(o_nsa, o_gla, x, wo, mod4, mod4, mod4, nw, wr, br)


def _rank_kernel(route_ref, dest_ref, meta_ref, rank_scr):
    n = route_ref.shape[0]
    tm = RANK_TM
    lane_i = lax.broadcasted_iota(I32, (1, LANES), 1)
    lane = lane_i.astype(F32)
    strict = jnp.where(lax.broadcasted_iota(I32, (tm, tm), 0) > lax.broadcasted_iota(I32, (tm, tm), 1),
                       1.0, 0.0).astype(BF16)

    def two_lanes(a, b):
        return jnp.where(lane_i == 0, a, jnp.where(lane_i == 1, b, 0.0))

    def pick(e, table):
        return jnp.sum(jnp.where(lane == e, table, 0.0), axis=-1, keepdims=True)

    def count(i, seen):
        r0 = pl.multiple_of(i * tm, tm)
        rt = route_ref[pl.ds(r0, tm), :]
        e1, e2 = rt[:, 0:1], rt[:, 1:2]
        member = jnp.where(lane == e1, 1.0, jnp.where(lane == e2, 1.0, 0.0))
        before = jnp.dot(strict, member.astype(BF16), preferred_element_type=F32) + seen
        rank_scr[pl.ds(r0, tm), :] = two_lanes(pick(e1, before), pick(e2, before))
        return seen + jnp.sum(member, axis=0, keepdims=True)

    counts = lax.fori_loop(0, n // tm, count, jnp.zeros((1, LANES), F32), unroll=4)
    ntile = jnp.floor((counts + (MOE_TB - 1)) * (1.0 / MOE_TB))
    incl = jnp.where(lax.broadcasted_iota(I32, (LANES, LANES), 0) <= lax.broadcasted_iota(I32, (LANES, LANES), 1),
                     1.0, 0.0).astype(BF16)
    tile_end = jnp.dot(jnp.broadcast_to(ntile, (8, LANES)).astype(BF16), incl,
                       preferred_element_type=F32)[0:1]
    row_start = (tile_end - ntile) * MOE_TB

    def place(i, carry):
        r0 = pl.multiple_of(i * tm, tm)
        rt = route_ref[pl.ds(r0, tm), :]
        rk = rank_scr[pl.ds(r0, tm), :]
        d1 = pick(rt[:, 0:1], row_start) + rk[:, 0:1]
        d2 = pick(rt[:, 1:2], row_start) + rk[:, 1:2]
        dest_ref[pl.ds(r0, tm), :] = two_lanes(d1, d2).astype(I32)
        return carry

    lax.fori_loop(0, n // tm, place, 0, unroll=4)
    trow = lax.broadcasted_iota(I32, (meta_ref.shape[0], 1), 0).astype(F32)
    texp = jnp.sum(jnp.where((tile_end <= trow) & (lane_i < N_EXPERTS), 1.0, 0.0), axis=-1, keepdims=True)
    texp = jnp.minimum(texp, N_EXPERTS - 1.0)
    used = pick(N_EXPERTS - 1.0, tile_end)
    diag = lax.broadcasted_iota(I32, (meta_ref.shape[0], LANES), 0) == lane_i
    end_rows = jnp.sum(jnp.where(diag, tile_end, 0.0), axis=-1, keepdims=True)
    ntile_rows = jnp.sum(jnp.where(diag, ntile, 0.0), axis=-1, keepdims=True)
    meta_ref[...] = jnp.where(lane_i == 2, end_rows, jnp.where(lane_i == 3, ntile_rows, two_lanes(
        texp, jnp.broadcast_to(used, texp.shape)))).astype(I32)


def _rank(route):
    n = route.shape[0]
    return pl.pallas_call(
        _rank_kernel,
        out_shape=[jax.ShapeDtypeStruct((n, LANES), I32), jax.ShapeDtypeStruct((LANES, LANES), I32)],
        scratch_shapes=[pltpu.VMEM((n, LANES), F32)],
        compiler_params=pltpu.CompilerParams(vmem_limit_bytes=VMEM_LIMIT),
        name="moe_rank",
    )(route)


def _dispatch_kernel(dest_ref, ends_ref, h_ref, xs_ref, zero_scr, sem, zsem):
    i = pl.program_id(0)
    tm = h_ref.shape[0]
    tb = zero_scr.shape[0]

    @pl.when(i == 0)
    def _():
        zero_scr[...] = jnp.zeros(zero_scr.shape, F32)

        def zero_copy(e):
            r0 = pl.multiple_of((ends_ref[e] - 1) * tb, tb)
            return pltpu.make_async_copy(zero_scr, xs_ref.at[pl.ds(r0, tb)], zsem)

        def start(e, carry):
            @pl.when(ends_ref[N_EXPERTS + e] > 0)
            def _():
                zero_copy(e).start()
            return carry

        def wait(e, carry):
            @pl.when(ends_ref[N_EXPERTS + e] > 0)
            def _():
                zero_copy(e).wait()
            return carry

        def tail_copy(t):
            return pltpu.make_async_copy(zero_scr, xs_ref.at[pl.ds(pl.multiple_of(t * tb, tb), tb)], zsem)

        def tail_start(t, carry):
            tail_copy(t).start()
            return carry

        def tail_wait(t, carry):
            tail_copy(t).wait()
            return carry

        used = ends_ref[2 * N_EXPERTS]
        lax.fori_loop(0, N_EXPERTS, start, 0)
        lax.fori_loop(used, xs_ref.shape[0] // tb, tail_start, 0)
        lax.fori_loop(0, N_EXPERTS, wait, 0)
        lax.fori_loop(used, xs_ref.shape[0] // tb, tail_wait, 0)

    def issue(r, carry):
        p = (i * tm + r) * TOP_K
        for k in range(TOP_K):
            pltpu.make_async_copy(h_ref.at[pl.ds(r, 1)], xs_ref.at[pl.ds(dest_ref[p + k], 1)], sem).start()
        return carry

    lax.fori_loop(0, tm, issue, 0, unroll=DMA_UNROLL)
    for k in range(TOP_K):
        pltpu.make_async_copy(h_ref, xs_ref.at[pl.ds(0, tm)], sem).wait()


def _dispatch(dest_flat, ends_flat, h, cap):
    n, d = h.shape
    tm = DISPATCH_TM
    return pl.pallas_call(
        _dispatch_kernel,
        grid_spec=pltpu.PrefetchScalarGridSpec(
            num_scalar_prefetch=2, grid=(n // tm,),
            in_specs=[pl.BlockSpec((tm, d), lambda i, dst, ends: (i, 0))],
            out_specs=pl.BlockSpec(memory_space=pl.ANY),
            scratch_shapes=[pltpu.VMEM((MOE_TB, d), F32), pltpu.SemaphoreType.DMA(()), pltpu.SemaphoreType.DMA(())]),
        out_shape=jax.ShapeDtypeStruct((cap, d), F32),
        compiler_params=_cparams(("arbitrary",)),
        name="moe_dispatch",
    )(dest_flat, ends_flat, h)


def _ffn_kernel(meta_ref, x_ref, wg_hbm, wu_hbm, wd_hbm, y_ref, wg_st, wu_st, wd_st, wgb, wub, wdb, wsem):
    t = pl.program_id(0)
    ntile = pl.num_programs(0)
    n_used = meta_ref[ntile]
    e = meta_ref[t]
    e_prev = meta_ref[jnp.maximum(t - 1, 0)]
    active = t < n_used

    def fetch(expert):
        return [pltpu.make_async_copy(src.at[expert], dst, wsem)
                for src, dst in ((wg_hbm, wg_st), (wu_hbm, wu_st), (wd_hbm, wd_st))]

    @pl.when(t == 0)
    def _():
        for cp in fetch(e):
            cp.start()

    @pl.when(active & ((t == 0) | (e != e_prev)))
    def _():
        for cp in fetch(e):
            cp.wait()
        wgb[...] = wg_st[...].astype(BF16)
        wub[...] = wu_st[...].astype(BF16)
        wdb[...] = wd_st[...].astype(BF16)
        end = meta_ref[ntile + 1 + e]

        @pl.when(end < n_used)
        def _():
            for cp in fetch(meta_ref[end]):
                cp.start()

    @pl.when(active)
    def _():
        x = x_ref[...].astype(BF16)
        gate = jnp.dot(x, wgb[...], preferred_element_type=F32)
        up = jnp.dot(x, wub[...], preferred_element_type=F32)
        act = (gate * jax.nn.sigmoid(gate)) * up
        y_ref[...] = jnp.dot(act.astype(BF16), wdb[...], preferred_element_type=F32)

    @pl.when(jnp.logical_not(active))
    def _():
        y_ref[...] = jnp.zeros(y_ref.shape, F32)


def _ffn(meta_flat, xs, wg, wu, wd):
    cap, d = xs.shape
    ff = wg.shape[2]
    tb = MOE_TB
    ntile = cap // tb
    return pl.pallas_call(
        _ffn_kernel,
        grid_spec=pltpu.PrefetchScalarGridSpec(
            num_scalar_prefetch=1, grid=(ntile,),
            in_specs=[pl.BlockSpec((tb, d), lambda t, m: (jnp.minimum(t, m[ntile] - 1), 0)),
                      pl.BlockSpec(memory_space=pl.ANY), pl.BlockSpec(memory_space=pl.ANY),
                      pl.BlockSpec(memory_space=pl.ANY)],
            out_specs=pl.BlockSpec((tb, d), lambda t, m: (t, 0)),
            scratch_shapes=[pltpu.VMEM((d, ff), F32), pltpu.VMEM((d, ff), F32), pltpu.VMEM((ff, d), F32),
                            pltpu.VMEM((d, ff), BF16), pltpu.VMEM((d, ff), BF16), pltpu.VMEM((ff, d), BF16),
                            pltpu.SemaphoreType.DMA(())]),
        out_shape=jax.ShapeDtypeStruct((cap, d), F32),
        compiler_params=_cparams(("arbitrary",)),
        name="moe_ffn",
    )(meta_flat, xs, wg, wu, wd)


def _combine_kernel(dest_ref, y_ref, x1_ref, route_ref, g2_ref, nf_ref, o_ref, ybuf, sem):
    i = pl.program_id(0)
    nstep = pl.num_programs(0)
    tm = x1_ref.shape[0]

    def issue(tile, slot):
        def body(r, carry):
            p = (tile * tm + r) * TOP_K
            for k in range(TOP_K):
                pltpu.make_async_copy(y_ref.at[pl.ds(dest_ref[p + k], 1)], ybuf.at[slot, k, pl.ds(r, 1)],
                                      sem.at[slot]).start()
            return carry

        lax.fori_loop(0, tm, body, 0, unroll=DMA_UNROLL)

    @pl.when(i == 0)
    def _():
        issue(0, 0)

    @pl.when(i + 1 < nstep)
    def _():
        issue(i + 1, (i + 1) % 2)

    slot = i % 2
    for k in range(TOP_K):
        pltpu.make_async_copy(y_ref.at[pl.ds(0, tm)], ybuf.at[slot, k], sem.at[slot]).wait()
    rt = route_ref[...]
    moe = rt[:, 2:3] * ybuf[slot, 0] + rt[:, 3:4] * ybuf[slot, 1]
    xo = x1_ref[...] + g2_ref[0, 0] * moe
    o_ref[...] = xo * lax.rsqrt(jnp.mean(xo * xo, axis=-1, keepdims=True) + EPS) * nf_ref[...]


def _combine(dest_flat, y, x1, route, mod4, nf, seq):
    n, d = x1.shape
    tm = COMB_TM
    tiles_per_seq = seq // tm
    return pl.pallas_call(
        _combine_kernel,
        grid_spec=pltpu.PrefetchScalarGridSpec(
            num_scalar_prefetch=1, grid=(n // tm,),
            in_specs=[pl.BlockSpec(memory_space=pl.ANY),
                      pl.BlockSpec((tm, d), lambda i, dst: (i, 0)),
                      pl.BlockSpec((tm, LANES), lambda i, dst: (i, 0)),
                      pl.BlockSpec((1, 1, 1, d), lambda i, dst: (i // tiles_per_seq, 5, 0, 0)),
                      pl.BlockSpec((1, d), lambda i, dst: (0, 0))],
            out_specs=pl.BlockSpec((tm, d), lambda i, dst: (i, 0)),
            scratch_shapes=[pltpu.VMEM((2, TOP_K, tm, d), F32), pltpu.SemaphoreType.DMA((2,))]),
        out_shape=jax.ShapeDtypeStruct((n, d), F32),
        compiler_params=_cparams(("arbitrary",)),
        name="moe_combine",
    )(dest_flat, y, x1, route, mod4, nf)


def _alibi_slopes():
    n = NSA_HEADS
    full = jnp.asarray(2.0 ** (-8.0 * np.arange(1, n + 1) / n), dtype=F32)
    pieces, rest = [], full * LOG2E
    for _ in range(3):
        piece = rest.astype(BF16).astype(F32)
        pieces.append(piece)
        rest = rest - piece
    return jnp.concatenate([full] + pieces)


def _layer(x, c, w_ada, b_ada, norm1_w, w_in, cmp_pos_k, cmp_w1_k, cmp_w2_k, cmp_pos_v, cmp_w1_v, cmp_w2_v,
           gla_w_gate2, gla_b_gate, gla_norm_w, w_out, norm2_w, w_rg, b_rg, w_re, b_re, w_eg, w_eu, w_ed):
    bsz, s, d = x.shape
    dh = NSA_HEAD_DIM
    mod4 = _adaln(c, w_ada, b_ada).reshape(bsz, 6, 1, d)

    o_gate = NSA_COLS
    o_gla = o_gate + NSA_GATE_COLS
    o_lr = o_gla + 2 * GLA_HEADS * GLA_DK + 2 * GLA_HEADS * GLA_DV
    w_nsa, w_gla = _prep_w_in(w_in.T, o_gate, o_gla, o_lr)
    nw1 = norm1_w.reshape(1, d)
    proj_nsa, proj_gla = _inproj(x, mod4, nw1, w_nsa, w_gla, sc_idx=1, sh_idx=0)

    pos = jnp.stack([cmp_pos_k, cmp_pos_v]).reshape(2, 2, CMP_STRIDE * dh)
    kvc = _compress(proj_nsa, pos, jnp.stack([cmp_w1_k, cmp_w1_v]), jnp.stack([cmp_w2_k, cmp_w2_v]))
    o_nsa = _nsa(_alibi_slopes(), proj_nsa, kvc, proj_gla)

    wg_pad = jnp.zeros((LANES, GLA_HEADS * GLA_DK), F32).at[
        NSA_GATE_COLS:NSA_GATE_COLS + GLA_GATE_RANK].set(gla_w_gate2)
    o_gla_out = _gla(proj_gla, wg_pad, gla_b_gate.reshape(1, -1), gla_norm_w.reshape(1, -1))

    wr = jnp.concatenate([w_re, w_rg, jnp.zeros((d, LANES - N_EXPERTS - N_GROUPS), F32)], axis=1)
    br = jnp.concatenate([b_re, b_rg, jnp.zeros((LANES - N_EXPERTS - N_GROUPS,), F32)]).reshape(1, LANES)
    wr_hi = wr.astype(BF16)
    wr_cat = jnp.concatenate([wr_hi, (wr - wr_hi.astype(F32)).astype(BF16)], axis=1)
    x1, h2, route = _outproj(o_nsa, o_gla_out, x, w_out.astype(BF16), mod4, norm2_w.reshape(1, d), wr_cat, br)

    n = bsz * s
    npair = n * TOP_K
    cap = npair + N_EXPERTS * MOE_TB
    ntile = cap // MOE_TB
    route2 = route.reshape(n, LANES)
    dest, meta = _rank(route2)
    dest_flat = dest[:, :TOP_K].reshape(npair)
    meta_flat = jnp.concatenate([meta[:ntile, 0], meta[:1, 1], meta[:N_EXPERTS, 2]])
    ends_flat = jnp.concatenate([meta[:N_EXPERTS, 2], meta[:N_EXPERTS, 3], meta[:1, 1]])
    xs = _dispatch(dest_flat, ends_flat, h2.reshape(n, d), cap)
    y = _ffn(meta_flat, xs, w_eg, w_eu, w_ed)
    return x1.reshape(n, d), y, dest_flat, route2, mod4


def kernel(x, c, w_ada, b_ada, norm1_w, w_in, cmp_pos_k, cmp_w1_k, cmp_w2_k, cmp_pos_v, cmp_w1_v, cmp_w2_v,
           gla_w_gate2, gla_b_gate, gla_norm_w, w_out, norm2_w, w_router_group, b_router_group, w_router_expert,
           b_router_expert, w_expert_gate, w_expert_up, w_expert_down, norm_f_w):
    bsz, s, d = x.shape
    assert w_ada.shape[0] == 1, "single layer"
    x1, y, dest_flat, route2, mod4 = _layer(
        x, c, w_ada[0], b_ada[0], norm1_w[0], w_in[0], cmp_pos_k[0], cmp_w1_k[0], cmp_w2_k[0], cmp_pos_v[0],
        cmp_w1_v[0], cmp_w2_v[0], gla_w_gate2[0], gla_b_gate[0], gla_norm_w[0], w_out[0], norm2_w[0],
        w_router_group[0], b_router_group[0], w_router_expert[0], b_router_expert[0],
        w_expert_gate[0], w_expert_up[0], w_expert_down[0])
    out = _combine(dest_flat, y, x1, route2, mod4, norm_f_w.reshape(1, d), s)
    return out.reshape(bsz, s, d)
```
